```python
import jax, jax.numpy as jnp
from jax import lax
import numpy as np

D_MODEL = 1024
BATCH = 2
SEQ = 8192
DEPTH = 1

CTX_LEN = 256
GRID_W = 64

GLA_HEADS = 4
GLA_DK = 128
GLA_DV = 256
GLA_RANK = 16
GLA_TAU = 16.0
GLA_CHUNK = 64
GLA_QK = GLA_HEADS * GLA_DK
GLA_V = GLA_HEADS * GLA_DV

RET_HEADS = 4
RET_DK = 128
RET_DV = 256
RET_CHUNK = 128
RET_QK = RET_HEADS * RET_DK
RET_V = RET_HEADS * RET_DV
ROPE_BASE = 10000.0

N_GROUPS = 4
EXPERTS_PER_GROUP = 8
N_EXPERTS = N_GROUPS * EXPERTS_PER_GROUP
TOP_K = 2
EXPERT_FF = 256
MOE_BLOCK = 128

NORM_EPS = 1e-6

IN_SPLITS = (GLA_QK, GLA_QK, GLA_V, GLA_V, GLA_RANK, GLA_RANK,
             RET_QK, RET_QK, RET_V, RET_V, D_MODEL, D_MODEL)
IN_WIDTH = sum(IN_SPLITS)

kernel_name = 'hybrid_gla_retention_hmoe_dit_block'


def rmsnorm(x, g):
    xf = x.astype(jnp.float32)
    y = xf * lax.rsqrt(jnp.mean(xf * xf, axis=-1, keepdims=True) + NORM_EPS)
    return (y * g.astype(jnp.float32)).astype(x.dtype)


def head_groupnorm(o):
    of = o.astype(jnp.float32)
    mu = jnp.mean(of, axis=-1, keepdims=True)
    var = jnp.mean(jnp.square(of - mu), axis=-1, keepdims=True)
    return ((of - mu) * lax.rsqrt(var + NORM_EPS)).astype(o.dtype)


def modulate(h, shift, scale):
    return h * (1.0 + scale) + shift


def split_columns(proj):
    out = []
    start = 0
    for w in IN_SPLITS:
        out.append(proj[..., start:start + w])
        start += w
    return out


def retention_log_decays():
    h = jnp.arange(RET_HEADS, dtype=jnp.float32)
    fwd = jnp.log1p(-jnp.exp2(-5.0 - h))
    bwd = jnp.log1p(-jnp.exp2(-5.5 - h))
    return fwd, bwd


def axial_rope(t, rows, cols):
    half = t.shape[-1] // 2
    n = half // 2
    inv = ROPE_BASE ** (-jnp.arange(n, dtype=jnp.float32) / n)

    def rot(u, p):
        ang = p[:, None] * inv[None, :]
        cos = jnp.cos(ang)[None, :, None, :].astype(u.dtype)
        sin = jnp.sin(ang)[None, :, None, :].astype(u.dtype)
        u1, u2 = u[..., :n], u[..., n:]
        return jnp.concatenate([u1 * cos - u2 * sin, u1 * sin + u2 * cos], axis=-1)

    return jnp.concatenate([rot(t[..., :half], rows), rot(t[..., half:], cols)], axis=-1)


def to_chunks(t, chunk):
    B, L, H, W = t.shape
    return jnp.moveaxis(t.reshape(B, L // chunk, chunk, H, W), 1, 0)


def from_chunks(o):
    N, B, C, H, W = o.shape
    return jnp.moveaxis(o, 0, 1).reshape(B, N * C, H, W)


def gla_scan(q, k, v, log_a, s0, strict):
    C = GLA_CHUNK
    ti = jnp.arange(C)
    mask = (ti[:, None] > ti[None, :]) if strict else (ti[:, None] >= ti[None, :])
    mask5 = mask[None, :, :, None, None]

    def step(S, inp):
        qc, kc, vc, lac = inp
        b = jnp.cumsum(lac.astype(jnp.float32), axis=1)
        inter = jnp.einsum('bthk,bhkv->bthv', qc * jnp.exp(b), S)
        decay = jnp.exp(jnp.where(mask5, b[:, :, None] - b[:, None, :], -jnp.inf))
        scores = jnp.einsum('bthk,btshk,bshk->btsh', qc, decay, kc)
        intra = jnp.einsum('btsh,bshv->bthv', scores, vc)
        b_last = b[:, -1]
        S_new = jnp.exp(b_last)[..., None] * S + jnp.einsum(
            'bshk,bshv->bhkv', kc * jnp.exp(b_last[:, None] - b), vc)
        return S_new, inter + intra

    S_fin, o = lax.scan(step, s0, (to_chunks(q, C), to_chunks(k, C), to_chunks(v, C), to_chunks(log_a, C)))
    return from_chunks(o).astype(v.dtype), S_fin


def retention_scan(q, k, v, log_gamma, s0, strict):
    C = RET_CHUNK
    pos = jnp.arange(C, dtype=jnp.float32)
    ti = jnp.arange(C)
    mask = (ti[:, None] > ti[None, :]) if strict else (ti[:, None] >= ti[None, :])
    rel = pos[:, None] - pos[None, :]
    dmat = jnp.exp(jnp.where(mask[None], rel[None] * log_gamma[:, None, None], -jnp.inf))
    q_decay = jnp.exp((pos + 1.0)[:, None] * log_gamma[None, :])
    k_decay = jnp.exp((C - 1.0 - pos)[:, None] * log_gamma[None, :])
    c_decay = jnp.exp(C * log_gamma)

    def step(S, inp):
        qc, kc, vc = inp
        inter = jnp.einsum('bthk,bhkv->bthv', qc, S) * q_decay[None, :, :, None]
        scores = jnp.einsum('bthk,bshk->bhts', qc, kc) * dmat[None]
        intra = jnp.einsum('bhts,bshv->bthv', scores, vc)
        S_new = c_decay[None, :, None, None] * S + jnp.einsum(
            'bshk,bshv->bhkv', kc * k_decay[None, :, :, None], vc)
        return S_new, inter + intra

    S_fin, o = lax.scan(step, s0, (to_chunks(q, C), to_chunks(k, C), to_chunks(v, C)))
    return from_chunks(o).astype(v.dtype), S_fin


def mixer_heads(h, w_in, gla_lr_w, gla_lr_b, gla_norm, states, rope):
    B, L, _ = h.shape
    (gq, gk, gv, gg, glr_f, glr_b, rq, rk, rv, rg, m_gla, m_ret) = split_columns(h @ w_in)

    def heads(t, n):
        return t.reshape(B, L, n, -1)

    def flip(t):
        return jnp.flip(t, axis=1)

    gq = heads(gq, GLA_HEADS) * GLA_DK ** -0.5
    gk = heads(gk, GLA_HEADS)
    gv = heads(gv, GLA_HEADS)

    def log_gate(lr, d):
        z = (lr @ gla_lr_w[d] + gla_lr_b[d]).astype(jnp.float32)
        return heads(jax.nn.log_sigmoid(z) / GLA_TAU, GLA_HEADS)

    o_f, s_gf = gla_scan(gq, gk, gv, log_gate(glr_f, 0), states[0], False)
    o_b, s_gb = gla_scan(flip(gq), flip(gk), flip(gv), flip(log_gate(glr_b, 1)), states[1], True)
    o_gla = rmsnorm(o_f + flip(o_b), gla_norm) * jax.nn.silu(heads(gg, GLA_HEADS))
    o_gla = o_gla.reshape(B, L, GLA_V)

    rq = heads(rq, RET_HEADS)
    rk = heads(rk, RET_HEADS) * RET_DK ** -0.5
    rv = heads(rv, RET_HEADS)
    if rope is not None:
        rq = axial_rope(rq, rope[0], rope[1])
        rk = axial_rope(rk, rope[0], rope[1])
    lg_f, lg_b = retention_log_decays()
    r_f, s_rf = retention_scan(rq, rk, rv, lg_f, states[2], False)
    r_b, s_rb = retention_scan(flip(rq), flip(rk), flip(rv), lg_b, states[3], True)
    o_ret = head_groupnorm(r_f + flip(r_b)) * jax.nn.silu(heads(rg, RET_HEADS))
    o_ret = o_ret.reshape(B, L, RET_V)
    return o_gla, o_ret, m_gla, m_ret, (s_gf, s_gb, s_rf, s_rb)


def merge_branches(o_gla, o_ret, m_gla, m_ret, w_branch_gla, w_branch_ret, w_out):
    y = jax.nn.sigmoid(m_gla) * (o_gla @ w_branch_gla) + jax.nn.sigmoid(m_ret) * (o_ret @ w_branch_ret)
    return y @ w_out


def hier_moe(h, w_rg, b_rg, w_re, b_re, w_gate, w_up, w_down):
    B, L, D = h.shape
    T = B * L
    xt = h.reshape(T, D)
    g_logits = (xt @ w_rg + b_rg).astype(jnp.float32)
    g_prob = jax.nn.softmax(g_logits, axis=-1)
    g_sel = jnp.argmax(g_logits, axis=-1).astype(jnp.int32)
    g_w = jnp.take_along_axis(g_prob, g_sel[:, None], axis=-1)[:, 0]
    e_logits = (xt @ w_re + b_re).astype(jnp.float32).reshape(T, N_GROUPS, EXPERTS_PER_GROUP)
    e_logits = jnp.take_along_axis(e_logits, g_sel[:, None, None], axis=1)[:, 0]
    top_v, top_i = lax.top_k(e_logits, TOP_K)
    e_w = jax.nn.softmax(top_v, axis=-1) * g_w[:, None]
    expert_id = g_sel[:, None] * EXPERTS_PER_GROUP + top_i.astype(jnp.int32)

    A = T * TOP_K
    flat_e = expert_id.reshape(A)
    flat_tok = jnp.repeat(jnp.arange(T, dtype=jnp.int32), TOP_K)
    flat_w = e_w.reshape(A)
    order = jnp.argsort(flat_e)
    se, stok, sw = flat_e[order], flat_tok[order], flat_w[order]
    counts = jax.ops.segment_sum(jnp.ones((A,), jnp.int32), flat_e, num_segments=N_EXPERTS)
    starts = jnp.cumsum(counts) - counts
    padded = (counts + MOE_BLOCK - 1) // MOE_BLOCK * MOE_BLOCK
    pends = jnp.cumsum(padded)
    pstarts = pends - padded
    dest = pstarts[se] + jnp.arange(A, dtype=jnp.int32) - starts[se]
    P = A + N_EXPERTS * MOE_BLOCK
    NB = P // MOE_BLOCK
    x_pad = jnp.zeros((P, D), h.dtype).at[dest].set(xt[stok])
    tok_pad = jnp.zeros((P,), jnp.int32).at[dest].set(stok)
    w_pad = jnp.zeros((P,), jnp.float32).at[dest].set(sw)
    block_start = jnp.arange(NB, dtype=jnp.int32) * MOE_BLOCK
    block_e = jnp.minimum(jnp.searchsorted(pends, block_start, side='right'), N_EXPERTS - 1)

    def expert_block(args):
        xb, e = args
        hid = jax.nn.silu(xb @ w_gate[e]) * (xb @ w_up[e])
        return hid @ w_down[e]

    y = lax.map(expert_block, (x_pad.reshape(NB, MOE_BLOCK, D), block_e)).reshape(P, D)
    out = jnp.zeros((T, D), jnp.float32).at[tok_pad].add(y.astype(jnp.float32) * w_pad[:, None])
    return out.astype(h.dtype).reshape(B, L, D)


def setup_inputs(seed: int = 0) -> dict:
    key = jax.random.key(seed)
    ks = jax.random.split(key, 24)

    def nrm(k, shape, scale):
        return jax.random.normal(k, shape, jnp.float32) * scale

    D = D_MODEL
    return {
        'x': nrm(ks[0], (BATCH, SEQ, D), 1.0),
        'c': nrm(ks[1], (BATCH, D), 1.0),
        'ctx': nrm(ks[2], (BATCH, CTX_LEN, D), 1.0),
        'c_ctx': nrm(ks[3], (D,), 1.0),
        'w_ada': nrm(ks[4], (DEPTH, D, 6 * D), D ** -0.5),
        'b_ada': nrm(ks[5], (DEPTH, 6 * D), 0.02),
        'norm_mix': 1.0 + nrm(ks[6], (DEPTH, D), 0.05),
        'norm_ffn': 1.0 + nrm(ks[7], (DEPTH, D), 0.05),
        'w_in': nrm(ks[8], (DEPTH, D, IN_WIDTH), D ** -0.5),
        'gla_lr_w': nrm(ks[9], (DEPTH, 2, GLA_RANK, GLA_QK), GLA_RANK ** -0.5),
        'gla_lr_b': nrm(ks[10], (DEPTH, 2, GLA_QK), 0.1),
        'gla_norm': 1.0 + nrm(ks[11], (DEPTH, GLA_DV), 0.05),
        'w_branch_gla': nrm(ks[12], (DEPTH, GLA_V, D), GLA_V ** -0.5),
        'w_branch_ret': nrm(ks[13], (DEPTH, RET_V, D), RET_V ** -0.5),
        'w_out': nrm(ks[14], (DEPTH, D, D), D ** -0.5),
        'w_router_group': nrm(ks[15], (DEPTH, D, N_GROUPS), D ** -0.5),
        'b_router_group': nrm(ks[16], (DEPTH, N_GROUPS), 0.01),
        'w_router_expert': nrm(ks[17], (DEPTH, D, N_EXPERTS), D ** -0.5),
        'b_router_expert': nrm(ks[18], (DEPTH, N_EXPERTS), 0.01),
        'w_expert_gate': nrm(ks[19], (DEPTH, N_EXPERTS, D, EXPERT_FF), D ** -0.5),
        'w_expert_up': nrm(ks[20], (DEPTH, N_EXPERTS, D, EXPERT_FF), D ** -0.5),
        'w_expert_down': nrm(ks[21], (DEPTH, N_EXPERTS, EXPERT_FF, D), EXPERT_FF ** -0.5),
        'norm_final': 1.0 + nrm(ks[22], (D,), 0.05),
    }


def reference(x, c, ctx, c_ctx, w_ada, b_ada, norm_mix, norm_ffn, w_in, gla_lr_w, gla_lr_b, gla_norm,
              w_branch_gla, w_branch_ret, w_out, w_router_group, b_router_group, w_router_expert,
              b_router_expert, w_expert_gate, w_expert_up, w_expert_down, norm_final):
    B, L, _ = x.shape
    ROWS = L // GRID_W
    rows = jnp.repeat(jnp.arange(ROWS, dtype=jnp.float32), GRID_W)
    cols = jnp.tile(jnp.arange(GRID_W, dtype=jnp.float32), ROWS)
    zero_states = (
        jnp.zeros((B, GLA_HEADS, GLA_DK, GLA_DV), jnp.float32),
        jnp.zeros((B, GLA_HEADS, GLA_DK, GLA_DV), jnp.float32),
        jnp.zeros((B, RET_HEADS, RET_DK, RET_DV), jnp.float32),
        jnp.zeros((B, RET_HEADS, RET_DK, RET_DV), jnp.float32),
    )
    h_lat = x
    h_ctx = ctx
    for layer in range(DEPTH):
        sh1, sc1, g1, sh2, sc2, g2 = jnp.split(
            (jax.nn.silu(c) @ w_ada[layer] + b_ada[layer])[:, None, :], 6, axis=-1)
        csh1, csc1, cg1, csh2, csc2, cg2 = jnp.split(
            (jax.nn.silu(c_ctx) @ w_ada[layer] + b_ada[layer])[None, None, :], 6, axis=-1)
        mix_args = (w_in[layer], gla_lr_w[layer], gla_lr_b[layer], gla_norm[layer])
        merge_args = (w_branch_gla[layer], w_branch_ret[layer], w_out[layer])
        moe_args = (w_router_group[layer], b_router_group[layer], w_router_expert[layer],
                    b_router_expert[layer], w_expert_gate[layer], w_expert_up[layer], w_expert_down[layer])

        hc = modulate(rmsnorm(h_ctx, norm_mix[layer]), csh1, csc1)
        c_gla, c_ret, c_mg, c_mr, ctx_states = mixer_heads(hc, *mix_args, zero_states, None)

        hl = modulate(rmsnorm(h_lat, norm_mix[layer]), sh1, sc1)
        l_gla, l_ret, l_mg, l_mr, _ = mixer_heads(hl, *mix_args, ctx_states, (rows, cols))
        h_lat = h_lat + g1 * merge_branches(l_gla, l_ret, l_mg, l_mr, *merge_args)
        h_lat = h_lat + g2 * hier_moe(modulate(rmsnorm(h_lat, norm_ffn[layer]), sh2, sc2), *moe_args)

        if layer < DEPTH - 1:
            h_ctx = h_ctx + cg1 * merge_branches(c_gla, c_ret, c_mg, c_mr, *merge_args)
            h_ctx = h_ctx + cg2 * hier_moe(modulate(rmsnorm(h_ctx, norm_ffn[layer]), csh2, csc2), *moe_args)
    return rmsnorm(h_lat, norm_final)
```

```python
import functools

import numpy as np
import jax
import jax.numpy as jnp
from jax import lax
from jax.experimental import pallas as pl
from jax.experimental.pallas import tpu as pltpu

F32 = jnp.float32
BF16 = jnp.bfloat16

D_MODEL = 1024
GRID_W = 64
HEADS = 4
DK = 128
DV = 256
GLA_RANK = 16
GLA_TAU = 16.0
GLA_CHUNK = 64
GLA_SUB = 16
RET_CHUNK = 128
ROPE_BASE = 10000.0
N_GROUPS = 4
EXPERTS_PER_GROUP = 8
N_EXPERTS = N_GROUPS * EXPERTS_PER_GROUP
TOP_K = 2
EXPERT_FF = 256
MOE_BLOCK = 128
NORM_EPS = 1e-6

COL_GQ, COL_GK, COL_GV, COL_GG = 0, 512, 1024, 2048
COL_RQ, COL_RK, COL_RV, COL_RG = 3072, 3584, 4096, 5120
COL_MG, COL_MR = 6144, 7168
PROJ_W = 8192
LANES = 128
ROUTER_W = 128
EXP_ROW0 = 8

VMEM_LIMIT = 56 * 1024 * 1024


def _cparams(sem):
    return pltpu.CompilerParams(dimension_semantics=sem, vmem_limit_bytes=VMEM_LIMIT)


def _sigmoid(x):
    return 1.0 / (1.0 + jnp.exp(-x))


def _silu(x):
    return x * _sigmoid(x)


def _dot(a, b):
    return jnp.dot(a, b, preferred_element_type=F32)


def _dot_nt(a, b):
    return lax.dot_general(a, b, (((1,), (1,)), ((), ())), preferred_element_type=F32)


def _ada_body(c_ref, w_ref, b_ref, o_ref):
    s = _silu(c_ref[...])
    o_ref[...] = _dot(s.astype(BF16), w_ref[...].astype(BF16)) + b_ref[...]


def _ada(c8, w, b):
    n = w.shape[1]
    tn = 1536
    return pl.pallas_call(
        _ada_body,
        grid=(n // tn,),
        in_specs=[pl.BlockSpec((8, D_MODEL), lambda j: (0, 0)),
                  pl.BlockSpec((D_MODEL, tn), lambda j: (0, j)),
                  pl.BlockSpec((1, tn), lambda j: (0, j))],
        out_specs=pl.BlockSpec((8, tn), lambda j: (0, j)),
        out_shape=jax.ShapeDtypeStruct((8, n), F32),
        compiler_params=_cparams(("arbitrary",)),
        name="ada",
    )(c8, w, b)


def _proj_body(x_ref, sh_ref, sc_ref, g_ref, w_ref, wlr_ref, o_ref, lr_ref, h_ref):
    @pl.when(pl.program_id(1) == 0)
    def _():
        x = x_ref[...]
        ms = jnp.mean(x * x, axis=-1, keepdims=True)
        y = x * lax.rsqrt(ms + NORM_EPS) * g_ref[...]
        hb = (y * (1.0 + sc_ref[0]) + sh_ref[0]).astype(BF16)
        h_ref[...] = hb
        lr_ref[...] = _dot(hb, wlr_ref[...])

    o_ref[...] = _dot(h_ref[...], w_ref[...]).astype(BF16)


def _proj(x2, sh, sc, g, w_main, w_lr, tm, rows_per_mod):
    t = x2.shape[0]
    tn = 2048
    tiles_per_mod = rows_per_mod // tm
    mod_map = lambda i, j: (i // tiles_per_mod, 0, 0)
    return pl.pallas_call(
        _proj_body,
        grid=(t // tm, PROJ_W // tn),
        in_specs=[pl.BlockSpec((tm, D_MODEL), lambda i, j: (i, 0)),
                  pl.BlockSpec((1, 1, D_MODEL), mod_map),
                  pl.BlockSpec((1, 1, D_MODEL), mod_map),
                  pl.BlockSpec((1, D_MODEL), lambda i, j: (0, 0)),
                  pl.BlockSpec((D_MODEL, tn), lambda i, j: (0, j)),
                  pl.BlockSpec((D_MODEL, LANES), lambda i, j: (0, 0))],
        out_specs=[pl.BlockSpec((tm, tn), lambda i, j: (i, j)),
                   pl.BlockSpec((tm, LANES), lambda i, j: (i, 0))],
        out_shape=[jax.ShapeDtypeStruct((t, PROJ_W), BF16),
                   jax.ShapeDtypeStruct((t, LANES), F32)],
        scratch_shapes=[pltpu.VMEM((tm, D_MODEL), BF16)],
        compiler_params=_cparams(("parallel", "arbitrary")),
        name="proj",
    )(x2, sh, sc, g, w_main, w_lr)


def _gla_consts():
    c, s = GLA_CHUNK, GLA_SUB
    nsub = c // s
    t = np.arange(c)[:, None]
    u = np.arange(c)[None, :]
    tri = {}
    msk = {}
    for rev in (False, True):
        if not rev:
            cum = u <= t
            ref = u < s * (t // s)
        else:
            cum = u >= t
            ref = u >= s * (t // s + 1)
        tri[rev] = np.concatenate([cum, ref], axis=0).astype(np.float32)
        wide = np.zeros((c, nsub * c), np.float32)
        for i in range(nsub):
            rows = slice(s * i, s * (i + 1))
            causal = (u > t) if rev else (u <= t)
            wide[rows, c * i:c * (i + 1)] = causal[rows]
        msk[rev] = wide
    return tri, msk


def _log_gate(lr, w, b):
    z = jnp.dot(lr, w, precision=lax.Precision.HIGHEST, preferred_element_type=F32) + b
    return -(jnp.maximum(-z, 0.0) + jnp.log(1.0 + jnp.exp(-jnp.abs(z)))) * (1.0 / GLA_TAU)


def _gla_chunk(q, k, v, la, state, tri, mask, rev):
    c, s = GLA_CHUNK, GLA_SUB
    la_hi = la.astype(BF16)
    la_lo = (la - la_hi.astype(F32)).astype(BF16)
    cums = _dot(tri, la_hi) + _dot(tri, la_lo)
    b = cums[:c]
    ref = cums[c:]
    qt = q.astype(F32) * (DK ** -0.5) * jnp.exp(b - ref)
    kf = k.astype(F32)
    inter = _dot((qt * jnp.exp(ref)).astype(BF16), state.astype(BF16))

    row = lax.broadcasted_iota(jnp.int32, (c, DK), 0)
    kparts = []
    for i in range(c // s):
        ref_i = ref[s * i:s * i + 1]
        needed = (row >= s * i) if rev else (row < s * (i + 1))
        kparts.append(kf * jnp.exp(jnp.where(needed, ref_i - b, 0.0)))
    kstack = jnp.concatenate(kparts, axis=0).astype(BF16)
    scores = jnp.where(mask != 0.0, _dot_nt(qt.astype(BF16), kstack), 0.0)
    vstack = jnp.concatenate([v] * (c // s), axis=0)
    out = inter + _dot(scores.astype(BF16), vstack)

    b_end = b[0:1] if rev else b[c - 1:c]
    kst = kf * jnp.exp(b_end - b)
    xt = jnp.concatenate([kst, jnp.broadcast_to(jnp.exp(b_end), (c, DK))], axis=0).T
    new_state = xt[:, c:c + 1] * state + _dot(xt[:, :c].astype(BF16), v)
    return out, new_state


def _gla_body(qf_ref, kf_ref, vf_ref, lrf_ref, qb_ref, kb_ref, vb_ref, lrb_ref,
              wf_ref, bf_ref, wb_ref, bb_ref, trif_ref, trib_ref, mskf_ref, mskb_ref,
              s0f_ref, s0b_ref, of_ref, ob_ref, sf_ref, sb_ref, *, nchunks):
    @pl.when(pl.program_id(2) == 0)
    def _():
        sf_ref[...] = s0f_ref[...]
        sb_ref[...] = s0b_ref[...]

    c = GLA_CHUNK
    la_f = _log_gate(lrf_ref[...], wf_ref[...], bf_ref[...])
    la_b = _log_gate(lrb_ref[...], wb_ref[...], bb_ref[...])
    trif, trib = trif_ref[...], trib_ref[...]
    mskf, mskb = mskf_ref[...], mskb_ref[...]
    st_f = sf_ref[0, 0]
    st_b = sb_ref[0, 0]
    for n in range(nchunks):
        rf = slice(c * n, c * (n + 1))
        o, st_f = _gla_chunk(qf_ref[rf, :], kf_ref[rf, :], vf_ref[rf, :], la_f[rf, :], st_f, trif, mskf, False)
        of_ref[rf, :] = o.astype(BF16)
        m = nchunks - 1 - n
        rb = slice(c * m, c * (m + 1))
        o, st_b = _gla_chunk(qb_ref[rb, :], kb_ref[rb, :], vb_ref[rb, :], la_b[rb, :], st_b, trib, mskb, True)
        ob_ref[rb, :] = o.astype(BF16)
    sf_ref[0, 0] = st_f
    sb_ref[0, 0] = st_b


def _gla(p, lr, wlr_f, blr_f, wlr_b, blr_b, s0f, s0b, batch, seq, tb):
    nb = seq // tb
    tri, msk = _gla_consts()
    trif, trib = jnp.asarray(tri[False], BF16), jnp.asarray(tri[True], BF16)
    mskf, mskb = jnp.asarray(msk[False]), jnp.asarray(msk[True])

    def fwd(col0, width):
        return lambda b, h, i: (b * nb + i, col0 // width + h)

    def bwd(col0, width):
        return lambda b, h, i: (b * nb + nb - 1 - i, col0 // width + h)

    const2 = lambda b, h, i: (0, 0)
    headcol = lambda b, h, i: (0, h)
    st_map = lambda b, h, i: (b, h, 0, 0)
    st_spec = pl.BlockSpec((1, 1, DK, DV), st_map)
    st_shape = jax.ShapeDtypeStruct((batch, HEADS, DK, DV), F32)
    o_shape = jax.ShapeDtypeStruct((batch * seq, HEADS * DV), BF16)
    return pl.pallas_call(
        functools.partial(_gla_body, nchunks=tb // GLA_CHUNK),
        grid=(batch, HEADS, nb),
        in_specs=[pl.BlockSpec((tb, DK), fwd(COL_GQ, DK)),
                  pl.BlockSpec((tb, DK), fwd(COL_GK, DK)),
                  pl.BlockSpec((tb, DV), fwd(COL_GV, DV)),
                  pl.BlockSpec((tb, LANES), lambda b, h, i: (b * nb + i, 0)),
                  pl.BlockSpec((tb, DK), bwd(COL_GQ, DK)),
                  pl.BlockSpec((tb, DK), bwd(COL_GK, DK)),
                  pl.BlockSpec((tb, DV), bwd(COL_GV, DV)),
                  pl.BlockSpec((tb, LANES), lambda b, h, i: (b * nb + nb - 1 - i, 0)),
                  pl.BlockSpec((LANES, DK), headcol),
                  pl.BlockSpec((1, DK), headcol),
                  pl.BlockSpec((LANES, DK), headcol),
                  pl.BlockSpec((1, DK), headcol),
                  pl.BlockSpec(trif.shape, const2),
                  pl.BlockSpec(trib.shape, const2),
                  pl.BlockSpec(mskf.shape, const2),
                  pl.BlockSpec(mskb.shape, const2),
                  st_spec, st_spec],
        out_specs=[pl.BlockSpec((tb, DV), lambda b, h, i: (b * nb + i, h)),
                   pl.BlockSpec((tb, DV), lambda b, h, i: (b * nb + nb - 1 - i, h)),
                   st_spec, st_spec],
        out_shape=[o_shape, o_shape, st_shape, st_shape],
        compiler_params=_cparams(("parallel", "parallel", "arbitrary")),
        name="gla",
    )(p, p, p, lr, p, p, p, lr, wlr_f, blr_f, wlr_b, blr_b, trif, trib, mskf, mskb, s0f, s0b)


def _ret_consts():
    c = RET_CHUNK
    hh = np.arange(HEADS, dtype=np.float64)
    lg = {False: np.log1p(-np.exp2(-5.0 - hh)), True: np.log1p(-np.exp2(-5.5 - hh))}
    t = np.arange(c, dtype=np.float64)[:, None]
    u = np.arange(c, dtype=np.float64)[None, :]
    pos = np.arange(c, dtype=np.float64)
    out = {}
    for rev in (False, True):
        g = lg[rev][:, None, None]
        if not rev:
            dmat = np.where(u <= t, np.exp((t - u) * g), 0.0)
            qd = np.exp((pos + 1.0)[None, :] * lg[rev][:, None])
            kd = np.exp((c - 1.0 - pos)[None, :] * lg[rev][:, None])
        else:
            dmat = np.where(u > t, np.exp((u - t) * g), 0.0)
            qd = np.exp((c - pos)[None, :] * lg[rev][:, None])
            kd = np.exp(pos[None, :] * lg[rev][:, None])
        cd = np.exp(c * lg[rev])
        out[rev] = (dmat.astype(np.float32),
                    np.repeat(qd[:, :, None], DV, axis=2).astype(np.float32),
                    np.repeat(kd[:, :, None], DK, axis=2).astype(np.float32),
                    np.repeat(cd[:, None, None], DV, axis=2).astype(np.float32))
    return out


def _rope(x, cos, sin):
    lane = lax.broadcasted_iota(jnp.int32, x.shape, 1)
    partner = jnp.where((lane % 64) < 32, pltpu.roll(x, DK - 32, 1), pltpu.roll(x, 32, 1))
    return x * cos + partner * sin


def _ret_chunk(q, k, v, cos, sin, state, dmat, qd, kd, cd):
    qr = _rope(q.astype(F32), cos, sin).astype(BF16)
    kr = _rope(k.astype(F32) * (DK ** -0.5), cos, sin)
    inter = _dot(qr, state.astype(BF16)) * qd
    scores = _dot_nt(qr, kr.astype(BF16)) * dmat
    out = inter + _dot(scores.astype(BF16), v)
    new_state = cd * state + _dot((kr * kd).T.astype(BF16), v)
    return out, new_state


def _ret_body(qf_ref, kf_ref, vf_ref, cosf_ref, sinf_ref, qb_ref, kb_ref, vb_ref, cosb_ref, sinb_ref,
              dmf_ref, qdf_ref, kdf_ref, cdf_ref, dmb_ref, qdb_ref, kdb_ref, cdb_ref,
              s0f_ref, s0b_ref, of_ref, ob_ref, sf_ref, sb_ref, *, nchunks):
    @pl.when(pl.program_id(2) == 0)
    def _():
        sf_ref[...] = s0f_ref[...]
        sb_ref[...] = s0b_ref[...]

    c = RET_CHUNK
    cf = (dmf_ref[0], qdf_ref[0], kdf_ref[0], cdf_ref[0])
    cb = (dmb_ref[0], qdb_ref[0], kdb_ref[0], cdb_ref[0])
    st_f = sf_ref[0, 0]
    st_b = sb_ref[0, 0]
    for n in range(nchunks):
        rf = slice(c * n, c * (n + 1))
        o, st_f = _ret_chunk(qf_ref[rf, :], kf_ref[rf, :], vf_ref[rf, :], cosf_ref[rf, :], sinf_ref[rf, :],
                             st_f, *cf)
        of_ref[rf, :] = o.astype(BF16)
        m = nchunks - 1 - n
        rb = slice(c * m, c * (m + 1))
        o, st_b = _ret_chunk(qb_ref[rb, :], kb_ref[rb, :], vb_ref[rb, :], cosb_ref[rb, :], sinb_ref[rb, :],
                             st_b, *cb)
        ob_ref[rb, :] = o.astype(BF16)
    sf_ref[0, 0] = st_f
    sb_ref[0, 0] = st_b


def _ret(p, cos, sin, s0f, s0b, batch, seq, tb):
    nb = seq // tb
    consts = _ret_consts()

    def fwd(col0, width):
        return lambda b, h, i: (b * nb + i, col0 // width + h)

    def bwd(col0, width):
        return lambda b, h, i: (b * nb + nb - 1 - i, col0 // width + h)

    head3 = lambda b, h, i: (h, 0, 0)
    st_spec = pl.BlockSpec((1, 1, DK, DV), lambda b, h, i: (b, h, 0, 0))
    st_shape = jax.ShapeDtypeStruct((batch, HEADS, DK, DV), F32)
    o_shape = jax.ShapeDtypeStruct((batch * seq, HEADS * DV), BF16)
    const_specs = []
    const_args = []
    for rev in (False, True):
        for a in consts[rev]:
            const_specs.append(pl.BlockSpec((1,) + a.shape[1:], head3))
            const_args.append(jnp.asarray(a))
    return pl.pallas_call(
        functools.partial(_ret_body, nchunks=tb // RET_CHUNK),
        grid=(batch, HEADS, nb),
        in_specs=[pl.BlockSpec((tb, DK), fwd(COL_RQ, DK)),
                  pl.BlockSpec((tb, DK), fwd(COL_RK, DK)),
                  pl.BlockSpec((tb, DV), fwd(COL_RV, DV)),
                  pl.BlockSpec((tb, DK), lambda b, h, i: (i, 0)),
                  pl.BlockSpec((tb, DK), lambda b, h, i: (i, 0)),
                  pl.BlockSpec((tb, DK), bwd(COL_RQ, DK)),
                  pl.BlockSpec((tb, DK), bwd(COL_RK, DK)),
                  pl.BlockSpec((tb, DV), bwd(COL_RV, DV)),
                  pl.BlockSpec((tb, DK), lambda b, h, i: (nb - 1 - i, 0)),
                  pl.BlockSpec((tb, DK), lambda b, h, i: (nb - 1 - i, 0))]
                 + const_specs + [st_spec, st_spec],
        out_specs=[pl.BlockSpec((tb, DV), lambda b, h, i: (b * nb + i, h)),
                   pl.BlockSpec((tb, DV), lambda b, h, i: (b * nb + nb - 1 - i, h)),
                   st_spec, st_spec],
        out_shape=[o_shape, o_shape, st_shape, st_shape],
        compiler_params=_cparams(("parallel", "parallel", "arbitrary")),
        name="ret",
    )(p, p, p, cos, sin, p, p, p, cos, sin, *const_args, s0f, s0b)


def _rope_tables(seq):
    n = DK // 4
    inv = ROPE_BASE ** (-jnp.arange(n, dtype=F32) / n)
    pos = jnp.arange(seq, dtype=jnp.int32)
    rows = (pos // GRID_W).astype(F32)
    cols = (pos % GRID_W).astype(F32)
    ar = rows[:, None] * inv[None, :]
    ac = cols[:, None] * inv[None, :]
    cos = jnp.concatenate([jnp.cos(ar), jnp.cos(ar), jnp.cos(ac), jnp.cos(ac)], axis=1)
    sin = jnp.concatenate([-jnp.sin(ar), jnp.sin(ar), -jnp.sin(ac), jnp.sin(ac)], axis=1)
    return cos, sin


def _route(logits_t):
    g = [logits_t[i:i + 1] for i in range(N_GROUPS)]
    gmax = jnp.maximum(jnp.maximum(g[0], g[1]), jnp.maximum(g[2], g[3]))
    gsel = jnp.where(g[0] == gmax, 0, jnp.where(g[1] == gmax, 1, jnp.where(g[2] == gmax, 2, 3)))
    gsum = (jnp.exp(g[0] - gmax) + jnp.exp(g[1] - gmax)) + (jnp.exp(g[2] - gmax) + jnp.exp(g[3] - gmax))
    gw = 1.0 / gsum
    e = [logits_t[EXP_ROW0 + EXPERTS_PER_GROUP * i:EXP_ROW0 + EXPERTS_PER_GROUP * (i + 1)]
         for i in range(N_GROUPS)]
    el = jnp.where(gsel == 0, e[0], jnp.where(gsel == 1, e[1], jnp.where(gsel == 2, e[2], e[3])))
    row = lax.broadcasted_iota(jnp.int32, el.shape, 0).astype(F32)
    none = float(EXPERTS_PER_GROUP)
    m1 = jnp.max(el, axis=0, keepdims=True)
    i1 = jnp.min(jnp.where(el == m1, row, none), axis=0, keepdims=True)
    el2 = jnp.where(row == i1, -jnp.inf, el)
    m2 = jnp.max(el2, axis=0, keepdims=True)
    i2 = jnp.min(jnp.where(el2 == m2, row, none), axis=0, keepdims=True)
    r = jnp.exp(m2 - m1)
    w1 = gw / (1.0 + r)
    w2 = gw * r / (1.0 + r)
    base = gsel * EXPERTS_PER_GROUP
    ids = jnp.concatenate([base + i1.astype(jnp.int32), base + i2.astype(jnp.int32)], axis=0)
    return ids, jnp.concatenate([w1, w2], axis=0)


def _merge_body(gf_ref, gb_ref, rf_ref, rb_ref, gg_ref, rg_ref, mg_ref, mr_ref, x_ref,
                g1_ref, sh2_ref, sc2_ref, gn_ref, nf_ref, wg_ref, wr_ref, wo_ref, wrt_ref, brt_ref,
                h_ref, hn_ref, ids_ref, ew_ref):
    og = gf_ref[...].astype(F32) + gb_ref[...].astype(F32)
    orr = rf_ref[...].astype(F32) + rb_ref[...].astype(F32)
    gparts, rparts = [], []
    for hh in range(HEADS):
        seg = og[:, DV * hh:DV * (hh + 1)]
        ms = jnp.mean(seg * seg, axis=-1, keepdims=True)
        gparts.append(seg * lax.rsqrt(ms + NORM_EPS))
        seg = orr[:, DV * hh:DV * (hh + 1)]
        mu = jnp.mean(seg, axis=-1, keepdims=True)
        cen = seg - mu
        var = jnp.mean(cen * cen, axis=-1, keepdims=True)
        rparts.append(cen * lax.rsqrt(var + NORM_EPS))
    o_gla = jnp.concatenate(gparts, axis=1) * gn_ref[...] * _silu(gg_ref[...].astype(F32))
    o_ret = jnp.concatenate(rparts, axis=1) * _silu(rg_ref[...].astype(F32))
    y = (_sigmoid(mg_ref[...].astype(F32)) * _dot(o_gla.astype(BF16), wg_ref[...])
         + _sigmoid(mr_ref[...].astype(F32)) * _dot(o_ret.astype(BF16), wr_ref[...]))
    h = x_ref[...] + g1_ref[0] * _dot(y.astype(BF16), wo_ref[...])
    h_ref[...] = h
    ms = jnp.mean(h * h, axis=-1, keepdims=True)
    hn = h * lax.rsqrt(ms + NORM_EPS) * nf_ref[...] * (1.0 + sc2_ref[0]) + sh2_ref[0]
    hn_ref[...] = hn
    logits = jnp.dot(hn, wrt_ref[...], precision=lax.Precision.HIGHEST,
                     preferred_element_type=F32) + brt_ref[...]
    ids, ew = _route(logits.T)
    tm = ids.shape[1]
    ids_ref[...] = jnp.concatenate([ids, jnp.zeros((6, tm), jnp.int32)], axis=0)
    ew_ref[...] = jnp.concatenate([ew, jnp.zeros((6, tm), F32)], axis=0)


def _merge(gf, gb, rf, rb, p, x2, g1, sh2, sc2, gn, nf, wg, wr, wo, wrt, brt, seq, tm):
    t = x2.shape[0]
    tiles_per_batch = seq // tm
    row = lambda i: (i, 0)
    mod = lambda i: (i // tiles_per_batch, 0, 0)
    const = lambda i: (0, 0)
    tok = pl.BlockSpec((tm, D_MODEL), row)
    vec = pl.BlockSpec((1, D_MODEL), const)
    modspec = pl.BlockSpec((1, 1, D_MODEL), mod)
    wspec = pl.BlockSpec((D_MODEL, D_MODEL), const)

    def pcol(col0):
        return pl.BlockSpec((tm, D_MODEL), lambda i: (i, col0 // D_MODEL))

    return pl.pallas_call(
        _merge_body,
        grid=(t // tm,),
        in_specs=[tok, tok, tok, tok, pcol(COL_GG), pcol(COL_RG), pcol(COL_MG), pcol(COL_MR), tok,
                  modspec, modspec, modspec, vec, vec, wspec, wspec, wspec,
                  pl.BlockSpec((D_MODEL, ROUTER_W), const), pl.BlockSpec((1, ROUTER_W), const)],
        out_specs=[tok, tok, pl.BlockSpec((8, tm), lambda i: (0, i)), pl.BlockSpec((8, tm), lambda i: (0, i))],
        out_shape=[jax.ShapeDtypeStruct((t, D_MODEL), F32), jax.ShapeDtypeStruct((t, D_MODEL), F32),
                   jax.ShapeDtypeStruct((8, t), jnp.int32), jax.ShapeDtypeStruct((8, t), F32)],
        compiler_params=_cparams(("parallel",)),
        name="merge",
    )(gf, gb, rf, rb, p, p, p, p, x2, g1, sh2, sc2, gn, nf, wg, wr, wo, wrt, brt)


def _row_gather_copy(src_hbm, dst, sem, src_row, dst_row):
    return pltpu.make_async_copy(src_hbm.at[pl.ds(src_row, 1), :], dst.at[pl.ds(dst_row, 1), :], sem)


def _expert_body(be_ref, tok_ref, nused_ref, hn_hbm, wg_ref, wu_ref, wd_ref, y_ref, xbuf, sems):
    i = pl.program_id(0)
    nused = nused_ref[0]
    slot = i % 2

    def issue(block, s):
        def body(r, carry):
            _row_gather_copy(hn_hbm, xbuf.at[s], sems.at[s], tok_ref[block * MOE_BLOCK + r], r).start()
            return carry
        lax.fori_loop(0, MOE_BLOCK, body, 0)

    @pl.when(jnp.logical_and(i == 0, nused > 0))
    def _():
        issue(0, 0)

    @pl.when(i + 1 < nused)
    def _():
        issue(i + 1, 1 - slot)

    @pl.when(i < nused)
    def _():
        def wait_body(r, carry):
            _row_gather_copy(hn_hbm, xbuf.at[slot], sems.at[slot], 0, r).wait()
            return carry
        lax.fori_loop(0, MOE_BLOCK, wait_body, 0)
        xb = xbuf[slot].astype(BF16)
        gate = _dot(xb, wg_ref[0].astype(BF16))
        up = _dot(xb, wu_ref[0].astype(BF16))
        hid = (_silu(gate) * up).astype(BF16)
        y_ref[...] = _dot(hid, wd_ref[0].astype(BF16))

    @pl.when(i >= nused)
    def _():
        y_ref[...] = jnp.zeros_like(y_ref)


def _experts(block_e, tok_pad, nused, hn, w_gate, w_up, w_down):
    p_rows = tok_pad.shape[0]
    nblk = p_rows // MOE_BLOCK
    wmap = lambda i, be, tok, nu: (be[i], 0, 0)
    return pl.pallas_call(
        _expert_body,
        grid_spec=pltpu.PrefetchScalarGridSpec(
            num_scalar_prefetch=3,
            grid=(nblk,),
            in_specs=[pl.BlockSpec(memory_space=pl.ANY),
                      pl.BlockSpec((1, D_MODEL, EXPERT_FF), wmap),
                      pl.BlockSpec((1, D_MODEL, EXPERT_FF), wmap),
                      pl.BlockSpec((1, EXPERT_FF, D_MODEL), wmap)],
            out_specs=pl.BlockSpec((MOE_BLOCK, D_MODEL), lambda i, be, tok, nu: (i, 0)),
            scratch_shapes=[pltpu.VMEM((2, MOE_BLOCK, D_MODEL), F32),
                            pltpu.SemaphoreType.DMA((2,))]),
        out_shape=jax.ShapeDtypeStruct((p_rows, D_MODEL), F32),
        compiler_params=_cparams(("arbitrary",)),
        name="experts",
    )(block_e, tok_pad, nused, hn, w_gate, w_up, w_down)


COMBINE_TM = 128


def _combine_body(pos_ref, y_hbm, h_ref, ew_ref, g2_ref, nf_ref, o_ref, ybuf, sems):
    i = pl.program_id(0)
    n = pl.num_programs(0)
    slot = i % 2
    tm = COMBINE_TM

    def issue(tile, s):
        def body(r, carry):
            a = (tile * tm + r) * TOP_K
            _row_gather_copy(y_hbm, ybuf.at[s, 0], sems.at[s], pos_ref[a], r).start()
            _row_gather_copy(y_hbm, ybuf.at[s, 1], sems.at[s], pos_ref[a + 1], r).start()
            return carry
        lax.fori_loop(0, tm, body, 0)

    @pl.when(i == 0)
    def _():
        issue(0, 0)

    @pl.when(i + 1 < n)
    def _():
        issue(i + 1, 1 - slot)

    def wait_body(r, carry):
        _row_gather_copy(y_hbm, ybuf.at[slot, 0], sems.at[slot], 0, r).wait()
        _row_gather_copy(y_hbm, ybuf.at[slot, 1], sems.at[slot], 0, r).wait()
        return carry
    lax.fori_loop(0, tm, wait_body, 0)

    wt = jnp.concatenate([ew_ref[...]] * (LANES // 8), axis=0).T
    moe = wt[:, 0:1] * ybuf[slot, 0] + wt[:, 1:2] * ybuf[slot, 1]
    h = h_ref[...] + g2_ref[0] * moe
    ms = jnp.mean(h * h, axis=-1, keepdims=True)
    o_ref[...] = h * lax.rsqrt(ms + NORM_EPS) * nf_ref[...]


def _combine(pos, y_pad, h, ew, g2, nf, seq):
    t = h.shape[0]
    tm = COMBINE_TM
    tiles_per_batch = seq // tm
    return pl.pallas_call(
        _combine_body,
        grid_spec=pltpu.PrefetchScalarGridSpec(
            num_scalar_prefetch=1,
            grid=(t // tm,),
            in_specs=[pl.BlockSpec(memory_space=pl.ANY),
                      pl.BlockSpec((tm, D_MODEL), lambda i, pos: (i, 0)),
                      pl.BlockSpec((8, tm), lambda i, pos: (0, i)),
                      pl.BlockSpec((1, 1, D_MODEL), lambda i, pos: (i // tiles_per_batch, 0, 0)),
                      pl.BlockSpec((1, D_MODEL), lambda i, pos: (0, 0))],
            out_specs=pl.BlockSpec((tm, D_MODEL), lambda i, pos: (i, 0)),
            scratch_shapes=[pltpu.VMEM((2, TOP_K, tm, D_MODEL), F32),
                            pltpu.SemaphoreType.DMA((2,))]),
        out_shape=jax.ShapeDtypeStruct((t, D_MODEL), F32),
        compiler_params=_cparams(("arbitrary",)),
        name="combine",
    )(pos, y_pad, h, ew, g2, nf)


def _dispatch_indices(ids, t):
    a = t * TOP_K
    flat_e = ids.T.reshape(a)
    flat_tok = jnp.arange(a, dtype=jnp.int32) // TOP_K
    order = jnp.argsort(flat_e)
    se, stok = flat_e[order], flat_tok[order]
    counts = jnp.zeros((N_EXPERTS,), jnp.int32).at[flat_e].add(1)
    starts = jnp.cumsum(counts) - counts
    padded = (counts + MOE_BLOCK - 1) // MOE_BLOCK * MOE_BLOCK
    pends = jnp.cumsum(padded)
    pstarts = pends - padded
    dest = pstarts[se] + jnp.arange(a, dtype=jnp.int32) - starts[se]
    p_rows = a + N_EXPERTS * MOE_BLOCK
    nblk = p_rows // MOE_BLOCK
    tok_pad = jnp.zeros((p_rows,), jnp.int32).at[dest].set(stok)
    pos = jnp.zeros((a,), jnp.int32).at[order].set(dest)
    block_start = jnp.arange(nblk, dtype=jnp.int32) * MOE_BLOCK
    block_e = jnp.minimum(jnp.searchsorted(pends, block_start, side='right'), N_EXPERTS - 1).astype(jnp.int32)
    nused = (pends[-1:] // MOE_BLOCK).astype(jnp.int32)
    return block_e, tok_pad, nused, pos


def kernel(x, c, ctx, c_ctx, w_ada, b_ada, norm_mix, norm_ffn, w_in, gla_lr_w, gla_lr_b, gla_norm,
           w_branch_gla, w_branch_ret, w_out, w_router_group, b_router_group, w_router_expert,
           b_router_expert, w_expert_gate, w_expert_up, w_expert_down, norm_final):
    batch, seq, d = x.shape
    ctx_len = ctx.shape[1]
    assert d == D_MODEL and w_ada.shape[0] == 1, "single-layer block with D_MODEL features"
    t = batch * seq

    c8 = jnp.zeros((8, d), F32).at[:batch].set(c).at[batch].set(c_ctx)
    mod = _ada(c8, w_ada[0], b_ada[0][None, :])
    sh1, sc1, g1, sh2, sc2, g2 = [mod[:, d * i:d * (i + 1)] for i in range(6)]
    lat = lambda m: m[:batch, None, :]
    cx = lambda m: m[batch:batch + 1, None, :]

    w = w_in[0]
    lr0 = COL_GG + HEADS * DV
    w_main = jnp.concatenate([w[:, :lr0], w[:, lr0 + 2 * GLA_RANK:]], axis=1).astype(BF16)
    w_lr = jnp.pad(w[:, lr0:lr0 + 2 * GLA_RANK], ((0, 0), (0, LANES - 2 * GLA_RANK))).astype(BF16)
    nm = norm_mix[0][None, :]

    p_ctx, lr_ctx = _proj(ctx.reshape(batch * ctx_len, d), cx(sh1), cx(sc1), nm, w_main, w_lr,
                          tm=ctx_len, rows_per_mod=batch * ctx_len)
    x2 = x.reshape(t, d)
    p_lat, lr_lat = _proj(x2, lat(sh1), lat(sc1), nm, w_main, w_lr, tm=1024, rows_per_mod=seq)

    wlr_f = jnp.zeros((LANES, HEADS * DK), F32).at[:GLA_RANK].set(gla_lr_w[0, 0])
    wlr_b = jnp.zeros((LANES, HEADS * DK), F32).at[GLA_RANK:2 * GLA_RANK].set(gla_lr_w[0, 1])
    blr_f, blr_b = gla_lr_b[0, 0][None, :], gla_lr_b[0, 1][None, :]
    zero_state = jnp.zeros((batch, HEADS, DK, DV), F32)

    _, _, gs_f, gs_b = _gla(p_ctx, lr_ctx, wlr_f, blr_f, wlr_b, blr_b, zero_state, zero_state,
                            batch, ctx_len, tb=ctx_len)
    gla_f, gla_b, _, _ = _gla(p_lat, lr_lat, wlr_f, blr_f, wlr_b, blr_b, gs_f, gs_b, batch, seq, tb=512)

    ones = jnp.ones((ctx_len, DK), F32)
    _, _, rs_f, rs_b = _ret(p_ctx, ones, jnp.zeros_like(ones), zero_state, zero_state, batch, ctx_len, tb=ctx_len)
    cos, sin = _rope_tables(seq)
    ret_f, ret_b, _, _ = _ret(p_lat, cos, sin, rs_f, rs_b, batch, seq, tb=512)

    wrt = jnp.zeros((d, ROUTER_W), F32)
    wrt = wrt.at[:, :N_GROUPS].set(w_router_group[0]).at[:, EXP_ROW0:EXP_ROW0 + N_EXPERTS].set(w_router_expert[0])
    brt = jnp.zeros((1, ROUTER_W), F32)
    brt = brt.at[0, :N_GROUPS].set(b_router_group[0]).at[0, EXP_ROW0:EXP_ROW0 + N_EXPERTS].set(b_router_expert[0])
    h, hn, ids8, ew8 = _merge(gla_f, gla_b, ret_f, ret_b, p_lat, x2, lat(g1), lat(sh2), lat(sc2),
                              jnp.tile(gla_norm[0], HEADS)[None, :], norm_ffn[0][None, :],
                              w_branch_gla[0].astype(BF16), w_branch_ret[0].astype(BF16), w_out[0].astype(BF16),
                              wrt, brt, seq, tm=256)

    block_e, tok_pad, nused, pos = _dispatch_indices(ids8[:TOP_K], t)
    y_pad = _experts(block_e, tok_pad, nused, hn, w_expert_gate[0], w_expert_up[0], w_expert_down[0])
    out = _combine(pos, y_pad, h, ew8, lat(g2), norm_final[None, :], seq)
    return out.reshape(batch, seq, d)
```

```python
import functools

import numpy as np
import jax
import jax.numpy as jnp
from jax import lax
from jax.experimental import pallas as pl
from jax.experimental.pallas import tpu as pltpu

F32 = jnp.float32
BF16 = jnp.bfloat16

D_MODEL = 1024
GRID_W = 64
HEADS = 4
DK = 128
DV = 256
GLA_RANK = 16
GLA_TAU = 16.0
GLA_CHUNK = 64
GLA_SUB = 16
RET_CHUNK = 128
ROPE_BASE = 10000.0
N_GROUPS = 4
EXPERTS_PER_GROUP = 8
N_EXPERTS = N_GROUPS * EXPERTS_PER_GROUP
TOP_K = 2
EXPERT_FF = 256
MOE_BLOCK = 128
NORM_EPS = 1e-6

COL_GQ, COL_GK, COL_GV, COL_GG = 0, 512, 1024, 2048
COL_RQ, COL_RK, COL_RV, COL_RG = 3072, 3584, 4096, 5120
COL_MG, COL_MR = 6144, 7168
PROJ_W = 8192
LANES = 128
FEAT_TILES = D_MODEL // LANES
ROUTER_W = 128
EXP_ROW0 = 8

VMEM_LIMIT = 56 * 1024 * 1024


def _cparams(sem):
    return pltpu.CompilerParams(dimension_semantics=sem, vmem_limit_bytes=VMEM_LIMIT)


def _sigmoid(x):
    return 1.0 / (1.0 + jnp.exp(-x))


def _silu(x):
    return x * _sigmoid(x)


def _dot(a, b):
    return jnp.dot(a, b, preferred_element_type=F32)


def _dot_nt(a, b):
    return lax.dot_general(a, b, (((1,), (1,)), ((), ())), preferred_element_type=F32)


def _ada_body(c_ref, w_ref, b_ref, o_ref):
    s = _silu(c_ref[...])
    o_ref[...] = _dot(s.astype(BF16), w_ref[...].astype(BF16)) + b_ref[...]


def _ada(c8, w, b):
    n = w.shape[1]
    tn = 1536
    return pl.pallas_call(
        _ada_body,
        grid=(n // tn,),
        in_specs=[pl.BlockSpec((8, D_MODEL), lambda j: (0, 0)),
                  pl.BlockSpec((D_MODEL, tn), lambda j: (0, j)),
                  pl.BlockSpec((1, tn), lambda j: (0, j))],
        out_specs=pl.BlockSpec((8, tn), lambda j: (0, j)),
        out_shape=jax.ShapeDtypeStruct((8, n), F32),
        compiler_params=_cparams(("arbitrary",)),
        name="ada",
    )(c8, w, b)


def _proj_body(x_ref, sh_ref, sc_ref, g_ref, w_ref, wlr_ref, o_ref, lr_ref, h_ref):
    @pl.when(pl.program_id(1) == 0)
    def _():
        x = x_ref[...]
        ms = jnp.mean(x * x, axis=-1, keepdims=True)
        y = x * lax.rsqrt(ms + NORM_EPS) * g_ref[...]
        hb = (y * (1.0 + sc_ref[0]) + sh_ref[0]).astype(BF16)
        h_ref[...] = hb
        lr_ref[...] = _dot(hb, wlr_ref[...])

    o_ref[...] = _dot(h_ref[...], w_ref[...]).astype(BF16)


def _proj(x2, sh, sc, g, w_main, w_lr, tm, rows_per_mod):
    t = x2.shape[0]
    tn = 2048
    tiles_per_mod = rows_per_mod // tm
    mod_map = lambda i, j: (i // tiles_per_mod, 0, 0)
    return pl.pallas_call(
        _proj_body,
        grid=(t // tm, PROJ_W // tn),
        in_specs=[pl.BlockSpec((tm, D_MODEL), lambda i, j: (i, 0)),
                  pl.BlockSpec((1, 1, D_MODEL), mod_map),
                  pl.BlockSpec((1, 1, D_MODEL), mod_map),
                  pl.BlockSpec((1, D_MODEL), lambda i, j: (0, 0)),
                  pl.BlockSpec((D_MODEL, tn), lambda i, j: (0, j)),
                  pl.BlockSpec((D_MODEL, LANES), lambda i, j: (0, 0))],
        out_specs=[pl.BlockSpec((tm, tn), lambda i, j: (i, j)),
                   pl.BlockSpec((tm, LANES), lambda i, j: (i, 0))],
        out_shape=[jax.ShapeDtypeStruct((t, PROJ_W), BF16),
                   jax.ShapeDtypeStruct((t, LANES), F32)],
        scratch_shapes=[pltpu.VMEM((tm, D_MODEL), BF16)],
        compiler_params=_cparams(("parallel", "arbitrary")),
        name="proj",
    )(x2, sh, sc, g, w_main, w_lr)


def _gla_consts():
    c, s = GLA_CHUNK, GLA_SUB
    nsub = c // s
    t = np.arange(c)[:, None]
    u = np.arange(c)[None, :]
    tri = {}
    msk = {}
    for rev in (False, True):
        if not rev:
            cum = u <= t
            ref = u < s * (t // s)
        else:
            cum = u >= t
            ref = u >= s * (t // s + 1)
        tri[rev] = np.concatenate([cum, ref], axis=0).astype(np.float32)
        wide = np.zeros((c, nsub * c), np.float32)
        for i in range(nsub):
            rows = slice(s * i, s * (i + 1))
            causal = (u > t) if rev else (u <= t)
            wide[rows, c * i:c * (i + 1)] = causal[rows]
        msk[rev] = wide
    return tri, msk


def _log_gate(lr, w, b):
    z = jnp.dot(lr, w, precision=lax.Precision.HIGHEST, preferred_element_type=F32) + b
    return -(jnp.maximum(-z, 0.0) + jnp.log(1.0 + jnp.exp(-jnp.abs(z)))) * (1.0 / GLA_TAU)


def _gla_chunk(q, k, v, la, state, tri, mask, rev):
    c, s = GLA_CHUNK, GLA_SUB
    la_hi = la.astype(BF16)
    la_lo = (la - la_hi.astype(F32)).astype(BF16)
    cums = _dot(tri, la_hi) + _dot(tri, la_lo)
    b = cums[:c]
    ref = cums[c:]
    qt = q.astype(F32) * (DK ** -0.5) * jnp.exp(b - ref)
    kf = k.astype(F32)
    inter = _dot((qt * jnp.exp(ref)).astype(BF16), state.astype(BF16))

    row = lax.broadcasted_iota(jnp.int32, (c, DK), 0)
    kparts = []
    for i in range(c // s):
        ref_i = ref[s * i:s * i + 1]
        needed = (row >= s * i) if rev else (row < s * (i + 1))
        kparts.append(kf * jnp.exp(jnp.where(needed, ref_i - b, 0.0)))
    kstack = jnp.concatenate(kparts, axis=0).astype(BF16)
    scores = jnp.where(mask != 0.0, _dot_nt(qt.astype(BF16), kstack), 0.0)
    vstack = jnp.concatenate([v] * (c // s), axis=0)
    out = inter + _dot(scores.astype(BF16), vstack)

    b_end = b[0:1] if rev else b[c - 1:c]
    kst = kf * jnp.exp(b_end - b)
    xt = jnp.concatenate([kst, jnp.broadcast_to(jnp.exp(b_end), (c, DK))], axis=0).T
    new_state = xt[:, c:c + 1] * state + _dot(xt[:, :c].astype(BF16), v)
    return out, new_state


def _gla_body(qf_ref, kf_ref, vf_ref, lrf_ref, qb_ref, kb_ref, vb_ref, lrb_ref,
              wf_ref, bf_ref, wb_ref, bb_ref, trif_ref, trib_ref, mskf_ref, mskb_ref,
              s0f_ref, s0b_ref, of_ref, ob_ref, sf_ref, sb_ref, *, nchunks):
    @pl.when(pl.program_id(2) == 0)
    def _():
        sf_ref[...] = s0f_ref[...]
        sb_ref[...] = s0b_ref[...]

    c = GLA_CHUNK
    la_f = _log_gate(lrf_ref[...], wf_ref[...], bf_ref[...])
    la_b = _log_gate(lrb_ref[...], wb_ref[...], bb_ref[...])
    trif, trib = trif_ref[...], trib_ref[...]
    mskf, mskb = mskf_ref[...], mskb_ref[...]
    st_f = sf_ref[0, 0]
    st_b = sb_ref[0, 0]
    for n in range(nchunks):
        rf = slice(c * n, c * (n + 1))
        o, st_f = _gla_chunk(qf_ref[rf, :], kf_ref[rf, :], vf_ref[rf, :], la_f[rf, :], st_f, trif, mskf, False)
        of_ref[rf, :] = o.astype(BF16)
        m = nchunks - 1 - n
        rb = slice(c * m, c * (m + 1))
        o, st_b = _gla_chunk(qb_ref[rb, :], kb_ref[rb, :], vb_ref[rb, :], la_b[rb, :], st_b, trib, mskb, True)
        ob_ref[rb, :] = o.astype(BF16)
    sf_ref[0, 0] = st_f
    sb_ref[0, 0] = st_b


def _gla(p, lr, wlr_f, blr_f, wlr_b, blr_b, s0f, s0b, batch, seq, tb):
    nb = seq // tb
    tri, msk = _gla_consts()
    trif, trib = jnp.asarray(tri[False], BF16), jnp.asarray(tri[True], BF16)
    mskf, mskb = jnp.asarray(msk[False]), jnp.asarray(msk[True])

    def fwd(col0, width):
        return lambda b, h, i: (b * nb + i, col0 // width + h)

    def bwd(col0, width):
        return lambda b, h, i: (b * nb + nb - 1 - i, col0 // width + h)

    const2 = lambda b, h, i: (0, 0)
    headcol = lambda b, h, i: (0, h)
    st_map = lambda b, h, i: (b, h, 0, 0)
    st_spec = pl.BlockSpec((1, 1, DK, DV), st_map)
    st_shape = jax.ShapeDtypeStruct((batch, HEADS, DK, DV), F32)
    o_shape = jax.ShapeDtypeStruct((batch * seq, HEADS * DV), BF16)
    return pl.pallas_call(
        functools.partial(_gla_body, nchunks=tb // GLA_CHUNK),
        grid=(batch, HEADS, nb),
        in_specs=[pl.BlockSpec((tb, DK), fwd(COL_GQ, DK)),
                  pl.BlockSpec((tb, DK), fwd(COL_GK, DK)),
                  pl.BlockSpec((tb, DV), fwd(COL_GV, DV)),
                  pl.BlockSpec((tb, LANES), lambda b, h, i: (b * nb + i, 0)),
                  pl.BlockSpec((tb, DK), bwd(COL_GQ, DK)),
                  pl.BlockSpec((tb, DK), bwd(COL_GK, DK)),
                  pl.BlockSpec((tb, DV), bwd(COL_GV, DV)),
                  pl.BlockSpec((tb, LANES), lambda b, h, i: (b * nb + nb - 1 - i, 0)),
                  pl.BlockSpec((LANES, DK), headcol),
                  pl.BlockSpec((1, DK), headcol),
                  pl.BlockSpec((LANES, DK), headcol),
                  pl.BlockSpec((1, DK), headcol),
                  pl.BlockSpec(trif.shape, const2),
                  pl.BlockSpec(trib.shape, const2),
                  pl.BlockSpec(mskf.shape, const2),
                  pl.BlockSpec(mskb.shape, const2),
                  st_spec, st_spec],
        out_specs=[pl.BlockSpec((tb, DV), lambda b, h, i: (b * nb + i, h)),
                   pl.BlockSpec((tb, DV), lambda b, h, i: (b * nb + nb - 1 - i, h)),
                   st_spec, st_spec],
        out_shape=[o_shape, o_shape, st_shape, st_shape],
        compiler_params=_cparams(("parallel", "parallel", "arbitrary")),
        name="gla",
    )(p, p, p, lr, p, p, p, lr, wlr_f, blr_f, wlr_b, blr_b, trif, trib, mskf, mskb, s0f, s0b)


def _ret_consts():
    c = RET_CHUNK
    hh = np.arange(HEADS, dtype=np.float64)
    lg = {False: np.log1p(-np.exp2(-5.0 - hh)), True: np.log1p(-np.exp2(-5.5 - hh))}
    t = np.arange(c, dtype=np.float64)[:, None]
    u = np.arange(c, dtype=np.float64)[None, :]
    pos = np.arange(c, dtype=np.float64)
    out = {}
    for rev in (False, True):
        g = lg[rev][:, None, None]
        if not rev:
            dmat = np.where(u <= t, np.exp((t - u) * g), 0.0)
            qd = np.exp((pos + 1.0)[None, :] * lg[rev][:, None])
            kd = np.exp((c - 1.0 - pos)[None, :] * lg[rev][:, None])
        else:
            dmat = np.where(u > t, np.exp((u - t) * g), 0.0)
            qd = np.exp((c - pos)[None, :] * lg[rev][:, None])
            kd = np.exp(pos[None, :] * lg[rev][:, None])
        cd = np.exp(c * lg[rev])
        out[rev] = (dmat.astype(np.float32),
                    np.repeat(qd[:, :, None], DV, axis=2).astype(np.float32),
                    np.repeat(kd[:, :, None], DK, axis=2).astype(np.float32),
                    np.repeat(cd[:, None, None], DV, axis=2).astype(np.float32))
    return out


def _rope(x, cos, sin):
    lane = lax.broadcasted_iota(jnp.int32, x.shape, 1)
    partner = jnp.where((lane % 64) < 32, pltpu.roll(x, DK - 32, 1), pltpu.roll(x, 32, 1))
    return x * cos + partner * sin


def _ret_chunk(q, k, v, cos, sin, state, dmat, qd, kd, cd):
    qr = _rope(q.astype(F32), cos, sin).astype(BF16)
    kr = _rope(k.astype(F32) * (DK ** -0.5), cos, sin)
    inter = _dot(qr, state.astype(BF16)) * qd
    scores = _dot_nt(qr, kr.astype(BF16)) * dmat
    out = inter + _dot(scores.astype(BF16), v)
    new_state = cd * state + _dot((kr * kd).T.astype(BF16), v)
    return out, new_state


def _ret_body(qf_ref, kf_ref, vf_ref, cosf_ref, sinf_ref, qb_ref, kb_ref, vb_ref, cosb_ref, sinb_ref,
              dmf_ref, qdf_ref, kdf_ref, cdf_ref, dmb_ref, qdb_ref, kdb_ref, cdb_ref,
              s0f_ref, s0b_ref, of_ref, ob_ref, sf_ref, sb_ref, *, nchunks):
    @pl.when(pl.program_id(2) == 0)
    def _():
        sf_ref[...] = s0f_ref[...]
        sb_ref[...] = s0b_ref[...]

    c = RET_CHUNK
    cf = (dmf_ref[0], qdf_ref[0], kdf_ref[0], cdf_ref[0])
    cb = (dmb_ref[0], qdb_ref[0], kdb_ref[0], cdb_ref[0])
    st_f = sf_ref[0, 0]
    st_b = sb_ref[0, 0]
    for n in range(nchunks):
        rf = slice(c * n, c * (n + 1))
        o, st_f = _ret_chunk(qf_ref[rf, :], kf_ref[rf, :], vf_ref[rf, :], cosf_ref[rf, :], sinf_ref[rf, :],
                             st_f, *cf)
        of_ref[rf, :] = o.astype(BF16)
        m = nchunks - 1 - n
        rb = slice(c * m, c * (m + 1))
        o, st_b = _ret_chunk(qb_ref[rb, :], kb_ref[rb, :], vb_ref[rb, :], cosb_ref[rb, :], sinb_ref[rb, :],
                             st_b, *cb)
        ob_ref[rb, :] = o.astype(BF16)
    sf_ref[0, 0] = st_f
    sb_ref[0, 0] = st_b


def _ret(p, cos, sin, s0f, s0b, batch, seq, tb):
    nb = seq // tb
    consts = _ret_consts()

    def fwd(col0, width):
        return lambda b, h, i: (b * nb + i, col0 // width + h)

    def bwd(col0, width):
        return lambda b, h, i: (b * nb + nb - 1 - i, col0 // width + h)

    head3 = lambda b, h, i: (h, 0, 0)
    st_spec = pl.BlockSpec((1, 1, DK, DV), lambda b, h, i: (b, h, 0, 0))
    st_shape = jax.ShapeDtypeStruct((batch, HEADS, DK, DV), F32)
    o_shape = jax.ShapeDtypeStruct((batch * seq, HEADS * DV), BF16)
    const_specs = []
    const_args = []
    for rev in (False, True):
        for a in consts[rev]:
            const_specs.append(pl.BlockSpec((1,) + a.shape[1:], head3))
            const_args.append(jnp.asarray(a))
    return pl.pallas_call(
        functools.partial(_ret_body, nchunks=tb // RET_CHUNK),
        grid=(batch, HEADS, nb),
        in_specs=[pl.BlockSpec((tb, DK), fwd(COL_RQ, DK)),
                  pl.BlockSpec((tb, DK), fwd(COL_RK, DK)),
                  pl.BlockSpec((tb, DV), fwd(COL_RV, DV)),
                  pl.BlockSpec((tb, DK), lambda b, h, i: (i, 0)),
                  pl.BlockSpec((tb, DK), lambda b, h, i: (i, 0)),
                  pl.BlockSpec((tb, DK), bwd(COL_RQ, DK)),
                  pl.BlockSpec((tb, DK), bwd(COL_RK, DK)),
                  pl.BlockSpec((tb, DV), bwd(COL_RV, DV)),
                  pl.BlockSpec((tb, DK), lambda b, h, i: (nb - 1 - i, 0)),
                  pl.BlockSpec((tb, DK), lambda b, h, i: (nb - 1 - i, 0))]
                 + const_specs + [st_spec, st_spec],
        out_specs=[pl.BlockSpec((tb, DV), lambda b, h, i: (b * nb + i, h)),
                   pl.BlockSpec((tb, DV), lambda b, h, i: (b * nb + nb - 1 - i, h)),
                   st_spec, st_spec],
        out_shape=[o_shape, o_shape, st_shape, st_shape],
        compiler_params=_cparams(("parallel", "parallel", "arbitrary")),
        name="ret",
    )(p, p, p, cos, sin, p, p, p, cos, sin, *const_args, s0f, s0b)


def _rope_tables(seq):
    n = DK // 4
    inv = ROPE_BASE ** (-jnp.arange(n, dtype=F32) / n)
    pos = jnp.arange(seq, dtype=jnp.int32)
    rows = (pos // GRID_W).astype(F32)
    cols = (pos % GRID_W).astype(F32)
    ar = rows[:, None] * inv[None, :]
    ac = cols[:, None] * inv[None, :]
    cos = jnp.concatenate([jnp.cos(ar), jnp.cos(ar), jnp.cos(ac), jnp.cos(ac)], axis=1)
    sin = jnp.concatenate([-jnp.sin(ar), jnp.sin(ar), -jnp.sin(ac), jnp.sin(ac)], axis=1)
    return cos, sin


def _route(logits_t):
    g = [logits_t[i:i + 1] for i in range(N_GROUPS)]
    gmax = jnp.maximum(jnp.maximum(g[0], g[1]), jnp.maximum(g[2], g[3]))
    gsel = jnp.where(g[0] == gmax, 0, jnp.where(g[1] == gmax, 1, jnp.where(g[2] == gmax, 2, 3)))
    gsum = (jnp.exp(g[0] - gmax) + jnp.exp(g[1] - gmax)) + (jnp.exp(g[2] - gmax) + jnp.exp(g[3] - gmax))
    gw = 1.0 / gsum
    e = [logits_t[EXP_ROW0 + EXPERTS_PER_GROUP * i:EXP_ROW0 + EXPERTS_PER_GROUP * (i + 1)]
         for i in range(N_GROUPS)]
    el = jnp.where(gsel == 0, e[0], jnp.where(gsel == 1, e[1], jnp.where(gsel == 2, e[2], e[3])))
    row = lax.broadcasted_iota(jnp.int32, el.shape, 0).astype(F32)
    none = float(EXPERTS_PER_GROUP)
    m1 = jnp.max(el, axis=0, keepdims=True)
    i1 = jnp.min(jnp.where(el == m1, row, none), axis=0, keepdims=True)
    el2 = jnp.where(row == i1, -jnp.inf, el)
    m2 = jnp.max(el2, axis=0, keepdims=True)
    i2 = jnp.min(jnp.where(el2 == m2, row, none), axis=0, keepdims=True)
    r = jnp.exp(m2 - m1)
    w1 = gw / (1.0 + r)
    w2 = gw * r / (1.0 + r)
    base = gsel * EXPERTS_PER_GROUP
    ids = jnp.concatenate([base + i1.astype(jnp.int32), base + i2.astype(jnp.int32)], axis=0)
    return ids, jnp.concatenate([w1, w2], axis=0)


def _store_token_tiled(ref, val):
    m = val.shape[0]
    for j in range(FEAT_TILES):
        ref[pl.ds(j, m, stride=FEAT_TILES), :] = val[:, LANES * j:LANES * (j + 1)]


def _load_token_tiled(ref, m):
    return jnp.concatenate([ref[pl.ds(j, m, stride=FEAT_TILES), :] for j in range(FEAT_TILES)], axis=1)


def _merge_body(gf_ref, gb_ref, rf_ref, rb_ref, gg_ref, rg_ref, mg_ref, mr_ref, x_ref,
                g1_ref, sh2_ref, sc2_ref, gn_ref, nf_ref, wg_ref, wr_ref, wo_ref, wrt_ref, brt_ref, upper_ref,
                h_ref, hn_ref, ids_ref, ew_ref, cnt_ref):
    @pl.when(pl.program_id(0) == 0)
    def _():
        cnt_ref[...] = jnp.zeros_like(cnt_ref)

    og = gf_ref[...].astype(F32) + gb_ref[...].astype(F32)
    orr = rf_ref[...].astype(F32) + rb_ref[...].astype(F32)
    gparts, rparts = [], []
    for hh in range(HEADS):
        seg = og[:, DV * hh:DV * (hh + 1)]
        ms = jnp.mean(seg * seg, axis=-1, keepdims=True)
        gparts.append(seg * lax.rsqrt(ms + NORM_EPS))
        seg = orr[:, DV * hh:DV * (hh + 1)]
        mu = jnp.mean(seg, axis=-1, keepdims=True)
        cen = seg - mu
        var = jnp.mean(cen * cen, axis=-1, keepdims=True)
        rparts.append(cen * lax.rsqrt(var + NORM_EPS))
    o_gla = jnp.concatenate(gparts, axis=1) * gn_ref[...] * _silu(gg_ref[...].astype(F32))
    o_ret = jnp.concatenate(rparts, axis=1) * _silu(rg_ref[...].astype(F32))
    y = (_sigmoid(mg_ref[...].astype(F32)) * _dot(o_gla.astype(BF16), wg_ref[...])
         + _sigmoid(mr_ref[...].astype(F32)) * _dot(o_ret.astype(BF16), wr_ref[...]))
    h = x_ref[...] + g1_ref[0] * _dot(y.astype(BF16), wo_ref[...])
    h_ref[...] = h
    ms = jnp.mean(h * h, axis=-1, keepdims=True)
    hn = h * lax.rsqrt(ms + NORM_EPS) * nf_ref[...] * (1.0 + sc2_ref[0]) + sh2_ref[0]
    _store_token_tiled(hn_ref, hn)
    logits = jnp.dot(hn, wrt_ref[...], precision=lax.Precision.HIGHEST,
                     preferred_element_type=F32) + brt_ref[...]
    ids, ew = _route(logits.T)
    tm = ids.shape[1]

    erow = lax.broadcasted_iota(jnp.int32, (N_EXPERTS, tm), 0)
    oh0 = jnp.where(erow == ids[0:1], 1.0, 0.0)
    oh1 = jnp.where(erow == ids[1:2], 1.0, 0.0)
    both = oh0 + oh1
    before = _dot(both.astype(BF16), upper_ref[...]) + cnt_ref[:, 0:1].astype(F32)
    rank0 = jnp.sum(oh0 * before, axis=0, keepdims=True)
    rank1 = jnp.sum(oh1 * before, axis=0, keepdims=True)
    total = cnt_ref[:, 0:1] + jnp.sum(both, axis=1, keepdims=True).astype(jnp.int32)
    cnt_ref[...] = jnp.broadcast_to(total, cnt_ref.shape)
    ids_ref[...] = jnp.concatenate([ids, rank0.astype(jnp.int32), rank1.astype(jnp.int32),
                                    jnp.zeros((4, tm), jnp.int32)], axis=0)
    ew_ref[...] = jnp.concatenate([ew, jnp.zeros((6, tm), F32)], axis=0)


def _merge(gf, gb, rf, rb, p, x2, g1, sh2, sc2, gn, nf, wg, wr, wo, wrt, brt, seq, tm):
    t = x2.shape[0]
    tiles_per_batch = seq // tm
    row = lambda i: (i, 0)
    mod = lambda i: (i // tiles_per_batch, 0, 0)
    const = lambda i: (0, 0)
    tok = pl.BlockSpec((tm, D_MODEL), row)
    vec = pl.BlockSpec((1, D_MODEL), const)
    modspec = pl.BlockSpec((1, 1, D_MODEL), mod)
    wspec = pl.BlockSpec((D_MODEL, D_MODEL), const)
    upper = jnp.asarray(np.triu(np.ones((tm, tm), np.float32), 1), BF16)

    def pcol(col0):
        return pl.BlockSpec((tm, D_MODEL), lambda i: (i, col0 // D_MODEL))

    return pl.pallas_call(
        _merge_body,
        grid=(t // tm,),
        in_specs=[tok, tok, tok, tok, pcol(COL_GG), pcol(COL_RG), pcol(COL_MG), pcol(COL_MR), tok,
                  modspec, modspec, modspec, vec, vec, wspec, wspec, wspec,
                  pl.BlockSpec((D_MODEL, ROUTER_W), const), pl.BlockSpec((1, ROUTER_W), const),
                  pl.BlockSpec((tm, tm), const)],
        out_specs=[tok, pl.BlockSpec((tm * FEAT_TILES, LANES), row),
                   pl.BlockSpec((8, tm), lambda i: (0, i)), pl.BlockSpec((8, tm), lambda i: (0, i)),
                   pl.BlockSpec((N_EXPERTS, LANES), const)],
        out_shape=[jax.ShapeDtypeStruct((t, D_MODEL), F32), jax.ShapeDtypeStruct((t * FEAT_TILES, LANES), F32),
                   jax.ShapeDtypeStruct((8, t), jnp.int32), jax.ShapeDtypeStruct((8, t), F32),
                   jax.ShapeDtypeStruct((N_EXPERTS, LANES), jnp.int32)],
        compiler_params=_cparams(("arbitrary",)),
        name="merge",
    )(gf, gb, rf, rb, p, p, p, p, x2, g1, sh2, sc2, gn, nf, wg, wr, wo, wrt, brt, upper)


GATHER_UNROLL = 8


def _token_copy(src_hbm, dst, sem, src_tok, dst_tok):
    return pltpu.make_async_copy(
        src_hbm.at[pl.ds(pl.multiple_of(src_tok * FEAT_TILES, FEAT_TILES), FEAT_TILES), :],
        dst.at[pl.ds(pl.multiple_of(dst_tok * FEAT_TILES, FEAT_TILES), FEAT_TILES), :], sem)


def _wait_tokens(src_hbm, dst, sem):
    pltpu.make_async_copy(src_hbm.at[pl.ds(0, dst.shape[0]), :], dst, sem).wait()


def _expert_body(be_ref, src0_ref, stok_ref, nused_ref, hn_hbm, wg_ref, wu_ref, wd_ref, y_ref, xbuf, sems):
    i = pl.program_id(0)
    nused = nused_ref[0]
    slot = i % 2
    last = stok_ref.shape[0] - 1

    def issue(block, s):
        base = src0_ref[block]

        def body(g, carry):
            for u in range(GATHER_UNROLL):
                r = g * GATHER_UNROLL + u
                tok = stok_ref[jnp.minimum(base + r, last)]
                _token_copy(hn_hbm, xbuf.at[s], sems.at[s], tok, r).start()
            return carry
        lax.fori_loop(0, MOE_BLOCK // GATHER_UNROLL, body, 0)

    @pl.when(jnp.logical_and(i == 0, nused > 0))
    def _():
        issue(0, 0)

    @pl.when(i + 1 < nused)
    def _():
        issue(i + 1, 1 - slot)

    @pl.when(i < nused)
    def _():
        _wait_tokens(hn_hbm, xbuf.at[slot], sems.at[slot])
        xb = _load_token_tiled(xbuf.at[slot], MOE_BLOCK).astype(BF16)
        gate = _dot(xb, wg_ref[0].astype(BF16))
        up = _dot(xb, wu_ref[0].astype(BF16))
        hid = (_silu(gate) * up).astype(BF16)
        _store_token_tiled(y_ref, _dot(hid, wd_ref[0].astype(BF16)))

    @pl.when(i >= nused)
    def _():
        y_ref[...] = jnp.zeros_like(y_ref)


def _experts(block_e, src0, stok, nused, hn, w_gate, w_up, w_down, nblk):
    wmap = lambda i, be, s0, st, nu: (be[i], 0, 0)
    rows = MOE_BLOCK * FEAT_TILES
    return pl.pallas_call(
        _expert_body,
        grid_spec=pltpu.PrefetchScalarGridSpec(
            num_scalar_prefetch=4,
            grid=(nblk,),
            in_specs=[pl.BlockSpec(memory_space=pl.ANY),
                      pl.BlockSpec((1, D_MODEL, EXPERT_FF), wmap),
                      pl.BlockSpec((1, D_MODEL, EXPERT_FF), wmap),
                      pl.BlockSpec((1, EXPERT_FF, D_MODEL), wmap)],
            out_specs=pl.BlockSpec((rows, LANES), lambda i, be, s0, st, nu: (i, 0)),
            scratch_shapes=[pltpu.VMEM((2, rows, LANES), F32),
                            pltpu.SemaphoreType.DMA((2,))]),
        out_shape=jax.ShapeDtypeStruct((nblk * rows, LANES), F32),
        compiler_params=_cparams(("arbitrary",)),
        name="experts",
    )(block_e, src0, stok, nused, hn, w_gate, w_up, w_down)


COMBINE_TM = 128


def _combine_body(pos_ref, y_hbm, h_ref, ew_ref, g2_ref, nf_ref, o_ref, ybuf, sems):
    i = pl.program_id(0)
    n = pl.num_programs(0)
    slot = i % 2
    tm = COMBINE_TM

    def issue(tile, s):
        def body(g, carry):
            for u in range(GATHER_UNROLL // TOP_K):
                r = g * (GATHER_UNROLL // TOP_K) + u
                a = (tile * tm + r) * TOP_K
                for k in range(TOP_K):
                    _token_copy(y_hbm, ybuf.at[s, k], sems.at[s], pos_ref[a + k], r).start()
            return carry
        lax.fori_loop(0, tm * TOP_K // GATHER_UNROLL, body, 0)

    @pl.when(i == 0)
    def _():
        issue(0, 0)

    @pl.when(i + 1 < n)
    def _():
        issue(i + 1, 1 - slot)

    for k in range(TOP_K):
        _wait_tokens(y_hbm, ybuf.at[slot, k], sems.at[slot])

    wt = jnp.concatenate([ew_ref[...]] * (LANES // 8), axis=0).T
    moe = (wt[:, 0:1] * _load_token_tiled(ybuf.at[slot, 0], tm)
           + wt[:, 1:2] * _load_token_tiled(ybuf.at[slot, 1], tm))
    h = h_ref[...] + g2_ref[0] * moe
    ms = jnp.mean(h * h, axis=-1, keepdims=True)
    o_ref[...] = h * lax.rsqrt(ms + NORM_EPS) * nf_ref[...]


def _combine(pos, y_pad, h, ew, g2, nf, seq):
    t = h.shape[0]
    tm = COMBINE_TM
    tiles_per_batch = seq // tm
    return pl.pallas_call(
        _combine_body,
        grid_spec=pltpu.PrefetchScalarGridSpec(
            num_scalar_prefetch=1,
            grid=(t // tm,),
            in_specs=[pl.BlockSpec(memory_space=pl.ANY),
                      pl.BlockSpec((tm, D_MODEL), lambda i, pos: (i, 0)),
                      pl.BlockSpec((8, tm), lambda i, pos: (0, i)),
                      pl.BlockSpec((1, 1, D_MODEL), lambda i, pos: (i // tiles_per_batch, 0, 0)),
                      pl.BlockSpec((1, D_MODEL), lambda i, pos: (0, 0))],
            out_specs=pl.BlockSpec((tm, D_MODEL), lambda i, pos: (i, 0)),
            scratch_shapes=[pltpu.VMEM((2, TOP_K, tm * FEAT_TILES, LANES), F32),
                            pltpu.SemaphoreType.DMA((2,))]),
        out_shape=jax.ShapeDtypeStruct((t, D_MODEL), F32),
        compiler_params=_cparams(("arbitrary",)),
        name="combine",
    )(pos, y_pad, h, ew, g2, nf)


def _dispatch_indices(ids, ranks, counts, t):
    a = t * TOP_K
    nblk = a // MOE_BLOCK + N_EXPERTS
    experts = jnp.arange(N_EXPERTS, dtype=jnp.int32)
    starts = jnp.cumsum(counts) - counts
    padded = (counts + MOE_BLOCK - 1) // MOE_BLOCK * MOE_BLOCK
    pends = jnp.cumsum(padded)
    pstarts = pends - padded
    block_start = jnp.arange(nblk, dtype=jnp.int32) * MOE_BLOCK
    block_e = jnp.minimum(jnp.sum((block_start[:, None] >= pends[None, :]).astype(jnp.int32), axis=1),
                          N_EXPERTS - 1)
    of_block = (block_e[:, None] == experts[None, :]).astype(jnp.int32)
    src0 = block_start + jnp.sum(of_block * (starts - pstarts)[None, :], axis=1)
    nused = pends[-1:] // MOE_BLOCK
    flat_e = ids.T.reshape(a)
    stok = (jnp.argsort(flat_e, stable=True) // TOP_K).astype(jnp.int32)
    of_assign = (ids[:, :, None] == experts[None, None, :]).astype(jnp.int32)
    pos = (ranks + jnp.sum(of_assign * pstarts[None, None, :], axis=2)).T.reshape(a)
    return block_e.astype(jnp.int32), src0.astype(jnp.int32), stok, nused.astype(jnp.int32), pos.astype(jnp.int32), nblk


def kernel(x, c, ctx, c_ctx, w_ada, b_ada, norm_mix, norm_ffn, w_in, gla_lr_w, gla_lr_b, gla_norm,
           w_branch_gla, w_branch_ret, w_out, w_router_group, b_router_group, w_router_expert,
           b_router_expert, w_expert_gate, w_expert_up, w_expert_down, norm_final):
    batch, seq, d = x.shape
    ctx_len = ctx.shape[1]
    assert d == D_MODEL and w_ada.shape[0] == 1, "single-layer block with D_MODEL features"
    t = batch * seq

    c8 = jnp.zeros((8, d), F32).at[:batch].set(c).at[batch].set(c_ctx)
    mod = _ada(c8, w_ada[0], b_ada[0][None, :])
    sh1, sc1, g1, sh2, sc2, g2 = [mod[:, d * i:d * (i + 1)] for i in range(6)]
    lat = lambda m: m[:batch, None, :]
    cx = lambda m: m[batch:batch + 1, None, :]

    w = w_in[0]
    lr0 = COL_GG + HEADS * DV
    w_main = jnp.concatenate([w[:, :lr0], w[:, lr0 + 2 * GLA_RANK:]], axis=1).astype(BF16)
    w_lr = jnp.pad(w[:, lr0:lr0 + 2 * GLA_RANK], ((0, 0), (0, LANES - 2 * GLA_RANK))).astype(BF16)
    nm = norm_mix[0][None, :]

    p_ctx, lr_ctx = _proj(ctx.reshape(batch * ctx_len, d), cx(sh1), cx(sc1), nm, w_main, w_lr,
                          tm=ctx_len, rows_per_mod=batch * ctx_len)
    x2 = x.reshape(t, d)
    p_lat, lr_lat = _proj(x2, lat(sh1), lat(sc1), nm, w_main, w_lr, tm=1024, rows_per_mod=seq)

    wlr_f = jnp.zeros((LANES, HEADS * DK), F32).at[:GLA_RANK].set(gla_lr_w[0, 0])
    wlr_b = jnp.zeros((LANES, HEADS * DK), F32).at[GLA_RANK:2 * GLA_RANK].set(gla_lr_w[0, 1])
    blr_f, blr_b = gla_lr_b[0, 0][None, :], gla_lr_b[0, 1][None, :]
    zero_state = jnp.zeros((batch, HEADS, DK, DV), F32)

    _, _, gs_f, gs_b = _gla(p_ctx, lr_ctx, wlr_f, blr_f, wlr_b, blr_b, zero_state, zero_state,
                            batch, ctx_len, tb=ctx_len)
    gla_f, gla_b, _, _ = _gla(p_lat, lr_lat, wlr_f, blr_f, wlr_b, blr_b, gs_f, gs_b, batch, seq, tb=512)

    ones = jnp.ones((ctx_len, DK), F32)
    _, _, rs_f, rs_b = _ret(p_ctx, ones, jnp.zeros_like(ones), zero_state, zero_state, batch, ctx_len, tb=ctx_len)
    cos, sin = _rope_tables(seq)
    ret_f, ret_b, _, _ = _ret(p_lat, cos, sin, rs_f, rs_b, batch, seq, tb=512)

    wrt = jnp.zeros((d, ROUTER_W), F32)
    wrt = wrt.at[:, :N_GROUPS].set(w_router_group[0]).at[:, EXP_ROW0:EXP_ROW0 + N_EXPERTS].set(w_router_expert[0])
    brt = jnp.zeros((1, ROUTER_W), F32)
    brt = brt.at[0, :N_GROUPS].set(b_router_group[0]).at[0, EXP_ROW0:EXP_ROW0 + N_EXPERTS].set(b_router_expert[0])
    h, hn, ids8, ew8, cnt = _merge(gla_f, gla_b, ret_f, ret_b, p_lat, x2, lat(g1), lat(sh2), lat(sc2),
                                   jnp.tile(gla_norm[0], HEADS)[None, :], norm_ffn[0][None, :],
                                   w_branch_gla[0].astype(BF16), w_branch_ret[0].astype(BF16),
                                   w_out[0].astype(BF16), wrt, brt, seq, tm=256)

    block_e, src0, stok, nused, pos, nblk = _dispatch_indices(ids8[:TOP_K], ids8[TOP_K:2 * TOP_K], cnt[:, 0], t)
    y_pad = _experts(block_e, src0, stok, nused, hn, w_expert_gate[0], w_expert_up[0], w_expert_down[0], nblk)
    out = _combine(pos, y_pad, h, ew8, lat(g2), norm_final[None, :], seq)
    return out.reshape(batch, seq, d)
```

```python
import functools

import numpy as np
import jax
import jax.numpy as jnp
from jax import lax
from jax.experimental import pallas as pl
from jax.experimental.pallas import tpu as pltpu

F32 = jnp.float32
BF16 = jnp.bfloat16

D_MODEL = 1024
GRID_W = 64
HEADS = 4
DK = 128
DV = 256
GLA_RANK = 16
GLA_TAU = 16.0
GLA_CHUNK = 64
GLA_SUB = 16
RET_CHUNK = 128
ROPE_BASE = 10000.0
N_GROUPS = 4
EXPERTS_PER_GROUP = 8
N_EXPERTS = N_GROUPS * EXPERTS_PER_GROUP
TOP_K = 2
EXPERT_FF = 256
MOE_BLOCK = 128
NORM_EPS = 1e-6

COL_GQ, COL_GK, COL_GV, COL_GG = 0, 512, 1024, 2048
COL_RQ, COL_RK, COL_RV, COL_RG = 3072, 3584, 4096, 5120
COL_MG, COL_MR = 6144, 7168
PROJ_W = 8192
LANES = 128
FEAT_TILES = D_MODEL // LANES
ROUTER_W = 128
EXP_ROW0 = 8

VMEM_LIMIT = 56 * 1024 * 1024


def _cparams(sem):
    return pltpu.CompilerParams(dimension_semantics=sem, vmem_limit_bytes=VMEM_LIMIT)


def _sigmoid(x):
    return 1.0 / (1.0 + jnp.exp(-x))


def _silu(x):
    return x * _sigmoid(x)


def _dot(a, b):
    return jnp.dot(a, b, preferred_element_type=F32)


def _dot_nt(a, b):
    return lax.dot_general(a, b, (((1,), (1,)), ((), ())), preferred_element_type=F32)


def _ada_body(c_ref, w_ref, b_ref, o_ref):
    s = _silu(c_ref[...])
    o_ref[...] = _dot(s.astype(BF16), w_ref[...].astype(BF16)) + b_ref[...]


def _ada(c8, w, b):
    n = w.shape[1]
    tn = 1536
    return pl.pallas_call(
        _ada_body,
        grid=(n // tn,),
        in_specs=[pl.BlockSpec((8, D_MODEL), lambda j: (0, 0)),
                  pl.BlockSpec((D_MODEL, tn), lambda j: (0, j)),
                  pl.BlockSpec((1, tn), lambda j: (0, j))],
        out_specs=pl.BlockSpec((8, tn), lambda j: (0, j)),
        out_shape=jax.ShapeDtypeStruct((8, n), F32),
        compiler_params=_cparams(("arbitrary",)),
        name="ada",
    )(c8, w, b)


def _proj_body(x_ref, sh_ref, sc_ref, g_ref, w_ref, wlr_ref, o_ref, lr_ref, h_ref):
    @pl.when(pl.program_id(1) == 0)
    def _():
        x = x_ref[...]
        ms = jnp.mean(x * x, axis=-1, keepdims=True)
        y = x * lax.rsqrt(ms + NORM_EPS) * g_ref[...]
        hb = (y * (1.0 + sc_ref[0]) + sh_ref[0]).astype(BF16)
        h_ref[...] = hb
        lr_ref[...] = _dot(hb, wlr_ref[...])

    o_ref[...] = _dot(h_ref[...], w_ref[...]).astype(BF16)


def _proj(x2, sh, sc, g, w_main, w_lr, tm, rows_per_mod):
    t = x2.shape[0]
    tn = 2048
    tiles_per_mod = rows_per_mod // tm
    mod_map = lambda i, j: (i // tiles_per_mod, 0, 0)
    return pl.pallas_call(
        _proj_body,
        grid=(t // tm, PROJ_W // tn),
        in_specs=[pl.BlockSpec((tm, D_MODEL), lambda i, j: (i, 0)),
                  pl.BlockSpec((1, 1, D_MODEL), mod_map),
                  pl.BlockSpec((1, 1, D_MODEL), mod_map),
                  pl.BlockSpec((1, D_MODEL), lambda i, j: (0, 0)),
                  pl.BlockSpec((D_MODEL, tn), lambda i, j: (0, j)),
                  pl.BlockSpec((D_MODEL, LANES), lambda i, j: (0, 0))],
        out_specs=[pl.BlockSpec((tm, tn), lambda i, j: (i, j)),
                   pl.BlockSpec((tm, LANES), lambda i, j: (i, 0))],
        out_shape=[jax.ShapeDtypeStruct((t, PROJ_W), BF16),
                   jax.ShapeDtypeStruct((t, LANES), F32)],
        scratch_shapes=[pltpu.VMEM((tm, D_MODEL), BF16)],
        compiler_params=_cparams(("parallel", "arbitrary")),
        name="proj",
    )(x2, sh, sc, g, w_main, w_lr)


def _gla_consts():
    c, s = GLA_CHUNK, GLA_SUB
    msk = {}
    for rev in (False, True):
        cols = []
        for i in range(c // s):
            keys = np.arange(c)[_key_rows(i, rev)][None, :]
            t = np.arange(c)[:, None]
            visible = (keys > t) if rev else (keys <= t)
            cols.append(np.where(t // s == i, visible, False))
        msk[rev] = np.concatenate(cols, axis=1).astype(np.float32)
    return msk


def _key_rows(i, rev):
    return slice(GLA_SUB * i, GLA_CHUNK) if rev else slice(0, GLA_SUB * (i + 1))


def _log_gate(lr, w, b):
    z = _dot(lr.astype(BF16), w.astype(BF16)) + b
    return -(jnp.maximum(-z, 0.0) + jnp.log(1.0 + jnp.exp(-jnp.abs(z)))) * (1.0 / GLA_TAU)


def _subchunk_scan(la, rev):
    c = la.shape[0]
    pos = lax.broadcasted_iota(jnp.int32, la.shape, 0) % GLA_SUB
    w = la
    step = 1
    while step < GLA_SUB:
        if rev:
            w = w + jnp.where(pos < GLA_SUB - step, pltpu.roll(w, c - step, 0), 0.0)
        else:
            w = w + jnp.where(pos >= step, pltpu.roll(w, step, 0), 0.0)
        step *= 2
    return w


def _gla_chunk(q, k, v, la, state, mask, rev):
    c, s = GLA_CHUNK, GLA_SUB
    nsub = c // s
    w = _subchunk_scan(la, rev)
    tot = [w[s * i:s * i + 1] if rev else w[s * (i + 1) - 1:s * (i + 1)] for i in range(nsub)]
    anchors = [None] * nsub
    acc = jnp.zeros_like(tot[0])
    for i in (reversed(range(nsub)) if rev else range(nsub)):
        anchors[i] = acc
        acc = acc + tot[i]
    b_end = acc
    ref = jnp.concatenate([jnp.broadcast_to(a, (s, DK)) for a in anchors], axis=0)
    b = w + ref

    qt = q.astype(F32) * (DK ** -0.5) * jnp.exp(w)
    kf = k.astype(F32)
    inter = _dot((qt * jnp.exp(ref)).astype(BF16), state.astype(BF16))

    kstack = jnp.concatenate([kf[_key_rows(i, rev)] * jnp.exp(anchors[i] - b[_key_rows(i, rev)])
                              for i in range(nsub)], axis=0).astype(BF16)
    vstack = jnp.concatenate([v[_key_rows(i, rev)] for i in range(nsub)], axis=0)
    scores = jnp.where(mask != 0.0, _dot_nt(qt.astype(BF16), kstack), 0.0)
    out = inter + _dot(scores.astype(BF16), vstack)

    kst = kf * jnp.exp(b_end - b)
    xt = jnp.concatenate([kst, jnp.broadcast_to(jnp.exp(b_end), (c, DK))], axis=0).T
    new_state = xt[:, c:c + 1] * state + _dot(xt[:, :c].astype(BF16), v)
    return out, new_state


def _gla_body(qf_ref, kf_ref, vf_ref, lrf_ref, qb_ref, kb_ref, vb_ref, lrb_ref,
              wf_ref, bf_ref, wb_ref, bb_ref, mskf_ref, mskb_ref,
              s0f_ref, s0b_ref, of_ref, ob_ref, sf_ref, sb_ref, *, nchunks):
    @pl.when(pl.program_id(2) == 0)
    def _():
        sf_ref[...] = s0f_ref[...]
        sb_ref[...] = s0b_ref[...]

    c = GLA_CHUNK
    la_f = _log_gate(lrf_ref[...], wf_ref[...], bf_ref[...])
    la_b = _log_gate(lrb_ref[...], wb_ref[...], bb_ref[...])
    mskf, mskb = mskf_ref[...], mskb_ref[...]
    st_f = sf_ref[0, 0]
    st_b = sb_ref[0, 0]
    for n in range(nchunks):
        rf = slice(c * n, c * (n + 1))
        o, st_f = _gla_chunk(qf_ref[rf, :], kf_ref[rf, :], vf_ref[rf, :], la_f[rf, :], st_f, mskf, False)
        of_ref[rf, :] = o.astype(BF16)
        m = nchunks - 1 - n
        rb = slice(c * m, c * (m + 1))
        o, st_b = _gla_chunk(qb_ref[rb, :], kb_ref[rb, :], vb_ref[rb, :], la_b[rb, :], st_b, mskb, True)
        ob_ref[rb, :] = o.astype(BF16)
    sf_ref[0, 0] = st_f
    sb_ref[0, 0] = st_b


def _gla(p, lr, wlr_f, blr_f, wlr_b, blr_b, s0f, s0b, batch, seq, tb):
    nb = seq // tb
    msk = _gla_consts()
    mskf, mskb = jnp.asarray(msk[False]), jnp.asarray(msk[True])

    def fwd(col0, width):
        return lambda b, h, i: (b * nb + i, col0 // width + h)

    def bwd(col0, width):
        return lambda b, h, i: (b * nb + nb - 1 - i, col0 // width + h)

    const2 = lambda b, h, i: (0, 0)
    headcol = lambda b, h, i: (0, h)
    st_map = lambda b, h, i: (b, h, 0, 0)
    st_spec = pl.BlockSpec((1, 1, DK, DV), st_map)
    st_shape = jax.ShapeDtypeStruct((batch, HEADS, DK, DV), F32)
    o_shape = jax.ShapeDtypeStruct((batch * seq, HEADS * DV), BF16)
    return pl.pallas_call(
        functools.partial(_gla_body, nchunks=tb // GLA_CHUNK),
        grid=(batch, HEADS, nb),
        in_specs=[pl.BlockSpec((tb, DK), fwd(COL_GQ, DK)),
                  pl.BlockSpec((tb, DK), fwd(COL_GK, DK)),
                  pl.BlockSpec((tb, DV), fwd(COL_GV, DV)),
                  pl.BlockSpec((tb, LANES), lambda b, h, i: (b * nb + i, 0)),
                  pl.BlockSpec((tb, DK), bwd(COL_GQ, DK)),
                  pl.BlockSpec((tb, DK), bwd(COL_GK, DK)),
                  pl.BlockSpec((tb, DV), bwd(COL_GV, DV)),
                  pl.BlockSpec((tb, LANES), lambda b, h, i: (b * nb + nb - 1 - i, 0)),
                  pl.BlockSpec((LANES, DK), headcol),
                  pl.BlockSpec((1, DK), headcol),
                  pl.BlockSpec((LANES, DK), headcol),
                  pl.BlockSpec((1, DK), headcol),
                  pl.BlockSpec(mskf.shape, const2),
                  pl.BlockSpec(mskb.shape, const2),
                  st_spec, st_spec],
        out_specs=[pl.BlockSpec((tb, DV), lambda b, h, i: (b * nb + i, h)),
                   pl.BlockSpec((tb, DV), lambda b, h, i: (b * nb + nb - 1 - i, h)),
                   st_spec, st_spec],
        out_shape=[o_shape, o_shape, st_shape, st_shape],
        compiler_params=_cparams(("parallel", "parallel", "arbitrary")),
        name="gla",
    )(p, p, p, lr, p, p, p, lr, wlr_f, blr_f, wlr_b, blr_b, mskf, mskb, s0f, s0b)


def _ret_consts():
    c = RET_CHUNK
    hh = np.arange(HEADS, dtype=np.float64)
    lg = {False: np.log1p(-np.exp2(-5.0 - hh)), True: np.log1p(-np.exp2(-5.5 - hh))}
    t = np.arange(c, dtype=np.float64)[:, None]
    u = np.arange(c, dtype=np.float64)[None, :]
    pos = np.arange(c, dtype=np.float64)
    out = {}
    for rev in (False, True):
        g = lg[rev][:, None, None]
        if not rev:
            dmat = np.where(u <= t, np.exp((t - u) * g), 0.0)
            qd = np.exp((pos + 1.0)[None, :] * lg[rev][:, None])
            kd = np.exp((c - 1.0 - pos)[None, :] * lg[rev][:, None])
        else:
            dmat = np.where(u > t, np.exp((u - t) * g), 0.0)
            qd = np.exp((c - pos)[None, :] * lg[rev][:, None])
            kd = np.exp(pos[None, :] * lg[rev][:, None])
        cd = np.exp(c * lg[rev])
        out[rev] = (dmat.astype(np.float32),
                    np.repeat(qd[:, :, None], DV, axis=2).astype(np.float32),
                    np.repeat(kd[:, :, None], DK, axis=2).astype(np.float32),
                    np.repeat(cd[:, None, None], DV, axis=2).astype(np.float32))
    return out


def _rope(x, cos, sin):
    lane = lax.broadcasted_iota(jnp.int32, x.shape, 1)
    partner = jnp.where((lane % 64) < 32, pltpu.roll(x, DK - 32, 1), pltpu.roll(x, 32, 1))
    return x * cos + partner * sin


def _ret_chunk(q, k, v, cos, sin, state, dmat, qd, kd, cd):
    qr = _rope(q.astype(F32), cos, sin).astype(BF16)
    kr = _rope(k.astype(F32) * (DK ** -0.5), cos, sin)
    inter = _dot(qr, state.astype(BF16)) * qd
    scores = _dot_nt(qr, kr.astype(BF16)) * dmat
    out = inter + _dot(scores.astype(BF16), v)
    new_state = cd * state + _dot((kr * kd).T.astype(BF16), v)
    return out, new_state


def _ret_body(qf_ref, kf_ref, vf_ref, cosf_ref, sinf_ref, qb_ref, kb_ref, vb_ref, cosb_ref, sinb_ref,
              dmf_ref, qdf_ref, kdf_ref, cdf_ref, dmb_ref, qdb_ref, kdb_ref, cdb_ref,
              s0f_ref, s0b_ref, of_ref, ob_ref, sf_ref, sb_ref, *, nchunks):
    @pl.when(pl.program_id(2) == 0)
    def _():
        sf_ref[...] = s0f_ref[...]
        sb_ref[...] = s0b_ref[...]

    c = RET_CHUNK
    cf = (dmf_ref[0], qdf_ref[0], kdf_ref[0], cdf_ref[0])
    cb = (dmb_ref[0], qdb_ref[0], kdb_ref[0], cdb_ref[0])
    st_f = sf_ref[0, 0]
    st_b = sb_ref[0, 0]
    for n in range(nchunks):
        rf = slice(c * n, c * (n + 1))
        o, st_f = _ret_chunk(qf_ref[rf, :], kf_ref[rf, :], vf_ref[rf, :], cosf_ref[rf, :], sinf_ref[rf, :],
                             st_f, *cf)
        of_ref[rf, :] = o.astype(BF16)
        m = nchunks - 1 - n
        rb = slice(c * m, c * (m + 1))
        o, st_b = _ret_chunk(qb_ref[rb, :], kb_ref[rb, :], vb_ref[rb, :], cosb_ref[rb, :], sinb_ref[rb, :],
                             st_b, *cb)
        ob_ref[rb, :] = o.astype(BF16)
    sf_ref[0, 0] = st_f
    sb_ref[0, 0] = st_b


def _ret(p, cos, sin, s0f, s0b, batch, seq, tb):
    nb = seq // tb
    consts = _ret_consts()

    def fwd(col0, width):
        return lambda b, h, i: (b * nb + i, col0 // width + h)

    def bwd(col0, width):
        return lambda b, h, i: (b * nb + nb - 1 - i, col0 // width + h)

    head3 = lambda b, h, i: (h, 0, 0)
    st_spec = pl.BlockSpec((1, 1, DK, DV), lambda b, h, i: (b, h, 0, 0))
    st_shape = jax.ShapeDtypeStruct((batch, HEADS, DK, DV), F32)
    o_shape = jax.ShapeDtypeStruct((batch * seq, HEADS * DV), BF16)
    const_specs = []
    const_args = []
    for rev in (False, True):
        for a in consts[rev]:
            const_specs.append(pl.BlockSpec((1,) + a.shape[1:], head3))
            const_args.append(jnp.asarray(a))
    return pl.pallas_call(
        functools.partial(_ret_body, nchunks=tb // RET_CHUNK),
        grid=(batch, HEADS, nb),
        in_specs=[pl.BlockSpec((tb, DK), fwd(COL_RQ, DK)),
                  pl.BlockSpec((tb, DK), fwd(COL_RK, DK)),
                  pl.BlockSpec((tb, DV), fwd(COL_RV, DV)),
                  pl.BlockSpec((tb, DK), lambda b, h, i: (i, 0)),
                  pl.BlockSpec((tb, DK), lambda b, h, i: (i, 0)),
                  pl.BlockSpec((tb, DK), bwd(COL_RQ, DK)),
                  pl.BlockSpec((tb, DK), bwd(COL_RK, DK)),
                  pl.BlockSpec((tb, DV), bwd(COL_RV, DV)),
                  pl.BlockSpec((tb, DK), lambda b, h, i: (nb - 1 - i, 0)),
                  pl.BlockSpec((tb, DK), lambda b, h, i: (nb - 1 - i, 0))]
                 + const_specs + [st_spec, st_spec],
        out_specs=[pl.BlockSpec((tb, DV), lambda b, h, i: (b * nb + i, h)),
                   pl.BlockSpec((tb, DV), lambda b, h, i: (b * nb + nb - 1 - i, h)),
                   st_spec, st_spec],
        out_shape=[o_shape, o_shape, st_shape, st_shape],
        compiler_params=_cparams(("parallel", "parallel", "arbitrary")),
        name="ret",
    )(p, p, p, cos, sin, p, p, p, cos, sin, *const_args, s0f, s0b)


def _rope_tables(seq):
    n = DK // 4
    inv = ROPE_BASE ** (-jnp.arange(n, dtype=F32) / n)
    pos = jnp.arange(seq, dtype=jnp.int32)
    rows = (pos // GRID_W).astype(F32)
    cols = (pos % GRID_W).astype(F32)
    ar = rows[:, None] * inv[None, :]
    ac = cols[:, None] * inv[None, :]
    cos = jnp.concatenate([jnp.cos(ar), jnp.cos(ar), jnp.cos(ac), jnp.cos(ac)], axis=1)
    sin = jnp.concatenate([-jnp.sin(ar), jnp.sin(ar), -jnp.sin(ac), jnp.sin(ac)], axis=1)
    return cos, sin


def _route(logits_t):
    g = [logits_t[i:i + 1] for i in range(N_GROUPS)]
    gmax = jnp.maximum(jnp.maximum(g[0], g[1]), jnp.maximum(g[2], g[3]))
    gsel = jnp.where(g[0] == gmax, 0, jnp.where(g[1] == gmax, 1, jnp.where(g[2] == gmax, 2, 3)))
    gsum = (jnp.exp(g[0] - gmax) + jnp.exp(g[1] - gmax)) + (jnp.exp(g[2] - gmax) + jnp.exp(g[3] - gmax))
    gw = 1.0 / gsum
    e = [logits_t[EXP_ROW0 + EXPERTS_PER_GROUP * i:EXP_ROW0 + EXPERTS_PER_GROUP * (i + 1)]
         for i in range(N_GROUPS)]
    el = jnp.where(gsel == 0, e[0], jnp.where(gsel == 1, e[1], jnp.where(gsel == 2, e[2], e[3])))
    row = lax.broadcasted_iota(jnp.int32, el.shape, 0).astype(F32)
    none = float(EXPERTS_PER_GROUP)
    m1 = jnp.max(el, axis=0, keepdims=True)
    i1 = jnp.min(jnp.where(el == m1, row, none), axis=0, keepdims=True)
    el2 = jnp.where(row == i1, -jnp.inf, el)
    m2 = jnp.max(el2, axis=0, keepdims=True)
    i2 = jnp.min(jnp.where(el2 == m2, row, none), axis=0, keepdims=True)
    r = jnp.exp(m2 - m1)
    w1 = gw / (1.0 + r)
    w2 = gw * r / (1.0 + r)
    base = gsel * EXPERTS_PER_GROUP
    ids = jnp.concatenate([base + i1.astype(jnp.int32), base + i2.astype(jnp.int32)], axis=0)
    return ids, jnp.concatenate([w1, w2], axis=0)


def _store_token_tiled(ref, val):
    m = val.shape[0]
    for j in range(FEAT_TILES):
        ref[pl.ds(j, m, stride=FEAT_TILES), :] = val[:, LANES * j:LANES * (j + 1)]


def _load_token_tiled(ref, m):
    return jnp.concatenate([ref[pl.ds(j, m, stride=FEAT_TILES), :] for j in range(FEAT_TILES)], axis=1)


def _merge_body(gf_ref, gb_ref, rf_ref, rb_ref, gg_ref, rg_ref, mg_ref, mr_ref, x_ref,
                g1_ref, sh2_ref, sc2_ref, gn_ref, nf_ref, wg_ref, wr_ref, wo_ref, wrt_ref, brt_ref, upper_ref,
                h_ref, hn_ref, ids_ref, ew_ref, cnt_ref):
    @pl.when(pl.program_id(0) == 0)
    def _():
        cnt_ref[...] = jnp.zeros_like(cnt_ref)

    og = gf_ref[...].astype(F32) + gb_ref[...].astype(F32)
    orr = rf_ref[...].astype(F32) + rb_ref[...].astype(F32)
    gparts, rparts = [], []
    for hh in range(HEADS):
        seg = og[:, DV * hh:DV * (hh + 1)]
        ms = jnp.mean(seg * seg, axis=-1, keepdims=True)
        gparts.append(seg * lax.rsqrt(ms + NORM_EPS))
        seg = orr[:, DV * hh:DV * (hh + 1)]
        mu = jnp.mean(seg, axis=-1, keepdims=True)
        cen = seg - mu
        var = jnp.mean(cen * cen, axis=-1, keepdims=True)
        rparts.append(cen * lax.rsqrt(var + NORM_EPS))
    o_gla = jnp.concatenate(gparts, axis=1) * gn_ref[...] * _silu(gg_ref[...].astype(F32))
    o_ret = jnp.concatenate(rparts, axis=1) * _silu(rg_ref[...].astype(F32))
    y = (_sigmoid(mg_ref[...].astype(F32)) * _dot(o_gla.astype(BF16), wg_ref[...])
         + _sigmoid(mr_ref[...].astype(F32)) * _dot(o_ret.astype(BF16), wr_ref[...]))
    h = x_ref[...] + g1_ref[0] * _dot(y.astype(BF16), wo_ref[...])
    h_ref[...] = h
    ms = jnp.mean(h * h, axis=-1, keepdims=True)
    hn = h * lax.rsqrt(ms + NORM_EPS) * nf_ref[...] * (1.0 + sc2_ref[0]) + sh2_ref[0]
    _store_token_tiled(hn_ref, hn)
    hn_hi = hn.astype(BF16)
    hn_lo = (hn - hn_hi.astype(F32)).astype(BF16)
    both_w = _dot(hn_hi, wrt_ref[...])
    logits = (both_w[:, :ROUTER_W] + both_w[:, ROUTER_W:] + _dot(hn_lo, wrt_ref[:, :ROUTER_W])) + brt_ref[...]
    ids, ew = _route(logits.T)
    tm = ids.shape[1]

    erow = lax.broadcasted_iota(jnp.int32, (N_EXPERTS, tm), 0)
    oh0 = jnp.where(erow == ids[0:1], 1.0, 0.0)
    oh1 = jnp.where(erow == ids[1:2], 1.0, 0.0)
    both = oh0 + oh1
    before = _dot(both.astype(BF16), upper_ref[...]) + cnt_ref[:, 0:1].astype(F32)
    rank0 = jnp.sum(oh0 * before, axis=0, keepdims=True)
    rank1 = jnp.sum(oh1 * before, axis=0, keepdims=True)
    total = cnt_ref[:, 0:1] + jnp.sum(both, axis=1, keepdims=True).astype(jnp.int32)
    cnt_ref[...] = jnp.broadcast_to(total, cnt_ref.shape)
    ids_ref[...] = jnp.concatenate([ids, rank0.astype(jnp.int32), rank1.astype(jnp.int32),
                                    jnp.zeros((4, tm), jnp.int32)], axis=0)
    ew_ref[...] = jnp.concatenate([ew, jnp.zeros((6, tm), F32)], axis=0)


def _merge(gf, gb, rf, rb, p, x2, g1, sh2, sc2, gn, nf, wg, wr, wo, wrt, brt, seq, tm):
    t = x2.shape[0]
    tiles_per_batch = seq // tm
    row = lambda i: (i, 0)
    mod = lambda i: (i // tiles_per_batch, 0, 0)
    const = lambda i: (0, 0)
    tok = pl.BlockSpec((tm, D_MODEL), row)
    vec = pl.BlockSpec((1, D_MODEL), const)
    modspec = pl.BlockSpec((1, 1, D_MODEL), mod)
    wspec = pl.BlockSpec((D_MODEL, D_MODEL), const)
    upper = jnp.asarray(np.triu(np.ones((tm, tm), np.float32), 1), BF16)

    def pcol(col0):
        return pl.BlockSpec((tm, D_MODEL), lambda i: (i, col0 // D_MODEL))

    return pl.pallas_call(
        _merge_body,
        grid=(t // tm,),
        in_specs=[tok, tok, tok, tok, pcol(COL_GG), pcol(COL_RG), pcol(COL_MG), pcol(COL_MR), tok,
                  modspec, modspec, modspec, vec, vec, wspec, wspec, wspec,
                  pl.BlockSpec((D_MODEL, 2 * ROUTER_W), const), pl.BlockSpec((1, ROUTER_W), const),
                  pl.BlockSpec((tm, tm), const)],
        out_specs=[tok, pl.BlockSpec((tm * FEAT_TILES, LANES), row),
                   pl.BlockSpec((8, tm), lambda i: (0, i)), pl.BlockSpec((8, tm), lambda i: (0, i)),
                   pl.BlockSpec((N_EXPERTS, LANES), const)],
        out_shape=[jax.ShapeDtypeStruct((t, D_MODEL), F32), jax.ShapeDtypeStruct((t * FEAT_TILES, LANES), F32),
                   jax.ShapeDtypeStruct((8, t), jnp.int32), jax.ShapeDtypeStruct((8, t), F32),
                   jax.ShapeDtypeStruct((N_EXPERTS, LANES), jnp.int32)],
        compiler_params=_cparams(("arbitrary",)),
        name="merge",
    )(gf, gb, rf, rb, p, p, p, p, x2, g1, sh2, sc2, gn, nf, wg, wr, wo, wrt, brt, upper)


GATHER_UNROLL = 8


def _token_copy(src_hbm, dst, sem, src_tok, dst_tok):
    return pltpu.make_async_copy(
        src_hbm.at[pl.ds(pl.multiple_of(src_tok * FEAT_TILES, FEAT_TILES), FEAT_TILES), :],
        dst.at[pl.ds(pl.multiple_of(dst_tok * FEAT_TILES, FEAT_TILES), FEAT_TILES), :], sem)


def _wait_tokens(src_hbm, dst, sem):
    pltpu.make_async_copy(src_hbm.at[pl.ds(0, dst.shape[0]), :], dst, sem).wait()


def _expert_body(be_ref, src0_ref, stok_ref, nused_ref, hn_hbm, wg_ref, wu_ref, wd_ref, y_ref, xbuf, sems):
    i = pl.program_id(0)
    nused = nused_ref[0]
    slot = i % 2
    last = stok_ref.shape[0] - 1

    def issue(block, s):
        base = src0_ref[block]

        def body(g, carry):
            for u in range(GATHER_UNROLL):
                r = g * GATHER_UNROLL + u
                tok = stok_ref[jnp.minimum(base + r, last)]
                _token_copy(hn_hbm, xbuf.at[s], sems.at[s], tok, r).start()
            return carry
        lax.fori_loop(0, MOE_BLOCK // GATHER_UNROLL, body, 0)

    @pl.when(jnp.logical_and(i == 0, nused > 0))
    def _():
        issue(0, 0)

    @pl.when(i + 1 < nused)
    def _():
        issue(i + 1, 1 - slot)

    @pl.when(i < nused)
    def _():
        _wait_tokens(hn_hbm, xbuf.at[slot], sems.at[slot])
        xb = _load_token_tiled(xbuf.at[slot], MOE_BLOCK).astype(BF16)
        gate = _dot(xb, wg_ref[0].astype(BF16))
        up = _dot(xb, wu_ref[0].astype(BF16))
        hid = (_silu(gate) * up).astype(BF16)
        _store_token_tiled(y_ref, _dot(hid, wd_ref[0].astype(BF16)))

    @pl.when(i >= nused)
    def _():
        y_ref[...] = jnp.zeros_like(y_ref)


def _experts(block_e, src0, stok, nused, hn, w_gate, w_up, w_down, nblk):
    wmap = lambda i, be, s0, st, nu: (be[i], 0, 0)
    rows = MOE_BLOCK * FEAT_TILES
    return pl.pallas_call(
        _expert_body,
        grid_spec=pltpu.PrefetchScalarGridSpec(
            num_scalar_prefetch=4,
            grid=(nblk,),
            in_specs=[pl.BlockSpec(memory_space=pl.ANY),
                      pl.BlockSpec((1, D_MODEL, EXPERT_FF), wmap),
                      pl.BlockSpec((1, D_MODEL, EXPERT_FF), wmap),
                      pl.BlockSpec((1, EXPERT_FF, D_MODEL), wmap)],
            out_specs=pl.BlockSpec((rows, LANES), lambda i, be, s0, st, nu: (i, 0)),
            scratch_shapes=[pltpu.VMEM((2, rows, LANES), F32),
                            pltpu.SemaphoreType.DMA((2,))]),
        out_shape=jax.ShapeDtypeStruct((nblk * rows, LANES), F32),
        compiler_params=_cparams(("arbitrary",)),
        name="experts",
    )(block_e, src0, stok, nused, hn, w_gate, w_up, w_down)


COMBINE_TM = 128


def _combine_body(pos_ref, y_hbm, h_ref, ew_ref, g2_ref, nf_ref, o_ref, ybuf, sems):
    i = pl.program_id(0)
    n = pl.num_programs(0)
    slot = i % 2
    tm = COMBINE_TM

    def issue(tile, s):
        def body(g, carry):
            for u in range(GATHER_UNROLL // TOP_K):
                r = g * (GATHER_UNROLL // TOP_K) + u
                a = (tile * tm + r) * TOP_K
                for k in range(TOP_K):
                    _token_copy(y_hbm, ybuf.at[s, k], sems.at[s], pos_ref[a + k], r).start()
            return carry
        lax.fori_loop(0, tm * TOP_K // GATHER_UNROLL, body, 0)

    @pl.when(i == 0)
    def _():
        issue(0, 0)

    @pl.when(i + 1 < n)
    def _():
        issue(i + 1, 1 - slot)

    for k in range(TOP_K):
        _wait_tokens(y_hbm, ybuf.at[slot, k], sems.at[slot])

    wt = jnp.concatenate([ew_ref[...]] * (LANES // 8), axis=0).T
    moe = (wt[:, 0:1] * _load_token_tiled(ybuf.at[slot, 0], tm)
           + wt[:, 1:2] * _load_token_tiled(ybuf.at[slot, 1], tm))
    h = h_ref[...] + g2_ref[0] * moe
    ms = jnp.mean(h * h, axis=-1, keepdims=True)
    o_ref[...] = h * lax.rsqrt(ms + NORM_EPS) * nf_ref[...]


def _combine(pos, y_pad, h, ew, g2, nf, seq):
    t = h.shape[0]
    tm = COMBINE_TM
    tiles_per_batch = seq // tm
    return pl.pallas_call(
        _combine_body,
        grid_spec=pltpu.PrefetchScalarGridSpec(
            num_scalar_prefetch=1,
            grid=(t // tm,),
            in_specs=[pl.BlockSpec(memory_space=pl.ANY),
                      pl.BlockSpec((tm, D_MODEL), lambda i, pos: (i, 0)),
                      pl.BlockSpec((8, tm), lambda i, pos: (0, i)),
                      pl.BlockSpec((1, 1, D_MODEL), lambda i, pos: (i // tiles_per_batch, 0, 0)),
                      pl.BlockSpec((1, D_MODEL), lambda i, pos: (0, 0))],
            out_specs=pl.BlockSpec((tm, D_MODEL), lambda i, pos: (i, 0)),
            scratch_shapes=[pltpu.VMEM((2, TOP_K, tm * FEAT_TILES, LANES), F32),
                            pltpu.SemaphoreType.DMA((2,))]),
        out_shape=jax.ShapeDtypeStruct((t, D_MODEL), F32),
        compiler_params=_cparams(("arbitrary",)),
        name="combine",
    )(pos, y_pad, h, ew, g2, nf)


def _dispatch_indices(ids, ranks, counts, t):
    a = t * TOP_K
    nblk = a // MOE_BLOCK + N_EXPERTS
    experts = jnp.arange(N_EXPERTS, dtype=jnp.int32)
    starts = jnp.cumsum(counts) - counts
    padded = (counts + MOE_BLOCK - 1) // MOE_BLOCK * MOE_BLOCK
    pends = jnp.cumsum(padded)
    pstarts = pends - padded
    block_start = jnp.arange(nblk, dtype=jnp.int32) * MOE_BLOCK
    block_e = jnp.minimum(jnp.sum((block_start[:, None] >= pends[None, :]).astype(jnp.int32), axis=1),
                          N_EXPERTS - 1)
    of_block = (block_e[:, None] == experts[None, :]).astype(jnp.int32)
    src0 = block_start + jnp.sum(of_block * (starts - pstarts)[None, :], axis=1)
    nused = pends[-1:] // MOE_BLOCK
    flat_e = ids.T.reshape(a)
    stok = (jnp.argsort(flat_e, stable=True) // TOP_K).astype(jnp.int32)
    of_assign = (ids[:, :, None] == experts[None, None, :]).astype(jnp.int32)
    pos = (ranks + jnp.sum(of_assign * pstarts[None, None, :], axis=2)).T.reshape(a)
    return block_e.astype(jnp.int32), src0.astype(jnp.int32), stok, nused.astype(jnp.int32), pos.astype(jnp.int32), nblk


def kernel(x, c, ctx, c_ctx, w_ada, b_ada, norm_mix, norm_ffn, w_in, gla_lr_w, gla_lr_b, gla_norm,
           w_branch_gla, w_branch_ret, w_out, w_router_group, b_router_group, w_router_expert,
           b_router_expert, w_expert_gate, w_expert_up, w_expert_down, norm_final):
    batch, seq, d = x.shape
    ctx_len = ctx.shape[1]
    assert d == D_MODEL and w_ada.shape[0] == 1, "single-layer block with D_MODEL features"
    t = batch * seq

    c8 = jnp.zeros((8, d), F32).at[:batch].set(c).at[batch].set(c_ctx)
    mod = _ada(c8, w_ada[0], b_ada[0][None, :])
    sh1, sc1, g1, sh2, sc2, g2 = [mod[:, d * i:d * (i + 1)] for i in range(6)]
    lat = lambda m: m[:batch, None, :]
    cx = lambda m: m[batch:batch + 1, None, :]

    w = w_in[0]
    lr0 = COL_GG + HEADS * DV
    w_main = jnp.concatenate([w[:, :lr0], w[:, lr0 + 2 * GLA_RANK:]], axis=1).astype(BF16)
    w_lr = jnp.pad(w[:, lr0:lr0 + 2 * GLA_RANK], ((0, 0), (0, LANES - 2 * GLA_RANK))).astype(BF16)
    nm = norm_mix[0][None, :]

    p_ctx, lr_ctx = _proj(ctx.reshape(batch * ctx_len, d), cx(sh1), cx(sc1), nm, w_main, w_lr,
                          tm=ctx_len, rows_per_mod=batch * ctx_len)
    x2 = x.reshape(t, d)
    p_lat, lr_lat = _proj(x2, lat(sh1), lat(sc1), nm, w_main, w_lr, tm=1024, rows_per_mod=seq)

    wlr_f = jnp.zeros((LANES, HEADS * DK), F32).at[:GLA_RANK].set(gla_lr_w[0, 0])
    wlr_b = jnp.zeros((LANES, HEADS * DK), F32).at[GLA_RANK:2 * GLA_RANK].set(gla_lr_w[0, 1])
    blr_f, blr_b = gla_lr_b[0, 0][None, :], gla_lr_b[0, 1][None, :]
    zero_state = jnp.zeros((batch, HEADS, DK, DV), F32)

    _, _, gs_f, gs_b = _gla(p_ctx, lr_ctx, wlr_f, blr_f, wlr_b, blr_b, zero_state, zero_state,
                            batch, ctx_len, tb=ctx_len)
    gla_f, gla_b, _, _ = _gla(p_lat, lr_lat, wlr_f, blr_f, wlr_b, blr_b, gs_f, gs_b, batch, seq, tb=512)

    ones = jnp.ones((ctx_len, DK), F32)
    _, _, rs_f, rs_b = _ret(p_ctx, ones, jnp.zeros_like(ones), zero_state, zero_state, batch, ctx_len, tb=ctx_len)
    cos, sin = _rope_tables(seq)
    ret_f, ret_b, _, _ = _ret(p_lat, cos, sin, rs_f, rs_b, batch, seq, tb=512)

    wrt = jnp.zeros((d, ROUTER_W), F32)
    wrt = wrt.at[:, :N_GROUPS].set(w_router_group[0]).at[:, EXP_ROW0:EXP_ROW0 + N_EXPERTS].set(w_router_expert[0])
    wrt_hi = wrt.astype(BF16)
    wrt = jnp.concatenate([wrt_hi, (wrt - wrt_hi.astype(F32)).astype(BF16)], axis=1)
    brt = jnp.zeros((1, ROUTER_W), F32)
    brt = brt.at[0, :N_GROUPS].set(b_router_group[0]).at[0, EXP_ROW0:EXP_ROW0 + N_EXPERTS].set(b_router_expert[0])
    h, hn, ids8, ew8, cnt = _merge(gla_f, gla_b, ret_f, ret_b, p_lat, x2, lat(g1), lat(sh2), lat(sc2),
                                   jnp.tile(gla_norm[0], HEADS)[None, :], norm_ffn[0][None, :],
                                   w_branch_gla[0].astype(BF16), w_branch_ret[0].astype(BF16),
                                   w_out[0].astype(BF16), wrt, brt, seq, tm=256)

    block_e, src0, stok, nused, pos, nblk = _dispatch_indices(ids8[:TOP_K], ids8[TOP_K:2 * TOP_K], cnt[:, 0], t)
    y_pad = _experts(block_e, src0, stok, nused, hn, w_expert_gate[0], w_expert_up[0], w_expert_down[0], nblk)
    out = _combine(pos, y_pad, h, ew8, lat(g2), norm_final[None, :], seq)
    return out.reshape(batch, seq, d)
```

```python
import functools

import numpy as np
import jax
import jax.numpy as jnp
from jax import lax
from jax.experimental import pallas as pl
from jax.experimental.pallas import tpu as pltpu

F32 = jnp.float32
BF16 = jnp.bfloat16

D_MODEL = 1024
GRID_W = 64
HEADS = 4
DK = 128
DV = 256
GLA_RANK = 16
GLA_TAU = 16.0
GLA_CHUNK = 64
GLA_SUB = 16
RET_CHUNK = 128
ROPE_BASE = 10000.0
N_GROUPS = 4
EXPERTS_PER_GROUP = 8
N_EXPERTS = N_GROUPS * EXPERTS_PER_GROUP
TOP_K = 2
EXPERT_FF = 256
MOE_BLOCK = 128
NORM_EPS = 1e-6

COL_GQ, COL_GK, COL_GV, COL_GG = 0, 512, 1024, 2048
COL_RQ, COL_RK, COL_RV, COL_RG = 3072, 3584, 4096, 5120
COL_MG, COL_MR = 6144, 7168
PROJ_W = 8192
LANES = 128
FEAT_TILES = D_MODEL // LANES
ROUTER_W = 128
EXP_ROW0 = 8

VMEM_LIMIT = 56 * 1024 * 1024


def _cparams(sem):
    return pltpu.CompilerParams(dimension_semantics=sem, vmem_limit_bytes=VMEM_LIMIT)


def _sigmoid(x):
    return 1.0 / (1.0 + jnp.exp(-x))


def _silu(x):
    return x * _sigmoid(x)


def _dot(a, b):
    return jnp.dot(a, b, preferred_element_type=F32)


def _dot_nt(a, b):
    return lax.dot_general(a, b, (((1,), (1,)), ((), ())), preferred_element_type=F32)


def _ada_body(c_ref, w_ref, b_ref, o_ref):
    s = _silu(c_ref[...])
    o_ref[...] = _dot(s.astype(BF16), w_ref[...].astype(BF16)) + b_ref[...]


def _ada(c8, w, b):
    n = w.shape[1]
    tn = 1536
    return pl.pallas_call(
        _ada_body,
        grid=(n // tn,),
        in_specs=[pl.BlockSpec((8, D_MODEL), lambda j: (0, 0)),
                  pl.BlockSpec((D_MODEL, tn), lambda j: (0, j)),
                  pl.BlockSpec((1, tn), lambda j: (0, j))],
        out_specs=pl.BlockSpec((8, tn), lambda j: (0, j)),
        out_shape=jax.ShapeDtypeStruct((8, n), F32),
        compiler_params=_cparams(("arbitrary",)),
        name="ada",
    )(c8, w, b)


def _proj_body(x_ref, sh_ref, sc_ref, g_ref, w_ref, wlr_ref, o_ref, lr_ref, h_ref):
    @pl.when(pl.program_id(1) == 0)
    def _():
        x = x_ref[...]
        ms = jnp.mean(x * x, axis=-1, keepdims=True)
        y = x * lax.rsqrt(ms + NORM_EPS) * g_ref[...]
        hb = (y * (1.0 + sc_ref[0]) + sh_ref[0]).astype(BF16)
        h_ref[...] = hb
        lr_ref[...] = _dot(hb, wlr_ref[...])

    o_ref[...] = _dot(h_ref[...], w_ref[...]).astype(BF16)


def _proj(x2, sh, sc, g, w_main, w_lr, tm, rows_per_mod):
    t = x2.shape[0]
    tn = 2048
    tiles_per_mod = rows_per_mod // tm
    mod_map = lambda i, j: (i // tiles_per_mod, 0, 0)
    return pl.pallas_call(
        _proj_body,
        grid=(t // tm, PROJ_W // tn),
        in_specs=[pl.BlockSpec((tm, D_MODEL), lambda i, j: (i, 0)),
                  pl.BlockSpec((1, 1, D_MODEL), mod_map),
                  pl.BlockSpec((1, 1, D_MODEL), mod_map),
                  pl.BlockSpec((1, D_MODEL), lambda i, j: (0, 0)),
                  pl.BlockSpec((D_MODEL, tn), lambda i, j: (0, j)),
                  pl.BlockSpec((D_MODEL, LANES), lambda i, j: (0, 0))],
        out_specs=[pl.BlockSpec((tm, tn), lambda i, j: (i, j)),
                   pl.BlockSpec((tm, LANES), lambda i, j: (i, 0))],
        out_shape=[jax.ShapeDtypeStruct((t, PROJ_W), BF16),
                   jax.ShapeDtypeStruct((t, LANES), F32)],
        scratch_shapes=[pltpu.VMEM((tm, D_MODEL), BF16)],
        compiler_params=_cparams(("parallel", "arbitrary")),
        name="proj",
    )(x2, sh, sc, g, w_main, w_lr)


def _gla_consts():
    c, s = GLA_CHUNK, GLA_SUB
    msk = {}
    for rev in (False, True):
        cols = []
        for i in range(c // s):
            keys = np.arange(c)[_key_rows(i, rev)][None, :]
            t = np.arange(c)[:, None]
            visible = (keys > t) if rev else (keys <= t)
            cols.append(np.where(t // s == i, visible, False))
        msk[rev] = np.concatenate(cols, axis=1).astype(np.float32)
    return msk


def _key_rows(i, rev):
    return slice(GLA_SUB * i, GLA_CHUNK) if rev else slice(0, GLA_SUB * (i + 1))


def _log_gate(lr, w, b):
    z = _dot(lr.astype(BF16), w.astype(BF16)) + b
    return -(jnp.maximum(-z, 0.0) + jnp.log(1.0 + jnp.exp(-jnp.abs(z)))) * (1.0 / GLA_TAU)


def _subchunk_scan(la, rev):
    c = la.shape[0]
    pos = lax.broadcasted_iota(jnp.int32, la.shape, 0) % GLA_SUB
    w = la
    step = 1
    while step < GLA_SUB:
        if rev:
            w = w + jnp.where(pos < GLA_SUB - step, pltpu.roll(w, c - step, 0), 0.0)
        else:
            w = w + jnp.where(pos >= step, pltpu.roll(w, step, 0), 0.0)
        step *= 2
    return w


def _gla_chunk(q, k, v, la, state, mask, rev):
    c, s = GLA_CHUNK, GLA_SUB
    nsub = c // s
    w = _subchunk_scan(la, rev)
    tot = [w[s * i:s * i + 1] if rev else w[s * (i + 1) - 1:s * (i + 1)] for i in range(nsub)]
    anchors = [None] * nsub
    acc = jnp.zeros_like(tot[0])
    for i in (reversed(range(nsub)) if rev else range(nsub)):
        anchors[i] = acc
        acc = acc + tot[i]
    b_end = acc
    ref = jnp.concatenate([jnp.broadcast_to(a, (s, DK)) for a in anchors], axis=0)
    b = w + ref

    qt = q.astype(F32) * (DK ** -0.5) * jnp.exp(w)
    kf = k.astype(F32)
    inter = _dot((qt * jnp.exp(ref)).astype(BF16), state.astype(BF16))

    kstack = jnp.concatenate([kf[_key_rows(i, rev)] * jnp.exp(anchors[i] - b[_key_rows(i, rev)])
                              for i in range(nsub)], axis=0).astype(BF16)
    vstack = jnp.concatenate([v[_key_rows(i, rev)] for i in range(nsub)], axis=0)
    scores = jnp.where(mask != 0.0, _dot_nt(qt.astype(BF16), kstack), 0.0)
    out = inter + _dot(scores.astype(BF16), vstack)

    kst = kf * jnp.exp(b_end - b)
    xt = jnp.concatenate([kst, jnp.broadcast_to(jnp.exp(b_end), (c, DK))], axis=0).T
    new_state = xt[:, c:c + 1] * state + _dot(xt[:, :c].astype(BF16), v)
    return out, new_state


def _gla_body(qf_ref, kf_ref, vf_ref, lrf_ref, qb_ref, kb_ref, vb_ref, lrb_ref,
              wf_ref, bf_ref, wb_ref, bb_ref, mskf_ref, mskb_ref,
              s0f_ref, s0b_ref, of_ref, ob_ref, sf_ref, sb_ref, *, nchunks):
    @pl.when(pl.program_id(2) == 0)
    def _():
        sf_ref[...] = s0f_ref[...]
        sb_ref[...] = s0b_ref[...]

    c = GLA_CHUNK
    la_f = _log_gate(lrf_ref[...], wf_ref[...], bf_ref[...])
    la_b = _log_gate(lrb_ref[...], wb_ref[...], bb_ref[...])
    mskf, mskb = mskf_ref[...], mskb_ref[...]
    st_f = sf_ref[0, 0]
    st_b = sb_ref[0, 0]
    for n in range(nchunks):
        rf = slice(c * n, c * (n + 1))
        o, st_f = _gla_chunk(qf_ref[rf, :], kf_ref[rf, :], vf_ref[rf, :], la_f[rf, :], st_f, mskf, False)
        of_ref[rf, :] = o.astype(BF16)
        m = nchunks - 1 - n
        rb = slice(c * m, c * (m + 1))
        o, st_b = _gla_chunk(qb_ref[rb, :], kb_ref[rb, :], vb_ref[rb, :], la_b[rb, :], st_b, mskb, True)
        ob_ref[rb, :] = o.astype(BF16)
    sf_ref[0, 0] = st_f
    sb_ref[0, 0] = st_b


def _gla(p, lr, wlr_f, blr_f, wlr_b, blr_b, s0f, s0b, batch, seq, tb):
    nb = seq // tb
    msk = _gla_consts()
    mskf, mskb = jnp.asarray(msk[False]), jnp.asarray(msk[True])

    def fwd(col0, width):
        return lambda b, h, i: (b * nb + i, col0 // width + h)

    def bwd(col0, width):
        return lambda b, h, i: (b * nb + nb - 1 - i, col0 // width + h)

    const2 = lambda b, h, i: (0, 0)
    headcol = lambda b, h, i: (0, h)
    st_map = lambda b, h, i: (b, h, 0, 0)
    st_spec = pl.BlockSpec((1, 1, DK, DV), st_map)
    st_shape = jax.ShapeDtypeStruct((batch, HEADS, DK, DV), F32)
    o_shape = jax.ShapeDtypeStruct((batch * seq, HEADS * DV), BF16)
    return pl.pallas_call(
        functools.partial(_gla_body, nchunks=tb // GLA_CHUNK),
        grid=(batch, HEADS, nb),
        in_specs=[pl.BlockSpec((tb, DK), fwd(COL_GQ, DK)),
                  pl.BlockSpec((tb, DK), fwd(COL_GK, DK)),
                  pl.BlockSpec((tb, DV), fwd(COL_GV, DV)),
                  pl.BlockSpec((tb, LANES), lambda b, h, i: (b * nb + i, 0)),
                  pl.BlockSpec((tb, DK), bwd(COL_GQ, DK)),
                  pl.BlockSpec((tb, DK), bwd(COL_GK, DK)),
                  pl.BlockSpec((tb, DV), bwd(COL_GV, DV)),
                  pl.BlockSpec((tb, LANES), lambda b, h, i: (b * nb + nb - 1 - i, 0)),
                  pl.BlockSpec((LANES, DK), headcol),
                  pl.BlockSpec((1, DK), headcol),
                  pl.BlockSpec((LANES, DK), headcol),
                  pl.BlockSpec((1, DK), headcol),
                  pl.BlockSpec(mskf.shape, const2),
                  pl.BlockSpec(mskb.shape, const2),
                  st_spec, st_spec],
        out_specs=[pl.BlockSpec((tb, DV), lambda b, h, i: (b * nb + i, h)),
                   pl.BlockSpec((tb, DV), lambda b, h, i: (b * nb + nb - 1 - i, h)),
                   st_spec, st_spec],
        out_shape=[o_shape, o_shape, st_shape, st_shape],
        compiler_params=_cparams(("parallel", "parallel", "arbitrary")),
        name="gla",
    )(p, p, p, lr, p, p, p, lr, wlr_f, blr_f, wlr_b, blr_b, mskf, mskb, s0f, s0b)


def _ret_consts():
    c = RET_CHUNK
    hh = np.arange(HEADS, dtype=np.float64)
    lg = {False: np.log1p(-np.exp2(-5.0 - hh)), True: np.log1p(-np.exp2(-5.5 - hh))}
    t = np.arange(c, dtype=np.float64)[:, None]
    u = np.arange(c, dtype=np.float64)[None, :]
    pos = np.arange(c, dtype=np.float64)
    out = {}
    for rev in (False, True):
        g = lg[rev][:, None, None]
        if not rev:
            dmat = np.where(u <= t, np.exp((t - u) * g), 0.0)
            qd = np.exp((pos + 1.0)[None, :] * lg[rev][:, None])
            kd = np.exp((c - 1.0 - pos)[None, :] * lg[rev][:, None])
        else:
            dmat = np.where(u > t, np.exp((u - t) * g), 0.0)
            qd = np.exp((c - pos)[None, :] * lg[rev][:, None])
            kd = np.exp(pos[None, :] * lg[rev][:, None])
        cd = np.exp(c * lg[rev])
        out[rev] = (dmat.astype(np.float32),
                    np.repeat(qd[:, :, None], DV, axis=2).astype(np.float32),
                    np.repeat(kd[:, :, None], DK, axis=2).astype(np.float32),
                    np.repeat(cd[:, None, None], DV, axis=2).astype(np.float32))
    return out


def _rope(x, cos, sin):
    lane = lax.broadcasted_iota(jnp.int32, x.shape, 1)
    partner = jnp.where((lane % 64) < 32, pltpu.roll(x, DK - 32, 1), pltpu.roll(x, 32, 1))
    return x * cos + partner * sin


def _ret_chunk(q, k, v, cos, sin, state, dmat, qd, kd, cd):
    qr = _rope(q.astype(F32), cos, sin).astype(BF16)
    kr = _rope(k.astype(F32) * (DK ** -0.5), cos, sin)
    inter = _dot(qr, state.astype(BF16)) * qd
    scores = _dot_nt(qr, kr.astype(BF16)) * dmat
    out = inter + _dot(scores.astype(BF16), v)
    new_state = cd * state + _dot((kr * kd).T.astype(BF16), v)
    return out, new_state


def _ret_body(qf_ref, kf_ref, vf_ref, cosf_ref, sinf_ref, qb_ref, kb_ref, vb_ref, cosb_ref, sinb_ref,
              dmf_ref, qdf_ref, kdf_ref, cdf_ref, dmb_ref, qdb_ref, kdb_ref, cdb_ref,
              s0f_ref, s0b_ref, of_ref, ob_ref, sf_ref, sb_ref, *, nchunks):
    @pl.when(pl.program_id(2) == 0)
    def _():
        sf_ref[...] = s0f_ref[...]
        sb_ref[...] = s0b_ref[...]

    c = RET_CHUNK
    cf = (dmf_ref[0], qdf_ref[0], kdf_ref[0], cdf_ref[0])
    cb = (dmb_ref[0], qdb_ref[0], kdb_ref[0], cdb_ref[0])
    st_f = sf_ref[0, 0]
    st_b = sb_ref[0, 0]
    for n in range(nchunks):
        rf = slice(c * n, c * (n + 1))
        o, st_f = _ret_chunk(qf_ref[rf, :], kf_ref[rf, :], vf_ref[rf, :], cosf_ref[rf, :], sinf_ref[rf, :],
                             st_f, *cf)
        of_ref[rf, :] = o.astype(BF16)
        m = nchunks - 1 - n
        rb = slice(c * m, c * (m + 1))
        o, st_b = _ret_chunk(qb_ref[rb, :], kb_ref[rb, :], vb_ref[rb, :], cosb_ref[rb, :], sinb_ref[rb, :],
                             st_b, *cb)
        ob_ref[rb, :] = o.astype(BF16)
    sf_ref[0, 0] = st_f
    sb_ref[0, 0] = st_b


def _ret(p, cos, sin, s0f, s0b, batch, seq, tb):
    nb = seq // tb
    consts = _ret_consts()

    def fwd(col0, width):
        return lambda b, h, i: (b * nb + i, col0 // width + h)

    def bwd(col0, width):
        return lambda b, h, i: (b * nb + nb - 1 - i, col0 // width + h)

    head3 = lambda b, h, i: (h, 0, 0)
    st_spec = pl.BlockSpec((1, 1, DK, DV), lambda b, h, i: (b, h, 0, 0))
    st_shape = jax.ShapeDtypeStruct((batch, HEADS, DK, DV), F32)
    o_shape = jax.ShapeDtypeStruct((batch * seq, HEADS * DV), BF16)
    const_specs = []
    const_args = []
    for rev in (False, True):
        for a in consts[rev]:
            const_specs.append(pl.BlockSpec((1,) + a.shape[1:], head3))
            const_args.append(jnp.asarray(a))
    return pl.pallas_call(
        functools.partial(_ret_body, nchunks=tb // RET_CHUNK),
        grid=(batch, HEADS, nb),
        in_specs=[pl.BlockSpec((tb, DK), fwd(COL_RQ, DK)),
                  pl.BlockSpec((tb, DK), fwd(COL_RK, DK)),
                  pl.BlockSpec((tb, DV), fwd(COL_RV, DV)),
                  pl.BlockSpec((tb, DK), lambda b, h, i: (i, 0)),
                  pl.BlockSpec((tb, DK), lambda b, h, i: (i, 0)),
                  pl.BlockSpec((tb, DK), bwd(COL_RQ, DK)),
                  pl.BlockSpec((tb, DK), bwd(COL_RK, DK)),
                  pl.BlockSpec((tb, DV), bwd(COL_RV, DV)),
                  pl.BlockSpec((tb, DK), lambda b, h, i: (nb - 1 - i, 0)),
                  pl.BlockSpec((tb, DK), lambda b, h, i: (nb - 1 - i, 0))]
                 + const_specs + [st_spec, st_spec],
        out_specs=[pl.BlockSpec((tb, DV), lambda b, h, i: (b * nb + i, h)),
                   pl.BlockSpec((tb, DV), lambda b, h, i: (b * nb + nb - 1 - i, h)),
                   st_spec, st_spec],
        out_shape=[o_shape, o_shape, st_shape, st_shape],
        compiler_params=_cparams(("parallel", "parallel", "arbitrary")),
        name="ret",
    )(p, p, p, cos, sin, p, p, p, cos, sin, *const_args, s0f, s0b)


def _rope_tables(seq):
    n = DK // 4
    inv = ROPE_BASE ** (-jnp.arange(n, dtype=F32) / n)
    pos = jnp.arange(seq, dtype=jnp.int32)
    rows = (pos // GRID_W).astype(F32)
    cols = (pos % GRID_W).astype(F32)
    ar = rows[:, None] * inv[None, :]
    ac = cols[:, None] * inv[None, :]
    cos = jnp.concatenate([jnp.cos(ar), jnp.cos(ar), jnp.cos(ac), jnp.cos(ac)], axis=1)
    sin = jnp.concatenate([-jnp.sin(ar), jnp.sin(ar), -jnp.sin(ac), jnp.sin(ac)], axis=1)
    return cos, sin


def _route(logits_t):
    g = [logits_t[i:i + 1] for i in range(N_GROUPS)]
    gmax = jnp.maximum(jnp.maximum(g[0], g[1]), jnp.maximum(g[2], g[3]))
    gsel = jnp.where(g[0] == gmax, 0, jnp.where(g[1] == gmax, 1, jnp.where(g[2] == gmax, 2, 3)))
    gsum = (jnp.exp(g[0] - gmax) + jnp.exp(g[1] - gmax)) + (jnp.exp(g[2] - gmax) + jnp.exp(g[3] - gmax))
    gw = 1.0 / gsum
    e = [logits_t[EXP_ROW0 + EXPERTS_PER_GROUP * i:EXP_ROW0 + EXPERTS_PER_GROUP * (i + 1)]
         for i in range(N_GROUPS)]
    el = jnp.where(gsel == 0, e[0], jnp.where(gsel == 1, e[1], jnp.where(gsel == 2, e[2], e[3])))
    row = lax.broadcasted_iota(jnp.int32, el.shape, 0).astype(F32)
    none = float(EXPERTS_PER_GROUP)
    m1 = jnp.max(el, axis=0, keepdims=True)
    i1 = jnp.min(jnp.where(el == m1, row, none), axis=0, keepdims=True)
    el2 = jnp.where(row == i1, -jnp.inf, el)
    m2 = jnp.max(el2, axis=0, keepdims=True)
    i2 = jnp.min(jnp.where(el2 == m2, row, none), axis=0, keepdims=True)
    r = jnp.exp(m2 - m1)
    w1 = gw / (1.0 + r)
    w2 = gw * r / (1.0 + r)
    base = gsel * EXPERTS_PER_GROUP
    ids = jnp.concatenate([base + i1.astype(jnp.int32), base + i2.astype(jnp.int32)], axis=0)
    return ids, jnp.concatenate([w1, w2], axis=0)


def _store_token_tiled(ref, val):
    m = val.shape[0]
    for j in range(FEAT_TILES):
        ref[pl.ds(j, m, stride=FEAT_TILES), :] = val[:, LANES * j:LANES * (j + 1)]


def _load_token_tiled(ref, m):
    return jnp.concatenate([ref[pl.ds(j, m, stride=FEAT_TILES), :] for j in range(FEAT_TILES)], axis=1)


def _merge_body(gf_ref, gb_ref, rf_ref, rb_ref, gg_ref, rg_ref, mg_ref, mr_ref, x_ref,
                g1_ref, sh2_ref, sc2_ref, gn_ref, nf_ref, wg_ref, wr_ref, wo_ref, wrt_ref, brt_ref, upper_ref,
                h_ref, hn_ref, ids_ref, ew_ref, cnt_ref):
    @pl.when(pl.program_id(0) == 0)
    def _():
        cnt_ref[...] = jnp.zeros_like(cnt_ref)

    og = gf_ref[...].astype(F32) + gb_ref[...].astype(F32)
    orr = rf_ref[...].astype(F32) + rb_ref[...].astype(F32)
    gparts, rparts = [], []
    for hh in range(HEADS):
        seg = og[:, DV * hh:DV * (hh + 1)]
        ms = jnp.mean(seg * seg, axis=-1, keepdims=True)
        gparts.append(seg * lax.rsqrt(ms + NORM_EPS))
        seg = orr[:, DV * hh:DV * (hh + 1)]
        mu = jnp.mean(seg, axis=-1, keepdims=True)
        cen = seg - mu
        var = jnp.mean(cen * cen, axis=-1, keepdims=True)
        rparts.append(cen * lax.rsqrt(var + NORM_EPS))
    o_gla = jnp.concatenate(gparts, axis=1) * gn_ref[...] * _silu(gg_ref[...].astype(F32))
    o_ret = jnp.concatenate(rparts, axis=1) * _silu(rg_ref[...].astype(F32))
    y = (_sigmoid(mg_ref[...].astype(F32)) * _dot(o_gla.astype(BF16), wg_ref[...])
         + _sigmoid(mr_ref[...].astype(F32)) * _dot(o_ret.astype(BF16), wr_ref[...]))
    h = x_ref[...] + g1_ref[0] * _dot(y.astype(BF16), wo_ref[...])
    h_ref[...] = h
    ms = jnp.mean(h * h, axis=-1, keepdims=True)
    hn = h * lax.rsqrt(ms + NORM_EPS) * nf_ref[...] * (1.0 + sc2_ref[0]) + sh2_ref[0]
    _store_token_tiled(hn_ref, hn)
    hn_hi = hn.astype(BF16)
    hn_lo = (hn - hn_hi.astype(F32)).astype(BF16)
    both_w = _dot(hn_hi, wrt_ref[...])
    logits = (both_w[:, :ROUTER_W] + both_w[:, ROUTER_W:] + _dot(hn_lo, wrt_ref[:, :ROUTER_W])) + brt_ref[...]
    ids, ew = _route(logits.T)
    tm = ids.shape[1]

    erow = lax.broadcasted_iota(jnp.int32, (N_EXPERTS, tm), 0)
    oh0 = jnp.where(erow == ids[0:1], 1.0, 0.0)
    oh1 = jnp.where(erow == ids[1:2], 1.0, 0.0)
    both = oh0 + oh1
    before = _dot(both.astype(BF16), upper_ref[...]) + cnt_ref[:, 0:1].astype(F32)
    rank0 = jnp.sum(oh0 * before, axis=0, keepdims=True)
    rank1 = jnp.sum(oh1 * before, axis=0, keepdims=True)
    total = cnt_ref[:, 0:1] + jnp.sum(both, axis=1, keepdims=True).astype(jnp.int32)
    cnt_ref[...] = jnp.broadcast_to(total, cnt_ref.shape)
    ids_ref[...] = jnp.concatenate([ids, rank0.astype(jnp.int32), rank1.astype(jnp.int32),
                                    jnp.zeros((4, tm), jnp.int32)], axis=0)
    ew_ref[...] = jnp.concatenate([ew, jnp.zeros((6, tm), F32)], axis=0)


def _merge(gf, gb, rf, rb, p, x2, g1, sh2, sc2, gn, nf, wg, wr, wo, wrt, brt, seq, tm):
    t = x2.shape[0]
    tiles_per_batch = seq // tm
    row = lambda i: (i, 0)
    mod = lambda i: (i // tiles_per_batch, 0, 0)
    const = lambda i: (0, 0)
    tok = pl.BlockSpec((tm, D_MODEL), row)
    vec = pl.BlockSpec((1, D_MODEL), const)
    modspec = pl.BlockSpec((1, 1, D_MODEL), mod)
    wspec = pl.BlockSpec((D_MODEL, D_MODEL), const)
    upper = jnp.asarray(np.triu(np.ones((tm, tm), np.float32), 1), BF16)

    def pcol(col0):
        return pl.BlockSpec((tm, D_MODEL), lambda i: (i, col0 // D_MODEL))

    return pl.pallas_call(
        _merge_body,
        grid=(t // tm,),
        in_specs=[tok, tok, tok, tok, pcol(COL_GG), pcol(COL_RG), pcol(COL_MG), pcol(COL_MR), tok,
                  modspec, modspec, modspec, vec, vec, wspec, wspec, wspec,
                  pl.BlockSpec((D_MODEL, 2 * ROUTER_W), const), pl.BlockSpec((1, ROUTER_W), const),
                  pl.BlockSpec((tm, tm), const)],
        out_specs=[tok, pl.BlockSpec((tm * FEAT_TILES, LANES), row),
                   pl.BlockSpec((8, tm), lambda i: (0, i)), pl.BlockSpec((8, tm), lambda i: (0, i)),
                   pl.BlockSpec((N_EXPERTS, LANES), const)],
        out_shape=[jax.ShapeDtypeStruct((t, D_MODEL), F32), jax.ShapeDtypeStruct((t * FEAT_TILES, LANES), F32),
                   jax.ShapeDtypeStruct((8, t), jnp.int32), jax.ShapeDtypeStruct((8, t), F32),
                   jax.ShapeDtypeStruct((N_EXPERTS, LANES), jnp.int32)],
        compiler_params=_cparams(("arbitrary",)),
        name="merge",
    )(gf, gb, rf, rb, p, p, p, p, x2, g1, sh2, sc2, gn, nf, wg, wr, wo, wrt, brt, upper)


GATHER_UNROLL = 8


def _aligned(tok):
    off = tok * FEAT_TILES
    return off if isinstance(off, int) else pl.multiple_of(off, FEAT_TILES)


def _token_copy(src_hbm, dst, sem, src_tok, dst_tok):
    return pltpu.make_async_copy(src_hbm.at[pl.ds(_aligned(src_tok), FEAT_TILES), :],
                                 dst.at[pl.ds(_aligned(dst_tok), FEAT_TILES), :], sem)


def _wait_tokens(src_hbm, dst, sem):
    pltpu.make_async_copy(src_hbm.at[pl.ds(0, dst.shape[0]), :], dst, sem).wait()


def _issue_rows(n, start_row, unrolled):
    if unrolled:
        for r in range(n):
            start_row(r, r % 2)
    else:
        def body(g, carry):
            for u in range(GATHER_UNROLL):
                start_row(g * GATHER_UNROLL + u, u % 2)
            return carry
        lax.fori_loop(0, n // GATHER_UNROLL, body, 0)


def _expert_body(be_ref, src0_ref, stok_ref, nused_ref, hn_hbm, wg_ref, wu_ref, wd_ref, y_ref,
                 xbuf0, xbuf1, sems):
    i = pl.program_id(0)
    nused = nused_ref[0]
    bufs = (xbuf0, xbuf1)
    last = stok_ref.shape[0] - 1

    def issue(block, s, unrolled):
        base = src0_ref[block]

        def start_row(r, priority):
            tok = stok_ref[jnp.minimum(base + r, last)]
            _token_copy(hn_hbm, bufs[s], sems.at[s], tok, r).start(priority=priority)
        _issue_rows(MOE_BLOCK, start_row, unrolled)

    @pl.when(i == 0)
    def _():
        issue(0, 0, False)

    for cur in range(2):
        @pl.when(jnp.logical_and(i % 2 == cur, i < nused))
        def _():
            _wait_tokens(hn_hbm, bufs[cur], sems.at[cur])
            issue(i + 1, 1 - cur, True)
            xb = _load_token_tiled(bufs[cur], MOE_BLOCK).astype(BF16)
            gate = _dot(xb, wg_ref[0].astype(BF16))
            up = _dot(xb, wu_ref[0].astype(BF16))
            hid = (_silu(gate) * up).astype(BF16)
            _store_token_tiled(y_ref, _dot(hid, wd_ref[0].astype(BF16)))

        @pl.when(jnp.logical_and(i % 2 == cur, i == nused))
        def _():
            _wait_tokens(hn_hbm, bufs[cur], sems.at[cur])

    @pl.when(i >= nused)
    def _():
        y_ref[...] = jnp.zeros_like(y_ref)


def _experts(block_e, src0, stok, nused, hn, w_gate, w_up, w_down, nblk):
    wmap = lambda i, be, s0, st, nu: (be[i], 0, 0)
    rows = MOE_BLOCK * FEAT_TILES
    return pl.pallas_call(
        _expert_body,
        grid_spec=pltpu.PrefetchScalarGridSpec(
            num_scalar_prefetch=4,
            grid=(nblk,),
            in_specs=[pl.BlockSpec(memory_space=pl.ANY),
                      pl.BlockSpec((1, D_MODEL, EXPERT_FF), wmap),
                      pl.BlockSpec((1, D_MODEL, EXPERT_FF), wmap),
                      pl.BlockSpec((1, EXPERT_FF, D_MODEL), wmap)],
            out_specs=pl.BlockSpec((rows, LANES), lambda i, be, s0, st, nu: (i, 0)),
            scratch_shapes=[pltpu.VMEM((rows, LANES), F32), pltpu.VMEM((rows, LANES), F32),
                            pltpu.SemaphoreType.DMA((2,))]),
        out_shape=jax.ShapeDtypeStruct((nblk * rows, LANES), F32),
        compiler_params=_cparams(("arbitrary",)),
        name="experts",
    )(block_e, src0, stok, nused, hn, w_gate, w_up, w_down)


COMBINE_TM = 128


def _combine_body(pos_ref, y_hbm, h_ref, ew_ref, g2_ref, nf_ref, o_ref, ybuf0, ybuf1, sems):
    i = pl.program_id(0)
    n = pl.num_programs(0)
    tm = COMBINE_TM
    t = pos_ref.shape[0] // TOP_K
    bufs = (ybuf0, ybuf1)

    def issue(tile, s, unrolled):
        def start_row(j, priority):
            k, r = j % TOP_K, j // TOP_K
            _token_copy(y_hbm, bufs[s].at[k], sems.at[s], pos_ref[k * t + tile * tm + r], r).start(priority=priority)
        _issue_rows(tm * TOP_K, start_row, unrolled)

    def wait(s):
        for k in range(TOP_K):
            _wait_tokens(y_hbm, bufs[s].at[k], sems.at[s])

    @pl.when(i == 0)
    def _():
        issue(0, 0, False)

    for cur in range(2):
        @pl.when(i % 2 == cur)
        def _():
            wait(cur)
            issue(jnp.minimum(i + 1, n - 1), 1 - cur, True)
            wt = jnp.concatenate([ew_ref[...]] * (LANES // 8), axis=0).T
            moe = (wt[:, 0:1] * _load_token_tiled(bufs[cur].at[0], tm)
                   + wt[:, 1:2] * _load_token_tiled(bufs[cur].at[1], tm))
            h = h_ref[...] + g2_ref[0] * moe
            ms = jnp.mean(h * h, axis=-1, keepdims=True)
            o_ref[...] = h * lax.rsqrt(ms + NORM_EPS) * nf_ref[...]

        @pl.when(jnp.logical_and(i % 2 == cur, i == n - 1))
        def _():
            wait(1 - cur)


def _combine(pos, y_pad, h, ew, g2, nf, seq):
    t = h.shape[0]
    tm = COMBINE_TM
    tiles_per_batch = seq // tm
    ybuf = pltpu.VMEM((TOP_K, tm * FEAT_TILES, LANES), F32)
    return pl.pallas_call(
        _combine_body,
        grid_spec=pltpu.PrefetchScalarGridSpec(
            num_scalar_prefetch=1,
            grid=(t // tm,),
            in_specs=[pl.BlockSpec(memory_space=pl.ANY),
                      pl.BlockSpec((tm, D_MODEL), lambda i, pos: (i, 0)),
                      pl.BlockSpec((8, tm), lambda i, pos: (0, i)),
                      pl.BlockSpec((1, 1, D_MODEL), lambda i, pos: (i // tiles_per_batch, 0, 0)),
                      pl.BlockSpec((1, D_MODEL), lambda i, pos: (0, 0))],
            out_specs=pl.BlockSpec((tm, D_MODEL), lambda i, pos: (i, 0)),
            scratch_shapes=[ybuf, ybuf, pltpu.SemaphoreType.DMA((2,))]),
        out_shape=jax.ShapeDtypeStruct((t, D_MODEL), F32),
        compiler_params=_cparams(("arbitrary",)),
        name="combine",
    )(pos, y_pad, h, ew, g2, nf)


def _dispatch_indices(ids, ranks, counts, t):
    a = t * TOP_K
    nblk = a // MOE_BLOCK + N_EXPERTS + 1
    experts = jnp.arange(N_EXPERTS, dtype=jnp.int32)
    starts = jnp.cumsum(counts) - counts
    padded = (counts + MOE_BLOCK - 1) // MOE_BLOCK * MOE_BLOCK
    pends = jnp.cumsum(padded)
    pstarts = pends - padded
    block_start = jnp.arange(nblk, dtype=jnp.int32) * MOE_BLOCK
    block_e = jnp.minimum(jnp.sum((block_start[:, None] >= pends[None, :]).astype(jnp.int32), axis=1),
                          N_EXPERTS - 1)
    of_block = (block_e[:, None] == experts[None, :]).astype(jnp.int32)
    src0 = block_start + jnp.sum(of_block * (starts - pstarts)[None, :], axis=1)
    nused = pends[-1:] // MOE_BLOCK
    tok = jnp.arange(t, dtype=jnp.int32)[None, :]
    slot = jnp.arange(TOP_K, dtype=jnp.int32)[:, None]
    key = ids * a + tok * TOP_K + slot
    stok = (jnp.sort(key.reshape(a)) % a) // TOP_K
    pstart_of = jnp.zeros_like(ids)
    for e in range(N_EXPERTS):
        pstart_of = jnp.where(ids == e, pstarts[e], pstart_of)
    pos = (ranks + pstart_of).reshape(a)
    i32 = lambda v: v.astype(jnp.int32)
    return i32(block_e), i32(src0), i32(stok), i32(nused), i32(pos), nblk


def kernel(x, c, ctx, c_ctx, w_ada, b_ada, norm_mix, norm_ffn, w_in, gla_lr_w, gla_lr_b, gla_norm,
           w_branch_gla, w_branch_ret, w_out, w_router_group, b_router_group, w_router_expert,
           b_router_expert, w_expert_gate, w_expert_up, w_expert_down, norm_final):
    batch, seq, d = x.shape
    ctx_len = ctx.shape[1]
    assert d == D_MODEL and w_ada.shape[0] == 1, "single-layer block with D_MODEL features"
    t = batch * seq

    c8 = jnp.zeros((8, d), F32).at[:batch].set(c).at[batch].set(c_ctx)
    mod = _ada(c8, w_ada[0], b_ada[0][None, :])
    sh1, sc1, g1, sh2, sc2, g2 = [mod[:, d * i:d * (i + 1)] for i in range(6)]
    lat = lambda m: m[:batch, None, :]
    cx = lambda m: m[batch:batch + 1, None, :]

    w = w_in[0]
    lr0 = COL_GG + HEADS * DV
    w_main = jnp.concatenate([w[:, :lr0], w[:, lr0 + 2 * GLA_RANK:]], axis=1).astype(BF16)
    w_lr = jnp.pad(w[:, lr0:lr0 + 2 * GLA_RANK], ((0, 0), (0, LANES - 2 * GLA_RANK))).astype(BF16)
    nm = norm_mix[0][None, :]

    p_ctx, lr_ctx = _proj(ctx.reshape(batch * ctx_len, d), cx(sh1), cx(sc1), nm, w_main, w_lr,
                          tm=ctx_len, rows_per_mod=batch * ctx_len)
    x2 = x.reshape(t, d)
    p_lat, lr_lat = _proj(x2, lat(sh1), lat(sc1), nm, w_main, w_lr, tm=1024, rows_per_mod=seq)

    wlr_f = jnp.zeros((LANES, HEADS * DK), F32).at[:GLA_RANK].set(gla_lr_w[0, 0])
    wlr_b = jnp.zeros((LANES, HEADS * DK), F32).at[GLA_RANK:2 * GLA_RANK].set(gla_lr_w[0, 1])
    blr_f, blr_b = gla_lr_b[0, 0][None, :], gla_lr_b[0, 1][None, :]
    zero_state = jnp.zeros((batch, HEADS, DK, DV), F32)

    _, _, gs_f, gs_b = _gla(p_ctx, lr_ctx, wlr_f, blr_f, wlr_b, blr_b, zero_state, zero_state,
                            batch, ctx_len, tb=ctx_len)
    gla_f, gla_b, _, _ = _gla(p_lat, lr_lat, wlr_f, blr_f, wlr_b, blr_b, gs_f, gs_b, batch, seq, tb=512)

    ones = jnp.ones((ctx_len, DK), F32)
    _, _, rs_f, rs_b = _ret(p_ctx, ones, jnp.zeros_like(ones), zero_state, zero_state, batch, ctx_len, tb=ctx_len)
    cos, sin = _rope_tables(seq)
    ret_f, ret_b, _, _ = _ret(p_lat, cos, sin, rs_f, rs_b, batch, seq, tb=512)

    wrt = jnp.zeros((d, ROUTER_W), F32)
    wrt = wrt.at[:, :N_GROUPS].set(w_router_group[0]).at[:, EXP_ROW0:EXP_ROW0 + N_EXPERTS].set(w_router_expert[0])
    wrt_hi = wrt.astype(BF16)
    wrt = jnp.concatenate([wrt_hi, (wrt - wrt_hi.astype(F32)).astype(BF16)], axis=1)
    brt = jnp.zeros((1, ROUTER_W), F32)
    brt = brt.at[0, :N_GROUPS].set(b_router_group[0]).at[0, EXP_ROW0:EXP_ROW0 + N_EXPERTS].set(b_router_expert[0])
    h, hn, ids8, ew8, cnt = _merge(gla_f, gla_b, ret_f, ret_b, p_lat, x2, lat(g1), lat(sh2), lat(sc2),
                                   jnp.tile(gla_norm[0], HEADS)[None, :], norm_ffn[0][None, :],
                                   w_branch_gla[0].astype(BF16), w_branch_ret[0].astype(BF16),
                                   w_out[0].astype(BF16), wrt, brt, seq, tm=256)

    block_e, src0, stok, nused, pos, nblk = _dispatch_indices(ids8[:TOP_K], ids8[TOP_K:2 * TOP_K], cnt[:, 0], t)
    y_pad = _experts(block_e, src0, stok, nused, hn, w_expert_gate[0], w_expert_up[0], w_expert_down[0], nblk)
    out = _combine(pos, y_pad, h, ew8, lat(g2), norm_final[None, :], seq)
    return out.reshape(batch, seq, d)
```

```python
import functools

import numpy as np
import jax
import jax.numpy as jnp
from jax import lax
from jax.experimental import pallas as pl
from jax.experimental.pallas import tpu as pltpu

F32 = jnp.float32
BF16 = jnp.bfloat16

D_MODEL = 1024
GRID_W = 64
HEADS = 4
DK = 128
DV = 256
GLA_RANK = 16
GLA_TAU = 16.0
GLA_CHUNK = 64
GLA_SUB = 16
RET_CHUNK = 128
ROPE_BASE = 10000.0
N_GROUPS = 4
EXPERTS_PER_GROUP = 8
N_EXPERTS = N_GROUPS * EXPERTS_PER_GROUP
TOP_K = 2
EXPERT_FF = 256
MOE_BLOCK = 128
NORM_EPS = 1e-6

COL_GQ, COL_GK, COL_GV, COL_GG = 0, 512, 1024, 2048
COL_RQ, COL_RK, COL_RV, COL_RG = 3072, 3584, 4096, 5120
COL_MG, COL_MR = 6144, 7168
PROJ_W = 8192
LANES = 128
FEAT_TILES = D_MODEL // LANES
ROUTER_W = 128
EXP_ROW0 = 8

VMEM_LIMIT = 56 * 1024 * 1024


def _cparams(sem):
    return pltpu.CompilerParams(dimension_semantics=sem, vmem_limit_bytes=VMEM_LIMIT)


def _sigmoid(x):
    return 1.0 / (1.0 + jnp.exp(-x))


def _silu(x):
    return x * _sigmoid(x)


def _dot(a, b):
    return jnp.dot(a, b, preferred_element_type=F32)


def _dot_nt(a, b):
    return lax.dot_general(a, b, (((1,), (1,)), ((), ())), preferred_element_type=F32)


def _ada_body(c_ref, w_ref, b_ref, o_ref):
    s = _silu(c_ref[...])
    o_ref[...] = _dot(s.astype(BF16), w_ref[...].astype(BF16)) + b_ref[...]


def _ada(c8, w, b):
    n = w.shape[1]
    tn = 1536
    return pl.pallas_call(
        _ada_body,
        grid=(n // tn,),
        in_specs=[pl.BlockSpec((8, D_MODEL), lambda j: (0, 0)),
                  pl.BlockSpec((D_MODEL, tn), lambda j: (0, j)),
                  pl.BlockSpec((1, tn), lambda j: (0, j))],
        out_specs=pl.BlockSpec((8, tn), lambda j: (0, j)),
        out_shape=jax.ShapeDtypeStruct((8, n), F32),
        compiler_params=_cparams(("arbitrary",)),
        name="ada",
    )(c8, w, b)


def _proj_body(x_ref, sh_ref, sc_ref, g_ref, w_ref, wlr_ref, o_ref, lr_ref, h_ref):
    @pl.when(pl.program_id(1) == 0)
    def _():
        x = x_ref[...]
        ms = jnp.mean(x * x, axis=-1, keepdims=True)
        y = x * lax.rsqrt(ms + NORM_EPS) * g_ref[...]
        hb = (y * (1.0 + sc_ref[0]) + sh_ref[0]).astype(BF16)
        h_ref[...] = hb
        lr_ref[...] = _dot(hb, wlr_ref[...])

    o_ref[...] = _dot(h_ref[...], w_ref[...]).astype(BF16)


def _proj(x2, sh, sc, g, w_main, w_lr, tm, rows_per_mod):
    t = x2.shape[0]
    tn = 2048
    tiles_per_mod = rows_per_mod // tm
    mod_map = lambda i, j: (i // tiles_per_mod, 0, 0)
    return pl.pallas_call(
        _proj_body,
        grid=(t // tm, PROJ_W // tn),
        in_specs=[pl.BlockSpec((tm, D_MODEL), lambda i, j: (i, 0)),
                  pl.BlockSpec((1, 1, D_MODEL), mod_map),
                  pl.BlockSpec((1, 1, D_MODEL), mod_map),
                  pl.BlockSpec((1, D_MODEL), lambda i, j: (0, 0)),
                  pl.BlockSpec((D_MODEL, tn), lambda i, j: (0, j)),
                  pl.BlockSpec((D_MODEL, LANES), lambda i, j: (0, 0))],
        out_specs=[pl.BlockSpec((tm, tn), lambda i, j: (i, j)),
                   pl.BlockSpec((tm, LANES), lambda i, j: (i, 0))],
        out_shape=[jax.ShapeDtypeStruct((t, PROJ_W), BF16),
                   jax.ShapeDtypeStruct((t, LANES), F32)],
        scratch_shapes=[pltpu.VMEM((tm, D_MODEL), BF16)],
        compiler_params=_cparams(("parallel", "arbitrary")),
        name="proj",
    )(x2, sh, sc, g, w_main, w_lr)


def _gla_consts():
    c, s = GLA_CHUNK, GLA_SUB
    msk = {}
    for rev in (False, True):
        cols = []
        for i in range(c // s):
            keys = np.arange(c)[_key_rows(i, rev)][None, :]
            t = np.arange(c)[:, None]
            visible = (keys > t) if rev else (keys <= t)
            cols.append(np.where(t // s == i, visible, False))
        msk[rev] = np.concatenate(cols, axis=1).astype(np.float32)
    return msk


def _key_rows(i, rev):
    return slice(GLA_SUB * i, GLA_CHUNK) if rev else slice(0, GLA_SUB * (i + 1))


def _log_gate(lr, w, b):
    z = _dot(lr.astype(BF16), w.astype(BF16)) + b
    return -(jnp.maximum(-z, 0.0) + jnp.log(1.0 + jnp.exp(-jnp.abs(z)))) * (1.0 / GLA_TAU)


def _subchunk_scan(la, rev):
    c = la.shape[0]
    pos = lax.broadcasted_iota(jnp.int32, la.shape, 0) % GLA_SUB
    w = la
    step = 1
    while step < GLA_SUB:
        if rev:
            w = w + jnp.where(pos < GLA_SUB - step, pltpu.roll(w, c - step, 0), 0.0)
        else:
            w = w + jnp.where(pos >= step, pltpu.roll(w, step, 0), 0.0)
        step *= 2
    return w


def _gla_chunk(q, k, v, la, state, mask, rev):
    c, s = GLA_CHUNK, GLA_SUB
    nsub = c // s
    w = _subchunk_scan(la, rev)
    tot = [w[s * i:s * i + 1] if rev else w[s * (i + 1) - 1:s * (i + 1)] for i in range(nsub)]
    anchors = [None] * nsub
    acc = jnp.zeros_like(tot[0])
    for i in (reversed(range(nsub)) if rev else range(nsub)):
        anchors[i] = acc
        acc = acc + tot[i]
    b_end = acc
    ref = jnp.concatenate([jnp.broadcast_to(a, (s, DK)) for a in anchors], axis=0)
    b = w + ref

    qt = q.astype(F32) * (DK ** -0.5) * jnp.exp(w)
    kf = k.astype(F32)
    inter = _dot((qt * jnp.exp(ref)).astype(BF16), state.astype(BF16))

    kstack = jnp.concatenate([kf[_key_rows(i, rev)] * jnp.exp(anchors[i] - b[_key_rows(i, rev)])
                              for i in range(nsub)], axis=0).astype(BF16)
    vstack = jnp.concatenate([v[_key_rows(i, rev)] for i in range(nsub)], axis=0)
    scores = jnp.where(mask != 0.0, _dot_nt(qt.astype(BF16), kstack), 0.0)
    out = inter + _dot(scores.astype(BF16), vstack)

    kst = kf * jnp.exp(b_end - b)
    xt = jnp.concatenate([kst, jnp.broadcast_to(jnp.exp(b_end), (c, DK))], axis=0).T
    new_state = xt[:, c:c + 1] * state + _dot(xt[:, :c].astype(BF16), v)
    return out, new_state


def _gla_body(qf_ref, kf_ref, vf_ref, lrf_ref, qb_ref, kb_ref, vb_ref, lrb_ref,
              wf_ref, bf_ref, wb_ref, bb_ref, mskf_ref, mskb_ref,
              s0f_ref, s0b_ref, of_ref, ob_ref, sf_ref, sb_ref, *, nchunks):
    @pl.when(pl.program_id(2) == 0)
    def _():
        sf_ref[...] = s0f_ref[...]
        sb_ref[...] = s0b_ref[...]

    c = GLA_CHUNK
    la_f = _log_gate(lrf_ref[...], wf_ref[...], bf_ref[...])
    la_b = _log_gate(lrb_ref[...], wb_ref[...], bb_ref[...])
    mskf, mskb = mskf_ref[...], mskb_ref[...]
    st_f = sf_ref[0, 0]
    st_b = sb_ref[0, 0]
    for n in range(nchunks):
        rf = slice(c * n, c * (n + 1))
        o, st_f = _gla_chunk(qf_ref[rf, :], kf_ref[rf, :], vf_ref[rf, :], la_f[rf, :], st_f, mskf, False)
        of_ref[rf, :] = o.astype(BF16)
        m = nchunks - 1 - n
        rb = slice(c * m, c * (m + 1))
        o, st_b = _gla_chunk(qb_ref[rb, :], kb_ref[rb, :], vb_ref[rb, :], la_b[rb, :], st_b, mskb, True)
        ob_ref[rb, :] = o.astype(BF16)
    sf_ref[0, 0] = st_f
    sb_ref[0, 0] = st_b


def _gla(p, lr, wlr_f, blr_f, wlr_b, blr_b, s0f, s0b, batch, seq, tb):
    nb = seq // tb
    msk = _gla_consts()
    mskf, mskb = jnp.asarray(msk[False]), jnp.asarray(msk[True])

    def fwd(col0, width):
        return lambda b, h, i: (b * nb + i, col0 // width + h)

    def bwd(col0, width):
        return lambda b, h, i: (b * nb + nb - 1 - i, col0 // width + h)

    const2 = lambda b, h, i: (0, 0)
    headcol = lambda b, h, i: (0, h)
    st_map = lambda b, h, i: (b, h, 0, 0)
    st_spec = pl.BlockSpec((1, 1, DK, DV), st_map)
    st_shape = jax.ShapeDtypeStruct((batch, HEADS, DK, DV), F32)
    o_shape = jax.ShapeDtypeStruct((batch * seq, HEADS * DV), BF16)
    return pl.pallas_call(
        functools.partial(_gla_body, nchunks=tb // GLA_CHUNK),
        grid=(batch, HEADS, nb),
        in_specs=[pl.BlockSpec((tb, DK), fwd(COL_GQ, DK)),
                  pl.BlockSpec((tb, DK), fwd(COL_GK, DK)),
                  pl.BlockSpec((tb, DV), fwd(COL_GV, DV)),
                  pl.BlockSpec((tb, LANES), lambda b, h, i: (b * nb + i, 0)),
                  pl.BlockSpec((tb, DK), bwd(COL_GQ, DK)),
                  pl.BlockSpec((tb, DK), bwd(COL_GK, DK)),
                  pl.BlockSpec((tb, DV), bwd(COL_GV, DV)),
                  pl.BlockSpec((tb, LANES), lambda b, h, i: (b * nb + nb - 1 - i, 0)),
                  pl.BlockSpec((LANES, DK), headcol),
                  pl.BlockSpec((1, DK), headcol),
                  pl.BlockSpec((LANES, DK), headcol),
                  pl.BlockSpec((1, DK), headcol),
                  pl.BlockSpec(mskf.shape, const2),
                  pl.BlockSpec(mskb.shape, const2),
                  st_spec, st_spec],
        out_specs=[pl.BlockSpec((tb, DV), lambda b, h, i: (b * nb + i, h)),
                   pl.BlockSpec((tb, DV), lambda b, h, i: (b * nb + nb - 1 - i, h)),
                   st_spec, st_spec],
        out_shape=[o_shape, o_shape, st_shape, st_shape],
        compiler_params=_cparams(("parallel", "parallel", "arbitrary")),
        name="gla",
    )(p, p, p, lr, p, p, p, lr, wlr_f, blr_f, wlr_b, blr_b, mskf, mskb, s0f, s0b)


def _ret_consts():
    c = RET_CHUNK
    hh = np.arange(HEADS, dtype=np.float64)
    lg = {False: np.log1p(-np.exp2(-5.0 - hh)), True: np.log1p(-np.exp2(-5.5 - hh))}
    t = np.arange(c, dtype=np.float64)[:, None]
    u = np.arange(c, dtype=np.float64)[None, :]
    pos = np.arange(c, dtype=np.float64)
    out = {}
    for rev in (False, True):
        g = lg[rev][:, None, None]
        if not rev:
            dmat = np.where(u <= t, np.exp((t - u) * g), 0.0)
            qd = np.exp((pos + 1.0)[None, :] * lg[rev][:, None])
            kd = np.exp((c - 1.0 - pos)[None, :] * lg[rev][:, None])
        else:
            dmat = np.where(u > t, np.exp((u - t) * g), 0.0)
            qd = np.exp((c - pos)[None, :] * lg[rev][:, None])
            kd = np.exp(pos[None, :] * lg[rev][:, None])
        cd = np.exp(c * lg[rev])
        out[rev] = (dmat.astype(np.float32),
                    np.repeat(qd[:, :, None], DV, axis=2).astype(np.float32),
                    np.repeat(kd[:, :, None], DK, axis=2).astype(np.float32),
                    np.repeat(cd[:, None, None], DV, axis=2).astype(np.float32))
    return out


def _rope(x, cos, sin):
    lane = lax.broadcasted_iota(jnp.int32, x.shape, 1)
    partner = jnp.where((lane % 64) < 32, pltpu.roll(x, DK - 32, 1), pltpu.roll(x, 32, 1))
    return x * cos + partner * sin


def _ret_chunk(q, k, v, cos, sin, state, dmat, qd, kd, cd):
    qr = _rope(q.astype(F32), cos, sin).astype(BF16)
    kr = _rope(k.astype(F32) * (DK ** -0.5), cos, sin)
    inter = _dot(qr, state.astype(BF16)) * qd
    scores = _dot_nt(qr, kr.astype(BF16)) * dmat
    out = inter + _dot(scores.astype(BF16), v)
    new_state = cd * state + _dot((kr * kd).T.astype(BF16), v)
    return out, new_state


def _ret_body(qf_ref, kf_ref, vf_ref, cosf_ref, sinf_ref, qb_ref, kb_ref, vb_ref, cosb_ref, sinb_ref,
              dmf_ref, qdf_ref, kdf_ref, cdf_ref, dmb_ref, qdb_ref, kdb_ref, cdb_ref,
              s0f_ref, s0b_ref, of_ref, ob_ref, sf_ref, sb_ref, *, nchunks):
    @pl.when(pl.program_id(2) == 0)
    def _():
        sf_ref[...] = s0f_ref[...]
        sb_ref[...] = s0b_ref[...]

    c = RET_CHUNK
    cf = (dmf_ref[0], qdf_ref[0], kdf_ref[0], cdf_ref[0])
    cb = (dmb_ref[0], qdb_ref[0], kdb_ref[0], cdb_ref[0])
    st_f = sf_ref[0, 0]
    st_b = sb_ref[0, 0]
    for n in range(nchunks):
        rf = slice(c * n, c * (n + 1))
        o, st_f = _ret_chunk(qf_ref[rf, :], kf_ref[rf, :], vf_ref[rf, :], cosf_ref[rf, :], sinf_ref[rf, :],
                             st_f, *cf)
        of_ref[rf, :] = o.astype(BF16)
        m = nchunks - 1 - n
        rb = slice(c * m, c * (m + 1))
        o, st_b = _ret_chunk(qb_ref[rb, :], kb_ref[rb, :], vb_ref[rb, :], cosb_ref[rb, :], sinb_ref[rb, :],
                             st_b, *cb)
        ob_ref[rb, :] = o.astype(BF16)
    sf_ref[0, 0] = st_f
    sb_ref[0, 0] = st_b


def _ret(p, cos, sin, s0f, s0b, batch, seq, tb):
    nb = seq // tb
    consts = _ret_consts()

    def fwd(col0, width):
        return lambda b, h, i: (b * nb + i, col0 // width + h)

    def bwd(col0, width):
        return lambda b, h, i: (b * nb + nb - 1 - i, col0 // width + h)

    head3 = lambda b, h, i: (h, 0, 0)
    st_spec = pl.BlockSpec((1, 1, DK, DV), lambda b, h, i: (b, h, 0, 0))
    st_shape = jax.ShapeDtypeStruct((batch, HEADS, DK, DV), F32)
    o_shape = jax.ShapeDtypeStruct((batch * seq, HEADS * DV), BF16)
    const_specs = []
    const_args = []
    for rev in (False, True):
        for a in consts[rev]:
            const_specs.append(pl.BlockSpec((1,) + a.shape[1:], head3))
            const_args.append(jnp.asarray(a))
    return pl.pallas_call(
        functools.partial(_ret_body, nchunks=tb // RET_CHUNK),
        grid=(batch, HEADS, nb),
        in_specs=[pl.BlockSpec((tb, DK), fwd(COL_RQ, DK)),
                  pl.BlockSpec((tb, DK), fwd(COL_RK, DK)),
                  pl.BlockSpec((tb, DV), fwd(COL_RV, DV)),
                  pl.BlockSpec((tb, DK), lambda b, h, i: (i, 0)),
                  pl.BlockSpec((tb, DK), lambda b, h, i: (i, 0)),
                  pl.BlockSpec((tb, DK), bwd(COL_RQ, DK)),
                  pl.BlockSpec((tb, DK), bwd(COL_RK, DK)),
                  pl.BlockSpec((tb, DV), bwd(COL_RV, DV)),
                  pl.BlockSpec((tb, DK), lambda b, h, i: (nb - 1 - i, 0)),
                  pl.BlockSpec((tb, DK), lambda b, h, i: (nb - 1 - i, 0))]
                 + const_specs + [st_spec, st_spec],
        out_specs=[pl.BlockSpec((tb, DV), lambda b, h, i: (b * nb + i, h)),
                   pl.BlockSpec((tb, DV), lambda b, h, i: (b * nb + nb - 1 - i, h)),
                   st_spec, st_spec],
        out_shape=[o_shape, o_shape, st_shape, st_shape],
        compiler_params=_cparams(("parallel", "parallel", "arbitrary")),
        name="ret",
    )(p, p, p, cos, sin, p, p, p, cos, sin, *const_args, s0f, s0b)


def _rope_tables(seq):
    n = DK // 4
    inv = ROPE_BASE ** (-jnp.arange(n, dtype=F32) / n)
    pos = jnp.arange(seq, dtype=jnp.int32)
    rows = (pos // GRID_W).astype(F32)
    cols = (pos % GRID_W).astype(F32)
    ar = rows[:, None] * inv[None, :]
    ac = cols[:, None] * inv[None, :]
    cos = jnp.concatenate([jnp.cos(ar), jnp.cos(ar), jnp.cos(ac), jnp.cos(ac)], axis=1)
    sin = jnp.concatenate([-jnp.sin(ar), jnp.sin(ar), -jnp.sin(ac), jnp.sin(ac)], axis=1)
    return cos, sin


def _route(logits_t):
    g = [logits_t[i:i + 1] for i in range(N_GROUPS)]
    gmax = jnp.maximum(jnp.maximum(g[0], g[1]), jnp.maximum(g[2], g[3]))
    gsel = jnp.where(g[0] == gmax, 0, jnp.where(g[1] == gmax, 1, jnp.where(g[2] == gmax, 2, 3)))
    gsum = (jnp.exp(g[0] - gmax) + jnp.exp(g[1] - gmax)) + (jnp.exp(g[2] - gmax) + jnp.exp(g[3] - gmax))
    gw = 1.0 / gsum
    e = [logits_t[EXP_ROW0 + EXPERTS_PER_GROUP * i:EXP_ROW0 + EXPERTS_PER_GROUP * (i + 1)]
         for i in range(N_GROUPS)]
    el = jnp.where(gsel == 0, e[0], jnp.where(gsel == 1, e[1], jnp.where(gsel == 2, e[2], e[3])))
    row = lax.broadcasted_iota(jnp.int32, el.shape, 0).astype(F32)
    none = float(EXPERTS_PER_GROUP)
    m1 = jnp.max(el, axis=0, keepdims=True)
    i1 = jnp.min(jnp.where(el == m1, row, none), axis=0, keepdims=True)
    el2 = jnp.where(row == i1, -jnp.inf, el)
    m2 = jnp.max(el2, axis=0, keepdims=True)
    i2 = jnp.min(jnp.where(el2 == m2, row, none), axis=0, keepdims=True)
    r = jnp.exp(m2 - m1)
    w1 = gw / (1.0 + r)
    w2 = gw * r / (1.0 + r)
    base = gsel * EXPERTS_PER_GROUP
    ids = jnp.concatenate([base + i1.astype(jnp.int32), base + i2.astype(jnp.int32)], axis=0)
    return ids, jnp.concatenate([w1, w2], axis=0)


def _store_token_tiled(ref, val):
    m = val.shape[0]
    for j in range(FEAT_TILES):
        ref[pl.ds(j, m, stride=FEAT_TILES), :] = val[:, LANES * j:LANES * (j + 1)]


def _load_token_tiled(ref, m):
    return jnp.concatenate([ref[pl.ds(j, m, stride=FEAT_TILES), :] for j in range(FEAT_TILES)], axis=1)


def _merge_body(gf_ref, gb_ref, rf_ref, rb_ref, gg_ref, rg_ref, mg_ref, mr_ref, x_ref,
                g1_ref, sh2_ref, sc2_ref, gn_ref, nf_ref, wg_ref, wr_ref, wo_ref, wrt_ref, brt_ref, upper_ref,
                h_ref, hn_ref, ids_ref, ew_ref, cnt_ref):
    @pl.when(pl.program_id(0) == 0)
    def _():
        cnt_ref[...] = jnp.zeros_like(cnt_ref)

    og = gf_ref[...].astype(F32) + gb_ref[...].astype(F32)
    orr = rf_ref[...].astype(F32) + rb_ref[...].astype(F32)
    gparts, rparts = [], []
    for hh in range(HEADS):
        seg = og[:, DV * hh:DV * (hh + 1)]
        ms = jnp.mean(seg * seg, axis=-1, keepdims=True)
        gparts.append(seg * lax.rsqrt(ms + NORM_EPS))
        seg = orr[:, DV * hh:DV * (hh + 1)]
        mu = jnp.mean(seg, axis=-1, keepdims=True)
        cen = seg - mu
        var = jnp.mean(cen * cen, axis=-1, keepdims=True)
        rparts.append(cen * lax.rsqrt(var + NORM_EPS))
    o_gla = jnp.concatenate(gparts, axis=1) * gn_ref[...] * _silu(gg_ref[...].astype(F32))
    o_ret = jnp.concatenate(rparts, axis=1) * _silu(rg_ref[...].astype(F32))
    y = (_sigmoid(mg_ref[...].astype(F32)) * _dot(o_gla.astype(BF16), wg_ref[...])
         + _sigmoid(mr_ref[...].astype(F32)) * _dot(o_ret.astype(BF16), wr_ref[...]))
    h = x_ref[...] + g1_ref[0] * _dot(y.astype(BF16), wo_ref[...])
    h_ref[...] = h
    ms = jnp.mean(h * h, axis=-1, keepdims=True)
    hn = h * lax.rsqrt(ms + NORM_EPS) * nf_ref[...] * (1.0 + sc2_ref[0]) + sh2_ref[0]
    _store_token_tiled(hn_ref, hn)
    hn_hi = hn.astype(BF16)
    hn_lo = (hn - hn_hi.astype(F32)).astype(BF16)
    both_w = _dot(hn_hi, wrt_ref[...])
    logits = (both_w[:, :ROUTER_W] + both_w[:, ROUTER_W:] + _dot(hn_lo, wrt_ref[:, :ROUTER_W])) + brt_ref[...]
    ids, ew = _route(logits.T)
    tm = ids.shape[1]

    erow = lax.broadcasted_iota(jnp.int32, (N_EXPERTS, tm), 0)
    oh0 = jnp.where(erow == ids[0:1], 1.0, 0.0)
    oh1 = jnp.where(erow == ids[1:2], 1.0, 0.0)
    both = oh0 + oh1
    before = _dot(both.astype(BF16), upper_ref[...]) + cnt_ref[:, 0:1].astype(F32)
    rank0 = jnp.sum(oh0 * before, axis=0, keepdims=True)
    rank1 = jnp.sum(oh1 * before, axis=0, keepdims=True)
    total = cnt_ref[:, 0:1] + jnp.sum(both, axis=1, keepdims=True).astype(jnp.int32)
    cnt_ref[...] = jnp.broadcast_to(total, cnt_ref.shape)
    ids_ref[...] = jnp.concatenate([ids, rank0.astype(jnp.int32), rank1.astype(jnp.int32),
                                    jnp.zeros((4, tm), jnp.int32)], axis=0)
    ew_ref[...] = jnp.concatenate([ew, jnp.zeros((6, tm), F32)], axis=0)


def _merge(gf, gb, rf, rb, p, x2, g1, sh2, sc2, gn, nf, wg, wr, wo, wrt, brt, seq, tm):
    t = x2.shape[0]
    tiles_per_batch = seq // tm
    row = lambda i: (i, 0)
    mod = lambda i: (i // tiles_per_batch, 0, 0)
    const = lambda i: (0, 0)
    tok = pl.BlockSpec((tm, D_MODEL), row)
    vec = pl.BlockSpec((1, D_MODEL), const)
    modspec = pl.BlockSpec((1, 1, D_MODEL), mod)
    wspec = pl.BlockSpec((D_MODEL, D_MODEL), const)
    upper = jnp.asarray(np.triu(np.ones((tm, tm), np.float32), 1), BF16)

    def pcol(col0):
        return pl.BlockSpec((tm, D_MODEL), lambda i: (i, col0 // D_MODEL))

    return pl.pallas_call(
        _merge_body,
        grid=(t // tm,),
        in_specs=[tok, tok, tok, tok, pcol(COL_GG), pcol(COL_RG), pcol(COL_MG), pcol(COL_MR), tok,
                  modspec, modspec, modspec, vec, vec, wspec, wspec, wspec,
                  pl.BlockSpec((D_MODEL, 2 * ROUTER_W), const), pl.BlockSpec((1, ROUTER_W), const),
                  pl.BlockSpec((tm, tm), const)],
        out_specs=[tok, pl.BlockSpec((tm * FEAT_TILES, LANES), row),
                   pl.BlockSpec((8, tm), lambda i: (0, i)), pl.BlockSpec((8, tm), lambda i: (0, i)),
                   pl.BlockSpec((N_EXPERTS, LANES), const)],
        out_shape=[jax.ShapeDtypeStruct((t, D_MODEL), F32), jax.ShapeDtypeStruct((t * FEAT_TILES, LANES), F32),
                   jax.ShapeDtypeStruct((8, t), jnp.int32), jax.ShapeDtypeStruct((8, t), F32),
                   jax.ShapeDtypeStruct((N_EXPERTS, LANES), jnp.int32)],
        compiler_params=_cparams(("arbitrary",)),
        name="merge",
    )(gf, gb, rf, rb, p, p, p, p, x2, g1, sh2, sc2, gn, nf, wg, wr, wo, wrt, brt, upper)


GATHER_UNROLL = 8


def _aligned(tok):
    off = tok * FEAT_TILES
    return off if isinstance(off, int) else pl.multiple_of(off, FEAT_TILES)


def _token_copy(src_hbm, dst, sem, src_tok, dst_tok):
    return pltpu.make_async_copy(src_hbm.at[pl.ds(_aligned(src_tok), FEAT_TILES), :],
                                 dst.at[pl.ds(_aligned(dst_tok), FEAT_TILES), :], sem)


def _wait_tokens(src_hbm, dst, sem):
    pltpu.make_async_copy(src_hbm.at[pl.ds(0, dst.shape[0]), :], dst, sem).wait()


GATHER_BUFS = 3
GATHER_AHEAD = GATHER_BUFS - 1


def _issue_rows(n, start_row, unrolled):
    if unrolled:
        for r in range(n):
            start_row(r)
    else:
        def body(g, carry):
            for u in range(GATHER_UNROLL):
                start_row(g * GATHER_UNROLL + u)
            return carry
        lax.fori_loop(0, n // GATHER_UNROLL, body, 0)


def _expert_body(be_ref, src0_ref, stok_ref, nused_ref, hn_hbm, wg_ref, wu_ref, wd_ref, y_ref, *scratch):
    i = pl.program_id(0)
    nused = nused_ref[0]
    bufs, sems = scratch[:GATHER_BUFS], scratch[GATHER_BUFS]
    last = stok_ref.shape[0] - 1

    def issue(block, s, unrolled):
        base = src0_ref[block]

        def start_row(r):
            tok = stok_ref[jnp.minimum(base + r, last)]
            _token_copy(hn_hbm, bufs[s], sems.at[s], tok, r).start()
        _issue_rows(MOE_BLOCK, start_row, unrolled)

    @pl.when(i == 0)
    def _():
        for b in range(GATHER_AHEAD):
            issue(b, b, False)

    for cur in range(GATHER_BUFS):
        mine = i % GATHER_BUFS == cur

        @pl.when(jnp.logical_and(mine, i < nused))
        def _():
            _wait_tokens(hn_hbm, bufs[cur], sems.at[cur])
            issue(i + GATHER_AHEAD, (cur + GATHER_AHEAD) % GATHER_BUFS, True)
            xb = _load_token_tiled(bufs[cur], MOE_BLOCK).astype(BF16)
            gate = _dot(xb, wg_ref[0].astype(BF16))
            up = _dot(xb, wu_ref[0].astype(BF16))
            hid = (_silu(gate) * up).astype(BF16)
            _store_token_tiled(y_ref, _dot(hid, wd_ref[0].astype(BF16)))

        @pl.when(jnp.logical_and(mine, jnp.logical_and(i >= nused, i < nused + GATHER_AHEAD)))
        def _():
            _wait_tokens(hn_hbm, bufs[cur], sems.at[cur])

    @pl.when(i >= nused)
    def _():
        y_ref[...] = jnp.zeros_like(y_ref)


def _experts(block_e, src0, stok, nused, hn, w_gate, w_up, w_down, nblk):
    wmap = lambda i, be, s0, st, nu: (be[i], 0, 0)
    rows = MOE_BLOCK * FEAT_TILES
    return pl.pallas_call(
        _expert_body,
        grid_spec=pltpu.PrefetchScalarGridSpec(
            num_scalar_prefetch=4,
            grid=(nblk,),
            in_specs=[pl.BlockSpec(memory_space=pl.ANY),
                      pl.BlockSpec((1, D_MODEL, EXPERT_FF), wmap),
                      pl.BlockSpec((1, D_MODEL, EXPERT_FF), wmap),
                      pl.BlockSpec((1, EXPERT_FF, D_MODEL), wmap)],
            out_specs=pl.BlockSpec((rows, LANES), lambda i, be, s0, st, nu: (i, 0)),
            scratch_shapes=[pltpu.VMEM((rows, LANES), F32)] * GATHER_BUFS
                           + [pltpu.SemaphoreType.DMA((GATHER_BUFS,))]),
        out_shape=jax.ShapeDtypeStruct((nblk * rows, LANES), F32),
        compiler_params=_cparams(("arbitrary",)),
        name="experts",
    )(block_e, src0, stok, nused, hn, w_gate, w_up, w_down)


COMBINE_TM = 128


def _combine_body(pos_ref, y_hbm, h_ref, ew_ref, g2_ref, nf_ref, o_ref, *scratch):
    i = pl.program_id(0)
    n = pl.num_programs(0)
    tm = COMBINE_TM
    t = pos_ref.shape[0] // TOP_K
    bufs, sems = scratch[:GATHER_BUFS], scratch[GATHER_BUFS]

    def issue(tile, s, unrolled):
        def start_row(j):
            k, r = j % TOP_K, j // TOP_K
            _token_copy(y_hbm, bufs[s].at[k], sems.at[s], pos_ref[k * t + tile * tm + r], r).start()
        _issue_rows(tm * TOP_K, start_row, unrolled)

    def wait(s):
        for k in range(TOP_K):
            _wait_tokens(y_hbm, bufs[s].at[k], sems.at[s])

    @pl.when(i == 0)
    def _():
        for b in range(GATHER_AHEAD):
            issue(b, b, False)

    for cur in range(GATHER_BUFS):
        @pl.when(i % GATHER_BUFS == cur)
        def _():
            wait(cur)
            issue(jnp.minimum(i + GATHER_AHEAD, n - 1), (cur + GATHER_AHEAD) % GATHER_BUFS, True)
            wt = jnp.concatenate([ew_ref[...]] * (LANES // 8), axis=0).T
            moe = (wt[:, 0:1] * _load_token_tiled(bufs[cur].at[0], tm)
                   + wt[:, 1:2] * _load_token_tiled(bufs[cur].at[1], tm))
            h = h_ref[...] + g2_ref[0] * moe
            ms = jnp.mean(h * h, axis=-1, keepdims=True)
            o_ref[...] = h * lax.rsqrt(ms + NORM_EPS) * nf_ref[...]

        @pl.when(jnp.logical_and(i % GATHER_BUFS == cur, i == n - 1))
        def _():
            for ahead in range(1, GATHER_BUFS):
                wait((cur + ahead) % GATHER_BUFS)


def _combine(pos, y_pad, h, ew, g2, nf, seq):
    t = h.shape[0]
    tm = COMBINE_TM
    tiles_per_batch = seq // tm
    ybuf = pltpu.VMEM((TOP_K, tm * FEAT_TILES, LANES), F32)
    return pl.pallas_call(
        _combine_body,
        grid_spec=pltpu.PrefetchScalarGridSpec(
            num_scalar_prefetch=1,
            grid=(t // tm,),
            in_specs=[pl.BlockSpec(memory_space=pl.ANY),
                      pl.BlockSpec((tm, D_MODEL), lambda i, pos: (i, 0)),
                      pl.BlockSpec((8, tm), lambda i, pos: (0, i)),
                      pl.BlockSpec((1, 1, D_MODEL), lambda i, pos: (i // tiles_per_batch, 0, 0)),
                      pl.BlockSpec((1, D_MODEL), lambda i, pos: (0, 0))],
            out_specs=pl.BlockSpec((tm, D_MODEL), lambda i, pos: (i, 0)),
            scratch_shapes=[ybuf] * GATHER_BUFS + [pltpu.SemaphoreType.DMA((GATHER_BUFS,))]),
        out_shape=jax.ShapeDtypeStruct((t, D_MODEL), F32),
        compiler_params=_cparams(("arbitrary",)),
        name="combine",
    )(pos, y_pad, h, ew, g2, nf)


def _dispatch_indices(ids, ranks, counts, t):
    a = t * TOP_K
    nblk = a // MOE_BLOCK + N_EXPERTS + GATHER_AHEAD
    experts = jnp.arange(N_EXPERTS, dtype=jnp.int32)
    starts = jnp.cumsum(counts) - counts
    padded = (counts + MOE_BLOCK - 1) // MOE_BLOCK * MOE_BLOCK
    pends = jnp.cumsum(padded)
    pstarts = pends - padded
    block_start = jnp.arange(nblk, dtype=jnp.int32) * MOE_BLOCK
    block_e = jnp.minimum(jnp.sum((block_start[:, None] >= pends[None, :]).astype(jnp.int32), axis=1),
                          N_EXPERTS - 1)
    of_block = (block_e[:, None] == experts[None, :]).astype(jnp.int32)
    src0 = block_start + jnp.sum(of_block * (starts - pstarts)[None, :], axis=1)
    nused = pends[-1:] // MOE_BLOCK
    tok = jnp.arange(t, dtype=jnp.int32)[None, :]
    slot = jnp.arange(TOP_K, dtype=jnp.int32)[:, None]
    key = ids * a + tok * TOP_K + slot
    stok = (jnp.sort(key.reshape(a)) % a) // TOP_K
    pstart_of = jnp.zeros_like(ids)
    for e in range(N_EXPERTS):
        pstart_of = jnp.where(ids == e, pstarts[e], pstart_of)
    pos = (ranks + pstart_of).reshape(a)
    i32 = lambda v: v.astype(jnp.int32)
    return i32(block_e), i32(src0), i32(stok), i32(nused), i32(pos), nblk


def kernel(x, c, ctx, c_ctx, w_ada, b_ada, norm_mix, norm_ffn, w_in, gla_lr_w, gla_lr_b, gla_norm,
           w_branch_gla, w_branch_ret, w_out, w_router_group, b_router_group, w_router_expert,
           b_router_expert, w_expert_gate, w_expert_up, w_expert_down, norm_final):
    batch, seq, d = x.shape
    ctx_len = ctx.shape[1]
    assert d == D_MODEL and w_ada.shape[0] == 1, "single-layer block with D_MODEL features"
    t = batch * seq

    c8 = jnp.zeros((8, d), F32).at[:batch].set(c).at[batch].set(c_ctx)
    mod = _ada(c8, w_ada[0], b_ada[0][None, :])
    sh1, sc1, g1, sh2, sc2, g2 = [mod[:, d * i:d * (i + 1)] for i in range(6)]
    lat = lambda m: m[:batch, None, :]
    cx = lambda m: m[batch:batch + 1, None, :]

    w = w_in[0]
    lr0 = COL_GG + HEADS * DV
    w_main = jnp.concatenate([w[:, :lr0], w[:, lr0 + 2 * GLA_RANK:]], axis=1).astype(BF16)
    w_lr = jnp.pad(w[:, lr0:lr0 + 2 * GLA_RANK], ((0, 0), (0, LANES - 2 * GLA_RANK))).astype(BF16)
    nm = norm_mix[0][None, :]

    p_ctx, lr_ctx = _proj(ctx.reshape(batch * ctx_len, d), cx(sh1), cx(sc1), nm, w_main, w_lr,
                          tm=ctx_len, rows_per_mod=batch * ctx_len)
    x2 = x.reshape(t, d)
    p_lat, lr_lat = _proj(x2, lat(sh1), lat(sc1), nm, w_main, w_lr, tm=1024, rows_per_mod=seq)

    wlr_f = jnp.zeros((LANES, HEADS * DK), F32).at[:GLA_RANK].set(gla_lr_w[0, 0])
    wlr_b = jnp.zeros((LANES, HEADS * DK), F32).at[GLA_RANK:2 * GLA_RANK].set(gla_lr_w[0, 1])
    blr_f, blr_b = gla_lr_b[0, 0][None, :], gla_lr_b[0, 1][None, :]
    zero_state = jnp.zeros((batch, HEADS, DK, DV), F32)

    _, _, gs_f, gs_b = _gla(p_ctx, lr_ctx, wlr_f, blr_f, wlr_b, blr_b, zero_state, zero_state,
                            batch, ctx_len, tb=ctx_len)
    gla_f, gla_b, _, _ = _gla(p_lat, lr_lat, wlr_f, blr_f, wlr_b, blr_b, gs_f, gs_b, batch, seq, tb=512)

    ones = jnp.ones((ctx_len, DK), F32)
    _, _, rs_f, rs_b = _ret(p_ctx, ones, jnp.zeros_like(ones), zero_state, zero_state, batch, ctx_len, tb=ctx_len)
    cos, sin = _rope_tables(seq)
    ret_f, ret_b, _, _ = _ret(p_lat, cos, sin, rs_f, rs_b, batch, seq, tb=512)

    wrt = jnp.zeros((d, ROUTER_W), F32)
    wrt = wrt.at[:, :N_GROUPS].set(w_router_group[0]).at[:, EXP_ROW0:EXP_ROW0 + N_EXPERTS].set(w_router_expert[0])
    wrt_hi = wrt.astype(BF16)
    wrt = jnp.concatenate([wrt_hi, (wrt - wrt_hi.astype(F32)).astype(BF16)], axis=1)
    brt = jnp.zeros((1, ROUTER_W), F32)
    brt = brt.at[0, :N_GROUPS].set(b_router_group[0]).at[0, EXP_ROW0:EXP_ROW0 + N_EXPERTS].set(b_router_expert[0])
    h, hn, ids8, ew8, cnt = _merge(gla_f, gla_b, ret_f, ret_b, p_lat, x2, lat(g1), lat(sh2), lat(sc2),
                                   jnp.tile(gla_norm[0], HEADS)[None, :], norm_ffn[0][None, :],
                                   w_branch_gla[0].astype(BF16), w_branch_ret[0].astype(BF16),
                                   w_out[0].astype(BF16), wrt, brt, seq, tm=256)

    block_e, src0, stok, nused, pos, nblk = _dispatch_indices(ids8[:TOP_K], ids8[TOP_K:2 * TOP_K], cnt[:, 0], t)
    y_pad = _experts(block_e, src0, stok, nused, hn, w_expert_gate[0], w_expert_up[0], w_expert_down[0], nblk)
    out = _combine(pos, y_pad, h, ew8, lat(g2), norm_final[None, :], seq)
    return out.reshape(batch, seq, d)
```

```python
import functools

import numpy as np
import jax
import jax.numpy as jnp
from jax import lax
from jax.experimental import pallas as pl
from jax.experimental.pallas import tpu as pltpu

F32 = jnp.float32
BF16 = jnp.bfloat16

D_MODEL = 1024
GRID_W = 64
HEADS = 4
DK = 128
DV = 256
GLA_RANK = 16
GLA_TAU = 16.0
GLA_CHUNK = 64
GLA_SUB = 16
RET_CHUNK = 128
ROPE_BASE = 10000.0
N_GROUPS = 4
EXPERTS_PER_GROUP = 8
N_EXPERTS = N_GROUPS * EXPERTS_PER_GROUP
TOP_K = 2
EXPERT_FF = 256
MOE_BLOCK = 128
NORM_EPS = 1e-6

COL_GQ, COL_GK, COL_GV, COL_GG = 0, 512, 1024, 2048
COL_RQ, COL_RK, COL_RV, COL_RG = 3072, 3584, 4096, 5120
COL_MG, COL_MR = 6144, 7168
PROJ_W = 8192
LANES = 128
FEAT_TILES = D_MODEL // LANES
ROUTER_W = 128
EXP_ROW0 = 8

VMEM_LIMIT = 56 * 1024 * 1024


def _cparams(sem):
    return pltpu.CompilerParams(dimension_semantics=sem, vmem_limit_bytes=VMEM_LIMIT)


def _sigmoid(x):
    return 1.0 / (1.0 + jnp.exp(-x))


def _silu(x):
    return x * _sigmoid(x)


def _dot(a, b):
    return jnp.dot(a, b, preferred_element_type=F32)


def _dot_nt(a, b):
    return lax.dot_general(a, b, (((1,), (1,)), ((), ())), preferred_element_type=F32)


def _ada_body(c_ref, w_ref, b_ref, o_ref):
    s = _silu(c_ref[...])
    o_ref[...] = _dot(s.astype(BF16), w_ref[...].astype(BF16)) + b_ref[...]


def _ada(c8, w, b):
    n = w.shape[1]
    tn = 1536
    return pl.pallas_call(
        _ada_body,
        grid=(n // tn,),
        in_specs=[pl.BlockSpec((8, D_MODEL), lambda j: (0, 0)),
                  pl.BlockSpec((D_MODEL, tn), lambda j: (0, j)),
                  pl.BlockSpec((1, tn), lambda j: (0, j))],
        out_specs=pl.BlockSpec((8, tn), lambda j: (0, j)),
        out_shape=jax.ShapeDtypeStruct((8, n), F32),
        compiler_params=_cparams(("arbitrary",)),
        name="ada",
    )(c8, w, b)


def _proj_body(x_ref, sh_ref, sc_ref, g_ref, w_ref, wlr_ref, o_ref, lr_ref, h_ref):
    @pl.when(pl.program_id(1) == 0)
    def _():
        x = x_ref[...]
        ms = jnp.mean(x * x, axis=-1, keepdims=True)
        y = x * lax.rsqrt(ms + NORM_EPS) * g_ref[...]
        hb = (y * (1.0 + sc_ref[0]) + sh_ref[0]).astype(BF16)
        h_ref[...] = hb
        lr_ref[...] = _dot(hb, wlr_ref[...])

    o_ref[...] = _dot(h_ref[...], w_ref[...]).astype(BF16)


def _proj(x2, sh, sc, g, w_main, w_lr, tm, rows_per_mod):
    t = x2.shape[0]
    tn = 2048
    tiles_per_mod = rows_per_mod // tm
    mod_map = lambda i, j: (i // tiles_per_mod, 0, 0)
    return pl.pallas_call(
        _proj_body,
        grid=(t // tm, PROJ_W // tn),
        in_specs=[pl.BlockSpec((tm, D_MODEL), lambda i, j: (i, 0)),
                  pl.BlockSpec((1, 1, D_MODEL), mod_map),
                  pl.BlockSpec((1, 1, D_MODEL), mod_map),
                  pl.BlockSpec((1, D_MODEL), lambda i, j: (0, 0)),
                  pl.BlockSpec((D_MODEL, tn), lambda i, j: (0, j)),
                  pl.BlockSpec((D_MODEL, LANES), lambda i, j: (0, 0))],
        out_specs=[pl.BlockSpec((tm, tn), lambda i, j: (i, j)),
                   pl.BlockSpec((tm, LANES), lambda i, j: (i, 0))],
        out_shape=[jax.ShapeDtypeStruct((t, PROJ_W), BF16),
                   jax.ShapeDtypeStruct((t, LANES), F32)],
        scratch_shapes=[pltpu.VMEM((tm, D_MODEL), BF16)],
        compiler_params=_cparams(("parallel", "arbitrary")),
        name="proj",
    )(x2, sh, sc, g, w_main, w_lr)


def _gla_consts():
    c, s = GLA_CHUNK, GLA_SUB
    msk = {}
    for rev in (False, True):
        cols = []
        for i in range(c // s):
            keys = np.arange(c)[_key_rows(i, rev)][None, :]
            t = np.arange(c)[:, None]
            visible = (keys > t) if rev else (keys <= t)
            cols.append(np.where(t // s == i, visible, False))
        msk[rev] = np.concatenate(cols, axis=1).astype(np.float32)
    return msk


def _key_rows(i, rev):
    return slice(GLA_SUB * i, GLA_CHUNK) if rev else slice(0, GLA_SUB * (i + 1))


def _log_gate(lr, w, b):
    z = _dot(lr.astype(BF16), w.astype(BF16)) + b
    return -(jnp.maximum(-z, 0.0) + jnp.log(1.0 + jnp.exp(-jnp.abs(z)))) * (1.0 / GLA_TAU)


def _subchunk_scan(la, rev):
    c = la.shape[0]
    pos = lax.broadcasted_iota(jnp.int32, la.shape, 0) % GLA_SUB
    w = la
    step = 1
    while step < GLA_SUB:
        if rev:
            w = w + jnp.where(pos < GLA_SUB - step, pltpu.roll(w, c - step, 0), 0.0)
        else:
            w = w + jnp.where(pos >= step, pltpu.roll(w, step, 0), 0.0)
        step *= 2
    return w


def _gla_chunk(q, k, v, la, state, mask, rev):
    c, s = GLA_CHUNK, GLA_SUB
    nsub = c // s
    w = _subchunk_scan(la, rev)
    tot = [w[s * i:s * i + 1] if rev else w[s * (i + 1) - 1:s * (i + 1)] for i in range(nsub)]
    anchors = [None] * nsub
    acc = jnp.zeros_like(tot[0])
    for i in (reversed(range(nsub)) if rev else range(nsub)):
        anchors[i] = acc
        acc = acc + tot[i]
    b_end = acc
    ref = jnp.concatenate([jnp.broadcast_to(a, (s, DK)) for a in anchors], axis=0)
    b = w + ref

    qt = q.astype(F32) * (DK ** -0.5) * jnp.exp(w)
    kf = k.astype(F32)
    inter = _dot((qt * jnp.exp(ref)).astype(BF16), state.astype(BF16))

    kstack = jnp.concatenate([kf[_key_rows(i, rev)] * jnp.exp(anchors[i] - b[_key_rows(i, rev)])
                              for i in range(nsub)], axis=0).astype(BF16)
    vstack = jnp.concatenate([v[_key_rows(i, rev)] for i in range(nsub)], axis=0)
    scores = jnp.where(mask != 0.0, _dot_nt(qt.astype(BF16), kstack), 0.0)
    out = inter + _dot(scores.astype(BF16), vstack)

    kst = kf * jnp.exp(b_end - b)
    xt = jnp.concatenate([kst, jnp.broadcast_to(jnp.exp(b_end), (c, DK))], axis=0).T
    new_state = xt[:, c:c + 1] * state + _dot(xt[:, :c].astype(BF16), v)
    return out, new_state


def _gla_body(qf_ref, kf_ref, vf_ref, lrf_ref, qb_ref, kb_ref, vb_ref, lrb_ref,
              wf_ref, bf_ref, wb_ref, bb_ref, mskf_ref, mskb_ref,
              s0f_ref, s0b_ref, of_ref, ob_ref, sf_ref, sb_ref, *, nchunks):
    @pl.when(pl.program_id(2) == 0)
    def _():
        sf_ref[...] = s0f_ref[...]
        sb_ref[...] = s0b_ref[...]

    c = GLA_CHUNK
    la_f = _log_gate(lrf_ref[...], wf_ref[...], bf_ref[...])
    la_b = _log_gate(lrb_ref[...], wb_ref[...], bb_ref[...])
    mskf, mskb = mskf_ref[...], mskb_ref[...]
    st_f = sf_ref[0, 0]
    st_b = sb_ref[0, 0]
    for n in range(nchunks):
        rf = slice(c * n, c * (n + 1))
        o, st_f = _gla_chunk(qf_ref[rf, :], kf_ref[rf, :], vf_ref[rf, :], la_f[rf, :], st_f, mskf, False)
        of_ref[rf, :] = o.astype(BF16)
        m = nchunks - 1 - n
        rb = slice(c * m, c * (m + 1))
        o, st_b = _gla_chunk(qb_ref[rb, :], kb_ref[rb, :], vb_ref[rb, :], la_b[rb, :], st_b, mskb, True)
        ob_ref[rb, :] = o.astype(BF16)
    sf_ref[0, 0] = st_f
    sb_ref[0, 0] = st_b


def _gla(p, lr, wlr_f, blr_f, wlr_b, blr_b, s0f, s0b, batch, seq, tb):
    nb = seq // tb
    msk = _gla_consts()
    mskf, mskb = jnp.asarray(msk[False]), jnp.asarray(msk[True])

    def fwd(col0, width):
        return lambda b, h, i: (b * nb + i, col0 // width + h)

    def bwd(col0, width):
        return lambda b, h, i: (b * nb + nb - 1 - i, col0 // width + h)

    const2 = lambda b, h, i: (0, 0)
    headcol = lambda b, h, i: (0, h)
    st_map = lambda b, h, i: (b, h, 0, 0)
    st_spec = pl.BlockSpec((1, 1, DK, DV), st_map)
    st_shape = jax.ShapeDtypeStruct((batch, HEADS, DK, DV), F32)
    o_shape = jax.ShapeDtypeStruct((batch * seq, HEADS * DV), BF16)
    return pl.pallas_call(
        functools.partial(_gla_body, nchunks=tb // GLA_CHUNK),
        grid=(batch, HEADS, nb),
        in_specs=[pl.BlockSpec((tb, DK), fwd(COL_GQ, DK)),
                  pl.BlockSpec((tb, DK), fwd(COL_GK, DK)),
                  pl.BlockSpec((tb, DV), fwd(COL_GV, DV)),
                  pl.BlockSpec((tb, LANES), lambda b, h, i: (b * nb + i, 0)),
                  pl.BlockSpec((tb, DK), bwd(COL_GQ, DK)),
                  pl.BlockSpec((tb, DK), bwd(COL_GK, DK)),
                  pl.BlockSpec((tb, DV), bwd(COL_GV, DV)),
                  pl.BlockSpec((tb, LANES), lambda b, h, i: (b * nb + nb - 1 - i, 0)),
                  pl.BlockSpec((LANES, DK), headcol),
                  pl.BlockSpec((1, DK), headcol),
                  pl.BlockSpec((LANES, DK), headcol),
                  pl.BlockSpec((1, DK), headcol),
                  pl.BlockSpec(mskf.shape, const2),
                  pl.BlockSpec(mskb.shape, const2),
                  st_spec, st_spec],
        out_specs=[pl.BlockSpec((tb, DV), lambda b, h, i: (b * nb + i, h)),
                   pl.BlockSpec((tb, DV), lambda b, h, i: (b * nb + nb - 1 - i, h)),
                   st_spec, st_spec],
        out_shape=[o_shape, o_shape, st_shape, st_shape],
        compiler_params=_cparams(("parallel", "parallel", "arbitrary")),
        name="gla",
    )(p, p, p, lr, p, p, p, lr, wlr_f, blr_f, wlr_b, blr_b, mskf, mskb, s0f, s0b)


def _ret_consts():
    c = RET_CHUNK
    hh = np.arange(HEADS, dtype=np.float64)
    lg = {False: np.log1p(-np.exp2(-5.0 - hh)), True: np.log1p(-np.exp2(-5.5 - hh))}
    t = np.arange(c, dtype=np.float64)[:, None]
    u = np.arange(c, dtype=np.float64)[None, :]
    pos = np.arange(c, dtype=np.float64)
    out = {}
    for rev in (False, True):
        g = lg[rev][:, None, None]
        if not rev:
            dmat = np.where(u <= t, np.exp((t - u) * g), 0.0)
            qd = np.exp((pos + 1.0)[None, :] * lg[rev][:, None])
            kd = np.exp((c - 1.0 - pos)[None, :] * lg[rev][:, None])
        else:
            dmat = np.where(u > t, np.exp((u - t) * g), 0.0)
            qd = np.exp((c - pos)[None, :] * lg[rev][:, None])
            kd = np.exp(pos[None, :] * lg[rev][:, None])
        cd = np.exp(c * lg[rev])
        out[rev] = (dmat.astype(np.float32),
                    np.repeat(qd[:, :, None], DV, axis=2).astype(np.float32),
                    np.repeat(kd[:, :, None], DK, axis=2).astype(np.float32),
                    np.repeat(cd[:, None, None], DV, axis=2).astype(np.float32))
    return out


def _rope(x, cos, sin):
    lane = lax.broadcasted_iota(jnp.int32, x.shape, 1)
    partner = jnp.where((lane % 64) < 32, pltpu.roll(x, DK - 32, 1), pltpu.roll(x, 32, 1))
    return x * cos + partner * sin


def _ret_chunk(q, k, v, cos, sin, state, dmat, qd, kd, cd):
    qr = _rope(q.astype(F32), cos, sin).astype(BF16)
    kr = _rope(k.astype(F32) * (DK ** -0.5), cos, sin)
    inter = _dot(qr, state.astype(BF16)) * qd
    scores = _dot_nt(qr, kr.astype(BF16)) * dmat
    out = inter + _dot(scores.astype(BF16), v)
    new_state = cd * state + _dot((kr * kd).T.astype(BF16), v)
    return out, new_state


def _ret_body(qf_ref, kf_ref, vf_ref, cosf_ref, sinf_ref, qb_ref, kb_ref, vb_ref, cosb_ref, sinb_ref,
              dmf_ref, qdf_ref, kdf_ref, cdf_ref, dmb_ref, qdb_ref, kdb_ref, cdb_ref,
              s0f_ref, s0b_ref, of_ref, ob_ref, sf_ref, sb_ref, *, nchunks):
    @pl.when(pl.program_id(2) == 0)
    def _():
        sf_ref[...] = s0f_ref[...]
        sb_ref[...] = s0b_ref[...]

    c = RET_CHUNK
    cf = (dmf_ref[0], qdf_ref[0], kdf_ref[0], cdf_ref[0])
    cb = (dmb_ref[0], qdb_ref[0], kdb_ref[0], cdb_ref[0])
    st_f = sf_ref[0, 0]
    st_b = sb_ref[0, 0]
    for n in range(nchunks):
        rf = slice(c * n, c * (n + 1))
        o, st_f = _ret_chunk(qf_ref[rf, :], kf_ref[rf, :], vf_ref[rf, :], cosf_ref[rf, :], sinf_ref[rf, :],
                             st_f, *cf)
        of_ref[rf, :] = o.astype(BF16)
        m = nchunks - 1 - n
        rb = slice(c * m, c * (m + 1))
        o, st_b = _ret_chunk(qb_ref[rb, :], kb_ref[rb, :], vb_ref[rb, :], cosb_ref[rb, :], sinb_ref[rb, :],
                             st_b, *cb)
        ob_ref[rb, :] = o.astype(BF16)
    sf_ref[0, 0] = st_f
    sb_ref[0, 0] = st_b


def _ret(p, cos, sin, s0f, s0b, batch, seq, tb):
    nb = seq // tb
    consts = _ret_consts()

    def fwd(col0, width):
        return lambda b, h, i: (b * nb + i, col0 // width + h)

    def bwd(col0, width):
        return lambda b, h, i: (b * nb + nb - 1 - i, col0 // width + h)

    head3 = lambda b, h, i: (h, 0, 0)
    st_spec = pl.BlockSpec((1, 1, DK, DV), lambda b, h, i: (b, h, 0, 0))
    st_shape = jax.ShapeDtypeStruct((batch, HEADS, DK, DV), F32)
    o_shape = jax.ShapeDtypeStruct((batch * seq, HEADS * DV), BF16)
    const_specs = []
    const_args = []
    for rev in (False, True):
        for a in consts[rev]:
            const_specs.append(pl.BlockSpec((1,) + a.shape[1:], head3))
            const_args.append(jnp.asarray(a))
    return pl.pallas_call(
        functools.partial(_ret_body, nchunks=tb // RET_CHUNK),
        grid=(batch, HEADS, nb),
        in_specs=[pl.BlockSpec((tb, DK), fwd(COL_RQ, DK)),
                  pl.BlockSpec((tb, DK), fwd(COL_RK, DK)),
                  pl.BlockSpec((tb, DV), fwd(COL_RV, DV)),
                  pl.BlockSpec((tb, DK), lambda b, h, i: (i, 0)),
                  pl.BlockSpec((tb, DK), lambda b, h, i: (i, 0)),
                  pl.BlockSpec((tb, DK), bwd(COL_RQ, DK)),
                  pl.BlockSpec((tb, DK), bwd(COL_RK, DK)),
                  pl.BlockSpec((tb, DV), bwd(COL_RV, DV)),
                  pl.BlockSpec((tb, DK), lambda b, h, i: (nb - 1 - i, 0)),
                  pl.BlockSpec((tb, DK), lambda b, h, i: (nb - 1 - i, 0))]
                 + const_specs + [st_spec, st_spec],
        out_specs=[pl.BlockSpec((tb, DV), lambda b, h, i: (b * nb + i, h)),
                   pl.BlockSpec((tb, DV), lambda b, h, i: (b * nb + nb - 1 - i, h)),
                   st_spec, st_spec],
        out_shape=[o_shape, o_shape, st_shape, st_shape],
        compiler_params=_cparams(("parallel", "parallel", "arbitrary")),
        name="ret",
    )(p, p, p, cos, sin, p, p, p, cos, sin, *const_args, s0f, s0b)


def _rope_tables(seq):
    n = DK // 4
    inv = ROPE_BASE ** (-jnp.arange(n, dtype=F32) / n)
    pos = jnp.arange(seq, dtype=jnp.int32)
    rows = (pos // GRID_W).astype(F32)
    cols = (pos % GRID_W).astype(F32)
    ar = rows[:, None] * inv[None, :]
    ac = cols[:, None] * inv[None, :]
    cos = jnp.concatenate([jnp.cos(ar), jnp.cos(ar), jnp.cos(ac), jnp.cos(ac)], axis=1)
    sin = jnp.concatenate([-jnp.sin(ar), jnp.sin(ar), -jnp.sin(ac), jnp.sin(ac)], axis=1)
    return cos, sin


def _route(logits_t):
    g = [logits_t[i:i + 1] for i in range(N_GROUPS)]
    gmax = jnp.maximum(jnp.maximum(g[0], g[1]), jnp.maximum(g[2], g[3]))
    gsel = jnp.where(g[0] == gmax, 0, jnp.where(g[1] == gmax, 1, jnp.where(g[2] == gmax, 2, 3)))
    gsum = (jnp.exp(g[0] - gmax) + jnp.exp(g[1] - gmax)) + (jnp.exp(g[2] - gmax) + jnp.exp(g[3] - gmax))
    gw = 1.0 / gsum
    e = [logits_t[EXP_ROW0 + EXPERTS_PER_GROUP * i:EXP_ROW0 + EXPERTS_PER_GROUP * (i + 1)]
         for i in range(N_GROUPS)]
    el = jnp.where(gsel == 0, e[0], jnp.where(gsel == 1, e[1], jnp.where(gsel == 2, e[2], e[3])))
    row = lax.broadcasted_iota(jnp.int32, el.shape, 0).astype(F32)
    none = float(EXPERTS_PER_GROUP)
    m1 = jnp.max(el, axis=0, keepdims=True)
    i1 = jnp.min(jnp.where(el == m1, row, none), axis=0, keepdims=True)
    el2 = jnp.where(row == i1, -jnp.inf, el)
    m2 = jnp.max(el2, axis=0, keepdims=True)
    i2 = jnp.min(jnp.where(el2 == m2, row, none), axis=0, keepdims=True)
    r = jnp.exp(m2 - m1)
    w1 = gw / (1.0 + r)
    w2 = gw * r / (1.0 + r)
    base = gsel * EXPERTS_PER_GROUP
    ids = jnp.concatenate([base + i1.astype(jnp.int32), base + i2.astype(jnp.int32)], axis=0)
    return ids, jnp.concatenate([w1, w2], axis=0)


def _store_token_tiled(ref, val):
    m = val.shape[0]
    for j in range(FEAT_TILES):
        ref[pl.ds(j, m, stride=FEAT_TILES), :] = val[:, LANES * j:LANES * (j + 1)]


def _load_token_tiled(ref, m):
    return jnp.concatenate([ref[pl.ds(j, m, stride=FEAT_TILES), :] for j in range(FEAT_TILES)], axis=1)


def _merge_body(gf_ref, gb_ref, rf_ref, rb_ref, gg_ref, rg_ref, mg_ref, mr_ref, x_ref,
                g1_ref, sh2_ref, sc2_ref, gn_ref, nf_ref, wg_ref, wr_ref, wo_ref, wrt_ref, brt_ref, upper_ref,
                h_ref, hn_ref, ids_ref, ew_ref, cnt_ref):
    @pl.when(pl.program_id(0) == 0)
    def _():
        cnt_ref[...] = jnp.zeros_like(cnt_ref)

    og = gf_ref[...].astype(F32) + gb_ref[...].astype(F32)
    orr = rf_ref[...].astype(F32) + rb_ref[...].astype(F32)
    gparts, rparts = [], []
    for hh in range(HEADS):
        seg = og[:, DV * hh:DV * (hh + 1)]
        ms = jnp.mean(seg * seg, axis=-1, keepdims=True)
        gparts.append(seg * lax.rsqrt(ms + NORM_EPS))
        seg = orr[:, DV * hh:DV * (hh + 1)]
        mu = jnp.mean(seg, axis=-1, keepdims=True)
        cen = seg - mu
        var = jnp.mean(cen * cen, axis=-1, keepdims=True)
        rparts.append(cen * lax.rsqrt(var + NORM_EPS))
    o_gla = jnp.concatenate(gparts, axis=1) * gn_ref[...] * _silu(gg_ref[...].astype(F32))
    o_ret = jnp.concatenate(rparts, axis=1) * _silu(rg_ref[...].astype(F32))
    y = (_sigmoid(mg_ref[...].astype(F32)) * _dot(o_gla.astype(BF16), wg_ref[...])
         + _sigmoid(mr_ref[...].astype(F32)) * _dot(o_ret.astype(BF16), wr_ref[...]))
    h = x_ref[...] + g1_ref[0] * _dot(y.astype(BF16), wo_ref[...])
    h_ref[...] = h
    ms = jnp.mean(h * h, axis=-1, keepdims=True)
    hn = h * lax.rsqrt(ms + NORM_EPS) * nf_ref[...] * (1.0 + sc2_ref[0]) + sh2_ref[0]
    _store_token_tiled(hn_ref, hn)
    hn_hi = hn.astype(BF16)
    hn_lo = (hn - hn_hi.astype(F32)).astype(BF16)
    both_w = _dot(hn_hi, wrt_ref[...])
    logits = (both_w[:, :ROUTER_W] + both_w[:, ROUTER_W:] + _dot(hn_lo, wrt_ref[:, :ROUTER_W])) + brt_ref[...]
    ids, ew = _route(logits.T)
    tm = ids.shape[1]

    erow = lax.broadcasted_iota(jnp.int32, (N_EXPERTS, tm), 0)
    oh0 = jnp.where(erow == ids[0:1], 1.0, 0.0)
    oh1 = jnp.where(erow == ids[1:2], 1.0, 0.0)
    both = oh0 + oh1
    before = _dot(both.astype(BF16), upper_ref[...]) + cnt_ref[:, 0:1].astype(F32)
    rank0 = jnp.sum(oh0 * before, axis=0, keepdims=True)
    rank1 = jnp.sum(oh1 * before, axis=0, keepdims=True)
    total = cnt_ref[:, 0:1] + jnp.sum(both, axis=1, keepdims=True).astype(jnp.int32)
    cnt_ref[...] = jnp.broadcast_to(total, cnt_ref.shape)
    ids_ref[...] = jnp.concatenate([ids, rank0.astype(jnp.int32), rank1.astype(jnp.int32),
                                    jnp.zeros((4, tm), jnp.int32)], axis=0)
    ew_ref[...] = jnp.concatenate([ew, jnp.zeros((6, tm), F32)], axis=0)


def _merge(gf, gb, rf, rb, p, x2, g1, sh2, sc2, gn, nf, wg, wr, wo, wrt, brt, seq, tm):
    t = x2.shape[0]
    tiles_per_batch = seq // tm
    row = lambda i: (i, 0)
    mod = lambda i: (i // tiles_per_batch, 0, 0)
    const = lambda i: (0, 0)
    tok = pl.BlockSpec((tm, D_MODEL), row)
    vec = pl.BlockSpec((1, D_MODEL), const)
    modspec = pl.BlockSpec((1, 1, D_MODEL), mod)
    wspec = pl.BlockSpec((D_MODEL, D_MODEL), const)
    upper = jnp.asarray(np.triu(np.ones((tm, tm), np.float32), 1), BF16)

    def pcol(col0):
        return pl.BlockSpec((tm, D_MODEL), lambda i: (i, col0 // D_MODEL))

    return pl.pallas_call(
        _merge_body,
        grid=(t // tm,),
        in_specs=[tok, tok, tok, tok, pcol(COL_GG), pcol(COL_RG), pcol(COL_MG), pcol(COL_MR), tok,
                  modspec, modspec, modspec, vec, vec, wspec, wspec, wspec,
                  pl.BlockSpec((D_MODEL, 2 * ROUTER_W), const), pl.BlockSpec((1, ROUTER_W), const),
                  pl.BlockSpec((tm, tm), const)],
        out_specs=[tok, pl.BlockSpec((tm * FEAT_TILES, LANES), row),
                   pl.BlockSpec((8, tm), lambda i: (0, i)), pl.BlockSpec((8, tm), lambda i: (0, i)),
                   pl.BlockSpec((N_EXPERTS, LANES), const)],
        out_shape=[jax.ShapeDtypeStruct((t, D_MODEL), F32), jax.ShapeDtypeStruct((t * FEAT_TILES, LANES), F32),
                   jax.ShapeDtypeStruct((8, t), jnp.int32), jax.ShapeDtypeStruct((8, t), F32),
                   jax.ShapeDtypeStruct((N_EXPERTS, LANES), jnp.int32)],
        compiler_params=_cparams(("arbitrary",)),
        name="merge",
    )(gf, gb, rf, rb, p, p, p, p, x2, g1, sh2, sc2, gn, nf, wg, wr, wo, wrt, brt, upper)


GATHER_UNROLL = 8


def _aligned(tok):
    off = tok * FEAT_TILES
    return off if isinstance(off, int) else pl.multiple_of(off, FEAT_TILES)


def _token_copy(src_hbm, dst, sem, src_tok, dst_tok):
    return pltpu.make_async_copy(src_hbm.at[pl.ds(_aligned(src_tok), FEAT_TILES), :],
                                 dst.at[pl.ds(_aligned(dst_tok), FEAT_TILES), :], sem)


def _wait_tokens(src_hbm, dst, sem):
    pltpu.make_async_copy(src_hbm.at[pl.ds(0, dst.shape[0]), :], dst, sem).wait()


GATHER_BUFS = 3
GATHER_AHEAD = GATHER_BUFS - 1


def _issue_rows(n, start_row, unrolled):
    if unrolled:
        for r in range(n):
            start_row(r, r % 2)
    else:
        def body(g, carry):
            for u in range(GATHER_UNROLL):
                start_row(g * GATHER_UNROLL + u, u % 2)
            return carry
        lax.fori_loop(0, n // GATHER_UNROLL, body, 0)


def _expert_body(be_ref, src0_ref, stok_ref, nused_ref, hn_hbm, wg_ref, wu_ref, wd_ref, y_ref, *scratch):
    i = pl.program_id(0)
    nused = nused_ref[0]
    bufs, sems = scratch[:GATHER_BUFS], scratch[GATHER_BUFS]
    last = stok_ref.shape[0] - 1

    def issue(block, s, unrolled):
        base = src0_ref[block]

        def start_row(r, priority):
            tok = stok_ref[jnp.minimum(base + r, last)]
            _token_copy(hn_hbm, bufs[s], sems.at[s], tok, r).start(priority=priority)
        _issue_rows(MOE_BLOCK, start_row, unrolled)

    @pl.when(i == 0)
    def _():
        for b in range(GATHER_AHEAD):
            issue(b, b, False)

    for cur in range(GATHER_BUFS):
        mine = i % GATHER_BUFS == cur

        @pl.when(jnp.logical_and(mine, i < nused))
        def _():
            _wait_tokens(hn_hbm, bufs[cur], sems.at[cur])
            issue(i + GATHER_AHEAD, (cur + GATHER_AHEAD) % GATHER_BUFS, True)
            xb = _load_token_tiled(bufs[cur], MOE_BLOCK).astype(BF16)
            gate = _dot(xb, wg_ref[0].astype(BF16))
            up = _dot(xb, wu_ref[0].astype(BF16))
            hid = (_silu(gate) * up).astype(BF16)
            _store_token_tiled(y_ref, _dot(hid, wd_ref[0].astype(BF16)))

        @pl.when(jnp.logical_and(mine, jnp.logical_and(i >= nused, i < nused + GATHER_AHEAD)))
        def _():
            _wait_tokens(hn_hbm, bufs[cur], sems.at[cur])

    @pl.when(i >= nused)
    def _():
        y_ref[...] = jnp.zeros_like(y_ref)


def _experts(block_e, src0, stok, nused, hn, w_gate, w_up, w_down, nblk):
    wmap = lambda i, be, s0, st, nu: (be[i], 0, 0)
    rows = MOE_BLOCK * FEAT_TILES
    return pl.pallas_call(
        _expert_body,
        grid_spec=pltpu.PrefetchScalarGridSpec(
            num_scalar_prefetch=4,
            grid=(nblk,),
            in_specs=[pl.BlockSpec(memory_space=pl.ANY),
                      pl.BlockSpec((1, D_MODEL, EXPERT_FF), wmap),
                      pl.BlockSpec((1, D_MODEL, EXPERT_FF), wmap),
                      pl.BlockSpec((1, EXPERT_FF, D_MODEL), wmap)],
            out_specs=pl.BlockSpec((rows, LANES), lambda i, be, s0, st, nu: (i, 0)),
            scratch_shapes=[pltpu.VMEM((rows, LANES), F32)] * GATHER_BUFS
                           + [pltpu.SemaphoreType.DMA((GATHER_BUFS,))]),
        out_shape=jax.ShapeDtypeStruct((nblk * rows, LANES), F32),
        compiler_params=_cparams(("arbitrary",)),
        name="experts",
    )(block_e, src0, stok, nused, hn, w_gate, w_up, w_down)


COMBINE_TM = 128


def _combine_body(pos_ref, y_hbm, h_ref, ew_ref, g2_ref, nf_ref, o_ref, *scratch):
    i = pl.program_id(0)
    n = pl.num_programs(0)
    tm = COMBINE_TM
    t = pos_ref.shape[0] // TOP_K
    bufs, sems = scratch[:GATHER_BUFS], scratch[GATHER_BUFS]

    def issue(tile, s, unrolled):
        def start_row(j, priority):
            k, r = j % TOP_K, j // TOP_K
            _token_copy(y_hbm, bufs[s].at[k], sems.at[s], pos_ref[k * t + tile * tm + r], r).start(priority=priority)
        _issue_rows(tm * TOP_K, start_row, unrolled)

    def wait(s):
        for k in range(TOP_K):
            _wait_tokens(y_hbm, bufs[s].at[k], sems.at[s])

    @pl.when(i == 0)
    def _():
        for b in range(GATHER_AHEAD):
            issue(b, b, False)

    for cur in range(GATHER_BUFS):
        @pl.when(i % GATHER_BUFS == cur)
        def _():
            wait(cur)
            issue(jnp.minimum(i + GATHER_AHEAD, n - 1), (cur + GATHER_AHEAD) % GATHER_BUFS, True)
            wt = jnp.concatenate([ew_ref[...]] * (LANES // 8), axis=0).T
            moe = (wt[:, 0:1] * _load_token_tiled(bufs[cur].at[0], tm)
                   + wt[:, 1:2] * _load_token_tiled(bufs[cur].at[1], tm))
            h = h_ref[...] + g2_ref[0] * moe
            ms = jnp.mean(h * h, axis=-1, keepdims=True)
            o_ref[...] = h * lax.rsqrt(ms + NORM_EPS) * nf_ref[...]

        @pl.when(jnp.logical_and(i % GATHER_BUFS == cur, i == n - 1))
        def _():
            for ahead in range(1, GATHER_BUFS):
                wait((cur + ahead) % GATHER_BUFS)


def _combine(pos, y_pad, h, ew, g2, nf, seq):
    t = h.shape[0]
    tm = COMBINE_TM
    tiles_per_batch = seq // tm
    ybuf = pltpu.VMEM((TOP_K, tm * FEAT_TILES, LANES), F32)
    return pl.pallas_call(
        _combine_body,
        grid_spec=pltpu.PrefetchScalarGridSpec(
            num_scalar_prefetch=1,
            grid=(t // tm,),
            in_specs=[pl.BlockSpec(memory_space=pl.ANY),
                      pl.BlockSpec((tm, D_MODEL), lambda i, pos: (i, 0)),
                      pl.BlockSpec((8, tm), lambda i, pos: (0, i)),
                      pl.BlockSpec((1, 1, D_MODEL), lambda i, pos: (i // tiles_per_batch, 0, 0)),
                      pl.BlockSpec((1, D_MODEL), lambda i, pos: (0, 0))],
            out_specs=pl.BlockSpec((tm, D_MODEL), lambda i, pos: (i, 0)),
            scratch_shapes=[ybuf] * GATHER_BUFS + [pltpu.SemaphoreType.DMA((GATHER_BUFS,))]),
        out_shape=jax.ShapeDtypeStruct((t, D_MODEL), F32),
        compiler_params=_cparams(("arbitrary",)),
        name="combine",
    )(pos, y_pad, h, ew, g2, nf)


def _dispatch_indices(ids, ranks, counts, t):
    a = t * TOP_K
    nblk = a // MOE_BLOCK + N_EXPERTS + GATHER_AHEAD
    experts = jnp.arange(N_EXPERTS, dtype=jnp.int32)
    starts = jnp.cumsum(counts) - counts
    padded = (counts + MOE_BLOCK - 1) // MOE_BLOCK * MOE_BLOCK
    pends = jnp.cumsum(padded)
    pstarts = pends - padded
    block_start = jnp.arange(nblk, dtype=jnp.int32) * MOE_BLOCK
    block_e = jnp.minimum(jnp.sum((block_start[:, None] >= pends[None, :]).astype(jnp.int32), axis=1),
                          N_EXPERTS - 1)
    of_block = (block_e[:, None] == experts[None, :]).astype(jnp.int32)
    src0 = block_start + jnp.sum(of_block * (starts - pstarts)[None, :], axis=1)
    nused = pends[-1:] // MOE_BLOCK
    tok = jnp.arange(t, dtype=jnp.int32)[None, :]
    slot = jnp.arange(TOP_K, dtype=jnp.int32)[:, None]
    key = ids * a + tok * TOP_K + slot
    stok = (jnp.sort(key.reshape(a)) % a) // TOP_K
    pstart_of = jnp.zeros_like(ids)
    for e in range(N_EXPERTS):
        pstart_of = jnp.where(ids == e, pstarts[e], pstart_of)
    pos = (ranks + pstart_of).reshape(a)
    i32 = lambda v: v.astype(jnp.int32)
    return i32(block_e), i32(src0), i32(stok), i32(nused), i32(pos), nblk


def kernel(x, c, ctx, c_ctx, w_ada, b_ada, norm_mix, norm_ffn, w_in, gla_lr_w, gla_lr_b, gla_norm,
           w_branch_gla, w_branch_ret, w_out, w_router_group, b_router_group, w_router_expert,
           b_router_expert, w_expert_gate, w_expert_up, w_expert_down, norm_final):
    batch, seq, d = x.shape
    ctx_len = ctx.shape[1]
    assert d == D_MODEL and w_ada.shape[0] == 1, "single-layer block with D_MODEL features"
    t = batch * seq

    c8 = jnp.zeros((8, d), F32).at[:batch].set(c).at[batch].set(c_ctx)
    mod = _ada(c8, w_ada[0], b_ada[0][None, :])
    sh1, sc1, g1, sh2, sc2, g2 = [mod[:, d * i:d * (i + 1)] for i in range(6)]
    lat = lambda m: m[:batch, None, :]
    cx = lambda m: m[batch:batch + 1, None, :]

    w = w_in[0]
    lr0 = COL_GG + HEADS * DV
    w_main = jnp.concatenate([w[:, :lr0], w[:, lr0 + 2 * GLA_RANK:]], axis=1).astype(BF16)
    w_lr = jnp.pad(w[:, lr0:lr0 + 2 * GLA_RANK], ((0, 0), (0, LANES - 2 * GLA_RANK))).astype(BF16)
    nm = norm_mix[0][None, :]

    p_ctx, lr_ctx = _proj(ctx.reshape(batch * ctx_len, d), cx(sh1), cx(sc1), nm, w_main, w_lr,
                          tm=ctx_len, rows_per_mod=batch * ctx_len)
    x2 = x.reshape(t, d)
    p_lat, lr_lat = _proj(x2, lat(sh1), lat(sc1), nm, w_main, w_lr, tm=1024, rows_per_mod=seq)

    wlr_f = jnp.zeros((LANES, HEADS * DK), F32).at[:GLA_RANK].set(gla_lr_w[0, 0])
    wlr_b = jnp.zeros((LANES, HEADS * DK), F32).at[GLA_RANK:2 * GLA_RANK].set(gla_lr_w[0, 1])
    blr_f, blr_b = gla_lr_b[0, 0][None, :], gla_lr_b[0, 1][None, :]
    zero_state = jnp.zeros((batch, HEADS, DK, DV), F32)

    _, _, gs_f, gs_b = _gla(p_ctx, lr_ctx, wlr_f, blr_f, wlr_b, blr_b, zero_state, zero_state,
                            batch, ctx_len, tb=ctx_len)
    gla_f, gla_b, _, _ = _gla(p_lat, lr_lat, wlr_f, blr_f, wlr_b, blr_b, gs_f, gs_b, batch, seq, tb=512)

    ones = jnp.ones((ctx_len, DK), F32)
    _, _, rs_f, rs_b = _ret(p_ctx, ones, jnp.zeros_like(ones), zero_state, zero_state, batch, ctx_len, tb=ctx_len)
    cos, sin = _rope_tables(seq)
    ret_f, ret_b, _, _ = _ret(p_lat, cos, sin, rs_f, rs_b, batch, seq, tb=512)

    wrt = jnp.zeros((d, ROUTER_W), F32)
    wrt = wrt.at[:, :N_GROUPS].set(w_router_group[0]).at[:, EXP_ROW0:EXP_ROW0 + N_EXPERTS].set(w_router_expert[0])
    wrt_hi = wrt.astype(BF16)
    wrt = jnp.concatenate([wrt_hi, (wrt - wrt_hi.astype(F32)).astype(BF16)], axis=1)
    brt = jnp.zeros((1, ROUTER_W), F32)
    brt = brt.at[0, :N_GROUPS].set(b_router_group[0]).at[0, EXP_ROW0:EXP_ROW0 + N_EXPERTS].set(b_router_expert[0])
    h, hn, ids8, ew8, cnt = _merge(gla_f, gla_b, ret_f, ret_b, p_lat, x2, lat(g1), lat(sh2), lat(sc2),
                                   jnp.tile(gla_norm[0], HEADS)[None, :], norm_ffn[0][None, :],
                                   w_branch_gla[0].astype(BF16), w_branch_ret[0].astype(BF16),
                                   w_out[0].astype(BF16), wrt, brt, seq, tm=256)

    block_e, src0, stok, nused, pos, nblk = _dispatch_indices(ids8[:TOP_K], ids8[TOP_K:2 * TOP_K], cnt[:, 0], t)
    y_pad = _experts(block_e, src0, stok, nused, hn, w_expert_gate[0], w_expert_up[0], w_expert_down[0], nblk)
    out = _combine(pos, y_pad, h, ew8, lat(g2), norm_final[None, :], seq)
    return out.reshape(batch, seq, d)
```

```python
import functools

import numpy as np
import jax
import jax.numpy as jnp
from jax import lax
from jax.experimental import pallas as pl
from jax.experimental.pallas import tpu as pltpu

F32 = jnp.float32
BF16 = jnp.bfloat16

D_MODEL = 1024
GRID_W = 64
HEADS = 4
DK = 128
DV = 256
GLA_RANK = 16
GLA_TAU = 16.0
GLA_CHUNK = 64
GLA_SUB = 16
RET_CHUNK = 128
ROPE_BASE = 10000.0
N_GROUPS = 4
EXPERTS_PER_GROUP = 8
N_EXPERTS = N_GROUPS * EXPERTS_PER_GROUP
TOP_K = 2
EXPERT_FF = 256
MOE_BLOCK = 128
NORM_EPS = 1e-6

COL_GQ, COL_GK, COL_GV, COL_GG = 0, 512, 1024, 2048
COL_RQ, COL_RK, COL_RV, COL_RG = 3072, 3584, 4096, 5120
COL_MG, COL_MR = 6144, 7168
PROJ_W = 8192
LANES = 128
FEAT_TILES = D_MODEL // LANES
ROUTER_W = 128
EXP_ROW0 = 8

VMEM_LIMIT = 56 * 1024 * 1024

PROJ_TM = 1024
PROJ_TN = 1024
SCAN_TB = 1024
MERGE_TM = 256
COMBINE_TM = 256


def _cparams(sem):
    return pltpu.CompilerParams(dimension_semantics=sem, vmem_limit_bytes=VMEM_LIMIT)


def _sigmoid(x):
    return 1.0 / (1.0 + jnp.exp(-x))


def _silu(x):
    return x * _sigmoid(x)


def _dot(a, b):
    return jnp.dot(a, b, preferred_element_type=F32)


def _dot_nt(a, b):
    return lax.dot_general(a, b, (((1,), (1,)), ((), ())), preferred_element_type=F32)


def _ada_body(c_ref, w_ref, b_ref, o_ref):
    s = _silu(c_ref[...])
    o_ref[...] = _dot(s.astype(BF16), w_ref[...].astype(BF16)) + b_ref[...]


def _ada(c8, w, b):
    n = w.shape[1]
    tn = 1536
    return pl.pallas_call(
        _ada_body,
        grid=(n // tn,),
        in_specs=[pl.BlockSpec((8, D_MODEL), lambda j: (0, 0)),
                  pl.BlockSpec((D_MODEL, tn), lambda j: (0, j)),
                  pl.BlockSpec((1, tn), lambda j: (0, j))],
        out_specs=pl.BlockSpec((8, tn), lambda j: (0, j)),
        out_shape=jax.ShapeDtypeStruct((8, n), F32),
        compiler_params=_cparams(("arbitrary",)),
        name="ada",
    )(c8, w, b)


def _proj_body(x_ref, sh_ref, sc_ref, g_ref, wa_ref, wb_ref, wlr_ref, o_ref, lr_ref, h_ref, *, tiles_a):
    j = pl.program_id(1)

    @pl.when(j == 0)
    def _():
        x = x_ref[...]
        ms = jnp.mean(x * x, axis=-1, keepdims=True)
        y = x * lax.rsqrt(ms + NORM_EPS) * g_ref[...]
        hb = (y * (1.0 + sc_ref[0]) + sh_ref[0]).astype(BF16)
        h_ref[...] = hb
        lr_ref[...] = _dot(hb, wlr_ref[...])

    @pl.when(j < tiles_a)
    def _():
        o_ref[...] = _dot(h_ref[...], wa_ref[...]).astype(BF16)

    @pl.when(j >= tiles_a)
    def _():
        o_ref[...] = _dot(h_ref[...], wb_ref[...]).astype(BF16)


def _proj(x2, sh, sc, g, w_a, w_b, w_lr, tm, rows_per_mod):
    t = x2.shape[0]
    tn = PROJ_TN
    tiles_a, tiles_b = w_a.shape[1] // tn, w_b.shape[1] // tn
    tiles_per_mod = rows_per_mod // tm
    mod_map = lambda i, j: (i // tiles_per_mod, 0, 0)
    return pl.pallas_call(
        functools.partial(_proj_body, tiles_a=tiles_a),
        grid=(t // tm, tiles_a + tiles_b),
        in_specs=[pl.BlockSpec((tm, D_MODEL), lambda i, j: (i, 0)),
                  pl.BlockSpec((1, 1, D_MODEL), mod_map),
                  pl.BlockSpec((1, 1, D_MODEL), mod_map),
                  pl.BlockSpec((1, D_MODEL), lambda i, j: (0, 0)),
                  pl.BlockSpec((D_MODEL, tn), lambda i, j: (0, jnp.minimum(j, tiles_a - 1))),
                  pl.BlockSpec((D_MODEL, tn), lambda i, j: (0, jnp.maximum(j - tiles_a, 0))),
                  pl.BlockSpec((D_MODEL, LANES), lambda i, j: (0, 0))],
        out_specs=[pl.BlockSpec((tm, tn), lambda i, j: (i, j)),
                   pl.BlockSpec((tm, LANES), lambda i, j: (i, 0))],
        out_shape=[jax.ShapeDtypeStruct((t, PROJ_W), BF16),
                   jax.ShapeDtypeStruct((t, LANES), F32)],
        scratch_shapes=[pltpu.VMEM((tm, D_MODEL), BF16)],
        compiler_params=_cparams(("parallel", "arbitrary")),
        name="proj",
    )(x2, sh, sc, g, w_a, w_b, w_lr)


def _gla_consts():
    c, s = GLA_CHUNK, GLA_SUB
    msk = {}
    for rev in (False, True):
        cols = []
        for i in range(c // s):
            keys = np.arange(c)[_key_rows(i, rev)][None, :]
            t = np.arange(c)[:, None]
            visible = (keys > t) if rev else (keys <= t)
            cols.append(np.where(t // s == i, visible, False))
        msk[rev] = np.concatenate(cols, axis=1).astype(np.float32)
    return msk


def _key_rows(i, rev):
    return slice(GLA_SUB * i, GLA_CHUNK) if rev else slice(0, GLA_SUB * (i + 1))


def _log_gate(lr, w, b):
    z = _dot(lr.astype(BF16), w.astype(BF16)) + b
    return -(jnp.maximum(-z, 0.0) + jnp.log(1.0 + jnp.exp(-jnp.abs(z)))) * (1.0 / GLA_TAU)


def _subchunk_scan(la, rev):
    c = la.shape[0]
    pos = lax.broadcasted_iota(jnp.int32, la.shape, 0) % GLA_SUB
    w = la
    step = 1
    while step < GLA_SUB:
        if rev:
            w = w + jnp.where(pos < GLA_SUB - step, pltpu.roll(w, c - step, 0), 0.0)
        else:
            w = w + jnp.where(pos >= step, pltpu.roll(w, step, 0), 0.0)
        step *= 2
    return w


def _gla_chunk(q, k, v, la, state, mask, rev):
    c, s = GLA_CHUNK, GLA_SUB
    nsub = c // s
    w = _subchunk_scan(la, rev)
    tot = [w[s * i:s * i + 1] if rev else w[s * (i + 1) - 1:s * (i + 1)] for i in range(nsub)]
    anchors = [None] * nsub
    acc = jnp.zeros_like(tot[0])
    for i in (reversed(range(nsub)) if rev else range(nsub)):
        anchors[i] = acc
        acc = acc + tot[i]
    b_end = acc
    ref = jnp.concatenate([jnp.broadcast_to(a, (s, DK)) for a in anchors], axis=0)
    b = w + ref

    qt = q.astype(F32) * (DK ** -0.5) * jnp.exp(w)
    kf = k.astype(F32)
    inter = _dot((qt * jnp.exp(ref)).astype(BF16), state.astype(BF16))

    kstack = jnp.concatenate([kf[_key_rows(i, rev)] * jnp.exp(anchors[i] - b[_key_rows(i, rev)])
                              for i in range(nsub)], axis=0).astype(BF16)
    vstack = jnp.concatenate([v[_key_rows(i, rev)] for i in range(nsub)], axis=0)
    scores = jnp.where(mask != 0.0, _dot_nt(qt.astype(BF16), kstack), 0.0)
    out = inter + _dot(scores.astype(BF16), vstack)

    kst = kf * jnp.exp(b_end - b)
    xt = jnp.concatenate([kst, jnp.broadcast_to(jnp.exp(b_end), (c, DK))], axis=0).T
    new_state = xt[:, c:c + 1] * state + _dot(xt[:, :c].astype(BF16), v)
    return out, new_state


def _gla_body(qf_ref, kf_ref, vf_ref, lrf_ref, qb_ref, kb_ref, vb_ref, lrb_ref,
              wf_ref, bf_ref, wb_ref, bb_ref, mskf_ref, mskb_ref,
              s0f_ref, s0b_ref, of_ref, ob_ref, sf_ref, sb_ref, *, nchunks):
    @pl.when(pl.program_id(2) == 0)
    def _():
        sf_ref[...] = s0f_ref[...]
        sb_ref[...] = s0b_ref[...]

    c = GLA_CHUNK
    la_f = _log_gate(lrf_ref[...], wf_ref[...], bf_ref[...])
    la_b = _log_gate(lrb_ref[...], wb_ref[...], bb_ref[...])
    mskf, mskb = mskf_ref[...], mskb_ref[...]
    st_f = sf_ref[0, 0]
    st_b = sb_ref[0, 0]
    for n in range(nchunks):
        rf = slice(c * n, c * (n + 1))
        o, st_f = _gla_chunk(qf_ref[rf, :], kf_ref[rf, :], vf_ref[rf, :], la_f[rf, :], st_f, mskf, False)
        of_ref[rf, :] = o.astype(BF16)
        m = nchunks - 1 - n
        rb = slice(c * m, c * (m + 1))
        o, st_b = _gla_chunk(qb_ref[rb, :], kb_ref[rb, :], vb_ref[rb, :], la_b[rb, :], st_b, mskb, True)
        ob_ref[rb, :] = o.astype(BF16)
    sf_ref[0, 0] = st_f
    sb_ref[0, 0] = st_b


def _gla(p, lr, wlr_f, blr_f, wlr_b, blr_b, s0f, s0b, batch, seq, tb):
    nb = seq // tb
    msk = _gla_consts()
    mskf, mskb = jnp.asarray(msk[False]), jnp.asarray(msk[True])

    def fwd(col0, width):
        return lambda b, h, i: (b * nb + i, col0 // width + h)

    def bwd(col0, width):
        return lambda b, h, i: (b * nb + nb - 1 - i, col0 // width + h)

    const2 = lambda b, h, i: (0, 0)
    headcol = lambda b, h, i: (0, h)
    st_map = lambda b, h, i: (b, h, 0, 0)
    st_spec = pl.BlockSpec((1, 1, DK, DV), st_map)
    st_shape = jax.ShapeDtypeStruct((batch, HEADS, DK, DV), F32)
    o_shape = jax.ShapeDtypeStruct((batch * seq, HEADS * DV), BF16)
    return pl.pallas_call(
        functools.partial(_gla_body, nchunks=tb // GLA_CHUNK),
        grid=(batch, HEADS, nb),
        in_specs=[pl.BlockSpec((tb, DK), fwd(COL_GQ, DK)),
                  pl.BlockSpec((tb, DK), fwd(COL_GK, DK)),
                  pl.BlockSpec((tb, DV), fwd(COL_GV, DV)),
                  pl.BlockSpec((tb, LANES), lambda b, h, i: (b * nb + i, 0)),
                  pl.BlockSpec((tb, DK), bwd(COL_GQ, DK)),
                  pl.BlockSpec((tb, DK), bwd(COL_GK, DK)),
                  pl.BlockSpec((tb, DV), bwd(COL_GV, DV)),
                  pl.BlockSpec((tb, LANES), lambda b, h, i: (b * nb + nb - 1 - i, 0)),
                  pl.BlockSpec((LANES, DK), headcol),
                  pl.BlockSpec((1, DK), headcol),
                  pl.BlockSpec((LANES, DK), headcol),
                  pl.BlockSpec((1, DK), headcol),
                  pl.BlockSpec(mskf.shape, const2),
                  pl.BlockSpec(mskb.shape, const2),
                  st_spec, st_spec],
        out_specs=[pl.BlockSpec((tb, DV), lambda b, h, i: (b * nb + i, h)),
                   pl.BlockSpec((tb, DV), lambda b, h, i: (b * nb + nb - 1 - i, h)),
                   st_spec, st_spec],
        out_shape=[o_shape, o_shape, st_shape, st_shape],
        compiler_params=_cparams(("parallel", "parallel", "arbitrary")),
        name="gla",
    )(p, p, p, lr, p, p, p, lr, wlr_f, blr_f, wlr_b, blr_b, mskf, mskb, s0f, s0b)


def _ret_consts():
    c = RET_CHUNK
    hh = np.arange(HEADS, dtype=np.float64)
    lg = {False: np.log1p(-np.exp2(-5.0 - hh)), True: np.log1p(-np.exp2(-5.5 - hh))}
    t = np.arange(c, dtype=np.float64)[:, None]
    u = np.arange(c, dtype=np.float64)[None, :]
    pos = np.arange(c, dtype=np.float64)
    out = {}
    for rev in (False, True):
        g = lg[rev][:, None, None]
        if not rev:
            dmat = np.where(u <= t, np.exp((t - u) * g), 0.0)
            qd = np.exp((pos + 1.0)[None, :] * lg[rev][:, None])
            kd = np.exp((c - 1.0 - pos)[None, :] * lg[rev][:, None])
        else:
            dmat = np.where(u > t, np.exp((u - t) * g), 0.0)
            qd = np.exp((c - pos)[None, :] * lg[rev][:, None])
            kd = np.exp(pos[None, :] * lg[rev][:, None])
        cd = np.exp(c * lg[rev])
        out[rev] = (dmat.astype(np.float32),
                    np.repeat(qd[:, :, None], DV, axis=2).astype(np.float32),
                    np.repeat(kd[:, :, None], DK, axis=2).astype(np.float32),
                    np.repeat(cd[:, None, None], DV, axis=2).astype(np.float32))
    return out


def _rope(x, cos, sin):
    lane = lax.broadcasted_iota(jnp.int32, x.shape, 1)
    partner = jnp.where((lane % 64) < 32, pltpu.roll(x, DK - 32, 1), pltpu.roll(x, 32, 1))
    return x * cos + partner * sin


def _ret_chunk(q, k, v, cos, sin, state, dmat, qd, kd, cd):
    qr = _rope(q.astype(F32), cos, sin).astype(BF16)
    kr = _rope(k.astype(F32) * (DK ** -0.5), cos, sin)
    inter = _dot(qr, state.astype(BF16)) * qd
    scores = _dot_nt(qr, kr.astype(BF16)) * dmat
    out = inter + _dot(scores.astype(BF16), v)
    new_state = cd * state + _dot((kr * kd).T.astype(BF16), v)
    return out, new_state


def _ret_body(qf_ref, kf_ref, vf_ref, cosf_ref, sinf_ref, qb_ref, kb_ref, vb_ref, cosb_ref, sinb_ref,
              dmf_ref, qdf_ref, kdf_ref, cdf_ref, dmb_ref, qdb_ref, kdb_ref, cdb_ref,
              s0f_ref, s0b_ref, of_ref, ob_ref, sf_ref, sb_ref, *, nchunks):
    @pl.when(pl.program_id(2) == 0)
    def _():
        sf_ref[...] = s0f_ref[...]
        sb_ref[...] = s0b_ref[...]

    c = RET_CHUNK
    cf = (dmf_ref[0], qdf_ref[0], kdf_ref[0], cdf_ref[0])
    cb = (dmb_ref[0], qdb_ref[0], kdb_ref[0], cdb_ref[0])
    st_f = sf_ref[0, 0]
    st_b = sb_ref[0, 0]
    for n in range(nchunks):
        rf = slice(c * n, c * (n + 1))
        o, st_f = _ret_chunk(qf_ref[rf, :], kf_ref[rf, :], vf_ref[rf, :], cosf_ref[rf, :], sinf_ref[rf, :],
                             st_f, *cf)
        of_ref[rf, :] = o.astype(BF16)
        m = nchunks - 1 - n
        rb = slice(c * m, c * (m + 1))
        o, st_b = _ret_chunk(qb_ref[rb, :], kb_ref[rb, :], vb_ref[rb, :], cosb_ref[rb, :], sinb_ref[rb, :],
                             st_b, *cb)
        ob_ref[rb, :] = o.astype(BF16)
    sf_ref[0, 0] = st_f
    sb_ref[0, 0] = st_b


def _ret(p, cos, sin, s0f, s0b, batch, seq, tb):
    nb = seq // tb
    consts = _ret_consts()

    def fwd(col0, width):
        return lambda b, h, i: (b * nb + i, col0 // width + h)

    def bwd(col0, width):
        return lambda b, h, i: (b * nb + nb - 1 - i, col0 // width + h)

    head3 = lambda b, h, i: (h, 0, 0)
    st_spec = pl.BlockSpec((1, 1, DK, DV), lambda b, h, i: (b, h, 0, 0))
    st_shape = jax.ShapeDtypeStruct((batch, HEADS, DK, DV), F32)
    o_shape = jax.ShapeDtypeStruct((batch * seq, HEADS * DV), BF16)
    const_specs = []
    const_args = []
    for rev in (False, True):
        for a in consts[rev]:
            const_specs.append(pl.BlockSpec((1,) + a.shape[1:], head3))
            const_args.append(jnp.asarray(a))
    return pl.pallas_call(
        functools.partial(_ret_body, nchunks=tb // RET_CHUNK),
        grid=(batch, HEADS, nb),
        in_specs=[pl.BlockSpec((tb, DK), fwd(COL_RQ, DK)),
                  pl.BlockSpec((tb, DK), fwd(COL_RK, DK)),
                  pl.BlockSpec((tb, DV), fwd(COL_RV, DV)),
                  pl.BlockSpec((tb, DK), lambda b, h, i: (i, 0)),
                  pl.BlockSpec((tb, DK), lambda b, h, i: (i, 0)),
                  pl.BlockSpec((tb, DK), bwd(COL_RQ, DK)),
                  pl.BlockSpec((tb, DK), bwd(COL_RK, DK)),
                  pl.BlockSpec((tb, DV), bwd(COL_RV, DV)),
                  pl.BlockSpec((tb, DK), lambda b, h, i: (nb - 1 - i, 0)),
                  pl.BlockSpec((tb, DK), lambda b, h, i: (nb - 1 - i, 0))]
                 + const_specs + [st_spec, st_spec],
        out_specs=[pl.BlockSpec((tb, DV), lambda b, h, i: (b * nb + i, h)),
                   pl.BlockSpec((tb, DV), lambda b, h, i: (b * nb + nb - 1 - i, h)),
                   st_spec, st_spec],
        out_shape=[o_shape, o_shape, st_shape, st_shape],
        compiler_params=_cparams(("parallel", "parallel", "arbitrary")),
        name="ret",
    )(p, p, p, cos, sin, p, p, p, cos, sin, *const_args, s0f, s0b)


def _rope_tables(seq):
    n = DK // 4
    inv = np.float32(ROPE_BASE) ** (-np.arange(n, dtype=np.float32) / np.float32(n))
    pos = np.arange(seq)
    ar = (pos // GRID_W).astype(np.float32)[:, None] * inv[None, :]
    ac = (pos % GRID_W).astype(np.float32)[:, None] * inv[None, :]
    cos = np.concatenate([np.cos(ar), np.cos(ar), np.cos(ac), np.cos(ac)], axis=1)
    sin = np.concatenate([-np.sin(ar), np.sin(ar), -np.sin(ac), np.sin(ac)], axis=1)
    return jnp.asarray(cos, F32), jnp.asarray(sin, F32)


def _route(logits_t):
    g = [logits_t[i:i + 1] for i in range(N_GROUPS)]
    gmax = jnp.maximum(jnp.maximum(g[0], g[1]), jnp.maximum(g[2], g[3]))
    gsel = jnp.where(g[0] == gmax, 0, jnp.where(g[1] == gmax, 1, jnp.where(g[2] == gmax, 2, 3)))
    gsum = (jnp.exp(g[0] - gmax) + jnp.exp(g[1] - gmax)) + (jnp.exp(g[2] - gmax) + jnp.exp(g[3] - gmax))
    gw = 1.0 / gsum
    e = [logits_t[EXP_ROW0 + EXPERTS_PER_GROUP * i:EXP_ROW0 + EXPERTS_PER_GROUP * (i + 1)]
         for i in range(N_GROUPS)]
    el = jnp.where(gsel == 0, e[0], jnp.where(gsel == 1, e[1], jnp.where(gsel == 2, e[2], e[3])))
    row = lax.broadcasted_iota(jnp.int32, el.shape, 0).astype(F32)
    none = float(EXPERTS_PER_GROUP)
    m1 = jnp.max(el, axis=0, keepdims=True)
    i1 = jnp.min(jnp.where(el == m1, row, none), axis=0, keepdims=True)
    el2 = jnp.where(row == i1, -jnp.inf, el)
    m2 = jnp.max(el2, axis=0, keepdims=True)
    i2 = jnp.min(jnp.where(el2 == m2, row, none), axis=0, keepdims=True)
    r = jnp.exp(m2 - m1)
    w1 = gw / (1.0 + r)
    w2 = gw * r / (1.0 + r)
    base = gsel * EXPERTS_PER_GROUP
    ids = jnp.concatenate([base + i1.astype(jnp.int32), base + i2.astype(jnp.int32)], axis=0)
    return ids, jnp.concatenate([w1, w2], axis=0)


def _store_token_tiled(ref, val):
    m = val.shape[0]
    for j in range(FEAT_TILES):
        ref[pl.ds(j, m, stride=FEAT_TILES), :] = val[:, LANES * j:LANES * (j + 1)]


def _load_token_tiled(ref, m):
    return jnp.concatenate([ref[pl.ds(j, m, stride=FEAT_TILES), :] for j in range(FEAT_TILES)], axis=1)


def _merge_body(gf_ref, gb_ref, rf_ref, rb_ref, gg_ref, rg_ref, mg_ref, mr_ref, x_ref,
                g1_ref, sh2_ref, sc2_ref, gn_ref, nf_ref, wg_ref, wr_ref, wo_ref, wrt_ref, brt_ref, upper_ref,
                h_ref, hn_ref, ids_ref, ew_ref, cnt_ref):
    @pl.when(pl.program_id(0) == 0)
    def _():
        cnt_ref[...] = jnp.zeros_like(cnt_ref)

    og = gf_ref[...].astype(F32) + gb_ref[...].astype(F32)
    orr = rf_ref[...].astype(F32) + rb_ref[...].astype(F32)
    gparts, rparts = [], []
    for hh in range(HEADS):
        seg = og[:, DV * hh:DV * (hh + 1)]
        ms = jnp.mean(seg * seg, axis=-1, keepdims=True)
        gparts.append(seg * lax.rsqrt(ms + NORM_EPS))
        seg = orr[:, DV * hh:DV * (hh + 1)]
        mu = jnp.mean(seg, axis=-1, keepdims=True)
        cen = seg - mu
        var = jnp.mean(cen * cen, axis=-1, keepdims=True)
        rparts.append(cen * lax.rsqrt(var + NORM_EPS))
    o_gla = jnp.concatenate(gparts, axis=1) * gn_ref[...] * _silu(gg_ref[...].astype(F32))
    o_ret = jnp.concatenate(rparts, axis=1) * _silu(rg_ref[...].astype(F32))
    y = (_sigmoid(mg_ref[...].astype(F32)) * _dot(o_gla.astype(BF16), wg_ref[...])
         + _sigmoid(mr_ref[...].astype(F32)) * _dot(o_ret.astype(BF16), wr_ref[...]))
    h = x_ref[...] + g1_ref[0] * _dot(y.astype(BF16), wo_ref[...])
    h_ref[...] = h
    ms = jnp.mean(h * h, axis=-1, keepdims=True)
    hn = h * lax.rsqrt(ms + NORM_EPS) * nf_ref[...] * (1.0 + sc2_ref[0]) + sh2_ref[0]
    _store_token_tiled(hn_ref, hn)
    hn_hi = hn.astype(BF16)
    hn_lo = (hn - hn_hi.astype(F32)).astype(BF16)
    both_w = _dot(hn_hi, wrt_ref[...])
    logits = (both_w[:, :ROUTER_W] + both_w[:, ROUTER_W:] + _dot(hn_lo, wrt_ref[:, :ROUTER_W])) + brt_ref[...]
    ids, ew = _route(logits.T)
    tm = ids.shape[1]

    erow = lax.broadcasted_iota(jnp.int32, (N_EXPERTS, tm), 0)
    oh0 = jnp.where(erow == ids[0:1], 1.0, 0.0)
    oh1 = jnp.where(erow == ids[1:2], 1.0, 0.0)
    both = oh0 + oh1
    before = _dot(both.astype(BF16), upper_ref[...]) + cnt_ref[:, 0:1].astype(F32)
    rank0 = jnp.sum(oh0 * before, axis=0, keepdims=True)
    rank1 = jnp.sum(oh1 * before, axis=0, keepdims=True)
    total = cnt_ref[:, 0:1] + jnp.sum(both, axis=1, keepdims=True).astype(jnp.int32)
    cnt_ref[...] = jnp.broadcast_to(total, cnt_ref.shape)
    ids_ref[...] = jnp.concatenate([ids, rank0.astype(jnp.int32), rank1.astype(jnp.int32),
                                    jnp.zeros((4, tm), jnp.int32)], axis=0)
    ew_ref[...] = jnp.concatenate([ew, jnp.zeros((6, tm), F32)], axis=0)


def _merge(gf, gb, rf, rb, p, x2, g1, sh2, sc2, gn, nf, wg, wr, wo, wrt, brt, seq, tm):
    t = x2.shape[0]
    tiles_per_batch = seq // tm
    row = lambda i: (i, 0)
    mod = lambda i: (i // tiles_per_batch, 0, 0)
    const = lambda i: (0, 0)
    tok = pl.BlockSpec((tm, D_MODEL), row)
    vec = pl.BlockSpec((1, D_MODEL), const)
    modspec = pl.BlockSpec((1, 1, D_MODEL), mod)
    wspec = pl.BlockSpec((D_MODEL, D_MODEL), const)
    upper = jnp.asarray(np.triu(np.ones((tm, tm), np.float32), 1), BF16)

    def pcol(col0):
        return pl.BlockSpec((tm, D_MODEL), lambda i: (i, col0 // D_MODEL))

    return pl.pallas_call(
        _merge_body,
        grid=(t // tm,),
        in_specs=[tok, tok, tok, tok, pcol(COL_GG), pcol(COL_RG), pcol(COL_MG), pcol(COL_MR), tok,
                  modspec, modspec, modspec, vec, vec, wspec, wspec, wspec,
                  pl.BlockSpec((D_MODEL, 2 * ROUTER_W), const), pl.BlockSpec((1, ROUTER_W), const),
                  pl.BlockSpec((tm, tm), const)],
        out_specs=[tok, pl.BlockSpec((tm * FEAT_TILES, LANES), row),
                   pl.BlockSpec((8, tm), lambda i: (0, i)), pl.BlockSpec((8, tm), lambda i: (0, i)),
                   pl.BlockSpec((N_EXPERTS, LANES), const)],
        out_shape=[jax.ShapeDtypeStruct((t, D_MODEL), F32), jax.ShapeDtypeStruct((t * FEAT_TILES, LANES), F32),
                   jax.ShapeDtypeStruct((8, t), jnp.int32), jax.ShapeDtypeStruct((8, t), F32),
                   jax.ShapeDtypeStruct((N_EXPERTS, LANES), jnp.int32)],
        compiler_params=_cparams(("arbitrary",)),
        name="merge",
    )(gf, gb, rf, rb, p, p, p, p, x2, g1, sh2, sc2, gn, nf, wg, wr, wo, wrt, brt, upper)


GATHER_UNROLL = 8


def _aligned(tok):
    off = tok * FEAT_TILES
    return off if isinstance(off, int) else pl.multiple_of(off, FEAT_TILES)


def _token_copy(src_hbm, dst, sem, src_tok, dst_tok):
    return pltpu.make_async_copy(src_hbm.at[pl.ds(_aligned(src_tok), FEAT_TILES), :],
                                 dst.at[pl.ds(_aligned(dst_tok), FEAT_TILES), :], sem)


def _wait_tokens(src_hbm, dst, sem):
    pltpu.make_async_copy(src_hbm.at[pl.ds(0, dst.shape[0]), :], dst, sem).wait()


GATHER_BUFS = 3
GATHER_AHEAD = GATHER_BUFS - 1


def _issue_rows(n, start_row, unrolled):
    if unrolled:
        for r in range(n):
            start_row(r, r % 2)
    else:
        def body(g, carry):
            for u in range(GATHER_UNROLL):
                start_row(g * GATHER_UNROLL + u, u % 2)
            return carry
        lax.fori_loop(0, n // GATHER_UNROLL, body, 0)


def _expert_body(be_ref, src0_ref, stok_ref, nused_ref, hn_hbm, wg_ref, wu_ref, wd_ref, y_ref, *scratch):
    i = pl.program_id(0)
    nused = nused_ref[0]
    bufs, sems = scratch[:GATHER_BUFS], scratch[GATHER_BUFS]
    last = stok_ref.shape[0] - 1

    def issue(block, s, unrolled):
        base = src0_ref[block]

        def start_row(r, priority):
            tok = stok_ref[jnp.minimum(base + r, last)]
            _token_copy(hn_hbm, bufs[s], sems.at[s], tok, r).start(priority=priority)
        _issue_rows(MOE_BLOCK, start_row, unrolled)

    @pl.when(i == 0)
    def _():
        for b in range(GATHER_AHEAD):
            issue(b, b, False)

    for cur in range(GATHER_BUFS):
        mine = i % GATHER_BUFS == cur

        @pl.when(jnp.logical_and(mine, i < nused))
        def _():
            _wait_tokens(hn_hbm, bufs[cur], sems.at[cur])
            issue(i + GATHER_AHEAD, (cur + GATHER_AHEAD) % GATHER_BUFS, True)
            xb = _load_token_tiled(bufs[cur], MOE_BLOCK).astype(BF16)
            gate = _dot(xb, wg_ref[0].astype(BF16))
            up = _dot(xb, wu_ref[0].astype(BF16))
            hid = (_silu(gate) * up).astype(BF16)
            _store_token_tiled(y_ref, _dot(hid, wd_ref[0].astype(BF16)))

        @pl.when(jnp.logical_and(mine, jnp.logical_and(i >= nused, i < nused + GATHER_AHEAD)))
        def _():
            _wait_tokens(hn_hbm, bufs[cur], sems.at[cur])

    @pl.when(i >= nused)
    def _():
        y_ref[...] = jnp.zeros_like(y_ref)


def _experts(block_e, src0, stok, nused, hn, w_gate, w_up, w_down, nblk):
    wmap = lambda i, be, s0, st, nu: (be[i], 0, 0)
    rows = MOE_BLOCK * FEAT_TILES
    return pl.pallas_call(
        _expert_body,
        grid_spec=pltpu.PrefetchScalarGridSpec(
            num_scalar_prefetch=4,
            grid=(nblk,),
            in_specs=[pl.BlockSpec(memory_space=pl.ANY),
                      pl.BlockSpec((1, D_MODEL, EXPERT_FF), wmap),
                      pl.BlockSpec((1, D_MODEL, EXPERT_FF), wmap),
                      pl.BlockSpec((1, EXPERT_FF, D_MODEL), wmap)],
            out_specs=pl.BlockSpec((rows, LANES), lambda i, be, s0, st, nu: (i, 0)),
            scratch_shapes=[pltpu.VMEM((rows, LANES), F32)] * GATHER_BUFS
                           + [pltpu.SemaphoreType.DMA((GATHER_BUFS,))]),
        out_shape=jax.ShapeDtypeStruct((nblk * rows, LANES), F32),
        compiler_params=_cparams(("arbitrary",)),
        name="experts",
    )(block_e, src0, stok, nused, hn, w_gate, w_up, w_down)


def _combine_body(pos_ref, y_hbm, h_ref, ew_ref, g2_ref, nf_ref, o_ref, *scratch):
    i = pl.program_id(0)
    n = pl.num_programs(0)
    tm = COMBINE_TM
    t = pos_ref.shape[0] // TOP_K
    bufs, sems = scratch[:GATHER_BUFS], scratch[GATHER_BUFS]

    def issue(tile, s, unrolled):
        def start_row(j, priority):
            k, r = j % TOP_K, j // TOP_K
            _token_copy(y_hbm, bufs[s].at[k], sems.at[s], pos_ref[k * t + tile * tm + r], r).start(priority=priority)
        _issue_rows(tm * TOP_K, start_row, unrolled)

    def wait(s):
        for k in range(TOP_K):
            _wait_tokens(y_hbm, bufs[s].at[k], sems.at[s])

    @pl.when(i == 0)
    def _():
        for b in range(GATHER_AHEAD):
            issue(b, b, False)

    for cur in range(GATHER_BUFS):
        @pl.when(i % GATHER_BUFS == cur)
        def _():
            wait(cur)
            issue(jnp.minimum(i + GATHER_AHEAD, n - 1), (cur + GATHER_AHEAD) % GATHER_BUFS, True)
            wt = jnp.concatenate([ew_ref[...]] * (LANES // 8), axis=0).T
            moe = (wt[:, 0:1] * _load_token_tiled(bufs[cur].at[0], tm)
                   + wt[:, 1:2] * _load_token_tiled(bufs[cur].at[1], tm))
            h = h_ref[...] + g2_ref[0] * moe
            ms = jnp.mean(h * h, axis=-1, keepdims=True)
            o_ref[...] = h * lax.rsqrt(ms + NORM_EPS) * nf_ref[...]

        @pl.when(jnp.logical_and(i % GATHER_BUFS == cur, i == n - 1))
        def _():
            for ahead in range(1, GATHER_BUFS):
                wait((cur + ahead) % GATHER_BUFS)


def _combine(pos, y_pad, h, ew, g2, nf, seq):
    t = h.shape[0]
    tm = COMBINE_TM
    tiles_per_batch = seq // tm
    ybuf = pltpu.VMEM((TOP_K, tm * FEAT_TILES, LANES), F32)
    return pl.pallas_call(
        _combine_body,
        grid_spec=pltpu.PrefetchScalarGridSpec(
            num_scalar_prefetch=1,
            grid=(t // tm,),
            in_specs=[pl.BlockSpec(memory_space=pl.ANY),
                      pl.BlockSpec((tm, D_MODEL), lambda i, pos: (i, 0)),
                      pl.BlockSpec((8, tm), lambda i, pos: (0, i)),
                      pl.BlockSpec((1, 1, D_MODEL), lambda i, pos: (i // tiles_per_batch, 0, 0)),
                      pl.BlockSpec((1, D_MODEL), lambda i, pos: (0, 0))],
            out_specs=pl.BlockSpec((tm, D_MODEL), lambda i, pos: (i, 0)),
            scratch_shapes=[ybuf] * GATHER_BUFS + [pltpu.SemaphoreType.DMA((GATHER_BUFS,))]),
        out_shape=jax.ShapeDtypeStruct((t, D_MODEL), F32),
        compiler_params=_cparams(("arbitrary",)),
        name="combine",
    )(pos, y_pad, h, ew, g2, nf)


def _dispatch_indices(ids, ranks, counts, t):
    a = t * TOP_K
    nblk = a // MOE_BLOCK + N_EXPERTS + GATHER_AHEAD
    experts = jnp.arange(N_EXPERTS, dtype=jnp.int32)
    starts = jnp.cumsum(counts) - counts
    padded = (counts + MOE_BLOCK - 1) // MOE_BLOCK * MOE_BLOCK
    pends = jnp.cumsum(padded)
    pstarts = pends - padded
    block_start = jnp.arange(nblk, dtype=jnp.int32) * MOE_BLOCK
    block_e = jnp.minimum(jnp.sum((block_start[:, None] >= pends[None, :]).astype(jnp.int32), axis=1),
                          N_EXPERTS - 1)
    of_block = (block_e[:, None] == experts[None, :]).astype(jnp.int32)
    src0 = block_start + jnp.sum(of_block * (starts - pstarts)[None, :], axis=1)
    nused = pends[-1:] // MOE_BLOCK
    tok = jnp.arange(t, dtype=jnp.int32)[None, :]
    slot = jnp.arange(TOP_K, dtype=jnp.int32)[:, None]
    key = ids * a + tok * TOP_K + slot
    stok = (jnp.sort(key.reshape(a)) % a) // TOP_K
    ids_d = ids.reshape(a // LANES, LANES)
    pstart_of = jnp.zeros_like(ids_d)
    for e in range(N_EXPERTS):
        pstart_of = jnp.where(ids_d == e, pstarts[e], pstart_of)
    pos = (ranks.reshape(a // LANES, LANES) + pstart_of).reshape(a)
    i32 = lambda v: v.astype(jnp.int32)
    return i32(block_e), i32(src0), i32(stok), i32(nused), i32(pos), nblk


def kernel(x, c, ctx, c_ctx, w_ada, b_ada, norm_mix, norm_ffn, w_in, gla_lr_w, gla_lr_b, gla_norm,
           w_branch_gla, w_branch_ret, w_out, w_router_group, b_router_group, w_router_expert,
           b_router_expert, w_expert_gate, w_expert_up, w_expert_down, norm_final):
    batch, seq, d = x.shape
    ctx_len = ctx.shape[1]
    assert d == D_MODEL and w_ada.shape[0] == 1, "single-layer block with D_MODEL features"
    t = batch * seq

    c8 = jnp.zeros((8, d), F32).at[:batch].set(c).at[batch].set(c_ctx)
    mod = _ada(c8, w_ada[0], b_ada[0][None, :])
    sh1, sc1, g1, sh2, sc2, g2 = [mod[:, d * i:d * (i + 1)] for i in range(6)]
    lat = lambda m: m[:batch, None, :]
    cx = lambda m: m[batch:batch + 1, None, :]

    w = w_in[0]
    lr0 = COL_GG + HEADS * DV
    w_a = w[:, :lr0].astype(BF16)
    w_b = w[:, lr0 + 2 * GLA_RANK:].astype(BF16)
    w_lr = jnp.pad(w[:, lr0:lr0 + 2 * GLA_RANK], ((0, 0), (0, LANES - 2 * GLA_RANK))).astype(BF16)
    nm = norm_mix[0][None, :]

    p_ctx, lr_ctx = _proj(ctx.reshape(batch * ctx_len, d), cx(sh1), cx(sc1), nm, w_a, w_b, w_lr,
                          tm=ctx_len, rows_per_mod=batch * ctx_len)
    x2 = x.reshape(t, d)
    p_lat, lr_lat = _proj(x2, lat(sh1), lat(sc1), nm, w_a, w_b, w_lr, tm=PROJ_TM, rows_per_mod=seq)

    wlr_f = jnp.zeros((LANES, HEADS * DK), F32).at[:GLA_RANK].set(gla_lr_w[0, 0])
    wlr_b = jnp.zeros((LANES, HEADS * DK), F32).at[GLA_RANK:2 * GLA_RANK].set(gla_lr_w[0, 1])
    blr_f, blr_b = gla_lr_b[0, 0][None, :], gla_lr_b[0, 1][None, :]
    zero_state = jnp.zeros((batch, HEADS, DK, DV), F32)

    _, _, gs_f, gs_b = _gla(p_ctx, lr_ctx, wlr_f, blr_f, wlr_b, blr_b, zero_state, zero_state,
                            batch, ctx_len, tb=ctx_len)
    gla_f, gla_b, _, _ = _gla(p_lat, lr_lat, wlr_f, blr_f, wlr_b, blr_b, gs_f, gs_b, batch, seq, tb=SCAN_TB)

    ones = jnp.ones((ctx_len, DK), F32)
    _, _, rs_f, rs_b = _ret(p_ctx, ones, jnp.zeros_like(ones), zero_state, zero_state, batch, ctx_len, tb=ctx_len)
    cos, sin = _rope_tables(seq)
    ret_f, ret_b, _, _ = _ret(p_lat, cos, sin, rs_f, rs_b, batch, seq, tb=SCAN_TB)

    wrt = jnp.zeros((d, ROUTER_W), F32)
    wrt = wrt.at[:, :N_GROUPS].set(w_router_group[0]).at[:, EXP_ROW0:EXP_ROW0 + N_EXPERTS].set(w_router_expert[0])
    wrt_hi = wrt.astype(BF16)
    wrt = jnp.concatenate([wrt_hi, (wrt - wrt_hi.astype(F32)).astype(BF16)], axis=1)
    brt = jnp.zeros((1, ROUTER_W), F32)
    brt = brt.at[0, :N_GROUPS].set(b_router_group[0]).at[0, EXP_ROW0:EXP_ROW0 + N_EXPERTS].set(b_router_expert[0])
    h, hn, ids8, ew8, cnt = _merge(gla_f, gla_b, ret_f, ret_b, p_lat, x2, lat(g1), lat(sh2), lat(sc2),
                                   jnp.tile(gla_norm[0], HEADS)[None, :], norm_ffn[0][None, :],
                                   w_branch_gla[0].astype(BF16), w_branch_ret[0].astype(BF16),
                                   w_out[0].astype(BF16), wrt, brt, seq, tm=MERGE_TM)

    block_e, src0, stok, nused, pos, nblk = _dispatch_indices(ids8[:TOP_K], ids8[TOP_K:2 * TOP_K], cnt[:, 0], t)
    y_pad = _experts(block_e, src0, stok, nused, hn, w_expert_gate[0], w_expert_up[0], w_expert_down[0], nblk)
    out = _combine(pos, y_pad, h, ew8, lat(g2), norm_final[None, :], seq)
    return out.reshape(batch, seq, d)
```

```python
import functools

import numpy as np
import jax
import jax.numpy as jnp
from jax import lax
from jax.experimental import pallas as pl
from jax.experimental.pallas import tpu as pltpu

F32 = jnp.float32
BF16 = jnp.bfloat16

D_MODEL = 1024
GRID_W = 64
HEADS = 4
DK = 128
DV = 256
GLA_RANK = 16
GLA_TAU = 16.0
GLA_CHUNK = 64
GLA_SUB = 16
RET_CHUNK = 128
ROPE_BASE = 10000.0
N_GROUPS = 4
EXPERTS_PER_GROUP = 8
N_EXPERTS = N_GROUPS * EXPERTS_PER_GROUP
TOP_K = 2
EXPERT_FF = 256
MOE_BLOCK = 128
NORM_EPS = 1e-6

COL_GQ, COL_GK, COL_GV, COL_GG = 0, 512, 1024, 2048
COL_RQ, COL_RK, COL_RV, COL_RG = 3072, 3584, 4096, 5120
COL_MG, COL_MR = 6144, 7168
PROJ_W = 8192
LANES = 128
FEAT_TILES = D_MODEL // LANES
ROUTER_W = 128
EXP_ROW0 = 8

VMEM_LIMIT = 56 * 1024 * 1024

PROJ_TM = 1024
PROJ_TN = 2048
SCAN_TB = 1024
MERGE_TM = 256
COMBINE_TM = 256


def _cparams(sem):
    return pltpu.CompilerParams(dimension_semantics=sem, vmem_limit_bytes=VMEM_LIMIT)


def _sigmoid(x):
    return 1.0 / (1.0 + jnp.exp(-x))


def _silu(x):
    return x * _sigmoid(x)


def _dot(a, b):
    return jnp.dot(a, b, preferred_element_type=F32)


def _dot_nt(a, b):
    return lax.dot_general(a, b, (((1,), (1,)), ((), ())), preferred_element_type=F32)


def _ada_body(c_ref, w_ref, b_ref, o_ref):
    s = _silu(c_ref[...])
    o_ref[...] = _dot(s.astype(BF16), w_ref[...].astype(BF16)) + b_ref[...]


def _ada(c8, w, b):
    n = w.shape[1]
    tn = 1536
    return pl.pallas_call(
        _ada_body,
        grid=(n // tn,),
        in_specs=[pl.BlockSpec((8, D_MODEL), lambda j: (0, 0)),
                  pl.BlockSpec((D_MODEL, tn), lambda j: (0, j)),
                  pl.BlockSpec((1, tn), lambda j: (0, j))],
        out_specs=pl.BlockSpec((8, tn), lambda j: (0, j)),
        out_shape=jax.ShapeDtypeStruct((8, n), F32),
        compiler_params=_cparams(("arbitrary",)),
        name="ada",
    )(c8, w, b)


LR_COL0 = COL_GG + HEADS * DV
WPREP_ROWS = 128


def _wprep_body(w_ref, main_ref, lr_ref):
    w = w_ref[...]
    rows = w.shape[0]
    main_ref[:, :LR_COL0] = w[:, :LR_COL0].astype(BF16)
    main_ref[:, LR_COL0:] = w[:, LR_COL0 + 2 * GLA_RANK:].astype(BF16)
    lr = jnp.concatenate([w[:, LR_COL0:LR_COL0 + 2 * GLA_RANK], jnp.zeros((rows, LANES - 2 * GLA_RANK), F32)], axis=1)
    lr_ref[...] = lr.astype(BF16)


def _wprep(w):
    k, n = w.shape
    return pl.pallas_call(
        _wprep_body,
        grid=(k // WPREP_ROWS,),
        in_specs=[pl.BlockSpec((WPREP_ROWS, n), lambda i: (i, 0))],
        out_specs=[pl.BlockSpec((WPREP_ROWS, PROJ_W), lambda i: (i, 0)),
                   pl.BlockSpec((WPREP_ROWS, LANES), lambda i: (i, 0))],
        out_shape=[jax.ShapeDtypeStruct((k, PROJ_W), BF16), jax.ShapeDtypeStruct((k, LANES), BF16)],
        compiler_params=_cparams(("parallel",)),
        name="wprep",
    )(w)


def _proj_body(x_ref, sh_ref, sc_ref, g_ref, w_ref, wlr_ref, o_ref, lr_ref, h_ref):
    @pl.when(pl.program_id(1) == 0)
    def _():
        x = x_ref[...]
        ms = jnp.mean(x * x, axis=-1, keepdims=True)
        y = x * lax.rsqrt(ms + NORM_EPS) * g_ref[...]
        hb = (y * (1.0 + sc_ref[0]) + sh_ref[0]).astype(BF16)
        h_ref[...] = hb
        lr_ref[...] = _dot(hb, wlr_ref[...])

    o_ref[...] = _dot(h_ref[...], w_ref[...]).astype(BF16)


def _proj(x2, sh, sc, g, w_main, w_lr, tm, rows_per_mod):
    t = x2.shape[0]
    tn = PROJ_TN
    tiles_per_mod = rows_per_mod // tm
    mod_map = lambda i, j: (i // tiles_per_mod, 0, 0)
    return pl.pallas_call(
        _proj_body,
        grid=(t // tm, PROJ_W // tn),
        in_specs=[pl.BlockSpec((tm, D_MODEL), lambda i, j: (i, 0)),
                  pl.BlockSpec((1, 1, D_MODEL), mod_map),
                  pl.BlockSpec((1, 1, D_MODEL), mod_map),
                  pl.BlockSpec((1, D_MODEL), lambda i, j: (0, 0)),
                  pl.BlockSpec((D_MODEL, tn), lambda i, j: (0, j)),
                  pl.BlockSpec((D_MODEL, LANES), lambda i, j: (0, 0))],
        out_specs=[pl.BlockSpec((tm, tn), lambda i, j: (i, j)),
                   pl.BlockSpec((tm, LANES), lambda i, j: (i, 0))],
        out_shape=[jax.ShapeDtypeStruct((t, PROJ_W), BF16),
                   jax.ShapeDtypeStruct((t, LANES), F32)],
        scratch_shapes=[pltpu.VMEM((tm, D_MODEL), BF16)],
        compiler_params=_cparams(("parallel", "arbitrary")),
        name="proj",
    )(x2, sh, sc, g, w_main, w_lr)


def _gla_consts():
    c, s = GLA_CHUNK, GLA_SUB
    msk = {}
    for rev in (False, True):
        cols = []
        for i in range(c // s):
            keys = np.arange(c)[_key_rows(i, rev)][None, :]
            t = np.arange(c)[:, None]
            visible = (keys > t) if rev else (keys <= t)
            cols.append(np.where(t // s == i, visible, False))
        msk[rev] = np.concatenate(cols, axis=1).astype(np.float32)
    return msk


def _key_rows(i, rev):
    return slice(GLA_SUB * i, GLA_CHUNK) if rev else slice(0, GLA_SUB * (i + 1))


def _log_gate(lr, w, b):
    z = _dot(lr.astype(BF16), w.astype(BF16)) + b
    return -(jnp.maximum(-z, 0.0) + jnp.log(1.0 + jnp.exp(-jnp.abs(z)))) * (1.0 / GLA_TAU)


def _subchunk_scan(la, rev):
    c = la.shape[0]
    pos = lax.broadcasted_iota(jnp.int32, la.shape, 0) % GLA_SUB
    w = la
    step = 1
    while step < GLA_SUB:
        if rev:
            w = w + jnp.where(pos < GLA_SUB - step, pltpu.roll(w, c - step, 0), 0.0)
        else:
            w = w + jnp.where(pos >= step, pltpu.roll(w, step, 0), 0.0)
        step *= 2
    return w


def _gla_chunk(q, k, v, la, state, mask, rev):
    c, s = GLA_CHUNK, GLA_SUB
    nsub = c // s
    w = _subchunk_scan(la, rev)
    tot = [w[s * i:s * i + 1] if rev else w[s * (i + 1) - 1:s * (i + 1)] for i in range(nsub)]
    anchors = [None] * nsub
    acc = jnp.zeros_like(tot[0])
    for i in (reversed(range(nsub)) if rev else range(nsub)):
        anchors[i] = acc
        acc = acc + tot[i]
    b_end = acc
    ref = jnp.concatenate([jnp.broadcast_to(a, (s, DK)) for a in anchors], axis=0)
    b = w + ref

    qt = q.astype(F32) * (DK ** -0.5) * jnp.exp(w)
    kf = k.astype(F32)
    inter = _dot((qt * jnp.exp(ref)).astype(BF16), state.astype(BF16))

    kstack = jnp.concatenate([kf[_key_rows(i, rev)] * jnp.exp(anchors[i] - b[_key_rows(i, rev)])
                              for i in range(nsub)], axis=0).astype(BF16)
    vstack = jnp.concatenate([v[_key_rows(i, rev)] for i in range(nsub)], axis=0)
    scores = jnp.where(mask != 0.0, _dot_nt(qt.astype(BF16), kstack), 0.0)
    out = inter + _dot(scores.astype(BF16), vstack)

    kst = kf * jnp.exp(b_end - b)
    xt = jnp.concatenate([kst, jnp.broadcast_to(jnp.exp(b_end), (c, DK))], axis=0).T
    new_state = xt[:, c:c + 1] * state + _dot(xt[:, :c].astype(BF16), v)
    return out, new_state


def _gla_body(qf_ref, kf_ref, vf_ref, lrf_ref, qb_ref, kb_ref, vb_ref, lrb_ref,
              wf_ref, bf_ref, wb_ref, bb_ref, mskf_ref, mskb_ref,
              s0f_ref, s0b_ref, of_ref, ob_ref, sf_ref, sb_ref, *, nchunks):
    @pl.when(pl.program_id(2) == 0)
    def _():
        sf_ref[...] = s0f_ref[...]
        sb_ref[...] = s0b_ref[...]

    c = GLA_CHUNK
    la_f = _log_gate(lrf_ref[...], wf_ref[...], bf_ref[...])
    la_b = _log_gate(lrb_ref[...], wb_ref[...], bb_ref[...])
    mskf, mskb = mskf_ref[...], mskb_ref[...]
    st_f = sf_ref[0, 0]
    st_b = sb_ref[0, 0]
    for n in range(nchunks):
        rf = slice(c * n, c * (n + 1))
        o, st_f = _gla_chunk(qf_ref[rf, :], kf_ref[rf, :], vf_ref[rf, :], la_f[rf, :], st_f, mskf, False)
        of_ref[rf, :] = o.astype(BF16)
        m = nchunks - 1 - n
        rb = slice(c * m, c * (m + 1))
        o, st_b = _gla_chunk(qb_ref[rb, :], kb_ref[rb, :], vb_ref[rb, :], la_b[rb, :], st_b, mskb, True)
        ob_ref[rb, :] = o.astype(BF16)
    sf_ref[0, 0] = st_f
    sb_ref[0, 0] = st_b


def _gla(p, lr, wlr_f, blr_f, wlr_b, blr_b, s0f, s0b, batch, seq, tb):
    nb = seq // tb
    msk = _gla_consts()
    mskf, mskb = jnp.asarray(msk[False]), jnp.asarray(msk[True])

    def fwd(col0, width):
        return lambda b, h, i: (b * nb + i, col0 // width + h)

    def bwd(col0, width):
        return lambda b, h, i: (b * nb + nb - 1 - i, col0 // width + h)

    const2 = lambda b, h, i: (0, 0)
    headcol = lambda b, h, i: (0, h)
    st_map = lambda b, h, i: (b, h, 0, 0)
    st_spec = pl.BlockSpec((1, 1, DK, DV), st_map)
    st_shape = jax.ShapeDtypeStruct((batch, HEADS, DK, DV), F32)
    o_shape = jax.ShapeDtypeStruct((batch * seq, HEADS * DV), BF16)
    return pl.pallas_call(
        functools.partial(_gla_body, nchunks=tb // GLA_CHUNK),
        grid=(batch, HEADS, nb),
        in_specs=[pl.BlockSpec((tb, DK), fwd(COL_GQ, DK)),
                  pl.BlockSpec((tb, DK), fwd(COL_GK, DK)),
                  pl.BlockSpec((tb, DV), fwd(COL_GV, DV)),
                  pl.BlockSpec((tb, LANES), lambda b, h, i: (b * nb + i, 0)),
                  pl.BlockSpec((tb, DK), bwd(COL_GQ, DK)),
                  pl.BlockSpec((tb, DK), bwd(COL_GK, DK)),
                  pl.BlockSpec((tb, DV), bwd(COL_GV, DV)),
                  pl.BlockSpec((tb, LANES), lambda b, h, i: (b * nb + nb - 1 - i, 0)),
                  pl.BlockSpec((LANES, DK), headcol),
                  pl.BlockSpec((1, DK), headcol),
                  pl.BlockSpec((LANES, DK), headcol),
                  pl.BlockSpec((1, DK), headcol),
                  pl.BlockSpec(mskf.shape, const2),
                  pl.BlockSpec(mskb.shape, const2),
                  st_spec, st_spec],
        out_specs=[pl.BlockSpec((tb, DV), lambda b, h, i: (b * nb + i, h)),
                   pl.BlockSpec((tb, DV), lambda b, h, i: (b * nb + nb - 1 - i, h)),
                   st_spec, st_spec],
        out_shape=[o_shape, o_shape, st_shape, st_shape],
        compiler_params=_cparams(("parallel", "parallel", "arbitrary")),
        name="gla",
    )(p, p, p, lr, p, p, p, lr, wlr_f, blr_f, wlr_b, blr_b, mskf, mskb, s0f, s0b)


def _ret_consts():
    c = RET_CHUNK
    hh = np.arange(HEADS, dtype=np.float64)
    lg = {False: np.log1p(-np.exp2(-5.0 - hh)), True: np.log1p(-np.exp2(-5.5 - hh))}
    t = np.arange(c, dtype=np.float64)[:, None]
    u = np.arange(c, dtype=np.float64)[None, :]
    pos = np.arange(c, dtype=np.float64)
    out = {}
    for rev in (False, True):
        g = lg[rev][:, None, None]
        if not rev:
            dmat = np.where(u <= t, np.exp((t - u) * g), 0.0)
            qd = np.exp((pos + 1.0)[None, :] * lg[rev][:, None])
            kd = np.exp((c - 1.0 - pos)[None, :] * lg[rev][:, None])
        else:
            dmat = np.where(u > t, np.exp((u - t) * g), 0.0)
            qd = np.exp((c - pos)[None, :] * lg[rev][:, None])
            kd = np.exp(pos[None, :] * lg[rev][:, None])
        cd = np.exp(c * lg[rev])
        out[rev] = (dmat.astype(np.float32),
                    np.repeat(qd[:, :, None], DV, axis=2).astype(np.float32),
                    np.repeat(kd[:, :, None], DK, axis=2).astype(np.float32),
                    np.repeat(cd[:, None, None], DV, axis=2).astype(np.float32))
    return out


def _rope(x, cos, sin):
    lane = lax.broadcasted_iota(jnp.int32, x.shape, 1)
    partner = jnp.where((lane % 64) < 32, pltpu.roll(x, DK - 32, 1), pltpu.roll(x, 32, 1))
    return x * cos + partner * sin


def _ret_chunk(q, k, v, cos, sin, state, dmat, qd, kd, cd):
    qr = _rope(q.astype(F32), cos, sin).astype(BF16)
    kr = _rope(k.astype(F32) * (DK ** -0.5), cos, sin)
    inter = _dot(qr, state.astype(BF16)) * qd
    scores = _dot_nt(qr, kr.astype(BF16)) * dmat
    out = inter + _dot(scores.astype(BF16), v)
    new_state = cd * state + _dot((kr * kd).T.astype(BF16), v)
    return out, new_state


def _ret_body(qf_ref, kf_ref, vf_ref, cosf_ref, sinf_ref, qb_ref, kb_ref, vb_ref, cosb_ref, sinb_ref,
              dmf_ref, qdf_ref, kdf_ref, cdf_ref, dmb_ref, qdb_ref, kdb_ref, cdb_ref,
              s0f_ref, s0b_ref, of_ref, ob_ref, sf_ref, sb_ref, *, nchunks):
    @pl.when(pl.program_id(2) == 0)
    def _():
        sf_ref[...] = s0f_ref[...]
        sb_ref[...] = s0b_ref[...]

    c = RET_CHUNK
    cf = (dmf_ref[0], qdf_ref[0], kdf_ref[0], cdf_ref[0])
    cb = (dmb_ref[0], qdb_ref[0], kdb_ref[0], cdb_ref[0])
    st_f = sf_ref[0, 0]
    st_b = sb_ref[0, 0]
    for n in range(nchunks):
        rf = slice(c * n, c * (n + 1))
        o, st_f = _ret_chunk(qf_ref[rf, :], kf_ref[rf, :], vf_ref[rf, :], cosf_ref[rf, :], sinf_ref[rf, :],
                             st_f, *cf)
        of_ref[rf, :] = o.astype(BF16)
        m = nchunks - 1 - n
        rb = slice(c * m, c * (m + 1))
        o, st_b = _ret_chunk(qb_ref[rb, :], kb_ref[rb, :], vb_ref[rb, :], cosb_ref[rb, :], sinb_ref[rb, :],
                             st_b, *cb)
        ob_ref[rb, :] = o.astype(BF16)
    sf_ref[0, 0] = st_f
    sb_ref[0, 0] = st_b


def _ret(p, cos, sin, s0f, s0b, batch, seq, tb):
    nb = seq // tb
    consts = _ret_consts()

    def fwd(col0, width):
        return lambda b, h, i: (b * nb + i, col0 // width + h)

    def bwd(col0, width):
        return lambda b, h, i: (b * nb + nb - 1 - i, col0 // width + h)

    head3 = lambda b, h, i: (h, 0, 0)
    st_spec = pl.BlockSpec((1, 1, DK, DV), lambda b, h, i: (b, h, 0, 0))
    st_shape = jax.ShapeDtypeStruct((batch, HEADS, DK, DV), F32)
    o_shape = jax.ShapeDtypeStruct((batch * seq, HEADS * DV), BF16)
    const_specs = []
    const_args = []
    for rev in (False, True):
        for a in consts[rev]:
            const_specs.append(pl.BlockSpec((1,) + a.shape[1:], head3))
            const_args.append(jnp.asarray(a))
    return pl.pallas_call(
        functools.partial(_ret_body, nchunks=tb // RET_CHUNK),
        grid=(batch, HEADS, nb),
        in_specs=[pl.BlockSpec((tb, DK), fwd(COL_RQ, DK)),
                  pl.BlockSpec((tb, DK), fwd(COL_RK, DK)),
                  pl.BlockSpec((tb, DV), fwd(COL_RV, DV)),
                  pl.BlockSpec((tb, DK), lambda b, h, i: (i, 0)),
                  pl.BlockSpec((tb, DK), lambda b, h, i: (i, 0)),
                  pl.BlockSpec((tb, DK), bwd(COL_RQ, DK)),
                  pl.BlockSpec((tb, DK), bwd(COL_RK, DK)),
                  pl.BlockSpec((tb, DV), bwd(COL_RV, DV)),
                  pl.BlockSpec((tb, DK), lambda b, h, i: (nb - 1 - i, 0)),
                  pl.BlockSpec((tb, DK), lambda b, h, i: (nb - 1 - i, 0))]
                 + const_specs + [st_spec, st_spec],
        out_specs=[pl.BlockSpec((tb, DV), lambda b, h, i: (b * nb + i, h)),
                   pl.BlockSpec((tb, DV), lambda b, h, i: (b * nb + nb - 1 - i, h)),
                   st_spec, st_spec],
        out_shape=[o_shape, o_shape, st_shape, st_shape],
        compiler_params=_cparams(("parallel", "parallel", "arbitrary")),
        name="ret",
    )(p, p, p, cos, sin, p, p, p, cos, sin, *const_args, s0f, s0b)


def _rope_tables(seq):
    n = DK // 4
    inv = np.float32(ROPE_BASE) ** (-np.arange(n, dtype=np.float32) / np.float32(n))
    pos = np.arange(seq)
    ar = (pos // GRID_W).astype(np.float32)[:, None] * inv[None, :]
    ac = (pos % GRID_W).astype(np.float32)[:, None] * inv[None, :]
    cos = np.concatenate([np.cos(ar), np.cos(ar), np.cos(ac), np.cos(ac)], axis=1)
    sin = np.concatenate([-np.sin(ar), np.sin(ar), -np.sin(ac), np.sin(ac)], axis=1)
    return jnp.asarray(cos, F32), jnp.asarray(sin, F32)


def _route(logits_t):
    g = [logits_t[i:i + 1] for i in range(N_GROUPS)]
    gmax = jnp.maximum(jnp.maximum(g[0], g[1]), jnp.maximum(g[2], g[3]))
    gsel = jnp.where(g[0] == gmax, 0, jnp.where(g[1] == gmax, 1, jnp.where(g[2] == gmax, 2, 3)))
    gsum = (jnp.exp(g[0] - gmax) + jnp.exp(g[1] - gmax)) + (jnp.exp(g[2] - gmax) + jnp.exp(g[3] - gmax))
    gw = 1.0 / gsum
    e = [logits_t[EXP_ROW0 + EXPERTS_PER_GROUP * i:EXP_ROW0 + EXPERTS_PER_GROUP * (i + 1)]
         for i in range(N_GROUPS)]
    el = jnp.where(gsel == 0, e[0], jnp.where(gsel == 1, e[1], jnp.where(gsel == 2, e[2], e[3])))
    row = lax.broadcasted_iota(jnp.int32, el.shape, 0).astype(F32)
    none = float(EXPERTS_PER_GROUP)
    m1 = jnp.max(el, axis=0, keepdims=True)
    i1 = jnp.min(jnp.where(el == m1, row, none), axis=0, keepdims=True)
    el2 = jnp.where(row == i1, -jnp.inf, el)
    m2 = jnp.max(el2, axis=0, keepdims=True)
    i2 = jnp.min(jnp.where(el2 == m2, row, none), axis=0, keepdims=True)
    r = jnp.exp(m2 - m1)
    w1 = gw / (1.0 + r)
    w2 = gw * r / (1.0 + r)
    base = gsel * EXPERTS_PER_GROUP
    ids = jnp.concatenate([base + i1.astype(jnp.int32), base + i2.astype(jnp.int32)], axis=0)
    return ids, jnp.concatenate([w1, w2], axis=0)


def _store_token_tiled(ref, val):
    m = val.shape[0]
    for j in range(FEAT_TILES):
        ref[pl.ds(j, m, stride=FEAT_TILES), :] = val[:, LANES * j:LANES * (j + 1)]


def _load_token_tiled(ref, m):
    return jnp.concatenate([ref[pl.ds(j, m, stride=FEAT_TILES), :] for j in range(FEAT_TILES)], axis=1)


def _merge_body(gf_ref, gb_ref, rf_ref, rb_ref, gg_ref, rg_ref, mg_ref, mr_ref, x_ref,
                g1_ref, sh2_ref, sc2_ref, gn_ref, nf_ref, wg_ref, wr_ref, wo_ref, wrt_ref, brt_ref, upper_ref,
                h_ref, hn_ref, ids_ref, ew_ref, cnt_ref):
    @pl.when(pl.program_id(0) == 0)
    def _():
        cnt_ref[...] = jnp.zeros_like(cnt_ref)

    og = gf_ref[...].astype(F32) + gb_ref[...].astype(F32)
    orr = rf_ref[...].astype(F32) + rb_ref[...].astype(F32)
    gparts, rparts = [], []
    for hh in range(HEADS):
        seg = og[:, DV * hh:DV * (hh + 1)]
        ms = jnp.mean(seg * seg, axis=-1, keepdims=True)
        gparts.append(seg * lax.rsqrt(ms + NORM_EPS))
        seg = orr[:, DV * hh:DV * (hh + 1)]
        mu = jnp.mean(seg, axis=-1, keepdims=True)
        cen = seg - mu
        var = jnp.mean(cen * cen, axis=-1, keepdims=True)
        rparts.append(cen * lax.rsqrt(var + NORM_EPS))
    o_gla = jnp.concatenate(gparts, axis=1) * gn_ref[...] * _silu(gg_ref[...].astype(F32))
    o_ret = jnp.concatenate(rparts, axis=1) * _silu(rg_ref[...].astype(F32))
    y = (_sigmoid(mg_ref[...].astype(F32)) * _dot(o_gla.astype(BF16), wg_ref[...])
         + _sigmoid(mr_ref[...].astype(F32)) * _dot(o_ret.astype(BF16), wr_ref[...]))
    h = x_ref[...] + g1_ref[0] * _dot(y.astype(BF16), wo_ref[...])
    h_ref[...] = h
    ms = jnp.mean(h * h, axis=-1, keepdims=True)
    hn = h * lax.rsqrt(ms + NORM_EPS) * nf_ref[...] * (1.0 + sc2_ref[0]) + sh2_ref[0]
    _store_token_tiled(hn_ref, hn)
    hn_hi = hn.astype(BF16)
    hn_lo = (hn - hn_hi.astype(F32)).astype(BF16)
    both_w = _dot(hn_hi, wrt_ref[...])
    logits = (both_w[:, :ROUTER_W] + both_w[:, ROUTER_W:] + _dot(hn_lo, wrt_ref[:, :ROUTER_W])) + brt_ref[...]
    ids, ew = _route(logits.T)
    tm = ids.shape[1]

    erow = lax.broadcasted_iota(jnp.int32, (N_EXPERTS, tm), 0)
    oh0 = jnp.where(erow == ids[0:1], 1.0, 0.0)
    oh1 = jnp.where(erow == ids[1:2], 1.0, 0.0)
    both = oh0 + oh1
    before = _dot(both.astype(BF16), upper_ref[...]) + cnt_ref[:, 0:1].astype(F32)
    rank0 = jnp.sum(oh0 * before, axis=0, keepdims=True)
    rank1 = jnp.sum(oh1 * before, axis=0, keepdims=True)
    total = cnt_ref[:, 0:1] + jnp.sum(both, axis=1, keepdims=True).astype(jnp.int32)
    cnt_ref[...] = jnp.broadcast_to(total, cnt_ref.shape)
    ids_ref[...] = jnp.concatenate([ids, rank0.astype(jnp.int32), rank1.astype(jnp.int32),
                                    jnp.zeros((4, tm), jnp.int32)], axis=0)
    ew_ref[...] = jnp.concatenate([ew, jnp.zeros((6, tm), F32)], axis=0)


def _merge(gf, gb, rf, rb, p, x2, g1, sh2, sc2, gn, nf, wg, wr, wo, wrt, brt, seq, tm):
    t = x2.shape[0]
    tiles_per_batch = seq // tm
    row = lambda i: (i, 0)
    mod = lambda i: (i // tiles_per_batch, 0, 0)
    const = lambda i: (0, 0)
    tok = pl.BlockSpec((tm, D_MODEL), row)
    vec = pl.BlockSpec((1, D_MODEL), const)
    modspec = pl.BlockSpec((1, 1, D_MODEL), mod)
    wspec = pl.BlockSpec((D_MODEL, D_MODEL), const)
    upper = jnp.asarray(np.triu(np.ones((tm, tm), np.float32), 1), BF16)

    def pcol(col0):
        return pl.BlockSpec((tm, D_MODEL), lambda i: (i, col0 // D_MODEL))

    return pl.pallas_call(
        _merge_body,
        grid=(t // tm,),
        in_specs=[tok, tok, tok, tok, pcol(COL_GG), pcol(COL_RG), pcol(COL_MG), pcol(COL_MR), tok,
                  modspec, modspec, modspec, vec, vec, wspec, wspec, wspec,
                  pl.BlockSpec((D_MODEL, 2 * ROUTER_W), const), pl.BlockSpec((1, ROUTER_W), const),
                  pl.BlockSpec((tm, tm), const)],
        out_specs=[tok, pl.BlockSpec((tm * FEAT_TILES, LANES), row),
                   pl.BlockSpec((8, tm), lambda i: (0, i)), pl.BlockSpec((8, tm), lambda i: (0, i)),
                   pl.BlockSpec((N_EXPERTS, LANES), const)],
        out_shape=[jax.ShapeDtypeStruct((t, D_MODEL), F32), jax.ShapeDtypeStruct((t * FEAT_TILES, LANES), F32),
                   jax.ShapeDtypeStruct((8, t), jnp.int32), jax.ShapeDtypeStruct((8, t), F32),
                   jax.ShapeDtypeStruct((N_EXPERTS, LANES), jnp.int32)],
        compiler_params=_cparams(("arbitrary",)),
        name="merge",
    )(gf, gb, rf, rb, p, p, p, p, x2, g1, sh2, sc2, gn, nf, wg, wr, wo, wrt, brt, upper)


GATHER_UNROLL = 8


def _aligned(tok):
    off = tok * FEAT_TILES
    return off if isinstance(off, int) else pl.multiple_of(off, FEAT_TILES)


def _token_copy(src_hbm, dst, sem, src_tok, dst_tok):
    return pltpu.make_async_copy(src_hbm.at[pl.ds(_aligned(src_tok), FEAT_TILES), :],
                                 dst.at[pl.ds(_aligned(dst_tok), FEAT_TILES), :], sem)


def _wait_tokens(src_hbm, dst, sem):
    pltpu.make_async_copy(src_hbm.at[pl.ds(0, dst.shape[0]), :], dst, sem).wait()


GATHER_BUFS = 3
GATHER_AHEAD = GATHER_BUFS - 1


def _issue_rows(n, start_row, unrolled):
    if unrolled:
        for r in range(n):
            start_row(r, r % 2)
    else:
        def body(g, carry):
            for u in range(GATHER_UNROLL):
                start_row(g * GATHER_UNROLL + u, u % 2)
            return carry
        lax.fori_loop(0, n // GATHER_UNROLL, body, 0)


def _expert_body(be_ref, src0_ref, stok_ref, nused_ref, hn_hbm, wg_ref, wu_ref, wd_ref, y_ref, *scratch):
    i = pl.program_id(0)
    nused = nused_ref[0]
    bufs, sems = scratch[:GATHER_BUFS], scratch[GATHER_BUFS]
    last = stok_ref.shape[0] - 1

    def issue(block, s, unrolled):
        base = src0_ref[block]

        def start_row(r, priority):
            tok = stok_ref[jnp.minimum(base + r, last)]
            _token_copy(hn_hbm, bufs[s], sems.at[s], tok, r).start(priority=priority)
        _issue_rows(MOE_BLOCK, start_row, unrolled)

    @pl.when(i == 0)
    def _():
        for b in range(GATHER_AHEAD):
            issue(b, b, False)

    for cur in range(GATHER_BUFS):
        mine = i % GATHER_BUFS == cur

        @pl.when(jnp.logical_and(mine, i < nused))
        def _():
            _wait_tokens(hn_hbm, bufs[cur], sems.at[cur])
            issue(i + GATHER_AHEAD, (cur + GATHER_AHEAD) % GATHER_BUFS, True)
            xb = _load_token_tiled(bufs[cur], MOE_BLOCK).astype(BF16)
            gate = _dot(xb, wg_ref[0].astype(BF16))
            up = _dot(xb, wu_ref[0].astype(BF16))
            hid = (_silu(gate) * up).astype(BF16)
            _store_token_tiled(y_ref, _dot(hid, wd_ref[0].astype(BF16)))

        @pl.when(jnp.logical_and(mine, jnp.logical_and(i >= nused, i < nused + GATHER_AHEAD)))
        def _():
            _wait_tokens(hn_hbm, bufs[cur], sems.at[cur])

    @pl.when(i >= nused)
    def _():
        y_ref[...] = jnp.zeros_like(y_ref)


def _experts(block_e, src0, stok, nused, hn, w_gate, w_up, w_down, nblk):
    wmap = lambda i, be, s0, st, nu: (be[i], 0, 0)
    rows = MOE_BLOCK * FEAT_TILES
    return pl.pallas_call(
        _expert_body,
        grid_spec=pltpu.PrefetchScalarGridSpec(
            num_scalar_prefetch=4,
            grid=(nblk,),
            in_specs=[pl.BlockSpec(memory_space=pl.ANY),
                      pl.BlockSpec((1, D_MODEL, EXPERT_FF), wmap),
                      pl.BlockSpec((1, D_MODEL, EXPERT_FF), wmap),
                      pl.BlockSpec((1, EXPERT_FF, D_MODEL), wmap)],
            out_specs=pl.BlockSpec((rows, LANES), lambda i, be, s0, st, nu: (i, 0)),
            scratch_shapes=[pltpu.VMEM((rows, LANES), F32)] * GATHER_BUFS
                           + [pltpu.SemaphoreType.DMA((GATHER_BUFS,))]),
        out_shape=jax.ShapeDtypeStruct((nblk * rows, LANES), F32),
        compiler_params=_cparams(("arbitrary",)),
        name="experts",
    )(block_e, src0, stok, nused, hn, w_gate, w_up, w_down)


def _combine_body(pos_ref, y_hbm, h_ref, ew_ref, g2_ref, nf_ref, o_ref, *scratch):
    i = pl.program_id(0)
    n = pl.num_programs(0)
    tm = COMBINE_TM
    t = pos_ref.shape[0] // TOP_K
    bufs, sems = scratch[:GATHER_BUFS], scratch[GATHER_BUFS]

    def issue(tile, s, unrolled):
        def start_row(j, priority):
            k, r = j % TOP_K, j // TOP_K
            _token_copy(y_hbm, bufs[s].at[k], sems.at[s], pos_ref[k * t + tile * tm + r], r).start(priority=priority)
        _issue_rows(tm * TOP_K, start_row, unrolled)

    def wait(s):
        for k in range(TOP_K):
            _wait_tokens(y_hbm, bufs[s].at[k], sems.at[s])

    @pl.when(i == 0)
    def _():
        for b in range(GATHER_AHEAD):
            issue(b, b, False)

    for cur in range(GATHER_BUFS):
        @pl.when(i % GATHER_BUFS == cur)
        def _():
            wait(cur)
            issue(jnp.minimum(i + GATHER_AHEAD, n - 1), (cur + GATHER_AHEAD) % GATHER_BUFS, True)
            wt = jnp.concatenate([ew_ref[...]] * (LANES // 8), axis=0).T
            moe = (wt[:, 0:1] * _load_token_tiled(bufs[cur].at[0], tm)
                   + wt[:, 1:2] * _load_token_tiled(bufs[cur].at[1], tm))
            h = h_ref[...] + g2_ref[0] * moe
            ms = jnp.mean(h * h, axis=-1, keepdims=True)
            o_ref[...] = h * lax.rsqrt(ms + NORM_EPS) * nf_ref[...]

        @pl.when(jnp.logical_and(i % GATHER_BUFS == cur, i == n - 1))
        def _():
            for ahead in range(1, GATHER_BUFS):
                wait((cur + ahead) % GATHER_BUFS)


def _combine(pos, y_pad, h, ew, g2, nf, seq):
    t = h.shape[0]
    tm = COMBINE_TM
    tiles_per_batch = seq // tm
    ybuf = pltpu.VMEM((TOP_K, tm * FEAT_TILES, LANES), F32)
    return pl.pallas_call(
        _combine_body,
        grid_spec=pltpu.PrefetchScalarGridSpec(
            num_scalar_prefetch=1,
            grid=(t // tm,),
            in_specs=[pl.BlockSpec(memory_space=pl.ANY),
                      pl.BlockSpec((tm, D_MODEL), lambda i, pos: (i, 0)),
                      pl.BlockSpec((8, tm), lambda i, pos: (0, i)),
                      pl.BlockSpec((1, 1, D_MODEL), lambda i, pos: (i // tiles_per_batch, 0, 0)),
                      pl.BlockSpec((1, D_MODEL), lambda i, pos: (0, 0))],
            out_specs=pl.BlockSpec((tm, D_MODEL), lambda i, pos: (i, 0)),
            scratch_shapes=[ybuf] * GATHER_BUFS + [pltpu.SemaphoreType.DMA((GATHER_BUFS,))]),
        out_shape=jax.ShapeDtypeStruct((t, D_MODEL), F32),
        compiler_params=_cparams(("arbitrary",)),
        name="combine",
    )(pos, y_pad, h, ew, g2, nf)


def _dispatch_indices(ids, ranks, counts, t):
    a = t * TOP_K
    nblk = a // MOE_BLOCK + N_EXPERTS + GATHER_AHEAD
    experts = jnp.arange(N_EXPERTS, dtype=jnp.int32)
    starts = jnp.cumsum(counts) - counts
    padded = (counts + MOE_BLOCK - 1) // MOE_BLOCK * MOE_BLOCK
    pends = jnp.cumsum(padded)
    pstarts = pends - padded
    block_start = jnp.arange(nblk, dtype=jnp.int32) * MOE_BLOCK
    block_e = jnp.minimum(jnp.sum((block_start[:, None] >= pends[None, :]).astype(jnp.int32), axis=1),
                          N_EXPERTS - 1)
    of_block = (block_e[:, None] == experts[None, :]).astype(jnp.int32)
    src0 = block_start + jnp.sum(of_block * (starts - pstarts)[None, :], axis=1)
    nused = pends[-1:] // MOE_BLOCK
    tok = jnp.arange(t, dtype=jnp.int32)[None, :]
    slot = jnp.arange(TOP_K, dtype=jnp.int32)[:, None]
    key = ids * a + tok * TOP_K + slot
    stok = (jnp.sort(key.reshape(a)) % a) // TOP_K
    ids_d = ids.reshape(a // LANES, LANES)
    pstart_of = jnp.zeros_like(ids_d)
    for e in range(N_EXPERTS):
        pstart_of = jnp.where(ids_d == e, pstarts[e], pstart_of)
    pos = (ranks.reshape(a // LANES, LANES) + pstart_of).reshape(a)
    i32 = lambda v: v.astype(jnp.int32)
    return i32(block_e), i32(src0), i32(stok), i32(nused), i32(pos), nblk


def kernel(x, c, ctx, c_ctx, w_ada, b_ada, norm_mix, norm_ffn, w_in, gla_lr_w, gla_lr_b, gla_norm,
           w_branch_gla, w_branch_ret, w_out, w_router_group, b_router_group, w_router_expert,
           b_router_expert, w_expert_gate, w_expert_up, w_expert_down, norm_final):
    batch, seq, d = x.shape
    ctx_len = ctx.shape[1]
    assert d == D_MODEL and w_ada.shape[0] == 1, "single-layer block with D_MODEL features"
    t = batch * seq

    c8 = jnp.zeros((8, d), F32).at[:batch].set(c).at[batch].set(c_ctx)
    mod = _ada(c8, w_ada[0], b_ada[0][None, :])
    sh1, sc1, g1, sh2, sc2, g2 = [mod[:, d * i:d * (i + 1)] for i in range(6)]
    lat = lambda m: m[:batch, None, :]
    cx = lambda m: m[batch:batch + 1, None, :]

    w_main, w_lr = _wprep(w_in[0])
    nm = norm_mix[0][None, :]

    p_ctx, lr_ctx = _proj(ctx.reshape(batch * ctx_len, d), cx(sh1), cx(sc1), nm, w_main, w_lr,
                          tm=ctx_len, rows_per_mod=batch * ctx_len)
    x2 = x.reshape(t, d)
    p_lat, lr_lat = _proj(x2, lat(sh1), lat(sc1), nm, w_main, w_lr, tm=PROJ_TM, rows_per_mod=seq)

    wlr_f = jnp.zeros((LANES, HEADS * DK), F32).at[:GLA_RANK].set(gla_lr_w[0, 0])
    wlr_b = jnp.zeros((LANES, HEADS * DK), F32).at[GLA_RANK:2 * GLA_RANK].set(gla_lr_w[0, 1])
    blr_f, blr_b = gla_lr_b[0, 0][None, :], gla_lr_b[0, 1][None, :]
    zero_state = jnp.zeros((batch, HEADS, DK, DV), F32)

    _, _, gs_f, gs_b = _gla(p_ctx, lr_ctx, wlr_f, blr_f, wlr_b, blr_b, zero_state, zero_state,
                            batch, ctx_len, tb=ctx_len)
    gla_f, gla_b, _, _ = _gla(p_lat, lr_lat, wlr_f, blr_f, wlr_b, blr_b, gs_f, gs_b, batch, seq, tb=SCAN_TB)

    ones = jnp.ones((ctx_len, DK), F32)
    _, _, rs_f, rs_b = _ret(p_ctx, ones, jnp.zeros_like(ones), zero_state, zero_state, batch, ctx_len, tb=ctx_len)
    cos, sin = _rope_tables(seq)
    ret_f, ret_b, _, _ = _ret(p_lat, cos, sin, rs_f, rs_b, batch, seq, tb=SCAN_TB)

    wrt = jnp.zeros((d, ROUTER_W), F32)
    wrt = wrt.at[:, :N_GROUPS].set(w_router_group[0]).at[:, EXP_ROW0:EXP_ROW0 + N_EXPERTS].set(w_router_expert[0])
    wrt_hi = wrt.astype(BF16)
    wrt = jnp.concatenate([wrt_hi, (wrt - wrt_hi.astype(F32)).astype(BF16)], axis=1)
    brt = jnp.zeros((1, ROUTER_W), F32)
    brt = brt.at[0, :N_GROUPS].set(b_router_group[0]).at[0, EXP_ROW0:EXP_ROW0 + N_EXPERTS].set(b_router_expert[0])
    h, hn, ids8, ew8, cnt = _merge(gla_f, gla_b, ret_f, ret_b, p_lat, x2, lat(g1), lat(sh2), lat(sc2),
                                   jnp.tile(gla_norm[0], HEADS)[None, :], norm_ffn[0][None, :],
                                   w_branch_gla[0].astype(BF16), w_branch_ret[0].astype(BF16),
                                   w_out[0].astype(BF16), wrt, brt, seq, tm=MERGE_TM)

    block_e, src0, stok, nused, pos, nblk = _dispatch_indices(ids8[:TOP_K], ids8[TOP_K:2 * TOP_K], cnt[:, 0], t)
    y_pad = _experts(block_e, src0, stok, nused, hn, w_expert_gate[0], w_expert_up[0], w_expert_down[0], nblk)
    out = _combine(pos, y_pad, h, ew8, lat(g2), norm_final[None, :], seq)
    return out.reshape(batch, seq, d)
```

```python
import functools

import numpy as np
import jax
import jax.numpy as jnp
from jax import lax
from jax.experimental import pallas as pl
from jax.experimental.pallas import tpu as pltpu

F32 = jnp.float32
BF16 = jnp.bfloat16

D_MODEL = 1024
GRID_W = 64
HEADS = 4
DK = 128
DV = 256
GLA_RANK = 16
GLA_TAU = 16.0
GLA_CHUNK = 64
GLA_SUB = 16
RET_CHUNK = 128
ROPE_BASE = 10000.0
N_GROUPS = 4
EXPERTS_PER_GROUP = 8
N_EXPERTS = N_GROUPS * EXPERTS_PER_GROUP
TOP_K = 2
EXPERT_FF = 256
MOE_BLOCK = 128
NORM_EPS = 1e-6

COL_GQ, COL_GK, COL_GV, COL_GG = 0, 512, 1024, 2048
COL_RQ, COL_RK, COL_RV, COL_RG = 3072, 3584, 4096, 5120
COL_MG, COL_MR = 6144, 7168
PROJ_W = 8192
LANES = 128
FEAT_TILES = D_MODEL // LANES
ROUTER_W = 128
EXP_ROW0 = 8

VMEM_LIMIT = 56 * 1024 * 1024

PROJ_TM = 1024
PROJ_TN = 2048
SCAN_TB = 1024
MERGE_TM = 256
COMBINE_TM = 256


def _cparams(sem):
    return pltpu.CompilerParams(dimension_semantics=sem, vmem_limit_bytes=VMEM_LIMIT)


def _sigmoid(x):
    return 1.0 / (1.0 + jnp.exp(-x))


def _silu(x):
    return x * _sigmoid(x)


def _dot(a, b):
    return jnp.dot(a, b, preferred_element_type=F32)


def _dot_nt(a, b):
    return lax.dot_general(a, b, (((1,), (1,)), ((), ())), preferred_element_type=F32)


def _ada_body(c_ref, w_ref, b_ref, o_ref):
    s = _silu(c_ref[...])
    o_ref[...] = _dot(s.astype(BF16), w_ref[...].astype(BF16)) + b_ref[...]


def _ada(c8, w, b):
    n = w.shape[1]
    tn = 1536
    return pl.pallas_call(
        _ada_body,
        grid=(n // tn,),
        in_specs=[pl.BlockSpec((8, D_MODEL), lambda j: (0, 0)),
                  pl.BlockSpec((D_MODEL, tn), lambda j: (0, j)),
                  pl.BlockSpec((1, tn), lambda j: (0, j))],
        out_specs=pl.BlockSpec((8, tn), lambda j: (0, j)),
        out_shape=jax.ShapeDtypeStruct((8, n), F32),
        compiler_params=_cparams(("arbitrary",)),
        name="ada",
    )(c8, w, b)


LR_COL0 = COL_GG + HEADS * DV
WPREP_ROWS = 1024


def _wprep_body(w_ref, wlr_ref, main_ref, lr_ref):
    main_ref[...] = w_ref[...].astype(BF16)
    pad = jnp.zeros((LANES - 2 * GLA_RANK, D_MODEL), F32)
    lr_ref[...] = jnp.concatenate([wlr_ref[...], pad], axis=0).astype(BF16)


def _wprep(wt):
    steps_before = LR_COL0 // WPREP_ROWS

    def src_row(i):
        row = jnp.where(i < steps_before, i * WPREP_ROWS, i * WPREP_ROWS + 2 * GLA_RANK)
        return (pl.multiple_of(row, 2 * GLA_RANK), 0)

    return pl.pallas_call(
        _wprep_body,
        grid=(PROJ_W // WPREP_ROWS,),
        in_specs=[pl.BlockSpec((pl.Element(WPREP_ROWS), pl.Element(D_MODEL)), src_row),
                  pl.BlockSpec((pl.Element(2 * GLA_RANK), pl.Element(D_MODEL)), lambda i: (LR_COL0, 0))],
        out_specs=[pl.BlockSpec((WPREP_ROWS, D_MODEL), lambda i: (i, 0)),
                   pl.BlockSpec((LANES, D_MODEL), lambda i: (0, 0))],
        out_shape=[jax.ShapeDtypeStruct((PROJ_W, D_MODEL), BF16), jax.ShapeDtypeStruct((LANES, D_MODEL), BF16)],
        compiler_params=_cparams(("arbitrary",)),
        name="wprep",
    )(wt, wt)


def _proj_body(x_ref, sh_ref, sc_ref, g_ref, w_ref, wlr_ref, o_ref, lr_ref, h_ref):
    @pl.when(pl.program_id(1) == 0)
    def _():
        x = x_ref[...]
        ms = jnp.mean(x * x, axis=-1, keepdims=True)
        y = x * lax.rsqrt(ms + NORM_EPS) * g_ref[...]
        hb = (y * (1.0 + sc_ref[0]) + sh_ref[0]).astype(BF16)
        h_ref[...] = hb
        lr_ref[...] = _dot_nt(hb, wlr_ref[...])

    o_ref[...] = _dot_nt(h_ref[...], w_ref[...]).astype(BF16)


def _proj(x2, sh, sc, g, w_main, w_lr, tm, rows_per_mod):
    t = x2.shape[0]
    tn = PROJ_TN
    tiles_per_mod = rows_per_mod // tm
    mod_map = lambda i, j: (i // tiles_per_mod, 0, 0)
    return pl.pallas_call(
        _proj_body,
        grid=(t // tm, PROJ_W // tn),
        in_specs=[pl.BlockSpec((tm, D_MODEL), lambda i, j: (i, 0)),
                  pl.BlockSpec((1, 1, D_MODEL), mod_map),
                  pl.BlockSpec((1, 1, D_MODEL), mod_map),
                  pl.BlockSpec((1, D_MODEL), lambda i, j: (0, 0)),
                  pl.BlockSpec((tn, D_MODEL), lambda i, j: (j, 0)),
                  pl.BlockSpec((LANES, D_MODEL), lambda i, j: (0, 0))],
        out_specs=[pl.BlockSpec((tm, tn), lambda i, j: (i, j)),
                   pl.BlockSpec((tm, LANES), lambda i, j: (i, 0))],
        out_shape=[jax.ShapeDtypeStruct((t, PROJ_W), BF16),
                   jax.ShapeDtypeStruct((t, LANES), F32)],
        scratch_shapes=[pltpu.VMEM((tm, D_MODEL), BF16)],
        compiler_params=_cparams(("parallel", "arbitrary")),
        name="proj",
    )(x2, sh, sc, g, w_main, w_lr)


def _gla_consts():
    c, s = GLA_CHUNK, GLA_SUB
    msk = {}
    for rev in (False, True):
        cols = []
        for i in range(c // s):
            keys = np.arange(c)[_key_rows(i, rev)][None, :]
            t = np.arange(c)[:, None]
            visible = (keys > t) if rev else (keys <= t)
            cols.append(np.where(t // s == i, visible, False))
        msk[rev] = np.concatenate(cols, axis=1).astype(np.float32)
    return msk


def _key_rows(i, rev):
    return slice(GLA_SUB * i, GLA_CHUNK) if rev else slice(0, GLA_SUB * (i + 1))


def _log_gate(lr, w, b):
    z = _dot(lr.astype(BF16), w.astype(BF16)) + b
    return -(jnp.maximum(-z, 0.0) + jnp.log(1.0 + jnp.exp(-jnp.abs(z)))) * (1.0 / GLA_TAU)


def _subchunk_scan(la, rev):
    c = la.shape[0]
    pos = lax.broadcasted_iota(jnp.int32, la.shape, 0) % GLA_SUB
    w = la
    step = 1
    while step < GLA_SUB:
        if rev:
            w = w + jnp.where(pos < GLA_SUB - step, pltpu.roll(w, c - step, 0), 0.0)
        else:
            w = w + jnp.where(pos >= step, pltpu.roll(w, step, 0), 0.0)
        step *= 2
    return w


def _gla_chunk(q, k, v, la, state, mask, rev):
    c, s = GLA_CHUNK, GLA_SUB
    nsub = c // s
    w = _subchunk_scan(la, rev)
    tot = [w[s * i:s * i + 1] if rev else w[s * (i + 1) - 1:s * (i + 1)] for i in range(nsub)]
    anchors = [None] * nsub
    acc = jnp.zeros_like(tot[0])
    for i in (reversed(range(nsub)) if rev else range(nsub)):
        anchors[i] = acc
        acc = acc + tot[i]
    b_end = acc
    ref = jnp.concatenate([jnp.broadcast_to(a, (s, DK)) for a in anchors], axis=0)
    b = w + ref

    qt = q.astype(F32) * (DK ** -0.5) * jnp.exp(w)
    kf = k.astype(F32)
    inter = _dot((qt * jnp.exp(ref)).astype(BF16), state.astype(BF16))

    kstack = jnp.concatenate([kf[_key_rows(i, rev)] * jnp.exp(anchors[i] - b[_key_rows(i, rev)])
                              for i in range(nsub)], axis=0).astype(BF16)
    vstack = jnp.concatenate([v[_key_rows(i, rev)] for i in range(nsub)], axis=0)
    scores = jnp.where(mask != 0.0, _dot_nt(qt.astype(BF16), kstack), 0.0)
    out = inter + _dot(scores.astype(BF16), vstack)

    kst = kf * jnp.exp(b_end - b)
    xt = jnp.concatenate([kst, jnp.broadcast_to(jnp.exp(b_end), (c, DK))], axis=0).T
    new_state = xt[:, c:c + 1] * state + _dot(xt[:, :c].astype(BF16), v)
    return out, new_state


def _gla_body(qf_ref, kf_ref, vf_ref, lrf_ref, qb_ref, kb_ref, vb_ref, lrb_ref,
              wf_ref, bf_ref, wb_ref, bb_ref, mskf_ref, mskb_ref,
              s0f_ref, s0b_ref, of_ref, ob_ref, sf_ref, sb_ref, *, nchunks):
    @pl.when(pl.program_id(2) == 0)
    def _():
        sf_ref[...] = s0f_ref[...]
        sb_ref[...] = s0b_ref[...]

    c = GLA_CHUNK
    la_f = _log_gate(lrf_ref[...], wf_ref[...], bf_ref[...])
    la_b = _log_gate(lrb_ref[...], wb_ref[...], bb_ref[...])
    mskf, mskb = mskf_ref[...], mskb_ref[...]
    st_f = sf_ref[0, 0]
    st_b = sb_ref[0, 0]
    for n in range(nchunks):
        rf = slice(c * n, c * (n + 1))
        o, st_f = _gla_chunk(qf_ref[rf, :], kf_ref[rf, :], vf_ref[rf, :], la_f[rf, :], st_f, mskf, False)
        of_ref[rf, :] = o.astype(BF16)
        m = nchunks - 1 - n
        rb = slice(c * m, c * (m + 1))
        o, st_b = _gla_chunk(qb_ref[rb, :], kb_ref[rb, :], vb_ref[rb, :], la_b[rb, :], st_b, mskb, True)
        ob_ref[rb, :] = o.astype(BF16)
    sf_ref[0, 0] = st_f
    sb_ref[0, 0] = st_b


def _gla(p, lr, wlr_f, blr_f, wlr_b, blr_b, s0f, s0b, batch, seq, tb):
    nb = seq // tb
    msk = _gla_consts()
    mskf, mskb = jnp.asarray(msk[False]), jnp.asarray(msk[True])

    def fwd(col0, width):
        return lambda b, h, i: (b * nb + i, col0 // width + h)

    def bwd(col0, width):
        return lambda b, h, i: (b * nb + nb - 1 - i, col0 // width + h)

    const2 = lambda b, h, i: (0, 0)
    headcol = lambda b, h, i: (0, h)
    st_map = lambda b, h, i: (b, h, 0, 0)
    st_spec = pl.BlockSpec((1, 1, DK, DV), st_map)
    st_shape = jax.ShapeDtypeStruct((batch, HEADS, DK, DV), F32)
    o_shape = jax.ShapeDtypeStruct((batch * seq, HEADS * DV), BF16)
    return pl.pallas_call(
        functools.partial(_gla_body, nchunks=tb // GLA_CHUNK),
        grid=(batch, HEADS, nb),
        in_specs=[pl.BlockSpec((tb, DK), fwd(COL_GQ, DK)),
                  pl.BlockSpec((tb, DK), fwd(COL_GK, DK)),
                  pl.BlockSpec((tb, DV), fwd(COL_GV, DV)),
                  pl.BlockSpec((tb, LANES), lambda b, h, i: (b * nb + i, 0)),
                  pl.BlockSpec((tb, DK), bwd(COL_GQ, DK)),
                  pl.BlockSpec((tb, DK), bwd(COL_GK, DK)),
                  pl.BlockSpec((tb, DV), bwd(COL_GV, DV)),
                  pl.BlockSpec((tb, LANES), lambda b, h, i: (b * nb + nb - 1 - i, 0)),
                  pl.BlockSpec((LANES, DK), headcol),
                  pl.BlockSpec((1, DK), headcol),
                  pl.BlockSpec((LANES, DK), headcol),
                  pl.BlockSpec((1, DK), headcol),
                  pl.BlockSpec(mskf.shape, const2),
                  pl.BlockSpec(mskb.shape, const2),
                  st_spec, st_spec],
        out_specs=[pl.BlockSpec((tb, DV), lambda b, h, i: (b * nb + i, h)),
                   pl.BlockSpec((tb, DV), lambda b, h, i: (b * nb + nb - 1 - i, h)),
                   st_spec, st_spec],
        out_shape=[o_shape, o_shape, st_shape, st_shape],
        compiler_params=_cparams(("parallel", "parallel", "arbitrary")),
        name="gla",
    )(p, p, p, lr, p, p, p, lr, wlr_f, blr_f, wlr_b, blr_b, mskf, mskb, s0f, s0b)


def _ret_consts():
    c = RET_CHUNK
    hh = np.arange(HEADS, dtype=np.float64)
    lg = {False: np.log1p(-np.exp2(-5.0 - hh)), True: np.log1p(-np.exp2(-5.5 - hh))}
    t = np.arange(c, dtype=np.float64)[:, None]
    u = np.arange(c, dtype=np.float64)[None, :]
    pos = np.arange(c, dtype=np.float64)
    out = {}
    for rev in (False, True):
        g = lg[rev][:, None, None]
        if not rev:
            dmat = np.where(u <= t, np.exp((t - u) * g), 0.0)
            qd = np.exp((pos + 1.0)[None, :] * lg[rev][:, None])
            kd = np.exp((c - 1.0 - pos)[None, :] * lg[rev][:, None])
        else:
            dmat = np.where(u > t, np.exp((u - t) * g), 0.0)
            qd = np.exp((c - pos)[None, :] * lg[rev][:, None])
            kd = np.exp(pos[None, :] * lg[rev][:, None])
        cd = np.exp(c * lg[rev])
        out[rev] = (dmat.astype(np.float32),
                    np.repeat(qd[:, :, None], DV, axis=2).astype(np.float32),
                    np.repeat(kd[:, :, None], DK, axis=2).astype(np.float32),
                    np.repeat(cd[:, None, None], DV, axis=2).astype(np.float32))
    return out


def _rope(x, cos, sin):
    lane = lax.broadcasted_iota(jnp.int32, x.shape, 1)
    partner = jnp.where((lane % 64) < 32, pltpu.roll(x, DK - 32, 1), pltpu.roll(x, 32, 1))
    return x * cos + partner * sin


def _ret_chunk(q, k, v, cos, sin, state, dmat, qd, kd, cd):
    qr = _rope(q.astype(F32), cos, sin).astype(BF16)
    kr = _rope(k.astype(F32) * (DK ** -0.5), cos, sin)
    inter = _dot(qr, state.astype(BF16)) * qd
    scores = _dot_nt(qr, kr.astype(BF16)) * dmat
    out = inter + _dot(scores.astype(BF16), v)
    new_state = cd * state + _dot((kr * kd).T.astype(BF16), v)
    return out, new_state


def _ret_body(qf_ref, kf_ref, vf_ref, cosf_ref, sinf_ref, qb_ref, kb_ref, vb_ref, cosb_ref, sinb_ref,
              dmf_ref, qdf_ref, kdf_ref, cdf_ref, dmb_ref, qdb_ref, kdb_ref, cdb_ref,
              s0f_ref, s0b_ref, of_ref, ob_ref, sf_ref, sb_ref, *, nchunks):
    @pl.when(pl.program_id(2) == 0)
    def _():
        sf_ref[...] = s0f_ref[...]
        sb_ref[...] = s0b_ref[...]

    c = RET_CHUNK
    cf = (dmf_ref[0], qdf_ref[0], kdf_ref[0], cdf_ref[0])
    cb = (dmb_ref[0], qdb_ref[0], kdb_ref[0], cdb_ref[0])
    st_f = sf_ref[0, 0]
    st_b = sb_ref[0, 0]
    for n in range(nchunks):
        rf = slice(c * n, c * (n + 1))
        o, st_f = _ret_chunk(qf_ref[rf, :], kf_ref[rf, :], vf_ref[rf, :], cosf_ref[rf, :], sinf_ref[rf, :],
                             st_f, *cf)
        of_ref[rf, :] = o.astype(BF16)
        m = nchunks - 1 - n
        rb = slice(c * m, c * (m + 1))
        o, st_b = _ret_chunk(qb_ref[rb, :], kb_ref[rb, :], vb_ref[rb, :], cosb_ref[rb, :], sinb_ref[rb, :],
                             st_b, *cb)
        ob_ref[rb, :] = o.astype(BF16)
    sf_ref[0, 0] = st_f
    sb_ref[0, 0] = st_b


def _ret(p, cos, sin, s0f, s0b, batch, seq, tb):
    nb = seq // tb
    consts = _ret_consts()

    def fwd(col0, width):
        return lambda b, h, i: (b * nb + i, col0 // width + h)

    def bwd(col0, width):
        return lambda b, h, i: (b * nb + nb - 1 - i, col0 // width + h)

    head3 = lambda b, h, i: (h, 0, 0)
    st_spec = pl.BlockSpec((1, 1, DK, DV), lambda b, h, i: (b, h, 0, 0))
    st_shape = jax.ShapeDtypeStruct((batch, HEADS, DK, DV), F32)
    o_shape = jax.ShapeDtypeStruct((batch * seq, HEADS * DV), BF16)
    const_specs = []
    const_args = []
    for rev in (False, True):
        for a in consts[rev]:
            const_specs.append(pl.BlockSpec((1,) + a.shape[1:], head3))
            const_args.append(jnp.asarray(a))
    return pl.pallas_call(
        functools.partial(_ret_body, nchunks=tb // RET_CHUNK),
        grid=(batch, HEADS, nb),
        in_specs=[pl.BlockSpec((tb, DK), fwd(COL_RQ, DK)),
                  pl.BlockSpec((tb, DK), fwd(COL_RK, DK)),
                  pl.BlockSpec((tb, DV), fwd(COL_RV, DV)),
                  pl.BlockSpec((tb, DK), lambda b, h, i: (i, 0)),
                  pl.BlockSpec((tb, DK), lambda b, h, i: (i, 0)),
                  pl.BlockSpec((tb, DK), bwd(COL_RQ, DK)),
                  pl.BlockSpec((tb, DK), bwd(COL_RK, DK)),
                  pl.BlockSpec((tb, DV), bwd(COL_RV, DV)),
                  pl.BlockSpec((tb, DK), lambda b, h, i: (nb - 1 - i, 0)),
                  pl.BlockSpec((tb, DK), lambda b, h, i: (nb - 1 - i, 0))]
                 + const_specs + [st_spec, st_spec],
        out_specs=[pl.BlockSpec((tb, DV), lambda b, h, i: (b * nb + i, h)),
                   pl.BlockSpec((tb, DV), lambda b, h, i: (b * nb + nb - 1 - i, h)),
                   st_spec, st_spec],
        out_shape=[o_shape, o_shape, st_shape, st_shape],
        compiler_params=_cparams(("parallel", "parallel", "arbitrary")),
        name="ret",
    )(p, p, p, cos, sin, p, p, p, cos, sin, *const_args, s0f, s0b)


def _rope_tables(seq):
    n = DK // 4
    inv = np.float32(ROPE_BASE) ** (-np.arange(n, dtype=np.float32) / np.float32(n))
    pos = np.arange(seq)
    ar = (pos // GRID_W).astype(np.float32)[:, None] * inv[None, :]
    ac = (pos % GRID_W).astype(np.float32)[:, None] * inv[None, :]
    cos = np.concatenate([np.cos(ar), np.cos(ar), np.cos(ac), np.cos(ac)], axis=1)
    sin = np.concatenate([-np.sin(ar), np.sin(ar), -np.sin(ac), np.sin(ac)], axis=1)
    return jnp.asarray(cos, F32), jnp.asarray(sin, F32)


def _route(logits_t):
    g = [logits_t[i:i + 1] for i in range(N_GROUPS)]
    gmax = jnp.maximum(jnp.maximum(g[0], g[1]), jnp.maximum(g[2], g[3]))
    gsel = jnp.where(g[0] == gmax, 0, jnp.where(g[1] == gmax, 1, jnp.where(g[2] == gmax, 2, 3)))
    gsum = (jnp.exp(g[0] - gmax) + jnp.exp(g[1] - gmax)) + (jnp.exp(g[2] - gmax) + jnp.exp(g[3] - gmax))
    gw = 1.0 / gsum
    e = [logits_t[EXP_ROW0 + EXPERTS_PER_GROUP * i:EXP_ROW0 + EXPERTS_PER_GROUP * (i + 1)]
         for i in range(N_GROUPS)]
    el = jnp.where(gsel == 0, e[0], jnp.where(gsel == 1, e[1], jnp.where(gsel == 2, e[2], e[3])))
    row = lax.broadcasted_iota(jnp.int32, el.shape, 0).astype(F32)
    none = float(EXPERTS_PER_GROUP)
    m1 = jnp.max(el, axis=0, keepdims=True)
    i1 = jnp.min(jnp.where(el == m1, row, none), axis=0, keepdims=True)
    el2 = jnp.where(row == i1, -jnp.inf, el)
    m2 = jnp.max(el2, axis=0, keepdims=True)
    i2 = jnp.min(jnp.where(el2 == m2, row, none), axis=0, keepdims=True)
    r = jnp.exp(m2 - m1)
    w1 = gw / (1.0 + r)
    w2 = gw * r / (1.0 + r)
    base = gsel * EXPERTS_PER_GROUP
    ids = jnp.concatenate([base + i1.astype(jnp.int32), base + i2.astype(jnp.int32)], axis=0)
    return ids, jnp.concatenate([w1, w2], axis=0)


def _store_token_tiled(ref, val):
    m = val.shape[0]
    for j in range(FEAT_TILES):
        ref[pl.ds(j, m, stride=FEAT_TILES), :] = val[:, LANES * j:LANES * (j + 1)]


def _load_token_tiled(ref, m):
    return jnp.concatenate([ref[pl.ds(j, m, stride=FEAT_TILES), :] for j in range(FEAT_TILES)], axis=1)


def _merge_body(gf_ref, gb_ref, rf_ref, rb_ref, gg_ref, rg_ref, mg_ref, mr_ref, x_ref,
                g1_ref, sh2_ref, sc2_ref, gn_ref, nf_ref, wg_ref, wr_ref, wo_ref, wrt_ref, brt_ref, upper_ref,
                h_ref, hn_ref, ids_ref, ew_ref, cnt_ref):
    @pl.when(pl.program_id(0) == 0)
    def _():
        cnt_ref[...] = jnp.zeros_like(cnt_ref)

    og = gf_ref[...].astype(F32) + gb_ref[...].astype(F32)
    orr = rf_ref[...].astype(F32) + rb_ref[...].astype(F32)
    gparts, rparts = [], []
    for hh in range(HEADS):
        seg = og[:, DV * hh:DV * (hh + 1)]
        ms = jnp.mean(seg * seg, axis=-1, keepdims=True)
        gparts.append(seg * lax.rsqrt(ms + NORM_EPS))
        seg = orr[:, DV * hh:DV * (hh + 1)]
        mu = jnp.mean(seg, axis=-1, keepdims=True)
        cen = seg - mu
        var = jnp.mean(cen * cen, axis=-1, keepdims=True)
        rparts.append(cen * lax.rsqrt(var + NORM_EPS))
    o_gla = jnp.concatenate(gparts, axis=1) * gn_ref[...] * _silu(gg_ref[...].astype(F32))
    o_ret = jnp.concatenate(rparts, axis=1) * _silu(rg_ref[...].astype(F32))
    y = (_sigmoid(mg_ref[...].astype(F32)) * _dot(o_gla.astype(BF16), wg_ref[...])
         + _sigmoid(mr_ref[...].astype(F32)) * _dot(o_ret.astype(BF16), wr_ref[...]))
    h = x_ref[...] + g1_ref[0] * _dot(y.astype(BF16), wo_ref[...])
    h_ref[...] = h
    ms = jnp.mean(h * h, axis=-1, keepdims=True)
    hn = h * lax.rsqrt(ms + NORM_EPS) * nf_ref[...] * (1.0 + sc2_ref[0]) + sh2_ref[0]
    _store_token_tiled(hn_ref, hn)
    hn_hi = hn.astype(BF16)
    hn_lo = (hn - hn_hi.astype(F32)).astype(BF16)
    both_w = _dot(hn_hi, wrt_ref[...])
    logits = (both_w[:, :ROUTER_W] + both_w[:, ROUTER_W:] + _dot(hn_lo, wrt_ref[:, :ROUTER_W])) + brt_ref[...]
    ids, ew = _route(logits.T)
    tm = ids.shape[1]

    erow = lax.broadcasted_iota(jnp.int32, (N_EXPERTS, tm), 0)
    oh0 = jnp.where(erow == ids[0:1], 1.0, 0.0)
    oh1 = jnp.where(erow == ids[1:2], 1.0, 0.0)
    both = oh0 + oh1
    before = _dot(both.astype(BF16), upper_ref[...]) + cnt_ref[:, 0:1].astype(F32)
    rank0 = jnp.sum(oh0 * before, axis=0, keepdims=True)
    rank1 = jnp.sum(oh1 * before, axis=0, keepdims=True)
    total = cnt_ref[:, 0:1] + jnp.sum(both, axis=1, keepdims=True).astype(jnp.int32)
    cnt_ref[...] = jnp.broadcast_to(total, cnt_ref.shape)
    ids_ref[...] = jnp.concatenate([ids, rank0.astype(jnp.int32), rank1.astype(jnp.int32),
                                    jnp.zeros((4, tm), jnp.int32)], axis=0)
    ew_ref[...] = jnp.concatenate([ew, jnp.zeros((6, tm), F32)], axis=0)


def _merge(gf, gb, rf, rb, p, x2, g1, sh2, sc2, gn, nf, wg, wr, wo, wrt, brt, seq, tm):
    t = x2.shape[0]
    tiles_per_batch = seq // tm
    row = lambda i: (i, 0)
    mod = lambda i: (i // tiles_per_batch, 0, 0)
    const = lambda i: (0, 0)
    tok = pl.BlockSpec((tm, D_MODEL), row)
    vec = pl.BlockSpec((1, D_MODEL), const)
    modspec = pl.BlockSpec((1, 1, D_MODEL), mod)
    wspec = pl.BlockSpec((D_MODEL, D_MODEL), const)
    upper = jnp.asarray(np.triu(np.ones((tm, tm), np.float32), 1), BF16)

    def pcol(col0):
        return pl.BlockSpec((tm, D_MODEL), lambda i: (i, col0 // D_MODEL))

    return pl.pallas_call(
        _merge_body,
        grid=(t // tm,),
        in_specs=[tok, tok, tok, tok, pcol(COL_GG), pcol(COL_RG), pcol(COL_MG), pcol(COL_MR), tok,
                  modspec, modspec, modspec, vec, vec, wspec, wspec, wspec,
                  pl.BlockSpec((D_MODEL, 2 * ROUTER_W), const), pl.BlockSpec((1, ROUTER_W), const),
                  pl.BlockSpec((tm, tm), const)],
        out_specs=[tok, pl.BlockSpec((tm * FEAT_TILES, LANES), row),
                   pl.BlockSpec((8, tm), lambda i: (0, i)), pl.BlockSpec((8, tm), lambda i: (0, i)),
                   pl.BlockSpec((N_EXPERTS, LANES), const)],
        out_shape=[jax.ShapeDtypeStruct((t, D_MODEL), F32), jax.ShapeDtypeStruct((t * FEAT_TILES, LANES), F32),
                   jax.ShapeDtypeStruct((8, t), jnp.int32), jax.ShapeDtypeStruct((8, t), F32),
                   jax.ShapeDtypeStruct((N_EXPERTS, LANES), jnp.int32)],
        compiler_params=_cparams(("arbitrary",)),
        name="merge",
    )(gf, gb, rf, rb, p, p, p, p, x2, g1, sh2, sc2, gn, nf, wg, wr, wo, wrt, brt, upper)


GATHER_UNROLL = 8


def _aligned(tok):
    off = tok * FEAT_TILES
    return off if isinstance(off, int) else pl.multiple_of(off, FEAT_TILES)


def _token_copy(src_hbm, dst, sem, src_tok, dst_tok):
    return pltpu.make_async_copy(src_hbm.at[pl.ds(_aligned(src_tok), FEAT_TILES), :],
                                 dst.at[pl.ds(_aligned(dst_tok), FEAT_TILES), :], sem)


def _wait_tokens(src_hbm, dst, sem):
    pltpu.make_async_copy(src_hbm.at[pl.ds(0, dst.shape[0]), :], dst, sem).wait()


GATHER_BUFS = 3
GATHER_AHEAD = GATHER_BUFS - 1


def _issue_rows(n, start_row, unrolled):
    if unrolled:
        for r in range(n):
            start_row(r, r % 2)
    else:
        def body(g, carry):
            for u in range(GATHER_UNROLL):
                start_row(g * GATHER_UNROLL + u, u % 2)
            return carry
        lax.fori_loop(0, n // GATHER_UNROLL, body, 0)


def _expert_body(be_ref, src0_ref, stok_ref, nused_ref, hn_hbm, wg_ref, wu_ref, wd_ref, y_ref, *scratch):
    i = pl.program_id(0)
    nused = nused_ref[0]
    bufs, sems = scratch[:GATHER_BUFS], scratch[GATHER_BUFS]
    last = stok_ref.shape[0] - 1

    def issue(block, s, unrolled):
        base = src0_ref[block]

        def start_row(r, priority):
            tok = stok_ref[jnp.minimum(base + r, last)]
            _token_copy(hn_hbm, bufs[s], sems.at[s], tok, r).start(priority=priority)
        _issue_rows(MOE_BLOCK, start_row, unrolled)

    @pl.when(i == 0)
    def _():
        for b in range(GATHER_AHEAD):
            issue(b, b, False)

    for cur in range(GATHER_BUFS):
        mine = i % GATHER_BUFS == cur

        @pl.when(jnp.logical_and(mine, i < nused))
        def _():
            _wait_tokens(hn_hbm, bufs[cur], sems.at[cur])
            issue(i + GATHER_AHEAD, (cur + GATHER_AHEAD) % GATHER_BUFS, True)
            xb = _load_token_tiled(bufs[cur], MOE_BLOCK).astype(BF16)
            gate = _dot(xb, wg_ref[0].astype(BF16))
            up = _dot(xb, wu_ref[0].astype(BF16))
            hid = (_silu(gate) * up).astype(BF16)
            _store_token_tiled(y_ref, _dot(hid, wd_ref[0].astype(BF16)))

        @pl.when(jnp.logical_and(mine, jnp.logical_and(i >= nused, i < nused + GATHER_AHEAD)))
        def _():
            _wait_tokens(hn_hbm, bufs[cur], sems.at[cur])

    @pl.when(i >= nused)
    def _():
        y_ref[...] = jnp.zeros_like(y_ref)


def _experts(block_e, src0, stok, nused, hn, w_gate, w_up, w_down, nblk):
    wmap = lambda i, be, s0, st, nu: (be[i], 0, 0)
    rows = MOE_BLOCK * FEAT_TILES
    return pl.pallas_call(
        _expert_body,
        grid_spec=pltpu.PrefetchScalarGridSpec(
            num_scalar_prefetch=4,
            grid=(nblk,),
            in_specs=[pl.BlockSpec(memory_space=pl.ANY),
                      pl.BlockSpec((1, D_MODEL, EXPERT_FF), wmap),
                      pl.BlockSpec((1, D_MODEL, EXPERT_FF), wmap),
                      pl.BlockSpec((1, EXPERT_FF, D_MODEL), wmap)],
            out_specs=pl.BlockSpec((rows, LANES), lambda i, be, s0, st, nu: (i, 0)),
            scratch_shapes=[pltpu.VMEM((rows, LANES), F32)] * GATHER_BUFS
                           + [pltpu.SemaphoreType.DMA((GATHER_BUFS,))]),
        out_shape=jax.ShapeDtypeStruct((nblk * rows, LANES), F32),
        compiler_params=_cparams(("arbitrary",)),
        name="experts",
    )(block_e, src0, stok, nused, hn, w_gate, w_up, w_down)


def _combine_body(pos_ref, y_hbm, h_ref, ew_ref, g2_ref, nf_ref, o_ref, *scratch):
    i = pl.program_id(0)
    n = pl.num_programs(0)
    tm = COMBINE_TM
    t = pos_ref.shape[0] // TOP_K
    bufs, sems = scratch[:GATHER_BUFS], scratch[GATHER_BUFS]

    def issue(tile, s, unrolled):
        def start_row(j, priority):
            k, r = j % TOP_K, j // TOP_K
            _token_copy(y_hbm, bufs[s].at[k], sems.at[s], pos_ref[k * t + tile * tm + r], r).start(priority=priority)
        _issue_rows(tm * TOP_K, start_row, unrolled)

    def wait(s):
        for k in range(TOP_K):
            _wait_tokens(y_hbm, bufs[s].at[k], sems.at[s])

    @pl.when(i == 0)
    def _():
        for b in range(GATHER_AHEAD):
            issue(b, b, False)

    for cur in range(GATHER_BUFS):
        @pl.when(i % GATHER_BUFS == cur)
        def _():
            wait(cur)
            issue(jnp.minimum(i + GATHER_AHEAD, n - 1), (cur + GATHER_AHEAD) % GATHER_BUFS, True)
            wt = jnp.concatenate([ew_ref[...]] * (LANES // 8), axis=0).T
            moe = (wt[:, 0:1] * _load_token_tiled(bufs[cur].at[0], tm)
                   + wt[:, 1:2] * _load_token_tiled(bufs[cur].at[1], tm))
            h = h_ref[...] + g2_ref[0] * moe
            ms = jnp.mean(h * h, axis=-1, keepdims=True)
            o_ref[...] = h * lax.rsqrt(ms + NORM_EPS) * nf_ref[...]

        @pl.when(jnp.logical_and(i % GATHER_BUFS == cur, i == n - 1))
        def _():
            for ahead in range(1, GATHER_BUFS):
                wait((cur + ahead) % GATHER_BUFS)


def _combine(pos, y_pad, h, ew, g2, nf, seq):
    t = h.shape[0]
    tm = COMBINE_TM
    tiles_per_batch = seq // tm
    ybuf = pltpu.VMEM((TOP_K, tm * FEAT_TILES, LANES), F32)
    return pl.pallas_call(
        _combine_body,
        grid_spec=pltpu.PrefetchScalarGridSpec(
            num_scalar_prefetch=1,
            grid=(t // tm,),
            in_specs=[pl.BlockSpec(memory_space=pl.ANY),
                      pl.BlockSpec((tm, D_MODEL), lambda i, pos: (i, 0)),
                      pl.BlockSpec((8, tm), lambda i, pos: (0, i)),
                      pl.BlockSpec((1, 1, D_MODEL), lambda i, pos: (i // tiles_per_batch, 0, 0)),
                      pl.BlockSpec((1, D_MODEL), lambda i, pos: (0, 0))],
            out_specs=pl.BlockSpec((tm, D_MODEL), lambda i, pos: (i, 0)),
            scratch_shapes=[ybuf] * GATHER_BUFS + [pltpu.SemaphoreType.DMA((GATHER_BUFS,))]),
        out_shape=jax.ShapeDtypeStruct((t, D_MODEL), F32),
        compiler_params=_cparams(("arbitrary",)),
        name="combine",
    )(pos, y_pad, h, ew, g2, nf)


def _dispatch_indices(ids, ranks, counts, t):
    a = t * TOP_K
    nblk = a // MOE_BLOCK + N_EXPERTS + GATHER_AHEAD
    experts = jnp.arange(N_EXPERTS, dtype=jnp.int32)
    starts = jnp.cumsum(counts) - counts
    padded = (counts + MOE_BLOCK - 1) // MOE_BLOCK * MOE_BLOCK
    pends = jnp.cumsum(padded)
    pstarts = pends - padded
    block_start = jnp.arange(nblk, dtype=jnp.int32) * MOE_BLOCK
    block_e = jnp.minimum(jnp.sum((block_start[:, None] >= pends[None, :]).astype(jnp.int32), axis=1),
                          N_EXPERTS - 1)
    of_block = (block_e[:, None] == experts[None, :]).astype(jnp.int32)
    src0 = block_start + jnp.sum(of_block * (starts - pstarts)[None, :], axis=1)
    nused = pends[-1:] // MOE_BLOCK
    tok = jnp.arange(t, dtype=jnp.int32)[None, :]
    slot = jnp.arange(TOP_K, dtype=jnp.int32)[:, None]
    key = ids * a + tok * TOP_K + slot
    stok = (jnp.sort(key.reshape(a)) % a) // TOP_K
    ids_d = ids.reshape(a // LANES, LANES)
    pstart_of = jnp.zeros_like(ids_d)
    for e in range(N_EXPERTS):
        pstart_of = jnp.where(ids_d == e, pstarts[e], pstart_of)
    pos = (ranks.reshape(a // LANES, LANES) + pstart_of).reshape(a)
    i32 = lambda v: v.astype(jnp.int32)
    return i32(block_e), i32(src0), i32(stok), i32(nused), i32(pos), nblk


def kernel(x, c, ctx, c_ctx, w_ada, b_ada, norm_mix, norm_ffn, w_in, gla_lr_w, gla_lr_b, gla_norm,
           w_branch_gla, w_branch_ret, w_out, w_router_group, b_router_group, w_router_expert,
           b_router_expert, w_expert_gate, w_expert_up, w_expert_down, norm_final):
    batch, seq, d = x.shape
    ctx_len = ctx.shape[1]
    assert d == D_MODEL and w_ada.shape[0] == 1, "single-layer block with D_MODEL features"
    t = batch * seq

    c8 = jnp.zeros((8, d), F32).at[:batch].set(c).at[batch].set(c_ctx)
    mod = _ada(c8, w_ada[0], b_ada[0][None, :])
    sh1, sc1, g1, sh2, sc2, g2 = [mod[:, d * i:d * (i + 1)] for i in range(6)]
    lat = lambda m: m[:batch, None, :]
    cx = lambda m: m[batch:batch + 1, None, :]

    w_main, w_lr = _wprep(jnp.swapaxes(w_in[0], 0, 1))
    nm = norm_mix[0][None, :]

    p_ctx, lr_ctx = _proj(ctx.reshape(batch * ctx_len, d), cx(sh1), cx(sc1), nm, w_main, w_lr,
                          tm=ctx_len, rows_per_mod=batch * ctx_len)
    x2 = x.reshape(t, d)
    p_lat, lr_lat = _proj(x2, lat(sh1), lat(sc1), nm, w_main, w_lr, tm=PROJ_TM, rows_per_mod=seq)

    wlr_f = jnp.zeros((LANES, HEADS * DK), F32).at[:GLA_RANK].set(gla_lr_w[0, 0])
    wlr_b = jnp.zeros((LANES, HEADS * DK), F32).at[GLA_RANK:2 * GLA_RANK].set(gla_lr_w[0, 1])
    blr_f, blr_b = gla_lr_b[0, 0][None, :], gla_lr_b[0, 1][None, :]
    zero_state = jnp.zeros((batch, HEADS, DK, DV), F32)

    _, _, gs_f, gs_b = _gla(p_ctx, lr_ctx, wlr_f, blr_f, wlr_b, blr_b, zero_state, zero_state,
                            batch, ctx_len, tb=ctx_len)
    gla_f, gla_b, _, _ = _gla(p_lat, lr_lat, wlr_f, blr_f, wlr_b, blr_b, gs_f, gs_b, batch, seq, tb=SCAN_TB)

    ones = jnp.ones((ctx_len, DK), F32)
    _, _, rs_f, rs_b = _ret(p_ctx, ones, jnp.zeros_like(ones), zero_state, zero_state, batch, ctx_len, tb=ctx_len)
    cos, sin = _rope_tables(seq)
    ret_f, ret_b, _, _ = _ret(p_lat, cos, sin, rs_f, rs_b, batch, seq, tb=SCAN_TB)

    wrt = jnp.zeros((d, ROUTER_W), F32)
    wrt = wrt.at[:, :N_GROUPS].set(w_router_group[0]).at[:, EXP_ROW0:EXP_ROW0 + N_EXPERTS].set(w_router_expert[0])
    wrt_hi = wrt.astype(BF16)
    wrt = jnp.concatenate([wrt_hi, (wrt - wrt_hi.astype(F32)).astype(BF16)], axis=1)
    brt = jnp.zeros((1, ROUTER_W), F32)
    brt = brt.at[0, :N_GROUPS].set(b_router_group[0]).at[0, EXP_ROW0:EXP_ROW0 + N_EXPERTS].set(b_router_expert[0])
    h, hn, ids8, ew8, cnt = _merge(gla_f, gla_b, ret_f, ret_b, p_lat, x2, lat(g1), lat(sh2), lat(sc2),
                                   jnp.tile(gla_norm[0], HEADS)[None, :], norm_ffn[0][None, :],
                                   w_branch_gla[0].astype(BF16), w_branch_ret[0].astype(BF16),
                                   w_out[0].astype(BF16), wrt, brt, seq, tm=MERGE_TM)

    block_e, src0, stok, nused, pos, nblk = _dispatch_indices(ids8[:TOP_K], ids8[TOP_K:2 * TOP_K], cnt[:, 0], t)
    y_pad = _experts(block_e, src0, stok, nused, hn, w_expert_gate[0], w_expert_up[0], w_expert_down[0], nblk)
    out = _combine(pos, y_pad, h, ew8, lat(g2), norm_final[None, :], seq)
    return out.reshape(batch, seq, d)
```

```python
import functools

import numpy as np
import jax
import jax.numpy as jnp
from jax import lax
from jax.experimental import pallas as pl
from jax.experimental.pallas import tpu as pltpu

F32 = jnp.float32
BF16 = jnp.bfloat16

D_MODEL = 1024
GRID_W = 64
HEADS = 4
DK = 128
DV = 256
GLA_RANK = 16
GLA_TAU = 16.0
GLA_CHUNK = 64
GLA_SUB = 16
RET_CHUNK = 128
ROPE_BASE = 10000.0
N_GROUPS = 4
EXPERTS_PER_GROUP = 8
N_EXPERTS = N_GROUPS * EXPERTS_PER_GROUP
TOP_K = 2
EXPERT_FF = 256
MOE_BLOCK = 128
NORM_EPS = 1e-6

COL_GQ, COL_GK, COL_GV, COL_GG = 0, 512, 1024, 2048
COL_RQ, COL_RK, COL_RV, COL_RG = 3072, 3584, 4096, 5120
COL_MG, COL_MR = 6144, 7168
PROJ_W = 8192
LANES = 128
FEAT_TILES = D_MODEL // LANES
ROUTER_W = 128
EXP_ROW0 = 8

VMEM_LIMIT = 56 * 1024 * 1024

PROJ_TM = 1024
PROJ_TN = 2048
SCAN_TB = 1024
MERGE_TM = 256
MERGE_SPLIT = 1
COMBINE_TM = 256


def _cparams(sem):
    return pltpu.CompilerParams(dimension_semantics=sem, vmem_limit_bytes=VMEM_LIMIT)


def _sigmoid(x):
    return 1.0 / (1.0 + jnp.exp(-x))


def _silu(x):
    return x * _sigmoid(x)


def _dot(a, b):
    return jnp.dot(a, b, preferred_element_type=F32)


def _dot_nt(a, b):
    return lax.dot_general(a, b, (((1,), (1,)), ((), ())), preferred_element_type=F32)


def _ada_body(c_ref, w_ref, b_ref, o_ref):
    s = _silu(c_ref[...])
    o_ref[...] = _dot(s.astype(BF16), w_ref[...].astype(BF16)) + b_ref[...]


def _ada(c8, w, b):
    n = w.shape[1]
    tn = 1536
    return pl.pallas_call(
        _ada_body,
        grid=(n // tn,),
        in_specs=[pl.BlockSpec((8, D_MODEL), lambda j: (0, 0)),
                  pl.BlockSpec((D_MODEL, tn), lambda j: (0, j)),
                  pl.BlockSpec((1, tn), lambda j: (0, j))],
        out_specs=pl.BlockSpec((8, tn), lambda j: (0, j)),
        out_shape=jax.ShapeDtypeStruct((8, n), F32),
        compiler_params=_cparams(("arbitrary",)),
        name="ada",
    )(c8, w, b)


LR_COL0 = COL_GG + HEADS * DV
WPREP_ROWS = 1024


def _wprep_body(w_ref, wlr_ref, main_ref, lr_ref):
    main_ref[...] = w_ref[...].astype(BF16)
    pad = jnp.zeros((LANES - 2 * GLA_RANK, D_MODEL), F32)
    lr_ref[...] = jnp.concatenate([wlr_ref[...], pad], axis=0).astype(BF16)


def _wprep(wt):
    steps_before = LR_COL0 // WPREP_ROWS

    def src_row(i):
        row = jnp.where(i < steps_before, i * WPREP_ROWS, i * WPREP_ROWS + 2 * GLA_RANK)
        return (pl.multiple_of(row, 2 * GLA_RANK), 0)

    return pl.pallas_call(
        _wprep_body,
        grid=(PROJ_W // WPREP_ROWS,),
        in_specs=[pl.BlockSpec((pl.Element(WPREP_ROWS), pl.Element(D_MODEL)), src_row),
                  pl.BlockSpec((pl.Element(2 * GLA_RANK), pl.Element(D_MODEL)), lambda i: (LR_COL0, 0))],
        out_specs=[pl.BlockSpec((WPREP_ROWS, D_MODEL), lambda i: (i, 0)),
                   pl.BlockSpec((LANES, D_MODEL), lambda i: (0, 0))],
        out_shape=[jax.ShapeDtypeStruct((PROJ_W, D_MODEL), BF16), jax.ShapeDtypeStruct((LANES, D_MODEL), BF16)],
        compiler_params=_cparams(("arbitrary",)),
        name="wprep",
    )(wt, wt)


def _proj_body(x_ref, sh_ref, sc_ref, g_ref, w_ref, wlr_ref, o_ref, lr_ref, h_ref):
    @pl.when(pl.program_id(1) == 0)
    def _():
        x = x_ref[...]
        ms = jnp.mean(x * x, axis=-1, keepdims=True)
        y = x * lax.rsqrt(ms + NORM_EPS) * g_ref[...]
        hb = (y * (1.0 + sc_ref[0]) + sh_ref[0]).astype(BF16)
        h_ref[...] = hb
        lr_ref[...] = _dot_nt(hb, wlr_ref[...])

    o_ref[...] = _dot_nt(h_ref[...], w_ref[...]).astype(BF16)


def _proj(x2, sh, sc, g, w_main, w_lr, tm, rows_per_mod):
    t = x2.shape[0]
    tn = PROJ_TN
    tiles_per_mod = rows_per_mod // tm
    mod_map = lambda i, j: (i // tiles_per_mod, 0, 0)
    return pl.pallas_call(
        _proj_body,
        grid=(t // tm, PROJ_W // tn),
        in_specs=[pl.BlockSpec((tm, D_MODEL), lambda i, j: (i, 0)),
                  pl.BlockSpec((1, 1, D_MODEL), mod_map),
                  pl.BlockSpec((1, 1, D_MODEL), mod_map),
                  pl.BlockSpec((1, D_MODEL), lambda i, j: (0, 0)),
                  pl.BlockSpec((tn, D_MODEL), lambda i, j: (j, 0)),
                  pl.BlockSpec((LANES, D_MODEL), lambda i, j: (0, 0))],
        out_specs=[pl.BlockSpec((tm, tn), lambda i, j: (i, j)),
                   pl.BlockSpec((tm, LANES), lambda i, j: (i, 0))],
        out_shape=[jax.ShapeDtypeStruct((t, PROJ_W), BF16),
                   jax.ShapeDtypeStruct((t, LANES), F32)],
        scratch_shapes=[pltpu.VMEM((tm, D_MODEL), BF16)],
        compiler_params=_cparams(("parallel", "arbitrary")),
        name="proj",
    )(x2, sh, sc, g, w_main, w_lr)


def _gla_consts():
    c, s = GLA_CHUNK, GLA_SUB
    msk = {}
    for rev in (False, True):
        cols = []
        for i in range(c // s):
            keys = np.arange(c)[_key_rows(i, rev)][None, :]
            t = np.arange(c)[:, None]
            visible = (keys > t) if rev else (keys <= t)
            cols.append(np.where(t // s == i, visible, False))
        msk[rev] = np.concatenate(cols, axis=1).astype(np.float32)
    return msk


def _key_rows(i, rev):
    return slice(GLA_SUB * i, GLA_CHUNK) if rev else slice(0, GLA_SUB * (i + 1))


def _log_gate(lr, w, b):
    z = _dot(lr.astype(BF16), w.astype(BF16)) + b
    return -(jnp.maximum(-z, 0.0) + jnp.log(1.0 + jnp.exp(-jnp.abs(z)))) * (1.0 / GLA_TAU)


def _subchunk_scan(la, rev):
    c = la.shape[0]
    pos = lax.broadcasted_iota(jnp.int32, la.shape, 0) % GLA_SUB
    w = la
    step = 1
    while step < GLA_SUB:
        if rev:
            w = w + jnp.where(pos < GLA_SUB - step, pltpu.roll(w, c - step, 0), 0.0)
        else:
            w = w + jnp.where(pos >= step, pltpu.roll(w, step, 0), 0.0)
        step *= 2
    return w


def _gla_chunk(q, k, v, la, state, mask, rev):
    c, s = GLA_CHUNK, GLA_SUB
    nsub = c // s
    w = _subchunk_scan(la, rev)
    tot = [w[s * i:s * i + 1] if rev else w[s * (i + 1) - 1:s * (i + 1)] for i in range(nsub)]
    anchors = [None] * nsub
    acc = jnp.zeros_like(tot[0])
    for i in (reversed(range(nsub)) if rev else range(nsub)):
        anchors[i] = acc
        acc = acc + tot[i]
    b_end = acc
    ref = jnp.concatenate([jnp.broadcast_to(a, (s, DK)) for a in anchors], axis=0)
    b = w + ref

    qt = q.astype(F32) * (DK ** -0.5) * jnp.exp(w)
    kf = k.astype(F32)
    inter = _dot((qt * jnp.exp(ref)).astype(BF16), state.astype(BF16))

    kstack = jnp.concatenate([kf[_key_rows(i, rev)] * jnp.exp(anchors[i] - b[_key_rows(i, rev)])
                              for i in range(nsub)], axis=0).astype(BF16)
    vstack = jnp.concatenate([v[_key_rows(i, rev)] for i in range(nsub)], axis=0)
    scores = jnp.where(mask != 0.0, _dot_nt(qt.astype(BF16), kstack), 0.0)
    out = inter + _dot(scores.astype(BF16), vstack)

    kst = kf * jnp.exp(b_end - b)
    xt = jnp.concatenate([kst, jnp.broadcast_to(jnp.exp(b_end), (c, DK))], axis=0).T
    new_state = xt[:, c:c + 1] * state + _dot(xt[:, :c].astype(BF16), v)
    return out, new_state


def _gla_body(qf_ref, kf_ref, vf_ref, lrf_ref, qb_ref, kb_ref, vb_ref, lrb_ref,
              wf_ref, bf_ref, wb_ref, bb_ref, mskf_ref, mskb_ref,
              s0f_ref, s0b_ref, of_ref, ob_ref, sf_ref, sb_ref, *, nchunks):
    @pl.when(pl.program_id(2) == 0)
    def _():
        sf_ref[...] = s0f_ref[...]
        sb_ref[...] = s0b_ref[...]

    c = GLA_CHUNK
    la_f = _log_gate(lrf_ref[...], wf_ref[...], bf_ref[...])
    la_b = _log_gate(lrb_ref[...], wb_ref[...], bb_ref[...])
    mskf, mskb = mskf_ref[...], mskb_ref[...]
    st_f = sf_ref[0, 0]
    st_b = sb_ref[0, 0]
    for n in range(nchunks):
        rf = slice(c * n, c * (n + 1))
        o, st_f = _gla_chunk(qf_ref[rf, :], kf_ref[rf, :], vf_ref[rf, :], la_f[rf, :], st_f, mskf, False)
        of_ref[rf, :] = o.astype(BF16)
        m = nchunks - 1 - n
        rb = slice(c * m, c * (m + 1))
        o, st_b = _gla_chunk(qb_ref[rb, :], kb_ref[rb, :], vb_ref[rb, :], la_b[rb, :], st_b, mskb, True)
        ob_ref[rb, :] = o.astype(BF16)
    sf_ref[0, 0] = st_f
    sb_ref[0, 0] = st_b


def _gla(p, lr, wlr_f, blr_f, wlr_b, blr_b, s0f, s0b, batch, seq, tb):
    nb = seq // tb
    msk = _gla_consts()
    mskf, mskb = jnp.asarray(msk[False]), jnp.asarray(msk[True])

    def fwd(col0, width):
        return lambda b, h, i: (b * nb + i, col0 // width + h)

    def bwd(col0, width):
        return lambda b, h, i: (b * nb + nb - 1 - i, col0 // width + h)

    const2 = lambda b, h, i: (0, 0)
    headcol = lambda b, h, i: (0, h)
    st_map = lambda b, h, i: (b, h, 0, 0)
    st_spec = pl.BlockSpec((1, 1, DK, DV), st_map)
    st_shape = jax.ShapeDtypeStruct((batch, HEADS, DK, DV), F32)
    o_shape = jax.ShapeDtypeStruct((batch * seq, HEADS * DV), BF16)
    return pl.pallas_call(
        functools.partial(_gla_body, nchunks=tb // GLA_CHUNK),
        grid=(batch, HEADS, nb),
        in_specs=[pl.BlockSpec((tb, DK), fwd(COL_GQ, DK)),
                  pl.BlockSpec((tb, DK), fwd(COL_GK, DK)),
                  pl.BlockSpec((tb, DV), fwd(COL_GV, DV)),
                  pl.BlockSpec((tb, LANES), lambda b, h, i: (b * nb + i, 0)),
                  pl.BlockSpec((tb, DK), bwd(COL_GQ, DK)),
                  pl.BlockSpec((tb, DK), bwd(COL_GK, DK)),
                  pl.BlockSpec((tb, DV), bwd(COL_GV, DV)),
                  pl.BlockSpec((tb, LANES), lambda b, h, i: (b * nb + nb - 1 - i, 0)),
                  pl.BlockSpec((LANES, DK), headcol),
                  pl.BlockSpec((1, DK), headcol),
                  pl.BlockSpec((LANES, DK), headcol),
                  pl.BlockSpec((1, DK), headcol),
                  pl.BlockSpec(mskf.shape, const2),
                  pl.BlockSpec(mskb.shape, const2),
                  st_spec, st_spec],
        out_specs=[pl.BlockSpec((tb, DV), lambda b, h, i: (b * nb + i, h)),
                   pl.BlockSpec((tb, DV), lambda b, h, i: (b * nb + nb - 1 - i, h)),
                   st_spec, st_spec],
        out_shape=[o_shape, o_shape, st_shape, st_shape],
        compiler_params=_cparams(("parallel", "parallel", "arbitrary")),
        name="gla",
    )(p, p, p, lr, p, p, p, lr, wlr_f, blr_f, wlr_b, blr_b, mskf, mskb, s0f, s0b)


def _ret_consts():
    c = RET_CHUNK
    hh = np.arange(HEADS, dtype=np.float64)
    lg = {False: np.log1p(-np.exp2(-5.0 - hh)), True: np.log1p(-np.exp2(-5.5 - hh))}
    t = np.arange(c, dtype=np.float64)[:, None]
    u = np.arange(c, dtype=np.float64)[None, :]
    pos = np.arange(c, dtype=np.float64)
    out = {}
    for rev in (False, True):
        g = lg[rev][:, None, None]
        if not rev:
            dmat = np.where(u <= t, np.exp((t - u) * g), 0.0)
            qd = np.exp((pos + 1.0)[None, :] * lg[rev][:, None])
            kd = np.exp((c - 1.0 - pos)[None, :] * lg[rev][:, None])
        else:
            dmat = np.where(u > t, np.exp((u - t) * g), 0.0)
            qd = np.exp((c - pos)[None, :] * lg[rev][:, None])
            kd = np.exp(pos[None, :] * lg[rev][:, None])
        cd = np.exp(c * lg[rev])
        out[rev] = (dmat.astype(np.float32),
                    np.repeat(qd[:, :, None], DV, axis=2).astype(np.float32),
                    np.repeat(kd[:, :, None], DK, axis=2).astype(np.float32),
                    np.repeat(cd[:, None, None], DV, axis=2).astype(np.float32))
    return out


def _rope(x, cos, sin):
    lane = lax.broadcasted_iota(jnp.int32, x.shape, 1)
    partner = jnp.where((lane % 64) < 32, pltpu.roll(x, DK - 32, 1), pltpu.roll(x, 32, 1))
    return x * cos + partner * sin


def _ret_chunk(q, k, v, cos, sin, state, dmat, qd, kd, cd):
    qr = _rope(q.astype(F32), cos, sin).astype(BF16)
    kr = _rope(k.astype(F32) * (DK ** -0.5), cos, sin)
    inter = _dot(qr, state.astype(BF16)) * qd
    scores = _dot_nt(qr, kr.astype(BF16)) * dmat
    out = inter + _dot(scores.astype(BF16), v)
    new_state = cd * state + _dot((kr * kd).T.astype(BF16), v)
    return out, new_state


def _ret_body(qf_ref, kf_ref, vf_ref, cosf_ref, sinf_ref, qb_ref, kb_ref, vb_ref, cosb_ref, sinb_ref,
              dmf_ref, qdf_ref, kdf_ref, cdf_ref, dmb_ref, qdb_ref, kdb_ref, cdb_ref,
              s0f_ref, s0b_ref, of_ref, ob_ref, sf_ref, sb_ref, *, nchunks):
    @pl.when(pl.program_id(2) == 0)
    def _():
        sf_ref[...] = s0f_ref[...]
        sb_ref[...] = s0b_ref[...]

    c = RET_CHUNK
    cf = (dmf_ref[0], qdf_ref[0], kdf_ref[0], cdf_ref[0])
    cb = (dmb_ref[0], qdb_ref[0], kdb_ref[0], cdb_ref[0])
    st_f = sf_ref[0, 0]
    st_b = sb_ref[0, 0]
    for n in range(nchunks):
        rf = slice(c * n, c * (n + 1))
        o, st_f = _ret_chunk(qf_ref[rf, :], kf_ref[rf, :], vf_ref[rf, :], cosf_ref[rf, :], sinf_ref[rf, :],
                             st_f, *cf)
        of_ref[rf, :] = o.astype(BF16)
        m = nchunks - 1 - n
        rb = slice(c * m, c * (m + 1))
        o, st_b = _ret_chunk(qb_ref[rb, :], kb_ref[rb, :], vb_ref[rb, :], cosb_ref[rb, :], sinb_ref[rb, :],
                             st_b, *cb)
        ob_ref[rb, :] = o.astype(BF16)
    sf_ref[0, 0] = st_f
    sb_ref[0, 0] = st_b


def _ret(p, cos, sin, s0f, s0b, batch, seq, tb):
    nb = seq // tb
    consts = _ret_consts()

    def fwd(col0, width):
        return lambda b, h, i: (b * nb + i, col0 // width + h)

    def bwd(col0, width):
        return lambda b, h, i: (b * nb + nb - 1 - i, col0 // width + h)

    head3 = lambda b, h, i: (h, 0, 0)
    st_spec = pl.BlockSpec((1, 1, DK, DV), lambda b, h, i: (b, h, 0, 0))
    st_shape = jax.ShapeDtypeStruct((batch, HEADS, DK, DV), F32)
    o_shape = jax.ShapeDtypeStruct((batch * seq, HEADS * DV), BF16)
    const_specs = []
    const_args = []
    for rev in (False, True):
        for a in consts[rev]:
            const_specs.append(pl.BlockSpec((1,) + a.shape[1:], head3))
            const_args.append(jnp.asarray(a))
    return pl.pallas_call(
        functools.partial(_ret_body, nchunks=tb // RET_CHUNK),
        grid=(batch, HEADS, nb),
        in_specs=[pl.BlockSpec((tb, DK), fwd(COL_RQ, DK)),
                  pl.BlockSpec((tb, DK), fwd(COL_RK, DK)),
                  pl.BlockSpec((tb, DV), fwd(COL_RV, DV)),
                  pl.BlockSpec((tb, DK), lambda b, h, i: (i, 0)),
                  pl.BlockSpec((tb, DK), lambda b, h, i: (i, 0)),
                  pl.BlockSpec((tb, DK), bwd(COL_RQ, DK)),
                  pl.BlockSpec((tb, DK), bwd(COL_RK, DK)),
                  pl.BlockSpec((tb, DV), bwd(COL_RV, DV)),
                  pl.BlockSpec((tb, DK), lambda b, h, i: (nb - 1 - i, 0)),
                  pl.BlockSpec((tb, DK), lambda b, h, i: (nb - 1 - i, 0))]
                 + const_specs + [st_spec, st_spec],
        out_specs=[pl.BlockSpec((tb, DV), lambda b, h, i: (b * nb + i, h)),
                   pl.BlockSpec((tb, DV), lambda b, h, i: (b * nb + nb - 1 - i, h)),
                   st_spec, st_spec],
        out_shape=[o_shape, o_shape, st_shape, st_shape],
        compiler_params=_cparams(("parallel", "parallel", "arbitrary")),
        name="ret",
    )(p, p, p, cos, sin, p, p, p, cos, sin, *const_args, s0f, s0b)


def _rope_tables(seq):
    n = DK // 4
    inv = np.float32(ROPE_BASE) ** (-np.arange(n, dtype=np.float32) / np.float32(n))
    pos = np.arange(seq)
    ar = (pos // GRID_W).astype(np.float32)[:, None] * inv[None, :]
    ac = (pos % GRID_W).astype(np.float32)[:, None] * inv[None, :]
    cos = np.concatenate([np.cos(ar), np.cos(ar), np.cos(ac), np.cos(ac)], axis=1)
    sin = np.concatenate([-np.sin(ar), np.sin(ar), -np.sin(ac), np.sin(ac)], axis=1)
    return jnp.asarray(cos, F32), jnp.asarray(sin, F32)


def _route(logits_t):
    g = [logits_t[i:i + 1] for i in range(N_GROUPS)]
    gmax = jnp.maximum(jnp.maximum(g[0], g[1]), jnp.maximum(g[2], g[3]))
    gsel = jnp.where(g[0] == gmax, 0, jnp.where(g[1] == gmax, 1, jnp.where(g[2] == gmax, 2, 3)))
    gsum = (jnp.exp(g[0] - gmax) + jnp.exp(g[1] - gmax)) + (jnp.exp(g[2] - gmax) + jnp.exp(g[3] - gmax))
    gw = 1.0 / gsum
    e = [logits_t[EXP_ROW0 + EXPERTS_PER_GROUP * i:EXP_ROW0 + EXPERTS_PER_GROUP * (i + 1)]
         for i in range(N_GROUPS)]
    el = jnp.where(gsel == 0, e[0], jnp.where(gsel == 1, e[1], jnp.where(gsel == 2, e[2], e[3])))
    row = lax.broadcasted_iota(jnp.int32, el.shape, 0).astype(F32)
    none = float(EXPERTS_PER_GROUP)
    m1 = jnp.max(el, axis=0, keepdims=True)
    i1 = jnp.min(jnp.where(el == m1, row, none), axis=0, keepdims=True)
    el2 = jnp.where(row == i1, -jnp.inf, el)
    m2 = jnp.max(el2, axis=0, keepdims=True)
    i2 = jnp.min(jnp.where(el2 == m2, row, none), axis=0, keepdims=True)
    r = jnp.exp(m2 - m1)
    w1 = gw / (1.0 + r)
    w2 = gw * r / (1.0 + r)
    base = gsel * EXPERTS_PER_GROUP
    ids = jnp.concatenate([base + i1.astype(jnp.int32), base + i2.astype(jnp.int32)], axis=0)
    return ids, jnp.concatenate([w1, w2], axis=0)


def _store_token_tiled(ref, val):
    m = val.shape[0]
    for j in range(FEAT_TILES):
        ref[pl.ds(j, m, stride=FEAT_TILES), :] = val[:, LANES * j:LANES * (j + 1)]


def _load_token_tiled(ref, m):
    return jnp.concatenate([ref[pl.ds(j, m, stride=FEAT_TILES), :] for j in range(FEAT_TILES)], axis=1)


def _merge_body(gf_ref, gb_ref, rf_ref, rb_ref, gg_ref, rg_ref, mg_ref, mr_ref, x_ref,
                g1_ref, sh2_ref, nfs_ref, gn_ref, wg_ref, wr_ref, wo_ref, wrt_ref, brt_ref, upper_ref,
                h_ref, hn_ref, ids_ref, ew_ref, cnt_ref, wg_s, wr_s, wo_s):
    @pl.when(pl.program_id(0) == 0)
    def _():
        cnt_ref[...] = jnp.zeros_like(cnt_ref)
        wg_s[...] = wg_ref[...]
        wr_s[...] = wr_ref[...]
        wo_s[...] = wo_ref[...]

    def rows_to_logits(r0, n):
        rows = pl.ds(r0, n)
        og = (gf_ref[rows, :] + gb_ref[rows, :]).astype(F32)
        orr = (rf_ref[rows, :] + rb_ref[rows, :]).astype(F32)
        gparts, rparts = [], []
        for hh in range(HEADS):
            seg = og[:, DV * hh:DV * (hh + 1)]
            ms = jnp.mean(seg * seg, axis=-1, keepdims=True)
            gparts.append(seg * lax.rsqrt(ms + NORM_EPS))
            seg = orr[:, DV * hh:DV * (hh + 1)]
            mu = jnp.mean(seg, axis=-1, keepdims=True)
            cen = seg - mu
            var = jnp.mean(cen * cen, axis=-1, keepdims=True)
            rparts.append(cen * lax.rsqrt(var + NORM_EPS))
        o_gla = (jnp.concatenate(gparts, axis=1) * gn_ref[...]).astype(BF16) * _silu(gg_ref[rows, :])
        o_ret = jnp.concatenate(rparts, axis=1).astype(BF16) * _silu(rg_ref[rows, :])
        y = (_sigmoid(mg_ref[rows, :]) * _dot(o_gla, wg_s[...]).astype(BF16)
             + _sigmoid(mr_ref[rows, :]) * _dot(o_ret, wr_s[...]).astype(BF16))
        h = x_ref[rows, :] + g1_ref[0] * _dot(y, wo_s[...])
        h_ref[rows, :] = h
        ms = jnp.mean(h * h, axis=-1, keepdims=True)
        hn = h * lax.rsqrt(ms + NORM_EPS) * nfs_ref[0] + sh2_ref[0]
        _store_token_tiled(hn_ref.at[pl.ds(r0 * FEAT_TILES, n * FEAT_TILES)], hn)
        hn_hi = hn.astype(BF16)
        hn_lo = (hn - hn_hi.astype(F32)).astype(BF16)
        both_w = _dot(hn_hi, wrt_ref[...])
        return (both_w[:, :ROUTER_W] + both_w[:, ROUTER_W:] + _dot(hn_lo, wrt_ref[:, :ROUTER_W])) + brt_ref[...]

    tm = x_ref.shape[0]
    n = tm // MERGE_SPLIT
    logits = jnp.concatenate([rows_to_logits(g * n, n) for g in range(MERGE_SPLIT)], axis=0)
    ids, ew = _route(logits.T)

    erow = lax.broadcasted_iota(jnp.int32, (N_EXPERTS, tm), 0)
    oh0 = jnp.where(erow == ids[0:1], 1.0, 0.0)
    oh1 = jnp.where(erow == ids[1:2], 1.0, 0.0)
    both = oh0 + oh1
    before = _dot(both.astype(BF16), upper_ref[...]) + cnt_ref[:, 0:1].astype(F32)
    rank0 = jnp.sum(oh0 * before, axis=0, keepdims=True)
    rank1 = jnp.sum(oh1 * before, axis=0, keepdims=True)
    total = cnt_ref[:, 0:1] + jnp.sum(both, axis=1, keepdims=True).astype(jnp.int32)
    cnt_ref[...] = jnp.broadcast_to(total, cnt_ref.shape)
    ids_ref[...] = jnp.concatenate([ids, rank0.astype(jnp.int32), rank1.astype(jnp.int32),
                                    jnp.zeros((4, tm), jnp.int32)], axis=0)
    ew_ref[...] = jnp.concatenate([ew, jnp.zeros((6, tm), F32)], axis=0)


def _merge(gf, gb, rf, rb, p, x2, g1, sh2, nfs, gn, wg, wr, wo, wrt, brt, seq, tm):
    t = x2.shape[0]
    tiles_per_batch = seq // tm
    row = lambda i: (i, 0)
    mod = lambda i: (i // tiles_per_batch, 0, 0)
    const = lambda i: (0, 0)
    tok = pl.BlockSpec((tm, D_MODEL), row)
    vec = pl.BlockSpec((1, D_MODEL), const)
    modspec = pl.BlockSpec((1, 1, D_MODEL), mod)
    wspec = pl.BlockSpec((D_MODEL, D_MODEL), const)
    upper = jnp.asarray(np.triu(np.ones((tm, tm), np.float32), 1), BF16)

    def pcol(col0):
        return pl.BlockSpec((tm, D_MODEL), lambda i: (i, col0 // D_MODEL))

    return pl.pallas_call(
        _merge_body,
        grid=(t // tm,),
        in_specs=[tok, tok, tok, tok, pcol(COL_GG), pcol(COL_RG), pcol(COL_MG), pcol(COL_MR), tok,
                  modspec, modspec, modspec, vec, wspec, wspec, wspec,
                  pl.BlockSpec((D_MODEL, 2 * ROUTER_W), const), pl.BlockSpec((1, ROUTER_W), const),
                  pl.BlockSpec((tm, tm), const)],
        out_specs=[tok, pl.BlockSpec((tm * FEAT_TILES, LANES), row),
                   pl.BlockSpec((8, tm), lambda i: (0, i)), pl.BlockSpec((8, tm), lambda i: (0, i)),
                   pl.BlockSpec((N_EXPERTS, LANES), const)],
        out_shape=[jax.ShapeDtypeStruct((t, D_MODEL), F32), jax.ShapeDtypeStruct((t * FEAT_TILES, LANES), F32),
                   jax.ShapeDtypeStruct((8, t), jnp.int32), jax.ShapeDtypeStruct((8, t), F32),
                   jax.ShapeDtypeStruct((N_EXPERTS, LANES), jnp.int32)],
        scratch_shapes=[pltpu.VMEM((D_MODEL, D_MODEL), BF16)] * 3,
        compiler_params=_cparams(("arbitrary",)),
        name="merge",
    )(gf, gb, rf, rb, p, p, p, p, x2, g1, sh2, nfs, gn, wg, wr, wo, wrt, brt, upper)


GATHER_UNROLL = 8


def _aligned(tok):
    off = tok * FEAT_TILES
    return off if isinstance(off, int) else pl.multiple_of(off, FEAT_TILES)


def _token_copy(src_hbm, dst, sem, src_tok, dst_tok):
    return pltpu.make_async_copy(src_hbm.at[pl.ds(_aligned(src_tok), FEAT_TILES), :],
                                 dst.at[pl.ds(_aligned(dst_tok), FEAT_TILES), :], sem)


def _wait_tokens(src_hbm, dst, sem):
    pltpu.make_async_copy(src_hbm.at[pl.ds(0, dst.shape[0]), :], dst, sem).wait()


GATHER_BUFS = 3
GATHER_AHEAD = GATHER_BUFS - 1


def _issue_rows(n, start_row, unrolled):
    if unrolled:
        for r in range(n):
            start_row(r, r % 2)
    else:
        def body(g, carry):
            for u in range(GATHER_UNROLL):
                start_row(g * GATHER_UNROLL + u, u % 2)
            return carry
        lax.fori_loop(0, n // GATHER_UNROLL, body, 0)


EXPERT_GROUP = 1


def _expert_body(be_ref, src0_ref, stok_ref, nused_ref, hn_hbm, *refs):
    nw = 3 * EXPERT_GROUP
    w_refs, y_ref, scratch = refs[:nw], refs[nw], refs[nw + 1:]
    i = pl.program_id(0)
    nsteps = (nused_ref[0] + EXPERT_GROUP - 1) // EXPERT_GROUP
    bufs, sems = scratch[:GATHER_BUFS], scratch[GATHER_BUFS]
    last = stok_ref.shape[0] - 1
    rows = MOE_BLOCK * FEAT_TILES

    def issue(step, s, unrolled):
        bases = [src0_ref[step * EXPERT_GROUP + g] for g in range(EXPERT_GROUP)]

        def start(g, r, priority):
            tok = stok_ref[jnp.minimum(bases[g] + r, last)]
            _token_copy(hn_hbm, bufs[s], sems.at[s], tok, g * MOE_BLOCK + r).start(priority=priority)

        if unrolled:
            _issue_rows(EXPERT_GROUP * MOE_BLOCK, lambda j, p: start(j // MOE_BLOCK, j % MOE_BLOCK, p), True)
        else:
            for g in range(EXPERT_GROUP):
                _issue_rows(MOE_BLOCK, lambda r, p, g=g: start(g, r, p), False)

    @pl.when(i == 0)
    def _():
        for b in range(GATHER_AHEAD):
            issue(b, b, False)

    for cur in range(GATHER_BUFS):
        mine = i % GATHER_BUFS == cur

        @pl.when(jnp.logical_and(mine, i < nsteps))
        def _():
            _wait_tokens(hn_hbm, bufs[cur], sems.at[cur])
            issue(i + GATHER_AHEAD, (cur + GATHER_AHEAD) % GATHER_BUFS, True)
            for g in range(EXPERT_GROUP):
                wg_ref, wu_ref, wd_ref = w_refs[3 * g:3 * g + 3]
                part = pl.ds(g * rows, rows)
                xb = _load_token_tiled(bufs[cur].at[part], MOE_BLOCK).astype(BF16)
                gate = _dot(xb, wg_ref[0].astype(BF16))
                up = _dot(xb, wu_ref[0].astype(BF16))
                hid = (_silu(gate) * up).astype(BF16)
                _store_token_tiled(y_ref.at[part], _dot(hid, wd_ref[0].astype(BF16)))

        @pl.when(jnp.logical_and(mine, jnp.logical_and(i >= nsteps, i < nsteps + GATHER_AHEAD)))
        def _():
            _wait_tokens(hn_hbm, bufs[cur], sems.at[cur])

    @pl.when(i >= nsteps)
    def _():
        y_ref[...] = jnp.zeros_like(y_ref)


def _experts(block_e, src0, stok, nused, hn, w_gate, w_up, w_down, nblk):
    rows = EXPERT_GROUP * MOE_BLOCK * FEAT_TILES
    w_specs, w_args = [], []
    for g in range(EXPERT_GROUP):
        wmap = lambda i, be, s0, st, nu, g=g: (be[i * EXPERT_GROUP + g], 0, 0)
        w_specs += [pl.BlockSpec((1, D_MODEL, EXPERT_FF), wmap), pl.BlockSpec((1, D_MODEL, EXPERT_FF), wmap),
                    pl.BlockSpec((1, EXPERT_FF, D_MODEL), wmap)]
        w_args += [w_gate, w_up, w_down]
    return pl.pallas_call(
        _expert_body,
        grid_spec=pltpu.PrefetchScalarGridSpec(
            num_scalar_prefetch=4,
            grid=(nblk // EXPERT_GROUP,),
            in_specs=[pl.BlockSpec(memory_space=pl.ANY)] + w_specs,
            out_specs=pl.BlockSpec((rows, LANES), lambda i, be, s0, st, nu: (i, 0)),
            scratch_shapes=[pltpu.VMEM((rows, LANES), F32)] * GATHER_BUFS
                           + [pltpu.SemaphoreType.DMA((GATHER_BUFS,))]),
        out_shape=jax.ShapeDtypeStruct((nblk // EXPERT_GROUP * rows, LANES), F32),
        compiler_params=_cparams(("arbitrary",)),
        name="experts",
    )(block_e, src0, stok, nused, hn, *w_args)


def _combine_body(pos_ref, y_hbm, h_ref, ew_ref, g2_ref, nf_ref, o_ref, *scratch):
    i = pl.program_id(0)
    n = pl.num_programs(0)
    tm = COMBINE_TM
    t = pos_ref.shape[0] // TOP_K
    bufs, sems = scratch[:GATHER_BUFS], scratch[GATHER_BUFS]

    def issue(tile, s, unrolled):
        def start_row(j, priority):
            k, r = j % TOP_K, j // TOP_K
            _token_copy(y_hbm, bufs[s].at[k], sems.at[s], pos_ref[k * t + tile * tm + r], r).start(priority=priority)
        _issue_rows(tm * TOP_K, start_row, unrolled)

    def wait(s):
        for k in range(TOP_K):
            _wait_tokens(y_hbm, bufs[s].at[k], sems.at[s])

    @pl.when(i == 0)
    def _():
        for b in range(GATHER_AHEAD):
            issue(b, b, False)

    for cur in range(GATHER_BUFS):
        @pl.when(i % GATHER_BUFS == cur)
        def _():
            wait(cur)
            issue(jnp.minimum(i + GATHER_AHEAD, n - 1), (cur + GATHER_AHEAD) % GATHER_BUFS, True)
            wt = jnp.concatenate([ew_ref[...]] * (LANES // 8), axis=0).T
            moe = (wt[:, 0:1] * _load_token_tiled(bufs[cur].at[0], tm)
                   + wt[:, 1:2] * _load_token_tiled(bufs[cur].at[1], tm))
            h = h_ref[...] + g2_ref[0] * moe
            ms = jnp.mean(h * h, axis=-1, keepdims=True)
            o_ref[...] = h * lax.rsqrt(ms + NORM_EPS) * nf_ref[...]

        @pl.when(jnp.logical_and(i % GATHER_BUFS == cur, i == n - 1))
        def _():
            for ahead in range(1, GATHER_BUFS):
                wait((cur + ahead) % GATHER_BUFS)


def _combine(pos, y_pad, h, ew, g2, nf, seq):
    t = h.shape[0]
    tm = COMBINE_TM
    tiles_per_batch = seq // tm
    ybuf = pltpu.VMEM((TOP_K, tm * FEAT_TILES, LANES), F32)
    return pl.pallas_call(
        _combine_body,
        grid_spec=pltpu.PrefetchScalarGridSpec(
            num_scalar_prefetch=1,
            grid=(t // tm,),
            in_specs=[pl.BlockSpec(memory_space=pl.ANY),
                      pl.BlockSpec((tm, D_MODEL), lambda i, pos: (i, 0)),
                      pl.BlockSpec((8, tm), lambda i, pos: (0, i)),
                      pl.BlockSpec((1, 1, D_MODEL), lambda i, pos: (i // tiles_per_batch, 0, 0)),
                      pl.BlockSpec((1, D_MODEL), lambda i, pos: (0, 0))],
            out_specs=pl.BlockSpec((tm, D_MODEL), lambda i, pos: (i, 0)),
            scratch_shapes=[ybuf] * GATHER_BUFS + [pltpu.SemaphoreType.DMA((GATHER_BUFS,))]),
        out_shape=jax.ShapeDtypeStruct((t, D_MODEL), F32),
        compiler_params=_cparams(("arbitrary",)),
        name="combine",
    )(pos, y_pad, h, ew, g2, nf)


def _dispatch_indices(ids, ranks, counts, t):
    a = t * TOP_K
    nblk = a // MOE_BLOCK + N_EXPERTS + GATHER_AHEAD * EXPERT_GROUP
    experts = jnp.arange(N_EXPERTS, dtype=jnp.int32)
    starts = jnp.cumsum(counts) - counts
    padded = (counts + MOE_BLOCK - 1) // MOE_BLOCK * MOE_BLOCK
    pends = jnp.cumsum(padded)
    pstarts = pends - padded
    block_start = jnp.arange(nblk, dtype=jnp.int32) * MOE_BLOCK
    block_e = jnp.minimum(jnp.sum((block_start[:, None] >= pends[None, :]).astype(jnp.int32), axis=1),
                          N_EXPERTS - 1)
    of_block = (block_e[:, None] == experts[None, :]).astype(jnp.int32)
    src0 = block_start + jnp.sum(of_block * (starts - pstarts)[None, :], axis=1)
    nused = pends[-1:] // MOE_BLOCK
    tok = jnp.arange(t, dtype=jnp.int32)[None, :]
    slot = jnp.arange(TOP_K, dtype=jnp.int32)[:, None]
    key = ids * a + tok * TOP_K + slot
    stok = (jnp.sort(key.reshape(a)) % a) // TOP_K
    ids_d = ids.reshape(a // LANES, LANES)
    pstart_of = jnp.zeros_like(ids_d)
    for e in range(N_EXPERTS):
        pstart_of = jnp.where(ids_d == e, pstarts[e], pstart_of)
    pos = (ranks.reshape(a // LANES, LANES) + pstart_of).reshape(a)
    i32 = lambda v: v.astype(jnp.int32)
    return i32(block_e), i32(src0), i32(stok), i32(nused), i32(pos), nblk


def kernel(x, c, ctx, c_ctx, w_ada, b_ada, norm_mix, norm_ffn, w_in, gla_lr_w, gla_lr_b, gla_norm,
           w_branch_gla, w_branch_ret, w_out, w_router_group, b_router_group, w_router_expert,
           b_router_expert, w_expert_gate, w_expert_up, w_expert_down, norm_final):
    batch, seq, d = x.shape
    ctx_len = ctx.shape[1]
    assert d == D_MODEL and w_ada.shape[0] == 1, "single-layer block with D_MODEL features"
    t = batch * seq

    c8 = jnp.zeros((8, d), F32).at[:batch].set(c).at[batch].set(c_ctx)
    mod = _ada(c8, w_ada[0], b_ada[0][None, :])
    sh1, sc1, g1, sh2, sc2, g2 = [mod[:, d * i:d * (i + 1)] for i in range(6)]
    lat = lambda m: m[:batch, None, :]
    cx = lambda m: m[batch:batch + 1, None, :]

    w_main, w_lr = _wprep(jnp.swapaxes(w_in[0], 0, 1))
    nm = norm_mix[0][None, :]

    p_ctx, lr_ctx = _proj(ctx.reshape(batch * ctx_len, d), cx(sh1), cx(sc1), nm, w_main, w_lr,
                          tm=ctx_len, rows_per_mod=batch * ctx_len)
    x2 = x.reshape(t, d)
    p_lat, lr_lat = _proj(x2, lat(sh1), lat(sc1), nm, w_main, w_lr, tm=PROJ_TM, rows_per_mod=seq)

    wlr_f = jnp.zeros((LANES, HEADS * DK), F32).at[:GLA_RANK].set(gla_lr_w[0, 0])
    wlr_b = jnp.zeros((LANES, HEADS * DK), F32).at[GLA_RANK:2 * GLA_RANK].set(gla_lr_w[0, 1])
    blr_f, blr_b = gla_lr_b[0, 0][None, :], gla_lr_b[0, 1][None, :]
    zero_state = jnp.zeros((batch, HEADS, DK, DV), F32)

    _, _, gs_f, gs_b = _gla(p_ctx, lr_ctx, wlr_f, blr_f, wlr_b, blr_b, zero_state, zero_state,
                            batch, ctx_len, tb=ctx_len)
    gla_f, gla_b, _, _ = _gla(p_lat, lr_lat, wlr_f, blr_f, wlr_b, blr_b, gs_f, gs_b, batch, seq, tb=SCAN_TB)

    ones = jnp.ones((ctx_len, DK), F32)
    _, _, rs_f, rs_b = _ret(p_ctx, ones, jnp.zeros_like(ones), zero_state, zero_state, batch, ctx_len, tb=ctx_len)
    cos, sin = _rope_tables(seq)
    ret_f, ret_b, _, _ = _ret(p_lat, cos, sin, rs_f, rs_b, batch, seq, tb=SCAN_TB)

    wrt = jnp.zeros((d, ROUTER_W), F32)
    wrt = wrt.at[:, :N_GROUPS].set(w_router_group[0]).at[:, EXP_ROW0:EXP_ROW0 + N_EXPERTS].set(w_router_expert[0])
    wrt_hi = wrt.astype(BF16)
    wrt = jnp.concatenate([wrt_hi, (wrt - wrt_hi.astype(F32)).astype(BF16)], axis=1)
    brt = jnp.zeros((1, ROUTER_W), F32)
    brt = brt.at[0, :N_GROUPS].set(b_router_group[0]).at[0, EXP_ROW0:EXP_ROW0 + N_EXPERTS].set(b_router_expert[0])
    h, hn, ids8, ew8, cnt = _merge(gla_f, gla_b, ret_f, ret_b, p_lat, x2, lat(g1), lat(sh2),
                                   norm_ffn[0][None, None, :] * (1.0 + lat(sc2)),
                                   jnp.tile(gla_norm[0], HEADS)[None, :],
                                   w_branch_gla[0].astype(BF16), w_branch_ret[0].astype(BF16),
                                   w_out[0].astype(BF16), wrt, brt, seq, tm=MERGE_TM)

    block_e, src0, stok, nused, pos, nblk = _dispatch_indices(ids8[:TOP_K], ids8[TOP_K:2 * TOP_K], cnt[:, 0], t)
    y_pad = _experts(block_e, src0, stok, nused, hn, w_expert_gate[0], w_expert_up[0], w_expert_down[0], nblk)
    out = _combine(pos, y_pad, h, ew8, lat(g2), norm_final[None, :], seq)
    return out.reshape(batch, seq, d)
```

```python
import functools

import numpy as np
import jax
import jax.numpy as jnp
from jax import lax
from jax.experimental import pallas as pl
from jax.experimental.pallas import tpu as pltpu

F32 = jnp.float32
BF16 = jnp.bfloat16

D_MODEL = 1024
GRID_W = 64
HEADS = 4
DK = 128
DV = 256
GLA_RANK = 16
GLA_TAU = 16.0
GLA_CHUNK = 64
GLA_SUB = 16
RET_CHUNK = 128
ROPE_BASE = 10000.0
N_GROUPS = 4
EXPERTS_PER_GROUP = 8
N_EXPERTS = N_GROUPS * EXPERTS_PER_GROUP
TOP_K = 2
EXPERT_FF = 256
MOE_BLOCK = 128
NORM_EPS = 1e-6

COL_GQ, COL_GK, COL_GV, COL_GG = 0, 512, 1024, 2048
COL_RQ, COL_RK, COL_RV, COL_RG = 3072, 3584, 4096, 5120
COL_MG, COL_MR = 6144, 7168
PROJ_W = 8192
LANES = 128
FEAT_TILES = D_MODEL // LANES
ROUTER_W = 128
EXP_ROW0 = 8

VMEM_LIMIT = 56 * 1024 * 1024

PROJ_TM = 1024
PROJ_TN = 2048
SCAN_TB = 1024
MERGE_TM = 256
MERGE_SPLIT = 1
COMBINE_TM = 256


def _cparams(sem):
    return pltpu.CompilerParams(dimension_semantics=sem, vmem_limit_bytes=VMEM_LIMIT)


def _sigmoid(x):
    return 1.0 / (1.0 + jnp.exp(-x))


def _silu(x):
    return x * _sigmoid(x)


def _dot(a, b):
    return jnp.dot(a, b, preferred_element_type=F32)


def _dot_nt(a, b):
    return lax.dot_general(a, b, (((1,), (1,)), ((), ())), preferred_element_type=F32)


def _ada_body(c_ref, w_ref, b_ref, o_ref):
    s = _silu(c_ref[...])
    o_ref[...] = _dot(s.astype(BF16), w_ref[...].astype(BF16)) + b_ref[...]


def _ada(c8, w, b):
    n = w.shape[1]
    tn = 1536
    return pl.pallas_call(
        _ada_body,
        grid=(n // tn,),
        in_specs=[pl.BlockSpec((8, D_MODEL), lambda j: (0, 0)),
                  pl.BlockSpec((D_MODEL, tn), lambda j: (0, j)),
                  pl.BlockSpec((1, tn), lambda j: (0, j))],
        out_specs=pl.BlockSpec((8, tn), lambda j: (0, j)),
        out_shape=jax.ShapeDtypeStruct((8, n), F32),
        compiler_params=_cparams(("arbitrary",)),
        name="ada",
    )(c8, w, b)


LR_COL0 = COL_GG + HEADS * DV
WPREP_ROWS = 1024


def _wprep_body(w_ref, wlr_ref, main_ref, lr_ref):
    main_ref[...] = w_ref[...].astype(BF16)
    pad = jnp.zeros((LANES - 2 * GLA_RANK, D_MODEL), F32)
    lr_ref[...] = jnp.concatenate([wlr_ref[...], pad], axis=0).astype(BF16)


def _wprep(wt):
    steps_before = LR_COL0 // WPREP_ROWS

    def src_row(i):
        row = jnp.where(i < steps_before, i * WPREP_ROWS, i * WPREP_ROWS + 2 * GLA_RANK)
        return (pl.multiple_of(row, 2 * GLA_RANK), 0)

    return pl.pallas_call(
        _wprep_body,
        grid=(PROJ_W // WPREP_ROWS,),
        in_specs=[pl.BlockSpec((pl.Element(WPREP_ROWS), pl.Element(D_MODEL)), src_row),
                  pl.BlockSpec((pl.Element(2 * GLA_RANK), pl.Element(D_MODEL)), lambda i: (LR_COL0, 0))],
        out_specs=[pl.BlockSpec((WPREP_ROWS, D_MODEL), lambda i: (i, 0)),
                   pl.BlockSpec((LANES, D_MODEL), lambda i: (0, 0))],
        out_shape=[jax.ShapeDtypeStruct((PROJ_W, D_MODEL), BF16), jax.ShapeDtypeStruct((LANES, D_MODEL), BF16)],
        compiler_params=_cparams(("arbitrary",)),
        name="wprep",
    )(wt, wt)


def _proj_body(x_ref, sh_ref, sc_ref, g_ref, w_ref, wlr_ref, o_ref, lr_ref, h_ref):
    @pl.when(pl.program_id(1) == 0)
    def _():
        x = x_ref[...]
        ms = jnp.mean(x * x, axis=-1, keepdims=True)
        y = x * lax.rsqrt(ms + NORM_EPS) * g_ref[...]
        hb = (y * (1.0 + sc_ref[0]) + sh_ref[0]).astype(BF16)
        h_ref[...] = hb
        lr_ref[...] = _dot_nt(hb, wlr_ref[...])

    o_ref[...] = _dot_nt(h_ref[...], w_ref[...]).astype(BF16)


def _proj(x2, sh, sc, g, w_main, w_lr, tm, rows_per_mod):
    t = x2.shape[0]
    tn = PROJ_TN
    tiles_per_mod = rows_per_mod // tm
    mod_map = lambda i, j: (i // tiles_per_mod, 0, 0)
    return pl.pallas_call(
        _proj_body,
        grid=(t // tm, PROJ_W // tn),
        in_specs=[pl.BlockSpec((tm, D_MODEL), lambda i, j: (i, 0)),
                  pl.BlockSpec((1, 1, D_MODEL), mod_map),
                  pl.BlockSpec((1, 1, D_MODEL), mod_map),
                  pl.BlockSpec((1, D_MODEL), lambda i, j: (0, 0)),
                  pl.BlockSpec((tn, D_MODEL), lambda i, j: (j, 0)),
                  pl.BlockSpec((LANES, D_MODEL), lambda i, j: (0, 0))],
        out_specs=[pl.BlockSpec((tm, tn), lambda i, j: (i, j)),
                   pl.BlockSpec((tm, LANES), lambda i, j: (i, 0))],
        out_shape=[jax.ShapeDtypeStruct((t, PROJ_W), BF16),
                   jax.ShapeDtypeStruct((t, LANES), F32)],
        scratch_shapes=[pltpu.VMEM((tm, D_MODEL), BF16)],
        compiler_params=_cparams(("parallel", "arbitrary")),
        name="proj",
    )(x2, sh, sc, g, w_main, w_lr)


def _gla_consts():
    c, s = GLA_CHUNK, GLA_SUB
    msk = {}
    for rev in (False, True):
        cols = []
        for i in range(c // s):
            keys = np.arange(c)[_key_rows(i, rev)][None, :]
            t = np.arange(c)[:, None]
            visible = (keys > t) if rev else (keys <= t)
            cols.append(np.where(t // s == i, visible, False))
        msk[rev] = np.concatenate(cols, axis=1).astype(np.float32)
    return msk


def _key_rows(i, rev):
    return slice(GLA_SUB * i, GLA_CHUNK) if rev else slice(0, GLA_SUB * (i + 1))


def _log_gate(lr, w, b):
    z = _dot(lr.astype(BF16), w.astype(BF16)) + b
    return -(jnp.maximum(-z, 0.0) + jnp.log(1.0 + jnp.exp(-jnp.abs(z)))) * (1.0 / GLA_TAU)


def _subchunk_scan(la, rev):
    c = la.shape[0]
    pos = lax.broadcasted_iota(jnp.int32, la.shape, 0) % GLA_SUB
    w = la
    step = 1
    while step < GLA_SUB:
        if rev:
            w = w + jnp.where(pos < GLA_SUB - step, pltpu.roll(w, c - step, 0), 0.0)
        else:
            w = w + jnp.where(pos >= step, pltpu.roll(w, step, 0), 0.0)
        step *= 2
    return w


def _gla_chunk(q, k, v, la, state, mask, rev):
    c, s = GLA_CHUNK, GLA_SUB
    nsub = c // s
    w = _subchunk_scan(la, rev)
    tot = [w[s * i:s * i + 1] if rev else w[s * (i + 1) - 1:s * (i + 1)] for i in range(nsub)]
    anchors = [None] * nsub
    acc = jnp.zeros_like(tot[0])
    for i in (reversed(range(nsub)) if rev else range(nsub)):
        anchors[i] = acc
        acc = acc + tot[i]
    b_end = acc
    ref = jnp.concatenate([jnp.broadcast_to(a, (s, DK)) for a in anchors], axis=0)
    b = w + ref

    qt = q.astype(F32) * (DK ** -0.5) * jnp.exp(w)
    kf = k.astype(F32)
    inter = _dot((qt * jnp.exp(ref)).astype(BF16), state.astype(BF16))

    kstack = jnp.concatenate([kf[_key_rows(i, rev)] * jnp.exp(anchors[i] - b[_key_rows(i, rev)])
                              for i in range(nsub)], axis=0).astype(BF16)
    vstack = jnp.concatenate([v[_key_rows(i, rev)] for i in range(nsub)], axis=0)
    scores = jnp.where(mask != 0.0, _dot_nt(qt.astype(BF16), kstack), 0.0)
    out = inter + _dot(scores.astype(BF16), vstack)

    kst = kf * jnp.exp(b_end - b)
    xt = jnp.concatenate([kst, jnp.broadcast_to(jnp.exp(b_end), (c, DK))], axis=0).T
    new_state = xt[:, c:c + 1] * state + _dot(xt[:, :c].astype(BF16), v)
    return out, new_state


def _gla_body(qf_ref, kf_ref, vf_ref, lrf_ref, qb_ref, kb_ref, vb_ref, lrb_ref,
              wf_ref, bf_ref, wb_ref, bb_ref, mskf_ref, mskb_ref,
              s0f_ref, s0b_ref, of_ref, ob_ref, sf_ref, sb_ref, *, nchunks):
    @pl.when(pl.program_id(2) == 0)
    def _():
        sf_ref[...] = s0f_ref[...]
        sb_ref[...] = s0b_ref[...]

    c = GLA_CHUNK
    la_f = _log_gate(lrf_ref[...], wf_ref[...], bf_ref[...])
    la_b = _log_gate(lrb_ref[...], wb_ref[...], bb_ref[...])
    mskf, mskb = mskf_ref[...], mskb_ref[...]
    st_f = sf_ref[0, 0]
    st_b = sb_ref[0, 0]
    for n in range(nchunks):
        rf = slice(c * n, c * (n + 1))
        o, st_f = _gla_chunk(qf_ref[rf, :], kf_ref[rf, :], vf_ref[rf, :], la_f[rf, :], st_f, mskf, False)
        of_ref[rf, :] = o.astype(BF16)
        m = nchunks - 1 - n
        rb = slice(c * m, c * (m + 1))
        o, st_b = _gla_chunk(qb_ref[rb, :], kb_ref[rb, :], vb_ref[rb, :], la_b[rb, :], st_b, mskb, True)
        ob_ref[rb, :] = o.astype(BF16)
    sf_ref[0, 0] = st_f
    sb_ref[0, 0] = st_b


def _gla(p, lr, wlr_f, blr_f, wlr_b, blr_b, s0f, s0b, batch, seq, tb):
    nb = seq // tb
    msk = _gla_consts()
    mskf, mskb = jnp.asarray(msk[False]), jnp.asarray(msk[True])

    def fwd(col0, width):
        return lambda b, h, i: (b * nb + i, col0 // width + h)

    def bwd(col0, width):
        return lambda b, h, i: (b * nb + nb - 1 - i, col0 // width + h)

    const2 = lambda b, h, i: (0, 0)
    headcol = lambda b, h, i: (0, h)
    st_map = lambda b, h, i: (b, h, 0, 0)
    st_spec = pl.BlockSpec((1, 1, DK, DV), st_map)
    st_shape = jax.ShapeDtypeStruct((batch, HEADS, DK, DV), F32)
    o_shape = jax.ShapeDtypeStruct((batch * seq, HEADS * DV), BF16)
    return pl.pallas_call(
        functools.partial(_gla_body, nchunks=tb // GLA_CHUNK),
        grid=(batch, HEADS, nb),
        in_specs=[pl.BlockSpec((tb, DK), fwd(COL_GQ, DK)),
                  pl.BlockSpec((tb, DK), fwd(COL_GK, DK)),
                  pl.BlockSpec((tb, DV), fwd(COL_GV, DV)),
                  pl.BlockSpec((tb, LANES), lambda b, h, i: (b * nb + i, 0)),
                  pl.BlockSpec((tb, DK), bwd(COL_GQ, DK)),
                  pl.BlockSpec((tb, DK), bwd(COL_GK, DK)),
                  pl.BlockSpec((tb, DV), bwd(COL_GV, DV)),
                  pl.BlockSpec((tb, LANES), lambda b, h, i: (b * nb + nb - 1 - i, 0)),
                  pl.BlockSpec((LANES, DK), headcol),
                  pl.BlockSpec((1, DK), headcol),
                  pl.BlockSpec((LANES, DK), headcol),
                  pl.BlockSpec((1, DK), headcol),
                  pl.BlockSpec(mskf.shape, const2),
                  pl.BlockSpec(mskb.shape, const2),
                  st_spec, st_spec],
        out_specs=[pl.BlockSpec((tb, DV), lambda b, h, i: (b * nb + i, h)),
                   pl.BlockSpec((tb, DV), lambda b, h, i: (b * nb + nb - 1 - i, h)),
                   st_spec, st_spec],
        out_shape=[o_shape, o_shape, st_shape, st_shape],
        compiler_params=_cparams(("parallel", "parallel", "arbitrary")),
        name="gla",
    )(p, p, p, lr, p, p, p, lr, wlr_f, blr_f, wlr_b, blr_b, mskf, mskb, s0f, s0b)


def _ret_consts():
    c = RET_CHUNK
    hh = np.arange(HEADS, dtype=np.float64)
    lg = {False: np.log1p(-np.exp2(-5.0 - hh)), True: np.log1p(-np.exp2(-5.5 - hh))}
    t = np.arange(c, dtype=np.float64)[:, None]
    u = np.arange(c, dtype=np.float64)[None, :]
    pos = np.arange(c, dtype=np.float64)
    out = {}
    for rev in (False, True):
        g = lg[rev][:, None, None]
        if not rev:
            dmat = np.where(u <= t, np.exp((t - u) * g), 0.0)
            qd = np.exp((pos + 1.0)[None, :] * lg[rev][:, None])
            kd = np.exp((c - 1.0 - pos)[None, :] * lg[rev][:, None])
        else:
            dmat = np.where(u > t, np.exp((u - t) * g), 0.0)
            qd = np.exp((c - pos)[None, :] * lg[rev][:, None])
            kd = np.exp(pos[None, :] * lg[rev][:, None])
        cd = np.exp(c * lg[rev])
        out[rev] = (dmat.astype(np.float32),
                    np.repeat(qd[:, :, None], DV, axis=2).astype(np.float32),
                    np.repeat(kd[:, :, None], DK, axis=2).astype(np.float32),
                    np.repeat(cd[:, None, None], DV, axis=2).astype(np.float32))
    return out


def _rope(x, cos, sin):
    lane = lax.broadcasted_iota(jnp.int32, x.shape, 1)
    partner = jnp.where((lane % 64) < 32, pltpu.roll(x, DK - 32, 1), pltpu.roll(x, 32, 1))
    return x * cos + partner * sin


def _ret_chunk(q, k, v, cos, sin, state, dmat, qd, kd, cd):
    qr = _rope(q.astype(F32), cos, sin).astype(BF16)
    kr = _rope(k.astype(F32) * (DK ** -0.5), cos, sin)
    inter = _dot(qr, state.astype(BF16)) * qd
    scores = _dot_nt(qr, kr.astype(BF16)) * dmat
    out = inter + _dot(scores.astype(BF16), v)
    new_state = cd * state + _dot((kr * kd).T.astype(BF16), v)
    return out, new_state


def _ret_body(qf_ref, kf_ref, vf_ref, cosf_ref, sinf_ref, qb_ref, kb_ref, vb_ref, cosb_ref, sinb_ref,
              dmf_ref, qdf_ref, kdf_ref, cdf_ref, dmb_ref, qdb_ref, kdb_ref, cdb_ref,
              s0f_ref, s0b_ref, of_ref, ob_ref, sf_ref, sb_ref, *, nchunks):
    @pl.when(pl.program_id(2) == 0)
    def _():
        sf_ref[...] = s0f_ref[...]
        sb_ref[...] = s0b_ref[...]

    c = RET_CHUNK
    cf = (dmf_ref[0], qdf_ref[0], kdf_ref[0], cdf_ref[0])
    cb = (dmb_ref[0], qdb_ref[0], kdb_ref[0], cdb_ref[0])
    st_f = sf_ref[0, 0]
    st_b = sb_ref[0, 0]
    for n in range(nchunks):
        rf = slice(c * n, c * (n + 1))
        o, st_f = _ret_chunk(qf_ref[rf, :], kf_ref[rf, :], vf_ref[rf, :], cosf_ref[rf, :], sinf_ref[rf, :],
                             st_f, *cf)
        of_ref[rf, :] = o.astype(BF16)
        m = nchunks - 1 - n
        rb = slice(c * m, c * (m + 1))
        o, st_b = _ret_chunk(qb_ref[rb, :], kb_ref[rb, :], vb_ref[rb, :], cosb_ref[rb, :], sinb_ref[rb, :],
                             st_b, *cb)
        ob_ref[rb, :] = o.astype(BF16)
    sf_ref[0, 0] = st_f
    sb_ref[0, 0] = st_b


def _ret(p, cos, sin, s0f, s0b, batch, seq, tb):
    nb = seq // tb
    consts = _ret_consts()

    def fwd(col0, width):
        return lambda b, h, i: (b * nb + i, col0 // width + h)

    def bwd(col0, width):
        return lambda b, h, i: (b * nb + nb - 1 - i, col0 // width + h)

    head3 = lambda b, h, i: (h, 0, 0)
    st_spec = pl.BlockSpec((1, 1, DK, DV), lambda b, h, i: (b, h, 0, 0))
    st_shape = jax.ShapeDtypeStruct((batch, HEADS, DK, DV), F32)
    o_shape = jax.ShapeDtypeStruct((batch * seq, HEADS * DV), BF16)
    const_specs = []
    const_args = []
    for rev in (False, True):
        for a in consts[rev]:
            const_specs.append(pl.BlockSpec((1,) + a.shape[1:], head3))
            const_args.append(jnp.asarray(a))
    return pl.pallas_call(
        functools.partial(_ret_body, nchunks=tb // RET_CHUNK),
        grid=(batch, HEADS, nb),
        in_specs=[pl.BlockSpec((tb, DK), fwd(COL_RQ, DK)),
                  pl.BlockSpec((tb, DK), fwd(COL_RK, DK)),
                  pl.BlockSpec((tb, DV), fwd(COL_RV, DV)),
                  pl.BlockSpec((tb, DK), lambda b, h, i: (i, 0)),
                  pl.BlockSpec((tb, DK), lambda b, h, i: (i, 0)),
                  pl.BlockSpec((tb, DK), bwd(COL_RQ, DK)),
                  pl.BlockSpec((tb, DK), bwd(COL_RK, DK)),
                  pl.BlockSpec((tb, DV), bwd(COL_RV, DV)),
                  pl.BlockSpec((tb, DK), lambda b, h, i: (nb - 1 - i, 0)),
                  pl.BlockSpec((tb, DK), lambda b, h, i: (nb - 1 - i, 0))]
                 + const_specs + [st_spec, st_spec],
        out_specs=[pl.BlockSpec((tb, DV), lambda b, h, i: (b * nb + i, h)),
                   pl.BlockSpec((tb, DV), lambda b, h, i: (b * nb + nb - 1 - i, h)),
                   st_spec, st_spec],
        out_shape=[o_shape, o_shape, st_shape, st_shape],
        compiler_params=_cparams(("parallel", "parallel", "arbitrary")),
        name="ret",
    )(p, p, p, cos, sin, p, p, p, cos, sin, *const_args, s0f, s0b)


def _rope_tables(seq):
    n = DK // 4
    inv = np.float32(ROPE_BASE) ** (-np.arange(n, dtype=np.float32) / np.float32(n))
    pos = np.arange(seq)
    ar = (pos // GRID_W).astype(np.float32)[:, None] * inv[None, :]
    ac = (pos % GRID_W).astype(np.float32)[:, None] * inv[None, :]
    cos = np.concatenate([np.cos(ar), np.cos(ar), np.cos(ac), np.cos(ac)], axis=1)
    sin = np.concatenate([-np.sin(ar), np.sin(ar), -np.sin(ac), np.sin(ac)], axis=1)
    return jnp.asarray(cos, F32), jnp.asarray(sin, F32)


def _route(logits_t):
    g = [logits_t[i:i + 1] for i in range(N_GROUPS)]
    gmax = jnp.maximum(jnp.maximum(g[0], g[1]), jnp.maximum(g[2], g[3]))
    gsel = jnp.where(g[0] == gmax, 0, jnp.where(g[1] == gmax, 1, jnp.where(g[2] == gmax, 2, 3)))
    gsum = (jnp.exp(g[0] - gmax) + jnp.exp(g[1] - gmax)) + (jnp.exp(g[2] - gmax) + jnp.exp(g[3] - gmax))
    gw = 1.0 / gsum
    e = [logits_t[EXP_ROW0 + EXPERTS_PER_GROUP * i:EXP_ROW0 + EXPERTS_PER_GROUP * (i + 1)]
         for i in range(N_GROUPS)]
    el = jnp.where(gsel == 0, e[0], jnp.where(gsel == 1, e[1], jnp.where(gsel == 2, e[2], e[3])))
    row = lax.broadcasted_iota(jnp.int32, el.shape, 0).astype(F32)
    none = float(EXPERTS_PER_GROUP)
    m1 = jnp.max(el, axis=0, keepdims=True)
    i1 = jnp.min(jnp.where(el == m1, row, none), axis=0, keepdims=True)
    el2 = jnp.where(row == i1, -jnp.inf, el)
    m2 = jnp.max(el2, axis=0, keepdims=True)
    i2 = jnp.min(jnp.where(el2 == m2, row, none), axis=0, keepdims=True)
    r = jnp.exp(m2 - m1)
    w1 = gw / (1.0 + r)
    w2 = gw * r / (1.0 + r)
    base = gsel * EXPERTS_PER_GROUP
    ids = jnp.concatenate([base + i1.astype(jnp.int32), base + i2.astype(jnp.int32)], axis=0)
    return ids, jnp.concatenate([w1, w2], axis=0)


def _store_token_tiled(ref, val):
    m = val.shape[0]
    for j in range(FEAT_TILES):
        ref[pl.ds(j, m, stride=FEAT_TILES), :] = val[:, LANES * j:LANES * (j + 1)]


def _load_token_tiled(ref, m):
    return jnp.concatenate([ref[pl.ds(j, m, stride=FEAT_TILES), :] for j in range(FEAT_TILES)], axis=1)


def _merge_body(gf_ref, gb_ref, rf_ref, rb_ref, gg_ref, rg_ref, mg_ref, mr_ref, x_ref,
                g1_ref, sh2_ref, nfs_ref, gn_ref, wg_ref, wr_ref, wo_ref, wrt_ref, brt_ref, upper_ref,
                h_ref, hn_ref, ids_ref, ew_ref, cnt_ref, wg_s, wr_s, wo_s):
    @pl.when(pl.program_id(0) == 0)
    def _():
        cnt_ref[...] = jnp.zeros_like(cnt_ref)
        wg_s[...] = wg_ref[...]
        wr_s[...] = wr_ref[...]
        wo_s[...] = wo_ref[...]

    def rows_to_logits(r0, n):
        rows = pl.ds(r0, n)
        og = (gf_ref[rows, :] + gb_ref[rows, :]).astype(F32)
        orr = (rf_ref[rows, :] + rb_ref[rows, :]).astype(F32)
        gparts, rparts = [], []
        for hh in range(HEADS):
            seg = og[:, DV * hh:DV * (hh + 1)]
            ms = jnp.mean(seg * seg, axis=-1, keepdims=True)
            gparts.append(seg * lax.rsqrt(ms + NORM_EPS))
            seg = orr[:, DV * hh:DV * (hh + 1)]
            mu = jnp.mean(seg, axis=-1, keepdims=True)
            cen = seg - mu
            var = jnp.mean(cen * cen, axis=-1, keepdims=True)
            rparts.append(cen * lax.rsqrt(var + NORM_EPS))
        o_gla = (jnp.concatenate(gparts, axis=1) * gn_ref[...]).astype(BF16) * _silu(gg_ref[rows, :])
        o_ret = jnp.concatenate(rparts, axis=1).astype(BF16) * _silu(rg_ref[rows, :])
        y = (_sigmoid(mg_ref[rows, :]) * _dot(o_gla, wg_s[...]).astype(BF16)
             + _sigmoid(mr_ref[rows, :]) * _dot(o_ret, wr_s[...]).astype(BF16))
        h = x_ref[rows, :] + g1_ref[0] * _dot(y, wo_s[...])
        h_ref[rows, :] = h
        ms = jnp.mean(h * h, axis=-1, keepdims=True)
        hn = h * lax.rsqrt(ms + NORM_EPS) * nfs_ref[0] + sh2_ref[0]
        _store_token_tiled(hn_ref.at[pl.ds(r0 * FEAT_TILES, n * FEAT_TILES)], hn)
        hn_hi = hn.astype(BF16)
        hn_lo = (hn - hn_hi.astype(F32)).astype(BF16)
        both_w = _dot(hn_hi, wrt_ref[...])
        return (both_w[:, :ROUTER_W] + both_w[:, ROUTER_W:] + _dot(hn_lo, wrt_ref[:, :ROUTER_W])) + brt_ref[...]

    tm = x_ref.shape[0]
    n = tm // MERGE_SPLIT
    logits = jnp.concatenate([rows_to_logits(g * n, n) for g in range(MERGE_SPLIT)], axis=0)
    ids, ew = _route(logits.T)

    erow = lax.broadcasted_iota(jnp.int32, (N_EXPERTS, tm), 0)
    oh0 = jnp.where(erow == ids[0:1], 1.0, 0.0)
    oh1 = jnp.where(erow == ids[1:2], 1.0, 0.0)
    both = oh0 + oh1
    before = _dot(both.astype(BF16), upper_ref[...]) + cnt_ref[:, 0:1].astype(F32)
    rank0 = jnp.sum(oh0 * before, axis=0, keepdims=True)
    rank1 = jnp.sum(oh1 * before, axis=0, keepdims=True)
    total = cnt_ref[:, 0:1] + jnp.sum(both, axis=1, keepdims=True).astype(jnp.int32)
    cnt_ref[...] = jnp.broadcast_to(total, cnt_ref.shape)
    ids_ref[...] = jnp.concatenate([ids, rank0.astype(jnp.int32), rank1.astype(jnp.int32),
                                    jnp.zeros((4, tm), jnp.int32)], axis=0)
    ew_ref[...] = jnp.concatenate([ew, jnp.zeros((6, tm), F32)], axis=0)


def _merge(gf, gb, rf, rb, p, x2, g1, sh2, nfs, gn, wg, wr, wo, wrt, brt, seq, tm):
    t = x2.shape[0]
    tiles_per_batch = seq // tm
    row = lambda i: (i, 0)
    mod = lambda i: (i // tiles_per_batch, 0, 0)
    const = lambda i: (0, 0)
    tok = pl.BlockSpec((tm, D_MODEL), row)
    vec = pl.BlockSpec((1, D_MODEL), const)
    modspec = pl.BlockSpec((1, 1, D_MODEL), mod)
    wspec = pl.BlockSpec((D_MODEL, D_MODEL), const)
    upper = jnp.asarray(np.triu(np.ones((tm, tm), np.float32), 1), BF16)

    def pcol(col0):
        return pl.BlockSpec((tm, D_MODEL), lambda i: (i, col0 // D_MODEL))

    return pl.pallas_call(
        _merge_body,
        grid=(t // tm,),
        in_specs=[tok, tok, tok, tok, pcol(COL_GG), pcol(COL_RG), pcol(COL_MG), pcol(COL_MR), tok,
                  modspec, modspec, modspec, vec, wspec, wspec, wspec,
                  pl.BlockSpec((D_MODEL, 2 * ROUTER_W), const), pl.BlockSpec((1, ROUTER_W), const),
                  pl.BlockSpec((tm, tm), const)],
        out_specs=[tok, pl.BlockSpec((tm * FEAT_TILES, LANES), row),
                   pl.BlockSpec((8, tm), lambda i: (0, i)), pl.BlockSpec((8, tm), lambda i: (0, i)),
                   pl.BlockSpec((N_EXPERTS, LANES), const)],
        out_shape=[jax.ShapeDtypeStruct((t, D_MODEL), F32), jax.ShapeDtypeStruct((t * FEAT_TILES, LANES), F32),
                   jax.ShapeDtypeStruct((8, t), jnp.int32), jax.ShapeDtypeStruct((8, t), F32),
                   jax.ShapeDtypeStruct((N_EXPERTS, LANES), jnp.int32)],
        scratch_shapes=[pltpu.VMEM((D_MODEL, D_MODEL), BF16)] * 3,
        compiler_params=_cparams(("arbitrary",)),
        name="merge",
    )(gf, gb, rf, rb, p, p, p, p, x2, g1, sh2, nfs, gn, wg, wr, wo, wrt, brt, upper)


GATHER_UNROLL = 8


def _aligned(tok):
    off = tok * FEAT_TILES
    return off if isinstance(off, int) else pl.multiple_of(off, FEAT_TILES)


def _token_copy(src_hbm, dst, sem, src_tok, dst_tok):
    return pltpu.make_async_copy(src_hbm.at[pl.ds(_aligned(src_tok), FEAT_TILES), :],
                                 dst.at[pl.ds(_aligned(dst_tok), FEAT_TILES), :], sem)


def _wait_tokens(src_hbm, dst, sem):
    pltpu.make_async_copy(src_hbm.at[pl.ds(0, dst.shape[0]), :], dst, sem).wait()


GATHER_BUFS = 3
GATHER_AHEAD = GATHER_BUFS - 1


def _issue_rows(n, start_row, unrolled):
    if unrolled:
        for r in range(n):
            start_row(r, r % 2)
    else:
        def body(g, carry):
            for u in range(GATHER_UNROLL):
                start_row(g * GATHER_UNROLL + u, u % 2)
            return carry
        lax.fori_loop(0, n // GATHER_UNROLL, body, 0)


def _expert_body(be_ref, src0_ref, stok_ref, nused_ref, hn_hbm, wg_ref, wu_ref, wd_ref, y_ref, *scratch):
    i = pl.program_id(0)
    nused = nused_ref[0]
    bufs, sems = scratch[:GATHER_BUFS], scratch[GATHER_BUFS]
    wg_s, wu_s, wd_s = scratch[GATHER_BUFS + 1:]
    last = stok_ref.shape[0] - 1

    def issue(block, s, unrolled):
        base = src0_ref[block]

        def start_row(r, priority):
            tok = stok_ref[jnp.minimum(base + r, last)]
            _token_copy(hn_hbm, bufs[s], sems.at[s], tok, r).start(priority=priority)
        _issue_rows(MOE_BLOCK, start_row, unrolled)

    @pl.when(i == 0)
    def _():
        for b in range(GATHER_AHEAD):
            issue(b, b, False)

    @pl.when(jnp.logical_and(i < nused, jnp.logical_or(i == 0, be_ref[i] != be_ref[jnp.maximum(i - 1, 0)])))
    def _():
        wg_s[...] = wg_ref[0].astype(BF16)
        wu_s[...] = wu_ref[0].astype(BF16)
        wd_s[...] = wd_ref[0].astype(BF16)

    for cur in range(GATHER_BUFS):
        mine = i % GATHER_BUFS == cur

        @pl.when(jnp.logical_and(mine, i < nused))
        def _():
            _wait_tokens(hn_hbm, bufs[cur], sems.at[cur])
            issue(i + GATHER_AHEAD, (cur + GATHER_AHEAD) % GATHER_BUFS, True)
            xb = _load_token_tiled(bufs[cur], MOE_BLOCK).astype(BF16)
            gate = _dot(xb, wg_s[...])
            up = _dot(xb, wu_s[...])
            hid = (_silu(gate) * up).astype(BF16)
            _store_token_tiled(y_ref, _dot(hid, wd_s[...]))

        @pl.when(jnp.logical_and(mine, jnp.logical_and(i >= nused, i < nused + GATHER_AHEAD)))
        def _():
            _wait_tokens(hn_hbm, bufs[cur], sems.at[cur])

    @pl.when(i >= nused)
    def _():
        y_ref[...] = jnp.zeros_like(y_ref)


def _experts(block_e, src0, stok, nused, hn, w_gate, w_up, w_down, nblk):
    wmap = lambda i, be, s0, st, nu: (be[i], 0, 0)
    rows = MOE_BLOCK * FEAT_TILES
    return pl.pallas_call(
        _expert_body,
        grid_spec=pltpu.PrefetchScalarGridSpec(
            num_scalar_prefetch=4,
            grid=(nblk,),
            in_specs=[pl.BlockSpec(memory_space=pl.ANY),
                      pl.BlockSpec((1, D_MODEL, EXPERT_FF), wmap),
                      pl.BlockSpec((1, D_MODEL, EXPERT_FF), wmap),
                      pl.BlockSpec((1, EXPERT_FF, D_MODEL), wmap)],
            out_specs=pl.BlockSpec((rows, LANES), lambda i, be, s0, st, nu: (i, 0)),
            scratch_shapes=[pltpu.VMEM((rows, LANES), F32)] * GATHER_BUFS
                           + [pltpu.SemaphoreType.DMA((GATHER_BUFS,)),
                              pltpu.VMEM((D_MODEL, EXPERT_FF), BF16), pltpu.VMEM((D_MODEL, EXPERT_FF), BF16),
                              pltpu.VMEM((EXPERT_FF, D_MODEL), BF16)]),
        out_shape=jax.ShapeDtypeStruct((nblk * rows, LANES), F32),
        compiler_params=_cparams(("arbitrary",)),
        name="experts",
    )(block_e, src0, stok, nused, hn, w_gate, w_up, w_down)


def _combine_body(pos_ref, y_hbm, h_ref, ew_ref, g2_ref, nf_ref, o_ref, *scratch):
    i = pl.program_id(0)
    n = pl.num_programs(0)
    tm = COMBINE_TM
    t = pos_ref.shape[0] // TOP_K
    bufs, sems = scratch[:GATHER_BUFS], scratch[GATHER_BUFS]

    def issue(tile, s, unrolled):
        def start_row(j, priority):
            k, r = j % TOP_K, j // TOP_K
            _token_copy(y_hbm, bufs[s].at[k], sems.at[s], pos_ref[k * t + tile * tm + r], r).start(priority=priority)
        _issue_rows(tm * TOP_K, start_row, unrolled)

    def wait(s):
        for k in range(TOP_K):
            _wait_tokens(y_hbm, bufs[s].at[k], sems.at[s])

    @pl.when(i == 0)
    def _():
        for b in range(GATHER_AHEAD):
            issue(b, b, False)

    for cur in range(GATHER_BUFS):
        @pl.when(i % GATHER_BUFS == cur)
        def _():
            wait(cur)
            issue(jnp.minimum(i + GATHER_AHEAD, n - 1), (cur + GATHER_AHEAD) % GATHER_BUFS, True)
            wt = jnp.concatenate([ew_ref[...]] * (LANES // 8), axis=0).T
            moe = (wt[:, 0:1] * _load_token_tiled(bufs[cur].at[0], tm)
                   + wt[:, 1:2] * _load_token_tiled(bufs[cur].at[1], tm))
            h = h_ref[...] + g2_ref[0] * moe
            ms = jnp.mean(h * h, axis=-1, keepdims=True)
            o_ref[...] = h * lax.rsqrt(ms + NORM_EPS) * nf_ref[...]

        @pl.when(jnp.logical_and(i % GATHER_BUFS == cur, i == n - 1))
        def _():
            for ahead in range(1, GATHER_BUFS):
                wait((cur + ahead) % GATHER_BUFS)


def _combine(pos, y_pad, h, ew, g2, nf, seq):
    t = h.shape[0]
    tm = COMBINE_TM
    tiles_per_batch = seq // tm
    ybuf = pltpu.VMEM((TOP_K, tm * FEAT_TILES, LANES), F32)
    return pl.pallas_call(
        _combine_body,
        grid_spec=pltpu.PrefetchScalarGridSpec(
            num_scalar_prefetch=1,
            grid=(t // tm,),
            in_specs=[pl.BlockSpec(memory_space=pl.ANY),
                      pl.BlockSpec((tm, D_MODEL), lambda i, pos: (i, 0)),
                      pl.BlockSpec((8, tm), lambda i, pos: (0, i)),
                      pl.BlockSpec((1, 1, D_MODEL), lambda i, pos: (i // tiles_per_batch, 0, 0)),
                      pl.BlockSpec((1, D_MODEL), lambda i, pos: (0, 0))],
            out_specs=pl.BlockSpec((tm, D_MODEL), lambda i, pos: (i, 0)),
            scratch_shapes=[ybuf] * GATHER_BUFS + [pltpu.SemaphoreType.DMA((GATHER_BUFS,))]),
        out_shape=jax.ShapeDtypeStruct((t, D_MODEL), F32),
        compiler_params=_cparams(("arbitrary",)),
        name="combine",
    )(pos, y_pad, h, ew, g2, nf)


def _dispatch_indices(ids, ranks, counts, t):
    a = t * TOP_K
    nblk = a // MOE_BLOCK + N_EXPERTS + GATHER_AHEAD
    experts = jnp.arange(N_EXPERTS, dtype=jnp.int32)
    starts = jnp.cumsum(counts) - counts
    padded = (counts + MOE_BLOCK - 1) // MOE_BLOCK * MOE_BLOCK
    pends = jnp.cumsum(padded)
    pstarts = pends - padded
    block_start = jnp.arange(nblk, dtype=jnp.int32) * MOE_BLOCK
    block_e = jnp.minimum(jnp.sum((block_start[:, None] >= pends[None, :]).astype(jnp.int32), axis=1),
                          N_EXPERTS - 1)
    of_block = (block_e[:, None] == experts[None, :]).astype(jnp.int32)
    src0 = block_start + jnp.sum(of_block * (starts - pstarts)[None, :], axis=1)
    nused = pends[-1:] // MOE_BLOCK
    tok = jnp.arange(t, dtype=jnp.int32)[None, :]
    slot = jnp.arange(TOP_K, dtype=jnp.int32)[:, None]
    key = ids * a + tok * TOP_K + slot
    stok = (jnp.sort(key.reshape(a)) % a) // TOP_K
    ids_d = ids.reshape(a // LANES, LANES)
    pstart_of = jnp.zeros_like(ids_d)
    for e in range(N_EXPERTS):
        pstart_of = jnp.where(ids_d == e, pstarts[e], pstart_of)
    pos = (ranks.reshape(a // LANES, LANES) + pstart_of).reshape(a)
    i32 = lambda v: v.astype(jnp.int32)
    return i32(block_e), i32(src0), i32(stok), i32(nused), i32(pos), nblk


def kernel(x, c, ctx, c_ctx, w_ada, b_ada, norm_mix, norm_ffn, w_in, gla_lr_w, gla_lr_b, gla_norm,
           w_branch_gla, w_branch_ret, w_out, w_router_group, b_router_group, w_router_expert,
           b_router_expert, w_expert_gate, w_expert_up, w_expert_down, norm_final):
    batch, seq, d = x.shape
    ctx_len = ctx.shape[1]
    assert d == D_MODEL and w_ada.shape[0] == 1, "single-layer block with D_MODEL features"
    t = batch * seq

    c8 = jnp.zeros((8, d), F32).at[:batch].set(c).at[batch].set(c_ctx)
    mod = _ada(c8, w_ada[0], b_ada[0][None, :])
    sh1, sc1, g1, sh2, sc2, g2 = [mod[:, d * i:d * (i + 1)] for i in range(6)]
    lat = lambda m: m[:batch, None, :]
    cx = lambda m: m[batch:batch + 1, None, :]

    w_main, w_lr = _wprep(jnp.swapaxes(w_in[0], 0, 1))
    nm = norm_mix[0][None, :]

    p_ctx, lr_ctx = _proj(ctx.reshape(batch * ctx_len, d), cx(sh1), cx(sc1), nm, w_main, w_lr,
                          tm=ctx_len, rows_per_mod=batch * ctx_len)
    x2 = x.reshape(t, d)
    p_lat, lr_lat = _proj(x2, lat(sh1), lat(sc1), nm, w_main, w_lr, tm=PROJ_TM, rows_per_mod=seq)

    wlr_f = jnp.zeros((LANES, HEADS * DK), F32).at[:GLA_RANK].set(gla_lr_w[0, 0])
    wlr_b = jnp.zeros((LANES, HEADS * DK), F32).at[GLA_RANK:2 * GLA_RANK].set(gla_lr_w[0, 1])
    blr_f, blr_b = gla_lr_b[0, 0][None, :], gla_lr_b[0, 1][None, :]
    zero_state = jnp.zeros((batch, HEADS, DK, DV), F32)

    _, _, gs_f, gs_b = _gla(p_ctx, lr_ctx, wlr_f, blr_f, wlr_b, blr_b, zero_state, zero_state,
                            batch, ctx_len, tb=ctx_len)
    gla_f, gla_b, _, _ = _gla(p_lat, lr_lat, wlr_f, blr_f, wlr_b, blr_b, gs_f, gs_b, batch, seq, tb=SCAN_TB)

    ones = jnp.ones((ctx_len, DK), F32)
    _, _, rs_f, rs_b = _ret(p_ctx, ones, jnp.zeros_like(ones), zero_state, zero_state, batch, ctx_len, tb=ctx_len)
    cos, sin = _rope_tables(seq)
    ret_f, ret_b, _, _ = _ret(p_lat, cos, sin, rs_f, rs_b, batch, seq, tb=SCAN_TB)

    wrt = jnp.zeros((d, ROUTER_W), F32)
    wrt = wrt.at[:, :N_GROUPS].set(w_router_group[0]).at[:, EXP_ROW0:EXP_ROW0 + N_EXPERTS].set(w_router_expert[0])
    wrt_hi = wrt.astype(BF16)
    wrt = jnp.concatenate([wrt_hi, (wrt - wrt_hi.astype(F32)).astype(BF16)], axis=1)
    brt = jnp.zeros((1, ROUTER_W), F32)
    brt = brt.at[0, :N_GROUPS].set(b_router_group[0]).at[0, EXP_ROW0:EXP_ROW0 + N_EXPERTS].set(b_router_expert[0])
    h, hn, ids8, ew8, cnt = _merge(gla_f, gla_b, ret_f, ret_b, p_lat, x2, lat(g1), lat(sh2),
                                   norm_ffn[0][None, None, :] * (1.0 + lat(sc2)),
                                   jnp.tile(gla_norm[0], HEADS)[None, :],
                                   w_branch_gla[0].astype(BF16), w_branch_ret[0].astype(BF16),
                                   w_out[0].astype(BF16), wrt, brt, seq, tm=MERGE_TM)

    block_e, src0, stok, nused, pos, nblk = _dispatch_indices(ids8[:TOP_K], ids8[TOP_K:2 * TOP_K], cnt[:, 0], t)
    y_pad = _experts(block_e, src0, stok, nused, hn, w_expert_gate[0], w_expert_up[0], w_expert_down[0], nblk)
    out = _combine(pos, y_pad, h, ew8, lat(g2), norm_final[None, :], seq)
    return out.reshape(batch, seq, d)
```

```python
import functools

import numpy as np
import jax
import jax.numpy as jnp
from jax import lax
from jax.experimental import pallas as pl
from jax.experimental.pallas import tpu as pltpu

F32 = jnp.float32
BF16 = jnp.bfloat16

D_MODEL = 1024
GRID_W = 64
HEADS = 4
DK = 128
DV = 256
GLA_RANK = 16
GLA_TAU = 16.0
GLA_CHUNK = 64
GLA_SUB = 16
RET_CHUNK = 128
ROPE_BASE = 10000.0
N_GROUPS = 4
EXPERTS_PER_GROUP = 8
N_EXPERTS = N_GROUPS * EXPERTS_PER_GROUP
TOP_K = 2
EXPERT_FF = 256
MOE_BLOCK = 128
NORM_EPS = 1e-6

COL_GQ, COL_GK, COL_GV, COL_GG = 0, 512, 1024, 2048
COL_RQ, COL_RK, COL_RV, COL_RG = 3072, 3584, 4096, 5120
COL_MG, COL_MR = 6144, 7168
PROJ_W = 8192
LANES = 128
FEAT_TILES = D_MODEL // LANES
ROUTER_W = 128
EXP_ROW0 = 8

VMEM_LIMIT = 56 * 1024 * 1024

PROJ_TM = 1024
PROJ_TN = 2048
SCAN_TB = 1024
GLA_EXACT_TB = 256
GLA_SAFE_RANGE = 60.0
MERGE_TM = 256
MERGE_SPLIT = 1
COMBINE_TM = 256


def _cparams(sem):
    return pltpu.CompilerParams(dimension_semantics=sem, vmem_limit_bytes=VMEM_LIMIT)


def _sigmoid(x):
    return 1.0 / (1.0 + jnp.exp(-x))


def _silu(x):
    return x * _sigmoid(x)


def _dot(a, b):
    return jnp.dot(a, b, preferred_element_type=F32)


def _dot_nt(a, b):
    return lax.dot_general(a, b, (((1,), (1,)), ((), ())), preferred_element_type=F32)


def _ada_body(c_ref, w_ref, b_ref, o_ref):
    s = _silu(c_ref[...])
    o_ref[...] = _dot(s.astype(BF16), w_ref[...].astype(BF16)) + b_ref[...]


def _ada(c8, w, b):
    n = w.shape[1]
    tn = 1536
    return pl.pallas_call(
        _ada_body,
        grid=(n // tn,),
        in_specs=[pl.BlockSpec((8, D_MODEL), lambda j: (0, 0)),
                  pl.BlockSpec((D_MODEL, tn), lambda j: (0, j)),
                  pl.BlockSpec((1, tn), lambda j: (0, j))],
        out_specs=pl.BlockSpec((8, tn), lambda j: (0, j)),
        out_shape=jax.ShapeDtypeStruct((8, n), F32),
        compiler_params=_cparams(("arbitrary",)),
        name="ada",
    )(c8, w, b)


LR_COL0 = COL_GG + HEADS * DV
WPREP_ROWS = 1024


def _wprep_body(w_ref, wlr_ref, main_ref, lr_ref):
    main_ref[...] = w_ref[...].astype(BF16)
    pad = jnp.zeros((LANES - 2 * GLA_RANK, D_MODEL), F32)
    lr_ref[...] = jnp.concatenate([wlr_ref[...], pad], axis=0).astype(BF16)


def _wprep(wt):
    steps_before = LR_COL0 // WPREP_ROWS

    def src_row(i):
        row = jnp.where(i < steps_before, i * WPREP_ROWS, i * WPREP_ROWS + 2 * GLA_RANK)
        return (pl.multiple_of(row, 2 * GLA_RANK), 0)

    return pl.pallas_call(
        _wprep_body,
        grid=(PROJ_W // WPREP_ROWS,),
        in_specs=[pl.BlockSpec((pl.Element(WPREP_ROWS), pl.Element(D_MODEL)), src_row),
                  pl.BlockSpec((pl.Element(2 * GLA_RANK), pl.Element(D_MODEL)), lambda i: (LR_COL0, 0))],
        out_specs=[pl.BlockSpec((WPREP_ROWS, D_MODEL), lambda i: (i, 0)),
                   pl.BlockSpec((LANES, D_MODEL), lambda i: (0, 0))],
        out_shape=[jax.ShapeDtypeStruct((PROJ_W, D_MODEL), BF16), jax.ShapeDtypeStruct((LANES, D_MODEL), BF16)],
        compiler_params=_cparams(("arbitrary",)),
        name="wprep",
    )(wt, wt)


def _proj_body(x_ref, sh_ref, sc_ref, g_ref, w_ref, wlr_ref, o_ref, lr_ref, h_ref):
    @pl.when(pl.program_id(1) == 0)
    def _():
        x = x_ref[...]
        ms = jnp.mean(x * x, axis=-1, keepdims=True)
        y = x * lax.rsqrt(ms + NORM_EPS) * g_ref[...]
        hb = (y * (1.0 + sc_ref[0]) + sh_ref[0]).astype(BF16)
        h_ref[...] = hb
        lr_ref[...] = _dot_nt(hb, wlr_ref[...])

    o_ref[...] = _dot_nt(h_ref[...], w_ref[...]).astype(BF16)


def _proj(x2, sh, sc, g, w_main, w_lr, tm, rows_per_mod):
    t = x2.shape[0]
    tn = PROJ_TN
    tiles_per_mod = rows_per_mod // tm
    mod_map = lambda i, j: (i // tiles_per_mod, 0, 0)
    return pl.pallas_call(
        _proj_body,
        grid=(t // tm, PROJ_W // tn),
        in_specs=[pl.BlockSpec((tm, D_MODEL), lambda i, j: (i, 0)),
                  pl.BlockSpec((1, 1, D_MODEL), mod_map),
                  pl.BlockSpec((1, 1, D_MODEL), mod_map),
                  pl.BlockSpec((1, D_MODEL), lambda i, j: (0, 0)),
                  pl.BlockSpec((tn, D_MODEL), lambda i, j: (j, 0)),
                  pl.BlockSpec((LANES, D_MODEL), lambda i, j: (0, 0))],
        out_specs=[pl.BlockSpec((tm, tn), lambda i, j: (i, j)),
                   pl.BlockSpec((tm, LANES), lambda i, j: (i, 0))],
        out_shape=[jax.ShapeDtypeStruct((t, PROJ_W), BF16),
                   jax.ShapeDtypeStruct((t, LANES), F32)],
        scratch_shapes=[pltpu.VMEM((tm, D_MODEL), BF16)],
        compiler_params=_cparams(("parallel", "arbitrary")),
        name="proj",
    )(x2, sh, sc, g, w_main, w_lr)


def _gla_consts():
    c, s = GLA_CHUNK, GLA_SUB
    msk = {}
    for rev in (False, True):
        cols = []
        for i in range(c // s):
            keys = np.arange(c)[_key_rows(i, rev)][None, :]
            t = np.arange(c)[:, None]
            visible = (keys > t) if rev else (keys <= t)
            cols.append(np.where(t // s == i, visible, False))
        msk[rev] = np.concatenate(cols, axis=1).astype(np.float32)
    return msk


def _key_rows(i, rev):
    return slice(GLA_SUB * i, GLA_CHUNK) if rev else slice(0, GLA_SUB * (i + 1))


def _log_gate(lr, w, b):
    z = _dot(lr.astype(BF16), w.astype(BF16)) + b
    return -(jnp.maximum(-z, 0.0) + jnp.log(1.0 + jnp.exp(-jnp.abs(z)))) * (1.0 / GLA_TAU)


def _subchunk_scan(la, rev):
    c = la.shape[0]
    pos = lax.broadcasted_iota(jnp.int32, la.shape, 0) % GLA_SUB
    w = la
    step = 1
    while step < GLA_SUB:
        if rev:
            w = w + jnp.where(pos < GLA_SUB - step, pltpu.roll(w, c - step, 0), 0.0)
        else:
            w = w + jnp.where(pos >= step, pltpu.roll(w, step, 0), 0.0)
        step *= 2
    return w


def _gla_intra_exact(q, kf, v, b, rev, scratch):
    q_s, b_s, o_s = scratch
    c = GLA_CHUNK
    q_s[...] = q.astype(F32) * (DK ** -0.5)
    b_s[...] = b
    vf = v.astype(F32)
    row = lax.broadcasted_iota(jnp.int32, (c, 1), 0)

    def body(t, carry):
        qt = q_s[pl.ds(t, 1), :]
        bt = b_s[pl.ds(t, 1), :]
        visible = (row > t) if rev else (row <= t)
        decay = jnp.exp(jnp.where(visible, bt - b, -jnp.inf))
        col = jnp.sum(kf * decay * qt, axis=1, keepdims=True)
        o_s[pl.ds(t, 1), :] = jnp.sum(col * vf, axis=0, keepdims=True)
        return carry

    lax.fori_loop(0, c, body, 0)
    return o_s[...]


def _gla_chunk(q, k, v, la, state, mask, rev, exact_scratch=None):
    c, s = GLA_CHUNK, GLA_SUB
    nsub = c // s
    w = _subchunk_scan(la, rev)
    tot = [w[s * i:s * i + 1] if rev else w[s * (i + 1) - 1:s * (i + 1)] for i in range(nsub)]
    anchors = [None] * nsub
    acc = jnp.zeros_like(tot[0])
    worst = jnp.zeros_like(tot[0])
    for i in (reversed(range(nsub)) if rev else range(nsub)):
        anchors[i] = acc
        acc = acc + tot[i]
        worst = jnp.maximum(worst, -tot[i])
    b_end = acc
    ref = jnp.concatenate([jnp.broadcast_to(a, (s, DK)) for a in anchors], axis=0)
    b = w + ref

    qt = q.astype(F32) * (DK ** -0.5) * jnp.exp(w)
    kf = k.astype(F32)
    inter = _dot((qt * jnp.exp(ref)).astype(BF16), state.astype(BF16))

    if exact_scratch is None:
        kstack = jnp.concatenate([kf[_key_rows(i, rev)] * jnp.exp(anchors[i] - b[_key_rows(i, rev)])
                                  for i in range(nsub)], axis=0).astype(BF16)
        vstack = jnp.concatenate([v[_key_rows(i, rev)] for i in range(nsub)], axis=0)
        scores = jnp.where(mask != 0.0, _dot_nt(qt.astype(BF16), kstack), 0.0)
        out = inter + _dot(scores.astype(BF16), vstack)
    else:
        out = inter + _gla_intra_exact(q, kf, v, b, rev, exact_scratch)

    kst = kf * jnp.exp(b_end - b)
    xt = jnp.concatenate([kst, jnp.broadcast_to(jnp.exp(b_end), (c, DK))], axis=0).T
    new_state = xt[:, c:c + 1] * state + _dot(xt[:, :c].astype(BF16), v)
    return out, new_state, worst


def _gla_body(qf_ref, kf_ref, vf_ref, lrf_ref, qb_ref, kb_ref, vb_ref, lrb_ref,
              wf_ref, bf_ref, wb_ref, bb_ref, mskf_ref, mskb_ref,
              s0f_ref, s0b_ref, of_ref, ob_ref, sf_ref, sb_ref, rng_ref, *exact_scratch, nchunks):
    @pl.when(pl.program_id(2) == 0)
    def _():
        sf_ref[...] = s0f_ref[...]
        sb_ref[...] = s0b_ref[...]

    c = GLA_CHUNK
    scratch = exact_scratch or None
    la_f = _log_gate(lrf_ref[...], wf_ref[...], bf_ref[...])
    la_b = _log_gate(lrb_ref[...], wb_ref[...], bb_ref[...])
    mskf, mskb = mskf_ref[...], mskb_ref[...]
    st_f = sf_ref[0, 0]
    st_b = sb_ref[0, 0]
    worst = jnp.zeros((1, DK), F32)
    for n in range(nchunks):
        rf = slice(c * n, c * (n + 1))
        o, st_f, wf = _gla_chunk(qf_ref[rf, :], kf_ref[rf, :], vf_ref[rf, :], la_f[rf, :], st_f, mskf, False,
                                 scratch)
        of_ref[rf, :] = o.astype(BF16)
        m = nchunks - 1 - n
        rb = slice(c * m, c * (m + 1))
        o, st_b, wb = _gla_chunk(qb_ref[rb, :], kb_ref[rb, :], vb_ref[rb, :], la_b[rb, :], st_b, mskb, True,
                                 scratch)
        ob_ref[rb, :] = o.astype(BF16)
        worst = jnp.maximum(worst, jnp.maximum(wf, wb))
    sf_ref[0, 0] = st_f
    sb_ref[0, 0] = st_b
    rng_ref[0, 0, 0] = jnp.broadcast_to(worst, (8, DK))


def _gla(p, lr, wlr_f, blr_f, wlr_b, blr_b, s0f, s0b, batch, seq, tb, exact=False):
    nb = seq // tb
    msk = _gla_consts()
    mskf, mskb = jnp.asarray(msk[False]), jnp.asarray(msk[True])

    def fwd(col0, width):
        return lambda b, h, i: (b * nb + i, col0 // width + h)

    def bwd(col0, width):
        return lambda b, h, i: (b * nb + nb - 1 - i, col0 // width + h)

    const2 = lambda b, h, i: (0, 0)
    headcol = lambda b, h, i: (0, h)
    st_map = lambda b, h, i: (b, h, 0, 0)
    st_spec = pl.BlockSpec((1, 1, DK, DV), st_map)
    st_shape = jax.ShapeDtypeStruct((batch, HEADS, DK, DV), F32)
    o_shape = jax.ShapeDtypeStruct((batch * seq, HEADS * DV), BF16)
    return pl.pallas_call(
        functools.partial(_gla_body, nchunks=tb // GLA_CHUNK),
        grid=(batch, HEADS, nb),
        in_specs=[pl.BlockSpec((tb, DK), fwd(COL_GQ, DK)),
                  pl.BlockSpec((tb, DK), fwd(COL_GK, DK)),
                  pl.BlockSpec((tb, DV), fwd(COL_GV, DV)),
                  pl.BlockSpec((tb, LANES), lambda b, h, i: (b * nb + i, 0)),
                  pl.BlockSpec((tb, DK), bwd(COL_GQ, DK)),
                  pl.BlockSpec((tb, DK), bwd(COL_GK, DK)),
                  pl.BlockSpec((tb, DV), bwd(COL_GV, DV)),
                  pl.BlockSpec((tb, LANES), lambda b, h, i: (b * nb + nb - 1 - i, 0)),
                  pl.BlockSpec((LANES, DK), headcol),
                  pl.BlockSpec((1, DK), headcol),
                  pl.BlockSpec((LANES, DK), headcol),
                  pl.BlockSpec((1, DK), headcol),
                  pl.BlockSpec(mskf.shape, const2),
                  pl.BlockSpec(mskb.shape, const2),
                  st_spec, st_spec],
        out_specs=[pl.BlockSpec((tb, DV), lambda b, h, i: (b * nb + i, h)),
                   pl.BlockSpec((tb, DV), lambda b, h, i: (b * nb + nb - 1 - i, h)),
                   st_spec, st_spec,
                   pl.BlockSpec((1, 1, 1, 8, DK), lambda b, h, i: (b, h, i, 0, 0))],
        out_shape=[o_shape, o_shape, st_shape, st_shape,
                   jax.ShapeDtypeStruct((batch, HEADS, nb, 8, DK), F32)],
        scratch_shapes=([pltpu.VMEM((GLA_CHUNK, DK), F32), pltpu.VMEM((GLA_CHUNK, DK), F32),
                         pltpu.VMEM((GLA_CHUNK, DV), F32)] if exact else []),
        compiler_params=_cparams(("parallel", "parallel", "arbitrary")),
        name="gla_exact" if exact else "gla",
    )(p, p, p, lr, p, p, p, lr, wlr_f, blr_f, wlr_b, blr_b, mskf, mskb, s0f, s0b)


def _ret_consts():
    c = RET_CHUNK
    hh = np.arange(HEADS, dtype=np.float64)
    lg = {False: np.log1p(-np.exp2(-5.0 - hh)), True: np.log1p(-np.exp2(-5.5 - hh))}
    t = np.arange(c, dtype=np.float64)[:, None]
    u = np.arange(c, dtype=np.float64)[None, :]
    pos = np.arange(c, dtype=np.float64)
    out = {}
    for rev in (False, True):
        g = lg[rev][:, None, None]
        if not rev:
            dmat = np.where(u <= t, np.exp((t - u) * g), 0.0)
            qd = np.exp((pos + 1.0)[None, :] * lg[rev][:, None])
            kd = np.exp((c - 1.0 - pos)[None, :] * lg[rev][:, None])
        else:
            dmat = np.where(u > t, np.exp((u - t) * g), 0.0)
            qd = np.exp((c - pos)[None, :] * lg[rev][:, None])
            kd = np.exp(pos[None, :] * lg[rev][:, None])
        cd = np.exp(c * lg[rev])
        out[rev] = (dmat.astype(np.float32),
                    np.repeat(qd[:, :, None], DV, axis=2).astype(np.float32),
                    np.repeat(kd[:, :, None], DK, axis=2).astype(np.float32),
                    np.repeat(cd[:, None, None], DV, axis=2).astype(np.float32))
    return out


def _rope(x, cos, sin):
    lane = lax.broadcasted_iota(jnp.int32, x.shape, 1)
    partner = jnp.where((lane % 64) < 32, pltpu.roll(x, DK - 32, 1), pltpu.roll(x, 32, 1))
    return x * cos + partner * sin


def _ret_chunk(q, k, v, cos, sin, state, dmat, qd, kd, cd):
    qr = _rope(q.astype(F32), cos, sin).astype(BF16)
    kr = _rope(k.astype(F32) * (DK ** -0.5), cos, sin)
    inter = _dot(qr, state.astype(BF16)) * qd
    scores = _dot_nt(qr, kr.astype(BF16)) * dmat
    out = inter + _dot(scores.astype(BF16), v)
    new_state = cd * state + _dot((kr * kd).T.astype(BF16), v)
    return out, new_state


def _ret_body(qf_ref, kf_ref, vf_ref, cosf_ref, sinf_ref, qb_ref, kb_ref, vb_ref, cosb_ref, sinb_ref,
              dmf_ref, qdf_ref, kdf_ref, cdf_ref, dmb_ref, qdb_ref, kdb_ref, cdb_ref,
              s0f_ref, s0b_ref, of_ref, ob_ref, sf_ref, sb_ref, *, nchunks):
    @pl.when(pl.program_id(2) == 0)
    def _():
        sf_ref[...] = s0f_ref[...]
        sb_ref[...] = s0b_ref[...]

    c = RET_CHUNK
    cf = (dmf_ref[0], qdf_ref[0], kdf_ref[0], cdf_ref[0])
    cb = (dmb_ref[0], qdb_ref[0], kdb_ref[0], cdb_ref[0])
    st_f = sf_ref[0, 0]
    st_b = sb_ref[0, 0]
    for n in range(nchunks):
        rf = slice(c * n, c * (n + 1))
        o, st_f = _ret_chunk(qf_ref[rf, :], kf_ref[rf, :], vf_ref[rf, :], cosf_ref[rf, :], sinf_ref[rf, :],
                             st_f, *cf)
        of_ref[rf, :] = o.astype(BF16)
        m = nchunks - 1 - n
        rb = slice(c * m, c * (m + 1))
        o, st_b = _ret_chunk(qb_ref[rb, :], kb_ref[rb, :], vb_ref[rb, :], cosb_ref[rb, :], sinb_ref[rb, :],
                             st_b, *cb)
        ob_ref[rb, :] = o.astype(BF16)
    sf_ref[0, 0] = st_f
    sb_ref[0, 0] = st_b


def _ret(p, cos, sin, s0f, s0b, batch, seq, tb):
    nb = seq // tb
    consts = _ret_consts()

    def fwd(col0, width):
        return lambda b, h, i: (b * nb + i, col0 // width + h)

    def bwd(col0, width):
        return lambda b, h, i: (b * nb + nb - 1 - i, col0 // width + h)

    head3 = lambda b, h, i: (h, 0, 0)
    st_spec = pl.BlockSpec((1, 1, DK, DV), lambda b, h, i: (b, h, 0, 0))
    st_shape = jax.ShapeDtypeStruct((batch, HEADS, DK, DV), F32)
    o_shape = jax.ShapeDtypeStruct((batch * seq, HEADS * DV), BF16)
    const_specs = []
    const_args = []
    for rev in (False, True):
        for a in consts[rev]:
            const_specs.append(pl.BlockSpec((1,) + a.shape[1:], head3))
            const_args.append(jnp.asarray(a))
    return pl.pallas_call(
        functools.partial(_ret_body, nchunks=tb // RET_CHUNK),
        grid=(batch, HEADS, nb),
        in_specs=[pl.BlockSpec((tb, DK), fwd(COL_RQ, DK)),
                  pl.BlockSpec((tb, DK), fwd(COL_RK, DK)),
                  pl.BlockSpec((tb, DV), fwd(COL_RV, DV)),
                  pl.BlockSpec((tb, DK), lambda b, h, i: (i, 0)),
                  pl.BlockSpec((tb, DK), lambda b, h, i: (i, 0)),
                  pl.BlockSpec((tb, DK), bwd(COL_RQ, DK)),
                  pl.BlockSpec((tb, DK), bwd(COL_RK, DK)),
                  pl.BlockSpec((tb, DV), bwd(COL_RV, DV)),
                  pl.BlockSpec((tb, DK), lambda b, h, i: (nb - 1 - i, 0)),
                  pl.BlockSpec((tb, DK), lambda b, h, i: (nb - 1 - i, 0))]
                 + const_specs + [st_spec, st_spec],
        out_specs=[pl.BlockSpec((tb, DV), lambda b, h, i: (b * nb + i, h)),
                   pl.BlockSpec((tb, DV), lambda b, h, i: (b * nb + nb - 1 - i, h)),
                   st_spec, st_spec],
        out_shape=[o_shape, o_shape, st_shape, st_shape],
        compiler_params=_cparams(("parallel", "parallel", "arbitrary")),
        name="ret",
    )(p, p, p, cos, sin, p, p, p, cos, sin, *const_args, s0f, s0b)


def _rope_tables(seq):
    n = DK // 4
    inv = np.float32(ROPE_BASE) ** (-np.arange(n, dtype=np.float32) / np.float32(n))
    pos = np.arange(seq)
    ar = (pos // GRID_W).astype(np.float32)[:, None] * inv[None, :]
    ac = (pos % GRID_W).astype(np.float32)[:, None] * inv[None, :]
    cos = np.concatenate([np.cos(ar), np.cos(ar), np.cos(ac), np.cos(ac)], axis=1)
    sin = np.concatenate([-np.sin(ar), np.sin(ar), -np.sin(ac), np.sin(ac)], axis=1)
    return jnp.asarray(cos, F32), jnp.asarray(sin, F32)


def _route(logits_t):
    g = [logits_t[i:i + 1] for i in range(N_GROUPS)]
    gmax = jnp.maximum(jnp.maximum(g[0], g[1]), jnp.maximum(g[2], g[3]))
    gsel = jnp.where(g[0] == gmax, 0, jnp.where(g[1] == gmax, 1, jnp.where(g[2] == gmax, 2, 3)))
    gsum = (jnp.exp(g[0] - gmax) + jnp.exp(g[1] - gmax)) + (jnp.exp(g[2] - gmax) + jnp.exp(g[3] - gmax))
    gw = 1.0 / gsum
    e = [logits_t[EXP_ROW0 + EXPERTS_PER_GROUP * i:EXP_ROW0 + EXPERTS_PER_GROUP * (i + 1)]
         for i in range(N_GROUPS)]
    el = jnp.where(gsel == 0, e[0], jnp.where(gsel == 1, e[1], jnp.where(gsel == 2, e[2], e[3])))
    row = lax.broadcasted_iota(jnp.int32, el.shape, 0).astype(F32)
    none = float(EXPERTS_PER_GROUP)
    m1 = jnp.max(el, axis=0, keepdims=True)
    i1 = jnp.min(jnp.where(el == m1, row, none), axis=0, keepdims=True)
    el2 = jnp.where(row == i1, -jnp.inf, el)
    m2 = jnp.max(el2, axis=0, keepdims=True)
    i2 = jnp.min(jnp.where(el2 == m2, row, none), axis=0, keepdims=True)
    r = jnp.exp(m2 - m1)
    w1 = gw / (1.0 + r)
    w2 = gw * r / (1.0 + r)
    base = gsel * EXPERTS_PER_GROUP
    ids = jnp.concatenate([base + i1.astype(jnp.int32), base + i2.astype(jnp.int32)], axis=0)
    return ids, jnp.concatenate([w1, w2], axis=0)


def _store_token_tiled(ref, val):
    m = val.shape[0]
    for j in range(FEAT_TILES):
        ref[pl.ds(j, m, stride=FEAT_TILES), :] = val[:, LANES * j:LANES * (j + 1)]


def _load_token_tiled(ref, m):
    return jnp.concatenate([ref[pl.ds(j, m, stride=FEAT_TILES), :] for j in range(FEAT_TILES)], axis=1)


def _merge_body(gf_ref, gb_ref, rf_ref, rb_ref, gg_ref, rg_ref, mg_ref, mr_ref, x_ref,
                g1_ref, sh2_ref, nfs_ref, gn_ref, wg_ref, wr_ref, wo_ref, wrt_ref, brt_ref, upper_ref,
                h_ref, hn_ref, ids_ref, ew_ref, cnt_ref, wg_s, wr_s, wo_s):
    @pl.when(pl.program_id(0) == 0)
    def _():
        cnt_ref[...] = jnp.zeros_like(cnt_ref)
        wg_s[...] = wg_ref[...]
        wr_s[...] = wr_ref[...]
        wo_s[...] = wo_ref[...]

    def rows_to_logits(r0, n):
        rows = pl.ds(r0, n)
        og = (gf_ref[rows, :] + gb_ref[rows, :]).astype(F32)
        orr = (rf_ref[rows, :] + rb_ref[rows, :]).astype(F32)
        gparts, rparts = [], []
        for hh in range(HEADS):
            seg = og[:, DV * hh:DV * (hh + 1)]
            ms = jnp.mean(seg * seg, axis=-1, keepdims=True)
            gparts.append(seg * lax.rsqrt(ms + NORM_EPS))
            seg = orr[:, DV * hh:DV * (hh + 1)]
            mu = jnp.mean(seg, axis=-1, keepdims=True)
            cen = seg - mu
            var = jnp.mean(cen * cen, axis=-1, keepdims=True)
            rparts.append(cen * lax.rsqrt(var + NORM_EPS))
        o_gla = (jnp.concatenate(gparts, axis=1) * gn_ref[...]).astype(BF16) * _silu(gg_ref[rows, :])
        o_ret = jnp.concatenate(rparts, axis=1).astype(BF16) * _silu(rg_ref[rows, :])
        y = (_sigmoid(mg_ref[rows, :]) * _dot(o_gla, wg_s[...]).astype(BF16)
             + _sigmoid(mr_ref[rows, :]) * _dot(o_ret, wr_s[...]).astype(BF16))
        h = x_ref[rows, :] + g1_ref[0] * _dot(y, wo_s[...])
        h_ref[rows, :] = h
        ms = jnp.mean(h * h, axis=-1, keepdims=True)
        hn = h * lax.rsqrt(ms + NORM_EPS) * nfs_ref[0] + sh2_ref[0]
        _store_token_tiled(hn_ref.at[pl.ds(r0 * FEAT_TILES, n * FEAT_TILES)], hn)
        hn_hi = hn.astype(BF16)
        hn_lo = (hn - hn_hi.astype(F32)).astype(BF16)
        both_w = _dot(hn_hi, wrt_ref[...])
        return (both_w[:, :ROUTER_W] + both_w[:, ROUTER_W:] + _dot(hn_lo, wrt_ref[:, :ROUTER_W])) + brt_ref[...]

    tm = x_ref.shape[0]
    n = tm // MERGE_SPLIT
    logits = jnp.concatenate([rows_to_logits(g * n, n) for g in range(MERGE_SPLIT)], axis=0)
    ids, ew = _route(logits.T)

    erow = lax.broadcasted_iota(jnp.int32, (N_EXPERTS, tm), 0)
    oh0 = jnp.where(erow == ids[0:1], 1.0, 0.0)
    oh1 = jnp.where(erow == ids[1:2], 1.0, 0.0)
    both = oh0 + oh1
    before = _dot(both.astype(BF16), upper_ref[...]) + cnt_ref[:, 0:1].astype(F32)
    rank0 = jnp.sum(oh0 * before, axis=0, keepdims=True)
    rank1 = jnp.sum(oh1 * before, axis=0, keepdims=True)
    total = cnt_ref[:, 0:1] + jnp.sum(both, axis=1, keepdims=True).astype(jnp.int32)
    cnt_ref[...] = jnp.broadcast_to(total, cnt_ref.shape)
    ids_ref[...] = jnp.concatenate([ids, rank0.astype(jnp.int32), rank1.astype(jnp.int32),
                                    jnp.zeros((4, tm), jnp.int32)], axis=0)
    ew_ref[...] = jnp.concatenate([ew, jnp.zeros((6, tm), F32)], axis=0)


def _merge(gf, gb, rf, rb, p, x2, g1, sh2, nfs, gn, wg, wr, wo, wrt, brt, seq, tm):
    t = x2.shape[0]
    tiles_per_batch = seq // tm
    row = lambda i: (i, 0)
    mod = lambda i: (i // tiles_per_batch, 0, 0)
    const = lambda i: (0, 0)
    tok = pl.BlockSpec((tm, D_MODEL), row)
    vec = pl.BlockSpec((1, D_MODEL), const)
    modspec = pl.BlockSpec((1, 1, D_MODEL), mod)
    wspec = pl.BlockSpec((D_MODEL, D_MODEL), const)
    upper = jnp.asarray(np.triu(np.ones((tm, tm), np.float32), 1), BF16)

    def pcol(col0):
        return pl.BlockSpec((tm, D_MODEL), lambda i: (i, col0 // D_MODEL))

    return pl.pallas_call(
        _merge_body,
        grid=(t // tm,),
        in_specs=[tok, tok, tok, tok, pcol(COL_GG), pcol(COL_RG), pcol(COL_MG), pcol(COL_MR), tok,
                  modspec, modspec, modspec, vec, wspec, wspec, wspec,
                  pl.BlockSpec((D_MODEL, 2 * ROUTER_W), const), pl.BlockSpec((1, ROUTER_W), const),
                  pl.BlockSpec((tm, tm), const)],
        out_specs=[tok, pl.BlockSpec((tm * FEAT_TILES, LANES), row),
                   pl.BlockSpec((8, tm), lambda i: (0, i)), pl.BlockSpec((8, tm), lambda i: (0, i)),
                   pl.BlockSpec((N_EXPERTS, LANES), const)],
        out_shape=[jax.ShapeDtypeStruct((t, D_MODEL), F32), jax.ShapeDtypeStruct((t * FEAT_TILES, LANES), F32),
                   jax.ShapeDtypeStruct((8, t), jnp.int32), jax.ShapeDtypeStruct((8, t), F32),
                   jax.ShapeDtypeStruct((N_EXPERTS, LANES), jnp.int32)],
        scratch_shapes=[pltpu.VMEM((D_MODEL, D_MODEL), BF16)] * 3,
        compiler_params=_cparams(("arbitrary",)),
        name="merge",
    )(gf, gb, rf, rb, p, p, p, p, x2, g1, sh2, nfs, gn, wg, wr, wo, wrt, brt, upper)


GATHER_UNROLL = 8


def _aligned(tok):
    off = tok * FEAT_TILES
    return off if isinstance(off, int) else pl.multiple_of(off, FEAT_TILES)


def _token_copy(src_hbm, dst, sem, src_tok, dst_tok):
    return pltpu.make_async_copy(src_hbm.at[pl.ds(_aligned(src_tok), FEAT_TILES), :],
                                 dst.at[pl.ds(_aligned(dst_tok), FEAT_TILES), :], sem)


def _wait_tokens(src_hbm, dst, sem):
    pltpu.make_async_copy(src_hbm.at[pl.ds(0, dst.shape[0]), :], dst, sem).wait()


GATHER_BUFS = 3
GATHER_AHEAD = GATHER_BUFS - 1


def _issue_rows(n, start_row, unrolled):
    if unrolled:
        for r in range(n):
            start_row(r, r % 2)
    else:
        def body(g, carry):
            for u in range(GATHER_UNROLL):
                start_row(g * GATHER_UNROLL + u, u % 2)
            return carry
        lax.fori_loop(0, n // GATHER_UNROLL, body, 0)


def _expert_body(be_ref, src0_ref, stok_ref, nused_ref, hn_hbm, wg_ref, wu_ref, wd_ref, y_ref, *scratch):
    i = pl.program_id(0)
    nused = nused_ref[0]
    bufs, sems = scratch[:GATHER_BUFS], scratch[GATHER_BUFS]
    wg_s, wu_s, wd_s = scratch[GATHER_BUFS + 1:]
    last = stok_ref.shape[0] - 1

    def issue(block, s, unrolled):
        base = src0_ref[block]

        def start_row(r, priority):
            tok = stok_ref[jnp.minimum(base + r, last)]
            _token_copy(hn_hbm, bufs[s], sems.at[s], tok, r).start(priority=priority)
        _issue_rows(MOE_BLOCK, start_row, unrolled)

    @pl.when(i == 0)
    def _():
        for b in range(GATHER_AHEAD):
            issue(b, b, False)

    @pl.when(jnp.logical_and(i < nused, jnp.logical_or(i == 0, be_ref[i] != be_ref[jnp.maximum(i - 1, 0)])))
    def _():
        wg_s[...] = wg_ref[0].astype(BF16)
        wu_s[...] = wu_ref[0].astype(BF16)
        wd_s[...] = wd_ref[0].astype(BF16)

    for cur in range(GATHER_BUFS):
        mine = i % GATHER_BUFS == cur

        @pl.when(jnp.logical_and(mine, i < nused))
        def _():
            _wait_tokens(hn_hbm, bufs[cur], sems.at[cur])
            issue(i + GATHER_AHEAD, (cur + GATHER_AHEAD) % GATHER_BUFS, True)
            xb = _load_token_tiled(bufs[cur], MOE_BLOCK).astype(BF16)
            gate = _dot(xb, wg_s[...])
            up = _dot(xb, wu_s[...])
            hid = (_silu(gate) * up).astype(BF16)
            _store_token_tiled(y_ref, _dot(hid, wd_s[...]))

        @pl.when(jnp.logical_and(mine, jnp.logical_and(i >= nused, i < nused + GATHER_AHEAD)))
        def _():
            _wait_tokens(hn_hbm, bufs[cur], sems.at[cur])

    @pl.when(i >= nused)
    def _():
        y_ref[...] = jnp.zeros_like(y_ref)


def _experts(block_e, src0, stok, nused, hn, w_gate, w_up, w_down, nblk):
    wmap = lambda i, be, s0, st, nu: (be[i], 0, 0)
    rows = MOE_BLOCK * FEAT_TILES
    return pl.pallas_call(
        _expert_body,
        grid_spec=pltpu.PrefetchScalarGridSpec(
            num_scalar_prefetch=4,
            grid=(nblk,),
            in_specs=[pl.BlockSpec(memory_space=pl.ANY),
                      pl.BlockSpec((1, D_MODEL, EXPERT_FF), wmap),
                      pl.BlockSpec((1, D_MODEL, EXPERT_FF), wmap),
                      pl.BlockSpec((1, EXPERT_FF, D_MODEL), wmap)],
            out_specs=pl.BlockSpec((rows, LANES), lambda i, be, s0, st, nu: (i, 0)),
            scratch_shapes=[pltpu.VMEM((rows, LANES), F32)] * GATHER_BUFS
                           + [pltpu.SemaphoreType.DMA((GATHER_BUFS,)),
                              pltpu.VMEM((D_MODEL, EXPERT_FF), BF16), pltpu.VMEM((D_MODEL, EXPERT_FF), BF16),
                              pltpu.VMEM((EXPERT_FF, D_MODEL), BF16)]),
        out_shape=jax.ShapeDtypeStruct((nblk * rows, LANES), F32),
        compiler_params=_cparams(("arbitrary",)),
        name="experts",
    )(block_e, src0, stok, nused, hn, w_gate, w_up, w_down)


def _combine_body(pos_ref, y_hbm, h_ref, ew_ref, g2_ref, nf_ref, o_ref, *scratch):
    i = pl.program_id(0)
    n = pl.num_programs(0)
    tm = COMBINE_TM
    t = pos_ref.shape[0] // TOP_K
    bufs, sems = scratch[:GATHER_BUFS], scratch[GATHER_BUFS]

    def issue(tile, s, unrolled):
        def start_row(j, priority):
            k, r = j % TOP_K, j // TOP_K
            _token_copy(y_hbm, bufs[s].at[k], sems.at[s], pos_ref[k * t + tile * tm + r], r).start(priority=priority)
        _issue_rows(tm * TOP_K, start_row, unrolled)

    def wait(s):
        for k in range(TOP_K):
            _wait_tokens(y_hbm, bufs[s].at[k], sems.at[s])

    @pl.when(i == 0)
    def _():
        for b in range(GATHER_AHEAD):
            issue(b, b, False)

    for cur in range(GATHER_BUFS):
        @pl.when(i % GATHER_BUFS == cur)
        def _():
            wait(cur)
            issue(jnp.minimum(i + GATHER_AHEAD, n - 1), (cur + GATHER_AHEAD) % GATHER_BUFS, True)
            wt = jnp.concatenate([ew_ref[...]] * (LANES // 8), axis=0).T
            moe = (wt[:, 0:1] * _load_token_tiled(bufs[cur].at[0], tm)
                   + wt[:, 1:2] * _load_token_tiled(bufs[cur].at[1], tm))
            h = h_ref[...] + g2_ref[0] * moe
            ms = jnp.mean(h * h, axis=-1, keepdims=True)
            o_ref[...] = h * lax.rsqrt(ms + NORM_EPS) * nf_ref[...]

        @pl.when(jnp.logical_and(i % GATHER_BUFS == cur, i == n - 1))
        def _():
            for ahead in range(1, GATHER_BUFS):
                wait((cur + ahead) % GATHER_BUFS)


def _combine(pos, y_pad, h, ew, g2, nf, seq):
    t = h.shape[0]
    tm = COMBINE_TM
    tiles_per_batch = seq // tm
    ybuf = pltpu.VMEM((TOP_K, tm * FEAT_TILES, LANES), F32)
    return pl.pallas_call(
        _combine_body,
        grid_spec=pltpu.PrefetchScalarGridSpec(
            num_scalar_prefetch=1,
            grid=(t // tm,),
            in_specs=[pl.BlockSpec(memory_space=pl.ANY),
                      pl.BlockSpec((tm, D_MODEL), lambda i, pos: (i, 0)),
                      pl.BlockSpec((8, tm), lambda i, pos: (0, i)),
                      pl.BlockSpec((1, 1, D_MODEL), lambda i, pos: (i // tiles_per_batch, 0, 0)),
                      pl.BlockSpec((1, D_MODEL), lambda i, pos: (0, 0))],
            out_specs=pl.BlockSpec((tm, D_MODEL), lambda i, pos: (i, 0)),
            scratch_shapes=[ybuf] * GATHER_BUFS + [pltpu.SemaphoreType.DMA((GATHER_BUFS,))]),
        out_shape=jax.ShapeDtypeStruct((t, D_MODEL), F32),
        compiler_params=_cparams(("arbitrary",)),
        name="combine",
    )(pos, y_pad, h, ew, g2, nf)


def _dispatch_indices(ids, ranks, counts, t):
    a = t * TOP_K
    nblk = a // MOE_BLOCK + N_EXPERTS + GATHER_AHEAD
    experts = jnp.arange(N_EXPERTS, dtype=jnp.int32)
    starts = jnp.cumsum(counts) - counts
    padded = (counts + MOE_BLOCK - 1) // MOE_BLOCK * MOE_BLOCK
    pends = jnp.cumsum(padded)
    pstarts = pends - padded
    block_start = jnp.arange(nblk, dtype=jnp.int32) * MOE_BLOCK
    block_e = jnp.minimum(jnp.sum((block_start[:, None] >= pends[None, :]).astype(jnp.int32), axis=1),
                          N_EXPERTS - 1)
    of_block = (block_e[:, None] == experts[None, :]).astype(jnp.int32)
    src0 = block_start + jnp.sum(of_block * (starts - pstarts)[None, :], axis=1)
    nused = pends[-1:] // MOE_BLOCK
    tok = jnp.arange(t, dtype=jnp.int32)[None, :]
    slot = jnp.arange(TOP_K, dtype=jnp.int32)[:, None]
    key = ids * a + tok * TOP_K + slot
    stok = (jnp.sort(key.reshape(a)) % a) // TOP_K
    ids_d = ids.reshape(a // LANES, LANES)
    pstart_of = jnp.zeros_like(ids_d)
    for e in range(N_EXPERTS):
        pstart_of = jnp.where(ids_d == e, pstarts[e], pstart_of)
    pos = (ranks.reshape(a // LANES, LANES) + pstart_of).reshape(a)
    i32 = lambda v: v.astype(jnp.int32)
    return i32(block_e), i32(src0), i32(stok), i32(nused), i32(pos), nblk


def kernel(x, c, ctx, c_ctx, w_ada, b_ada, norm_mix, norm_ffn, w_in, gla_lr_w, gla_lr_b, gla_norm,
           w_branch_gla, w_branch_ret, w_out, w_router_group, b_router_group, w_router_expert,
           b_router_expert, w_expert_gate, w_expert_up, w_expert_down, norm_final):
    batch, seq, d = x.shape
    ctx_len = ctx.shape[1]
    assert d == D_MODEL and w_ada.shape[0] == 1, "single-layer block with D_MODEL features"
    t = batch * seq

    c8 = jnp.zeros((8, d), F32).at[:batch].set(c).at[batch].set(c_ctx)
    mod = _ada(c8, w_ada[0], b_ada[0][None, :])
    sh1, sc1, g1, sh2, sc2, g2 = [mod[:, d * i:d * (i + 1)] for i in range(6)]
    lat = lambda m: m[:batch, None, :]
    cx = lambda m: m[batch:batch + 1, None, :]

    w_main, w_lr = _wprep(jnp.swapaxes(w_in[0], 0, 1))
    nm = norm_mix[0][None, :]

    p_ctx, lr_ctx = _proj(ctx.reshape(batch * ctx_len, d), cx(sh1), cx(sc1), nm, w_main, w_lr,
                          tm=ctx_len, rows_per_mod=batch * ctx_len)
    x2 = x.reshape(t, d)
    p_lat, lr_lat = _proj(x2, lat(sh1), lat(sc1), nm, w_main, w_lr, tm=PROJ_TM, rows_per_mod=seq)

    wlr_f = jnp.zeros((LANES, HEADS * DK), F32).at[:GLA_RANK].set(gla_lr_w[0, 0])
    wlr_b = jnp.zeros((LANES, HEADS * DK), F32).at[GLA_RANK:2 * GLA_RANK].set(gla_lr_w[0, 1])
    blr_f, blr_b = gla_lr_b[0, 0][None, :], gla_lr_b[0, 1][None, :]
    zero_state = jnp.zeros((batch, HEADS, DK, DV), F32)

    _, _, gs_f, gs_b, _ = _gla(p_ctx, lr_ctx, wlr_f, blr_f, wlr_b, blr_b, zero_state, zero_state,
                               batch, ctx_len, tb=ctx_len)
    gla_args = (p_lat, lr_lat, wlr_f, blr_f, wlr_b, blr_b, gs_f, gs_b, batch, seq)
    gla_f, gla_b, _, _, decay_range = _gla(*gla_args, tb=SCAN_TB)
    gla_f, gla_b = lax.cond(jnp.max(decay_range) > GLA_SAFE_RANGE,
                            lambda: tuple(_gla(*gla_args, tb=GLA_EXACT_TB, exact=True)[:2]),
                            lambda: (gla_f, gla_b))

    ones = jnp.ones((ctx_len, DK), F32)
    _, _, rs_f, rs_b = _ret(p_ctx, ones, jnp.zeros_like(ones), zero_state, zero_state, batch, ctx_len, tb=ctx_len)
    cos, sin = _rope_tables(seq)
    ret_f, ret_b, _, _ = _ret(p_lat, cos, sin, rs_f, rs_b, batch, seq, tb=SCAN_TB)

    wrt = jnp.zeros((d, ROUTER_W), F32)
    wrt = wrt.at[:, :N_GROUPS].set(w_router_group[0]).at[:, EXP_ROW0:EXP_ROW0 + N_EXPERTS].set(w_router_expert[0])
    wrt_hi = wrt.astype(BF16)
    wrt = jnp.concatenate([wrt_hi, (wrt - wrt_hi.astype(F32)).astype(BF16)], axis=1)
    brt = jnp.zeros((1, ROUTER_W), F32)
    brt = brt.at[0, :N_GROUPS].set(b_router_group[0]).at[0, EXP_ROW0:EXP_ROW0 + N_EXPERTS].set(b_router_expert[0])
    h, hn, ids8, ew8, cnt = _merge(gla_f, gla_b, ret_f, ret_b, p_lat, x2, lat(g1), lat(sh2),
                                   norm_ffn[0][None, None, :] * (1.0 + lat(sc2)),
                                   jnp.tile(gla_norm[0], HEADS)[None, :],
                                   w_branch_gla[0].astype(BF16), w_branch_ret[0].astype(BF16),
                                   w_out[0].astype(BF16), wrt, brt, seq, tm=MERGE_TM)

    block_e, src0, stok, nused, pos, nblk = _dispatch_indices(ids8[:TOP_K], ids8[TOP_K:2 * TOP_K], cnt[:, 0], t)
    y_pad = _experts(block_e, src0, stok, nused, hn, w_expert_gate[0], w_expert_up[0], w_expert_down[0], nblk)
    out = _combine(pos, y_pad, h, ew8, lat(g2), norm_final[None, :], seq)
    return out.reshape(batch, seq, d)
```

```python
import functools

import numpy as np
import jax
import jax.numpy as jnp
from jax import lax
from jax.experimental import pallas as pl
from jax.experimental.pallas import tpu as pltpu

F32 = jnp.float32
BF16 = jnp.bfloat16

D_MODEL = 1024
GRID_W = 64
HEADS = 4
DK = 128
DV = 256
GLA_RANK = 16
GLA_TAU = 16.0
GLA_CHUNK = 64
GLA_SUB = 16
RET_CHUNK = 128
ROPE_BASE = 10000.0
N_GROUPS = 4
EXPERTS_PER_GROUP = 8
N_EXPERTS = N_GROUPS * EXPERTS_PER_GROUP
TOP_K = 2
EXPERT_FF = 256
MOE_BLOCK = 128
NORM_EPS = 1e-6

COL_GQ, COL_GK, COL_GV, COL_GG = 0, 512, 1024, 2048
COL_RQ, COL_RK, COL_RV, COL_RG = 3072, 3584, 4096, 5120
COL_MG, COL_MR = 6144, 7168
PROJ_W = 8192
LANES = 128
FEAT_TILES = D_MODEL // LANES
ROUTER_W = 128
EXP_ROW0 = 8

VMEM_LIMIT = 56 * 1024 * 1024

PROJ_TM = 1024
PROJ_TN = 2048
SCAN_TB = 1024
GLA_EXACT_TB = 256
GLA_SAFE_RANGE = 60.0
MERGE_TM = 256
MERGE_SPLIT = 1
COMBINE_TM = 256


def _cparams(sem):
    return pltpu.CompilerParams(dimension_semantics=sem, vmem_limit_bytes=VMEM_LIMIT)


def _sigmoid(x):
    return 1.0 / (1.0 + jnp.exp(-x))


def _silu(x):
    return x * _sigmoid(x)


def _dot(a, b):
    return jnp.dot(a, b, preferred_element_type=F32)


def _dot_nt(a, b):
    return lax.dot_general(a, b, (((1,), (1,)), ((), ())), preferred_element_type=F32)


def _ada_body(c_ref, w_ref, b_ref, o_ref):
    s = _silu(c_ref[...])
    o_ref[...] = _dot(s.astype(BF16), w_ref[...].astype(BF16)) + b_ref[...]


def _ada(c8, w, b):
    n = w.shape[1]
    tn = 1536
    return pl.pallas_call(
        _ada_body,
        grid=(n // tn,),
        in_specs=[pl.BlockSpec((8, D_MODEL), lambda j: (0, 0)),
                  pl.BlockSpec((D_MODEL, tn), lambda j: (0, j)),
                  pl.BlockSpec((1, tn), lambda j: (0, j))],
        out_specs=pl.BlockSpec((8, tn), lambda j: (0, j)),
        out_shape=jax.ShapeDtypeStruct((8, n), F32),
        compiler_params=_cparams(("arbitrary",)),
        name="ada",
    )(c8, w, b)


LR_COL0 = COL_GG + HEADS * DV
WPREP_ROWS = 1024


def _wprep_body(w_ref, wlr_ref, main_ref, lr_ref):
    main_ref[...] = w_ref[...].astype(BF16)
    pad = jnp.zeros((LANES - 2 * GLA_RANK, D_MODEL), F32)
    lr_ref[...] = jnp.concatenate([wlr_ref[...], pad], axis=0).astype(BF16)


def _wprep(wt):
    steps_before = LR_COL0 // WPREP_ROWS

    def src_row(i):
        row = jnp.where(i < steps_before, i * WPREP_ROWS, i * WPREP_ROWS + 2 * GLA_RANK)
        return (pl.multiple_of(row, 2 * GLA_RANK), 0)

    return pl.pallas_call(
        _wprep_body,
        grid=(PROJ_W // WPREP_ROWS,),
        in_specs=[pl.BlockSpec((pl.Element(WPREP_ROWS), pl.Element(D_MODEL)), src_row),
                  pl.BlockSpec((pl.Element(2 * GLA_RANK), pl.Element(D_MODEL)), lambda i: (LR_COL0, 0))],
        out_specs=[pl.BlockSpec((WPREP_ROWS, D_MODEL), lambda i: (i, 0)),
                   pl.BlockSpec((LANES, D_MODEL), lambda i: (0, 0))],
        out_shape=[jax.ShapeDtypeStruct((PROJ_W, D_MODEL), BF16), jax.ShapeDtypeStruct((LANES, D_MODEL), BF16)],
        compiler_params=_cparams(("arbitrary",)),
        name="wprep",
    )(wt, wt)


def _proj_body(x_ref, sh_ref, sc_ref, g_ref, w_ref, wlr_ref, o_ref, lr_ref, h_ref):
    @pl.when(pl.program_id(1) == 0)
    def _():
        x = x_ref[...]
        ms = jnp.mean(x * x, axis=-1, keepdims=True)
        y = x * lax.rsqrt(ms + NORM_EPS) * g_ref[...]
        hb = (y * (1.0 + sc_ref[0]) + sh_ref[0]).astype(BF16)
        h_ref[...] = hb
        lr_ref[...] = _dot_nt(hb, wlr_ref[...])

    o_ref[...] = _dot_nt(h_ref[...], w_ref[...]).astype(BF16)


def _proj(x2, sh, sc, g, w_main, w_lr, tm, rows_per_mod):
    t = x2.shape[0]
    tn = PROJ_TN
    tiles_per_mod = rows_per_mod // tm
    mod_map = lambda i, j: (i // tiles_per_mod, 0, 0)
    return pl.pallas_call(
        _proj_body,
        grid=(t // tm, PROJ_W // tn),
        in_specs=[pl.BlockSpec((tm, D_MODEL), lambda i, j: (i, 0)),
                  pl.BlockSpec((1, 1, D_MODEL), mod_map),
                  pl.BlockSpec((1, 1, D_MODEL), mod_map),
                  pl.BlockSpec((1, D_MODEL), lambda i, j: (0, 0)),
                  pl.BlockSpec((tn, D_MODEL), lambda i, j: (j, 0)),
                  pl.BlockSpec((LANES, D_MODEL), lambda i, j: (0, 0))],
        out_specs=[pl.BlockSpec((tm, tn), lambda i, j: (i, j)),
                   pl.BlockSpec((tm, LANES), lambda i, j: (i, 0))],
        out_shape=[jax.ShapeDtypeStruct((t, PROJ_W), BF16),
                   jax.ShapeDtypeStruct((t, LANES), F32)],
        scratch_shapes=[pltpu.VMEM((tm, D_MODEL), BF16)],
        compiler_params=_cparams(("parallel", "arbitrary")),
        name="proj",
    )(x2, sh, sc, g, w_main, w_lr)


def _gla_consts():
    c, s = GLA_CHUNK, GLA_SUB
    msk = {}
    for rev in (False, True):
        cols = []
        for i in range(c // s):
            keys = np.arange(c)[_key_rows(i, rev)][None, :]
            t = np.arange(c)[:, None]
            visible = (keys > t) if rev else (keys <= t)
            cols.append(np.where(t // s == i, visible, False))
        msk[rev] = np.concatenate(cols, axis=1).astype(np.float32)
    return msk


def _key_rows(i, rev):
    return slice(GLA_SUB * i, GLA_CHUNK) if rev else slice(0, GLA_SUB * (i + 1))


def _log_gate(lr, w, b):
    z = _dot(lr.astype(BF16), w.astype(BF16)) + b
    return -(jnp.maximum(-z, 0.0) + jnp.log(1.0 + jnp.exp(-jnp.abs(z)))) * (1.0 / GLA_TAU)


def _subchunk_scan(la, rev):
    c = la.shape[0]
    pos = lax.broadcasted_iota(jnp.int32, la.shape, 0) % GLA_SUB
    w = la
    step = 1
    while step < GLA_SUB:
        if rev:
            w = w + jnp.where(pos < GLA_SUB - step, pltpu.roll(w, c - step, 0), 0.0)
        else:
            w = w + jnp.where(pos >= step, pltpu.roll(w, step, 0), 0.0)
        step *= 2
    return w


def _gla_intra_exact(q, kf, v, b, rev, scratch):
    q_s, b_s, o_s = scratch
    c = GLA_CHUNK
    q_s[...] = q.astype(F32) * (DK ** -0.5)
    b_s[...] = b
    vf = v.astype(F32)
    row = lax.broadcasted_iota(jnp.int32, (c, 1), 0)

    def body(t, carry):
        qt = q_s[pl.ds(t, 1), :]
        bt = b_s[pl.ds(t, 1), :]
        visible = (row > t) if rev else (row <= t)
        decay = jnp.exp(jnp.where(visible, bt - b, -jnp.inf))
        col = jnp.sum(kf * decay * qt, axis=1, keepdims=True)
        o_s[pl.ds(t, 1), :] = jnp.sum(col * vf, axis=0, keepdims=True)
        return carry

    lax.fori_loop(0, c, body, 0)
    return o_s[...]


def _gla_chunk(q, k, v, la, state_ref, mask, rev, exact_scratch=None):
    c, s = GLA_CHUNK, GLA_SUB
    nsub = c // s
    w = _subchunk_scan(la, rev)
    tot = [w[s * i:s * i + 1] if rev else w[s * (i + 1) - 1:s * (i + 1)] for i in range(nsub)]
    anchors = [None] * nsub
    acc = jnp.zeros_like(tot[0])
    worst = jnp.zeros_like(tot[0])
    for i in (reversed(range(nsub)) if rev else range(nsub)):
        anchors[i] = acc
        acc = acc + tot[i]
        worst = jnp.maximum(worst, -tot[i])
    b_end = acc
    ref = jnp.concatenate([jnp.broadcast_to(a, (s, DK)) for a in anchors], axis=0)
    b = w + ref

    qt = q.astype(F32) * (DK ** -0.5) * jnp.exp(w)
    kf = k.astype(F32)
    if exact_scratch is None:
        kstack = jnp.concatenate([kf[_key_rows(i, rev)] * jnp.exp(anchors[i] - b[_key_rows(i, rev)])
                                  for i in range(nsub)], axis=0).astype(BF16)
        vstack = jnp.concatenate([v[_key_rows(i, rev)] for i in range(nsub)], axis=0)
        scores = jnp.where(mask != 0.0, _dot_nt(qt.astype(BF16), kstack), 0.0)
        intra = _dot(scores.astype(BF16), vstack)
    else:
        intra = _gla_intra_exact(q, kf, v, b, rev, exact_scratch)

    kst = kf * jnp.exp(b_end - b)
    xt = jnp.concatenate([kst, jnp.broadcast_to(jnp.exp(b_end), (c, DK))], axis=0).T
    update = _dot(xt[:, :c].astype(BF16), v)
    state = state_ref[...]
    out = intra + _dot((qt * jnp.exp(ref)).astype(BF16), state.astype(BF16))
    state_ref[...] = xt[:, c:c + 1] * state + update
    return out, worst


def _gla_body(qf_ref, kf_ref, vf_ref, lrf_ref, qb_ref, kb_ref, vb_ref, lrb_ref,
              wf_ref, bf_ref, wb_ref, bb_ref, mskf_ref, mskb_ref,
              s0f_ref, s0b_ref, of_ref, ob_ref, sf_ref, sb_ref, rng_ref, *exact_scratch, nchunks):
    @pl.when(pl.program_id(2) == 0)
    def _():
        sf_ref[...] = s0f_ref[...]
        sb_ref[...] = s0b_ref[...]

    c = GLA_CHUNK
    scratch = exact_scratch or None
    la_f = _log_gate(lrf_ref[...], wf_ref[...], bf_ref[...])
    la_b = _log_gate(lrb_ref[...], wb_ref[...], bb_ref[...])
    mskf, mskb = mskf_ref[...], mskb_ref[...]
    st_f, st_b = sf_ref.at[0, 0], sb_ref.at[0, 0]
    worst = jnp.zeros((1, DK), F32)
    for n in range(nchunks):
        rf = slice(c * n, c * (n + 1))
        o, wf = _gla_chunk(qf_ref[rf, :], kf_ref[rf, :], vf_ref[rf, :], la_f[rf, :], st_f, mskf, False, scratch)
        of_ref[rf, :] = o.astype(BF16)
        m = nchunks - 1 - n
        rb = slice(c * m, c * (m + 1))
        o, wb = _gla_chunk(qb_ref[rb, :], kb_ref[rb, :], vb_ref[rb, :], la_b[rb, :], st_b, mskb, True, scratch)
        ob_ref[rb, :] = o.astype(BF16)
        worst = jnp.maximum(worst, jnp.maximum(wf, wb))
    rng_ref[0, 0, 0] = jnp.broadcast_to(worst, (8, DK))


def _gla(p, lr, wlr_f, blr_f, wlr_b, blr_b, s0f, s0b, batch, seq, tb, exact=False):
    nb = seq // tb
    msk = _gla_consts()
    mskf, mskb = jnp.asarray(msk[False]), jnp.asarray(msk[True])

    def fwd(col0, width):
        return lambda b, h, i: (b * nb + i, col0 // width + h)

    def bwd(col0, width):
        return lambda b, h, i: (b * nb + nb - 1 - i, col0 // width + h)

    const2 = lambda b, h, i: (0, 0)
    headcol = lambda b, h, i: (0, h)
    st_map = lambda b, h, i: (b, h, 0, 0)
    st_spec = pl.BlockSpec((1, 1, DK, DV), st_map)
    st_shape = jax.ShapeDtypeStruct((batch, HEADS, DK, DV), F32)
    o_shape = jax.ShapeDtypeStruct((batch * seq, HEADS * DV), BF16)
    return pl.pallas_call(
        functools.partial(_gla_body, nchunks=tb // GLA_CHUNK),
        grid=(batch, HEADS, nb),
        in_specs=[pl.BlockSpec((tb, DK), fwd(COL_GQ, DK)),
                  pl.BlockSpec((tb, DK), fwd(COL_GK, DK)),
                  pl.BlockSpec((tb, DV), fwd(COL_GV, DV)),
                  pl.BlockSpec((tb, LANES), lambda b, h, i: (b * nb + i, 0)),
                  pl.BlockSpec((tb, DK), bwd(COL_GQ, DK)),
                  pl.BlockSpec((tb, DK), bwd(COL_GK, DK)),
                  pl.BlockSpec((tb, DV), bwd(COL_GV, DV)),
                  pl.BlockSpec((tb, LANES), lambda b, h, i: (b * nb + nb - 1 - i, 0)),
                  pl.BlockSpec((LANES, DK), headcol),
                  pl.BlockSpec((1, DK), headcol),
                  pl.BlockSpec((LANES, DK), headcol),
                  pl.BlockSpec((1, DK), headcol),
                  pl.BlockSpec(mskf.shape, const2),
                  pl.BlockSpec(mskb.shape, const2),
                  st_spec, st_spec],
        out_specs=[pl.BlockSpec((tb, DV), lambda b, h, i: (b * nb + i, h)),
                   pl.BlockSpec((tb, DV), lambda b, h, i: (b * nb + nb - 1 - i, h)),
                   st_spec, st_spec,
                   pl.BlockSpec((1, 1, 1, 8, DK), lambda b, h, i: (b, h, i, 0, 0))],
        out_shape=[o_shape, o_shape, st_shape, st_shape,
                   jax.ShapeDtypeStruct((batch, HEADS, nb, 8, DK), F32)],
        scratch_shapes=([pltpu.VMEM((GLA_CHUNK, DK), F32), pltpu.VMEM((GLA_CHUNK, DK), F32),
                         pltpu.VMEM((GLA_CHUNK, DV), F32)] if exact else []),
        compiler_params=_cparams(("parallel", "parallel", "arbitrary")),
        name="gla_exact" if exact else "gla",
    )(p, p, p, lr, p, p, p, lr, wlr_f, blr_f, wlr_b, blr_b, mskf, mskb, s0f, s0b)


def _ret_consts():
    c = RET_CHUNK
    hh = np.arange(HEADS, dtype=np.float64)
    lg = {False: np.log1p(-np.exp2(-5.0 - hh)), True: np.log1p(-np.exp2(-5.5 - hh))}
    t = np.arange(c, dtype=np.float64)[:, None]
    u = np.arange(c, dtype=np.float64)[None, :]
    pos = np.arange(c, dtype=np.float64)
    out = {}
    for rev in (False, True):
        g = lg[rev][:, None, None]
        if not rev:
            dmat = np.where(u <= t, np.exp((t - u) * g), 0.0)
            qd = np.exp((pos + 1.0)[None, :] * lg[rev][:, None])
            kd = np.exp((c - 1.0 - pos)[None, :] * lg[rev][:, None])
        else:
            dmat = np.where(u > t, np.exp((u - t) * g), 0.0)
            qd = np.exp((c - pos)[None, :] * lg[rev][:, None])
            kd = np.exp(pos[None, :] * lg[rev][:, None])
        cd = np.exp(c * lg[rev])
        out[rev] = (dmat.astype(np.float32),
                    np.repeat(qd[:, :, None], DV, axis=2).astype(np.float32),
                    np.repeat(kd[:, :, None], DK, axis=2).astype(np.float32),
                    np.repeat(cd[:, None, None], DV, axis=2).astype(np.float32))
    return out


def _rope(x, cos, sin):
    lane = lax.broadcasted_iota(jnp.int32, x.shape, 1)
    partner = jnp.where((lane % 64) < 32, pltpu.roll(x, DK - 32, 1), pltpu.roll(x, 32, 1))
    return x * cos + partner * sin


def _ret_chunk(q, k, v, cos, sin, state_ref, dmat, qd, kd, cd):
    qr = _rope(q.astype(F32), cos, sin).astype(BF16)
    kr = _rope(k.astype(F32) * (DK ** -0.5), cos, sin)
    scores = _dot_nt(qr, kr.astype(BF16)) * dmat
    intra = _dot(scores.astype(BF16), v)
    update = _dot((kr * kd).T.astype(BF16), v)
    state = state_ref[...]
    out = intra + _dot(qr, state.astype(BF16)) * qd
    state_ref[...] = cd * state + update
    return out


def _ret_body(qf_ref, kf_ref, vf_ref, cosf_ref, sinf_ref, qb_ref, kb_ref, vb_ref, cosb_ref, sinb_ref,
              dmf_ref, qdf_ref, kdf_ref, cdf_ref, dmb_ref, qdb_ref, kdb_ref, cdb_ref,
              s0f_ref, s0b_ref, of_ref, ob_ref, sf_ref, sb_ref, *, nchunks):
    @pl.when(pl.program_id(2) == 0)
    def _():
        sf_ref[...] = s0f_ref[...]
        sb_ref[...] = s0b_ref[...]

    c = RET_CHUNK
    cf = (dmf_ref[0], qdf_ref[0], kdf_ref[0], cdf_ref[0])
    cb = (dmb_ref[0], qdb_ref[0], kdb_ref[0], cdb_ref[0])
    st_f, st_b = sf_ref.at[0, 0], sb_ref.at[0, 0]
    for n in range(nchunks):
        rf = slice(c * n, c * (n + 1))
        o = _ret_chunk(qf_ref[rf, :], kf_ref[rf, :], vf_ref[rf, :], cosf_ref[rf, :], sinf_ref[rf, :], st_f, *cf)
        of_ref[rf, :] = o.astype(BF16)
        m = nchunks - 1 - n
        rb = slice(c * m, c * (m + 1))
        o = _ret_chunk(qb_ref[rb, :], kb_ref[rb, :], vb_ref[rb, :], cosb_ref[rb, :], sinb_ref[rb, :], st_b, *cb)
        ob_ref[rb, :] = o.astype(BF16)


def _ret(p, cos, sin, s0f, s0b, batch, seq, tb):
    nb = seq // tb
    consts = _ret_consts()

    def fwd(col0, width):
        return lambda b, h, i: (b * nb + i, col0 // width + h)

    def bwd(col0, width):
        return lambda b, h, i: (b * nb + nb - 1 - i, col0 // width + h)

    head3 = lambda b, h, i: (h, 0, 0)
    st_spec = pl.BlockSpec((1, 1, DK, DV), lambda b, h, i: (b, h, 0, 0))
    st_shape = jax.ShapeDtypeStruct((batch, HEADS, DK, DV), F32)
    o_shape = jax.ShapeDtypeStruct((batch * seq, HEADS * DV), BF16)
    const_specs = []
    const_args = []
    for rev in (False, True):
        for a in consts[rev]:
            const_specs.append(pl.BlockSpec((1,) + a.shape[1:], head3))
            const_args.append(jnp.asarray(a))
    return pl.pallas_call(
        functools.partial(_ret_body, nchunks=tb // RET_CHUNK),
        grid=(batch, HEADS, nb),
        in_specs=[pl.BlockSpec((tb, DK), fwd(COL_RQ, DK)),
                  pl.BlockSpec((tb, DK), fwd(COL_RK, DK)),
                  pl.BlockSpec((tb, DV), fwd(COL_RV, DV)),
                  pl.BlockSpec((tb, DK), lambda b, h, i: (i, 0)),
                  pl.BlockSpec((tb, DK), lambda b, h, i: (i, 0)),
                  pl.BlockSpec((tb, DK), bwd(COL_RQ, DK)),
                  pl.BlockSpec((tb, DK), bwd(COL_RK, DK)),
                  pl.BlockSpec((tb, DV), bwd(COL_RV, DV)),
                  pl.BlockSpec((tb, DK), lambda b, h, i: (nb - 1 - i, 0)),
                  pl.BlockSpec((tb, DK), lambda b, h, i: (nb - 1 - i, 0))]
                 + const_specs + [st_spec, st_spec],
        out_specs=[pl.BlockSpec((tb, DV), lambda b, h, i: (b * nb + i, h)),
                   pl.BlockSpec((tb, DV), lambda b, h, i: (b * nb + nb - 1 - i, h)),
                   st_spec, st_spec],
        out_shape=[o_shape, o_shape, st_shape, st_shape],
        compiler_params=_cparams(("parallel", "parallel", "arbitrary")),
        name="ret",
    )(p, p, p, cos, sin, p, p, p, cos, sin, *const_args, s0f, s0b)


def _rope_tables(seq):
    n = DK // 4
    inv = np.float32(ROPE_BASE) ** (-np.arange(n, dtype=np.float32) / np.float32(n))
    pos = np.arange(seq)
    ar = (pos // GRID_W).astype(np.float32)[:, None] * inv[None, :]
    ac = (pos % GRID_W).astype(np.float32)[:, None] * inv[None, :]
    cos = np.concatenate([np.cos(ar), np.cos(ar), np.cos(ac), np.cos(ac)], axis=1)
    sin = np.concatenate([-np.sin(ar), np.sin(ar), -np.sin(ac), np.sin(ac)], axis=1)
    return jnp.asarray(cos, F32), jnp.asarray(sin, F32)


def _route(logits_t):
    g = [logits_t[i:i + 1] for i in range(N_GROUPS)]
    gmax = jnp.maximum(jnp.maximum(g[0], g[1]), jnp.maximum(g[2], g[3]))
    gsel = jnp.where(g[0] == gmax, 0, jnp.where(g[1] == gmax, 1, jnp.where(g[2] == gmax, 2, 3)))
    gsum = (jnp.exp(g[0] - gmax) + jnp.exp(g[1] - gmax)) + (jnp.exp(g[2] - gmax) + jnp.exp(g[3] - gmax))
    gw = 1.0 / gsum
    e = [logits_t[EXP_ROW0 + EXPERTS_PER_GROUP * i:EXP_ROW0 + EXPERTS_PER_GROUP * (i + 1)]
         for i in range(N_GROUPS)]
    el = jnp.where(gsel == 0, e[0], jnp.where(gsel == 1, e[1], jnp.where(gsel == 2, e[2], e[3])))
    row = lax.broadcasted_iota(jnp.int32, el.shape, 0).astype(F32)
    none = float(EXPERTS_PER_GROUP)
    m1 = jnp.max(el, axis=0, keepdims=True)
    i1 = jnp.min(jnp.where(el == m1, row, none), axis=0, keepdims=True)
    el2 = jnp.where(row == i1, -jnp.inf, el)
    m2 = jnp.max(el2, axis=0, keepdims=True)
    i2 = jnp.min(jnp.where(el2 == m2, row, none), axis=0, keepdims=True)
    r = jnp.exp(m2 - m1)
    w1 = gw / (1.0 + r)
    w2 = gw * r / (1.0 + r)
    base = gsel * EXPERTS_PER_GROUP
    ids = jnp.concatenate([base + i1.astype(jnp.int32), base + i2.astype(jnp.int32)], axis=0)
    return ids, jnp.concatenate([w1, w2], axis=0)


def _store_token_tiled(ref, val):
    m = val.shape[0]
    for j in range(FEAT_TILES):
        ref[pl.ds(j, m, stride=FEAT_TILES), :] = val[:, LANES * j:LANES * (j + 1)]


def _load_token_tiled(ref, m):
    return jnp.concatenate([ref[pl.ds(j, m, stride=FEAT_TILES), :] for j in range(FEAT_TILES)], axis=1)


def _merge_body(gf_ref, gb_ref, rf_ref, rb_ref, gg_ref, rg_ref, mg_ref, mr_ref, x_ref,
                g1_ref, sh2_ref, nfs_ref, gn_ref, wg_ref, wr_ref, wo_ref, wrt_ref, brt_ref, upper_ref,
                h_ref, hn_ref, ids_ref, ew_ref, cnt_ref, wg_s, wr_s, wo_s):
    @pl.when(pl.program_id(0) == 0)
    def _():
        cnt_ref[...] = jnp.zeros_like(cnt_ref)
        wg_s[...] = wg_ref[...]
        wr_s[...] = wr_ref[...]
        wo_s[...] = wo_ref[...]

    def rows_to_logits(r0, n):
        rows = pl.ds(r0, n)
        og = (gf_ref[rows, :] + gb_ref[rows, :]).astype(F32)
        orr = (rf_ref[rows, :] + rb_ref[rows, :]).astype(F32)
        gparts, rparts = [], []
        for hh in range(HEADS):
            seg = og[:, DV * hh:DV * (hh + 1)]
            ms = jnp.mean(seg * seg, axis=-1, keepdims=True)
            gparts.append(seg * lax.rsqrt(ms + NORM_EPS))
            seg = orr[:, DV * hh:DV * (hh + 1)]
            mu = jnp.mean(seg, axis=-1, keepdims=True)
            cen = seg - mu
            var = jnp.mean(cen * cen, axis=-1, keepdims=True)
            rparts.append(cen * lax.rsqrt(var + NORM_EPS))
        o_gla = (jnp.concatenate(gparts, axis=1) * gn_ref[...]).astype(BF16) * _silu(gg_ref[rows, :])
        o_ret = jnp.concatenate(rparts, axis=1).astype(BF16) * _silu(rg_ref[rows, :])
        y = (_sigmoid(mg_ref[rows, :]) * _dot(o_gla, wg_s[...]).astype(BF16)
             + _sigmoid(mr_ref[rows, :]) * _dot(o_ret, wr_s[...]).astype(BF16))
        h = x_ref[rows, :] + g1_ref[0] * _dot(y, wo_s[...])
        h_ref[rows, :] = h
        ms = jnp.mean(h * h, axis=-1, keepdims=True)
        hn = h * lax.rsqrt(ms + NORM_EPS) * nfs_ref[0] + sh2_ref[0]
        _store_token_tiled(hn_ref.at[pl.ds(r0 * FEAT_TILES, n * FEAT_TILES)], hn)
        hn_hi = hn.astype(BF16)
        hn_lo = (hn - hn_hi.astype(F32)).astype(BF16)
        both_w = _dot(hn_hi, wrt_ref[...])
        return (both_w[:, :ROUTER_W] + both_w[:, ROUTER_W:] + _dot(hn_lo, wrt_ref[:, :ROUTER_W])) + brt_ref[...]

    tm = x_ref.shape[0]
    n = tm // MERGE_SPLIT
    logits = jnp.concatenate([rows_to_logits(g * n, n) for g in range(MERGE_SPLIT)], axis=0)
    ids, ew = _route(logits.T)

    erow = lax.broadcasted_iota(jnp.int32, (N_EXPERTS, tm), 0)
    oh0 = jnp.where(erow == ids[0:1], 1.0, 0.0)
    oh1 = jnp.where(erow == ids[1:2], 1.0, 0.0)
    both = oh0 + oh1
    before = _dot(both.astype(BF16), upper_ref[...]) + cnt_ref[:, 0:1].astype(F32)
    rank0 = jnp.sum(oh0 * before, axis=0, keepdims=True)
    rank1 = jnp.sum(oh1 * before, axis=0, keepdims=True)
    total = cnt_ref[:, 0:1] + jnp.sum(both, axis=1, keepdims=True).astype(jnp.int32)
    cnt_ref[...] = jnp.broadcast_to(total, cnt_ref.shape)
    ids_ref[...] = jnp.concatenate([ids, rank0.astype(jnp.int32), rank1.astype(jnp.int32),
                                    jnp.zeros((4, tm), jnp.int32)], axis=0)
    ew_ref[...] = jnp.concatenate([ew, jnp.zeros((6, tm), F32)], axis=0)


def _merge(gf, gb, rf, rb, p, x2, g1, sh2, nfs, gn, wg, wr, wo, wrt, brt, seq, tm):
    t = x2.shape[0]
    tiles_per_batch = seq // tm
    row = lambda i: (i, 0)
    mod = lambda i: (i // tiles_per_batch, 0, 0)
    const = lambda i: (0, 0)
    tok = pl.BlockSpec((tm, D_MODEL), row)
    vec = pl.BlockSpec((1, D_MODEL), const)
    modspec = pl.BlockSpec((1, 1, D_MODEL), mod)
    wspec = pl.BlockSpec((D_MODEL, D_MODEL), const)
    upper = jnp.asarray(np.triu(np.ones((tm, tm), np.float32), 1), BF16)

    def pcol(col0):
        return pl.BlockSpec((tm, D_MODEL), lambda i: (i, col0 // D_MODEL))

    return pl.pallas_call(
        _merge_body,
        grid=(t // tm,),
        in_specs=[tok, tok, tok, tok, pcol(COL_GG), pcol(COL_RG), pcol(COL_MG), pcol(COL_MR), tok,
                  modspec, modspec, modspec, vec, wspec, wspec, wspec,
                  pl.BlockSpec((D_MODEL, 2 * ROUTER_W), const), pl.BlockSpec((1, ROUTER_W), const),
                  pl.BlockSpec((tm, tm), const)],
        out_specs=[tok, pl.BlockSpec((tm * FEAT_TILES, LANES), row),
                   pl.BlockSpec((8, tm), lambda i: (0, i)), pl.BlockSpec((8, tm), lambda i: (0, i)),
                   pl.BlockSpec((N_EXPERTS, LANES), const)],
        out_shape=[jax.ShapeDtypeStruct((t, D_MODEL), F32), jax.ShapeDtypeStruct((t * FEAT_TILES, LANES), F32),
                   jax.ShapeDtypeStruct((8, t), jnp.int32), jax.ShapeDtypeStruct((8, t), F32),
                   jax.ShapeDtypeStruct((N_EXPERTS, LANES), jnp.int32)],
        scratch_shapes=[pltpu.VMEM((D_MODEL, D_MODEL), BF16)] * 3,
        compiler_params=_cparams(("arbitrary",)),
        name="merge",
    )(gf, gb, rf, rb, p, p, p, p, x2, g1, sh2, nfs, gn, wg, wr, wo, wrt, brt, upper)


GATHER_UNROLL = 8


def _aligned(tok):
    off = tok * FEAT_TILES
    return off if isinstance(off, int) else pl.multiple_of(off, FEAT_TILES)


def _token_copy(src_hbm, dst, sem, src_tok, dst_tok):
    return pltpu.make_async_copy(src_hbm.at[pl.ds(_aligned(src_tok), FEAT_TILES), :],
                                 dst.at[pl.ds(_aligned(dst_tok), FEAT_TILES), :], sem)


def _wait_tokens(src_hbm, dst, sem):
    pltpu.make_async_copy(src_hbm.at[pl.ds(0, dst.shape[0]), :], dst, sem).wait()


GATHER_BUFS = 3
GATHER_AHEAD = GATHER_BUFS - 1


def _issue_rows(n, start_row, unrolled):
    if unrolled:
        for r in range(n):
            start_row(r, r % 2)
    else:
        def body(g, carry):
            for u in range(GATHER_UNROLL):
                start_row(g * GATHER_UNROLL + u, u % 2)
            return carry
        lax.fori_loop(0, n // GATHER_UNROLL, body, 0)


def _expert_body(be_ref, src0_ref, stok_ref, nused_ref, hn_hbm, wg_ref, wu_ref, wd_ref, y_ref, *scratch):
    i = pl.program_id(0)
    nused = nused_ref[0]
    bufs, sems = scratch[:GATHER_BUFS], scratch[GATHER_BUFS]
    wg_s, wu_s, wd_s = scratch[GATHER_BUFS + 1:]
    last = stok_ref.shape[0] - 1

    def issue(block, s, unrolled):
        base = src0_ref[block]

        def start_row(r, priority):
            tok = stok_ref[jnp.minimum(base + r, last)]
            _token_copy(hn_hbm, bufs[s], sems.at[s], tok, r).start(priority=priority)
        _issue_rows(MOE_BLOCK, start_row, unrolled)

    @pl.when(i == 0)
    def _():
        for b in range(GATHER_AHEAD):
            issue(b, b, False)

    @pl.when(jnp.logical_and(i < nused, jnp.logical_or(i == 0, be_ref[i] != be_ref[jnp.maximum(i - 1, 0)])))
    def _():
        wg_s[...] = wg_ref[0].astype(BF16)
        wu_s[...] = wu_ref[0].astype(BF16)
        wd_s[...] = wd_ref[0].astype(BF16)

    for cur in range(GATHER_BUFS):
        mine = i % GATHER_BUFS == cur

        @pl.when(jnp.logical_and(mine, i < nused))
        def _():
            _wait_tokens(hn_hbm, bufs[cur], sems.at[cur])
            issue(i + GATHER_AHEAD, (cur + GATHER_AHEAD) % GATHER_BUFS, True)
            xb = _load_token_tiled(bufs[cur], MOE_BLOCK).astype(BF16)
            gate = _dot(xb, wg_s[...])
            up = _dot(xb, wu_s[...])
            hid = (_silu(gate) * up).astype(BF16)
            _store_token_tiled(y_ref, _dot(hid, wd_s[...]))

        @pl.when(jnp.logical_and(mine, jnp.logical_and(i >= nused, i < nused + GATHER_AHEAD)))
        def _():
            _wait_tokens(hn_hbm, bufs[cur], sems.at[cur])

    @pl.when(i >= nused)
    def _():
        y_ref[...] = jnp.zeros_like(y_ref)


def _experts(block_e, src0, stok, nused, hn, w_gate, w_up, w_down, nblk):
    wmap = lambda i, be, s0, st, nu: (be[i], 0, 0)
    rows = MOE_BLOCK * FEAT_TILES
    return pl.pallas_call(
        _expert_body,
        grid_spec=pltpu.PrefetchScalarGridSpec(
            num_scalar_prefetch=4,
            grid=(nblk,),
            in_specs=[pl.BlockSpec(memory_space=pl.ANY),
                      pl.BlockSpec((1, D_MODEL, EXPERT_FF), wmap),
                      pl.BlockSpec((1, D_MODEL, EXPERT_FF), wmap),
                      pl.BlockSpec((1, EXPERT_FF, D_MODEL), wmap)],
            out_specs=pl.BlockSpec((rows, LANES), lambda i, be, s0, st, nu: (i, 0)),
            scratch_shapes=[pltpu.VMEM((rows, LANES), F32)] * GATHER_BUFS
                           + [pltpu.SemaphoreType.DMA((GATHER_BUFS,)),
                              pltpu.VMEM((D_MODEL, EXPERT_FF), BF16), pltpu.VMEM((D_MODEL, EXPERT_FF), BF16),
                              pltpu.VMEM((EXPERT_FF, D_MODEL), BF16)]),
        out_shape=jax.ShapeDtypeStruct((nblk * rows, LANES), F32),
        compiler_params=_cparams(("arbitrary",)),
        name="experts",
    )(block_e, src0, stok, nused, hn, w_gate, w_up, w_down)


def _combine_body(pos_ref, y_hbm, h_ref, ew_ref, g2_ref, nf_ref, o_ref, *scratch):
    i = pl.program_id(0)
    n = pl.num_programs(0)
    tm = COMBINE_TM
    t = pos_ref.shape[0] // TOP_K
    bufs, sems = scratch[:GATHER_BUFS], scratch[GATHER_BUFS]

    def issue(tile, s, unrolled):
        def start_row(j, priority):
            k, r = j % TOP_K, j // TOP_K
            _token_copy(y_hbm, bufs[s].at[k], sems.at[s], pos_ref[k * t + tile * tm + r], r).start(priority=priority)
        _issue_rows(tm * TOP_K, start_row, unrolled)

    def wait(s):
        for k in range(TOP_K):
            _wait_tokens(y_hbm, bufs[s].at[k], sems.at[s])

    @pl.when(i == 0)
    def _():
        for b in range(GATHER_AHEAD):
            issue(b, b, False)

    for cur in range(GATHER_BUFS):
        @pl.when(i % GATHER_BUFS == cur)
        def _():
            wait(cur)
            issue(jnp.minimum(i + GATHER_AHEAD, n - 1), (cur + GATHER_AHEAD) % GATHER_BUFS, True)
            wt = jnp.concatenate([ew_ref[...]] * (LANES // 8), axis=0).T
            moe = (wt[:, 0:1] * _load_token_tiled(bufs[cur].at[0], tm)
                   + wt[:, 1:2] * _load_token_tiled(bufs[cur].at[1], tm))
            h = h_ref[...] + g2_ref[0] * moe
            ms = jnp.mean(h * h, axis=-1, keepdims=True)
            o_ref[...] = h * lax.rsqrt(ms + NORM_EPS) * nf_ref[...]

        @pl.when(jnp.logical_and(i % GATHER_BUFS == cur, i == n - 1))
        def _():
            for ahead in range(1, GATHER_BUFS):
                wait((cur + ahead) % GATHER_BUFS)


def _combine(pos, y_pad, h, ew, g2, nf, seq):
    t = h.shape[0]
    tm = COMBINE_TM
    tiles_per_batch = seq // tm
    ybuf = pltpu.VMEM((TOP_K, tm * FEAT_TILES, LANES), F32)
    return pl.pallas_call(
        _combine_body,
        grid_spec=pltpu.PrefetchScalarGridSpec(
            num_scalar_prefetch=1,
            grid=(t // tm,),
            in_specs=[pl.BlockSpec(memory_space=pl.ANY),
                      pl.BlockSpec((tm, D_MODEL), lambda i, pos: (i, 0)),
                      pl.BlockSpec((8, tm), lambda i, pos: (0, i)),
                      pl.BlockSpec((1, 1, D_MODEL), lambda i, pos: (i // tiles_per_batch, 0, 0)),
                      pl.BlockSpec((1, D_MODEL), lambda i, pos: (0, 0))],
            out_specs=pl.BlockSpec((tm, D_MODEL), lambda i, pos: (i, 0)),
            scratch_shapes=[ybuf] * GATHER_BUFS + [pltpu.SemaphoreType.DMA((GATHER_BUFS,))]),
        out_shape=jax.ShapeDtypeStruct((t, D_MODEL), F32),
        compiler_params=_cparams(("arbitrary",)),
        name="combine",
    )(pos, y_pad, h, ew, g2, nf)


def _dispatch_indices(ids, ranks, counts, t):
    a = t * TOP_K
    nblk = a // MOE_BLOCK + N_EXPERTS + GATHER_AHEAD
    experts = jnp.arange(N_EXPERTS, dtype=jnp.int32)
    starts = jnp.cumsum(counts) - counts
    padded = (counts + MOE_BLOCK - 1) // MOE_BLOCK * MOE_BLOCK
    pends = jnp.cumsum(padded)
    pstarts = pends - padded
    block_start = jnp.arange(nblk, dtype=jnp.int32) * MOE_BLOCK
    block_e = jnp.minimum(jnp.sum((block_start[:, None] >= pends[None, :]).astype(jnp.int32), axis=1),
                          N_EXPERTS - 1)
    of_block = (block_e[:, None] == experts[None, :]).astype(jnp.int32)
    src0 = block_start + jnp.sum(of_block * (starts - pstarts)[None, :], axis=1)
    nused = pends[-1:] // MOE_BLOCK
    tok = jnp.arange(t, dtype=jnp.int32)[None, :]
    slot = jnp.arange(TOP_K, dtype=jnp.int32)[:, None]
    key = ids * a + tok * TOP_K + slot
    stok = (jnp.sort(key.reshape(a)) % a) // TOP_K
    ids_d = ids.reshape(a // LANES, LANES)
    pstart_of = jnp.zeros_like(ids_d)
    for e in range(N_EXPERTS):
        pstart_of = jnp.where(ids_d == e, pstarts[e], pstart_of)
    pos = (ranks.reshape(a // LANES, LANES) + pstart_of).reshape(a)
    i32 = lambda v: v.astype(jnp.int32)
    return i32(block_e), i32(src0), i32(stok), i32(nused), i32(pos), nblk


def kernel(x, c, ctx, c_ctx, w_ada, b_ada, norm_mix, norm_ffn, w_in, gla_lr_w, gla_lr_b, gla_norm,
           w_branch_gla, w_branch_ret, w_out, w_router_group, b_router_group, w_router_expert,
           b_router_expert, w_expert_gate, w_expert_up, w_expert_down, norm_final):
    batch, seq, d = x.shape
    ctx_len = ctx.shape[1]
    assert d == D_MODEL and w_ada.shape[0] == 1, "single-layer block with D_MODEL features"
    t = batch * seq

    c8 = jnp.zeros((8, d), F32).at[:batch].set(c).at[batch].set(c_ctx)
    mod = _ada(c8, w_ada[0], b_ada[0][None, :])
    sh1, sc1, g1, sh2, sc2, g2 = [mod[:, d * i:d * (i + 1)] for i in range(6)]
    lat = lambda m: m[:batch, None, :]
    cx = lambda m: m[batch:batch + 1, None, :]

    w_main, w_lr = _wprep(jnp.swapaxes(w_in[0], 0, 1))
    nm = norm_mix[0][None, :]

    p_ctx, lr_ctx = _proj(ctx.reshape(batch * ctx_len, d), cx(sh1), cx(sc1), nm, w_main, w_lr,
                          tm=ctx_len, rows_per_mod=batch * ctx_len)
    x2 = x.reshape(t, d)
    p_lat, lr_lat = _proj(x2, lat(sh1), lat(sc1), nm, w_main, w_lr, tm=PROJ_TM, rows_per_mod=seq)

    wlr_f = jnp.zeros((LANES, HEADS * DK), F32).at[:GLA_RANK].set(gla_lr_w[0, 0])
    wlr_b = jnp.zeros((LANES, HEADS * DK), F32).at[GLA_RANK:2 * GLA_RANK].set(gla_lr_w[0, 1])
    blr_f, blr_b = gla_lr_b[0, 0][None, :], gla_lr_b[0, 1][None, :]
    zero_state = jnp.zeros((batch, HEADS, DK, DV), F32)

    _, _, gs_f, gs_b, _ = _gla(p_ctx, lr_ctx, wlr_f, blr_f, wlr_b, blr_b, zero_state, zero_state,
                               batch, ctx_len, tb=ctx_len)
    gla_args = (p_lat, lr_lat, wlr_f, blr_f, wlr_b, blr_b, gs_f, gs_b, batch, seq)
    gla_f, gla_b, _, _, decay_range = _gla(*gla_args, tb=SCAN_TB)
    gla_f, gla_b = lax.cond(jnp.max(decay_range) > GLA_SAFE_RANGE,
                            lambda: tuple(_gla(*gla_args, tb=GLA_EXACT_TB, exact=True)[:2]),
                            lambda: (gla_f, gla_b))

    ones = jnp.ones((ctx_len, DK), F32)
    _, _, rs_f, rs_b = _ret(p_ctx, ones, jnp.zeros_like(ones), zero_state, zero_state, batch, ctx_len, tb=ctx_len)
    cos, sin = _rope_tables(seq)
    ret_f, ret_b, _, _ = _ret(p_lat, cos, sin, rs_f, rs_b, batch, seq, tb=SCAN_TB)

    wrt = jnp.zeros((d, ROUTER_W), F32)
    wrt = wrt.at[:, :N_GROUPS].set(w_router_group[0]).at[:, EXP_ROW0:EXP_ROW0 + N_EXPERTS].set(w_router_expert[0])
    wrt_hi = wrt.astype(BF16)
    wrt = jnp.concatenate([wrt_hi, (wrt - wrt_hi.astype(F32)).astype(BF16)], axis=1)
    brt = jnp.zeros((1, ROUTER_W), F32)
    brt = brt.at[0, :N_GROUPS].set(b_router_group[0]).at[0, EXP_ROW0:EXP_ROW0 + N_EXPERTS].set(b_router_expert[0])
    h, hn, ids8, ew8, cnt = _merge(gla_f, gla_b, ret_f, ret_b, p_lat, x2, lat(g1), lat(sh2),
                                   norm_ffn[0][None, None, :] * (1.0 + lat(sc2)),
                                   jnp.tile(gla_norm[0], HEADS)[None, :],
                                   w_branch_gla[0].astype(BF16), w_branch_ret[0].astype(BF16),
                                   w_out[0].astype(BF16), wrt, brt, seq, tm=MERGE_TM)

    block_e, src0, stok, nused, pos, nblk = _dispatch_indices(ids8[:TOP_K], ids8[TOP_K:2 * TOP_K], cnt[:, 0], t)
    y_pad = _experts(block_e, src0, stok, nused, hn, w_expert_gate[0], w_expert_up[0], w_expert_down[0], nblk)
    out = _combine(pos, y_pad, h, ew8, lat(g2), norm_final[None, :], seq)
    return out.reshape(batch, seq, d)
```

```python
import functools

import numpy as np
import jax
import jax.numpy as jnp
from jax import lax
from jax.experimental import pallas as pl
from jax.experimental.pallas import tpu as pltpu

F32 = jnp.float32
BF16 = jnp.bfloat16

D_MODEL = 1024
GRID_W = 64
HEADS = 4
DK = 128
DV = 256
GLA_RANK = 16
GLA_TAU = 16.0
GLA_CHUNK = 64
GLA_SUB = 16
RET_CHUNK = 128
ROPE_BASE = 10000.0
N_GROUPS = 4
EXPERTS_PER_GROUP = 8
N_EXPERTS = N_GROUPS * EXPERTS_PER_GROUP
TOP_K = 2
EXPERT_FF = 256
MOE_BLOCK = 128
NORM_EPS = 1e-6

COL_GQ, COL_GK, COL_GV, COL_GG = 0, 512, 1024, 2048
COL_RQ, COL_RK, COL_RV, COL_RG = 3072, 3584, 4096, 5120
COL_MG, COL_MR = 6144, 7168
PROJ_W = 8192
LANES = 128
FEAT_TILES = D_MODEL // LANES
ROUTER_W = 128
EXP_ROW0 = 8

VMEM_LIMIT = 56 * 1024 * 1024

PROJ_TM = 1024
PROJ_TN = 2048
SCAN_TB = 1024
GLA_PHASE_LAG = 2
RET_PHASE_LAG = 0
GLA_EXACT_TB = 256
GLA_SAFE_RANGE = 60.0
MERGE_TM = 256
MERGE_SPLIT = 1
COMBINE_TM = 256


def _cparams(sem):
    return pltpu.CompilerParams(dimension_semantics=sem, vmem_limit_bytes=VMEM_LIMIT)


def _sigmoid(x):
    return 1.0 / (1.0 + jnp.exp(-x))


def _silu(x):
    return x * _sigmoid(x)


def _dot(a, b):
    return jnp.dot(a, b, preferred_element_type=F32)


def _dot_nt(a, b):
    return lax.dot_general(a, b, (((1,), (1,)), ((), ())), preferred_element_type=F32)


def _ada_body(c_ref, w_ref, b_ref, o_ref):
    s = _silu(c_ref[...])
    o_ref[...] = _dot(s.astype(BF16), w_ref[...].astype(BF16)) + b_ref[...]


def _ada(c8, w, b):
    n = w.shape[1]
    tn = 1536
    return pl.pallas_call(
        _ada_body,
        grid=(n // tn,),
        in_specs=[pl.BlockSpec((8, D_MODEL), lambda j: (0, 0)),
                  pl.BlockSpec((D_MODEL, tn), lambda j: (0, j)),
                  pl.BlockSpec((1, tn), lambda j: (0, j))],
        out_specs=pl.BlockSpec((8, tn), lambda j: (0, j)),
        out_shape=jax.ShapeDtypeStruct((8, n), F32),
        compiler_params=_cparams(("arbitrary",)),
        name="ada",
    )(c8, w, b)


LR_COL0 = COL_GG + HEADS * DV
WPREP_ROWS = 1024


def _wprep_body(w_ref, wlr_ref, main_ref, lr_ref):
    main_ref[...] = w_ref[...].astype(BF16)
    pad = jnp.zeros((LANES - 2 * GLA_RANK, D_MODEL), F32)
    lr_ref[...] = jnp.concatenate([wlr_ref[...], pad], axis=0).astype(BF16)


def _wprep(wt):
    steps_before = LR_COL0 // WPREP_ROWS

    def src_row(i):
        row = jnp.where(i < steps_before, i * WPREP_ROWS, i * WPREP_ROWS + 2 * GLA_RANK)
        return (pl.multiple_of(row, 2 * GLA_RANK), 0)

    return pl.pallas_call(
        _wprep_body,
        grid=(PROJ_W // WPREP_ROWS,),
        in_specs=[pl.BlockSpec((pl.Element(WPREP_ROWS), pl.Element(D_MODEL)), src_row),
                  pl.BlockSpec((pl.Element(2 * GLA_RANK), pl.Element(D_MODEL)), lambda i: (LR_COL0, 0))],
        out_specs=[pl.BlockSpec((WPREP_ROWS, D_MODEL), lambda i: (i, 0)),
                   pl.BlockSpec((LANES, D_MODEL), lambda i: (0, 0))],
        out_shape=[jax.ShapeDtypeStruct((PROJ_W, D_MODEL), BF16), jax.ShapeDtypeStruct((LANES, D_MODEL), BF16)],
        compiler_params=_cparams(("arbitrary",)),
        name="wprep",
    )(wt, wt)


def _proj_body(x_ref, sh_ref, sc_ref, g_ref, w_ref, wlr_ref, o_ref, lr_ref, h_ref):
    @pl.when(pl.program_id(1) == 0)
    def _():
        x = x_ref[...]
        ms = jnp.mean(x * x, axis=-1, keepdims=True)
        y = x * lax.rsqrt(ms + NORM_EPS) * g_ref[...]
        hb = (y * (1.0 + sc_ref[0]) + sh_ref[0]).astype(BF16)
        h_ref[...] = hb
        lr_ref[...] = _dot_nt(hb, wlr_ref[...])

    o_ref[...] = _dot_nt(h_ref[...], w_ref[...]).astype(BF16)


def _proj(x2, sh, sc, g, w_main, w_lr, tm, rows_per_mod):
    t = x2.shape[0]
    tn = PROJ_TN
    tiles_per_mod = rows_per_mod // tm
    mod_map = lambda i, j: (i // tiles_per_mod, 0, 0)
    return pl.pallas_call(
        _proj_body,
        grid=(t // tm, PROJ_W // tn),
        in_specs=[pl.BlockSpec((tm, D_MODEL), lambda i, j: (i, 0)),
                  pl.BlockSpec((1, 1, D_MODEL), mod_map),
                  pl.BlockSpec((1, 1, D_MODEL), mod_map),
                  pl.BlockSpec((1, D_MODEL), lambda i, j: (0, 0)),
                  pl.BlockSpec((tn, D_MODEL), lambda i, j: (j, 0)),
                  pl.BlockSpec((LANES, D_MODEL), lambda i, j: (0, 0))],
        out_specs=[pl.BlockSpec((tm, tn), lambda i, j: (i, j)),
                   pl.BlockSpec((tm, LANES), lambda i, j: (i, 0))],
        out_shape=[jax.ShapeDtypeStruct((t, PROJ_W), BF16),
                   jax.ShapeDtypeStruct((t, LANES), F32)],
        scratch_shapes=[pltpu.VMEM((tm, D_MODEL), BF16)],
        compiler_params=_cparams(("parallel", "arbitrary")),
        name="proj",
    )(x2, sh, sc, g, w_main, w_lr)


def _gla_consts():
    c, s = GLA_CHUNK, GLA_SUB
    msk = {}
    for rev in (False, True):
        cols = []
        for i in range(c // s):
            keys = np.arange(c)[_key_rows(i, rev)][None, :]
            t = np.arange(c)[:, None]
            visible = (keys > t) if rev else (keys <= t)
            cols.append(np.where(t // s == i, visible, False))
        msk[rev] = np.concatenate(cols, axis=1).astype(np.float32)
    return msk


def _key_rows(i, rev):
    return slice(GLA_SUB * i, GLA_CHUNK) if rev else slice(0, GLA_SUB * (i + 1))


def _log_gate(lr, w, b):
    z = _dot(lr.astype(BF16), w.astype(BF16)) + b
    return -(jnp.maximum(-z, 0.0) + jnp.log(1.0 + jnp.exp(-jnp.abs(z)))) * (1.0 / GLA_TAU)


def _subchunk_scan(la, rev):
    c = la.shape[0]
    pos = lax.broadcasted_iota(jnp.int32, la.shape, 0) % GLA_SUB
    w = la
    step = 1
    while step < GLA_SUB:
        if rev:
            w = w + jnp.where(pos < GLA_SUB - step, pltpu.roll(w, c - step, 0), 0.0)
        else:
            w = w + jnp.where(pos >= step, pltpu.roll(w, step, 0), 0.0)
        step *= 2
    return w


def _gla_intra_exact(q, kf, v, b, rev, scratch):
    q_s, b_s, o_s = scratch
    c = GLA_CHUNK
    q_s[...] = q.astype(F32) * (DK ** -0.5)
    b_s[...] = b
    vf = v.astype(F32)
    row = lax.broadcasted_iota(jnp.int32, (c, 1), 0)

    def body(t, carry):
        qt = q_s[pl.ds(t, 1), :]
        bt = b_s[pl.ds(t, 1), :]
        visible = (row > t) if rev else (row <= t)
        decay = jnp.exp(jnp.where(visible, bt - b, -jnp.inf))
        col = jnp.sum(kf * decay * qt, axis=1, keepdims=True)
        o_s[pl.ds(t, 1), :] = jnp.sum(col * vf, axis=0, keepdims=True)
        return carry

    lax.fori_loop(0, c, body, 0)
    return o_s[...]


def _gla_chunk(q, k, v, la, state_ref, mask, rev, exact_scratch=None):
    c, s = GLA_CHUNK, GLA_SUB
    nsub = c // s
    w = _subchunk_scan(la, rev)
    tot = [w[s * i:s * i + 1] if rev else w[s * (i + 1) - 1:s * (i + 1)] for i in range(nsub)]
    anchors = [None] * nsub
    acc = jnp.zeros_like(tot[0])
    worst = jnp.zeros_like(tot[0])
    for i in (reversed(range(nsub)) if rev else range(nsub)):
        anchors[i] = acc
        acc = acc + tot[i]
        worst = jnp.maximum(worst, -tot[i])
    b_end = acc
    ref = jnp.concatenate([jnp.broadcast_to(a, (s, DK)) for a in anchors], axis=0)
    b = w + ref

    qt = q.astype(F32) * (DK ** -0.5) * jnp.exp(w)
    kf = k.astype(F32)
    if exact_scratch is None:
        kstack = jnp.concatenate([kf[_key_rows(i, rev)] * jnp.exp(anchors[i] - b[_key_rows(i, rev)])
                                  for i in range(nsub)], axis=0).astype(BF16)
        raw = _dot_nt(qt.astype(BF16), kstack)
    kst = kf * jnp.exp(b_end - b)
    xt = jnp.concatenate([kst, jnp.broadcast_to(jnp.exp(b_end), (c, DK))], axis=0).T
    update = _dot(xt[:, :c].astype(BF16), v)
    state = state_ref[...]
    inter = _dot((qt * jnp.exp(ref)).astype(BF16), state.astype(BF16))
    state_ref[...] = xt[:, c:c + 1] * state + update
    yield
    if exact_scratch is None:
        vstack = jnp.concatenate([v[_key_rows(i, rev)] for i in range(nsub)], axis=0)
        intra = _dot(jnp.where(mask != 0.0, raw, 0.0).astype(BF16), vstack)
    else:
        intra = _gla_intra_exact(q, kf, v, b, rev, exact_scratch)
    return inter + intra, worst


def _run_phased(chunks, emit, lag):
    def finish(tag, gen):
        try:
            next(gen)
        except StopIteration as done:
            emit(tag, done.value)
            return
        raise AssertionError("chunk generator has more than two phases")

    pending = []
    for tag, gen in chunks:
        next(gen)
        pending.append((tag, gen))
        if len(pending) > lag:
            finish(*pending.pop(0))
    for item in pending:
        finish(*item)


def _gla_body(qf_ref, kf_ref, vf_ref, lrf_ref, qb_ref, kb_ref, vb_ref, lrb_ref,
              wf_ref, bf_ref, wb_ref, bb_ref, mskf_ref, mskb_ref,
              s0f_ref, s0b_ref, of_ref, ob_ref, sf_ref, sb_ref, rng_ref, *exact_scratch, nchunks):
    @pl.when(pl.program_id(2) == 0)
    def _():
        sf_ref[...] = s0f_ref[...]
        sb_ref[...] = s0b_ref[...]

    c = GLA_CHUNK
    scratch = exact_scratch or None
    la_f = _log_gate(lrf_ref[...], wf_ref[...], bf_ref[...])
    la_b = _log_gate(lrb_ref[...], wb_ref[...], bb_ref[...])
    mskf, mskb = mskf_ref[...], mskb_ref[...]
    st_f, st_b = sf_ref.at[0, 0], sb_ref.at[0, 0]
    worst = [jnp.zeros((1, DK), F32)]

    def chunks():
        for n in range(nchunks):
            rf = slice(c * n, c * (n + 1))
            yield (of_ref, rf), _gla_chunk(qf_ref[rf, :], kf_ref[rf, :], vf_ref[rf, :], la_f[rf, :], st_f, mskf,
                                           False, scratch)
            m = nchunks - 1 - n
            rb = slice(c * m, c * (m + 1))
            yield (ob_ref, rb), _gla_chunk(qb_ref[rb, :], kb_ref[rb, :], vb_ref[rb, :], la_b[rb, :], st_b, mskb,
                                           True, scratch)

    def emit(tag, result):
        o_ref, rows = tag
        o_ref[rows, :] = result[0].astype(BF16)
        worst[0] = jnp.maximum(worst[0], result[1])

    _run_phased(chunks(), emit, GLA_PHASE_LAG)
    rng_ref[0, 0, 0] = jnp.broadcast_to(worst[0], (8, DK))


def _gla(p, lr, wlr_f, blr_f, wlr_b, blr_b, s0f, s0b, batch, seq, tb, exact=False):
    nb = seq // tb
    msk = _gla_consts()
    mskf, mskb = jnp.asarray(msk[False]), jnp.asarray(msk[True])

    def fwd(col0, width):
        return lambda b, h, i: (b * nb + i, col0 // width + h)

    def bwd(col0, width):
        return lambda b, h, i: (b * nb + nb - 1 - i, col0 // width + h)

    const2 = lambda b, h, i: (0, 0)
    headcol = lambda b, h, i: (0, h)
    st_map = lambda b, h, i: (b, h, 0, 0)
    st_spec = pl.BlockSpec((1, 1, DK, DV), st_map)
    st_shape = jax.ShapeDtypeStruct((batch, HEADS, DK, DV), F32)
    o_shape = jax.ShapeDtypeStruct((batch * seq, HEADS * DV), BF16)
    return pl.pallas_call(
        functools.partial(_gla_body, nchunks=tb // GLA_CHUNK),
        grid=(batch, HEADS, nb),
        in_specs=[pl.BlockSpec((tb, DK), fwd(COL_GQ, DK)),
                  pl.BlockSpec((tb, DK), fwd(COL_GK, DK)),
                  pl.BlockSpec((tb, DV), fwd(COL_GV, DV)),
                  pl.BlockSpec((tb, LANES), lambda b, h, i: (b * nb + i, 0)),
                  pl.BlockSpec((tb, DK), bwd(COL_GQ, DK)),
                  pl.BlockSpec((tb, DK), bwd(COL_GK, DK)),
                  pl.BlockSpec((tb, DV), bwd(COL_GV, DV)),
                  pl.BlockSpec((tb, LANES), lambda b, h, i: (b * nb + nb - 1 - i, 0)),
                  pl.BlockSpec((LANES, DK), headcol),
                  pl.BlockSpec((1, DK), headcol),
                  pl.BlockSpec((LANES, DK), headcol),
                  pl.BlockSpec((1, DK), headcol),
                  pl.BlockSpec(mskf.shape, const2),
                  pl.BlockSpec(mskb.shape, const2),
                  st_spec, st_spec],
        out_specs=[pl.BlockSpec((tb, DV), lambda b, h, i: (b * nb + i, h)),
                   pl.BlockSpec((tb, DV), lambda b, h, i: (b * nb + nb - 1 - i, h)),
                   st_spec, st_spec,
                   pl.BlockSpec((1, 1, 1, 8, DK), lambda b, h, i: (b, h, i, 0, 0))],
        out_shape=[o_shape, o_shape, st_shape, st_shape,
                   jax.ShapeDtypeStruct((batch, HEADS, nb, 8, DK), F32)],
        scratch_shapes=([pltpu.VMEM((GLA_CHUNK, DK), F32), pltpu.VMEM((GLA_CHUNK, DK), F32),
                         pltpu.VMEM((GLA_CHUNK, DV), F32)] if exact else []),
        compiler_params=_cparams(("parallel", "parallel", "arbitrary")),
        name="gla_exact" if exact else "gla",
    )(p, p, p, lr, p, p, p, lr, wlr_f, blr_f, wlr_b, blr_b, mskf, mskb, s0f, s0b)


def _ret_consts():
    c = RET_CHUNK
    hh = np.arange(HEADS, dtype=np.float64)
    lg = {False: np.log1p(-np.exp2(-5.0 - hh)), True: np.log1p(-np.exp2(-5.5 - hh))}
    t = np.arange(c, dtype=np.float64)[:, None]
    u = np.arange(c, dtype=np.float64)[None, :]
    pos = np.arange(c, dtype=np.float64)
    out = {}
    for rev in (False, True):
        g = lg[rev][:, None, None]
        if not rev:
            dmat = np.where(u <= t, np.exp((t - u) * g), 0.0)
            qd = np.exp((pos + 1.0)[None, :] * lg[rev][:, None])
            kd = np.exp((c - 1.0 - pos)[None, :] * lg[rev][:, None])
        else:
            dmat = np.where(u > t, np.exp((u - t) * g), 0.0)
            qd = np.exp((c - pos)[None, :] * lg[rev][:, None])
            kd = np.exp(pos[None, :] * lg[rev][:, None])
        cd = np.exp(c * lg[rev])
        out[rev] = (dmat.astype(np.float32),
                    np.repeat(qd[:, :, None], DV, axis=2).astype(np.float32),
                    np.repeat(kd[:, :, None], DK, axis=2).astype(np.float32),
                    np.repeat(cd[:, None, None], DV, axis=2).astype(np.float32))
    return out


def _rope(x, cos, sin):
    lane = lax.broadcasted_iota(jnp.int32, x.shape, 1)
    partner = jnp.where((lane % 64) < 32, pltpu.roll(x, DK - 32, 1), pltpu.roll(x, 32, 1))
    return x * cos + partner * sin


def _ret_chunk(q, k, v, cos, sin, state_ref, dmat, qd, kd, cd):
    qr = _rope(q.astype(F32), cos, sin).astype(BF16)
    kr = _rope(k.astype(F32) * (DK ** -0.5), cos, sin)
    raw = _dot_nt(qr, kr.astype(BF16))
    update = _dot((kr * kd).T.astype(BF16), v)
    state = state_ref[...]
    inter = _dot(qr, state.astype(BF16)) * qd
    state_ref[...] = cd * state + update
    yield
    return inter + _dot((raw * dmat).astype(BF16), v)


def _ret_body(qf_ref, kf_ref, vf_ref, cosf_ref, sinf_ref, qb_ref, kb_ref, vb_ref, cosb_ref, sinb_ref,
              dmf_ref, qdf_ref, kdf_ref, cdf_ref, dmb_ref, qdb_ref, kdb_ref, cdb_ref,
              s0f_ref, s0b_ref, of_ref, ob_ref, sf_ref, sb_ref, *, nchunks):
    @pl.when(pl.program_id(2) == 0)
    def _():
        sf_ref[...] = s0f_ref[...]
        sb_ref[...] = s0b_ref[...]

    c = RET_CHUNK
    cf = (dmf_ref[0], qdf_ref[0], kdf_ref[0], cdf_ref[0])
    cb = (dmb_ref[0], qdb_ref[0], kdb_ref[0], cdb_ref[0])
    st_f, st_b = sf_ref.at[0, 0], sb_ref.at[0, 0]
    def chunks():
        for n in range(nchunks):
            rf = slice(c * n, c * (n + 1))
            yield (of_ref, rf), _ret_chunk(qf_ref[rf, :], kf_ref[rf, :], vf_ref[rf, :], cosf_ref[rf, :],
                                           sinf_ref[rf, :], st_f, *cf)
            m = nchunks - 1 - n
            rb = slice(c * m, c * (m + 1))
            yield (ob_ref, rb), _ret_chunk(qb_ref[rb, :], kb_ref[rb, :], vb_ref[rb, :], cosb_ref[rb, :],
                                           sinb_ref[rb, :], st_b, *cb)

    def emit(tag, out):
        o_ref, rows = tag
        o_ref[rows, :] = out.astype(BF16)

    _run_phased(chunks(), emit, RET_PHASE_LAG)


def _ret(p, cos, sin, s0f, s0b, batch, seq, tb):
    nb = seq // tb
    consts = _ret_consts()

    def fwd(col0, width):
        return lambda b, h, i: (b * nb + i, col0 // width + h)

    def bwd(col0, width):
        return lambda b, h, i: (b * nb + nb - 1 - i, col0 // width + h)

    head3 = lambda b, h, i: (h, 0, 0)
    st_spec = pl.BlockSpec((1, 1, DK, DV), lambda b, h, i: (b, h, 0, 0))
    st_shape = jax.ShapeDtypeStruct((batch, HEADS, DK, DV), F32)
    o_shape = jax.ShapeDtypeStruct((batch * seq, HEADS * DV), BF16)
    const_specs = []
    const_args = []
    for rev in (False, True):
        for a in consts[rev]:
            const_specs.append(pl.BlockSpec((1,) + a.shape[1:], head3))
            const_args.append(jnp.asarray(a))
    return pl.pallas_call(
        functools.partial(_ret_body, nchunks=tb // RET_CHUNK),
        grid=(batch, HEADS, nb),
        in_specs=[pl.BlockSpec((tb, DK), fwd(COL_RQ, DK)),
                  pl.BlockSpec((tb, DK), fwd(COL_RK, DK)),
                  pl.BlockSpec((tb, DV), fwd(COL_RV, DV)),
                  pl.BlockSpec((tb, DK), lambda b, h, i: (i, 0)),
                  pl.BlockSpec((tb, DK), lambda b, h, i: (i, 0)),
                  pl.BlockSpec((tb, DK), bwd(COL_RQ, DK)),
                  pl.BlockSpec((tb, DK), bwd(COL_RK, DK)),
                  pl.BlockSpec((tb, DV), bwd(COL_RV, DV)),
                  pl.BlockSpec((tb, DK), lambda b, h, i: (nb - 1 - i, 0)),
                  pl.BlockSpec((tb, DK), lambda b, h, i: (nb - 1 - i, 0))]
                 + const_specs + [st_spec, st_spec],
        out_specs=[pl.BlockSpec((tb, DV), lambda b, h, i: (b * nb + i, h)),
                   pl.BlockSpec((tb, DV), lambda b, h, i: (b * nb + nb - 1 - i, h)),
                   st_spec, st_spec],
        out_shape=[o_shape, o_shape, st_shape, st_shape],
        compiler_params=_cparams(("parallel", "parallel", "arbitrary")),
        name="ret",
    )(p, p, p, cos, sin, p, p, p, cos, sin, *const_args, s0f, s0b)


def _rope_tables(seq):
    n = DK // 4
    inv = np.float32(ROPE_BASE) ** (-np.arange(n, dtype=np.float32) / np.float32(n))
    pos = np.arange(seq)
    ar = (pos // GRID_W).astype(np.float32)[:, None] * inv[None, :]
    ac = (pos % GRID_W).astype(np.float32)[:, None] * inv[None, :]
    cos = np.concatenate([np.cos(ar), np.cos(ar), np.cos(ac), np.cos(ac)], axis=1)
    sin = np.concatenate([-np.sin(ar), np.sin(ar), -np.sin(ac), np.sin(ac)], axis=1)
    return jnp.asarray(cos, F32), jnp.asarray(sin, F32)


def _route(logits_t):
    g = [logits_t[i:i + 1] for i in range(N_GROUPS)]
    gmax = jnp.maximum(jnp.maximum(g[0], g[1]), jnp.maximum(g[2], g[3]))
    gsel = jnp.where(g[0] == gmax, 0, jnp.where(g[1] == gmax, 1, jnp.where(g[2] == gmax, 2, 3)))
    gsum = (jnp.exp(g[0] - gmax) + jnp.exp(g[1] - gmax)) + (jnp.exp(g[2] - gmax) + jnp.exp(g[3] - gmax))
    gw = 1.0 / gsum
    e = [logits_t[EXP_ROW0 + EXPERTS_PER_GROUP * i:EXP_ROW0 + EXPERTS_PER_GROUP * (i + 1)]
         for i in range(N_GROUPS)]
    el = jnp.where(gsel == 0, e[0], jnp.where(gsel == 1, e[1], jnp.where(gsel == 2, e[2], e[3])))
    row = lax.broadcasted_iota(jnp.int32, el.shape, 0).astype(F32)
    none = float(EXPERTS_PER_GROUP)
    m1 = jnp.max(el, axis=0, keepdims=True)
    i1 = jnp.min(jnp.where(el == m1, row, none), axis=0, keepdims=True)
    el2 = jnp.where(row == i1, -jnp.inf, el)
    m2 = jnp.max(el2, axis=0, keepdims=True)
    i2 = jnp.min(jnp.where(el2 == m2, row, none), axis=0, keepdims=True)
    r = jnp.exp(m2 - m1)
    w1 = gw / (1.0 + r)
    w2 = gw * r / (1.0 + r)
    base = gsel * EXPERTS_PER_GROUP
    ids = jnp.concatenate([base + i1.astype(jnp.int32), base + i2.astype(jnp.int32)], axis=0)
    return ids, jnp.concatenate([w1, w2], axis=0)


def _store_token_tiled(ref, val):
    m = val.shape[0]
    for j in range(FEAT_TILES):
        ref[pl.ds(j, m, stride=FEAT_TILES), :] = val[:, LANES * j:LANES * (j + 1)]


def _load_token_tiled(ref, m):
    return jnp.concatenate([ref[pl.ds(j, m, stride=FEAT_TILES), :] for j in range(FEAT_TILES)], axis=1)


def _merge_body(gf_ref, gb_ref, rf_ref, rb_ref, gg_ref, rg_ref, mg_ref, mr_ref, x_ref,
                g1_ref, sh2_ref, nfs_ref, gn_ref, wg_ref, wr_ref, wo_ref, wrt_ref, brt_ref, upper_ref,
                h_ref, hn_ref, ids_ref, ew_ref, cnt_ref, wg_s, wr_s, wo_s):
    @pl.when(pl.program_id(0) == 0)
    def _():
        cnt_ref[...] = jnp.zeros_like(cnt_ref)
        wg_s[...] = wg_ref[...]
        wr_s[...] = wr_ref[...]
        wo_s[...] = wo_ref[...]

    def rows_to_logits(r0, n):
        rows = pl.ds(r0, n)
        og = (gf_ref[rows, :] + gb_ref[rows, :]).astype(F32)
        orr = (rf_ref[rows, :] + rb_ref[rows, :]).astype(F32)
        gparts, rparts = [], []
        for hh in range(HEADS):
            seg = og[:, DV * hh:DV * (hh + 1)]
            ms = jnp.mean(seg * seg, axis=-1, keepdims=True)
            gparts.append(seg * lax.rsqrt(ms + NORM_EPS))
            seg = orr[:, DV * hh:DV * (hh + 1)]
            mu = jnp.mean(seg, axis=-1, keepdims=True)
            cen = seg - mu
            var = jnp.mean(cen * cen, axis=-1, keepdims=True)
            rparts.append(cen * lax.rsqrt(var + NORM_EPS))
        o_gla = (jnp.concatenate(gparts, axis=1) * gn_ref[...]).astype(BF16) * _silu(gg_ref[rows, :])
        o_ret = jnp.concatenate(rparts, axis=1).astype(BF16) * _silu(rg_ref[rows, :])
        y = (_sigmoid(mg_ref[rows, :]) * _dot(o_gla, wg_s[...]).astype(BF16)
             + _sigmoid(mr_ref[rows, :]) * _dot(o_ret, wr_s[...]).astype(BF16))
        h = x_ref[rows, :] + g1_ref[0] * _dot(y, wo_s[...])
        h_ref[rows, :] = h
        ms = jnp.mean(h * h, axis=-1, keepdims=True)
        hn = h * lax.rsqrt(ms + NORM_EPS) * nfs_ref[0] + sh2_ref[0]
        _store_token_tiled(hn_ref.at[pl.ds(r0 * FEAT_TILES, n * FEAT_TILES)], hn)
        hn_hi = hn.astype(BF16)
        hn_lo = (hn - hn_hi.astype(F32)).astype(BF16)
        both_w = _dot(hn_hi, wrt_ref[...])
        return (both_w[:, :ROUTER_W] + both_w[:, ROUTER_W:] + _dot(hn_lo, wrt_ref[:, :ROUTER_W])) + brt_ref[...]

    tm = x_ref.shape[0]
    n = tm // MERGE_SPLIT
    logits = jnp.concatenate([rows_to_logits(g * n, n) for g in range(MERGE_SPLIT)], axis=0)
    ids, ew = _route(logits.T)

    erow = lax.broadcasted_iota(jnp.int32, (N_EXPERTS, tm), 0)
    oh0 = jnp.where(erow == ids[0:1], 1.0, 0.0)
    oh1 = jnp.where(erow == ids[1:2], 1.0, 0.0)
    both = oh0 + oh1
    before = _dot(both.astype(BF16), upper_ref[...]) + cnt_ref[:, 0:1].astype(F32)
    rank0 = jnp.sum(oh0 * before, axis=0, keepdims=True)
    rank1 = jnp.sum(oh1 * before, axis=0, keepdims=True)
    total = cnt_ref[:, 0:1] + jnp.sum(both, axis=1, keepdims=True).astype(jnp.int32)
    cnt_ref[...] = jnp.broadcast_to(total, cnt_ref.shape)
    ids_ref[...] = jnp.concatenate([ids, rank0.astype(jnp.int32), rank1.astype(jnp.int32),
                                    jnp.zeros((4, tm), jnp.int32)], axis=0)
    ew_ref[...] = jnp.concatenate([ew, jnp.zeros((6, tm), F32)], axis=0)


def _merge(gf, gb, rf, rb, p, x2, g1, sh2, nfs, gn, wg, wr, wo, wrt, brt, seq, tm):
    t = x2.shape[0]
    tiles_per_batch = seq // tm
    row = lambda i: (i, 0)
    mod = lambda i: (i // tiles_per_batch, 0, 0)
    const = lambda i: (0, 0)
    tok = pl.BlockSpec((tm, D_MODEL), row)
    vec = pl.BlockSpec((1, D_MODEL), const)
    modspec = pl.BlockSpec((1, 1, D_MODEL), mod)
    wspec = pl.BlockSpec((D_MODEL, D_MODEL), const)
    upper = jnp.asarray(np.triu(np.ones((tm, tm), np.float32), 1), BF16)

    def pcol(col0):
        return pl.BlockSpec((tm, D_MODEL), lambda i: (i, col0 // D_MODEL))

    return pl.pallas_call(
        _merge_body,
        grid=(t // tm,),
        in_specs=[tok, tok, tok, tok, pcol(COL_GG), pcol(COL_RG), pcol(COL_MG), pcol(COL_MR), tok,
                  modspec, modspec, modspec, vec, wspec, wspec, wspec,
                  pl.BlockSpec((D_MODEL, 2 * ROUTER_W), const), pl.BlockSpec((1, ROUTER_W), const),
                  pl.BlockSpec((tm, tm), const)],
        out_specs=[tok, pl.BlockSpec((tm * FEAT_TILES, LANES), row),
                   pl.BlockSpec((8, tm), lambda i: (0, i)), pl.BlockSpec((8, tm), lambda i: (0, i)),
                   pl.BlockSpec((N_EXPERTS, LANES), const)],
        out_shape=[jax.ShapeDtypeStruct((t, D_MODEL), F32), jax.ShapeDtypeStruct((t * FEAT_TILES, LANES), F32),
                   jax.ShapeDtypeStruct((8, t), jnp.int32), jax.ShapeDtypeStruct((8, t), F32),
                   jax.ShapeDtypeStruct((N_EXPERTS, LANES), jnp.int32)],
        scratch_shapes=[pltpu.VMEM((D_MODEL, D_MODEL), BF16)] * 3,
        compiler_params=_cparams(("arbitrary",)),
        name="merge",
    )(gf, gb, rf, rb, p, p, p, p, x2, g1, sh2, nfs, gn, wg, wr, wo, wrt, brt, upper)


GATHER_UNROLL = 8


def _aligned(tok):
    off = tok * FEAT_TILES
    return off if isinstance(off, int) else pl.multiple_of(off, FEAT_TILES)


def _token_copy(src_hbm, dst, sem, src_tok, dst_tok):
    return pltpu.make_async_copy(src_hbm.at[pl.ds(_aligned(src_tok), FEAT_TILES), :],
                                 dst.at[pl.ds(_aligned(dst_tok), FEAT_TILES), :], sem)


def _wait_tokens(src_hbm, dst, sem):
    pltpu.make_async_copy(src_hbm.at[pl.ds(0, dst.shape[0]), :], dst, sem).wait()


GATHER_BUFS = 3
GATHER_AHEAD = GATHER_BUFS - 1


def _issue_rows(n, start_row, unrolled):
    if unrolled:
        for r in range(n):
            start_row(r, r % 2)
    else:
        def body(g, carry):
            for u in range(GATHER_UNROLL):
                start_row(g * GATHER_UNROLL + u, u % 2)
            return carry
        lax.fori_loop(0, n // GATHER_UNROLL, body, 0)


def _expert_body(be_ref, src0_ref, stok_ref, nused_ref, hn_hbm, wg_ref, wu_ref, wd_ref, y_ref, *scratch):
    i = pl.program_id(0)
    nused = nused_ref[0]
    bufs, sems = scratch[:GATHER_BUFS], scratch[GATHER_BUFS]
    wg_s, wu_s, wd_s = scratch[GATHER_BUFS + 1:]
    last = stok_ref.shape[0] - 1

    def issue(block, s, unrolled):
        base = src0_ref[block]

        def start_row(r, priority):
            tok = stok_ref[jnp.minimum(base + r, last)]
            _token_copy(hn_hbm, bufs[s], sems.at[s], tok, r).start(priority=priority)
        _issue_rows(MOE_BLOCK, start_row, unrolled)

    @pl.when(i == 0)
    def _():
        for b in range(GATHER_AHEAD):
            issue(b, b, False)

    @pl.when(jnp.logical_and(i < nused, jnp.logical_or(i == 0, be_ref[i] != be_ref[jnp.maximum(i - 1, 0)])))
    def _():
        wg_s[...] = wg_ref[0].astype(BF16)
        wu_s[...] = wu_ref[0].astype(BF16)
        wd_s[...] = wd_ref[0].astype(BF16)

    for cur in range(GATHER_BUFS):
        mine = i % GATHER_BUFS == cur

        @pl.when(jnp.logical_and(mine, i < nused))
        def _():
            _wait_tokens(hn_hbm, bufs[cur], sems.at[cur])
            issue(i + GATHER_AHEAD, (cur + GATHER_AHEAD) % GATHER_BUFS, True)
            xb = _load_token_tiled(bufs[cur], MOE_BLOCK).astype(BF16)
            gate = _dot(xb, wg_s[...])
            up = _dot(xb, wu_s[...])
            hid = (_silu(gate) * up).astype(BF16)
            _store_token_tiled(y_ref, _dot(hid, wd_s[...]))

        @pl.when(jnp.logical_and(mine, jnp.logical_and(i >= nused, i < nused + GATHER_AHEAD)))
        def _():
            _wait_tokens(hn_hbm, bufs[cur], sems.at[cur])

    @pl.when(i >= nused)
    def _():
        y_ref[...] = jnp.zeros_like(y_ref)


def _experts(block_e, src0, stok, nused, hn, w_gate, w_up, w_down, nblk):
    wmap = lambda i, be, s0, st, nu: (be[i], 0, 0)
    rows = MOE_BLOCK * FEAT_TILES
    return pl.pallas_call(
        _expert_body,
        grid_spec=pltpu.PrefetchScalarGridSpec(
            num_scalar_prefetch=4,
            grid=(nblk,),
            in_specs=[pl.BlockSpec(memory_space=pl.ANY),
                      pl.BlockSpec((1, D_MODEL, EXPERT_FF), wmap),
                      pl.BlockSpec((1, D_MODEL, EXPERT_FF), wmap),
                      pl.BlockSpec((1, EXPERT_FF, D_MODEL), wmap)],
            out_specs=pl.BlockSpec((rows, LANES), lambda i, be, s0, st, nu: (i, 0)),
            scratch_shapes=[pltpu.VMEM((rows, LANES), F32)] * GATHER_BUFS
                           + [pltpu.SemaphoreType.DMA((GATHER_BUFS,)),
                              pltpu.VMEM((D_MODEL, EXPERT_FF), BF16), pltpu.VMEM((D_MODEL, EXPERT_FF), BF16),
                              pltpu.VMEM((EXPERT_FF, D_MODEL), BF16)]),
        out_shape=jax.ShapeDtypeStruct((nblk * rows, LANES), F32),
        compiler_params=_cparams(("arbitrary",)),
        name="experts",
    )(block_e, src0, stok, nused, hn, w_gate, w_up, w_down)


def _combine_body(pos_ref, y_hbm, h_ref, ew_ref, g2_ref, nf_ref, o_ref, *scratch):
    i = pl.program_id(0)
    n = pl.num_programs(0)
    tm = COMBINE_TM
    t = pos_ref.shape[0] // TOP_K
    bufs, sems = scratch[:GATHER_BUFS], scratch[GATHER_BUFS]

    def issue(tile, s, unrolled):
        def start_row(j, priority):
            k, r = j % TOP_K, j // TOP_K
            _token_copy(y_hbm, bufs[s].at[k], sems.at[s], pos_ref[k * t + tile * tm + r], r).start(priority=priority)
        _issue_rows(tm * TOP_K, start_row, unrolled)

    def wait(s):
        for k in range(TOP_K):
            _wait_tokens(y_hbm, bufs[s].at[k], sems.at[s])

    @pl.when(i == 0)
    def _():
        for b in range(GATHER_AHEAD):
            issue(b, b, False)

    for cur in range(GATHER_BUFS):
        @pl.when(i % GATHER_BUFS == cur)
        def _():
            wait(cur)
            issue(jnp.minimum(i + GATHER_AHEAD, n - 1), (cur + GATHER_AHEAD) % GATHER_BUFS, True)
            wt = jnp.concatenate([ew_ref[...]] * (LANES // 8), axis=0).T
            moe = (wt[:, 0:1] * _load_token_tiled(bufs[cur].at[0], tm)
                   + wt[:, 1:2] * _load_token_tiled(bufs[cur].at[1], tm))
            h = h_ref[...] + g2_ref[0] * moe
            ms = jnp.mean(h * h, axis=-1, keepdims=True)
            o_ref[...] = h * lax.rsqrt(ms + NORM_EPS) * nf_ref[...]

        @pl.when(jnp.logical_and(i % GATHER_BUFS == cur, i == n - 1))
        def _():
            for ahead in range(1, GATHER_BUFS):
                wait((cur + ahead) % GATHER_BUFS)


def _combine(pos, y_pad, h, ew, g2, nf, seq):
    t = h.shape[0]
    tm = COMBINE_TM
    tiles_per_batch = seq // tm
    ybuf = pltpu.VMEM((TOP_K, tm * FEAT_TILES, LANES), F32)
    return pl.pallas_call(
        _combine_body,
        grid_spec=pltpu.PrefetchScalarGridSpec(
            num_scalar_prefetch=1,
            grid=(t // tm,),
            in_specs=[pl.BlockSpec(memory_space=pl.ANY),
                      pl.BlockSpec((tm, D_MODEL), lambda i, pos: (i, 0)),
                      pl.BlockSpec((8, tm), lambda i, pos: (0, i)),
                      pl.BlockSpec((1, 1, D_MODEL), lambda i, pos: (i // tiles_per_batch, 0, 0)),
                      pl.BlockSpec((1, D_MODEL), lambda i, pos: (0, 0))],
            out_specs=pl.BlockSpec((tm, D_MODEL), lambda i, pos: (i, 0)),
            scratch_shapes=[ybuf] * GATHER_BUFS + [pltpu.SemaphoreType.DMA((GATHER_BUFS,))]),
        out_shape=jax.ShapeDtypeStruct((t, D_MODEL), F32),
        compiler_params=_cparams(("arbitrary",)),
        name="combine",
    )(pos, y_pad, h, ew, g2, nf)


def _dispatch_indices(ids, ranks, counts, t):
    a = t * TOP_K
    nblk = a // MOE_BLOCK + N_EXPERTS + GATHER_AHEAD
    experts = jnp.arange(N_EXPERTS, dtype=jnp.int32)
    starts = jnp.cumsum(counts) - counts
    padded = (counts + MOE_BLOCK - 1) // MOE_BLOCK * MOE_BLOCK
    pends = jnp.cumsum(padded)
    pstarts = pends - padded
    block_start = jnp.arange(nblk, dtype=jnp.int32) * MOE_BLOCK
    block_e = jnp.minimum(jnp.sum((block_start[:, None] >= pends[None, :]).astype(jnp.int32), axis=1),
                          N_EXPERTS - 1)
    of_block = (block_e[:, None] == experts[None, :]).astype(jnp.int32)
    src0 = block_start + jnp.sum(of_block * (starts - pstarts)[None, :], axis=1)
    nused = pends[-1:] // MOE_BLOCK
    tok = jnp.arange(t, dtype=jnp.int32)[None, :]
    slot = jnp.arange(TOP_K, dtype=jnp.int32)[:, None]
    key = ids * a + tok * TOP_K + slot
    stok = (jnp.sort(key.reshape(a)) % a) // TOP_K
    ids_d = ids.reshape(a // LANES, LANES)
    pstart_of = jnp.zeros_like(ids_d)
    for e in range(N_EXPERTS):
        pstart_of = jnp.where(ids_d == e, pstarts[e], pstart_of)
    pos = (ranks.reshape(a // LANES, LANES) + pstart_of).reshape(a)
    i32 = lambda v: v.astype(jnp.int32)
    return i32(block_e), i32(src0), i32(stok), i32(nused), i32(pos), nblk


def kernel(x, c, ctx, c_ctx, w_ada, b_ada, norm_mix, norm_ffn, w_in, gla_lr_w, gla_lr_b, gla_norm,
           w_branch_gla, w_branch_ret, w_out, w_router_group, b_router_group, w_router_expert,
           b_router_expert, w_expert_gate, w_expert_up, w_expert_down, norm_final):
    batch, seq, d = x.shape
    ctx_len = ctx.shape[1]
    assert d == D_MODEL and w_ada.shape[0] == 1, "single-layer block with D_MODEL features"
    t = batch * seq

    c8 = jnp.zeros((8, d), F32).at[:batch].set(c).at[batch].set(c_ctx)
    mod = _ada(c8, w_ada[0], b_ada[0][None, :])
    sh1, sc1, g1, sh2, sc2, g2 = [mod[:, d * i:d * (i + 1)] for i in range(6)]
    lat = lambda m: m[:batch, None, :]
    cx = lambda m: m[batch:batch + 1, None, :]

    w_main, w_lr = _wprep(jnp.swapaxes(w_in[0], 0, 1))
    nm = norm_mix[0][None, :]

    p_ctx, lr_ctx = _proj(ctx.reshape(batch * ctx_len, d), cx(sh1), cx(sc1), nm, w_main, w_lr,
                          tm=ctx_len, rows_per_mod=batch * ctx_len)
    x2 = x.reshape(t, d)
    p_lat, lr_lat = _proj(x2, lat(sh1), lat(sc1), nm, w_main, w_lr, tm=PROJ_TM, rows_per_mod=seq)

    wlr_f = jnp.zeros((LANES, HEADS * DK), F32).at[:GLA_RANK].set(gla_lr_w[0, 0])
    wlr_b = jnp.zeros((LANES, HEADS * DK), F32).at[GLA_RANK:2 * GLA_RANK].set(gla_lr_w[0, 1])
    blr_f, blr_b = gla_lr_b[0, 0][None, :], gla_lr_b[0, 1][None, :]
    zero_state = jnp.zeros((batch, HEADS, DK, DV), F32)

    _, _, gs_f, gs_b, _ = _gla(p_ctx, lr_ctx, wlr_f, blr_f, wlr_b, blr_b, zero_state, zero_state,
                               batch, ctx_len, tb=ctx_len)
    gla_args = (p_lat, lr_lat, wlr_f, blr_f, wlr_b, blr_b, gs_f, gs_b, batch, seq)
    gla_f, gla_b, _, _, decay_range = _gla(*gla_args, tb=SCAN_TB)
    gla_f, gla_b = lax.cond(jnp.max(decay_range) > GLA_SAFE_RANGE,
                            lambda: tuple(_gla(*gla_args, tb=GLA_EXACT_TB, exact=True)[:2]),
                            lambda: (gla_f, gla_b))

    ones = jnp.ones((ctx_len, DK), F32)
    _, _, rs_f, rs_b = _ret(p_ctx, ones, jnp.zeros_like(ones), zero_state, zero_state, batch, ctx_len, tb=ctx_len)
    cos, sin = _rope_tables(seq)
    ret_f, ret_b, _, _ = _ret(p_lat, cos, sin, rs_f, rs_b, batch, seq, tb=SCAN_TB)

    wrt = jnp.zeros((d, ROUTER_W), F32)
    wrt = wrt.at[:, :N_GROUPS].set(w_router_group[0]).at[:, EXP_ROW0:EXP_ROW0 + N_EXPERTS].set(w_router_expert[0])
    wrt_hi = wrt.astype(BF16)
    wrt = jnp.concatenate([wrt_hi, (wrt - wrt_hi.astype(F32)).astype(BF16)], axis=1)
    brt = jnp.zeros((1, ROUTER_W), F32)
    brt = brt.at[0, :N_GROUPS].set(b_router_group[0]).at[0, EXP_ROW0:EXP_ROW0 + N_EXPERTS].set(b_router_expert[0])
    h, hn, ids8, ew8, cnt = _merge(gla_f, gla_b, ret_f, ret_b, p_lat, x2, lat(g1), lat(sh2),
                                   norm_ffn[0][None, None, :] * (1.0 + lat(sc2)),
                                   jnp.tile(gla_norm[0], HEADS)[None, :],
                                   w_branch_gla[0].astype(BF16), w_branch_ret[0].astype(BF16),
                                   w_out[0].astype(BF16), wrt, brt, seq, tm=MERGE_TM)

    block_e, src0, stok, nused, pos, nblk = _dispatch_indices(ids8[:TOP_K], ids8[TOP_K:2 * TOP_K], cnt[:, 0], t)
    y_pad = _experts(block_e, src0, stok, nused, hn, w_expert_gate[0], w_expert_up[0], w_expert_down[0], nblk)
    out = _combine(pos, y_pad, h, ew8, lat(g2), norm_final[None, :], seq)
    return out.reshape(batch, seq, d)
```

```python
import functools

import numpy as np
import jax
import jax.numpy as jnp
from jax import lax
from jax.experimental import pallas as pl
from jax.experimental.pallas import tpu as pltpu

F32 = jnp.float32
BF16 = jnp.bfloat16

D_MODEL = 1024
GRID_W = 64
HEADS = 4
DK = 128
DV = 256
GLA_RANK = 16
GLA_TAU = 16.0
GLA_CHUNK = 64
GLA_SUB = 16
RET_CHUNK = 128
ROPE_BASE = 10000.0
N_GROUPS = 4
EXPERTS_PER_GROUP = 8
N_EXPERTS = N_GROUPS * EXPERTS_PER_GROUP
TOP_K = 2
EXPERT_FF = 256
MOE_BLOCK = 128
NORM_EPS = 1e-6

COL_GQ, COL_GK, COL_GV, COL_GG = 0, 512, 1024, 2048
COL_RQ, COL_RK, COL_RV, COL_RG = 3072, 3584, 4096, 5120
COL_MG, COL_MR = 6144, 7168
PROJ_W = 8192
LANES = 128
FEAT_TILES = D_MODEL // LANES
ROUTER_W = 128
EXP_ROW0 = 8

VMEM_LIMIT = 56 * 1024 * 1024

PROJ_TM = 1024
PROJ_TN = 2048
SCAN_TB = 1024
GLA_PHASE_LAG = 2
RET_PHASE_LAG = 0
GLA_EXACT_TB = 256
GLA_SAFE_RANGE = 60.0
MERGE_TM = 512
MERGE_SPLIT = 2
COMBINE_TM = 256


def _cparams(sem):
    return pltpu.CompilerParams(dimension_semantics=sem, vmem_limit_bytes=VMEM_LIMIT)


def _sigmoid(x):
    return 1.0 / (1.0 + jnp.exp(-x))


def _silu(x):
    return x * _sigmoid(x)


def _dot(a, b):
    return jnp.dot(a, b, preferred_element_type=F32)


def _dot_nt(a, b):
    return lax.dot_general(a, b, (((1,), (1,)), ((), ())), preferred_element_type=F32)


def _ada_body(c_ref, w_ref, b_ref, o_ref):
    s = _silu(c_ref[...])
    o_ref[...] = _dot(s.astype(BF16), w_ref[...].astype(BF16)) + b_ref[...]


def _ada(c8, w, b):
    n = w.shape[1]
    tn = 1536
    return pl.pallas_call(
        _ada_body,
        grid=(n // tn,),
        in_specs=[pl.BlockSpec((8, D_MODEL), lambda j: (0, 0)),
                  pl.BlockSpec((D_MODEL, tn), lambda j: (0, j)),
                  pl.BlockSpec((1, tn), lambda j: (0, j))],
        out_specs=pl.BlockSpec((8, tn), lambda j: (0, j)),
        out_shape=jax.ShapeDtypeStruct((8, n), F32),
        compiler_params=_cparams(("arbitrary",)),
        name="ada",
    )(c8, w, b)


LR_COL0 = COL_GG + HEADS * DV
WPREP_ROWS = 1024


def _wprep_body(w_ref, wlr_ref, main_ref, lr_ref):
    main_ref[...] = w_ref[...].astype(BF16)
    pad = jnp.zeros((LANES - 2 * GLA_RANK, D_MODEL), F32)
    lr_ref[...] = jnp.concatenate([wlr_ref[...], pad], axis=0).astype(BF16)


def _wprep(wt):
    steps_before = LR_COL0 // WPREP_ROWS

    def src_row(i):
        row = jnp.where(i < steps_before, i * WPREP_ROWS, i * WPREP_ROWS + 2 * GLA_RANK)
        return (pl.multiple_of(row, 2 * GLA_RANK), 0)

    return pl.pallas_call(
        _wprep_body,
        grid=(PROJ_W // WPREP_ROWS,),
        in_specs=[pl.BlockSpec((pl.Element(WPREP_ROWS), pl.Element(D_MODEL)), src_row),
                  pl.BlockSpec((pl.Element(2 * GLA_RANK), pl.Element(D_MODEL)), lambda i: (LR_COL0, 0))],
        out_specs=[pl.BlockSpec((WPREP_ROWS, D_MODEL), lambda i: (i, 0)),
                   pl.BlockSpec((LANES, D_MODEL), lambda i: (0, 0))],
        out_shape=[jax.ShapeDtypeStruct((PROJ_W, D_MODEL), BF16), jax.ShapeDtypeStruct((LANES, D_MODEL), BF16)],
        compiler_params=_cparams(("arbitrary",)),
        name="wprep",
    )(wt, wt)


def _proj_body(x_ref, sh_ref, sc_ref, g_ref, w_ref, wlr_ref, o_ref, lr_ref, h_ref):
    @pl.when(pl.program_id(1) == 0)
    def _():
        x = x_ref[...]
        ms = jnp.mean(x * x, axis=-1, keepdims=True)
        y = x * lax.rsqrt(ms + NORM_EPS) * g_ref[...]
        hb = (y * (1.0 + sc_ref[0]) + sh_ref[0]).astype(BF16)
        h_ref[...] = hb
        lr_ref[...] = _dot_nt(hb, wlr_ref[...])

    o_ref[...] = _dot_nt(h_ref[...], w_ref[...]).astype(BF16)


def _proj(x2, sh, sc, g, w_main, w_lr, tm, rows_per_mod):
    t = x2.shape[0]
    tn = PROJ_TN
    tiles_per_mod = rows_per_mod // tm
    mod_map = lambda i, j: (i // tiles_per_mod, 0, 0)
    return pl.pallas_call(
        _proj_body,
        grid=(t // tm, PROJ_W // tn),
        in_specs=[pl.BlockSpec((tm, D_MODEL), lambda i, j: (i, 0)),
                  pl.BlockSpec((1, 1, D_MODEL), mod_map),
                  pl.BlockSpec((1, 1, D_MODEL), mod_map),
                  pl.BlockSpec((1, D_MODEL), lambda i, j: (0, 0)),
                  pl.BlockSpec((tn, D_MODEL), lambda i, j: (j, 0)),
                  pl.BlockSpec((LANES, D_MODEL), lambda i, j: (0, 0))],
        out_specs=[pl.BlockSpec((tm, tn), lambda i, j: (i, j)),
                   pl.BlockSpec((tm, LANES), lambda i, j: (i, 0))],
        out_shape=[jax.ShapeDtypeStruct((t, PROJ_W), BF16),
                   jax.ShapeDtypeStruct((t, LANES), F32)],
        scratch_shapes=[pltpu.VMEM((tm, D_MODEL), BF16)],
        compiler_params=_cparams(("parallel", "arbitrary")),
        name="proj",
    )(x2, sh, sc, g, w_main, w_lr)


def _gla_consts():
    c, s = GLA_CHUNK, GLA_SUB
    msk = {}
    for rev in (False, True):
        cols = []
        for i in range(c // s):
            keys = np.arange(c)[_key_rows(i, rev)][None, :]
            t = np.arange(c)[:, None]
            visible = (keys > t) if rev else (keys <= t)
            cols.append(np.where(t // s == i, visible, False))
        msk[rev] = np.concatenate(cols, axis=1).astype(np.float32)
    return msk


def _key_rows(i, rev):
    return slice(GLA_SUB * i, GLA_CHUNK) if rev else slice(0, GLA_SUB * (i + 1))


def _log_gate(lr, w, b):
    z = _dot(lr.astype(BF16), w.astype(BF16)) + b
    return -(jnp.maximum(-z, 0.0) + jnp.log(1.0 + jnp.exp(-jnp.abs(z)))) * (1.0 / GLA_TAU)


def _subchunk_scan(la, rev):
    c = la.shape[0]
    pos = lax.broadcasted_iota(jnp.int32, la.shape, 0) % GLA_SUB
    w = la
    step = 1
    while step < GLA_SUB:
        if rev:
            w = w + jnp.where(pos < GLA_SUB - step, pltpu.roll(w, c - step, 0), 0.0)
        else:
            w = w + jnp.where(pos >= step, pltpu.roll(w, step, 0), 0.0)
        step *= 2
    return w


def _gla_intra_exact(q, kf, v, b, rev, scratch):
    q_s, b_s, o_s = scratch
    c = GLA_CHUNK
    q_s[...] = q.astype(F32) * (DK ** -0.5)
    b_s[...] = b
    vf = v.astype(F32)
    row = lax.broadcasted_iota(jnp.int32, (c, 1), 0)

    def body(t, carry):
        qt = q_s[pl.ds(t, 1), :]
        bt = b_s[pl.ds(t, 1), :]
        visible = (row > t) if rev else (row <= t)
        decay = jnp.exp(jnp.where(visible, bt - b, -jnp.inf))
        col = jnp.sum(kf * decay * qt, axis=1, keepdims=True)
        o_s[pl.ds(t, 1), :] = jnp.sum(col * vf, axis=0, keepdims=True)
        return carry

    lax.fori_loop(0, c, body, 0)
    return o_s[...]


def _gla_chunk(q, k, v, la, state_ref, mask, rev, exact_scratch=None):
    c, s = GLA_CHUNK, GLA_SUB
    nsub = c // s
    w = _subchunk_scan(la, rev)
    tot = [w[s * i:s * i + 1] if rev else w[s * (i + 1) - 1:s * (i + 1)] for i in range(nsub)]
    anchors = [None] * nsub
    acc = jnp.zeros_like(tot[0])
    worst = jnp.zeros_like(tot[0])
    for i in (reversed(range(nsub)) if rev else range(nsub)):
        anchors[i] = acc
        acc = acc + tot[i]
        worst = jnp.maximum(worst, -tot[i])
    b_end = acc
    ref = jnp.concatenate([jnp.broadcast_to(a, (s, DK)) for a in anchors], axis=0)
    b = w + ref

    qt = q.astype(F32) * (DK ** -0.5) * jnp.exp(w)
    kf = k.astype(F32)
    if exact_scratch is None:
        kstack = jnp.concatenate([kf[_key_rows(i, rev)] * jnp.exp(anchors[i] - b[_key_rows(i, rev)])
                                  for i in range(nsub)], axis=0).astype(BF16)
        raw = _dot_nt(qt.astype(BF16), kstack)
    kst = kf * jnp.exp(b_end - b)
    xt = jnp.concatenate([kst, jnp.broadcast_to(jnp.exp(b_end), (c, DK))], axis=0).T
    update = _dot(xt[:, :c].astype(BF16), v)
    state = state_ref[...]
    inter = _dot((qt * jnp.exp(ref)).astype(BF16), state.astype(BF16))
    state_ref[...] = xt[:, c:c + 1] * state + update
    yield
    if exact_scratch is None:
        vstack = jnp.concatenate([v[_key_rows(i, rev)] for i in range(nsub)], axis=0)
        intra = _dot(jnp.where(mask != 0.0, raw, 0.0).astype(BF16), vstack)
    else:
        intra = _gla_intra_exact(q, kf, v, b, rev, exact_scratch)
    return inter + intra, worst


def _run_phased(chunks, emit, lag):
    def finish(tag, gen):
        try:
            next(gen)
        except StopIteration as done:
            emit(tag, done.value)
            return
        raise AssertionError("chunk generator has more than two phases")

    pending = []
    for tag, gen in chunks:
        next(gen)
        pending.append((tag, gen))
        if len(pending) > lag:
            finish(*pending.pop(0))
    for item in pending:
        finish(*item)


def _gla_body(qf_ref, kf_ref, vf_ref, lrf_ref, qb_ref, kb_ref, vb_ref, lrb_ref,
              wf_ref, bf_ref, wb_ref, bb_ref, mskf_ref, mskb_ref,
              s0f_ref, s0b_ref, of_ref, ob_ref, sf_ref, sb_ref, rng_ref, *exact_scratch, nchunks):
    @pl.when(pl.program_id(2) == 0)
    def _():
        sf_ref[...] = s0f_ref[...]
        sb_ref[...] = s0b_ref[...]

    c = GLA_CHUNK
    scratch = exact_scratch or None
    la_f = _log_gate(lrf_ref[...], wf_ref[...], bf_ref[...])
    la_b = _log_gate(lrb_ref[...], wb_ref[...], bb_ref[...])
    mskf, mskb = mskf_ref[...], mskb_ref[...]
    st_f, st_b = sf_ref.at[0, 0], sb_ref.at[0, 0]
    worst = [jnp.zeros((1, DK), F32)]

    def chunks():
        for n in range(nchunks):
            rf = slice(c * n, c * (n + 1))
            yield (of_ref, rf), _gla_chunk(qf_ref[rf, :], kf_ref[rf, :], vf_ref[rf, :], la_f[rf, :], st_f, mskf,
                                           False, scratch)
            m = nchunks - 1 - n
            rb = slice(c * m, c * (m + 1))
            yield (ob_ref, rb), _gla_chunk(qb_ref[rb, :], kb_ref[rb, :], vb_ref[rb, :], la_b[rb, :], st_b, mskb,
                                           True, scratch)

    def emit(tag, result):
        o_ref, rows = tag
        o_ref[rows, :] = result[0].astype(BF16)
        worst[0] = jnp.maximum(worst[0], result[1])

    _run_phased(chunks(), emit, GLA_PHASE_LAG)
    rng_ref[0, 0, 0] = jnp.broadcast_to(worst[0], (8, DK))


def _gla(p, lr, wlr_f, blr_f, wlr_b, blr_b, s0f, s0b, batch, seq, tb, exact=False):
    nb = seq // tb
    msk = _gla_consts()
    mskf, mskb = jnp.asarray(msk[False]), jnp.asarray(msk[True])

    def fwd(col0, width):
        return lambda b, h, i: (b * nb + i, col0 // width + h)

    def bwd(col0, width):
        return lambda b, h, i: (b * nb + nb - 1 - i, col0 // width + h)

    const2 = lambda b, h, i: (0, 0)
    headcol = lambda b, h, i: (0, h)
    st_map = lambda b, h, i: (b, h, 0, 0)
    st_spec = pl.BlockSpec((1, 1, DK, DV), st_map)
    st_shape = jax.ShapeDtypeStruct((batch, HEADS, DK, DV), F32)
    o_shape = jax.ShapeDtypeStruct((batch * seq, HEADS * DV), BF16)
    return pl.pallas_call(
        functools.partial(_gla_body, nchunks=tb // GLA_CHUNK),
        grid=(batch, HEADS, nb),
        in_specs=[pl.BlockSpec((tb, DK), fwd(COL_GQ, DK)),
                  pl.BlockSpec((tb, DK), fwd(COL_GK, DK)),
                  pl.BlockSpec((tb, DV), fwd(COL_GV, DV)),
                  pl.BlockSpec((tb, LANES), lambda b, h, i: (b * nb + i, 0)),
                  pl.BlockSpec((tb, DK), bwd(COL_GQ, DK)),
                  pl.BlockSpec((tb, DK), bwd(COL_GK, DK)),
                  pl.BlockSpec((tb, DV), bwd(COL_GV, DV)),
                  pl.BlockSpec((tb, LANES), lambda b, h, i: (b * nb + nb - 1 - i, 0)),
                  pl.BlockSpec((LANES, DK), headcol),
                  pl.BlockSpec((1, DK), headcol),
                  pl.BlockSpec((LANES, DK), headcol),
                  pl.BlockSpec((1, DK), headcol),
                  pl.BlockSpec(mskf.shape, const2),
                  pl.BlockSpec(mskb.shape, const2),
                  st_spec, st_spec],
        out_specs=[pl.BlockSpec((tb, DV), lambda b, h, i: (b * nb + i, h)),
                   pl.BlockSpec((tb, DV), lambda b, h, i: (b * nb + nb - 1 - i, h)),
                   st_spec, st_spec,
                   pl.BlockSpec((1, 1, 1, 8, DK), lambda b, h, i: (b, h, i, 0, 0))],
        out_shape=[o_shape, o_shape, st_shape, st_shape,
                   jax.ShapeDtypeStruct((batch, HEADS, nb, 8, DK), F32)],
        scratch_shapes=([pltpu.VMEM((GLA_CHUNK, DK), F32), pltpu.VMEM((GLA_CHUNK, DK), F32),
                         pltpu.VMEM((GLA_CHUNK, DV), F32)] if exact else []),
        compiler_params=_cparams(("parallel", "parallel", "arbitrary")),
        name="gla_exact" if exact else "gla",
    )(p, p, p, lr, p, p, p, lr, wlr_f, blr_f, wlr_b, blr_b, mskf, mskb, s0f, s0b)


def _ret_consts():
    c = RET_CHUNK
    hh = np.arange(HEADS, dtype=np.float64)
    lg = {False: np.log1p(-np.exp2(-5.0 - hh)), True: np.log1p(-np.exp2(-5.5 - hh))}
    t = np.arange(c, dtype=np.float64)[:, None]
    u = np.arange(c, dtype=np.float64)[None, :]
    pos = np.arange(c, dtype=np.float64)
    out = {}
    for rev in (False, True):
        g = lg[rev][:, None, None]
        if not rev:
            dmat = np.where(u <= t, np.exp((t - u) * g), 0.0)
            qd = np.exp((pos + 1.0)[None, :] * lg[rev][:, None])
            kd = np.exp((c - 1.0 - pos)[None, :] * lg[rev][:, None])
        else:
            dmat = np.where(u > t, np.exp((u - t) * g), 0.0)
            qd = np.exp((c - pos)[None, :] * lg[rev][:, None])
            kd = np.exp(pos[None, :] * lg[rev][:, None])
        cd = np.exp(c * lg[rev])
        out[rev] = (dmat.astype(np.float32),
                    np.repeat(qd[:, :, None], DV, axis=2).astype(np.float32),
                    np.repeat(kd[:, :, None], DK, axis=2).astype(np.float32),
                    np.repeat(cd[:, None, None], DV, axis=2).astype(np.float32))
    return out


def _rope(x, cos, sin):
    lane = lax.broadcasted_iota(jnp.int32, x.shape, 1)
    partner = jnp.where((lane % 64) < 32, pltpu.roll(x, DK - 32, 1), pltpu.roll(x, 32, 1))
    return x * cos + partner * sin


def _ret_chunk(q, k, v, cos, sin, state_ref, dmat, qd, kd, cd):
    qr = _rope(q.astype(F32), cos, sin).astype(BF16)
    kr = _rope(k.astype(F32) * (DK ** -0.5), cos, sin)
    raw = _dot_nt(qr, kr.astype(BF16))
    update = _dot((kr * kd).T.astype(BF16), v)
    state = state_ref[...]
    inter = _dot(qr, state.astype(BF16)) * qd
    state_ref[...] = cd * state + update
    yield
    return inter + _dot((raw * dmat).astype(BF16), v)


def _ret_body(qf_ref, kf_ref, vf_ref, cosf_ref, sinf_ref, qb_ref, kb_ref, vb_ref, cosb_ref, sinb_ref,
              dmf_ref, qdf_ref, kdf_ref, cdf_ref, dmb_ref, qdb_ref, kdb_ref, cdb_ref,
              s0f_ref, s0b_ref, of_ref, ob_ref, sf_ref, sb_ref, *, nchunks):
    @pl.when(pl.program_id(2) == 0)
    def _():
        sf_ref[...] = s0f_ref[...]
        sb_ref[...] = s0b_ref[...]

    c = RET_CHUNK
    cf = (dmf_ref[0], qdf_ref[0], kdf_ref[0], cdf_ref[0])
    cb = (dmb_ref[0], qdb_ref[0], kdb_ref[0], cdb_ref[0])
    st_f, st_b = sf_ref.at[0, 0], sb_ref.at[0, 0]
    def chunks():
        for n in range(nchunks):
            rf = slice(c * n, c * (n + 1))
            yield (of_ref, rf), _ret_chunk(qf_ref[rf, :], kf_ref[rf, :], vf_ref[rf, :], cosf_ref[rf, :],
                                           sinf_ref[rf, :], st_f, *cf)
            m = nchunks - 1 - n
            rb = slice(c * m, c * (m + 1))
            yield (ob_ref, rb), _ret_chunk(qb_ref[rb, :], kb_ref[rb, :], vb_ref[rb, :], cosb_ref[rb, :],
                                           sinb_ref[rb, :], st_b, *cb)

    def emit(tag, out):
        o_ref, rows = tag
        o_ref[rows, :] = out.astype(BF16)

    _run_phased(chunks(), emit, RET_PHASE_LAG)


def _ret(p, cos, sin, s0f, s0b, batch, seq, tb):
    nb = seq // tb
    consts = _ret_consts()

    def fwd(col0, width):
        return lambda b, h, i: (b * nb + i, col0 // width + h)

    def bwd(col0, width):
        return lambda b, h, i: (b * nb + nb - 1 - i, col0 // width + h)

    head3 = lambda b, h, i: (h, 0, 0)
    st_spec = pl.BlockSpec((1, 1, DK, DV), lambda b, h, i: (b, h, 0, 0))
    st_shape = jax.ShapeDtypeStruct((batch, HEADS, DK, DV), F32)
    o_shape = jax.ShapeDtypeStruct((batch * seq, HEADS * DV), BF16)
    const_specs = []
    const_args = []
    for rev in (False, True):
        for a in consts[rev]:
            const_specs.append(pl.BlockSpec((1,) + a.shape[1:], head3))
            const_args.append(jnp.asarray(a))
    return pl.pallas_call(
        functools.partial(_ret_body, nchunks=tb // RET_CHUNK),
        grid=(batch, HEADS, nb),
        in_specs=[pl.BlockSpec((tb, DK), fwd(COL_RQ, DK)),
                  pl.BlockSpec((tb, DK), fwd(COL_RK, DK)),
                  pl.BlockSpec((tb, DV), fwd(COL_RV, DV)),
                  pl.BlockSpec((tb, DK), lambda b, h, i: (i, 0)),
                  pl.BlockSpec((tb, DK), lambda b, h, i: (i, 0)),
                  pl.BlockSpec((tb, DK), bwd(COL_RQ, DK)),
                  pl.BlockSpec((tb, DK), bwd(COL_RK, DK)),
                  pl.BlockSpec((tb, DV), bwd(COL_RV, DV)),
                  pl.BlockSpec((tb, DK), lambda b, h, i: (nb - 1 - i, 0)),
                  pl.BlockSpec((tb, DK), lambda b, h, i: (nb - 1 - i, 0))]
                 + const_specs + [st_spec, st_spec],
        out_specs=[pl.BlockSpec((tb, DV), lambda b, h, i: (b * nb + i, h)),
                   pl.BlockSpec((tb, DV), lambda b, h, i: (b * nb + nb - 1 - i, h)),
                   st_spec, st_spec],
        out_shape=[o_shape, o_shape, st_shape, st_shape],
        compiler_params=_cparams(("parallel", "parallel", "arbitrary")),
        name="ret",
    )(p, p, p, cos, sin, p, p, p, cos, sin, *const_args, s0f, s0b)


def _rope_tables(seq):
    n = DK // 4
    inv = np.float32(ROPE_BASE) ** (-np.arange(n, dtype=np.float32) / np.float32(n))
    pos = np.arange(seq)
    ar = (pos // GRID_W).astype(np.float32)[:, None] * inv[None, :]
    ac = (pos % GRID_W).astype(np.float32)[:, None] * inv[None, :]
    cos = np.concatenate([np.cos(ar), np.cos(ar), np.cos(ac), np.cos(ac)], axis=1)
    sin = np.concatenate([-np.sin(ar), np.sin(ar), -np.sin(ac), np.sin(ac)], axis=1)
    return jnp.asarray(cos, F32), jnp.asarray(sin, F32)


def _route(logits_t):
    g = [logits_t[i:i + 1] for i in range(N_GROUPS)]
    gmax = jnp.maximum(jnp.maximum(g[0], g[1]), jnp.maximum(g[2], g[3]))
    gsel = jnp.where(g[0] == gmax, 0, jnp.where(g[1] == gmax, 1, jnp.where(g[2] == gmax, 2, 3)))
    gsum = (jnp.exp(g[0] - gmax) + jnp.exp(g[1] - gmax)) + (jnp.exp(g[2] - gmax) + jnp.exp(g[3] - gmax))
    gw = 1.0 / gsum
    e = [logits_t[EXP_ROW0 + EXPERTS_PER_GROUP * i:EXP_ROW0 + EXPERTS_PER_GROUP * (i + 1)]
         for i in range(N_GROUPS)]
    el = jnp.where(gsel == 0, e[0], jnp.where(gsel == 1, e[1], jnp.where(gsel == 2, e[2], e[3])))
    row = lax.broadcasted_iota(jnp.int32, el.shape, 0).astype(F32)
    none = float(EXPERTS_PER_GROUP)
    m1 = jnp.max(el, axis=0, keepdims=True)
    i1 = jnp.min(jnp.where(el == m1, row, none), axis=0, keepdims=True)
    el2 = jnp.where(row == i1, -jnp.inf, el)
    m2 = jnp.max(el2, axis=0, keepdims=True)
    i2 = jnp.min(jnp.where(el2 == m2, row, none), axis=0, keepdims=True)
    r = jnp.exp(m2 - m1)
    w1 = gw / (1.0 + r)
    w2 = gw * r / (1.0 + r)
    base = gsel * EXPERTS_PER_GROUP
    ids = jnp.concatenate([base + i1.astype(jnp.int32), base + i2.astype(jnp.int32)], axis=0)
    return ids, jnp.concatenate([w1, w2], axis=0)


def _store_token_tiled(ref, val):
    m = val.shape[0]
    for j in range(FEAT_TILES):
        ref[pl.ds(j, m, stride=FEAT_TILES), :] = val[:, LANES * j:LANES * (j + 1)]


def _load_token_tiled(ref, m):
    return jnp.concatenate([ref[pl.ds(j, m, stride=FEAT_TILES), :] for j in range(FEAT_TILES)], axis=1)


def _merge_body(gf_ref, gb_ref, rf_ref, rb_ref, gg_ref, rg_ref, mg_ref, mr_ref, x_ref,
                g1_ref, sh2_ref, nfs_ref, gn_ref, wg_ref, wr_ref, wo_ref, wrt_ref, brt_ref, upper_ref,
                h_ref, hn_ref, ids_ref, ew_ref, cnt_ref, wg_s, wr_s, wo_s, stage, sem):
    @pl.when(pl.program_id(0) == 0)
    def _():
        cnt_ref[...] = jnp.zeros_like(cnt_ref)
        for w_hbm, w_s in ((wg_ref, wg_s), (wr_ref, wr_s), (wo_ref, wo_s)):
            copy = pltpu.make_async_copy(w_hbm, stage, sem.at[0])
            copy.start()
            copy.wait()
            w_s[...] = stage[...].astype(BF16)

    def rows_to_logits(r0, n):
        rows = pl.ds(r0, n)
        og = (gf_ref[rows, :] + gb_ref[rows, :]).astype(F32)
        orr = (rf_ref[rows, :] + rb_ref[rows, :]).astype(F32)
        gparts, rparts = [], []
        for hh in range(HEADS):
            seg = og[:, DV * hh:DV * (hh + 1)]
            ms = jnp.mean(seg * seg, axis=-1, keepdims=True)
            gparts.append(seg * lax.rsqrt(ms + NORM_EPS))
            seg = orr[:, DV * hh:DV * (hh + 1)]
            mu = jnp.mean(seg, axis=-1, keepdims=True)
            cen = seg - mu
            var = jnp.mean(cen * cen, axis=-1, keepdims=True)
            rparts.append(cen * lax.rsqrt(var + NORM_EPS))
        o_gla = (jnp.concatenate(gparts, axis=1) * gn_ref[...]).astype(BF16) * _silu(gg_ref[rows, :])
        o_ret = jnp.concatenate(rparts, axis=1).astype(BF16) * _silu(rg_ref[rows, :])
        a_gla = _dot(o_gla, wg_s[...])
        a_ret = _dot(o_ret, wr_s[...])
        yield
        y = _sigmoid(mg_ref[rows, :]) * a_gla.astype(BF16) + _sigmoid(mr_ref[rows, :]) * a_ret.astype(BF16)
        out = _dot(y, wo_s[...])
        yield
        h = x_ref[rows, :] + g1_ref[0] * out
        h_ref[rows, :] = h
        ms = jnp.mean(h * h, axis=-1, keepdims=True)
        hn = h * lax.rsqrt(ms + NORM_EPS) * nfs_ref[0] + sh2_ref[0]
        _store_token_tiled(hn_ref.at[pl.ds(r0 * FEAT_TILES, n * FEAT_TILES)], hn)
        hn_hi = hn.astype(BF16)
        hn_lo = (hn - hn_hi.astype(F32)).astype(BF16)
        both_w = _dot(hn_hi, wrt_ref[...])
        lo_w = _dot(hn_lo, wrt_ref[:, :ROUTER_W])
        yield
        return (both_w[:, :ROUTER_W] + both_w[:, ROUTER_W:] + lo_w) + brt_ref[...]

    tm = x_ref.shape[0]
    n = tm // MERGE_SPLIT
    groups = [rows_to_logits(g * n, n) for g in range(MERGE_SPLIT)]
    parts = [None] * MERGE_SPLIT
    while any(p is None for p in parts):
        for g, gen in enumerate(groups):
            if parts[g] is None:
                try:
                    next(gen)
                except StopIteration as done:
                    parts[g] = done.value
    logits = jnp.concatenate(parts, axis=0)
    ids, ew = _route(logits.T)

    erow = lax.broadcasted_iota(jnp.int32, (N_EXPERTS, tm), 0)
    oh0 = jnp.where(erow == ids[0:1], 1.0, 0.0)
    oh1 = jnp.where(erow == ids[1:2], 1.0, 0.0)
    both = oh0 + oh1
    before = _dot(both.astype(BF16), upper_ref[...]) + cnt_ref[:, 0:1].astype(F32)
    rank0 = jnp.sum(oh0 * before, axis=0, keepdims=True)
    rank1 = jnp.sum(oh1 * before, axis=0, keepdims=True)
    total = cnt_ref[:, 0:1] + jnp.sum(both, axis=1, keepdims=True).astype(jnp.int32)
    cnt_ref[...] = jnp.broadcast_to(total, cnt_ref.shape)
    ids_ref[...] = jnp.concatenate([ids, rank0.astype(jnp.int32), rank1.astype(jnp.int32),
                                    jnp.zeros((4, tm), jnp.int32)], axis=0)
    ew_ref[...] = jnp.concatenate([ew, jnp.zeros((6, tm), F32)], axis=0)


def _merge(gf, gb, rf, rb, p, x2, g1, sh2, nfs, gn, wg, wr, wo, wrt, brt, seq, tm):
    t = x2.shape[0]
    tiles_per_batch = seq // tm
    row = lambda i: (i, 0)
    mod = lambda i: (i // tiles_per_batch, 0, 0)
    const = lambda i: (0, 0)
    tok = pl.BlockSpec((tm, D_MODEL), row)
    vec = pl.BlockSpec((1, D_MODEL), const)
    modspec = pl.BlockSpec((1, 1, D_MODEL), mod)
    wspec = pl.BlockSpec(memory_space=pl.ANY)
    upper = jnp.asarray(np.triu(np.ones((tm, tm), np.float32), 1), BF16)

    def pcol(col0):
        return pl.BlockSpec((tm, D_MODEL), lambda i: (i, col0 // D_MODEL))

    return pl.pallas_call(
        _merge_body,
        grid=(t // tm,),
        in_specs=[tok, tok, tok, tok, pcol(COL_GG), pcol(COL_RG), pcol(COL_MG), pcol(COL_MR), tok,
                  modspec, modspec, modspec, vec, wspec, wspec, wspec,
                  pl.BlockSpec((D_MODEL, 2 * ROUTER_W), const), pl.BlockSpec((1, ROUTER_W), const),
                  pl.BlockSpec((tm, tm), const)],
        out_specs=[tok, pl.BlockSpec((tm * FEAT_TILES, LANES), row),
                   pl.BlockSpec((8, tm), lambda i: (0, i)), pl.BlockSpec((8, tm), lambda i: (0, i)),
                   pl.BlockSpec((N_EXPERTS, LANES), const)],
        out_shape=[jax.ShapeDtypeStruct((t, D_MODEL), F32), jax.ShapeDtypeStruct((t * FEAT_TILES, LANES), F32),
                   jax.ShapeDtypeStruct((8, t), jnp.int32), jax.ShapeDtypeStruct((8, t), F32),
                   jax.ShapeDtypeStruct((N_EXPERTS, LANES), jnp.int32)],
        scratch_shapes=[pltpu.VMEM((D_MODEL, D_MODEL), BF16)] * 3
                       + [pltpu.VMEM((D_MODEL, D_MODEL), F32), pltpu.SemaphoreType.DMA((1,))],
        compiler_params=_cparams(("arbitrary",)),
        name="merge",
    )(gf, gb, rf, rb, p, p, p, p, x2, g1, sh2, nfs, gn, wg, wr, wo, wrt, brt, upper)


GATHER_UNROLL = 8


def _aligned(tok):
    off = tok * FEAT_TILES
    return off if isinstance(off, int) else pl.multiple_of(off, FEAT_TILES)


def _token_copy(src_hbm, dst, sem, src_tok, dst_tok):
    return pltpu.make_async_copy(src_hbm.at[pl.ds(_aligned(src_tok), FEAT_TILES), :],
                                 dst.at[pl.ds(_aligned(dst_tok), FEAT_TILES), :], sem)


def _wait_tokens(src_hbm, dst, sem):
    pltpu.make_async_copy(src_hbm.at[pl.ds(0, dst.shape[0]), :], dst, sem).wait()


GATHER_BUFS = 3
GATHER_AHEAD = GATHER_BUFS - 1


def _issue_rows(n, start_row, unrolled):
    if unrolled:
        for r in range(n):
            start_row(r, r % 2)
    else:
        def body(g, carry):
            for u in range(GATHER_UNROLL):
                start_row(g * GATHER_UNROLL + u, u % 2)
            return carry
        lax.fori_loop(0, n // GATHER_UNROLL, body, 0)


def _expert_body(be_ref, src0_ref, stok_ref, nused_ref, hn_hbm, wg_ref, wu_ref, wd_ref, y_ref, *scratch):
    i = pl.program_id(0)
    nused = nused_ref[0]
    bufs, sems = scratch[:GATHER_BUFS], scratch[GATHER_BUFS]
    wg_s, wu_s, wd_s = scratch[GATHER_BUFS + 1:]
    last = stok_ref.shape[0] - 1

    def issue(block, s, unrolled):
        base = src0_ref[block]

        def start_row(r, priority):
            tok = stok_ref[jnp.minimum(base + r, last)]
            _token_copy(hn_hbm, bufs[s], sems.at[s], tok, r).start(priority=priority)
        _issue_rows(MOE_BLOCK, start_row, unrolled)

    @pl.when(i == 0)
    def _():
        for b in range(GATHER_AHEAD):
            issue(b, b, False)

    @pl.when(jnp.logical_and(i < nused, jnp.logical_or(i == 0, be_ref[i] != be_ref[jnp.maximum(i - 1, 0)])))
    def _():
        wg_s[...] = wg_ref[0].astype(BF16)
        wu_s[...] = wu_ref[0].astype(BF16)
        wd_s[...] = wd_ref[0].astype(BF16)

    for cur in range(GATHER_BUFS):
        mine = i % GATHER_BUFS == cur

        @pl.when(jnp.logical_and(mine, i < nused))
        def _():
            _wait_tokens(hn_hbm, bufs[cur], sems.at[cur])
            issue(i + GATHER_AHEAD, (cur + GATHER_AHEAD) % GATHER_BUFS, True)
            xb = _load_token_tiled(bufs[cur], MOE_BLOCK).astype(BF16)
            gate = _dot(xb, wg_s[...])
            up = _dot(xb, wu_s[...])
            hid = (_silu(gate) * up).astype(BF16)
            _store_token_tiled(y_ref, _dot(hid, wd_s[...]))

        @pl.when(jnp.logical_and(mine, jnp.logical_and(i >= nused, i < nused + GATHER_AHEAD)))
        def _():
            _wait_tokens(hn_hbm, bufs[cur], sems.at[cur])

    @pl.when(i >= nused)
    def _():
        y_ref[...] = jnp.zeros_like(y_ref)


def _experts(block_e, src0, stok, nused, hn, w_gate, w_up, w_down, nblk):
    wmap = lambda i, be, s0, st, nu: (be[i], 0, 0)
    rows = MOE_BLOCK * FEAT_TILES
    return pl.pallas_call(
        _expert_body,
        grid_spec=pltpu.PrefetchScalarGridSpec(
            num_scalar_prefetch=4,
            grid=(nblk,),
            in_specs=[pl.BlockSpec(memory_space=pl.ANY),
                      pl.BlockSpec((1, D_MODEL, EXPERT_FF), wmap),
                      pl.BlockSpec((1, D_MODEL, EXPERT_FF), wmap),
                      pl.BlockSpec((1, EXPERT_FF, D_MODEL), wmap)],
            out_specs=pl.BlockSpec((rows, LANES), lambda i, be, s0, st, nu: (i, 0)),
            scratch_shapes=[pltpu.VMEM((rows, LANES), F32)] * GATHER_BUFS
                           + [pltpu.SemaphoreType.DMA((GATHER_BUFS,)),
                              pltpu.VMEM((D_MODEL, EXPERT_FF), BF16), pltpu.VMEM((D_MODEL, EXPERT_FF), BF16),
                              pltpu.VMEM((EXPERT_FF, D_MODEL), BF16)]),
        out_shape=jax.ShapeDtypeStruct((nblk * rows, LANES), F32),
        compiler_params=_cparams(("arbitrary",)),
        name="experts",
    )(block_e, src0, stok, nused, hn, w_gate, w_up, w_down)


def _combine_body(pos_ref, y_hbm, h_ref, ew_ref, g2_ref, nf_ref, o_ref, *scratch):
    i = pl.program_id(0)
    n = pl.num_programs(0)
    tm = COMBINE_TM
    t = pos_ref.shape[0] // TOP_K
    bufs, sems = scratch[:GATHER_BUFS], scratch[GATHER_BUFS]

    def issue(tile, s, unrolled):
        def start_row(j, priority):
            k, r = j % TOP_K, j // TOP_K
            _token_copy(y_hbm, bufs[s].at[k], sems.at[s], pos_ref[k * t + tile * tm + r], r).start(priority=priority)
        _issue_rows(tm * TOP_K, start_row, unrolled)

    def wait(s):
        for k in range(TOP_K):
            _wait_tokens(y_hbm, bufs[s].at[k], sems.at[s])

    @pl.when(i == 0)
    def _():
        for b in range(GATHER_AHEAD):
            issue(b, b, False)

    for cur in range(GATHER_BUFS):
        @pl.when(i % GATHER_BUFS == cur)
        def _():
            wait(cur)
            issue(jnp.minimum(i + GATHER_AHEAD, n - 1), (cur + GATHER_AHEAD) % GATHER_BUFS, True)
            wt = jnp.concatenate([ew_ref[...]] * (LANES // 8), axis=0).T
            moe = (wt[:, 0:1] * _load_token_tiled(bufs[cur].at[0], tm)
                   + wt[:, 1:2] * _load_token_tiled(bufs[cur].at[1], tm))
            h = h_ref[...] + g2_ref[0] * moe
            ms = jnp.mean(h * h, axis=-1, keepdims=True)
            o_ref[...] = h * lax.rsqrt(ms + NORM_EPS) * nf_ref[...]

        @pl.when(jnp.logical_and(i % GATHER_BUFS == cur, i == n - 1))
        def _():
            for ahead in range(1, GATHER_BUFS):
                wait((cur + ahead) % GATHER_BUFS)


def _combine(pos, y_pad, h, ew, g2, nf, seq):
    t = h.shape[0]
    tm = COMBINE_TM
    tiles_per_batch = seq // tm
    ybuf = pltpu.VMEM((TOP_K, tm * FEAT_TILES, LANES), F32)
    return pl.pallas_call(
        _combine_body,
        grid_spec=pltpu.PrefetchScalarGridSpec(
            num_scalar_prefetch=1,
            grid=(t // tm,),
            in_specs=[pl.BlockSpec(memory_space=pl.ANY),
                      pl.BlockSpec((tm, D_MODEL), lambda i, pos: (i, 0)),
                      pl.BlockSpec((8, tm), lambda i, pos: (0, i)),
                      pl.BlockSpec((1, 1, D_MODEL), lambda i, pos: (i // tiles_per_batch, 0, 0)),
                      pl.BlockSpec((1, D_MODEL), lambda i, pos: (0, 0))],
            out_specs=pl.BlockSpec((tm, D_MODEL), lambda i, pos: (i, 0)),
            scratch_shapes=[ybuf] * GATHER_BUFS + [pltpu.SemaphoreType.DMA((GATHER_BUFS,))]),
        out_shape=jax.ShapeDtypeStruct((t, D_MODEL), F32),
        compiler_params=_cparams(("arbitrary",)),
        name="combine",
    )(pos, y_pad, h, ew, g2, nf)


def _dispatch_indices(ids, ranks, counts, t):
    a = t * TOP_K
    nblk = a // MOE_BLOCK + N_EXPERTS + GATHER_AHEAD
    experts = jnp.arange(N_EXPERTS, dtype=jnp.int32)
    starts = jnp.cumsum(counts) - counts
    padded = (counts + MOE_BLOCK - 1) // MOE_BLOCK * MOE_BLOCK
    pends = jnp.cumsum(padded)
    pstarts = pends - padded
    block_start = jnp.arange(nblk, dtype=jnp.int32) * MOE_BLOCK
    block_e = jnp.minimum(jnp.sum((block_start[:, None] >= pends[None, :]).astype(jnp.int32), axis=1),
                          N_EXPERTS - 1)
    of_block = (block_e[:, None] == experts[None, :]).astype(jnp.int32)
    src0 = block_start + jnp.sum(of_block * (starts - pstarts)[None, :], axis=1)
    nused = pends[-1:] // MOE_BLOCK
    tok = jnp.arange(t, dtype=jnp.int32)[None, :]
    slot = jnp.arange(TOP_K, dtype=jnp.int32)[:, None]
    key = ids * a + tok * TOP_K + slot
    stok = (jnp.sort(key.reshape(a)) % a) // TOP_K
    ids_d = ids.reshape(a // LANES, LANES)
    pstart_of = jnp.zeros_like(ids_d)
    for e in range(N_EXPERTS):
        pstart_of = jnp.where(ids_d == e, pstarts[e], pstart_of)
    pos = (ranks.reshape(a // LANES, LANES) + pstart_of).reshape(a)
    i32 = lambda v: v.astype(jnp.int32)
    return i32(block_e), i32(src0), i32(stok), i32(nused), i32(pos), nblk


def kernel(x, c, ctx, c_ctx, w_ada, b_ada, norm_mix, norm_ffn, w_in, gla_lr_w, gla_lr_b, gla_norm,
           w_branch_gla, w_branch_ret, w_out, w_router_group, b_router_group, w_router_expert,
           b_router_expert, w_expert_gate, w_expert_up, w_expert_down, norm_final):
    batch, seq, d = x.shape
    ctx_len = ctx.shape[1]
    assert d == D_MODEL and w_ada.shape[0] == 1, "single-layer block with D_MODEL features"
    t = batch * seq

    c8 = jnp.zeros((8, d), F32).at[:batch].set(c).at[batch].set(c_ctx)
    mod = _ada(c8, w_ada[0], b_ada[0][None, :])
    sh1, sc1, g1, sh2, sc2, g2 = [mod[:, d * i:d * (i + 1)] for i in range(6)]
    lat = lambda m: m[:batch, None, :]
    cx = lambda m: m[batch:batch + 1, None, :]

    w_main, w_lr = _wprep(jnp.swapaxes(w_in[0], 0, 1))
    nm = norm_mix[0][None, :]

    p_ctx, lr_ctx = _proj(ctx.reshape(batch * ctx_len, d), cx(sh1), cx(sc1), nm, w_main, w_lr,
                          tm=ctx_len, rows_per_mod=batch * ctx_len)
    x2 = x.reshape(t, d)
    p_lat, lr_lat = _proj(x2, lat(sh1), lat(sc1), nm, w_main, w_lr, tm=PROJ_TM, rows_per_mod=seq)

    wlr_f = jnp.zeros((LANES, HEADS * DK), F32).at[:GLA_RANK].set(gla_lr_w[0, 0])
    wlr_b = jnp.zeros((LANES, HEADS * DK), F32).at[GLA_RANK:2 * GLA_RANK].set(gla_lr_w[0, 1])
    blr_f, blr_b = gla_lr_b[0, 0][None, :], gla_lr_b[0, 1][None, :]
    zero_state = jnp.zeros((batch, HEADS, DK, DV), F32)

    _, _, gs_f, gs_b, _ = _gla(p_ctx, lr_ctx, wlr_f, blr_f, wlr_b, blr_b, zero_state, zero_state,
                               batch, ctx_len, tb=ctx_len)
    gla_args = (p_lat, lr_lat, wlr_f, blr_f, wlr_b, blr_b, gs_f, gs_b, batch, seq)
    gla_f, gla_b, _, _, decay_range = _gla(*gla_args, tb=SCAN_TB)
    gla_f, gla_b = lax.cond(jnp.max(decay_range) > GLA_SAFE_RANGE,
                            lambda: tuple(_gla(*gla_args, tb=GLA_EXACT_TB, exact=True)[:2]),
                            lambda: (gla_f, gla_b))

    ones = jnp.ones((ctx_len, DK), F32)
    _, _, rs_f, rs_b = _ret(p_ctx, ones, jnp.zeros_like(ones), zero_state, zero_state, batch, ctx_len, tb=ctx_len)
    cos, sin = _rope_tables(seq)
    ret_f, ret_b, _, _ = _ret(p_lat, cos, sin, rs_f, rs_b, batch, seq, tb=SCAN_TB)

    wrt = jnp.zeros((d, ROUTER_W), F32)
    wrt = wrt.at[:, :N_GROUPS].set(w_router_group[0]).at[:, EXP_ROW0:EXP_ROW0 + N_EXPERTS].set(w_router_expert[0])
    wrt_hi = wrt.astype(BF16)
    wrt = jnp.concatenate([wrt_hi, (wrt - wrt_hi.astype(F32)).astype(BF16)], axis=1)
    brt = jnp.zeros((1, ROUTER_W), F32)
    brt = brt.at[0, :N_GROUPS].set(b_router_group[0]).at[0, EXP_ROW0:EXP_ROW0 + N_EXPERTS].set(b_router_expert[0])
    h, hn, ids8, ew8, cnt = _merge(gla_f, gla_b, ret_f, ret_b, p_lat, x2, lat(g1), lat(sh2),
                                   norm_ffn[0][None, None, :] * (1.0 + lat(sc2)),
                                   jnp.tile(gla_norm[0], HEADS)[None, :],
                                   w_branch_gla[0], w_branch_ret[0], w_out[0], wrt, brt, seq, tm=MERGE_TM)

    block_e, src0, stok, nused, pos, nblk = _dispatch_indices(ids8[:TOP_K], ids8[TOP_K:2 * TOP_K], cnt[:, 0], t)
    y_pad = _experts(block_e, src0, stok, nused, hn, w_expert_gate[0], w_expert_up[0], w_expert_down[0], nblk)
    out = _combine(pos, y_pad, h, ew8, lat(g2), norm_final[None, :], seq)
    return out.reshape(batch, seq, d)
```

```python
import functools

import numpy as np
import jax
import jax.numpy as jnp
from jax import lax
from jax.experimental import pallas as pl
from jax.experimental.pallas import tpu as pltpu

F32 = jnp.float32
BF16 = jnp.bfloat16

D_MODEL = 1024
GRID_W = 64
HEADS = 4
DK = 128
DV = 256
GLA_RANK = 16
GLA_TAU = 16.0
GLA_CHUNK = 64
GLA_SUB = 16
RET_CHUNK = 128
ROPE_BASE = 10000.0
N_GROUPS = 4
EXPERTS_PER_GROUP = 8
N_EXPERTS = N_GROUPS * EXPERTS_PER_GROUP
TOP_K = 2
EXPERT_FF = 256
MOE_BLOCK = 128
NORM_EPS = 1e-6

COL_GQ, COL_GK, COL_GV, COL_GG = 0, 512, 1024, 2048
COL_RQ, COL_RK, COL_RV, COL_RG = 3072, 3584, 4096, 5120
COL_MG, COL_MR = 6144, 7168
PROJ_W = 8192
LANES = 128
FEAT_TILES = D_MODEL // LANES
ROUTER_W = 128
EXP_ROW0 = 8

VMEM_LIMIT = 56 * 1024 * 1024

PROJ_TM = 1024
PROJ_TN = 2048
SCAN_TB = 1024
GLA_PHASE_LAG = 2
RET_PHASE_LAG = 0
GLA_EXACT_TB = 256
GLA_SAFE_RANGE = 60.0
MERGE_TM = 512
MERGE_SPLIT = 2
COMBINE_TM = 256


def _cparams(sem):
    return pltpu.CompilerParams(dimension_semantics=sem, vmem_limit_bytes=VMEM_LIMIT)


def _sigmoid(x):
    return 1.0 / (1.0 + jnp.exp(-x))


def _silu(x):
    return x * _sigmoid(x)


def _dot(a, b):
    return jnp.dot(a, b, preferred_element_type=F32)


def _dot_nt(a, b):
    return lax.dot_general(a, b, (((1,), (1,)), ((), ())), preferred_element_type=F32)


def _ada_body(c_ref, w_ref, b_ref, o_ref):
    s = _silu(c_ref[...])
    o_ref[...] = _dot(s.astype(BF16), w_ref[...].astype(BF16)) + b_ref[...]


def _ada(c8, w, b):
    n = w.shape[1]
    tn = 1536
    return pl.pallas_call(
        _ada_body,
        grid=(n // tn,),
        in_specs=[pl.BlockSpec((8, D_MODEL), lambda j: (0, 0)),
                  pl.BlockSpec((D_MODEL, tn), lambda j: (0, j)),
                  pl.BlockSpec((1, tn), lambda j: (0, j))],
        out_specs=pl.BlockSpec((8, tn), lambda j: (0, j)),
        out_shape=jax.ShapeDtypeStruct((8, n), F32),
        compiler_params=_cparams(("arbitrary",)),
        name="ada",
    )(c8, w, b)


LR_COL0 = COL_GG + HEADS * DV
WPREP_ROWS = 1024


def _wprep_body(w_ref, wlr_ref, main_ref, lr_ref):
    main_ref[...] = w_ref[...].astype(BF16)
    pad = jnp.zeros((LANES - 2 * GLA_RANK, D_MODEL), F32)
    lr_ref[...] = jnp.concatenate([wlr_ref[...], pad], axis=0).astype(BF16)


def _wprep(wt):
    steps_before = LR_COL0 // WPREP_ROWS

    def src_row(i):
        row = jnp.where(i < steps_before, i * WPREP_ROWS, i * WPREP_ROWS + 2 * GLA_RANK)
        return (pl.multiple_of(row, 2 * GLA_RANK), 0)

    return pl.pallas_call(
        _wprep_body,
        grid=(PROJ_W // WPREP_ROWS,),
        in_specs=[pl.BlockSpec((pl.Element(WPREP_ROWS), pl.Element(D_MODEL)), src_row),
                  pl.BlockSpec((pl.Element(2 * GLA_RANK), pl.Element(D_MODEL)), lambda i: (LR_COL0, 0))],
        out_specs=[pl.BlockSpec((WPREP_ROWS, D_MODEL), lambda i: (i, 0)),
                   pl.BlockSpec((LANES, D_MODEL), lambda i: (0, 0))],
        out_shape=[jax.ShapeDtypeStruct((PROJ_W, D_MODEL), BF16), jax.ShapeDtypeStruct((LANES, D_MODEL), BF16)],
        compiler_params=_cparams(("arbitrary",)),
        name="wprep",
    )(wt, wt)


def _proj_body(x_ref, sh_ref, sc_ref, g_ref, w_ref, wlr_ref, o_ref, lr_ref, h_ref):
    @pl.when(pl.program_id(1) == 0)
    def _():
        x = x_ref[...]
        ms = jnp.mean(x * x, axis=-1, keepdims=True)
        y = x * lax.rsqrt(ms + NORM_EPS) * g_ref[...]
        hb = (y * (1.0 + sc_ref[0]) + sh_ref[0]).astype(BF16)
        h_ref[...] = hb
        lr_ref[...] = _dot_nt(hb, wlr_ref[...])

    o_ref[...] = _dot_nt(h_ref[...], w_ref[...]).astype(BF16)


def _proj(x2, sh, sc, g, w_main, w_lr, tm, rows_per_mod):
    t = x2.shape[0]
    tn = PROJ_TN
    tiles_per_mod = rows_per_mod // tm
    mod_map = lambda i, j: (i // tiles_per_mod, 0, 0)
    return pl.pallas_call(
        _proj_body,
        grid=(t // tm, PROJ_W // tn),
        in_specs=[pl.BlockSpec((tm, D_MODEL), lambda i, j: (i, 0)),
                  pl.BlockSpec((1, 1, D_MODEL), mod_map),
                  pl.BlockSpec((1, 1, D_MODEL), mod_map),
                  pl.BlockSpec((1, D_MODEL), lambda i, j: (0, 0)),
                  pl.BlockSpec((tn, D_MODEL), lambda i, j: (j, 0)),
                  pl.BlockSpec((LANES, D_MODEL), lambda i, j: (0, 0))],
        out_specs=[pl.BlockSpec((tm, tn), lambda i, j: (i, j)),
                   pl.BlockSpec((tm, LANES), lambda i, j: (i, 0))],
        out_shape=[jax.ShapeDtypeStruct((t, PROJ_W), BF16),
                   jax.ShapeDtypeStruct((t, LANES), F32)],
        scratch_shapes=[pltpu.VMEM((tm, D_MODEL), BF16)],
        compiler_params=_cparams(("parallel", "arbitrary")),
        name="proj",
    )(x2, sh, sc, g, w_main, w_lr)


def _gla_consts():
    c, s = GLA_CHUNK, GLA_SUB
    msk = {}
    for rev in (False, True):
        cols = []
        for i in range(c // s):
            keys = np.arange(c)[_key_rows(i, rev)][None, :]
            t = np.arange(c)[:, None]
            visible = (keys > t) if rev else (keys <= t)
            cols.append(np.where(t // s == i, visible, False))
        msk[rev] = np.concatenate(cols, axis=1).astype(np.float32)
    return msk


def _key_rows(i, rev):
    return slice(GLA_SUB * i, GLA_CHUNK) if rev else slice(0, GLA_SUB * (i + 1))


def _log_gate(lr, w, b):
    z = _dot(lr.astype(BF16), w.astype(BF16)) + b
    return -(jnp.maximum(-z, 0.0) + jnp.log(1.0 + jnp.exp(-jnp.abs(z)))) * (1.0 / GLA_TAU)


def _subchunk_scan(la, rev):
    c = la.shape[0]
    pos = lax.broadcasted_iota(jnp.int32, la.shape, 0) % GLA_SUB
    w = la
    step = 1
    while step < GLA_SUB:
        if rev:
            w = w + jnp.where(pos < GLA_SUB - step, pltpu.roll(w, c - step, 0), 0.0)
        else:
            w = w + jnp.where(pos >= step, pltpu.roll(w, step, 0), 0.0)
        step *= 2
    return w


def _gla_intra_exact(q, kf, v, b, rev, scratch):
    q_s, b_s, o_s = scratch
    c = GLA_CHUNK
    q_s[...] = q.astype(F32) * (DK ** -0.5)
    b_s[...] = b
    vf = v.astype(F32)
    row = lax.broadcasted_iota(jnp.int32, (c, 1), 0)

    def body(t, carry):
        qt = q_s[pl.ds(t, 1), :]
        bt = b_s[pl.ds(t, 1), :]
        visible = (row > t) if rev else (row <= t)
        decay = jnp.exp(jnp.where(visible, bt - b, -jnp.inf))
        col = jnp.sum(kf * decay * qt, axis=1, keepdims=True)
        o_s[pl.ds(t, 1), :] = jnp.sum(col * vf, axis=0, keepdims=True)
        return carry

    lax.fori_loop(0, c, body, 0)
    return o_s[...]


def _gla_chunk(q, k, v, la, state_ref, mask, rev, exact_scratch=None):
    c, s = GLA_CHUNK, GLA_SUB
    nsub = c // s
    w = _subchunk_scan(la, rev)
    tot = [w[s * i:s * i + 1] if rev else w[s * (i + 1) - 1:s * (i + 1)] for i in range(nsub)]
    anchors = [None] * nsub
    acc = jnp.zeros_like(tot[0])
    worst = jnp.zeros_like(tot[0])
    for i in (reversed(range(nsub)) if rev else range(nsub)):
        anchors[i] = acc
        acc = acc + tot[i]
        worst = jnp.maximum(worst, -tot[i])
    b_end = acc
    ref = jnp.concatenate([jnp.broadcast_to(a, (s, DK)) for a in anchors], axis=0)
    b = w + ref

    qt = q.astype(F32) * (DK ** -0.5) * jnp.exp(w)
    kf = k.astype(F32)
    if exact_scratch is None:
        kstack = jnp.concatenate([kf[_key_rows(i, rev)] * jnp.exp(anchors[i] - b[_key_rows(i, rev)])
                                  for i in range(nsub)], axis=0).astype(BF16)
        raw = _dot_nt(qt.astype(BF16), kstack)
    kst = kf * jnp.exp(b_end - b)
    xt = jnp.concatenate([kst, jnp.broadcast_to(jnp.exp(b_end), (c, DK))], axis=0).T
    update = _dot(xt[:, :c].astype(BF16), v)
    state = state_ref[...]
    inter = _dot((qt * jnp.exp(ref)).astype(BF16), state.astype(BF16))
    state_ref[...] = xt[:, c:c + 1] * state + update
    yield
    if exact_scratch is None:
        vstack = jnp.concatenate([v[_key_rows(i, rev)] for i in range(nsub)], axis=0)
        intra = _dot(jnp.where(mask != 0.0, raw, 0.0).astype(BF16), vstack)
    else:
        intra = _gla_intra_exact(q, kf, v, b, rev, exact_scratch)
    return inter + intra, worst


def _round_robin(gens):
    results = [None] * len(gens)
    live = list(range(len(gens)))
    while live:
        for g in list(live):
            try:
                next(gens[g])
            except StopIteration as done:
                results[g] = done.value
                live.remove(g)
    return results


def _run_phased(chunks, emit, lag):
    def finish(tag, gen):
        try:
            next(gen)
        except StopIteration as done:
            emit(tag, done.value)
            return
        raise AssertionError("chunk generator has more than two phases")

    pending = []
    for tag, gen in chunks:
        next(gen)
        pending.append((tag, gen))
        if len(pending) > lag:
            finish(*pending.pop(0))
    for item in pending:
        finish(*item)


def _gla_body(qf_ref, kf_ref, vf_ref, lrf_ref, qb_ref, kb_ref, vb_ref, lrb_ref,
              wf_ref, bf_ref, wb_ref, bb_ref, mskf_ref, mskb_ref,
              s0f_ref, s0b_ref, of_ref, ob_ref, sf_ref, sb_ref, rng_ref, *exact_scratch, nchunks):
    @pl.when(pl.program_id(2) == 0)
    def _():
        sf_ref[...] = s0f_ref[...]
        sb_ref[...] = s0b_ref[...]

    c = GLA_CHUNK
    scratch = exact_scratch or None
    la_f = _log_gate(lrf_ref[...], wf_ref[...], bf_ref[...])
    la_b = _log_gate(lrb_ref[...], wb_ref[...], bb_ref[...])
    mskf, mskb = mskf_ref[...], mskb_ref[...]
    st_f, st_b = sf_ref.at[0, 0], sb_ref.at[0, 0]
    worst = [jnp.zeros((1, DK), F32)]

    def chunks():
        for n in range(nchunks):
            rf = slice(c * n, c * (n + 1))
            yield (of_ref, rf), _gla_chunk(qf_ref[rf, :], kf_ref[rf, :], vf_ref[rf, :], la_f[rf, :], st_f, mskf,
                                           False, scratch)
            m = nchunks - 1 - n
            rb = slice(c * m, c * (m + 1))
            yield (ob_ref, rb), _gla_chunk(qb_ref[rb, :], kb_ref[rb, :], vb_ref[rb, :], la_b[rb, :], st_b, mskb,
                                           True, scratch)

    def emit(tag, result):
        o_ref, rows = tag
        o_ref[rows, :] = result[0].astype(BF16)
        worst[0] = jnp.maximum(worst[0], result[1])

    _run_phased(chunks(), emit, GLA_PHASE_LAG)
    rng_ref[0, 0, 0] = jnp.broadcast_to(worst[0], (8, DK))


def _gla(p, lr, wlr_f, blr_f, wlr_b, blr_b, s0f, s0b, batch, seq, tb, exact=False):
    nb = seq // tb
    msk = _gla_consts()
    mskf, mskb = jnp.asarray(msk[False]), jnp.asarray(msk[True])

    def fwd(col0, width):
        return lambda b, h, i: (b * nb + i, col0 // width + h)

    def bwd(col0, width):
        return lambda b, h, i: (b * nb + nb - 1 - i, col0 // width + h)

    const2 = lambda b, h, i: (0, 0)
    headcol = lambda b, h, i: (0, h)
    st_map = lambda b, h, i: (b, h, 0, 0)
    st_spec = pl.BlockSpec((1, 1, DK, DV), st_map)
    st_shape = jax.ShapeDtypeStruct((batch, HEADS, DK, DV), F32)
    o_shape = jax.ShapeDtypeStruct((batch * seq, HEADS * DV), BF16)
    return pl.pallas_call(
        functools.partial(_gla_body, nchunks=tb // GLA_CHUNK),
        grid=(batch, HEADS, nb),
        in_specs=[pl.BlockSpec((tb, DK), fwd(COL_GQ, DK)),
                  pl.BlockSpec((tb, DK), fwd(COL_GK, DK)),
                  pl.BlockSpec((tb, DV), fwd(COL_GV, DV)),
                  pl.BlockSpec((tb, LANES), lambda b, h, i: (b * nb + i, 0)),
                  pl.BlockSpec((tb, DK), bwd(COL_GQ, DK)),
                  pl.BlockSpec((tb, DK), bwd(COL_GK, DK)),
                  pl.BlockSpec((tb, DV), bwd(COL_GV, DV)),
                  pl.BlockSpec((tb, LANES), lambda b, h, i: (b * nb + nb - 1 - i, 0)),
                  pl.BlockSpec((LANES, DK), headcol),
                  pl.BlockSpec((1, DK), headcol),
                  pl.BlockSpec((LANES, DK), headcol),
                  pl.BlockSpec((1, DK), headcol),
                  pl.BlockSpec(mskf.shape, const2),
                  pl.BlockSpec(mskb.shape, const2),
                  st_spec, st_spec],
        out_specs=[pl.BlockSpec((tb, DV), lambda b, h, i: (b * nb + i, h)),
                   pl.BlockSpec((tb, DV), lambda b, h, i: (b * nb + nb - 1 - i, h)),
                   st_spec, st_spec,
                   pl.BlockSpec((1, 1, 1, 8, DK), lambda b, h, i: (b, h, i, 0, 0))],
        out_shape=[o_shape, o_shape, st_shape, st_shape,
                   jax.ShapeDtypeStruct((batch, HEADS, nb, 8, DK), F32)],
        scratch_shapes=([pltpu.VMEM((GLA_CHUNK, DK), F32), pltpu.VMEM((GLA_CHUNK, DK), F32),
                         pltpu.VMEM((GLA_CHUNK, DV), F32)] if exact else []),
        compiler_params=_cparams(("parallel", "parallel", "arbitrary")),
        name="gla_exact" if exact else "gla",
    )(p, p, p, lr, p, p, p, lr, wlr_f, blr_f, wlr_b, blr_b, mskf, mskb, s0f, s0b)


def _ret_consts():
    c = RET_CHUNK
    hh = np.arange(HEADS, dtype=np.float64)
    lg = {False: np.log1p(-np.exp2(-5.0 - hh)), True: np.log1p(-np.exp2(-5.5 - hh))}
    t = np.arange(c, dtype=np.float64)[:, None]
    u = np.arange(c, dtype=np.float64)[None, :]
    pos = np.arange(c, dtype=np.float64)
    out = {}
    for rev in (False, True):
        g = lg[rev][:, None, None]
        if not rev:
            dmat = np.where(u <= t, np.exp((t - u) * g), 0.0)
            qd = np.exp((pos + 1.0)[None, :] * lg[rev][:, None])
            kd = np.exp((c - 1.0 - pos)[None, :] * lg[rev][:, None])
        else:
            dmat = np.where(u > t, np.exp((u - t) * g), 0.0)
            qd = np.exp((c - pos)[None, :] * lg[rev][:, None])
            kd = np.exp(pos[None, :] * lg[rev][:, None])
        cd = np.exp(c * lg[rev])
        out[rev] = (dmat.astype(np.float32),
                    np.repeat(qd[:, :, None], DV, axis=2).astype(np.float32),
                    np.repeat(kd[:, :, None], DK, axis=2).astype(np.float32),
                    np.repeat(cd[:, None, None], DV, axis=2).astype(np.float32))
    return out


def _rope(x, cos, sin):
    lane = lax.broadcasted_iota(jnp.int32, x.shape, 1)
    partner = jnp.where((lane % 64) < 32, pltpu.roll(x, DK - 32, 1), pltpu.roll(x, 32, 1))
    return x * cos + partner * sin


def _ret_chunk(q, k, v, cos, sin, state_ref, dmat, qd, kd, cd):
    qr = _rope(q.astype(F32), cos, sin).astype(BF16)
    kr = _rope(k.astype(F32) * (DK ** -0.5), cos, sin)
    raw = _dot_nt(qr, kr.astype(BF16))
    update = _dot((kr * kd).T.astype(BF16), v)
    state = state_ref[...]
    inter = _dot(qr, state.astype(BF16)) * qd
    state_ref[...] = cd * state + update
    yield
    return inter + _dot((raw * dmat).astype(BF16), v)


def _ret_body(qf_ref, kf_ref, vf_ref, cosf_ref, sinf_ref, qb_ref, kb_ref, vb_ref, cosb_ref, sinb_ref,
              dmf_ref, qdf_ref, kdf_ref, cdf_ref, dmb_ref, qdb_ref, kdb_ref, cdb_ref,
              s0f_ref, s0b_ref, of_ref, ob_ref, sf_ref, sb_ref, *, nchunks):
    @pl.when(pl.program_id(2) == 0)
    def _():
        sf_ref[...] = s0f_ref[...]
        sb_ref[...] = s0b_ref[...]

    c = RET_CHUNK
    cf = (dmf_ref[0], qdf_ref[0], kdf_ref[0], cdf_ref[0])
    cb = (dmb_ref[0], qdb_ref[0], kdb_ref[0], cdb_ref[0])
    st_f, st_b = sf_ref.at[0, 0], sb_ref.at[0, 0]
    def chunks():
        for n in range(nchunks):
            rf = slice(c * n, c * (n + 1))
            yield (of_ref, rf), _ret_chunk(qf_ref[rf, :], kf_ref[rf, :], vf_ref[rf, :], cosf_ref[rf, :],
                                           sinf_ref[rf, :], st_f, *cf)
            m = nchunks - 1 - n
            rb = slice(c * m, c * (m + 1))
            yield (ob_ref, rb), _ret_chunk(qb_ref[rb, :], kb_ref[rb, :], vb_ref[rb, :], cosb_ref[rb, :],
                                           sinb_ref[rb, :], st_b, *cb)

    def emit(tag, out):
        o_ref, rows = tag
        o_ref[rows, :] = out.astype(BF16)

    _run_phased(chunks(), emit, RET_PHASE_LAG)


def _ret(p, cos, sin, s0f, s0b, batch, seq, tb):
    nb = seq // tb
    consts = _ret_consts()

    def fwd(col0, width):
        return lambda b, h, i: (b * nb + i, col0 // width + h)

    def bwd(col0, width):
        return lambda b, h, i: (b * nb + nb - 1 - i, col0 // width + h)

    head3 = lambda b, h, i: (h, 0, 0)
    st_spec = pl.BlockSpec((1, 1, DK, DV), lambda b, h, i: (b, h, 0, 0))
    st_shape = jax.ShapeDtypeStruct((batch, HEADS, DK, DV), F32)
    o_shape = jax.ShapeDtypeStruct((batch * seq, HEADS * DV), BF16)
    const_specs = []
    const_args = []
    for rev in (False, True):
        for a in consts[rev]:
            const_specs.append(pl.BlockSpec((1,) + a.shape[1:], head3))
            const_args.append(jnp.asarray(a))
    return pl.pallas_call(
        functools.partial(_ret_body, nchunks=tb // RET_CHUNK),
        grid=(batch, HEADS, nb),
        in_specs=[pl.BlockSpec((tb, DK), fwd(COL_RQ, DK)),
                  pl.BlockSpec((tb, DK), fwd(COL_RK, DK)),
                  pl.BlockSpec((tb, DV), fwd(COL_RV, DV)),
                  pl.BlockSpec((tb, DK), lambda b, h, i: (i, 0)),
                  pl.BlockSpec((tb, DK), lambda b, h, i: (i, 0)),
                  pl.BlockSpec((tb, DK), bwd(COL_RQ, DK)),
                  pl.BlockSpec((tb, DK), bwd(COL_RK, DK)),
                  pl.BlockSpec((tb, DV), bwd(COL_RV, DV)),
                  pl.BlockSpec((tb, DK), lambda b, h, i: (nb - 1 - i, 0)),
                  pl.BlockSpec((tb, DK), lambda b, h, i: (nb - 1 - i, 0))]
                 + const_specs + [st_spec, st_spec],
        out_specs=[pl.BlockSpec((tb, DV), lambda b, h, i: (b * nb + i, h)),
                   pl.BlockSpec((tb, DV), lambda b, h, i: (b * nb + nb - 1 - i, h)),
                   st_spec, st_spec],
        out_shape=[o_shape, o_shape, st_shape, st_shape],
        compiler_params=_cparams(("parallel", "parallel", "arbitrary")),
        name="ret",
    )(p, p, p, cos, sin, p, p, p, cos, sin, *const_args, s0f, s0b)


def _rope_tables(seq):
    n = DK // 4
    inv = np.float32(ROPE_BASE) ** (-np.arange(n, dtype=np.float32) / np.float32(n))
    pos = np.arange(seq)
    ar = (pos // GRID_W).astype(np.float32)[:, None] * inv[None, :]
    ac = (pos % GRID_W).astype(np.float32)[:, None] * inv[None, :]
    cos = np.concatenate([np.cos(ar), np.cos(ar), np.cos(ac), np.cos(ac)], axis=1)
    sin = np.concatenate([-np.sin(ar), np.sin(ar), -np.sin(ac), np.sin(ac)], axis=1)
    return jnp.asarray(cos, F32), jnp.asarray(sin, F32)


def _route(logits_t):
    g = [logits_t[i:i + 1] for i in range(N_GROUPS)]
    gmax = jnp.maximum(jnp.maximum(g[0], g[1]), jnp.maximum(g[2], g[3]))
    gsel = jnp.where(g[0] == gmax, 0, jnp.where(g[1] == gmax, 1, jnp.where(g[2] == gmax, 2, 3)))
    gsum = (jnp.exp(g[0] - gmax) + jnp.exp(g[1] - gmax)) + (jnp.exp(g[2] - gmax) + jnp.exp(g[3] - gmax))
    gw = 1.0 / gsum
    e = [logits_t[EXP_ROW0 + EXPERTS_PER_GROUP * i:EXP_ROW0 + EXPERTS_PER_GROUP * (i + 1)]
         for i in range(N_GROUPS)]
    el = jnp.where(gsel == 0, e[0], jnp.where(gsel == 1, e[1], jnp.where(gsel == 2, e[2], e[3])))
    row = lax.broadcasted_iota(jnp.int32, el.shape, 0).astype(F32)
    none = float(EXPERTS_PER_GROUP)
    m1 = jnp.max(el, axis=0, keepdims=True)
    i1 = jnp.min(jnp.where(el == m1, row, none), axis=0, keepdims=True)
    el2 = jnp.where(row == i1, -jnp.inf, el)
    m2 = jnp.max(el2, axis=0, keepdims=True)
    i2 = jnp.min(jnp.where(el2 == m2, row, none), axis=0, keepdims=True)
    r = jnp.exp(m2 - m1)
    w1 = gw / (1.0 + r)
    w2 = gw * r / (1.0 + r)
    base = gsel * EXPERTS_PER_GROUP
    ids = jnp.concatenate([base + i1.astype(jnp.int32), base + i2.astype(jnp.int32)], axis=0)
    return ids, jnp.concatenate([w1, w2], axis=0)


def _store_token_tiled(ref, val):
    m = val.shape[0]
    for j in range(FEAT_TILES):
        ref[pl.ds(j, m, stride=FEAT_TILES), :] = val[:, LANES * j:LANES * (j + 1)]


def _load_token_tiled(ref, m):
    return jnp.concatenate([ref[pl.ds(j, m, stride=FEAT_TILES), :] for j in range(FEAT_TILES)], axis=1)


def _merge_body(gf_ref, gb_ref, rf_ref, rb_ref, gg_ref, rg_ref, mg_ref, mr_ref, x_ref,
                g1_ref, sh2_ref, nfs_ref, gn_ref, wg_ref, wr_ref, wo_ref, wrt_ref, brt_ref, upper_ref,
                h_ref, hn_ref, ids_ref, ew_ref, cnt_ref, wg_s, wr_s, wo_s, stage, sem):
    @pl.when(pl.program_id(0) == 0)
    def _():
        cnt_ref[...] = jnp.zeros_like(cnt_ref)
        for w_hbm, w_s in ((wg_ref, wg_s), (wr_ref, wr_s), (wo_ref, wo_s)):
            copy = pltpu.make_async_copy(w_hbm, stage, sem.at[0])
            copy.start()
            copy.wait()
            w_s[...] = stage[...].astype(BF16)

    def rows_to_logits(r0, n):
        rows = pl.ds(r0, n)
        og = (gf_ref[rows, :] + gb_ref[rows, :]).astype(F32)
        orr = (rf_ref[rows, :] + rb_ref[rows, :]).astype(F32)
        gparts, rparts = [], []
        for hh in range(HEADS):
            seg = og[:, DV * hh:DV * (hh + 1)]
            ms = jnp.mean(seg * seg, axis=-1, keepdims=True)
            gparts.append(seg * lax.rsqrt(ms + NORM_EPS))
            seg = orr[:, DV * hh:DV * (hh + 1)]
            mu = jnp.mean(seg, axis=-1, keepdims=True)
            cen = seg - mu
            var = jnp.mean(cen * cen, axis=-1, keepdims=True)
            rparts.append(cen * lax.rsqrt(var + NORM_EPS))
        o_gla = (jnp.concatenate(gparts, axis=1) * gn_ref[...]).astype(BF16) * _silu(gg_ref[rows, :])
        o_ret = jnp.concatenate(rparts, axis=1).astype(BF16) * _silu(rg_ref[rows, :])
        a_gla = _dot(o_gla, wg_s[...])
        a_ret = _dot(o_ret, wr_s[...])
        yield
        y = _sigmoid(mg_ref[rows, :]) * a_gla.astype(BF16) + _sigmoid(mr_ref[rows, :]) * a_ret.astype(BF16)
        out = _dot(y, wo_s[...])
        yield
        h = x_ref[rows, :] + g1_ref[0] * out
        h_ref[rows, :] = h
        ms = jnp.mean(h * h, axis=-1, keepdims=True)
        hn = h * lax.rsqrt(ms + NORM_EPS) * nfs_ref[0] + sh2_ref[0]
        _store_token_tiled(hn_ref.at[pl.ds(r0 * FEAT_TILES, n * FEAT_TILES)], hn)
        hn_hi = hn.astype(BF16)
        hn_lo = (hn - hn_hi.astype(F32)).astype(BF16)
        both_w = _dot(hn_hi, wrt_ref[...])
        lo_w = _dot(hn_lo, wrt_ref[:, :ROUTER_W])
        yield
        return (both_w[:, :ROUTER_W] + both_w[:, ROUTER_W:] + lo_w) + brt_ref[...]

    tm = x_ref.shape[0]
    n = tm // MERGE_SPLIT
    logits = jnp.concatenate(_round_robin([rows_to_logits(g * n, n) for g in range(MERGE_SPLIT)]), axis=0)
    ids, ew = _route(logits.T)

    erow = lax.broadcasted_iota(jnp.int32, (N_EXPERTS, tm), 0)
    oh0 = jnp.where(erow == ids[0:1], 1.0, 0.0)
    oh1 = jnp.where(erow == ids[1:2], 1.0, 0.0)
    both = oh0 + oh1
    before = _dot(both.astype(BF16), upper_ref[...]) + cnt_ref[:, 0:1].astype(F32)
    rank0 = jnp.sum(oh0 * before, axis=0, keepdims=True)
    rank1 = jnp.sum(oh1 * before, axis=0, keepdims=True)
    total = cnt_ref[:, 0:1] + jnp.sum(both, axis=1, keepdims=True).astype(jnp.int32)
    cnt_ref[...] = jnp.broadcast_to(total, cnt_ref.shape)
    ids_ref[...] = jnp.concatenate([ids, rank0.astype(jnp.int32), rank1.astype(jnp.int32),
                                    jnp.zeros((4, tm), jnp.int32)], axis=0)
    ew_ref[...] = jnp.concatenate([ew, jnp.zeros((6, tm), F32)], axis=0)


def _merge(gf, gb, rf, rb, p, x2, g1, sh2, nfs, gn, wg, wr, wo, wrt, brt, seq, tm):
    t = x2.shape[0]
    tiles_per_batch = seq // tm
    row = lambda i: (i, 0)
    mod = lambda i: (i // tiles_per_batch, 0, 0)
    const = lambda i: (0, 0)
    tok = pl.BlockSpec((tm, D_MODEL), row)
    vec = pl.BlockSpec((1, D_MODEL), const)
    modspec = pl.BlockSpec((1, 1, D_MODEL), mod)
    wspec = pl.BlockSpec(memory_space=pl.ANY)
    upper = jnp.asarray(np.triu(np.ones((tm, tm), np.float32), 1), BF16)

    def pcol(col0):
        return pl.BlockSpec((tm, D_MODEL), lambda i: (i, col0 // D_MODEL))

    return pl.pallas_call(
        _merge_body,
        grid=(t // tm,),
        in_specs=[tok, tok, tok, tok, pcol(COL_GG), pcol(COL_RG), pcol(COL_MG), pcol(COL_MR), tok,
                  modspec, modspec, modspec, vec, wspec, wspec, wspec,
                  pl.BlockSpec((D_MODEL, 2 * ROUTER_W), const), pl.BlockSpec((1, ROUTER_W), const),
                  pl.BlockSpec((tm, tm), const)],
        out_specs=[tok, pl.BlockSpec((tm * FEAT_TILES, LANES), row),
                   pl.BlockSpec((8, tm), lambda i: (0, i)), pl.BlockSpec((8, tm), lambda i: (0, i)),
                   pl.BlockSpec((N_EXPERTS, LANES), const)],
        out_shape=[jax.ShapeDtypeStruct((t, D_MODEL), F32), jax.ShapeDtypeStruct((t * FEAT_TILES, LANES), F32),
                   jax.ShapeDtypeStruct((8, t), jnp.int32), jax.ShapeDtypeStruct((8, t), F32),
                   jax.ShapeDtypeStruct((N_EXPERTS, LANES), jnp.int32)],
        scratch_shapes=[pltpu.VMEM((D_MODEL, D_MODEL), BF16)] * 3
                       + [pltpu.VMEM((D_MODEL, D_MODEL), F32), pltpu.SemaphoreType.DMA((1,))],
        compiler_params=_cparams(("arbitrary",)),
        name="merge",
    )(gf, gb, rf, rb, p, p, p, p, x2, g1, sh2, nfs, gn, wg, wr, wo, wrt, brt, upper)


GATHER_UNROLL = 8


def _aligned(tok):
    off = tok * FEAT_TILES
    return off if isinstance(off, int) else pl.multiple_of(off, FEAT_TILES)


def _token_copy(src_hbm, dst, sem, src_tok, dst_tok):
    return pltpu.make_async_copy(src_hbm.at[pl.ds(_aligned(src_tok), FEAT_TILES), :],
                                 dst.at[pl.ds(_aligned(dst_tok), FEAT_TILES), :], sem)


def _wait_tokens(src_hbm, dst, sem):
    pltpu.make_async_copy(src_hbm.at[pl.ds(0, dst.shape[0]), :], dst, sem).wait()


GATHER_BUFS = 3
GATHER_AHEAD = GATHER_BUFS - 1


def _issue_rows(n, start_row, unrolled):
    if unrolled:
        for r in range(n):
            start_row(r, r % 2)
    else:
        def body(g, carry):
            for u in range(GATHER_UNROLL):
                start_row(g * GATHER_UNROLL + u, u % 2)
            return carry
        lax.fori_loop(0, n // GATHER_UNROLL, body, 0)


EXPERT_GROUP = 2


def _expert_body(be_ref, src0_ref, stok_ref, nused_ref, hn_hbm, *refs):
    nw = 3 * EXPERT_GROUP
    w_refs, y_ref, scratch = refs[:nw], refs[nw], refs[nw + 1:]
    bufs, sems, w_s = scratch[:GATHER_BUFS], scratch[GATHER_BUFS], scratch[GATHER_BUFS + 1:]
    i = pl.program_id(0)
    nsteps = (nused_ref[0] + EXPERT_GROUP - 1) // EXPERT_GROUP
    last = stok_ref.shape[0] - 1
    rows = MOE_BLOCK * FEAT_TILES

    def issue(step, s, unrolled):
        bases = [src0_ref[step * EXPERT_GROUP + g] for g in range(EXPERT_GROUP)]

        def start(g, r, priority):
            tok = stok_ref[jnp.minimum(bases[g] + r, last)]
            _token_copy(hn_hbm, bufs[s], sems.at[s], tok, g * MOE_BLOCK + r).start(priority=priority)

        if unrolled:
            _issue_rows(EXPERT_GROUP * MOE_BLOCK, lambda j, p: start(j // MOE_BLOCK, j % MOE_BLOCK, p), True)
        else:
            for g in range(EXPERT_GROUP):
                _issue_rows(MOE_BLOCK, lambda r, p, g=g: start(g, r, p), False)

    @pl.when(i == 0)
    def _():
        for b in range(GATHER_AHEAD):
            issue(b, b, False)

    for g in range(EXPERT_GROUP):
        blk = i * EXPERT_GROUP + g

        @pl.when(jnp.logical_and(i < nsteps, jnp.logical_or(
            i == 0, be_ref[blk] != be_ref[jnp.maximum(blk - EXPERT_GROUP, 0)])))
        def _():
            for k in range(3):
                w_s[3 * g + k][...] = w_refs[3 * g + k][0].astype(BF16)

    def block_phases(cur, g):
        part = pl.ds(g * rows, rows)
        xb = _load_token_tiled(bufs[cur].at[part], MOE_BLOCK).astype(BF16)
        gate = _dot(xb, w_s[3 * g][...])
        up = _dot(xb, w_s[3 * g + 1][...])
        yield
        y = _dot((_silu(gate) * up).astype(BF16), w_s[3 * g + 2][...])
        yield
        _store_token_tiled(y_ref.at[part], y)

    for cur in range(GATHER_BUFS):
        mine = i % GATHER_BUFS == cur

        @pl.when(jnp.logical_and(mine, i < nsteps))
        def _():
            _wait_tokens(hn_hbm, bufs[cur], sems.at[cur])
            issue(i + GATHER_AHEAD, (cur + GATHER_AHEAD) % GATHER_BUFS, True)
            _round_robin([block_phases(cur, g) for g in range(EXPERT_GROUP)])

        @pl.when(jnp.logical_and(mine, jnp.logical_and(i >= nsteps, i < nsteps + GATHER_AHEAD)))
        def _():
            _wait_tokens(hn_hbm, bufs[cur], sems.at[cur])

    @pl.when(i >= nsteps)
    def _():
        y_ref[...] = jnp.zeros_like(y_ref)


def _experts(block_e, src0, stok, nused, hn, w_gate, w_up, w_down, nblk):
    rows = EXPERT_GROUP * MOE_BLOCK * FEAT_TILES
    w_specs, w_args, w_scratch = [], [], []
    for g in range(EXPERT_GROUP):
        wmap = lambda i, be, s0, st, nu, g=g: (be[i * EXPERT_GROUP + g], 0, 0)
        w_specs += [pl.BlockSpec((1, D_MODEL, EXPERT_FF), wmap), pl.BlockSpec((1, D_MODEL, EXPERT_FF), wmap),
                    pl.BlockSpec((1, EXPERT_FF, D_MODEL), wmap)]
        w_args += [w_gate, w_up, w_down]
        w_scratch += [pltpu.VMEM((D_MODEL, EXPERT_FF), BF16), pltpu.VMEM((D_MODEL, EXPERT_FF), BF16),
                      pltpu.VMEM((EXPERT_FF, D_MODEL), BF16)]
    return pl.pallas_call(
        _expert_body,
        grid_spec=pltpu.PrefetchScalarGridSpec(
            num_scalar_prefetch=4,
            grid=(nblk // EXPERT_GROUP,),
            in_specs=[pl.BlockSpec(memory_space=pl.ANY)] + w_specs,
            out_specs=pl.BlockSpec((rows, LANES), lambda i, be, s0, st, nu: (i, 0)),
            scratch_shapes=[pltpu.VMEM((rows, LANES), F32)] * GATHER_BUFS
                           + [pltpu.SemaphoreType.DMA((GATHER_BUFS,))] + w_scratch),
        out_shape=jax.ShapeDtypeStruct((nblk // EXPERT_GROUP * rows, LANES), F32),
        compiler_params=_cparams(("arbitrary",)),
        name="experts",
    )(block_e, src0, stok, nused, hn, *w_args)


def _combine_body(pos_ref, y_hbm, h_ref, ew_ref, g2_ref, nf_ref, o_ref, *scratch):
    i = pl.program_id(0)
    n = pl.num_programs(0)
    tm = COMBINE_TM
    t = pos_ref.shape[0] // TOP_K
    bufs, sems = scratch[:GATHER_BUFS], scratch[GATHER_BUFS]

    def issue(tile, s, unrolled):
        def start_row(j, priority):
            k, r = j % TOP_K, j // TOP_K
            _token_copy(y_hbm, bufs[s].at[k], sems.at[s], pos_ref[k * t + tile * tm + r], r).start(priority=priority)
        _issue_rows(tm * TOP_K, start_row, unrolled)

    def wait(s):
        for k in range(TOP_K):
            _wait_tokens(y_hbm, bufs[s].at[k], sems.at[s])

    @pl.when(i == 0)
    def _():
        for b in range(GATHER_AHEAD):
            issue(b, b, False)

    for cur in range(GATHER_BUFS):
        @pl.when(i % GATHER_BUFS == cur)
        def _():
            wait(cur)
            issue(jnp.minimum(i + GATHER_AHEAD, n - 1), (cur + GATHER_AHEAD) % GATHER_BUFS, True)
            wt = jnp.concatenate([ew_ref[...]] * (LANES // 8), axis=0).T
            moe = (wt[:, 0:1] * _load_token_tiled(bufs[cur].at[0], tm)
                   + wt[:, 1:2] * _load_token_tiled(bufs[cur].at[1], tm))
            h = h_ref[...] + g2_ref[0] * moe
            ms = jnp.mean(h * h, axis=-1, keepdims=True)
            o_ref[...] = h * lax.rsqrt(ms + NORM_EPS) * nf_ref[...]

        @pl.when(jnp.logical_and(i % GATHER_BUFS == cur, i == n - 1))
        def _():
            for ahead in range(1, GATHER_BUFS):
                wait((cur + ahead) % GATHER_BUFS)


def _combine(pos, y_pad, h, ew, g2, nf, seq):
    t = h.shape[0]
    tm = COMBINE_TM
    tiles_per_batch = seq // tm
    ybuf = pltpu.VMEM((TOP_K, tm * FEAT_TILES, LANES), F32)
    return pl.pallas_call(
        _combine_body,
        grid_spec=pltpu.PrefetchScalarGridSpec(
            num_scalar_prefetch=1,
            grid=(t // tm,),
            in_specs=[pl.BlockSpec(memory_space=pl.ANY),
                      pl.BlockSpec((tm, D_MODEL), lambda i, pos: (i, 0)),
                      pl.BlockSpec((8, tm), lambda i, pos: (0, i)),
                      pl.BlockSpec((1, 1, D_MODEL), lambda i, pos: (i // tiles_per_batch, 0, 0)),
                      pl.BlockSpec((1, D_MODEL), lambda i, pos: (0, 0))],
            out_specs=pl.BlockSpec((tm, D_MODEL), lambda i, pos: (i, 0)),
            scratch_shapes=[ybuf] * GATHER_BUFS + [pltpu.SemaphoreType.DMA((GATHER_BUFS,))]),
        out_shape=jax.ShapeDtypeStruct((t, D_MODEL), F32),
        compiler_params=_cparams(("arbitrary",)),
        name="combine",
    )(pos, y_pad, h, ew, g2, nf)


def _dispatch_indices(ids, ranks, counts, t):
    a = t * TOP_K
    nblk = a // MOE_BLOCK + N_EXPERTS + GATHER_AHEAD * EXPERT_GROUP
    experts = jnp.arange(N_EXPERTS, dtype=jnp.int32)
    starts = jnp.cumsum(counts) - counts
    padded = (counts + MOE_BLOCK - 1) // MOE_BLOCK * MOE_BLOCK
    pends = jnp.cumsum(padded)
    pstarts = pends - padded
    block_start = jnp.arange(nblk, dtype=jnp.int32) * MOE_BLOCK
    block_e = jnp.minimum(jnp.sum((block_start[:, None] >= pends[None, :]).astype(jnp.int32), axis=1),
                          N_EXPERTS - 1)
    of_block = (block_e[:, None] == experts[None, :]).astype(jnp.int32)
    src0 = block_start + jnp.sum(of_block * (starts - pstarts)[None, :], axis=1)
    nused = pends[-1:] // MOE_BLOCK
    tok = jnp.arange(t, dtype=jnp.int32)[None, :]
    slot = jnp.arange(TOP_K, dtype=jnp.int32)[:, None]
    key = ids * a + tok * TOP_K + slot
    stok = (jnp.sort(key.reshape(a)) % a) // TOP_K
    ids_d = ids.reshape(a // LANES, LANES)
    pstart_of = jnp.zeros_like(ids_d)
    for e in range(N_EXPERTS):
        pstart_of = jnp.where(ids_d == e, pstarts[e], pstart_of)
    pos = (ranks.reshape(a // LANES, LANES) + pstart_of).reshape(a)
    i32 = lambda v: v.astype(jnp.int32)
    return i32(block_e), i32(src0), i32(stok), i32(nused), i32(pos), nblk


def kernel(x, c, ctx, c_ctx, w_ada, b_ada, norm_mix, norm_ffn, w_in, gla_lr_w, gla_lr_b, gla_norm,
           w_branch_gla, w_branch_ret, w_out, w_router_group, b_router_group, w_router_expert,
           b_router_expert, w_expert_gate, w_expert_up, w_expert_down, norm_final):
    batch, seq, d = x.shape
    ctx_len = ctx.shape[1]
    assert d == D_MODEL and w_ada.shape[0] == 1, "single-layer block with D_MODEL features"
    t = batch * seq

    c8 = jnp.zeros((8, d), F32).at[:batch].set(c).at[batch].set(c_ctx)
    mod = _ada(c8, w_ada[0], b_ada[0][None, :])
    sh1, sc1, g1, sh2, sc2, g2 = [mod[:, d * i:d * (i + 1)] for i in range(6)]
    lat = lambda m: m[:batch, None, :]
    cx = lambda m: m[batch:batch + 1, None, :]

    w_main, w_lr = _wprep(jnp.swapaxes(w_in[0], 0, 1))
    nm = norm_mix[0][None, :]

    p_ctx, lr_ctx = _proj(ctx.reshape(batch * ctx_len, d), cx(sh1), cx(sc1), nm, w_main, w_lr,
                          tm=ctx_len, rows_per_mod=batch * ctx_len)
    x2 = x.reshape(t, d)
    p_lat, lr_lat = _proj(x2, lat(sh1), lat(sc1), nm, w_main, w_lr, tm=PROJ_TM, rows_per_mod=seq)

    wlr_f = jnp.zeros((LANES, HEADS * DK), F32).at[:GLA_RANK].set(gla_lr_w[0, 0])
    wlr_b = jnp.zeros((LANES, HEADS * DK), F32).at[GLA_RANK:2 * GLA_RANK].set(gla_lr_w[0, 1])
    blr_f, blr_b = gla_lr_b[0, 0][None, :], gla_lr_b[0, 1][None, :]
    zero_state = jnp.zeros((batch, HEADS, DK, DV), F32)

    _, _, gs_f, gs_b, _ = _gla(p_ctx, lr_ctx, wlr_f, blr_f, wlr_b, blr_b, zero_state, zero_state,
                               batch, ctx_len, tb=ctx_len)
    gla_args = (p_lat, lr_lat, wlr_f, blr_f, wlr_b, blr_b, gs_f, gs_b, batch, seq)
    gla_f, gla_b, _, _, decay_range = _gla(*gla_args, tb=SCAN_TB)
    gla_f, gla_b = lax.cond(jnp.max(decay_range) > GLA_SAFE_RANGE,
                            lambda: tuple(_gla(*gla_args, tb=GLA_EXACT_TB, exact=True)[:2]),
                            lambda: (gla_f, gla_b))

    ones = jnp.ones((ctx_len, DK), F32)
    _, _, rs_f, rs_b = _ret(p_ctx, ones, jnp.zeros_like(ones), zero_state, zero_state, batch, ctx_len, tb=ctx_len)
    cos, sin = _rope_tables(seq)
    ret_f, ret_b, _, _ = _ret(p_lat, cos, sin, rs_f, rs_b, batch, seq, tb=SCAN_TB)

    wrt = jnp.zeros((d, ROUTER_W), F32)
    wrt = wrt.at[:, :N_GROUPS].set(w_router_group[0]).at[:, EXP_ROW0:EXP_ROW0 + N_EXPERTS].set(w_router_expert[0])
    wrt_hi = wrt.astype(BF16)
    wrt = jnp.concatenate([wrt_hi, (wrt - wrt_hi.astype(F32)).astype(BF16)], axis=1)
    brt = jnp.zeros((1, ROUTER_W), F32)
    brt = brt.at[0, :N_GROUPS].set(b_router_group[0]).at[0, EXP_ROW0:EXP_ROW0 + N_EXPERTS].set(b_router_expert[0])
    h, hn, ids8, ew8, cnt = _merge(gla_f, gla_b, ret_f, ret_b, p_lat, x2, lat(g1), lat(sh2),
                                   norm_ffn[0][None, None, :] * (1.0 + lat(sc2)),
                                   jnp.tile(gla_norm[0], HEADS)[None, :],
                                   w_branch_gla[0], w_branch_ret[0], w_out[0], wrt, brt, seq, tm=MERGE_TM)

    block_e, src0, stok, nused, pos, nblk = _dispatch_indices(ids8[:TOP_K], ids8[TOP_K:2 * TOP_K], cnt[:, 0], t)
    y_pad = _experts(block_e, src0, stok, nused, hn, w_expert_gate[0], w_expert_up[0], w_expert_down[0], nblk)
    out = _combine(pos, y_pad, h, ew8, lat(g2), norm_final[None, :], seq)
    return out.reshape(batch, seq, d)
```

```python
import functools

import numpy as np
import jax
import jax.numpy as jnp
from jax import lax
from jax.experimental import pallas as pl
from jax.experimental.pallas import tpu as pltpu

F32 = jnp.float32
BF16 = jnp.bfloat16

D_MODEL = 1024
GRID_W = 64
HEADS = 4
DK = 128
DV = 256
GLA_RANK = 16
GLA_TAU = 16.0
GLA_CHUNK = 64
GLA_SUB = 16
RET_CHUNK = 128
ROPE_BASE = 10000.0
N_GROUPS = 4
EXPERTS_PER_GROUP = 8
N_EXPERTS = N_GROUPS * EXPERTS_PER_GROUP
TOP_K = 2
EXPERT_FF = 256
MOE_BLOCK = 128
NORM_EPS = 1e-6

COL_GQ, COL_GK, COL_GV, COL_GG = 0, 512, 1024, 2048
COL_RQ, COL_RK, COL_RV, COL_RG = 3072, 3584, 4096, 5120
COL_MG, COL_MR = 6144, 7168
PROJ_W = 8192
LANES = 128
FEAT_TILES = D_MODEL // LANES
ROUTER_W = 128
EXP_ROW0 = 8

VMEM_LIMIT = 56 * 1024 * 1024

PROJ_TM = 1024
PROJ_TN = 2048
SCAN_TB = 1024
GLA_PHASE_LAG = 2
RET_PHASE_LAG = 0
GLA_EXACT_TB = 256
GLA_SAFE_RANGE = 60.0
MERGE_TM = 512
MERGE_SPLIT = 2
COMBINE_TM = 256


def _cparams(sem):
    return pltpu.CompilerParams(dimension_semantics=sem, vmem_limit_bytes=VMEM_LIMIT)


def _sigmoid(x):
    return 1.0 / (1.0 + jnp.exp(-x))


def _silu(x):
    return x * _sigmoid(x)


def _dot(a, b):
    return jnp.dot(a, b, preferred_element_type=F32)


def _dot_nt(a, b):
    return lax.dot_general(a, b, (((1,), (1,)), ((), ())), preferred_element_type=F32)


def _ada_body(c_ref, w_ref, b_ref, o_ref):
    s = _silu(c_ref[...])
    o_ref[...] = _dot(s.astype(BF16), w_ref[...].astype(BF16)) + b_ref[...]


def _ada(c8, w, b):
    n = w.shape[1]
    tn = 1536
    return pl.pallas_call(
        _ada_body,
        grid=(n // tn,),
        in_specs=[pl.BlockSpec((8, D_MODEL), lambda j: (0, 0)),
                  pl.BlockSpec((D_MODEL, tn), lambda j: (0, j)),
                  pl.BlockSpec((1, tn), lambda j: (0, j))],
        out_specs=pl.BlockSpec((8, tn), lambda j: (0, j)),
        out_shape=jax.ShapeDtypeStruct((8, n), F32),
        compiler_params=_cparams(("arbitrary",)),
        name="ada",
    )(c8, w, b)


LR_COL0 = COL_GG + HEADS * DV
WPREP_ROWS = 1024


def _wprep_body(w_ref, wlr_ref, main_ref, lr_ref):
    main_ref[...] = w_ref[...].astype(BF16)
    pad = jnp.zeros((LANES - 2 * GLA_RANK, D_MODEL), F32)
    lr_ref[...] = jnp.concatenate([wlr_ref[...], pad], axis=0).astype(BF16)


def _wprep(wt):
    steps_before = LR_COL0 // WPREP_ROWS

    def src_row(i):
        row = jnp.where(i < steps_before, i * WPREP_ROWS, i * WPREP_ROWS + 2 * GLA_RANK)
        return (pl.multiple_of(row, 2 * GLA_RANK), 0)

    return pl.pallas_call(
        _wprep_body,
        grid=(PROJ_W // WPREP_ROWS,),
        in_specs=[pl.BlockSpec((pl.Element(WPREP_ROWS), pl.Element(D_MODEL)), src_row),
                  pl.BlockSpec((pl.Element(2 * GLA_RANK), pl.Element(D_MODEL)), lambda i: (LR_COL0, 0))],
        out_specs=[pl.BlockSpec((WPREP_ROWS, D_MODEL), lambda i: (i, 0)),
                   pl.BlockSpec((LANES, D_MODEL), lambda i: (0, 0))],
        out_shape=[jax.ShapeDtypeStruct((PROJ_W, D_MODEL), BF16), jax.ShapeDtypeStruct((LANES, D_MODEL), BF16)],
        compiler_params=_cparams(("arbitrary",)),
        name="wprep",
    )(wt, wt)


ROPE_TILE = COL_RQ // PROJ_TN
ROPE_COL0 = COL_RQ % PROJ_TN


def _rope(x, cos, sin):
    lane = lax.broadcasted_iota(jnp.int32, x.shape, 1)
    partner = jnp.where((lane % 64) < 32, pltpu.roll(x, DK - 32, 1), pltpu.roll(x, 32, 1))
    return x * cos + partner * sin


def _proj_body(x_ref, sh_ref, sc_ref, g_ref, w_ref, wlr_ref, cos_ref, sin_ref, o_ref, lr_ref, h_ref):
    j = pl.program_id(1)

    @pl.when(j == 0)
    def _():
        x = x_ref[...]
        ms = jnp.mean(x * x, axis=-1, keepdims=True)
        y = x * lax.rsqrt(ms + NORM_EPS) * g_ref[...]
        hb = (y * (1.0 + sc_ref[0]) + sh_ref[0]).astype(BF16)
        h_ref[...] = hb
        lr_ref[...] = _dot_nt(hb, wlr_ref[...])

    @pl.when(j != ROPE_TILE)
    def _():
        o_ref[...] = _dot_nt(h_ref[...], w_ref[...]).astype(BF16)

    @pl.when(j == ROPE_TILE)
    def _():
        h = h_ref[...]
        cos, sin = cos_ref[...], sin_ref[...]
        rope_end = ROPE_COL0 + 2 * HEADS * DK
        for lo, hi in ((0, ROPE_COL0), (rope_end, PROJ_TN)):
            if lo < hi:
                o_ref[:, lo:hi] = _dot_nt(h, w_ref[lo:hi, :]).astype(BF16)
        for pair in range(HEADS):
            lo = ROPE_COL0 + 2 * DK * pair
            acc = _dot_nt(h, w_ref[lo:lo + 2 * DK, :])
            if pair >= HEADS // 2:
                acc = acc * (DK ** -0.5)
            for half in range(2):
                blk = acc[:, DK * half:DK * (half + 1)]
                o_ref[:, lo + DK * half:lo + DK * (half + 1)] = _rope(blk, cos, sin).astype(BF16)


def _proj(x2, sh, sc, g, w_main, w_lr, cos, sin, tm, rows_per_mod):
    t = x2.shape[0]
    tn = PROJ_TN
    tiles_per_mod = rows_per_mod // tm
    mod_map = lambda i, j: (i // tiles_per_mod, 0, 0)
    pos_tiles = cos.shape[0] // tm
    pos_map = lambda i, j: (i % pos_tiles, 0)
    return pl.pallas_call(
        _proj_body,
        grid=(t // tm, PROJ_W // tn),
        in_specs=[pl.BlockSpec((tm, D_MODEL), lambda i, j: (i, 0)),
                  pl.BlockSpec((1, 1, D_MODEL), mod_map),
                  pl.BlockSpec((1, 1, D_MODEL), mod_map),
                  pl.BlockSpec((1, D_MODEL), lambda i, j: (0, 0)),
                  pl.BlockSpec((tn, D_MODEL), lambda i, j: (j, 0)),
                  pl.BlockSpec((LANES, D_MODEL), lambda i, j: (0, 0)),
                  pl.BlockSpec((tm, DK), pos_map), pl.BlockSpec((tm, DK), pos_map)],
        out_specs=[pl.BlockSpec((tm, tn), lambda i, j: (i, j)),
                   pl.BlockSpec((tm, LANES), lambda i, j: (i, 0))],
        out_shape=[jax.ShapeDtypeStruct((t, PROJ_W), BF16),
                   jax.ShapeDtypeStruct((t, LANES), F32)],
        scratch_shapes=[pltpu.VMEM((tm, D_MODEL), BF16)],
        compiler_params=_cparams(("parallel", "arbitrary")),
        name="proj",
    )(x2, sh, sc, g, w_main, w_lr, cos, sin)


def _gla_consts():
    c, s = GLA_CHUNK, GLA_SUB
    msk = {}
    for rev in (False, True):
        cols = []
        for i in range(c // s):
            keys = np.arange(c)[_key_rows(i, rev)][None, :]
            t = np.arange(c)[:, None]
            visible = (keys > t) if rev else (keys <= t)
            cols.append(np.where(t // s == i, visible, False))
        msk[rev] = np.concatenate(cols, axis=1).astype(np.float32)
    return msk


def _key_rows(i, rev):
    return slice(GLA_SUB * i, GLA_CHUNK) if rev else slice(0, GLA_SUB * (i + 1))


def _log_gate(lr, w, b):
    z = _dot(lr.astype(BF16), w.astype(BF16)) + b
    return -(jnp.maximum(-z, 0.0) + jnp.log(1.0 + jnp.exp(-jnp.abs(z)))) * (1.0 / GLA_TAU)


def _subchunk_scan(la, rev):
    c = la.shape[0]
    pos = lax.broadcasted_iota(jnp.int32, la.shape, 0) % GLA_SUB
    w = la
    step = 1
    while step < GLA_SUB:
        if rev:
            w = w + jnp.where(pos < GLA_SUB - step, pltpu.roll(w, c - step, 0), 0.0)
        else:
            w = w + jnp.where(pos >= step, pltpu.roll(w, step, 0), 0.0)
        step *= 2
    return w


def _gla_intra_exact(q, kf, v, b, rev, scratch):
    q_s, b_s, o_s = scratch
    c = GLA_CHUNK
    q_s[...] = q.astype(F32) * (DK ** -0.5)
    b_s[...] = b
    vf = v.astype(F32)
    row = lax.broadcasted_iota(jnp.int32, (c, 1), 0)

    def body(t, carry):
        qt = q_s[pl.ds(t, 1), :]
        bt = b_s[pl.ds(t, 1), :]
        visible = (row > t) if rev else (row <= t)
        decay = jnp.exp(jnp.where(visible, bt - b, -jnp.inf))
        col = jnp.sum(kf * decay * qt, axis=1, keepdims=True)
        o_s[pl.ds(t, 1), :] = jnp.sum(col * vf, axis=0, keepdims=True)
        return carry

    lax.fori_loop(0, c, body, 0)
    return o_s[...]


def _gla_chunk(q, k, v, la, state_ref, mask, rev, exact_scratch=None):
    c, s = GLA_CHUNK, GLA_SUB
    nsub = c // s
    w = _subchunk_scan(la, rev)
    tot = [w[s * i:s * i + 1] if rev else w[s * (i + 1) - 1:s * (i + 1)] for i in range(nsub)]
    anchors = [None] * nsub
    acc = jnp.zeros_like(tot[0])
    worst = jnp.zeros_like(tot[0])
    for i in (reversed(range(nsub)) if rev else range(nsub)):
        anchors[i] = acc
        acc = acc + tot[i]
        worst = jnp.maximum(worst, -tot[i])
    b_end = acc
    ref = jnp.concatenate([jnp.broadcast_to(a, (s, DK)) for a in anchors], axis=0)
    b = w + ref

    qt = q.astype(F32) * (DK ** -0.5) * jnp.exp(w)
    kf = k.astype(F32)
    if exact_scratch is None:
        kstack = jnp.concatenate([kf[_key_rows(i, rev)] * jnp.exp(anchors[i] - b[_key_rows(i, rev)])
                                  for i in range(nsub)], axis=0).astype(BF16)
        raw = _dot_nt(qt.astype(BF16), kstack)
    kst = kf * jnp.exp(b_end - b)
    xt = jnp.concatenate([kst, jnp.broadcast_to(jnp.exp(b_end), (c, DK))], axis=0).T
    update = _dot(xt[:, :c].astype(BF16), v)
    state = state_ref[...]
    inter = _dot((qt * jnp.exp(ref)).astype(BF16), state.astype(BF16))
    state_ref[...] = xt[:, c:c + 1] * state + update
    yield
    if exact_scratch is None:
        vstack = jnp.concatenate([v[_key_rows(i, rev)] for i in range(nsub)], axis=0)
        intra = _dot(jnp.where(mask != 0.0, raw, 0.0).astype(BF16), vstack)
    else:
        intra = _gla_intra_exact(q, kf, v, b, rev, exact_scratch)
    return inter + intra, worst


def _round_robin(gens):
    results = [None] * len(gens)
    live = list(range(len(gens)))
    while live:
        for g in list(live):
            try:
                next(gens[g])
            except StopIteration as done:
                results[g] = done.value
                live.remove(g)
    return results


def _run_phased(chunks, emit, lag):
    def finish(tag, gen):
        try:
            next(gen)
        except StopIteration as done:
            emit(tag, done.value)
            return
        raise AssertionError("chunk generator has more than two phases")

    pending = []
    for tag, gen in chunks:
        next(gen)
        pending.append((tag, gen))
        if len(pending) > lag:
            finish(*pending.pop(0))
    for item in pending:
        finish(*item)


def _gla_body(qf_ref, kf_ref, vf_ref, lrf_ref, qb_ref, kb_ref, vb_ref, lrb_ref,
              wf_ref, bf_ref, wb_ref, bb_ref, mskf_ref, mskb_ref,
              s0f_ref, s0b_ref, of_ref, ob_ref, sf_ref, sb_ref, rng_ref, *exact_scratch, nchunks):
    @pl.when(pl.program_id(2) == 0)
    def _():
        sf_ref[...] = s0f_ref[...]
        sb_ref[...] = s0b_ref[...]

    c = GLA_CHUNK
    scratch = exact_scratch or None
    la_f = _log_gate(lrf_ref[...], wf_ref[...], bf_ref[...])
    la_b = _log_gate(lrb_ref[...], wb_ref[...], bb_ref[...])
    mskf, mskb = mskf_ref[...], mskb_ref[...]
    st_f, st_b = sf_ref.at[0, 0], sb_ref.at[0, 0]
    worst = [jnp.zeros((1, DK), F32)]

    def chunks():
        for n in range(nchunks):
            rf = slice(c * n, c * (n + 1))
            yield (of_ref, rf), _gla_chunk(qf_ref[rf, :], kf_ref[rf, :], vf_ref[rf, :], la_f[rf, :], st_f, mskf,
                                           False, scratch)
            m = nchunks - 1 - n
            rb = slice(c * m, c * (m + 1))
            yield (ob_ref, rb), _gla_chunk(qb_ref[rb, :], kb_ref[rb, :], vb_ref[rb, :], la_b[rb, :], st_b, mskb,
                                           True, scratch)

    def emit(tag, result):
        o_ref, rows = tag
        o_ref[rows, :] = result[0].astype(BF16)
        worst[0] = jnp.maximum(worst[0], result[1])

    _run_phased(chunks(), emit, GLA_PHASE_LAG)
    rng_ref[0, 0, 0] = jnp.broadcast_to(worst[0], (8, DK))


def _gla(p, lr, wlr_f, blr_f, wlr_b, blr_b, s0f, s0b, batch, seq, tb, exact=False):
    nb = seq // tb
    msk = _gla_consts()
    mskf, mskb = jnp.asarray(msk[False]), jnp.asarray(msk[True])

    def fwd(col0, width):
        return lambda b, h, i: (b * nb + i, col0 // width + h)

    def bwd(col0, width):
        return lambda b, h, i: (b * nb + nb - 1 - i, col0 // width + h)

    const2 = lambda b, h, i: (0, 0)
    headcol = lambda b, h, i: (0, h)
    st_map = lambda b, h, i: (b, h, 0, 0)
    st_spec = pl.BlockSpec((1, 1, DK, DV), st_map)
    st_shape = jax.ShapeDtypeStruct((batch, HEADS, DK, DV), F32)
    o_shape = jax.ShapeDtypeStruct((batch * seq, HEADS * DV), BF16)
    return pl.pallas_call(
        functools.partial(_gla_body, nchunks=tb // GLA_CHUNK),
        grid=(batch, HEADS, nb),
        in_specs=[pl.BlockSpec((tb, DK), fwd(COL_GQ, DK)),
                  pl.BlockSpec((tb, DK), fwd(COL_GK, DK)),
                  pl.BlockSpec((tb, DV), fwd(COL_GV, DV)),
                  pl.BlockSpec((tb, LANES), lambda b, h, i: (b * nb + i, 0)),
                  pl.BlockSpec((tb, DK), bwd(COL_GQ, DK)),
                  pl.BlockSpec((tb, DK), bwd(COL_GK, DK)),
                  pl.BlockSpec((tb, DV), bwd(COL_GV, DV)),
                  pl.BlockSpec((tb, LANES), lambda b, h, i: (b * nb + nb - 1 - i, 0)),
                  pl.BlockSpec((LANES, DK), headcol),
                  pl.BlockSpec((1, DK), headcol),
                  pl.BlockSpec((LANES, DK), headcol),
                  pl.BlockSpec((1, DK), headcol),
                  pl.BlockSpec(mskf.shape, const2),
                  pl.BlockSpec(mskb.shape, const2),
                  st_spec, st_spec],
        out_specs=[pl.BlockSpec((tb, DV), lambda b, h, i: (b * nb + i, h)),
                   pl.BlockSpec((tb, DV), lambda b, h, i: (b * nb + nb - 1 - i, h)),
                   st_spec, st_spec,
                   pl.BlockSpec((1, 1, 1, 8, DK), lambda b, h, i: (b, h, i, 0, 0))],
        out_shape=[o_shape, o_shape, st_shape, st_shape,
                   jax.ShapeDtypeStruct((batch, HEADS, nb, 8, DK), F32)],
        scratch_shapes=([pltpu.VMEM((GLA_CHUNK, DK), F32), pltpu.VMEM((GLA_CHUNK, DK), F32),
                         pltpu.VMEM((GLA_CHUNK, DV), F32)] if exact else []),
        compiler_params=_cparams(("parallel", "parallel", "arbitrary")),
        name="gla_exact" if exact else "gla",
    )(p, p, p, lr, p, p, p, lr, wlr_f, blr_f, wlr_b, blr_b, mskf, mskb, s0f, s0b)


def _ret_consts():
    c = RET_CHUNK
    hh = np.arange(HEADS, dtype=np.float64)
    lg = {False: np.log1p(-np.exp2(-5.0 - hh)), True: np.log1p(-np.exp2(-5.5 - hh))}
    t = np.arange(c, dtype=np.float64)[:, None]
    u = np.arange(c, dtype=np.float64)[None, :]
    pos = np.arange(c, dtype=np.float64)
    out = {}
    for rev in (False, True):
        g = lg[rev][:, None, None]
        if not rev:
            dmat = np.where(u <= t, np.exp((t - u) * g), 0.0)
            qd = np.exp((pos + 1.0)[None, :] * lg[rev][:, None])
            kd = np.exp((c - 1.0 - pos)[None, :] * lg[rev][:, None])
        else:
            dmat = np.where(u > t, np.exp((u - t) * g), 0.0)
            qd = np.exp((c - pos)[None, :] * lg[rev][:, None])
            kd = np.exp(pos[None, :] * lg[rev][:, None])
        cd = np.exp(c * lg[rev])
        out[rev] = (dmat.astype(np.float32),
                    np.repeat(qd[:, :, None], DV, axis=2).astype(np.float32),
                    np.repeat(kd[:, :, None], DK, axis=2).astype(np.float32),
                    np.repeat(cd[:, None, None], DV, axis=2).astype(np.float32))
    return out


def _ret_chunk(q, k, v, state_ref, dmat, qd, kd, cd):
    raw = _dot_nt(q, k)
    update = _dot((k.astype(F32) * kd).T.astype(BF16), v)
    state = state_ref[...]
    inter = _dot(q, state.astype(BF16)) * qd
    state_ref[...] = cd * state + update
    yield
    return inter + _dot((raw * dmat).astype(BF16), v)


def _ret_body(qf_ref, kf_ref, vf_ref, qb_ref, kb_ref, vb_ref,
              dmf_ref, qdf_ref, kdf_ref, cdf_ref, dmb_ref, qdb_ref, kdb_ref, cdb_ref,
              s0f_ref, s0b_ref, of_ref, ob_ref, sf_ref, sb_ref, *, nchunks):
    @pl.when(pl.program_id(2) == 0)
    def _():
        sf_ref[...] = s0f_ref[...]
        sb_ref[...] = s0b_ref[...]

    c = RET_CHUNK
    cf = (dmf_ref[0], qdf_ref[0], kdf_ref[0], cdf_ref[0])
    cb = (dmb_ref[0], qdb_ref[0], kdb_ref[0], cdb_ref[0])
    st_f, st_b = sf_ref.at[0, 0], sb_ref.at[0, 0]
    def chunks():
        for n in range(nchunks):
            rf = slice(c * n, c * (n + 1))
            yield (of_ref, rf), _ret_chunk(qf_ref[rf, :], kf_ref[rf, :], vf_ref[rf, :], st_f, *cf)
            m = nchunks - 1 - n
            rb = slice(c * m, c * (m + 1))
            yield (ob_ref, rb), _ret_chunk(qb_ref[rb, :], kb_ref[rb, :], vb_ref[rb, :], st_b, *cb)

    def emit(tag, out):
        o_ref, rows = tag
        o_ref[rows, :] = out.astype(BF16)

    _run_phased(chunks(), emit, RET_PHASE_LAG)


def _ret(p, s0f, s0b, batch, seq, tb):
    nb = seq // tb
    consts = _ret_consts()

    def fwd(col0, width):
        return lambda b, h, i: (b * nb + i, col0 // width + h)

    def bwd(col0, width):
        return lambda b, h, i: (b * nb + nb - 1 - i, col0 // width + h)

    head3 = lambda b, h, i: (h, 0, 0)
    st_spec = pl.BlockSpec((1, 1, DK, DV), lambda b, h, i: (b, h, 0, 0))
    st_shape = jax.ShapeDtypeStruct((batch, HEADS, DK, DV), F32)
    o_shape = jax.ShapeDtypeStruct((batch * seq, HEADS * DV), BF16)
    const_specs = []
    const_args = []
    for rev in (False, True):
        for a in consts[rev]:
            const_specs.append(pl.BlockSpec((1,) + a.shape[1:], head3))
            const_args.append(jnp.asarray(a))
    return pl.pallas_call(
        functools.partial(_ret_body, nchunks=tb // RET_CHUNK),
        grid=(batch, HEADS, nb),
        in_specs=[pl.BlockSpec((tb, DK), fwd(COL_RQ, DK)),
                  pl.BlockSpec((tb, DK), fwd(COL_RK, DK)),
                  pl.BlockSpec((tb, DV), fwd(COL_RV, DV)),
                  pl.BlockSpec((tb, DK), bwd(COL_RQ, DK)),
                  pl.BlockSpec((tb, DK), bwd(COL_RK, DK)),
                  pl.BlockSpec((tb, DV), bwd(COL_RV, DV))]
                 + const_specs + [st_spec, st_spec],
        out_specs=[pl.BlockSpec((tb, DV), lambda b, h, i: (b * nb + i, h)),
                   pl.BlockSpec((tb, DV), lambda b, h, i: (b * nb + nb - 1 - i, h)),
                   st_spec, st_spec],
        out_shape=[o_shape, o_shape, st_shape, st_shape],
        compiler_params=_cparams(("parallel", "parallel", "arbitrary")),
        name="ret",
    )(p, p, p, p, p, p, *const_args, s0f, s0b)


def _rope_tables(seq):
    n = DK // 4
    inv = np.float32(ROPE_BASE) ** (-np.arange(n, dtype=np.float32) / np.float32(n))
    pos = np.arange(seq)
    ar = (pos // GRID_W).astype(np.float32)[:, None] * inv[None, :]
    ac = (pos % GRID_W).astype(np.float32)[:, None] * inv[None, :]
    cos = np.concatenate([np.cos(ar), np.cos(ar), np.cos(ac), np.cos(ac)], axis=1)
    sin = np.concatenate([-np.sin(ar), np.sin(ar), -np.sin(ac), np.sin(ac)], axis=1)
    return jnp.asarray(cos, F32), jnp.asarray(sin, F32)


def _route(logits_t):
    g = [logits_t[i:i + 1] for i in range(N_GROUPS)]
    gmax = jnp.maximum(jnp.maximum(g[0], g[1]), jnp.maximum(g[2], g[3]))
    gsel = jnp.where(g[0] == gmax, 0, jnp.where(g[1] == gmax, 1, jnp.where(g[2] == gmax, 2, 3)))
    gsum = (jnp.exp(g[0] - gmax) + jnp.exp(g[1] - gmax)) + (jnp.exp(g[2] - gmax) + jnp.exp(g[3] - gmax))
    gw = 1.0 / gsum
    e = [logits_t[EXP_ROW0 + EXPERTS_PER_GROUP * i:EXP_ROW0 + EXPERTS_PER_GROUP * (i + 1)]
         for i in range(N_GROUPS)]
    el = jnp.where(gsel == 0, e[0], jnp.where(gsel == 1, e[1], jnp.where(gsel == 2, e[2], e[3])))
    row = lax.broadcasted_iota(jnp.int32, el.shape, 0).astype(F32)
    none = float(EXPERTS_PER_GROUP)
    m1 = jnp.max(el, axis=0, keepdims=True)
    i1 = jnp.min(jnp.where(el == m1, row, none), axis=0, keepdims=True)
    el2 = jnp.where(row == i1, -jnp.inf, el)
    m2 = jnp.max(el2, axis=0, keepdims=True)
    i2 = jnp.min(jnp.where(el2 == m2, row, none), axis=0, keepdims=True)
    r = jnp.exp(m2 - m1)
    w1 = gw / (1.0 + r)
    w2 = gw * r / (1.0 + r)
    base = gsel * EXPERTS_PER_GROUP
    ids = jnp.concatenate([base + i1.astype(jnp.int32), base + i2.astype(jnp.int32)], axis=0)
    return ids, jnp.concatenate([w1, w2], axis=0)


def _store_token_tiled(ref, val):
    m = val.shape[0]
    for j in range(FEAT_TILES):
        ref[pl.ds(j, m, stride=FEAT_TILES), :] = val[:, LANES * j:LANES * (j + 1)]


def _load_token_tiled(ref, m):
    return jnp.concatenate([ref[pl.ds(j, m, stride=FEAT_TILES), :] for j in range(FEAT_TILES)], axis=1)


def _merge_body(gf_ref, gb_ref, rf_ref, rb_ref, gg_ref, rg_ref, mg_ref, mr_ref, x_ref,
                g1_ref, sh2_ref, nfs_ref, gn_ref, wg_ref, wr_ref, wo_ref, wrt_ref, brt_ref, upper_ref,
                h_ref, hn_ref, ids_ref, ew_ref, cnt_ref, wg_s, wr_s, wo_s, stage, sem):
    @pl.when(pl.program_id(0) == 0)
    def _():
        cnt_ref[...] = jnp.zeros_like(cnt_ref)
        for w_hbm, w_s in ((wg_ref, wg_s), (wr_ref, wr_s), (wo_ref, wo_s)):
            copy = pltpu.make_async_copy(w_hbm, stage, sem.at[0])
            copy.start()
            copy.wait()
            w_s[...] = stage[...].astype(BF16)

    def rows_to_logits(r0, n):
        rows = pl.ds(r0, n)
        og = (gf_ref[rows, :] + gb_ref[rows, :]).astype(F32)
        orr = (rf_ref[rows, :] + rb_ref[rows, :]).astype(F32)
        gparts, rparts = [], []
        for hh in range(HEADS):
            seg = og[:, DV * hh:DV * (hh + 1)]
            ms = jnp.mean(seg * seg, axis=-1, keepdims=True)
            gparts.append(seg * lax.rsqrt(ms + NORM_EPS))
            seg = orr[:, DV * hh:DV * (hh + 1)]
            mu = jnp.mean(seg, axis=-1, keepdims=True)
            cen = seg - mu
            var = jnp.mean(cen * cen, axis=-1, keepdims=True)
            rparts.append(cen * lax.rsqrt(var + NORM_EPS))
        o_gla = (jnp.concatenate(gparts, axis=1) * gn_ref[...]).astype(BF16) * _silu(gg_ref[rows, :])
        o_ret = jnp.concatenate(rparts, axis=1).astype(BF16) * _silu(rg_ref[rows, :])
        a_gla = _dot(o_gla, wg_s[...])
        a_ret = _dot(o_ret, wr_s[...])
        yield
        y = _sigmoid(mg_ref[rows, :]) * a_gla.astype(BF16) + _sigmoid(mr_ref[rows, :]) * a_ret.astype(BF16)
        out = _dot(y, wo_s[...])
        yield
        h = x_ref[rows, :] + g1_ref[0] * out
        h_ref[rows, :] = h
        ms = jnp.mean(h * h, axis=-1, keepdims=True)
        hn = h * lax.rsqrt(ms + NORM_EPS) * nfs_ref[0] + sh2_ref[0]
        _store_token_tiled(hn_ref.at[pl.ds(r0 * FEAT_TILES, n * FEAT_TILES)], hn)
        hn_hi = hn.astype(BF16)
        hn_lo = (hn - hn_hi.astype(F32)).astype(BF16)
        both_w = _dot(hn_hi, wrt_ref[...])
        lo_w = _dot(hn_lo, wrt_ref[:, :ROUTER_W])
        yield
        return (both_w[:, :ROUTER_W] + both_w[:, ROUTER_W:] + lo_w) + brt_ref[...]

    tm = x_ref.shape[0]
    n = tm // MERGE_SPLIT
    logits = jnp.concatenate(_round_robin([rows_to_logits(g * n, n) for g in range(MERGE_SPLIT)]), axis=0)
    ids, ew = _route(logits.T)

    erow = lax.broadcasted_iota(jnp.int32, (N_EXPERTS, tm), 0)
    oh0 = jnp.where(erow == ids[0:1], 1.0, 0.0)
    oh1 = jnp.where(erow == ids[1:2], 1.0, 0.0)
    both = oh0 + oh1
    before = _dot(both.astype(BF16), upper_ref[...]) + cnt_ref[:, 0:1].astype(F32)
    rank0 = jnp.sum(oh0 * before, axis=0, keepdims=True)
    rank1 = jnp.sum(oh1 * before, axis=0, keepdims=True)
    total = cnt_ref[:, 0:1] + jnp.sum(both, axis=1, keepdims=True).astype(jnp.int32)
    cnt_ref[...] = jnp.broadcast_to(total, cnt_ref.shape)
    ids_ref[...] = jnp.concatenate([ids, rank0.astype(jnp.int32), rank1.astype(jnp.int32),
                                    jnp.zeros((4, tm), jnp.int32)], axis=0)
    ew_ref[...] = jnp.concatenate([ew, jnp.zeros((6, tm), F32)], axis=0)


def _merge(gf, gb, rf, rb, p, x2, g1, sh2, nfs, gn, wg, wr, wo, wrt, brt, seq, tm):
    t = x2.shape[0]
    tiles_per_batch = seq // tm
    row = lambda i: (i, 0)
    mod = lambda i: (i // tiles_per_batch, 0, 0)
    const = lambda i: (0, 0)
    tok = pl.BlockSpec((tm, D_MODEL), row)
    vec = pl.BlockSpec((1, D_MODEL), const)
    modspec = pl.BlockSpec((1, 1, D_MODEL), mod)
    wspec = pl.BlockSpec(memory_space=pl.ANY)
    upper = jnp.asarray(np.triu(np.ones((tm, tm), np.float32), 1), BF16)

    def pcol(col0):
        return pl.BlockSpec((tm, D_MODEL), lambda i: (i, col0 // D_MODEL))

    return pl.pallas_call(
        _merge_body,
        grid=(t // tm,),
        in_specs=[tok, tok, tok, tok, pcol(COL_GG), pcol(COL_RG), pcol(COL_MG), pcol(COL_MR), tok,
                  modspec, modspec, modspec, vec, wspec, wspec, wspec,
                  pl.BlockSpec((D_MODEL, 2 * ROUTER_W), const), pl.BlockSpec((1, ROUTER_W), const),
                  pl.BlockSpec((tm, tm), const)],
        out_specs=[tok, pl.BlockSpec((tm * FEAT_TILES, LANES), row),
                   pl.BlockSpec((8, tm), lambda i: (0, i)), pl.BlockSpec((8, tm), lambda i: (0, i)),
                   pl.BlockSpec((N_EXPERTS, LANES), const)],
        out_shape=[jax.ShapeDtypeStruct((t, D_MODEL), F32), jax.ShapeDtypeStruct((t * FEAT_TILES, LANES), F32),
                   jax.ShapeDtypeStruct((8, t), jnp.int32), jax.ShapeDtypeStruct((8, t), F32),
                   jax.ShapeDtypeStruct((N_EXPERTS, LANES), jnp.int32)],
        scratch_shapes=[pltpu.VMEM((D_MODEL, D_MODEL), BF16)] * 3
                       + [pltpu.VMEM((D_MODEL, D_MODEL), F32), pltpu.SemaphoreType.DMA((1,))],
        compiler_params=_cparams(("arbitrary",)),
        name="merge",
    )(gf, gb, rf, rb, p, p, p, p, x2, g1, sh2, nfs, gn, wg, wr, wo, wrt, brt, upper)


GATHER_UNROLL = 8


def _aligned(tok):
    off = tok * FEAT_TILES
    return off if isinstance(off, int) else pl.multiple_of(off, FEAT_TILES)


def _token_copy(src_hbm, dst, sem, src_tok, dst_tok):
    return pltpu.make_async_copy(src_hbm.at[pl.ds(_aligned(src_tok), FEAT_TILES), :],
                                 dst.at[pl.ds(_aligned(dst_tok), FEAT_TILES), :], sem)


def _wait_tokens(src_hbm, dst, sem):
    pltpu.make_async_copy(src_hbm.at[pl.ds(0, dst.shape[0]), :], dst, sem).wait()


GATHER_BUFS = 3
GATHER_AHEAD = GATHER_BUFS - 1


def _issue_rows(n, start_row, unrolled):
    if unrolled:
        for r in range(n):
            start_row(r, r % 2)
    else:
        def body(g, carry):
            for u in range(GATHER_UNROLL):
                start_row(g * GATHER_UNROLL + u, u % 2)
            return carry
        lax.fori_loop(0, n // GATHER_UNROLL, body, 0)


EXPERT_GROUP = 2


def _expert_body(be_ref, src0_ref, stok_ref, nused_ref, hn_hbm, *refs):
    nw = 3 * EXPERT_GROUP
    w_refs, y_ref, scratch = refs[:nw], refs[nw], refs[nw + 1:]
    bufs, sems, w_s = scratch[:GATHER_BUFS], scratch[GATHER_BUFS], scratch[GATHER_BUFS + 1:]
    i = pl.program_id(0)
    nsteps = (nused_ref[0] + EXPERT_GROUP - 1) // EXPERT_GROUP
    last = stok_ref.shape[0] - 1
    rows = MOE_BLOCK * FEAT_TILES

    def issue(step, s, unrolled):
        bases = [src0_ref[step * EXPERT_GROUP + g] for g in range(EXPERT_GROUP)]

        def start(g, r, priority):
            tok = stok_ref[jnp.minimum(bases[g] + r, last)]
            _token_copy(hn_hbm, bufs[s], sems.at[s], tok, g * MOE_BLOCK + r).start(priority=priority)

        if unrolled:
            _issue_rows(EXPERT_GROUP * MOE_BLOCK, lambda j, p: start(j // MOE_BLOCK, j % MOE_BLOCK, p), True)
        else:
            for g in range(EXPERT_GROUP):
                _issue_rows(MOE_BLOCK, lambda r, p, g=g: start(g, r, p), False)

    @pl.when(i == 0)
    def _():
        for b in range(GATHER_AHEAD):
            issue(b, b, False)

    for g in range(EXPERT_GROUP):
        blk = i * EXPERT_GROUP + g

        @pl.when(jnp.logical_and(i < nsteps, jnp.logical_or(
            i == 0, be_ref[blk] != be_ref[jnp.maximum(blk - EXPERT_GROUP, 0)])))
        def _():
            for k in range(3):
                w_s[3 * g + k][...] = w_refs[3 * g + k][0].astype(BF16)

    def block_phases(cur, g):
        part = pl.ds(g * rows, rows)
        xb = _load_token_tiled(bufs[cur].at[part], MOE_BLOCK).astype(BF16)
        gate = _dot(xb, w_s[3 * g][...])
        up = _dot(xb, w_s[3 * g + 1][...])
        yield
        y = _dot((_silu(gate) * up).astype(BF16), w_s[3 * g + 2][...])
        yield
        _store_token_tiled(y_ref.at[part], y)

    for cur in range(GATHER_BUFS):
        mine = i % GATHER_BUFS == cur

        @pl.when(jnp.logical_and(mine, i < nsteps))
        def _():
            _wait_tokens(hn_hbm, bufs[cur], sems.at[cur])
            issue(i + GATHER_AHEAD, (cur + GATHER_AHEAD) % GATHER_BUFS, True)
            _round_robin([block_phases(cur, g) for g in range(EXPERT_GROUP)])

        @pl.when(jnp.logical_and(mine, jnp.logical_and(i >= nsteps, i < nsteps + GATHER_AHEAD)))
        def _():
            _wait_tokens(hn_hbm, bufs[cur], sems.at[cur])

    @pl.when(i >= nsteps)
    def _():
        y_ref[...] = jnp.zeros_like(y_ref)


def _experts(block_e, src0, stok, nused, hn, w_gate, w_up, w_down, nblk):
    rows = EXPERT_GROUP * MOE_BLOCK * FEAT_TILES
    w_specs, w_args, w_scratch = [], [], []
    for g in range(EXPERT_GROUP):
        wmap = lambda i, be, s0, st, nu, g=g: (be[i * EXPERT_GROUP + g], 0, 0)
        w_specs += [pl.BlockSpec((1, D_MODEL, EXPERT_FF), wmap), pl.BlockSpec((1, D_MODEL, EXPERT_FF), wmap),
                    pl.BlockSpec((1, EXPERT_FF, D_MODEL), wmap)]
        w_args += [w_gate, w_up, w_down]
        w_scratch += [pltpu.VMEM((D_MODEL, EXPERT_FF), BF16), pltpu.VMEM((D_MODEL, EXPERT_FF), BF16),
                      pltpu.VMEM((EXPERT_FF, D_MODEL), BF16)]
    return pl.pallas_call(
        _expert_body,
        grid_spec=pltpu.PrefetchScalarGridSpec(
            num_scalar_prefetch=4,
            grid=(nblk // EXPERT_GROUP,),
            in_specs=[pl.BlockSpec(memory_space=pl.ANY)] + w_specs,
            out_specs=pl.BlockSpec((rows, LANES), lambda i, be, s0, st, nu: (i, 0)),
            scratch_shapes=[pltpu.VMEM((rows, LANES), F32)] * GATHER_BUFS
                           + [pltpu.SemaphoreType.DMA((GATHER_BUFS,))] + w_scratch),
        out_shape=jax.ShapeDtypeStruct((nblk // EXPERT_GROUP * rows, LANES), F32),
        compiler_params=_cparams(("arbitrary",)),
        name="experts",
    )(block_e, src0, stok, nused, hn, *w_args)


def _combine_body(pos_ref, y_hbm, h_ref, ew_ref, g2_ref, nf_ref, o_ref, *scratch):
    i = pl.program_id(0)
    n = pl.num_programs(0)
    tm = COMBINE_TM
    t = pos_ref.shape[0] // TOP_K
    bufs, sems = scratch[:GATHER_BUFS], scratch[GATHER_BUFS]

    def issue(tile, s, unrolled):
        def start_row(j, priority):
            k, r = j % TOP_K, j // TOP_K
            _token_copy(y_hbm, bufs[s].at[k], sems.at[s], pos_ref[k * t + tile * tm + r], r).start(priority=priority)
        _issue_rows(tm * TOP_K, start_row, unrolled)

    def wait(s):
        for k in range(TOP_K):
            _wait_tokens(y_hbm, bufs[s].at[k], sems.at[s])

    @pl.when(i == 0)
    def _():
        for b in range(GATHER_AHEAD):
            issue(b, b, False)

    for cur in range(GATHER_BUFS):
        @pl.when(i % GATHER_BUFS == cur)
        def _():
            wait(cur)
            issue(jnp.minimum(i + GATHER_AHEAD, n - 1), (cur + GATHER_AHEAD) % GATHER_BUFS, True)
            wt = jnp.concatenate([ew_ref[...]] * (LANES // 8), axis=0).T
            moe = (wt[:, 0:1] * _load_token_tiled(bufs[cur].at[0], tm)
                   + wt[:, 1:2] * _load_token_tiled(bufs[cur].at[1], tm))
            h = h_ref[...] + g2_ref[0] * moe
            ms = jnp.mean(h * h, axis=-1, keepdims=True)
            o_ref[...] = h * lax.rsqrt(ms + NORM_EPS) * nf_ref[...]

        @pl.when(jnp.logical_and(i % GATHER_BUFS == cur, i == n - 1))
        def _():
            for ahead in range(1, GATHER_BUFS):
                wait((cur + ahead) % GATHER_BUFS)


def _combine(pos, y_pad, h, ew, g2, nf, seq):
    t = h.shape[0]
    tm = COMBINE_TM
    tiles_per_batch = seq // tm
    ybuf = pltpu.VMEM((TOP_K, tm * FEAT_TILES, LANES), F32)
    return pl.pallas_call(
        _combine_body,
        grid_spec=pltpu.PrefetchScalarGridSpec(
            num_scalar_prefetch=1,
            grid=(t // tm,),
            in_specs=[pl.BlockSpec(memory_space=pl.ANY),
                      pl.BlockSpec((tm, D_MODEL), lambda i, pos: (i, 0)),
                      pl.BlockSpec((8, tm), lambda i, pos: (0, i)),
                      pl.BlockSpec((1, 1, D_MODEL), lambda i, pos: (i // tiles_per_batch, 0, 0)),
                      pl.BlockSpec((1, D_MODEL), lambda i, pos: (0, 0))],
            out_specs=pl.BlockSpec((tm, D_MODEL), lambda i, pos: (i, 0)),
            scratch_shapes=[ybuf] * GATHER_BUFS + [pltpu.SemaphoreType.DMA((GATHER_BUFS,))]),
        out_shape=jax.ShapeDtypeStruct((t, D_MODEL), F32),
        compiler_params=_cparams(("arbitrary",)),
        name="combine",
    )(pos, y_pad, h, ew, g2, nf)


def _dispatch_indices(ids, ranks, counts, t):
    a = t * TOP_K
    nblk = a // MOE_BLOCK + N_EXPERTS + GATHER_AHEAD * EXPERT_GROUP
    experts = jnp.arange(N_EXPERTS, dtype=jnp.int32)
    starts = jnp.cumsum(counts) - counts
    padded = (counts + MOE_BLOCK - 1) // MOE_BLOCK * MOE_BLOCK
    pends = jnp.cumsum(padded)
    pstarts = pends - padded
    block_start = jnp.arange(nblk, dtype=jnp.int32) * MOE_BLOCK
    block_e = jnp.minimum(jnp.sum((block_start[:, None] >= pends[None, :]).astype(jnp.int32), axis=1),
                          N_EXPERTS - 1)
    of_block = (block_e[:, None] == experts[None, :]).astype(jnp.int32)
    src0 = block_start + jnp.sum(of_block * (starts - pstarts)[None, :], axis=1)
    nused = pends[-1:] // MOE_BLOCK
    tok = jnp.arange(t, dtype=jnp.int32)[None, :]
    slot = jnp.arange(TOP_K, dtype=jnp.int32)[:, None]
    key = ids * a + tok * TOP_K + slot
    stok = (jnp.sort(key.reshape(a)) % a) // TOP_K
    ids_d = ids.reshape(a // LANES, LANES)
    pstart_of = jnp.zeros_like(ids_d)
    for e in range(N_EXPERTS):
        pstart_of = jnp.where(ids_d == e, pstarts[e], pstart_of)
    pos = (ranks.reshape(a // LANES, LANES) + pstart_of).reshape(a)
    i32 = lambda v: v.astype(jnp.int32)
    return i32(block_e), i32(src0), i32(stok), i32(nused), i32(pos), nblk


def kernel(x, c, ctx, c_ctx, w_ada, b_ada, norm_mix, norm_ffn, w_in, gla_lr_w, gla_lr_b, gla_norm,
           w_branch_gla, w_branch_ret, w_out, w_router_group, b_router_group, w_router_expert,
           b_router_expert, w_expert_gate, w_expert_up, w_expert_down, norm_final):
    batch, seq, d = x.shape
    ctx_len = ctx.shape[1]
    assert d == D_MODEL and w_ada.shape[0] == 1, "single-layer block with D_MODEL features"
    t = batch * seq

    c8 = jnp.zeros((8, d), F32).at[:batch].set(c).at[batch].set(c_ctx)
    mod = _ada(c8, w_ada[0], b_ada[0][None, :])
    sh1, sc1, g1, sh2, sc2, g2 = [mod[:, d * i:d * (i + 1)] for i in range(6)]
    lat = lambda m: m[:batch, None, :]
    cx = lambda m: m[batch:batch + 1, None, :]

    w_main, w_lr = _wprep(jnp.swapaxes(w_in[0], 0, 1))
    nm = norm_mix[0][None, :]

    ones = jnp.ones((ctx_len, DK), F32)
    p_ctx, lr_ctx = _proj(ctx.reshape(batch * ctx_len, d), cx(sh1), cx(sc1), nm, w_main, w_lr,
                          ones, jnp.zeros_like(ones), tm=ctx_len, rows_per_mod=batch * ctx_len)
    x2 = x.reshape(t, d)
    cos, sin = _rope_tables(seq)
    p_lat, lr_lat = _proj(x2, lat(sh1), lat(sc1), nm, w_main, w_lr, cos, sin, tm=PROJ_TM, rows_per_mod=seq)

    wlr_f = jnp.zeros((LANES, HEADS * DK), F32).at[:GLA_RANK].set(gla_lr_w[0, 0])
    wlr_b = jnp.zeros((LANES, HEADS * DK), F32).at[GLA_RANK:2 * GLA_RANK].set(gla_lr_w[0, 1])
    blr_f, blr_b = gla_lr_b[0, 0][None, :], gla_lr_b[0, 1][None, :]
    zero_state = jnp.zeros((batch, HEADS, DK, DV), F32)

    _, _, gs_f, gs_b, _ = _gla(p_ctx, lr_ctx, wlr_f, blr_f, wlr_b, blr_b, zero_state, zero_state,
                               batch, ctx_len, tb=ctx_len)
    gla_args = (p_lat, lr_lat, wlr_f, blr_f, wlr_b, blr_b, gs_f, gs_b, batch, seq)
    gla_f, gla_b, _, _, decay_range = _gla(*gla_args, tb=SCAN_TB)
    gla_f, gla_b = lax.cond(jnp.max(decay_range) > GLA_SAFE_RANGE,
                            lambda: tuple(_gla(*gla_args, tb=GLA_EXACT_TB, exact=True)[:2]),
                            lambda: (gla_f, gla_b))

    _, _, rs_f, rs_b = _ret(p_ctx, zero_state, zero_state, batch, ctx_len, tb=ctx_len)
    ret_f, ret_b, _, _ = _ret(p_lat, rs_f, rs_b, batch, seq, tb=SCAN_TB)

    wrt = jnp.zeros((d, ROUTER_W), F32)
    wrt = wrt.at[:, :N_GROUPS].set(w_router_group[0]).at[:, EXP_ROW0:EXP_ROW0 + N_EXPERTS].set(w_router_expert[0])
    wrt_hi = wrt.astype(BF16)
    wrt = jnp.concatenate([wrt_hi, (wrt - wrt_hi.astype(F32)).astype(BF16)], axis=1)
    brt = jnp.zeros((1, ROUTER_W), F32)
    brt = brt.at[0, :N_GROUPS].set(b_router_group[0]).at[0, EXP_ROW0:EXP_ROW0 + N_EXPERTS].set(b_router_expert[0])
    h, hn, ids8, ew8, cnt = _merge(gla_f, gla_b, ret_f, ret_b, p_lat, x2, lat(g1), lat(sh2),
                                   norm_ffn[0][None, None, :] * (1.0 + lat(sc2)),
                                   jnp.tile(gla_norm[0], HEADS)[None, :],
                                   w_branch_gla[0], w_branch_ret[0], w_out[0], wrt, brt, seq, tm=MERGE_TM)

    block_e, src0, stok, nused, pos, nblk = _dispatch_indices(ids8[:TOP_K], ids8[TOP_K:2 * TOP_K], cnt[:, 0], t)
    y_pad = _experts(block_e, src0, stok, nused, hn, w_expert_gate[0], w_expert_up[0], w_expert_down[0], nblk)
    out = _combine(pos, y_pad, h, ew8, lat(g2), norm_final[None, :], seq)
    return out.reshape(batch, seq, d)
```

```python
import functools

import numpy as np
import jax
import jax.numpy as jnp
from jax import lax
from jax.experimental import pallas as pl
from jax.experimental.pallas import tpu as pltpu

F32 = jnp.float32
BF16 = jnp.bfloat16

D_MODEL = 1024
GRID_W = 64
HEADS = 4
DK = 128
DV = 256
GLA_RANK = 16
GLA_TAU = 16.0
GLA_CHUNK = 64
GLA_SUB = 16
RET_CHUNK = 128
ROPE_BASE = 10000.0
N_GROUPS = 4
EXPERTS_PER_GROUP = 8
N_EXPERTS = N_GROUPS * EXPERTS_PER_GROUP
TOP_K = 2
EXPERT_FF = 256
MOE_BLOCK = 128
NORM_EPS = 1e-6

COL_GQ, COL_GK, COL_GV, COL_GG = 0, 512, 1024, 2048
COL_RQ, COL_RK, COL_RV, COL_RG = 3072, 3584, 4096, 5120
COL_MG, COL_MR = 6144, 7168
PROJ_W = 8192
LANES = 128
FEAT_TILES = D_MODEL // LANES
ROUTER_W = 128
EXP_ROW0 = 8

VMEM_LIMIT = 56 * 1024 * 1024

PROJ_TM = 1024
PROJ_TN = 2048
SCAN_TB = 1024
GLA_PHASE_LAG = 2
RET_PHASE_LAG = 0
GLA_EXACT_TB = 256
GLA_SAFE_RANGE = 60.0
MERGE_TM = 512
MERGE_SPLIT = 2
COMBINE_TM = 256


def _cparams(sem):
    return pltpu.CompilerParams(dimension_semantics=sem, vmem_limit_bytes=VMEM_LIMIT)


def _sigmoid(x):
    return 1.0 / (1.0 + jnp.exp(-x))


def _silu(x):
    return x * _sigmoid(x)


def _dot(a, b):
    return jnp.dot(a, b, preferred_element_type=F32)


def _dot_nt(a, b):
    return lax.dot_general(a, b, (((1,), (1,)), ((), ())), preferred_element_type=F32)


def _ada_body(c_ref, w_ref, b_ref, o_ref):
    s = _silu(c_ref[...])
    o_ref[...] = _dot(s.astype(BF16), w_ref[...].astype(BF16)) + b_ref[...]


def _ada(c8, w, b):
    n = w.shape[1]
    tn = 1536
    return pl.pallas_call(
        _ada_body,
        grid=(n // tn,),
        in_specs=[pl.BlockSpec((8, D_MODEL), lambda j: (0, 0)),
                  pl.BlockSpec((D_MODEL, tn), lambda j: (0, j)),
                  pl.BlockSpec((1, tn), lambda j: (0, j))],
        out_specs=pl.BlockSpec((8, tn), lambda j: (0, j)),
        out_shape=jax.ShapeDtypeStruct((8, n), F32),
        compiler_params=_cparams(("arbitrary",)),
        name="ada",
    )(c8, w, b)


LR_COL0 = COL_GG + HEADS * DV
WPREP_ROWS = 1024


def _wprep_body(w_ref, wlr_ref, main_ref, lr_ref):
    main_ref[...] = w_ref[...].astype(BF16)
    pad = jnp.zeros((LANES - 2 * GLA_RANK, D_MODEL), F32)
    lr_ref[...] = jnp.concatenate([wlr_ref[...], pad], axis=0).astype(BF16)


def _wprep(wt):
    steps_before = LR_COL0 // WPREP_ROWS

    def src_row(i):
        row = jnp.where(i < steps_before, i * WPREP_ROWS, i * WPREP_ROWS + 2 * GLA_RANK)
        return (pl.multiple_of(row, 2 * GLA_RANK), 0)

    return pl.pallas_call(
        _wprep_body,
        grid=(PROJ_W // WPREP_ROWS,),
        in_specs=[pl.BlockSpec((pl.Element(WPREP_ROWS), pl.Element(D_MODEL)), src_row),
                  pl.BlockSpec((pl.Element(2 * GLA_RANK), pl.Element(D_MODEL)), lambda i: (LR_COL0, 0))],
        out_specs=[pl.BlockSpec((WPREP_ROWS, D_MODEL), lambda i: (i, 0)),
                   pl.BlockSpec((LANES, D_MODEL), lambda i: (0, 0))],
        out_shape=[jax.ShapeDtypeStruct((PROJ_W, D_MODEL), BF16), jax.ShapeDtypeStruct((LANES, D_MODEL), BF16)],
        compiler_params=_cparams(("arbitrary",)),
        name="wprep",
    )(wt, wt)


ROPE_TILE = COL_RQ // PROJ_TN
ROPE_COL0 = COL_RQ % PROJ_TN


def _rope(x, cos, sin):
    lane = lax.broadcasted_iota(jnp.int32, x.shape, 1)
    partner = jnp.where((lane % 64) < 32, pltpu.roll(x, DK - 32, 1), pltpu.roll(x, 32, 1))
    return x * cos + partner * sin


def _proj_body(x_ref, sh_ref, sc_ref, g_ref, w_ref, wlr_ref, cos_ref, sin_ref, o_ref, lr_ref, h_ref):
    j = pl.program_id(1)

    @pl.when(j == 0)
    def _():
        x = x_ref[...]
        ms = jnp.mean(x * x, axis=-1, keepdims=True)
        y = x * lax.rsqrt(ms + NORM_EPS) * g_ref[...]
        hb = (y * (1.0 + sc_ref[0]) + sh_ref[0]).astype(BF16)
        h_ref[...] = hb
        lr_ref[...] = _dot_nt(hb, wlr_ref[...])

    @pl.when(j != ROPE_TILE)
    def _():
        o_ref[...] = _dot_nt(h_ref[...], w_ref[...]).astype(BF16)

    @pl.when(j == ROPE_TILE)
    def _():
        h = h_ref[...]
        cos, sin = cos_ref[...], sin_ref[...]
        for pair in range(HEADS):
            lo = ROPE_COL0 + 2 * DK * pair
            acc = _dot_nt(h, w_ref[lo:lo + 2 * DK, :])
            if pair >= HEADS // 2:
                acc = acc * (DK ** -0.5)
            for half in range(2):
                blk = acc[:, DK * half:DK * (half + 1)]
                o_ref[:, lo + DK * half:lo + DK * (half + 1)] = _rope(blk, cos, sin).astype(BF16)
        rope_end = ROPE_COL0 + 2 * HEADS * DK
        for lo, hi in ((0, ROPE_COL0), (rope_end, PROJ_TN)):
            if lo < hi:
                o_ref[:, lo:hi] = _dot_nt(h, w_ref[lo:hi, :]).astype(BF16)


def _proj(x2, sh, sc, g, w_main, w_lr, cos, sin, tm, rows_per_mod):
    t = x2.shape[0]
    tn = PROJ_TN
    tiles_per_mod = rows_per_mod // tm
    mod_map = lambda i, j: (i // tiles_per_mod, 0, 0)
    pos_tiles = cos.shape[0] // tm
    pos_map = lambda i, j: (i % pos_tiles, 0)
    return pl.pallas_call(
        _proj_body,
        grid=(t // tm, PROJ_W // tn),
        in_specs=[pl.BlockSpec((tm, D_MODEL), lambda i, j: (i, 0)),
                  pl.BlockSpec((1, 1, D_MODEL), mod_map),
                  pl.BlockSpec((1, 1, D_MODEL), mod_map),
                  pl.BlockSpec((1, D_MODEL), lambda i, j: (0, 0)),
                  pl.BlockSpec((tn, D_MODEL), lambda i, j: (j, 0)),
                  pl.BlockSpec((LANES, D_MODEL), lambda i, j: (0, 0)),
                  pl.BlockSpec((tm, DK), pos_map), pl.BlockSpec((tm, DK), pos_map)],
        out_specs=[pl.BlockSpec((tm, tn), lambda i, j: (i, j)),
                   pl.BlockSpec((tm, LANES), lambda i, j: (i, 0))],
        out_shape=[jax.ShapeDtypeStruct((t, PROJ_W), BF16),
                   jax.ShapeDtypeStruct((t, LANES), F32)],
        scratch_shapes=[pltpu.VMEM((tm, D_MODEL), BF16)],
        compiler_params=_cparams(("parallel", "arbitrary")),
        name="proj",
    )(x2, sh, sc, g, w_main, w_lr, cos, sin)


def _gla_consts():
    c, s = GLA_CHUNK, GLA_SUB
    msk = {}
    for rev in (False, True):
        cols = []
        for i in range(c // s):
            keys = np.arange(c)[_key_rows(i, rev)][None, :]
            t = np.arange(c)[:, None]
            visible = (keys > t) if rev else (keys <= t)
            cols.append(np.where(t // s == i, visible, False))
        msk[rev] = np.concatenate(cols, axis=1).astype(np.float32)
    return msk


def _key_rows(i, rev):
    return slice(GLA_SUB * i, GLA_CHUNK) if rev else slice(0, GLA_SUB * (i + 1))


def _log_gate(lr, w, b):
    z = _dot(lr.astype(BF16), w.astype(BF16)) + b
    return -(jnp.maximum(-z, 0.0) + jnp.log(1.0 + jnp.exp(-jnp.abs(z)))) * (1.0 / GLA_TAU)


def _subchunk_scan(la, rev):
    c = la.shape[0]
    pos = lax.broadcasted_iota(jnp.int32, la.shape, 0) % GLA_SUB
    w = la
    step = 1
    while step < GLA_SUB:
        if rev:
            w = w + jnp.where(pos < GLA_SUB - step, pltpu.roll(w, c - step, 0), 0.0)
        else:
            w = w + jnp.where(pos >= step, pltpu.roll(w, step, 0), 0.0)
        step *= 2
    return w


def _gla_intra_exact(q, kf, v, b, rev, scratch):
    q_s, b_s, o_s = scratch
    c = GLA_CHUNK
    q_s[...] = q.astype(F32) * (DK ** -0.5)
    b_s[...] = b
    vf = v.astype(F32)
    row = lax.broadcasted_iota(jnp.int32, (c, 1), 0)

    def body(t, carry):
        qt = q_s[pl.ds(t, 1), :]
        bt = b_s[pl.ds(t, 1), :]
        visible = (row > t) if rev else (row <= t)
        decay = jnp.exp(jnp.where(visible, bt - b, -jnp.inf))
        col = jnp.sum(kf * decay * qt, axis=1, keepdims=True)
        o_s[pl.ds(t, 1), :] = jnp.sum(col * vf, axis=0, keepdims=True)
        return carry

    lax.fori_loop(0, c, body, 0)
    return o_s[...]


def _gla_chunk(q, k, v, la, state_ref, mask, rev, exact_scratch=None):
    c, s = GLA_CHUNK, GLA_SUB
    nsub = c // s
    w = _subchunk_scan(la, rev)
    tot = [w[s * i:s * i + 1] if rev else w[s * (i + 1) - 1:s * (i + 1)] for i in range(nsub)]
    anchors = [None] * nsub
    acc = jnp.zeros_like(tot[0])
    worst = jnp.zeros_like(tot[0])
    for i in (reversed(range(nsub)) if rev else range(nsub)):
        anchors[i] = acc
        acc = acc + tot[i]
        worst = jnp.maximum(worst, -tot[i])
    b_end = acc
    ref = jnp.concatenate([jnp.broadcast_to(a, (s, DK)) for a in anchors], axis=0)
    b = w + ref

    qt = q.astype(F32) * (DK ** -0.5) * jnp.exp(w)
    kf = k.astype(F32)
    if exact_scratch is None:
        kstack = jnp.concatenate([kf[_key_rows(i, rev)] * jnp.exp(anchors[i] - b[_key_rows(i, rev)])
                                  for i in range(nsub)], axis=0).astype(BF16)
        raw = _dot_nt(qt.astype(BF16), kstack)
    kst = kf * jnp.exp(b_end - b)
    xt = jnp.concatenate([kst, jnp.broadcast_to(jnp.exp(b_end), (c, DK))], axis=0).T
    update = _dot(xt[:, :c].astype(BF16), v)
    state = state_ref[...]
    inter = _dot((qt * jnp.exp(ref)).astype(BF16), state.astype(BF16))
    state_ref[...] = xt[:, c:c + 1] * state + update
    yield
    if exact_scratch is None:
        vstack = jnp.concatenate([v[_key_rows(i, rev)] for i in range(nsub)], axis=0)
        intra = _dot(jnp.where(mask != 0.0, raw, 0.0).astype(BF16), vstack)
    else:
        intra = _gla_intra_exact(q, kf, v, b, rev, exact_scratch)
    return inter + intra, worst


def _round_robin(gens):
    results = [None] * len(gens)
    live = list(range(len(gens)))
    while live:
        for g in list(live):
            try:
                next(gens[g])
            except StopIteration as done:
                results[g] = done.value
                live.remove(g)
    return results


def _run_phased(chunks, emit, lag):
    def finish(tag, gen):
        try:
            next(gen)
        except StopIteration as done:
            emit(tag, done.value)
            return
        raise AssertionError("chunk generator has more than two phases")

    pending = []
    for tag, gen in chunks:
        next(gen)
        pending.append((tag, gen))
        if len(pending) > lag:
            finish(*pending.pop(0))
    for item in pending:
        finish(*item)


def _gla_body(qf_ref, kf_ref, vf_ref, lrf_ref, qb_ref, kb_ref, vb_ref, lrb_ref,
              wf_ref, bf_ref, wb_ref, bb_ref, mskf_ref, mskb_ref,
              s0f_ref, s0b_ref, of_ref, ob_ref, sf_ref, sb_ref, rng_ref, *exact_scratch, nchunks):
    @pl.when(pl.program_id(2) == 0)
    def _():
        sf_ref[...] = s0f_ref[...]
        sb_ref[...] = s0b_ref[...]

    c = GLA_CHUNK
    scratch = exact_scratch or None
    la_f = _log_gate(lrf_ref[...], wf_ref[...], bf_ref[...])
    la_b = _log_gate(lrb_ref[...], wb_ref[...], bb_ref[...])
    mskf, mskb = mskf_ref[...], mskb_ref[...]
    st_f, st_b = sf_ref.at[0, 0], sb_ref.at[0, 0]
    worst = [jnp.zeros((1, DK), F32)]

    def chunks():
        for n in range(nchunks):
            rf = slice(c * n, c * (n + 1))
            yield (of_ref, rf), _gla_chunk(qf_ref[rf, :], kf_ref[rf, :], vf_ref[rf, :], la_f[rf, :], st_f, mskf,
                                           False, scratch)
            m = nchunks - 1 - n
            rb = slice(c * m, c * (m + 1))
            yield (ob_ref, rb), _gla_chunk(qb_ref[rb, :], kb_ref[rb, :], vb_ref[rb, :], la_b[rb, :], st_b, mskb,
                                           True, scratch)

    def emit(tag, result):
        o_ref, rows = tag
        o_ref[rows, :] = result[0].astype(BF16)
        worst[0] = jnp.maximum(worst[0], result[1])

    _run_phased(chunks(), emit, GLA_PHASE_LAG)
    rng_ref[0, 0, 0] = jnp.broadcast_to(worst[0], (8, DK))


def _gla(p, lr, wlr_f, blr_f, wlr_b, blr_b, s0f, s0b, batch, seq, tb, exact=False):
    nb = seq // tb
    msk = _gla_consts()
    mskf, mskb = jnp.asarray(msk[False]), jnp.asarray(msk[True])

    def fwd(col0, width):
        return lambda b, h, i: (b * nb + i, col0 // width + h)

    def bwd(col0, width):
        return lambda b, h, i: (b * nb + nb - 1 - i, col0 // width + h)

    const2 = lambda b, h, i: (0, 0)
    headcol = lambda b, h, i: (0, h)
    st_map = lambda b, h, i: (b, h, 0, 0)
    st_spec = pl.BlockSpec((1, 1, DK, DV), st_map)
    st_shape = jax.ShapeDtypeStruct((batch, HEADS, DK, DV), F32)
    o_shape = jax.ShapeDtypeStruct((batch * seq, HEADS * DV), BF16)
    return pl.pallas_call(
        functools.partial(_gla_body, nchunks=tb // GLA_CHUNK),
        grid=(batch, HEADS, nb),
        in_specs=[pl.BlockSpec((tb, DK), fwd(COL_GQ, DK)),
                  pl.BlockSpec((tb, DK), fwd(COL_GK, DK)),
                  pl.BlockSpec((tb, DV), fwd(COL_GV, DV)),
                  pl.BlockSpec((tb, LANES), lambda b, h, i: (b * nb + i, 0)),
                  pl.BlockSpec((tb, DK), bwd(COL_GQ, DK)),
                  pl.BlockSpec((tb, DK), bwd(COL_GK, DK)),
                  pl.BlockSpec((tb, DV), bwd(COL_GV, DV)),
                  pl.BlockSpec((tb, LANES), lambda b, h, i: (b * nb + nb - 1 - i, 0)),
                  pl.BlockSpec((LANES, DK), headcol),
                  pl.BlockSpec((1, DK), headcol),
                  pl.BlockSpec((LANES, DK), headcol),
                  pl.BlockSpec((1, DK), headcol),
                  pl.BlockSpec(mskf.shape, const2),
                  pl.BlockSpec(mskb.shape, const2),
                  st_spec, st_spec],
        out_specs=[pl.BlockSpec((tb, DV), lambda b, h, i: (b * nb + i, h)),
                   pl.BlockSpec((tb, DV), lambda b, h, i: (b * nb + nb - 1 - i, h)),
                   st_spec, st_spec,
                   pl.BlockSpec((1, 1, 1, 8, DK), lambda b, h, i: (b, h, i, 0, 0))],
        out_shape=[o_shape, o_shape, st_shape, st_shape,
                   jax.ShapeDtypeStruct((batch, HEADS, nb, 8, DK), F32)],
        scratch_shapes=([pltpu.VMEM((GLA_CHUNK, DK), F32), pltpu.VMEM((GLA_CHUNK, DK), F32),
                         pltpu.VMEM((GLA_CHUNK, DV), F32)] if exact else []),
        compiler_params=_cparams(("parallel", "parallel", "arbitrary")),
        name="gla_exact" if exact else "gla",
    )(p, p, p, lr, p, p, p, lr, wlr_f, blr_f, wlr_b, blr_b, mskf, mskb, s0f, s0b)


def _ret_consts():
    c = RET_CHUNK
    hh = np.arange(HEADS, dtype=np.float64)
    lg = {False: np.log1p(-np.exp2(-5.0 - hh)), True: np.log1p(-np.exp2(-5.5 - hh))}
    t = np.arange(c, dtype=np.float64)[:, None]
    u = np.arange(c, dtype=np.float64)[None, :]
    pos = np.arange(c, dtype=np.float64)
    out = {}
    for rev in (False, True):
        g = lg[rev][:, None, None]
        if not rev:
            dmat = np.where(u <= t, np.exp((t - u) * g), 0.0)
            qd = np.exp((pos + 1.0)[None, :] * lg[rev][:, None])
            kd = np.exp((c - 1.0 - pos)[None, :] * lg[rev][:, None])
        else:
            dmat = np.where(u > t, np.exp((u - t) * g), 0.0)
            qd = np.exp((c - pos)[None, :] * lg[rev][:, None])
            kd = np.exp(pos[None, :] * lg[rev][:, None])
        cd = np.exp(c * lg[rev])
        out[rev] = (dmat.astype(np.float32),
                    np.repeat(qd[:, :, None], DV, axis=2).astype(np.float32),
                    np.repeat(kd[:, :, None], DK, axis=2).astype(np.float32),
                    np.repeat(cd[:, None, None], DV, axis=2).astype(np.float32))
    return out


def _ret_chunk(q, k, v, state_ref, dmat, qd, kd, cd):
    raw = _dot_nt(q, k)
    update = _dot((k.astype(F32) * kd).T.astype(BF16), v)
    state = state_ref[...]
    inter = _dot(q, state.astype(BF16)) * qd
    state_ref[...] = cd * state + update
    yield
    return inter + _dot((raw * dmat).astype(BF16), v)


def _ret_body(qf_ref, kf_ref, vf_ref, qb_ref, kb_ref, vb_ref,
              dmf_ref, qdf_ref, kdf_ref, cdf_ref, dmb_ref, qdb_ref, kdb_ref, cdb_ref,
              s0f_ref, s0b_ref, of_ref, ob_ref, sf_ref, sb_ref, *, nchunks):
    @pl.when(pl.program_id(2) == 0)
    def _():
        sf_ref[...] = s0f_ref[...]
        sb_ref[...] = s0b_ref[...]

    c = RET_CHUNK
    cf = (dmf_ref[0], qdf_ref[0], kdf_ref[0], cdf_ref[0])
    cb = (dmb_ref[0], qdb_ref[0], kdb_ref[0], cdb_ref[0])
    st_f, st_b = sf_ref.at[0, 0], sb_ref.at[0, 0]
    def chunks():
        for n in range(nchunks):
            rf = slice(c * n, c * (n + 1))
            yield (of_ref, rf), _ret_chunk(qf_ref[rf, :], kf_ref[rf, :], vf_ref[rf, :], st_f, *cf)
            m = nchunks - 1 - n
            rb = slice(c * m, c * (m + 1))
            yield (ob_ref, rb), _ret_chunk(qb_ref[rb, :], kb_ref[rb, :], vb_ref[rb, :], st_b, *cb)

    def emit(tag, out):
        o_ref, rows = tag
        o_ref[rows, :] = out.astype(BF16)

    _run_phased(chunks(), emit, RET_PHASE_LAG)


def _ret(p, s0f, s0b, batch, seq, tb):
    nb = seq // tb
    consts = _ret_consts()

    def fwd(col0, width):
        return lambda b, h, i: (b * nb + i, col0 // width + h)

    def bwd(col0, width):
        return lambda b, h, i: (b * nb + nb - 1 - i, col0 // width + h)

    head3 = lambda b, h, i: (h, 0, 0)
    st_spec = pl.BlockSpec((1, 1, DK, DV), lambda b, h, i: (b, h, 0, 0))
    st_shape = jax.ShapeDtypeStruct((batch, HEADS, DK, DV), F32)
    o_shape = jax.ShapeDtypeStruct((batch * seq, HEADS * DV), BF16)
    const_specs = []
    const_args = []
    for rev in (False, True):
        for a in consts[rev]:
            const_specs.append(pl.BlockSpec((1,) + a.shape[1:], head3))
            const_args.append(jnp.asarray(a))
    return pl.pallas_call(
        functools.partial(_ret_body, nchunks=tb // RET_CHUNK),
        grid=(batch, HEADS, nb),
        in_specs=[pl.BlockSpec((tb, DK), fwd(COL_RQ, DK)),
                  pl.BlockSpec((tb, DK), fwd(COL_RK, DK)),
                  pl.BlockSpec((tb, DV), fwd(COL_RV, DV)),
                  pl.BlockSpec((tb, DK), bwd(COL_RQ, DK)),
                  pl.BlockSpec((tb, DK), bwd(COL_RK, DK)),
                  pl.BlockSpec((tb, DV), bwd(COL_RV, DV))]
                 + const_specs + [st_spec, st_spec],
        out_specs=[pl.BlockSpec((tb, DV), lambda b, h, i: (b * nb + i, h)),
                   pl.BlockSpec((tb, DV), lambda b, h, i: (b * nb + nb - 1 - i, h)),
                   st_spec, st_spec],
        out_shape=[o_shape, o_shape, st_shape, st_shape],
        compiler_params=_cparams(("parallel", "parallel", "arbitrary")),
        name="ret",
    )(p, p, p, p, p, p, *const_args, s0f, s0b)


def _rope_tables(seq):
    n = DK // 4
    inv = np.float32(ROPE_BASE) ** (-np.arange(n, dtype=np.float32) / np.float32(n))
    pos = np.arange(seq)
    ar = (pos // GRID_W).astype(np.float32)[:, None] * inv[None, :]
    ac = (pos % GRID_W).astype(np.float32)[:, None] * inv[None, :]
    cos = np.concatenate([np.cos(ar), np.cos(ar), np.cos(ac), np.cos(ac)], axis=1)
    sin = np.concatenate([-np.sin(ar), np.sin(ar), -np.sin(ac), np.sin(ac)], axis=1)
    return jnp.asarray(cos, F32), jnp.asarray(sin, F32)


def _route(logits_t):
    g = [logits_t[i:i + 1] for i in range(N_GROUPS)]
    gmax = jnp.maximum(jnp.maximum(g[0], g[1]), jnp.maximum(g[2], g[3]))
    gsel = jnp.where(g[0] == gmax, 0, jnp.where(g[1] == gmax, 1, jnp.where(g[2] == gmax, 2, 3)))
    gsum = (jnp.exp(g[0] - gmax) + jnp.exp(g[1] - gmax)) + (jnp.exp(g[2] - gmax) + jnp.exp(g[3] - gmax))
    gw = 1.0 / gsum
    e = [logits_t[EXP_ROW0 + EXPERTS_PER_GROUP * i:EXP_ROW0 + EXPERTS_PER_GROUP * (i + 1)]
         for i in range(N_GROUPS)]
    el = jnp.where(gsel == 0, e[0], jnp.where(gsel == 1, e[1], jnp.where(gsel == 2, e[2], e[3])))
    row = lax.broadcasted_iota(jnp.int32, el.shape, 0).astype(F32)
    none = float(EXPERTS_PER_GROUP)
    m1 = jnp.max(el, axis=0, keepdims=True)
    i1 = jnp.min(jnp.where(el == m1, row, none), axis=0, keepdims=True)
    el2 = jnp.where(row == i1, -jnp.inf, el)
    m2 = jnp.max(el2, axis=0, keepdims=True)
    i2 = jnp.min(jnp.where(el2 == m2, row, none), axis=0, keepdims=True)
    r = jnp.exp(m2 - m1)
    w1 = gw / (1.0 + r)
    w2 = gw * r / (1.0 + r)
    base = gsel * EXPERTS_PER_GROUP
    ids = jnp.concatenate([base + i1.astype(jnp.int32), base + i2.astype(jnp.int32)], axis=0)
    return ids, jnp.concatenate([w1, w2], axis=0)


def _store_token_tiled(ref, val):
    m = val.shape[0]
    for j in range(FEAT_TILES):
        ref[pl.ds(j, m, stride=FEAT_TILES), :] = val[:, LANES * j:LANES * (j + 1)]


def _load_token_tiled(ref, m):
    return jnp.concatenate([ref[pl.ds(j, m, stride=FEAT_TILES), :] for j in range(FEAT_TILES)], axis=1)


def _merge_body(gf_ref, gb_ref, rf_ref, rb_ref, gg_ref, rg_ref, mg_ref, mr_ref, x_ref,
                g1_ref, sh2_ref, nfs_ref, gn_ref, wg_ref, wr_ref, wo_ref, wrt_ref, brt_ref, upper_ref,
                h_ref, hn_ref, ids_ref, ew_ref, cnt_ref, wg_s, wr_s, wo_s, stage, sem):
    @pl.when(pl.program_id(0) == 0)
    def _():
        cnt_ref[...] = jnp.zeros_like(cnt_ref)
        for w_hbm, w_s in ((wg_ref, wg_s), (wr_ref, wr_s), (wo_ref, wo_s)):
            copy = pltpu.make_async_copy(w_hbm, stage, sem.at[0])
            copy.start()
            copy.wait()
            w_s[...] = stage[...].astype(BF16)

    def rows_to_logits(r0, n):
        rows = pl.ds(r0, n)
        og = (gf_ref[rows, :] + gb_ref[rows, :]).astype(F32)
        orr = (rf_ref[rows, :] + rb_ref[rows, :]).astype(F32)
        gparts, rparts = [], []
        for hh in range(HEADS):
            seg = og[:, DV * hh:DV * (hh + 1)]
            ms = jnp.mean(seg * seg, axis=-1, keepdims=True)
            gparts.append(seg * lax.rsqrt(ms + NORM_EPS))
            seg = orr[:, DV * hh:DV * (hh + 1)]
            mu = jnp.mean(seg, axis=-1, keepdims=True)
            cen = seg - mu
            var = jnp.mean(cen * cen, axis=-1, keepdims=True)
            rparts.append(cen * lax.rsqrt(var + NORM_EPS))
        o_gla = (jnp.concatenate(gparts, axis=1) * gn_ref[...]).astype(BF16) * _silu(gg_ref[rows, :])
        o_ret = jnp.concatenate(rparts, axis=1).astype(BF16) * _silu(rg_ref[rows, :])
        a_gla = _dot(o_gla, wg_s[...])
        a_ret = _dot(o_ret, wr_s[...])
        yield
        y = _sigmoid(mg_ref[rows, :]) * a_gla.astype(BF16) + _sigmoid(mr_ref[rows, :]) * a_ret.astype(BF16)
        out = _dot(y, wo_s[...])
        yield
        h = x_ref[rows, :] + g1_ref[0] * out
        h_ref[rows, :] = h
        ms = jnp.mean(h * h, axis=-1, keepdims=True)
        hn = h * lax.rsqrt(ms + NORM_EPS) * nfs_ref[0] + sh2_ref[0]
        _store_token_tiled(hn_ref.at[pl.ds(r0 * FEAT_TILES, n * FEAT_TILES)], hn)
        hn_hi = hn.astype(BF16)
        hn_lo = (hn - hn_hi.astype(F32)).astype(BF16)
        both_w = _dot(hn_hi, wrt_ref[...])
        lo_w = _dot(hn_lo, wrt_ref[:, :ROUTER_W])
        yield
        return (both_w[:, :ROUTER_W] + both_w[:, ROUTER_W:] + lo_w) + brt_ref[...]

    tm = x_ref.shape[0]
    n = tm // MERGE_SPLIT
    logits = jnp.concatenate(_round_robin([rows_to_logits(g * n, n) for g in range(MERGE_SPLIT)]), axis=0)
    ids, ew = _route(logits.T)

    erow = lax.broadcasted_iota(jnp.int32, (N_EXPERTS, tm), 0)
    oh0 = jnp.where(erow == ids[0:1], 1.0, 0.0)
    oh1 = jnp.where(erow == ids[1:2], 1.0, 0.0)
    both = oh0 + oh1
    before = _dot(both.astype(BF16), upper_ref[...]) + cnt_ref[:, 0:1].astype(F32)
    rank0 = jnp.sum(oh0 * before, axis=0, keepdims=True)
    rank1 = jnp.sum(oh1 * before, axis=0, keepdims=True)
    total = cnt_ref[:, 0:1] + jnp.sum(both, axis=1, keepdims=True).astype(jnp.int32)
    cnt_ref[...] = jnp.broadcast_to(total, cnt_ref.shape)
    ids_ref[...] = jnp.concatenate([ids, rank0.astype(jnp.int32), rank1.astype(jnp.int32),
                                    jnp.zeros((4, tm), jnp.int32)], axis=0)
    ew_ref[...] = jnp.concatenate([ew, jnp.zeros((6, tm), F32)], axis=0)


def _merge(gf, gb, rf, rb, p, x2, g1, sh2, nfs, gn, wg, wr, wo, wrt, brt, seq, tm):
    t = x2.shape[0]
    tiles_per_batch = seq // tm
    row = lambda i: (i, 0)
    mod = lambda i: (i // tiles_per_batch, 0, 0)
    const = lambda i: (0, 0)
    tok = pl.BlockSpec((tm, D_MODEL), row)
    vec = pl.BlockSpec((1, D_MODEL), const)
    modspec = pl.BlockSpec((1, 1, D_MODEL), mod)
    wspec = pl.BlockSpec(memory_space=pl.ANY)
    upper = jnp.asarray(np.triu(np.ones((tm, tm), np.float32), 1), BF16)

    def pcol(col0):
        return pl.BlockSpec((tm, D_MODEL), lambda i: (i, col0 // D_MODEL))

    return pl.pallas_call(
        _merge_body,
        grid=(t // tm,),
        in_specs=[tok, tok, tok, tok, pcol(COL_GG), pcol(COL_RG), pcol(COL_MG), pcol(COL_MR), tok,
                  modspec, modspec, modspec, vec, wspec, wspec, wspec,
                  pl.BlockSpec((D_MODEL, 2 * ROUTER_W), const), pl.BlockSpec((1, ROUTER_W), const),
                  pl.BlockSpec((tm, tm), const)],
        out_specs=[tok, pl.BlockSpec((tm * FEAT_TILES, LANES), row),
                   pl.BlockSpec((8, tm), lambda i: (0, i)), pl.BlockSpec((8, tm), lambda i: (0, i)),
                   pl.BlockSpec((N_EXPERTS, LANES), const)],
        out_shape=[jax.ShapeDtypeStruct((t, D_MODEL), F32), jax.ShapeDtypeStruct((t * FEAT_TILES, LANES), F32),
                   jax.ShapeDtypeStruct((8, t), jnp.int32), jax.ShapeDtypeStruct((8, t), F32),
                   jax.ShapeDtypeStruct((N_EXPERTS, LANES), jnp.int32)],
        scratch_shapes=[pltpu.VMEM((D_MODEL, D_MODEL), BF16)] * 3
                       + [pltpu.VMEM((D_MODEL, D_MODEL), F32), pltpu.SemaphoreType.DMA((1,))],
        compiler_params=_cparams(("arbitrary",)),
        name="merge",
    )(gf, gb, rf, rb, p, p, p, p, x2, g1, sh2, nfs, gn, wg, wr, wo, wrt, brt, upper)


GATHER_UNROLL = 8


def _aligned(tok):
    off = tok * FEAT_TILES
    return off if isinstance(off, int) else pl.multiple_of(off, FEAT_TILES)


def _token_copy(src_hbm, dst, sem, src_tok, dst_tok):
    return pltpu.make_async_copy(src_hbm.at[pl.ds(_aligned(src_tok), FEAT_TILES), :],
                                 dst.at[pl.ds(_aligned(dst_tok), FEAT_TILES), :], sem)


def _wait_tokens(src_hbm, dst, sem):
    pltpu.make_async_copy(src_hbm.at[pl.ds(0, dst.shape[0]), :], dst, sem).wait()


GATHER_BUFS = 3
GATHER_AHEAD = GATHER_BUFS - 1


def _issue_rows(n, start_row, unrolled):
    if unrolled:
        for r in range(n):
            start_row(r, r % 2)
    else:
        def body(g, carry):
            for u in range(GATHER_UNROLL):
                start_row(g * GATHER_UNROLL + u, u % 2)
            return carry
        lax.fori_loop(0, n // GATHER_UNROLL, body, 0)


EXPERT_GROUP = 2


def _expert_body(be_ref, src0_ref, stok_ref, nused_ref, hn_hbm, *refs):
    nw = 3 * EXPERT_GROUP
    w_refs, y_ref, scratch = refs[:nw], refs[nw], refs[nw + 1:]
    bufs, sems, w_s = scratch[:GATHER_BUFS], scratch[GATHER_BUFS], scratch[GATHER_BUFS + 1:]
    i = pl.program_id(0)
    nsteps = (nused_ref[0] + EXPERT_GROUP - 1) // EXPERT_GROUP
    last = stok_ref.shape[0] - 1
    rows = MOE_BLOCK * FEAT_TILES

    def issue(step, s, unrolled):
        bases = [src0_ref[step * EXPERT_GROUP + g] for g in range(EXPERT_GROUP)]

        def start(g, r, priority):
            tok = stok_ref[jnp.minimum(bases[g] + r, last)]
            _token_copy(hn_hbm, bufs[s], sems.at[s], tok, g * MOE_BLOCK + r).start(priority=priority)

        if unrolled:
            _issue_rows(EXPERT_GROUP * MOE_BLOCK, lambda j, p: start(j // MOE_BLOCK, j % MOE_BLOCK, p), True)
        else:
            for g in range(EXPERT_GROUP):
                _issue_rows(MOE_BLOCK, lambda r, p, g=g: start(g, r, p), False)

    @pl.when(i == 0)
    def _():
        for b in range(GATHER_AHEAD):
            issue(b, b, False)

    for g in range(EXPERT_GROUP):
        blk = i * EXPERT_GROUP + g

        @pl.when(jnp.logical_and(i < nsteps, jnp.logical_or(
            i == 0, be_ref[blk] != be_ref[jnp.maximum(blk - EXPERT_GROUP, 0)])))
        def _():
            for k in range(3):
                w_s[3 * g + k][...] = w_refs[3 * g + k][0].astype(BF16)

    def block_phases(cur, g):
        part = pl.ds(g * rows, rows)
        xb = _load_token_tiled(bufs[cur].at[part], MOE_BLOCK).astype(BF16)
        gate = _dot(xb, w_s[3 * g][...])
        up = _dot(xb, w_s[3 * g + 1][...])
        yield
        y = _dot((_silu(gate) * up).astype(BF16), w_s[3 * g + 2][...])
        yield
        _store_token_tiled(y_ref.at[part], y)

    for cur in range(GATHER_BUFS):
        mine = i % GATHER_BUFS == cur

        @pl.when(jnp.logical_and(mine, i < nsteps))
        def _():
            _wait_tokens(hn_hbm, bufs[cur], sems.at[cur])
            issue(i + GATHER_AHEAD, (cur + GATHER_AHEAD) % GATHER_BUFS, True)
            _round_robin([block_phases(cur, g) for g in range(EXPERT_GROUP)])

        @pl.when(jnp.logical_and(mine, jnp.logical_and(i >= nsteps, i < nsteps + GATHER_AHEAD)))
        def _():
            _wait_tokens(hn_hbm, bufs[cur], sems.at[cur])

    @pl.when(i >= nsteps)
    def _():
        y_ref[...] = jnp.zeros_like(y_ref)


def _experts(block_e, src0, stok, nused, hn, w_gate, w_up, w_down, nblk):
    rows = EXPERT_GROUP * MOE_BLOCK * FEAT_TILES
    w_specs, w_args, w_scratch = [], [], []
    for g in range(EXPERT_GROUP):
        wmap = lambda i, be, s0, st, nu, g=g: (be[i * EXPERT_GROUP + g], 0, 0)
        w_specs += [pl.BlockSpec((1, D_MODEL, EXPERT_FF), wmap), pl.BlockSpec((1, D_MODEL, EXPERT_FF), wmap),
                    pl.BlockSpec((1, EXPERT_FF, D_MODEL), wmap)]
        w_args += [w_gate, w_up, w_down]
        w_scratch += [pltpu.VMEM((D_MODEL, EXPERT_FF), BF16), pltpu.VMEM((D_MODEL, EXPERT_FF), BF16),
                      pltpu.VMEM((EXPERT_FF, D_MODEL), BF16)]
    return pl.pallas_call(
        _expert_body,
        grid_spec=pltpu.PrefetchScalarGridSpec(
            num_scalar_prefetch=4,
            grid=(nblk // EXPERT_GROUP,),
            in_specs=[pl.BlockSpec(memory_space=pl.ANY)] + w_specs,
            out_specs=pl.BlockSpec((rows, LANES), lambda i, be, s0, st, nu: (i, 0)),
            scratch_shapes=[pltpu.VMEM((rows, LANES), F32)] * GATHER_BUFS
                           + [pltpu.SemaphoreType.DMA((GATHER_BUFS,))] + w_scratch),
        out_shape=jax.ShapeDtypeStruct((nblk // EXPERT_GROUP * rows, LANES), F32),
        compiler_params=_cparams(("arbitrary",)),
        name="experts",
    )(block_e, src0, stok, nused, hn, *w_args)


def _combine_body(pos_ref, y_hbm, h_ref, ew_ref, g2_ref, nf_ref, o_ref, *scratch):
    i = pl.program_id(0)
    n = pl.num_programs(0)
    tm = COMBINE_TM
    t = pos_ref.shape[0] // TOP_K
    bufs, sems = scratch[:GATHER_BUFS], scratch[GATHER_BUFS]

    def issue(tile, s, unrolled):
        def start_row(j, priority):
            k, r = j % TOP_K, j // TOP_K
            _token_copy(y_hbm, bufs[s].at[k], sems.at[s], pos_ref[k * t + tile * tm + r], r).start(priority=priority)
        _issue_rows(tm * TOP_K, start_row, unrolled)

    def wait(s):
        for k in range(TOP_K):
            _wait_tokens(y_hbm, bufs[s].at[k], sems.at[s])

    @pl.when(i == 0)
    def _():
        for b in range(GATHER_AHEAD):
            issue(b, b, False)

    for cur in range(GATHER_BUFS):
        @pl.when(i % GATHER_BUFS == cur)
        def _():
            wait(cur)
            issue(jnp.minimum(i + GATHER_AHEAD, n - 1), (cur + GATHER_AHEAD) % GATHER_BUFS, True)
            wt = jnp.concatenate([ew_ref[...]] * (LANES // 8), axis=0).T
            moe = (wt[:, 0:1] * _load_token_tiled(bufs[cur].at[0], tm)
                   + wt[:, 1:2] * _load_token_tiled(bufs[cur].at[1], tm))
            h = h_ref[...] + g2_ref[0] * moe
            ms = jnp.mean(h * h, axis=-1, keepdims=True)
            o_ref[...] = h * lax.rsqrt(ms + NORM_EPS) * nf_ref[...]

        @pl.when(jnp.logical_and(i % GATHER_BUFS == cur, i == n - 1))
        def _():
            for ahead in range(1, GATHER_BUFS):
                wait((cur + ahead) % GATHER_BUFS)


def _combine(pos, y_pad, h, ew, g2, nf, seq):
    t = h.shape[0]
    tm = COMBINE_TM
    tiles_per_batch = seq // tm
    ybuf = pltpu.VMEM((TOP_K, tm * FEAT_TILES, LANES), F32)
    return pl.pallas_call(
        _combine_body,
        grid_spec=pltpu.PrefetchScalarGridSpec(
            num_scalar_prefetch=1,
            grid=(t // tm,),
            in_specs=[pl.BlockSpec(memory_space=pl.ANY),
                      pl.BlockSpec((tm, D_MODEL), lambda i, pos: (i, 0)),
                      pl.BlockSpec((8, tm), lambda i, pos: (0, i)),
                      pl.BlockSpec((1, 1, D_MODEL), lambda i, pos: (i // tiles_per_batch, 0, 0)),
                      pl.BlockSpec((1, D_MODEL), lambda i, pos: (0, 0))],
            out_specs=pl.BlockSpec((tm, D_MODEL), lambda i, pos: (i, 0)),
            scratch_shapes=[ybuf] * GATHER_BUFS + [pltpu.SemaphoreType.DMA((GATHER_BUFS,))]),
        out_shape=jax.ShapeDtypeStruct((t, D_MODEL), F32),
        compiler_params=_cparams(("arbitrary",)),
        name="combine",
    )(pos, y_pad, h, ew, g2, nf)


def _dispatch_indices(ids, ranks, counts, t):
    a = t * TOP_K
    nblk = a // MOE_BLOCK + N_EXPERTS + GATHER_AHEAD * EXPERT_GROUP
    experts = jnp.arange(N_EXPERTS, dtype=jnp.int32)
    starts = jnp.cumsum(counts) - counts
    padded = (counts + MOE_BLOCK - 1) // MOE_BLOCK * MOE_BLOCK
    pends = jnp.cumsum(padded)
    pstarts = pends - padded
    block_start = jnp.arange(nblk, dtype=jnp.int32) * MOE_BLOCK
    block_e = jnp.minimum(jnp.sum((block_start[:, None] >= pends[None, :]).astype(jnp.int32), axis=1),
                          N_EXPERTS - 1)
    of_block = (block_e[:, None] == experts[None, :]).astype(jnp.int32)
    src0 = block_start + jnp.sum(of_block * (starts - pstarts)[None, :], axis=1)
    nused = pends[-1:] // MOE_BLOCK
    tok = jnp.arange(t, dtype=jnp.int32)[None, :]
    slot = jnp.arange(TOP_K, dtype=jnp.int32)[:, None]
    key = ids * a + tok * TOP_K + slot
    stok = (jnp.sort(key.reshape(a)) % a) // TOP_K
    ids_d = ids.reshape(a // LANES, LANES)
    pstart_of = jnp.zeros_like(ids_d)
    for e in range(N_EXPERTS):
        pstart_of = jnp.where(ids_d == e, pstarts[e], pstart_of)
    pos = (ranks.reshape(a // LANES, LANES) + pstart_of).reshape(a)
    i32 = lambda v: v.astype(jnp.int32)
    return i32(block_e), i32(src0), i32(stok), i32(nused), i32(pos), nblk


def kernel(x, c, ctx, c_ctx, w_ada, b_ada, norm_mix, norm_ffn, w_in, gla_lr_w, gla_lr_b, gla_norm,
           w_branch_gla, w_branch_ret, w_out, w_router_group, b_router_group, w_router_expert,
           b_router_expert, w_expert_gate, w_expert_up, w_expert_down, norm_final):
    batch, seq, d = x.shape
    ctx_len = ctx.shape[1]
    assert d == D_MODEL and w_ada.shape[0] == 1, "single-layer block with D_MODEL features"
    t = batch * seq

    c8 = jnp.zeros((8, d), F32).at[:batch].set(c).at[batch].set(c_ctx)
    mod = _ada(c8, w_ada[0], b_ada[0][None, :])
    sh1, sc1, g1, sh2, sc2, g2 = [mod[:, d * i:d * (i + 1)] for i in range(6)]
    lat = lambda m: m[:batch, None, :]
    cx = lambda m: m[batch:batch + 1, None, :]

    w_main, w_lr = _wprep(jnp.swapaxes(w_in[0], 0, 1))
    nm = norm_mix[0][None, :]

    ones = jnp.ones((ctx_len, DK), F32)
    p_ctx, lr_ctx = _proj(ctx.reshape(batch * ctx_len, d), cx(sh1), cx(sc1), nm, w_main, w_lr,
                          ones, jnp.zeros_like(ones), tm=ctx_len, rows_per_mod=batch * ctx_len)
    x2 = x.reshape(t, d)
    cos, sin = _rope_tables(seq)
    p_lat, lr_lat = _proj(x2, lat(sh1), lat(sc1), nm, w_main, w_lr, cos, sin, tm=PROJ_TM, rows_per_mod=seq)

    wlr_f = jnp.zeros((LANES, HEADS * DK), F32).at[:GLA_RANK].set(gla_lr_w[0, 0])
    wlr_b = jnp.zeros((LANES, HEADS * DK), F32).at[GLA_RANK:2 * GLA_RANK].set(gla_lr_w[0, 1])
    blr_f, blr_b = gla_lr_b[0, 0][None, :], gla_lr_b[0, 1][None, :]
    zero_state = jnp.zeros((batch, HEADS, DK, DV), F32)

    _, _, gs_f, gs_b, _ = _gla(p_ctx, lr_ctx, wlr_f, blr_f, wlr_b, blr_b, zero_state, zero_state,
                               batch, ctx_len, tb=ctx_len)
    gla_args = (p_lat, lr_lat, wlr_f, blr_f, wlr_b, blr_b, gs_f, gs_b, batch, seq)
    gla_f, gla_b, _, _, decay_range = _gla(*gla_args, tb=SCAN_TB)
    gla_f, gla_b = lax.cond(jnp.max(decay_range) > GLA_SAFE_RANGE,
                            lambda: tuple(_gla(*gla_args, tb=GLA_EXACT_TB, exact=True)[:2]),
                            lambda: (gla_f, gla_b))

    _, _, rs_f, rs_b = _ret(p_ctx, zero_state, zero_state, batch, ctx_len, tb=ctx_len)
    ret_f, ret_b, _, _ = _ret(p_lat, rs_f, rs_b, batch, seq, tb=SCAN_TB)

    wrt = jnp.zeros((d, ROUTER_W), F32)
    wrt = wrt.at[:, :N_GROUPS].set(w_router_group[0]).at[:, EXP_ROW0:EXP_ROW0 + N_EXPERTS].set(w_router_expert[0])
    wrt_hi = wrt.astype(BF16)
    wrt = jnp.concatenate([wrt_hi, (wrt - wrt_hi.astype(F32)).astype(BF16)], axis=1)
    brt = jnp.zeros((1, ROUTER_W), F32)
    brt = brt.at[0, :N_GROUPS].set(b_router_group[0]).at[0, EXP_ROW0:EXP_ROW0 + N_EXPERTS].set(b_router_expert[0])
    h, hn, ids8, ew8, cnt = _merge(gla_f, gla_b, ret_f, ret_b, p_lat, x2, lat(g1), lat(sh2),
                                   norm_ffn[0][None, None, :] * (1.0 + lat(sc2)),
                                   jnp.tile(gla_norm[0], HEADS)[None, :],
                                   w_branch_gla[0], w_branch_ret[0], w_out[0], wrt, brt, seq, tm=MERGE_TM)

    block_e, src0, stok, nused, pos, nblk = _dispatch_indices(ids8[:TOP_K], ids8[TOP_K:2 * TOP_K], cnt[:, 0], t)
    y_pad = _experts(block_e, src0, stok, nused, hn, w_expert_gate[0], w_expert_up[0], w_expert_down[0], nblk)
    out = _combine(pos, y_pad, h, ew8, lat(g2), norm_final[None, :], seq)
    return out.reshape(batch, seq, d)
```

```python
import functools

import numpy as np
import jax
import jax.numpy as jnp
from jax import lax
from jax.experimental import pallas as pl
from jax.experimental.pallas import tpu as pltpu

F32 = jnp.float32
BF16 = jnp.bfloat16

D_MODEL = 1024
GRID_W = 64
HEADS = 4
DK = 128
DV = 256
GLA_RANK = 16
GLA_TAU = 16.0
GLA_CHUNK = 64
GLA_SUB = 16
RET_CHUNK = 128
ROPE_BASE = 10000.0
N_GROUPS = 4
EXPERTS_PER_GROUP = 8
N_EXPERTS = N_GROUPS * EXPERTS_PER_GROUP
TOP_K = 2
EXPERT_FF = 256
MOE_BLOCK = 128
NORM_EPS = 1e-6

COL_GQ, COL_GK, COL_GV, COL_GG = 0, 512, 1024, 2048
COL_RQ, COL_RK, COL_RV, COL_RG = 3072, 3584, 4096, 5120
COL_MG, COL_MR = 6144, 7168
PROJ_W = 8192
LANES = 128
FEAT_TILES = D_MODEL // LANES
ROUTER_W = 128
EXP_ROW0 = 8

VMEM_LIMIT = 56 * 1024 * 1024

PROJ_TM = 1024
PROJ_TN = 2048
SCAN_TB = 1024
GLA_PHASE_LAG = 2
RET_PHASE_LAG = 0
GLA_EXACT_TB = 256
GLA_SAFE_RANGE = 60.0
MERGE_TM = 512
MERGE_SPLIT = 2
COMBINE_TM = 256


def _cparams(sem):
    return pltpu.CompilerParams(dimension_semantics=sem, vmem_limit_bytes=VMEM_LIMIT)


def _sigmoid(x):
    return 1.0 / (1.0 + jnp.exp(-x))


def _silu(x):
    return x * _sigmoid(x)


def _dot(a, b):
    return jnp.dot(a, b, preferred_element_type=F32)


def _dot_nt(a, b):
    return lax.dot_general(a, b, (((1,), (1,)), ((), ())), preferred_element_type=F32)


def _ada_body(c_ref, w_ref, b_ref, o_ref):
    s = _silu(c_ref[...])
    o_ref[...] = _dot(s.astype(BF16), w_ref[...].astype(BF16)) + b_ref[...]


def _ada(c8, w, b):
    n = w.shape[1]
    tn = 1536
    return pl.pallas_call(
        _ada_body,
        grid=(n // tn,),
        in_specs=[pl.BlockSpec((8, D_MODEL), lambda j: (0, 0)),
                  pl.BlockSpec((D_MODEL, tn), lambda j: (0, j)),
                  pl.BlockSpec((1, tn), lambda j: (0, j))],
        out_specs=pl.BlockSpec((8, tn), lambda j: (0, j)),
        out_shape=jax.ShapeDtypeStruct((8, n), F32),
        compiler_params=_cparams(("arbitrary",)),
        name="ada",
    )(c8, w, b)


LR_COL0 = COL_GG + HEADS * DV
WPREP_ROWS = 1024


def _wprep_body(w_ref, wlr_ref, main_ref, lr_ref):
    main_ref[...] = w_ref[...].astype(BF16)
    pad = jnp.zeros((LANES - 2 * GLA_RANK, D_MODEL), F32)
    lr_ref[...] = jnp.concatenate([wlr_ref[...], pad], axis=0).astype(BF16)


def _wprep(wt):
    steps_before = LR_COL0 // WPREP_ROWS

    def src_row(i):
        row = jnp.where(i < steps_before, i * WPREP_ROWS, i * WPREP_ROWS + 2 * GLA_RANK)
        return (pl.multiple_of(row, 2 * GLA_RANK), 0)

    return pl.pallas_call(
        _wprep_body,
        grid=(PROJ_W // WPREP_ROWS,),
        in_specs=[pl.BlockSpec((pl.Element(WPREP_ROWS), pl.Element(D_MODEL)), src_row),
                  pl.BlockSpec((pl.Element(2 * GLA_RANK), pl.Element(D_MODEL)), lambda i: (LR_COL0, 0))],
        out_specs=[pl.BlockSpec((WPREP_ROWS, D_MODEL), lambda i: (i, 0)),
                   pl.BlockSpec((LANES, D_MODEL), lambda i: (0, 0))],
        out_shape=[jax.ShapeDtypeStruct((PROJ_W, D_MODEL), BF16), jax.ShapeDtypeStruct((LANES, D_MODEL), BF16)],
        compiler_params=_cparams(("arbitrary",)),
        name="wprep",
    )(wt, wt)


ROPE_TILE = COL_RQ // PROJ_TN
ROPE_COL0 = COL_RQ % PROJ_TN


def _rope(x, cos, sin):
    lane = lax.broadcasted_iota(jnp.int32, x.shape, 1)
    partner = jnp.where((lane % 64) < 32, pltpu.roll(x, DK - 32, 1), pltpu.roll(x, 32, 1))
    return x * cos + partner * sin


def _proj_body(x_ref, sh_ref, sc_ref, g_ref, w_ref, wlr_ref, cos_ref, sin_ref, o_ref, lr_ref, h_ref):
    j = pl.program_id(1)

    @pl.when(j == 0)
    def _():
        x = x_ref[...]
        ms = jnp.mean(x * x, axis=-1, keepdims=True)
        y = x * lax.rsqrt(ms + NORM_EPS) * g_ref[...]
        hb = (y * (1.0 + sc_ref[0]) + sh_ref[0]).astype(BF16)
        h_ref[...] = hb
        lr_ref[...] = _dot_nt(hb, wlr_ref[...])

    @pl.when(j != ROPE_TILE)
    def _():
        o_ref[...] = _dot_nt(h_ref[...], w_ref[...]).astype(BF16)

    @pl.when(j == ROPE_TILE)
    def _():
        h = h_ref[...]
        cos, sin = cos_ref[...], sin_ref[...]
        for pair in range(HEADS):
            lo = ROPE_COL0 + 2 * DK * pair
            acc = _dot_nt(h, w_ref[lo:lo + 2 * DK, :])
            if pair >= HEADS // 2:
                acc = acc * (DK ** -0.5)
            for half in range(2):
                blk = acc[:, DK * half:DK * (half + 1)]
                o_ref[:, lo + DK * half:lo + DK * (half + 1)] = _rope(blk, cos, sin).astype(BF16)
        rope_end = ROPE_COL0 + 2 * HEADS * DK
        for lo, hi in ((0, ROPE_COL0), (rope_end, PROJ_TN)):
            if lo < hi:
                o_ref[:, lo:hi] = _dot_nt(h, w_ref[lo:hi, :]).astype(BF16)


def _proj(x2, sh, sc, g, w_main, w_lr, cos, sin, tm, rows_per_mod):
    t = x2.shape[0]
    tn = PROJ_TN
    tiles_per_mod = rows_per_mod // tm
    mod_map = lambda i, j: (i // tiles_per_mod, 0, 0)
    pos_tiles = cos.shape[0] // tm
    pos_map = lambda i, j: (i % pos_tiles, 0)
    return pl.pallas_call(
        _proj_body,
        grid=(t // tm, PROJ_W // tn),
        in_specs=[pl.BlockSpec((tm, D_MODEL), lambda i, j: (i, 0)),
                  pl.BlockSpec((1, 1, D_MODEL), mod_map),
                  pl.BlockSpec((1, 1, D_MODEL), mod_map),
                  pl.BlockSpec((1, D_MODEL), lambda i, j: (0, 0)),
                  pl.BlockSpec((tn, D_MODEL), lambda i, j: (j, 0)),
                  pl.BlockSpec((LANES, D_MODEL), lambda i, j: (0, 0)),
                  pl.BlockSpec((tm, DK), pos_map), pl.BlockSpec((tm, DK), pos_map)],
        out_specs=[pl.BlockSpec((tm, tn), lambda i, j: (i, j)),
                   pl.BlockSpec((tm, LANES), lambda i, j: (i, 0))],
        out_shape=[jax.ShapeDtypeStruct((t, PROJ_W), BF16),
                   jax.ShapeDtypeStruct((t, LANES), F32)],
        scratch_shapes=[pltpu.VMEM((tm, D_MODEL), BF16)],
        compiler_params=_cparams(("parallel", "arbitrary")),
        name="proj",
    )(x2, sh, sc, g, w_main, w_lr, cos, sin)


def _gla_consts():
    c, s = GLA_CHUNK, GLA_SUB
    msk = {}
    for rev in (False, True):
        cols = []
        for i in range(c // s):
            keys = np.arange(c)[_key_rows(i, rev)][None, :]
            t = np.arange(c)[:, None]
            visible = (keys > t) if rev else (keys <= t)
            cols.append(np.where(t // s == i, visible, False))
        msk[rev] = np.concatenate(cols, axis=1).astype(np.float32)
    return msk


def _key_rows(i, rev):
    return slice(GLA_SUB * i, GLA_CHUNK) if rev else slice(0, GLA_SUB * (i + 1))


def _log_gate(lr, w, b):
    z = _dot(lr.astype(BF16), w.astype(BF16)) + b
    return -(jnp.maximum(-z, 0.0) + jnp.log(1.0 + jnp.exp(-jnp.abs(z)))) * (1.0 / GLA_TAU)


def _subchunk_scan(la, rev):
    c = la.shape[0]
    pos = lax.broadcasted_iota(jnp.int32, la.shape, 0) % GLA_SUB
    w = la
    step = 1
    while step < GLA_SUB:
        if rev:
            w = w + jnp.where(pos < GLA_SUB - step, pltpu.roll(w, c - step, 0), 0.0)
        else:
            w = w + jnp.where(pos >= step, pltpu.roll(w, step, 0), 0.0)
        step *= 2
    return w


def _gla_intra_exact(q, kf, v, b, rev, scratch):
    q_s, b_s, o_s = scratch
    c = GLA_CHUNK
    q_s[...] = q.astype(F32) * (DK ** -0.5)
    b_s[...] = b
    vf = v.astype(F32)
    row = lax.broadcasted_iota(jnp.int32, (c, 1), 0)

    def body(t, carry):
        qt = q_s[pl.ds(t, 1), :]
        bt = b_s[pl.ds(t, 1), :]
        visible = (row > t) if rev else (row <= t)
        decay = jnp.exp(jnp.where(visible, bt - b, -jnp.inf))
        col = jnp.sum(kf * decay * qt, axis=1, keepdims=True)
        o_s[pl.ds(t, 1), :] = jnp.sum(col * vf, axis=0, keepdims=True)
        return carry

    lax.fori_loop(0, c, body, 0)
    return o_s[...]


def _gla_chunk(q, k, v, la, state_ref, mask, rev, exact_scratch=None):
    c, s = GLA_CHUNK, GLA_SUB
    nsub = c // s
    w = _subchunk_scan(la, rev)
    tot = [w[s * i:s * i + 1] if rev else w[s * (i + 1) - 1:s * (i + 1)] for i in range(nsub)]
    anchors = [None] * nsub
    acc = jnp.zeros_like(tot[0])
    worst = jnp.zeros_like(tot[0])
    for i in (reversed(range(nsub)) if rev else range(nsub)):
        anchors[i] = acc
        acc = acc + tot[i]
        worst = jnp.maximum(worst, -tot[i])
    b_end = acc
    ref = jnp.concatenate([jnp.broadcast_to(a, (s, DK)) for a in anchors], axis=0)
    b = w + ref

    qt = q.astype(F32) * (DK ** -0.5) * jnp.exp(w)
    kf = k.astype(F32)
    if exact_scratch is None:
        kstack = jnp.concatenate([kf[_key_rows(i, rev)] * jnp.exp(anchors[i] - b[_key_rows(i, rev)])
                                  for i in range(nsub)], axis=0).astype(BF16)
        raw = _dot_nt(qt.astype(BF16), kstack)
    kst = kf * jnp.exp(b_end - b)
    xt = jnp.concatenate([kst, jnp.broadcast_to(jnp.exp(b_end), (c, DK))], axis=0).T
    update = _dot(xt[:, :c].astype(BF16), v)
    state = state_ref[...]
    inter = _dot((qt * jnp.exp(ref)).astype(BF16), state.astype(BF16))
    state_ref[...] = xt[:, c:c + 1] * state + update
    yield
    if exact_scratch is None:
        vstack = jnp.concatenate([v[_key_rows(i, rev)] for i in range(nsub)], axis=0)
        intra = _dot(jnp.where(mask != 0.0, raw, 0.0).astype(BF16), vstack)
    else:
        intra = _gla_intra_exact(q, kf, v, b, rev, exact_scratch)
    return inter + intra, worst


def _round_robin(gens):
    results = [None] * len(gens)
    live = list(range(len(gens)))
    while live:
        for g in list(live):
            try:
                next(gens[g])
            except StopIteration as done:
                results[g] = done.value
                live.remove(g)
    return results


def _run_phased(chunks, emit, lag):
    def finish(tag, gen):
        try:
            next(gen)
        except StopIteration as done:
            emit(tag, done.value)
            return
        raise AssertionError("chunk generator has more than two phases")

    pending = []
    for tag, gen in chunks:
        next(gen)
        pending.append((tag, gen))
        if len(pending) > lag:
            finish(*pending.pop(0))
    for item in pending:
        finish(*item)


def _gla_body(qf_ref, kf_ref, vf_ref, lrf_ref, qb_ref, kb_ref, vb_ref, lrb_ref,
              wf_ref, bf_ref, wb_ref, bb_ref, mskf_ref, mskb_ref,
              s0f_ref, s0b_ref, of_ref, ob_ref, sf_ref, sb_ref, rng_ref, *exact_scratch, nchunks):
    @pl.when(pl.program_id(2) == 0)
    def _():
        sf_ref[...] = s0f_ref[...]
        sb_ref[...] = s0b_ref[...]

    c = GLA_CHUNK
    scratch = exact_scratch or None
    la_f = _log_gate(lrf_ref[...], wf_ref[...], bf_ref[...])
    la_b = _log_gate(lrb_ref[...], wb_ref[...], bb_ref[...])
    mskf, mskb = mskf_ref[...], mskb_ref[...]
    st_f, st_b = sf_ref.at[0, 0], sb_ref.at[0, 0]
    worst = [jnp.zeros((1, DK), F32)]

    def chunks():
        for n in range(nchunks):
            rf = slice(c * n, c * (n + 1))
            yield (of_ref, rf), _gla_chunk(qf_ref[rf, :], kf_ref[rf, :], vf_ref[rf, :], la_f[rf, :], st_f, mskf,
                                           False, scratch)
            m = nchunks - 1 - n
            rb = slice(c * m, c * (m + 1))
            yield (ob_ref, rb), _gla_chunk(qb_ref[rb, :], kb_ref[rb, :], vb_ref[rb, :], la_b[rb, :], st_b, mskb,
                                           True, scratch)

    def emit(tag, result):
        o_ref, rows = tag
        o_ref[rows, :] = result[0].astype(BF16)
        worst[0] = jnp.maximum(worst[0], result[1])

    _run_phased(chunks(), emit, GLA_PHASE_LAG)
    rng_ref[0, 0, 0] = jnp.broadcast_to(worst[0], (8, DK))


def _gla(p, lr, wlr_f, blr_f, wlr_b, blr_b, s0f, s0b, batch, seq, tb, exact=False):
    nb = seq // tb
    msk = _gla_consts()
    mskf, mskb = jnp.asarray(msk[False]), jnp.asarray(msk[True])

    def fwd(col0, width):
        return lambda b, h, i: (b * nb + i, col0 // width + h)

    def bwd(col0, width):
        return lambda b, h, i: (b * nb + nb - 1 - i, col0 // width + h)

    const2 = lambda b, h, i: (0, 0)
    headcol = lambda b, h, i: (0, h)
    st_map = lambda b, h, i: (b, h, 0, 0)
    st_spec = pl.BlockSpec((1, 1, DK, DV), st_map)
    st_shape = jax.ShapeDtypeStruct((batch, HEADS, DK, DV), F32)
    o_shape = jax.ShapeDtypeStruct((batch * seq, HEADS * DV), BF16)
    return pl.pallas_call(
        functools.partial(_gla_body, nchunks=tb // GLA_CHUNK),
        grid=(batch, HEADS, nb),
        in_specs=[pl.BlockSpec((tb, DK), fwd(COL_GQ, DK)),
                  pl.BlockSpec((tb, DK), fwd(COL_GK, DK)),
                  pl.BlockSpec((tb, DV), fwd(COL_GV, DV)),
                  pl.BlockSpec((tb, LANES), lambda b, h, i: (b * nb + i, 0)),
                  pl.BlockSpec((tb, DK), bwd(COL_GQ, DK)),
                  pl.BlockSpec((tb, DK), bwd(COL_GK, DK)),
                  pl.BlockSpec((tb, DV), bwd(COL_GV, DV)),
                  pl.BlockSpec((tb, LANES), lambda b, h, i: (b * nb + nb - 1 - i, 0)),
                  pl.BlockSpec((LANES, DK), headcol),
                  pl.BlockSpec((1, DK), headcol),
                  pl.BlockSpec((LANES, DK), headcol),
                  pl.BlockSpec((1, DK), headcol),
                  pl.BlockSpec(mskf.shape, const2),
                  pl.BlockSpec(mskb.shape, const2),
                  st_spec, st_spec],
        out_specs=[pl.BlockSpec((tb, DV), lambda b, h, i: (b * nb + i, h)),
                   pl.BlockSpec((tb, DV), lambda b, h, i: (b * nb + nb - 1 - i, h)),
                   st_spec, st_spec,
                   pl.BlockSpec((1, 1, 1, 8, DK), lambda b, h, i: (b, h, i, 0, 0))],
        out_shape=[o_shape, o_shape, st_shape, st_shape,
                   jax.ShapeDtypeStruct((batch, HEADS, nb, 8, DK), F32)],
        scratch_shapes=([pltpu.VMEM((GLA_CHUNK, DK), F32), pltpu.VMEM((GLA_CHUNK, DK), F32),
                         pltpu.VMEM((GLA_CHUNK, DV), F32)] if exact else []),
        compiler_params=_cparams(("parallel", "parallel", "arbitrary")),
        name="gla_exact" if exact else "gla",
    )(p, p, p, lr, p, p, p, lr, wlr_f, blr_f, wlr_b, blr_b, mskf, mskb, s0f, s0b)


def _ret_consts():
    c = RET_CHUNK
    hh = np.arange(HEADS, dtype=np.float64)
    lg = {False: np.log1p(-np.exp2(-5.0 - hh)), True: np.log1p(-np.exp2(-5.5 - hh))}
    t = np.arange(c, dtype=np.float64)[:, None]
    u = np.arange(c, dtype=np.float64)[None, :]
    pos = np.arange(c, dtype=np.float64)
    out = {}
    for rev in (False, True):
        g = lg[rev][:, None, None]
        if not rev:
            dmat = np.where(u <= t, np.exp((t - u) * g), 0.0)
            qd = np.exp((pos + 1.0)[None, :] * lg[rev][:, None])
            kd = np.exp((c - 1.0 - pos)[None, :] * lg[rev][:, None])
        else:
            dmat = np.where(u > t, np.exp((u - t) * g), 0.0)
            qd = np.exp((c - pos)[None, :] * lg[rev][:, None])
            kd = np.exp(pos[None, :] * lg[rev][:, None])
        cd = np.exp(c * lg[rev])
        out[rev] = (dmat.astype(np.float32),
                    np.repeat(qd[:, :, None], DV, axis=2).astype(np.float32),
                    np.repeat(kd[:, :, None], DK, axis=2).astype(np.float32),
                    np.repeat(cd[:, None, None], DV, axis=2).astype(np.float32))
    return out


def _ret_chunk(q, k, v, state_ref, dmat, qd, kd, cd):
    raw = _dot_nt(q, k)
    update = _dot((k.astype(F32) * kd).T.astype(BF16), v)
    state = state_ref[...]
    inter = _dot(q, state.astype(BF16)) * qd
    state_ref[...] = cd * state + update
    yield
    return inter + _dot((raw * dmat).astype(BF16), v)


def _ret_body(qf_ref, kf_ref, vf_ref, qb_ref, kb_ref, vb_ref,
              dmf_ref, qdf_ref, kdf_ref, cdf_ref, dmb_ref, qdb_ref, kdb_ref, cdb_ref,
              s0f_ref, s0b_ref, of_ref, ob_ref, sf_ref, sb_ref, *, nchunks):
    @pl.when(pl.program_id(2) == 0)
    def _():
        sf_ref[...] = s0f_ref[...]
        sb_ref[...] = s0b_ref[...]

    c = RET_CHUNK
    cf = (dmf_ref[0], qdf_ref[0], kdf_ref[0], cdf_ref[0])
    cb = (dmb_ref[0], qdb_ref[0], kdb_ref[0], cdb_ref[0])
    st_f, st_b = sf_ref.at[0, 0], sb_ref.at[0, 0]
    def chunks():
        for n in range(nchunks):
            rf = slice(c * n, c * (n + 1))
            yield (of_ref, rf), _ret_chunk(qf_ref[rf, :], kf_ref[rf, :], vf_ref[rf, :], st_f, *cf)
            m = nchunks - 1 - n
            rb = slice(c * m, c * (m + 1))
            yield (ob_ref, rb), _ret_chunk(qb_ref[rb, :], kb_ref[rb, :], vb_ref[rb, :], st_b, *cb)

    def emit(tag, out):
        o_ref, rows = tag
        o_ref[rows, :] = out.astype(BF16)

    _run_phased(chunks(), emit, RET_PHASE_LAG)


def _ret(p, s0f, s0b, batch, seq, tb):
    nb = seq // tb
    consts = _ret_consts()

    def fwd(col0, width):
        return lambda b, h, i: (b * nb + i, col0 // width + h)

    def bwd(col0, width):
        return lambda b, h, i: (b * nb + nb - 1 - i, col0 // width + h)

    head3 = lambda b, h, i: (h, 0, 0)
    st_spec = pl.BlockSpec((1, 1, DK, DV), lambda b, h, i: (b, h, 0, 0))
    st_shape = jax.ShapeDtypeStruct((batch, HEADS, DK, DV), F32)
    o_shape = jax.ShapeDtypeStruct((batch * seq, HEADS * DV), BF16)
    const_specs = []
    const_args = []
    for rev in (False, True):
        for a in consts[rev]:
            const_specs.append(pl.BlockSpec((1,) + a.shape[1:], head3))
            const_args.append(jnp.asarray(a))
    return pl.pallas_call(
        functools.partial(_ret_body, nchunks=tb // RET_CHUNK),
        grid=(batch, HEADS, nb),
        in_specs=[pl.BlockSpec((tb, DK), fwd(COL_RQ, DK)),
                  pl.BlockSpec((tb, DK), fwd(COL_RK, DK)),
                  pl.BlockSpec((tb, DV), fwd(COL_RV, DV)),
                  pl.BlockSpec((tb, DK), bwd(COL_RQ, DK)),
                  pl.BlockSpec((tb, DK), bwd(COL_RK, DK)),
                  pl.BlockSpec((tb, DV), bwd(COL_RV, DV))]
                 + const_specs + [st_spec, st_spec],
        out_specs=[pl.BlockSpec((tb, DV), lambda b, h, i: (b * nb + i, h)),
                   pl.BlockSpec((tb, DV), lambda b, h, i: (b * nb + nb - 1 - i, h)),
                   st_spec, st_spec],
        out_shape=[o_shape, o_shape, st_shape, st_shape],
        compiler_params=_cparams(("parallel", "parallel", "arbitrary")),
        name="ret",
    )(p, p, p, p, p, p, *const_args, s0f, s0b)


def _rope_tables(seq):
    n = DK // 4
    inv = np.float32(ROPE_BASE) ** (-np.arange(n, dtype=np.float32) / np.float32(n))
    pos = np.arange(seq)
    ar = (pos // GRID_W).astype(np.float32)[:, None] * inv[None, :]
    ac = (pos % GRID_W).astype(np.float32)[:, None] * inv[None, :]
    cos = np.concatenate([np.cos(ar), np.cos(ar), np.cos(ac), np.cos(ac)], axis=1)
    sin = np.concatenate([-np.sin(ar), np.sin(ar), -np.sin(ac), np.sin(ac)], axis=1)
    return jnp.asarray(cos, F32), jnp.asarray(sin, F32)


def _route(logits_t):
    g = [logits_t[i:i + 1] for i in range(N_GROUPS)]
    gmax = jnp.maximum(jnp.maximum(g[0], g[1]), jnp.maximum(g[2], g[3]))
    gsel = jnp.where(g[0] == gmax, 0, jnp.where(g[1] == gmax, 1, jnp.where(g[2] == gmax, 2, 3)))
    gsum = (jnp.exp(g[0] - gmax) + jnp.exp(g[1] - gmax)) + (jnp.exp(g[2] - gmax) + jnp.exp(g[3] - gmax))
    gw = 1.0 / gsum
    e = [logits_t[EXP_ROW0 + EXPERTS_PER_GROUP * i:EXP_ROW0 + EXPERTS_PER_GROUP * (i + 1)]
         for i in range(N_GROUPS)]
    el = jnp.where(gsel == 0, e[0], jnp.where(gsel == 1, e[1], jnp.where(gsel == 2, e[2], e[3])))
    row = lax.broadcasted_iota(jnp.int32, el.shape, 0).astype(F32)
    none = float(EXPERTS_PER_GROUP)
    m1 = jnp.max(el, axis=0, keepdims=True)
    i1 = jnp.min(jnp.where(el == m1, row, none), axis=0, keepdims=True)
    el2 = jnp.where(row == i1, -jnp.inf, el)
    m2 = jnp.max(el2, axis=0, keepdims=True)
    i2 = jnp.min(jnp.where(el2 == m2, row, none), axis=0, keepdims=True)
    r = jnp.exp(m2 - m1)
    w1 = gw / (1.0 + r)
    w2 = gw * r / (1.0 + r)
    base = gsel * EXPERTS_PER_GROUP
    ids = jnp.concatenate([base + i1.astype(jnp.int32), base + i2.astype(jnp.int32)], axis=0)
    return ids, jnp.concatenate([w1, w2], axis=0)


def _store_token_tiled(ref, val):
    m = val.shape[0]
    for j in range(FEAT_TILES):
        ref[pl.ds(j, m, stride=FEAT_TILES), :] = val[:, LANES * j:LANES * (j + 1)]


def _load_token_tiled(ref, m):
    return jnp.concatenate([ref[pl.ds(j, m, stride=FEAT_TILES), :] for j in range(FEAT_TILES)], axis=1)


def _merge_body(gf_ref, gb_ref, rf_ref, rb_ref, gg_ref, rg_ref, mg_ref, mr_ref, x_ref,
                g1_ref, sh2_ref, nfs_ref, gn_ref, wg_ref, wr_ref, wo_ref, wrt_ref, brt_ref, upper_ref,
                h_ref, hn_ref, ids_ref, ew_ref, cnt_ref, wg_s, wr_s, wo_s, stage, sem):
    @pl.when(pl.program_id(0) == 0)
    def _():
        cnt_ref[...] = jnp.zeros_like(cnt_ref)
        for w_hbm, w_s in ((wg_ref, wg_s), (wr_ref, wr_s), (wo_ref, wo_s)):
            copy = pltpu.make_async_copy(w_hbm, stage, sem.at[0])
            copy.start()
            copy.wait()
            w_s[...] = stage[...].astype(BF16)

    def rows_to_logits(r0, n):
        rows = pl.ds(r0, n)
        og = (gf_ref[rows, :] + gb_ref[rows, :]).astype(F32)
        orr = (rf_ref[rows, :] + rb_ref[rows, :]).astype(F32)
        gparts, rparts = [], []
        for hh in range(HEADS):
            seg = og[:, DV * hh:DV * (hh + 1)]
            ms = jnp.mean(seg * seg, axis=-1, keepdims=True)
            gparts.append(seg * lax.rsqrt(ms + NORM_EPS))
            seg = orr[:, DV * hh:DV * (hh + 1)]
            mu = jnp.mean(seg, axis=-1, keepdims=True)
            cen = seg - mu
            var = jnp.mean(cen * cen, axis=-1, keepdims=True)
            rparts.append(cen * lax.rsqrt(var + NORM_EPS))
        o_gla = (jnp.concatenate(gparts, axis=1) * gn_ref[...]).astype(BF16) * _silu(gg_ref[rows, :])
        o_ret = jnp.concatenate(rparts, axis=1).astype(BF16) * _silu(rg_ref[rows, :])
        a_gla = _dot(o_gla, wg_s[...])
        a_ret = _dot(o_ret, wr_s[...])
        yield
        y = _sigmoid(mg_ref[rows, :]) * a_gla.astype(BF16) + _sigmoid(mr_ref[rows, :]) * a_ret.astype(BF16)
        out = _dot(y, wo_s[...])
        yield
        h = x_ref[rows, :] + g1_ref[0] * out
        h_ref[rows, :] = h
        ms = jnp.mean(h * h, axis=-1, keepdims=True)
        hn = h * lax.rsqrt(ms + NORM_EPS) * nfs_ref[0] + sh2_ref[0]
        _store_token_tiled(hn_ref.at[pl.ds(r0 * FEAT_TILES, n * FEAT_TILES)], hn)
        hn_hi = hn.astype(BF16)
        hn_lo = (hn - hn_hi.astype(F32)).astype(BF16)
        both_w = _dot(hn_hi, wrt_ref[...])
        lo_w = _dot(hn_lo, wrt_ref[:, :ROUTER_W])
        yield
        return (both_w[:, :ROUTER_W] + both_w[:, ROUTER_W:] + lo_w) + brt_ref[...]

    tm = x_ref.shape[0]
    n = tm // MERGE_SPLIT
    logits = jnp.concatenate(_round_robin([rows_to_logits(g * n, n) for g in range(MERGE_SPLIT)]), axis=0)
    ids, ew = _route(logits.T)

    erow = lax.broadcasted_iota(jnp.int32, (N_EXPERTS, tm), 0)
    oh0 = jnp.where(erow == ids[0:1], 1.0, 0.0)
    oh1 = jnp.where(erow == ids[1:2], 1.0, 0.0)
    both = oh0 + oh1
    before = _dot(both.astype(BF16), upper_ref[...]) + cnt_ref[:, 0:1].astype(F32)
    rank0 = jnp.sum(oh0 * before, axis=0, keepdims=True)
    rank1 = jnp.sum(oh1 * before, axis=0, keepdims=True)
    total = cnt_ref[:, 0:1] + jnp.sum(both, axis=1, keepdims=True).astype(jnp.int32)
    cnt_ref[...] = jnp.broadcast_to(total, cnt_ref.shape)
    ids_ref[...] = jnp.concatenate([ids, rank0.astype(jnp.int32), rank1.astype(jnp.int32),
                                    jnp.zeros((4, tm), jnp.int32)], axis=0)
    ew_ref[...] = jnp.concatenate([ew, jnp.zeros((6, tm), F32)], axis=0)


def _merge(gf, gb, rf, rb, p, x2, g1, sh2, nfs, gn, wg, wr, wo, wrt, brt, seq, tm):
    t = x2.shape[0]
    tiles_per_batch = seq // tm
    row = lambda i: (i, 0)
    mod = lambda i: (i // tiles_per_batch, 0, 0)
    const = lambda i: (0, 0)
    tok = pl.BlockSpec((tm, D_MODEL), row)
    vec = pl.BlockSpec((1, D_MODEL), const)
    modspec = pl.BlockSpec((1, 1, D_MODEL), mod)
    wspec = pl.BlockSpec(memory_space=pl.ANY)
    upper = jnp.asarray(np.triu(np.ones((tm, tm), np.float32), 1), BF16)

    def pcol(col0):
        return pl.BlockSpec((tm, D_MODEL), lambda i: (i, col0 // D_MODEL))

    return pl.pallas_call(
        _merge_body,
        grid=(t // tm,),
        in_specs=[tok, tok, tok, tok, pcol(COL_GG), pcol(COL_RG), pcol(COL_MG), pcol(COL_MR), tok,
                  modspec, modspec, modspec, vec, wspec, wspec, wspec,
                  pl.BlockSpec((D_MODEL, 2 * ROUTER_W), const), pl.BlockSpec((1, ROUTER_W), const),
                  pl.BlockSpec((tm, tm), const)],
        out_specs=[tok, pl.BlockSpec((tm * FEAT_TILES, LANES), row),
                   pl.BlockSpec((8, tm), lambda i: (0, i)), pl.BlockSpec((8, tm), lambda i: (0, i)),
                   pl.BlockSpec((N_EXPERTS, LANES), const)],
        out_shape=[jax.ShapeDtypeStruct((t, D_MODEL), F32), jax.ShapeDtypeStruct((t * FEAT_TILES, LANES), F32),
                   jax.ShapeDtypeStruct((8, t), jnp.int32), jax.ShapeDtypeStruct((8, t), F32),
                   jax.ShapeDtypeStruct((N_EXPERTS, LANES), jnp.int32)],
        scratch_shapes=[pltpu.VMEM((D_MODEL, D_MODEL), BF16)] * 3
                       + [pltpu.VMEM((D_MODEL, D_MODEL), F32), pltpu.SemaphoreType.DMA((1,))],
        compiler_params=_cparams(("arbitrary",)),
        name="merge",
    )(gf, gb, rf, rb, p, p, p, p, x2, g1, sh2, nfs, gn, wg, wr, wo, wrt, brt, upper)


GATHER_UNROLL = 8


def _aligned(tok):
    off = tok * FEAT_TILES
    return off if isinstance(off, int) else pl.multiple_of(off, FEAT_TILES)


def _token_copy(src_hbm, dst, sem, src_tok, dst_tok):
    return pltpu.make_async_copy(src_hbm.at[pl.ds(_aligned(src_tok), FEAT_TILES), :],
                                 dst.at[pl.ds(_aligned(dst_tok), FEAT_TILES), :], sem)


def _wait_tokens(src_hbm, dst, sem):
    pltpu.make_async_copy(src_hbm.at[pl.ds(0, dst.shape[0]), :], dst, sem).wait()


GATHER_BUFS = 3
GATHER_AHEAD = GATHER_BUFS - 1


def _issue_rows(n, start_row, unrolled):
    if unrolled:
        for r in range(n):
            start_row(r, r % 2)
    else:
        def body(g, carry):
            for u in range(GATHER_UNROLL):
                start_row(g * GATHER_UNROLL + u, u % 2)
            return carry
        lax.fori_loop(0, n // GATHER_UNROLL, body, 0)


EXPERT_GROUP = 4


def _expert_body(be_ref, src0_ref, stok_ref, nused_ref, hn_hbm, *refs):
    nw = 3 * EXPERT_GROUP
    w_refs, y_ref, scratch = refs[:nw], refs[nw], refs[nw + 1:]
    bufs, sems, w_s = scratch[:GATHER_BUFS], scratch[GATHER_BUFS], scratch[GATHER_BUFS + 1:]
    i = pl.program_id(0)
    nsteps = (nused_ref[0] + EXPERT_GROUP - 1) // EXPERT_GROUP
    last = stok_ref.shape[0] - 1
    rows = MOE_BLOCK * FEAT_TILES

    def issue(step, s, unrolled):
        bases = [src0_ref[step * EXPERT_GROUP + g] for g in range(EXPERT_GROUP)]

        def start(g, r, priority):
            tok = stok_ref[jnp.minimum(bases[g] + r, last)]
            _token_copy(hn_hbm, bufs[s], sems.at[s], tok, g * MOE_BLOCK + r).start(priority=priority)

        if unrolled:
            _issue_rows(EXPERT_GROUP * MOE_BLOCK, lambda j, p: start(j // MOE_BLOCK, j % MOE_BLOCK, p), True)
        else:
            for g in range(EXPERT_GROUP):
                _issue_rows(MOE_BLOCK, lambda r, p, g=g: start(g, r, p), False)

    @pl.when(i == 0)
    def _():
        for b in range(GATHER_AHEAD):
            issue(b, b, False)

    for g in range(EXPERT_GROUP):
        blk = i * EXPERT_GROUP + g

        @pl.when(jnp.logical_and(i < nsteps, jnp.logical_or(
            i == 0, be_ref[blk] != be_ref[jnp.maximum(blk - EXPERT_GROUP, 0)])))
        def _():
            for k in range(3):
                w_s[3 * g + k][...] = w_refs[3 * g + k][0].astype(BF16)

    def block_phases(cur, g):
        part = pl.ds(g * rows, rows)
        xb = _load_token_tiled(bufs[cur].at[part], MOE_BLOCK).astype(BF16)
        gate = _dot(xb, w_s[3 * g][...])
        up = _dot(xb, w_s[3 * g + 1][...])
        yield
        y = _dot((_silu(gate) * up).astype(BF16), w_s[3 * g + 2][...])
        yield
        _store_token_tiled(y_ref.at[part], y)

    for cur in range(GATHER_BUFS):
        mine = i % GATHER_BUFS == cur

        @pl.when(jnp.logical_and(mine, i < nsteps))
        def _():
            _wait_tokens(hn_hbm, bufs[cur], sems.at[cur])
            issue(i + GATHER_AHEAD, (cur + GATHER_AHEAD) % GATHER_BUFS, True)
            _round_robin([block_phases(cur, g) for g in range(EXPERT_GROUP)])

        @pl.when(jnp.logical_and(mine, jnp.logical_and(i >= nsteps, i < nsteps + GATHER_AHEAD)))
        def _():
            _wait_tokens(hn_hbm, bufs[cur], sems.at[cur])

    @pl.when(i >= nsteps)
    def _():
        y_ref[...] = jnp.zeros_like(y_ref)


def _experts(block_e, src0, stok, nused, hn, w_gate, w_up, w_down, nblk):
    rows = EXPERT_GROUP * MOE_BLOCK * FEAT_TILES
    w_specs, w_args, w_scratch = [], [], []
    for g in range(EXPERT_GROUP):
        wmap = lambda i, be, s0, st, nu, g=g: (be[i * EXPERT_GROUP + g], 0, 0)
        w_specs += [pl.BlockSpec((1, D_MODEL, EXPERT_FF), wmap), pl.BlockSpec((1, D_MODEL, EXPERT_FF), wmap),
                    pl.BlockSpec((1, EXPERT_FF, D_MODEL), wmap)]
        w_args += [w_gate, w_up, w_down]
        w_scratch += [pltpu.VMEM((D_MODEL, EXPERT_FF), BF16), pltpu.VMEM((D_MODEL, EXPERT_FF), BF16),
                      pltpu.VMEM((EXPERT_FF, D_MODEL), BF16)]
    return pl.pallas_call(
        _expert_body,
        grid_spec=pltpu.PrefetchScalarGridSpec(
            num_scalar_prefetch=4,
            grid=(nblk // EXPERT_GROUP,),
            in_specs=[pl.BlockSpec(memory_space=pl.ANY)] + w_specs,
            out_specs=pl.BlockSpec((rows, LANES), lambda i, be, s0, st, nu: (i, 0)),
            scratch_shapes=[pltpu.VMEM((rows, LANES), F32)] * GATHER_BUFS
                           + [pltpu.SemaphoreType.DMA((GATHER_BUFS,))] + w_scratch),
        out_shape=jax.ShapeDtypeStruct((nblk // EXPERT_GROUP * rows, LANES), F32),
        compiler_params=_cparams(("arbitrary",)),
        name="experts",
    )(block_e, src0, stok, nused, hn, *w_args)


def _combine_body(pos_ref, y_hbm, h_ref, ew_ref, g2_ref, nf_ref, o_ref, *scratch):
    i = pl.program_id(0)
    n = pl.num_programs(0)
    tm = COMBINE_TM
    t = pos_ref.shape[0] // TOP_K
    bufs, sems = scratch[:GATHER_BUFS], scratch[GATHER_BUFS]

    def issue(tile, s, unrolled):
        def start_row(j, priority):
            k, r = j % TOP_K, j // TOP_K
            _token_copy(y_hbm, bufs[s].at[k], sems.at[s], pos_ref[k * t + tile * tm + r], r).start(priority=priority)
        _issue_rows(tm * TOP_K, start_row, unrolled)

    def wait(s):
        for k in range(TOP_K):
            _wait_tokens(y_hbm, bufs[s].at[k], sems.at[s])

    @pl.when(i == 0)
    def _():
        for b in range(GATHER_AHEAD):
            issue(b, b, False)

    for cur in range(GATHER_BUFS):
        @pl.when(i % GATHER_BUFS == cur)
        def _():
            wait(cur)
            issue(jnp.minimum(i + GATHER_AHEAD, n - 1), (cur + GATHER_AHEAD) % GATHER_BUFS, True)
            wt = jnp.concatenate([ew_ref[...]] * (LANES // 8), axis=0).T
            moe = (wt[:, 0:1] * _load_token_tiled(bufs[cur].at[0], tm)
                   + wt[:, 1:2] * _load_token_tiled(bufs[cur].at[1], tm))
            h = h_ref[...] + g2_ref[0] * moe
            ms = jnp.mean(h * h, axis=-1, keepdims=True)
            o_ref[...] = h * lax.rsqrt(ms + NORM_EPS) * nf_ref[...]

        @pl.when(jnp.logical_and(i % GATHER_BUFS == cur, i == n - 1))
        def _():
            for ahead in range(1, GATHER_BUFS):
                wait((cur + ahead) % GATHER_BUFS)


def _combine(pos, y_pad, h, ew, g2, nf, seq):
    t = h.shape[0]
    tm = COMBINE_TM
    tiles_per_batch = seq // tm
    ybuf = pltpu.VMEM((TOP_K, tm * FEAT_TILES, LANES), F32)
    return pl.pallas_call(
        _combine_body,
        grid_spec=pltpu.PrefetchScalarGridSpec(
            num_scalar_prefetch=1,
            grid=(t // tm,),
            in_specs=[pl.BlockSpec(memory_space=pl.ANY),
                      pl.BlockSpec((tm, D_MODEL), lambda i, pos: (i, 0)),
                      pl.BlockSpec((8, tm), lambda i, pos: (0, i)),
                      pl.BlockSpec((1, 1, D_MODEL), lambda i, pos: (i // tiles_per_batch, 0, 0)),
                      pl.BlockSpec((1, D_MODEL), lambda i, pos: (0, 0))],
            out_specs=pl.BlockSpec((tm, D_MODEL), lambda i, pos: (i, 0)),
            scratch_shapes=[ybuf] * GATHER_BUFS + [pltpu.SemaphoreType.DMA((GATHER_BUFS,))]),
        out_shape=jax.ShapeDtypeStruct((t, D_MODEL), F32),
        compiler_params=_cparams(("arbitrary",)),
        name="combine",
    )(pos, y_pad, h, ew, g2, nf)


def _dispatch_indices(ids, ranks, counts, t):
    a = t * TOP_K
    nblk = a // MOE_BLOCK + N_EXPERTS + GATHER_AHEAD * EXPERT_GROUP
    experts = jnp.arange(N_EXPERTS, dtype=jnp.int32)
    starts = jnp.cumsum(counts) - counts
    padded = (counts + MOE_BLOCK - 1) // MOE_BLOCK * MOE_BLOCK
    pends = jnp.cumsum(padded)
    pstarts = pends - padded
    block_start = jnp.arange(nblk, dtype=jnp.int32) * MOE_BLOCK
    block_e = jnp.minimum(jnp.sum((block_start[:, None] >= pends[None, :]).astype(jnp.int32), axis=1),
                          N_EXPERTS - 1)
    of_block = (block_e[:, None] == experts[None, :]).astype(jnp.int32)
    src0 = block_start + jnp.sum(of_block * (starts - pstarts)[None, :], axis=1)
    nused = pends[-1:] // MOE_BLOCK
    tok = jnp.arange(t, dtype=jnp.int32)[None, :]
    slot = jnp.arange(TOP_K, dtype=jnp.int32)[:, None]
    key = ids * a + tok * TOP_K + slot
    stok = (jnp.sort(key.reshape(a)) % a) // TOP_K
    ids_d = ids.reshape(a // LANES, LANES)
    pstart_of = jnp.zeros_like(ids_d)
    for e in range(N_EXPERTS):
        pstart_of = jnp.where(ids_d == e, pstarts[e], pstart_of)
    pos = (ranks.reshape(a // LANES, LANES) + pstart_of).reshape(a)
    i32 = lambda v: v.astype(jnp.int32)
    return i32(block_e), i32(src0), i32(stok), i32(nused), i32(pos), nblk


def kernel(x, c, ctx, c_ctx, w_ada, b_ada, norm_mix, norm_ffn, w_in, gla_lr_w, gla_lr_b, gla_norm,
           w_branch_gla, w_branch_ret, w_out, w_router_group, b_router_group, w_router_expert,
           b_router_expert, w_expert_gate, w_expert_up, w_expert_down, norm_final):
    batch, seq, d = x.shape
    ctx_len = ctx.shape[1]
    assert d == D_MODEL and w_ada.shape[0] == 1, "single-layer block with D_MODEL features"
    t = batch * seq

    c8 = jnp.zeros((8, d), F32).at[:batch].set(c).at[batch].set(c_ctx)
    mod = _ada(c8, w_ada[0], b_ada[0][None, :])
    sh1, sc1, g1, sh2, sc2, g2 = [mod[:, d * i:d * (i + 1)] for i in range(6)]
    lat = lambda m: m[:batch, None, :]
    cx = lambda m: m[batch:batch + 1, None, :]

    w_main, w_lr = _wprep(jnp.swapaxes(w_in[0], 0, 1))
    nm = norm_mix[0][None, :]

    ones = jnp.ones((ctx_len, DK), F32)
    p_ctx, lr_ctx = _proj(ctx.reshape(batch * ctx_len, d), cx(sh1), cx(sc1), nm, w_main, w_lr,
                          ones, jnp.zeros_like(ones), tm=ctx_len, rows_per_mod=batch * ctx_len)
    x2 = x.reshape(t, d)
    cos, sin = _rope_tables(seq)
    p_lat, lr_lat = _proj(x2, lat(sh1), lat(sc1), nm, w_main, w_lr, cos, sin, tm=PROJ_TM, rows_per_mod=seq)

    wlr_f = jnp.zeros((LANES, HEADS * DK), F32).at[:GLA_RANK].set(gla_lr_w[0, 0])
    wlr_b = jnp.zeros((LANES, HEADS * DK), F32).at[GLA_RANK:2 * GLA_RANK].set(gla_lr_w[0, 1])
    blr_f, blr_b = gla_lr_b[0, 0][None, :], gla_lr_b[0, 1][None, :]
    zero_state = jnp.zeros((batch, HEADS, DK, DV), F32)

    _, _, gs_f, gs_b, _ = _gla(p_ctx, lr_ctx, wlr_f, blr_f, wlr_b, blr_b, zero_state, zero_state,
                               batch, ctx_len, tb=ctx_len)
    gla_args = (p_lat, lr_lat, wlr_f, blr_f, wlr_b, blr_b, gs_f, gs_b, batch, seq)
    gla_f, gla_b, _, _, decay_range = _gla(*gla_args, tb=SCAN_TB)
    gla_f, gla_b = lax.cond(jnp.max(decay_range) > GLA_SAFE_RANGE,
                            lambda: tuple(_gla(*gla_args, tb=GLA_EXACT_TB, exact=True)[:2]),
                            lambda: (gla_f, gla_b))

    _, _, rs_f, rs_b = _ret(p_ctx, zero_state, zero_state, batch, ctx_len, tb=ctx_len)
    ret_f, ret_b, _, _ = _ret(p_lat, rs_f, rs_b, batch, seq, tb=SCAN_TB)

    wrt = jnp.zeros((d, ROUTER_W), F32)
    wrt = wrt.at[:, :N_GROUPS].set(w_router_group[0]).at[:, EXP_ROW0:EXP_ROW0 + N_EXPERTS].set(w_router_expert[0])
    wrt_hi = wrt.astype(BF16)
    wrt = jnp.concatenate([wrt_hi, (wrt - wrt_hi.astype(F32)).astype(BF16)], axis=1)
    brt = jnp.zeros((1, ROUTER_W), F32)
    brt = brt.at[0, :N_GROUPS].set(b_router_group[0]).at[0, EXP_ROW0:EXP_ROW0 + N_EXPERTS].set(b_router_expert[0])
    h, hn, ids8, ew8, cnt = _merge(gla_f, gla_b, ret_f, ret_b, p_lat, x2, lat(g1), lat(sh2),
                                   norm_ffn[0][None, None, :] * (1.0 + lat(sc2)),
                                   jnp.tile(gla_norm[0], HEADS)[None, :],
                                   w_branch_gla[0], w_branch_ret[0], w_out[0], wrt, brt, seq, tm=MERGE_TM)

    block_e, src0, stok, nused, pos, nblk = _dispatch_indices(ids8[:TOP_K], ids8[TOP_K:2 * TOP_K], cnt[:, 0], t)
    y_pad = _experts(block_e, src0, stok, nused, hn, w_expert_gate[0], w_expert_up[0], w_expert_down[0], nblk)
    out = _combine(pos, y_pad, h, ew8, lat(g2), norm_final[None, :], seq)
    return out.reshape(batch, seq, d)
```

```python
import functools

import numpy as np
import jax
import jax.numpy as jnp
from jax import lax
from jax.experimental import pallas as pl
from jax.experimental.pallas import tpu as pltpu

F32 = jnp.float32
BF16 = jnp.bfloat16

D_MODEL = 1024
GRID_W = 64
HEADS = 4
DK = 128
DV = 256
GLA_RANK = 16
GLA_TAU = 16.0
GLA_CHUNK = 64
GLA_SUB = 16
RET_CHUNK = 128
ROPE_BASE = 10000.0
N_GROUPS = 4
EXPERTS_PER_GROUP = 8
N_EXPERTS = N_GROUPS * EXPERTS_PER_GROUP
TOP_K = 2
EXPERT_FF = 256
MOE_BLOCK = 128
NORM_EPS = 1e-6

COL_GQ, COL_GK, COL_GV, COL_GG = 0, 512, 1024, 2048
COL_RQ, COL_RK, COL_RV, COL_RG = 3072, 3584, 4096, 5120
COL_MG, COL_MR = 6144, 7168
PROJ_W = 8192
LANES = 128
FEAT_TILES = D_MODEL // LANES
ROUTER_W = 128
EXP_ROW0 = 8

VMEM_LIMIT = 56 * 1024 * 1024

PROJ_TM = 1024
PROJ_TN = 2048
SCAN_TB = 1024
GLA_PHASE_LAG = 2
RET_PHASE_LAG = 0
GLA_EXACT_TB = 256
GLA_SAFE_RANGE = 60.0
MERGE_TM = 512
MERGE_SPLIT = 2
COMBINE_TM = 256


def _cparams(sem):
    return pltpu.CompilerParams(dimension_semantics=sem, vmem_limit_bytes=VMEM_LIMIT)


def _sigmoid(x):
    return 1.0 / (1.0 + jnp.exp(-x))


def _silu(x):
    return x * _sigmoid(x)


def _dot(a, b):
    return jnp.dot(a, b, preferred_element_type=F32)


def _dot_nt(a, b):
    return lax.dot_general(a, b, (((1,), (1,)), ((), ())), preferred_element_type=F32)


def _ada_body(c_ref, w_ref, b_ref, o_ref):
    s = _silu(c_ref[...])
    o_ref[...] = _dot(s.astype(BF16), w_ref[...].astype(BF16)) + b_ref[...]


def _ada(c8, w, b):
    n = w.shape[1]
    tn = 1536
    return pl.pallas_call(
        _ada_body,
        grid=(n // tn,),
        in_specs=[pl.BlockSpec((8, D_MODEL), lambda j: (0, 0)),
                  pl.BlockSpec((D_MODEL, tn), lambda j: (0, j)),
                  pl.BlockSpec((1, tn), lambda j: (0, j))],
        out_specs=pl.BlockSpec((8, tn), lambda j: (0, j)),
        out_shape=jax.ShapeDtypeStruct((8, n), F32),
        compiler_params=_cparams(("arbitrary",)),
        name="ada",
    )(c8, w, b)


LR_COL0 = COL_GG + HEADS * DV
WPREP_ROWS = 1024


def _wprep_body(w_ref, wlr_ref, main_ref, lr_ref):
    main_ref[...] = w_ref[...].astype(BF16)
    pad = jnp.zeros((LANES - 2 * GLA_RANK, D_MODEL), F32)
    lr_ref[...] = jnp.concatenate([wlr_ref[...], pad], axis=0).astype(BF16)


def _wprep(wt):
    steps_before = LR_COL0 // WPREP_ROWS

    def src_row(i):
        row = jnp.where(i < steps_before, i * WPREP_ROWS, i * WPREP_ROWS + 2 * GLA_RANK)
        return (pl.multiple_of(row, 2 * GLA_RANK), 0)

    return pl.pallas_call(
        _wprep_body,
        grid=(PROJ_W // WPREP_ROWS,),
        in_specs=[pl.BlockSpec((pl.Element(WPREP_ROWS), pl.Element(D_MODEL)), src_row),
                  pl.BlockSpec((pl.Element(2 * GLA_RANK), pl.Element(D_MODEL)), lambda i: (LR_COL0, 0))],
        out_specs=[pl.BlockSpec((WPREP_ROWS, D_MODEL), lambda i: (i, 0)),
                   pl.BlockSpec((LANES, D_MODEL), lambda i: (0, 0))],
        out_shape=[jax.ShapeDtypeStruct((PROJ_W, D_MODEL), BF16), jax.ShapeDtypeStruct((LANES, D_MODEL), BF16)],
        compiler_params=_cparams(("arbitrary",)),
        name="wprep",
    )(wt, wt)


ROPE_TILE = COL_RQ // PROJ_TN
ROPE_COL0 = COL_RQ % PROJ_TN


def _rope(x, cos, sin):
    lane = lax.broadcasted_iota(jnp.int32, x.shape, 1)
    partner = jnp.where((lane % 64) < 32, pltpu.roll(x, DK - 32, 1), pltpu.roll(x, 32, 1))
    return x * cos + partner * sin


def _proj_body(x_ref, sh_ref, sc_ref, g_ref, w_ref, wlr_ref, cos_ref, sin_ref, o_ref, lr_ref, h_ref):
    j = pl.program_id(1)

    @pl.when(j == 0)
    def _():
        x = x_ref[...]
        ms = jnp.mean(x * x, axis=-1, keepdims=True)
        y = x * lax.rsqrt(ms + NORM_EPS) * g_ref[...]
        hb = (y * (1.0 + sc_ref[0]) + sh_ref[0]).astype(BF16)
        h_ref[...] = hb
        lr_ref[...] = _dot_nt(hb, wlr_ref[...])

    @pl.when(j != ROPE_TILE)
    def _():
        o_ref[...] = _dot_nt(h_ref[...], w_ref[...]).astype(BF16)

    @pl.when(j == ROPE_TILE)
    def _():
        h = h_ref[...]
        cos, sin = cos_ref[...], sin_ref[...]
        for pair in range(HEADS):
            lo = ROPE_COL0 + 2 * DK * pair
            acc = _dot_nt(h, w_ref[lo:lo + 2 * DK, :])
            if pair >= HEADS // 2:
                acc = acc * (DK ** -0.5)
            for half in range(2):
                blk = acc[:, DK * half:DK * (half + 1)]
                o_ref[:, lo + DK * half:lo + DK * (half + 1)] = _rope(blk, cos, sin).astype(BF16)
        rope_end = ROPE_COL0 + 2 * HEADS * DK
        for lo, hi in ((0, ROPE_COL0), (rope_end, PROJ_TN)):
            if lo < hi:
                o_ref[:, lo:hi] = _dot_nt(h, w_ref[lo:hi, :]).astype(BF16)


def _proj(x2, sh, sc, g, w_main, w_lr, cos, sin, tm, rows_per_mod):
    t = x2.shape[0]
    tn = PROJ_TN
    tiles_per_mod = rows_per_mod // tm
    mod_map = lambda i, j: (i // tiles_per_mod, 0, 0)
    pos_tiles = cos.shape[0] // tm
    pos_map = lambda i, j: (i % pos_tiles, 0)
    return pl.pallas_call(
        _proj_body,
        grid=(t // tm, PROJ_W // tn),
        in_specs=[pl.BlockSpec((tm, D_MODEL), lambda i, j: (i, 0)),
                  pl.BlockSpec((1, 1, D_MODEL), mod_map),
                  pl.BlockSpec((1, 1, D_MODEL), mod_map),
                  pl.BlockSpec((1, D_MODEL), lambda i, j: (0, 0)),
                  pl.BlockSpec((tn, D_MODEL), lambda i, j: (j, 0)),
                  pl.BlockSpec((LANES, D_MODEL), lambda i, j: (0, 0)),
                  pl.BlockSpec((tm, DK), pos_map), pl.BlockSpec((tm, DK), pos_map)],
        out_specs=[pl.BlockSpec((tm, tn), lambda i, j: (i, j)),
                   pl.BlockSpec((tm, LANES), lambda i, j: (i, 0))],
        out_shape=[jax.ShapeDtypeStruct((t, PROJ_W), BF16),
                   jax.ShapeDtypeStruct((t, LANES), F32)],
        scratch_shapes=[pltpu.VMEM((tm, D_MODEL), BF16)],
        compiler_params=_cparams(("parallel", "arbitrary")),
        name="proj",
    )(x2, sh, sc, g, w_main, w_lr, cos, sin)


def _gla_consts():
    c, s = GLA_CHUNK, GLA_SUB
    msk = {}
    for rev in (False, True):
        cols = []
        for i in range(c // s):
            keys = np.arange(c)[_key_rows(i, rev)][None, :]
            t = np.arange(c)[:, None]
            visible = (keys > t) if rev else (keys <= t)
            cols.append(np.where(t // s == i, visible, False))
        msk[rev] = np.concatenate(cols, axis=1).astype(np.float32)
    return msk


def _key_rows(i, rev):
    return slice(GLA_SUB * i, GLA_CHUNK) if rev else slice(0, GLA_SUB * (i + 1))


def _log_gate(lr, w, b):
    z = _dot(lr.astype(BF16), w.astype(BF16)) + b
    return -(jnp.maximum(-z, 0.0) + jnp.log(1.0 + jnp.exp(-jnp.abs(z)))) * (1.0 / GLA_TAU)


def _subchunk_scan(la, rev):
    c = la.shape[0]
    pos = lax.broadcasted_iota(jnp.int32, la.shape, 0) % GLA_SUB
    w = la
    step = 1
    while step < GLA_SUB:
        if rev:
            w = w + jnp.where(pos < GLA_SUB - step, pltpu.roll(w, c - step, 0), 0.0)
        else:
            w = w + jnp.where(pos >= step, pltpu.roll(w, step, 0), 0.0)
        step *= 2
    return w


def _gla_intra_exact(q, kf, v, b, rev, scratch):
    q_s, b_s, o_s = scratch
    c = GLA_CHUNK
    q_s[...] = q.astype(F32) * (DK ** -0.5)
    b_s[...] = b
    vf = v.astype(F32)
    row = lax.broadcasted_iota(jnp.int32, (c, 1), 0)

    def body(t, carry):
        qt = q_s[pl.ds(t, 1), :]
        bt = b_s[pl.ds(t, 1), :]
        visible = (row > t) if rev else (row <= t)
        decay = jnp.exp(jnp.where(visible, bt - b, -jnp.inf))
        col = jnp.sum(kf * decay * qt, axis=1, keepdims=True)
        o_s[pl.ds(t, 1), :] = jnp.sum(col * vf, axis=0, keepdims=True)
        return carry

    lax.fori_loop(0, c, body, 0)
    return o_s[...]


def _gla_chunk(q, k, v, la, state_ref, mask, rev, exact_scratch=None):
    c, s = GLA_CHUNK, GLA_SUB
    nsub = c // s
    w = _subchunk_scan(la, rev)
    tot = [w[s * i:s * i + 1] if rev else w[s * (i + 1) - 1:s * (i + 1)] for i in range(nsub)]
    anchors = [None] * nsub
    acc = jnp.zeros_like(tot[0])
    worst = jnp.zeros_like(tot[0])
    for i in (reversed(range(nsub)) if rev else range(nsub)):
        anchors[i] = acc
        acc = acc + tot[i]
        worst = jnp.maximum(worst, -tot[i])
    b_end = acc
    ref = jnp.concatenate([jnp.broadcast_to(a, (s, DK)) for a in anchors], axis=0)
    b = w + ref

    qt = q.astype(F32) * (DK ** -0.5) * jnp.exp(w)
    kf = k.astype(F32)
    if exact_scratch is None:
        kstack = jnp.concatenate([kf[_key_rows(i, rev)] * jnp.exp(anchors[i] - b[_key_rows(i, rev)])
                                  for i in range(nsub)], axis=0).astype(BF16)
        raw = _dot_nt(qt.astype(BF16), kstack)
    kst = kf * jnp.exp(b_end - b)
    xt = jnp.concatenate([kst, jnp.broadcast_to(jnp.exp(b_end), (c, DK))], axis=0).T
    update = _dot(xt[:, :c].astype(BF16), v)
    state = state_ref[...]
    inter = _dot((qt * jnp.exp(ref)).astype(BF16), state.astype(BF16))
    state_ref[...] = xt[:, c:c + 1] * state + update
    yield
    if exact_scratch is None:
        vstack = jnp.concatenate([v[_key_rows(i, rev)] for i in range(nsub)], axis=0)
        intra = _dot(jnp.where(mask != 0.0, raw, 0.0).astype(BF16), vstack)
    else:
        intra = _gla_intra_exact(q, kf, v, b, rev, exact_scratch)
    return inter + intra, worst


def _round_robin(gens):
    results = [None] * len(gens)
    live = list(range(len(gens)))
    while live:
        for g in list(live):
            try:
                next(gens[g])
            except StopIteration as done:
                results[g] = done.value
                live.remove(g)
    return results


def _run_phased(chunks, emit, lag):
    def finish(tag, gen):
        try:
            next(gen)
        except StopIteration as done:
            emit(tag, done.value)
            return
        raise AssertionError("chunk generator has more than two phases")

    pending = []
    for tag, gen in chunks:
        next(gen)
        pending.append((tag, gen))
        if len(pending) > lag:
            finish(*pending.pop(0))
    for item in pending:
        finish(*item)


def _gla_body(qf_ref, kf_ref, vf_ref, lrf_ref, qb_ref, kb_ref, vb_ref, lrb_ref,
              wf_ref, bf_ref, wb_ref, bb_ref, mskf_ref, mskb_ref,
              s0f_ref, s0b_ref, of_ref, ob_ref, sf_ref, sb_ref, rng_ref, *exact_scratch, nchunks):
    @pl.when(pl.program_id(2) == 0)
    def _():
        sf_ref[...] = s0f_ref[...]
        sb_ref[...] = s0b_ref[...]

    c = GLA_CHUNK
    scratch = exact_scratch or None
    la_f = _log_gate(lrf_ref[...], wf_ref[...], bf_ref[...])
    la_b = _log_gate(lrb_ref[...], wb_ref[...], bb_ref[...])
    mskf, mskb = mskf_ref[...], mskb_ref[...]
    st_f, st_b = sf_ref.at[0, 0], sb_ref.at[0, 0]
    worst = [jnp.zeros((1, DK), F32)]

    def chunks():
        for n in range(nchunks):
            rf = slice(c * n, c * (n + 1))
            yield (of_ref, rf), _gla_chunk(qf_ref[rf, :], kf_ref[rf, :], vf_ref[rf, :], la_f[rf, :], st_f, mskf,
                                           False, scratch)
            m = nchunks - 1 - n
            rb = slice(c * m, c * (m + 1))
            yield (ob_ref, rb), _gla_chunk(qb_ref[rb, :], kb_ref[rb, :], vb_ref[rb, :], la_b[rb, :], st_b, mskb,
                                           True, scratch)

    def emit(tag, result):
        o_ref, rows = tag
        o_ref[rows, :] = result[0].astype(BF16)
        worst[0] = jnp.maximum(worst[0], result[1])

    _run_phased(chunks(), emit, GLA_PHASE_LAG)
    rng_ref[0, 0, 0] = jnp.broadcast_to(worst[0], (8, DK))


def _gla(p, lr, wlr_f, blr_f, wlr_b, blr_b, s0f, s0b, batch, seq, tb, exact=False):
    nb = seq // tb
    msk = _gla_consts()
    mskf, mskb = jnp.asarray(msk[False]), jnp.asarray(msk[True])

    def fwd(col0, width):
        return lambda b, h, i: (b * nb + i, col0 // width + h)

    def bwd(col0, width):
        return lambda b, h, i: (b * nb + nb - 1 - i, col0 // width + h)

    const2 = lambda b, h, i: (0, 0)
    headcol = lambda b, h, i: (0, h)
    st_map = lambda b, h, i: (b, h, 0, 0)
    st_spec = pl.BlockSpec((1, 1, DK, DV), st_map)
    st_shape = jax.ShapeDtypeStruct((batch, HEADS, DK, DV), F32)
    o_shape = jax.ShapeDtypeStruct((batch * seq, HEADS * DV), BF16)
    return pl.pallas_call(
        functools.partial(_gla_body, nchunks=tb // GLA_CHUNK),
        grid=(batch, HEADS, nb),
        in_specs=[pl.BlockSpec((tb, DK), fwd(COL_GQ, DK)),
                  pl.BlockSpec((tb, DK), fwd(COL_GK, DK)),
                  pl.BlockSpec((tb, DV), fwd(COL_GV, DV)),
                  pl.BlockSpec((tb, LANES), lambda b, h, i: (b * nb + i, 0)),
                  pl.BlockSpec((tb, DK), bwd(COL_GQ, DK)),
                  pl.BlockSpec((tb, DK), bwd(COL_GK, DK)),
                  pl.BlockSpec((tb, DV), bwd(COL_GV, DV)),
                  pl.BlockSpec((tb, LANES), lambda b, h, i: (b * nb + nb - 1 - i, 0)),
                  pl.BlockSpec((LANES, DK), headcol),
                  pl.BlockSpec((1, DK), headcol),
                  pl.BlockSpec((LANES, DK), headcol),
                  pl.BlockSpec((1, DK), headcol),
                  pl.BlockSpec(mskf.shape, const2),
                  pl.BlockSpec(mskb.shape, const2),
                  st_spec, st_spec],
        out_specs=[pl.BlockSpec((tb, DV), lambda b, h, i: (b * nb + i, h)),
                   pl.BlockSpec((tb, DV), lambda b, h, i: (b * nb + nb - 1 - i, h)),
                   st_spec, st_spec,
                   pl.BlockSpec((1, 1, 1, 8, DK), lambda b, h, i: (b, h, i, 0, 0))],
        out_shape=[o_shape, o_shape, st_shape, st_shape,
                   jax.ShapeDtypeStruct((batch, HEADS, nb, 8, DK), F32)],
        scratch_shapes=([pltpu.VMEM((GLA_CHUNK, DK), F32), pltpu.VMEM((GLA_CHUNK, DK), F32),
                         pltpu.VMEM((GLA_CHUNK, DV), F32)] if exact else []),
        compiler_params=_cparams(("parallel", "parallel", "arbitrary")),
        name="gla_exact" if exact else "gla",
    )(p, p, p, lr, p, p, p, lr, wlr_f, blr_f, wlr_b, blr_b, mskf, mskb, s0f, s0b)


def _ret_consts():
    c = RET_CHUNK
    hh = np.arange(HEADS, dtype=np.float64)
    lg = {False: np.log1p(-np.exp2(-5.0 - hh)), True: np.log1p(-np.exp2(-5.5 - hh))}
    t = np.arange(c, dtype=np.float64)[:, None]
    u = np.arange(c, dtype=np.float64)[None, :]
    pos = np.arange(c, dtype=np.float64)
    out = {}
    for rev in (False, True):
        g = lg[rev][:, None, None]
        if not rev:
            dmat = np.where(u <= t, np.exp((t - u) * g), 0.0)
            qd = np.exp((pos + 1.0)[None, :] * lg[rev][:, None])
            kd = np.exp((c - 1.0 - pos)[None, :] * lg[rev][:, None])
        else:
            dmat = np.where(u > t, np.exp((u - t) * g), 0.0)
            qd = np.exp((c - pos)[None, :] * lg[rev][:, None])
            kd = np.exp(pos[None, :] * lg[rev][:, None])
        cd = np.exp(c * lg[rev])
        out[rev] = (dmat.astype(np.float32),
                    np.repeat(qd[:, :, None], DV, axis=2).astype(np.float32),
                    np.repeat(kd[:, :, None], DK, axis=2).astype(np.float32),
                    np.repeat(cd[:, None, None], DV, axis=2).astype(np.float32))
    return out


def _ret_chunk(q, k, v, state_ref, dmat, qd, kd, cd):
    raw = _dot_nt(q, k)
    update = _dot((k.astype(F32) * kd).T.astype(BF16), v)
    state = state_ref[...]
    inter = _dot(q, state.astype(BF16)) * qd
    state_ref[...] = cd * state + update
    yield
    return inter + _dot((raw * dmat).astype(BF16), v)


def _ret_body(qf_ref, kf_ref, vf_ref, qb_ref, kb_ref, vb_ref,
              dmf_ref, qdf_ref, kdf_ref, cdf_ref, dmb_ref, qdb_ref, kdb_ref, cdb_ref,
              s0f_ref, s0b_ref, of_ref, ob_ref, sf_ref, sb_ref, *, nchunks):
    @pl.when(pl.program_id(2) == 0)
    def _():
        sf_ref[...] = s0f_ref[...]
        sb_ref[...] = s0b_ref[...]

    c = RET_CHUNK
    cf = (dmf_ref[0], qdf_ref[0], kdf_ref[0], cdf_ref[0])
    cb = (dmb_ref[0], qdb_ref[0], kdb_ref[0], cdb_ref[0])
    st_f, st_b = sf_ref.at[0, 0], sb_ref.at[0, 0]
    def chunks():
        for n in range(nchunks):
            rf = slice(c * n, c * (n + 1))
            yield (of_ref, rf), _ret_chunk(qf_ref[rf, :], kf_ref[rf, :], vf_ref[rf, :], st_f, *cf)
            m = nchunks - 1 - n
            rb = slice(c * m, c * (m + 1))
            yield (ob_ref, rb), _ret_chunk(qb_ref[rb, :], kb_ref[rb, :], vb_ref[rb, :], st_b, *cb)

    def emit(tag, out):
        o_ref, rows = tag
        o_ref[rows, :] = out.astype(BF16)

    _run_phased(chunks(), emit, RET_PHASE_LAG)


def _ret(p, s0f, s0b, batch, seq, tb):
    nb = seq // tb
    consts = _ret_consts()

    def fwd(col0, width):
        return lambda b, h, i: (b * nb + i, col0 // width + h)

    def bwd(col0, width):
        return lambda b, h, i: (b * nb + nb - 1 - i, col0 // width + h)

    head3 = lambda b, h, i: (h, 0, 0)
    st_spec = pl.BlockSpec((1, 1, DK, DV), lambda b, h, i: (b, h, 0, 0))
    st_shape = jax.ShapeDtypeStruct((batch, HEADS, DK, DV), F32)
    o_shape = jax.ShapeDtypeStruct((batch * seq, HEADS * DV), BF16)
    const_specs = []
    const_args = []
    for rev in (False, True):
        for a in consts[rev]:
            const_specs.append(pl.BlockSpec((1,) + a.shape[1:], head3))
            const_args.append(jnp.asarray(a))
    return pl.pallas_call(
        functools.partial(_ret_body, nchunks=tb // RET_CHUNK),
        grid=(batch, HEADS, nb),
        in_specs=[pl.BlockSpec((tb, DK), fwd(COL_RQ, DK)),
                  pl.BlockSpec((tb, DK), fwd(COL_RK, DK)),
                  pl.BlockSpec((tb, DV), fwd(COL_RV, DV)),
                  pl.BlockSpec((tb, DK), bwd(COL_RQ, DK)),
                  pl.BlockSpec((tb, DK), bwd(COL_RK, DK)),
                  pl.BlockSpec((tb, DV), bwd(COL_RV, DV))]
                 + const_specs + [st_spec, st_spec],
        out_specs=[pl.BlockSpec((tb, DV), lambda b, h, i: (b * nb + i, h)),
                   pl.BlockSpec((tb, DV), lambda b, h, i: (b * nb + nb - 1 - i, h)),
                   st_spec, st_spec],
        out_shape=[o_shape, o_shape, st_shape, st_shape],
        compiler_params=_cparams(("parallel", "parallel", "arbitrary")),
        name="ret",
    )(p, p, p, p, p, p, *const_args, s0f, s0b)


def _rope_tables(seq):
    n = DK // 4
    inv = np.float32(ROPE_BASE) ** (-np.arange(n, dtype=np.float32) / np.float32(n))
    pos = np.arange(seq)
    ar = (pos // GRID_W).astype(np.float32)[:, None] * inv[None, :]
    ac = (pos % GRID_W).astype(np.float32)[:, None] * inv[None, :]
    cos = np.concatenate([np.cos(ar), np.cos(ar), np.cos(ac), np.cos(ac)], axis=1)
    sin = np.concatenate([-np.sin(ar), np.sin(ar), -np.sin(ac), np.sin(ac)], axis=1)
    return jnp.asarray(cos, F32), jnp.asarray(sin, F32)


def _route(logits_t):
    g = [logits_t[i:i + 1] for i in range(N_GROUPS)]
    gmax = jnp.maximum(jnp.maximum(g[0], g[1]), jnp.maximum(g[2], g[3]))
    gsel = jnp.where(g[0] == gmax, 0, jnp.where(g[1] == gmax, 1, jnp.where(g[2] == gmax, 2, 3)))
    gsum = (jnp.exp(g[0] - gmax) + jnp.exp(g[1] - gmax)) + (jnp.exp(g[2] - gmax) + jnp.exp(g[3] - gmax))
    gw = 1.0 / gsum
    e = [logits_t[EXP_ROW0 + EXPERTS_PER_GROUP * i:EXP_ROW0 + EXPERTS_PER_GROUP * (i + 1)]
         for i in range(N_GROUPS)]
    el = jnp.where(gsel == 0, e[0], jnp.where(gsel == 1, e[1], jnp.where(gsel == 2, e[2], e[3])))
    row = lax.broadcasted_iota(jnp.int32, el.shape, 0).astype(F32)
    none = float(EXPERTS_PER_GROUP)
    m1 = jnp.max(el, axis=0, keepdims=True)
    i1 = jnp.min(jnp.where(el == m1, row, none), axis=0, keepdims=True)
    el2 = jnp.where(row == i1, -jnp.inf, el)
    m2 = jnp.max(el2, axis=0, keepdims=True)
    i2 = jnp.min(jnp.where(el2 == m2, row, none), axis=0, keepdims=True)
    r = jnp.exp(m2 - m1)
    w1 = gw / (1.0 + r)
    w2 = gw * r / (1.0 + r)
    base = gsel * EXPERTS_PER_GROUP
    ids = jnp.concatenate([base + i1.astype(jnp.int32), base + i2.astype(jnp.int32)], axis=0)
    return ids, jnp.concatenate([w1, w2], axis=0)


def _store_token_tiled(ref, val):
    m = val.shape[0]
    for j in range(FEAT_TILES):
        ref[pl.ds(j, m, stride=FEAT_TILES), :] = val[:, LANES * j:LANES * (j + 1)]


def _load_token_tiled(ref, m):
    return jnp.concatenate([ref[pl.ds(j, m, stride=FEAT_TILES), :] for j in range(FEAT_TILES)], axis=1)


def _merge_body(gf_ref, gb_ref, rf_ref, rb_ref, gg_ref, rg_ref, mg_ref, mr_ref, x_ref,
                g1_ref, sh2_ref, nfs_ref, gn_ref, wg_ref, wr_ref, wo_ref, wrt_ref, brt_ref, upper_ref,
                h_ref, hn_ref, ids_ref, ew_ref, cnt_ref, wg_s, wr_s, wo_s, stage, sem):
    @pl.when(pl.program_id(0) == 0)
    def _():
        cnt_ref[...] = jnp.zeros_like(cnt_ref)
        for w_hbm, w_s in ((wg_ref, wg_s), (wr_ref, wr_s), (wo_ref, wo_s)):
            copy = pltpu.make_async_copy(w_hbm, stage, sem.at[0])
            copy.start()
            copy.wait()
            w_s[...] = stage[...].astype(BF16)

    def rows_to_logits(r0, n):
        rows = pl.ds(r0, n)
        og = (gf_ref[rows, :] + gb_ref[rows, :]).astype(F32)
        orr = (rf_ref[rows, :] + rb_ref[rows, :]).astype(F32)
        gparts, rparts = [], []
        for hh in range(HEADS):
            seg = og[:, DV * hh:DV * (hh + 1)]
            ms = jnp.mean(seg * seg, axis=-1, keepdims=True)
            gparts.append(seg * lax.rsqrt(ms + NORM_EPS))
            seg = orr[:, DV * hh:DV * (hh + 1)]
            mu = jnp.mean(seg, axis=-1, keepdims=True)
            cen = seg - mu
            var = jnp.mean(cen * cen, axis=-1, keepdims=True)
            rparts.append(cen * lax.rsqrt(var + NORM_EPS))
        o_gla = (jnp.concatenate(gparts, axis=1) * gn_ref[...]).astype(BF16) * _silu(gg_ref[rows, :])
        o_ret = jnp.concatenate(rparts, axis=1).astype(BF16) * _silu(rg_ref[rows, :])
        a_gla = _dot(o_gla, wg_s[...])
        a_ret = _dot(o_ret, wr_s[...])
        yield
        y = _sigmoid(mg_ref[rows, :]) * a_gla.astype(BF16) + _sigmoid(mr_ref[rows, :]) * a_ret.astype(BF16)
        out = _dot(y, wo_s[...])
        yield
        h = x_ref[rows, :] + g1_ref[0] * out
        h_ref[rows, :] = h
        ms = jnp.mean(h * h, axis=-1, keepdims=True)
        hn = h * lax.rsqrt(ms + NORM_EPS) * nfs_ref[0] + sh2_ref[0]
        _store_token_tiled(hn_ref.at[pl.ds(r0 * FEAT_TILES, n * FEAT_TILES)], hn)
        hn_hi = hn.astype(BF16)
        hn_lo = (hn - hn_hi.astype(F32)).astype(BF16)
        both_w = _dot(hn_hi, wrt_ref[...])
        lo_w = _dot(hn_lo, wrt_ref[:, :ROUTER_W])
        yield
        return (both_w[:, :ROUTER_W] + both_w[:, ROUTER_W:] + lo_w) + brt_ref[...]

    tm = x_ref.shape[0]
    n = tm // MERGE_SPLIT
    logits = jnp.concatenate(_round_robin([rows_to_logits(g * n, n) for g in range(MERGE_SPLIT)]), axis=0)
    ids, ew = _route(logits.T)

    erow = lax.broadcasted_iota(jnp.int32, (N_EXPERTS, tm), 0)
    oh0 = jnp.where(erow == ids[0:1], 1.0, 0.0)
    oh1 = jnp.where(erow == ids[1:2], 1.0, 0.0)
    both = oh0 + oh1
    before = _dot(both.astype(BF16), upper_ref[...]) + cnt_ref[:, 0:1].astype(F32)
    rank0 = jnp.sum(oh0 * before, axis=0, keepdims=True)
    rank1 = jnp.sum(oh1 * before, axis=0, keepdims=True)
    total = cnt_ref[:, 0:1] + jnp.sum(both, axis=1, keepdims=True).astype(jnp.int32)
    cnt_ref[...] = jnp.broadcast_to(total, cnt_ref.shape)
    ids_ref[...] = jnp.concatenate([ids, rank0.astype(jnp.int32), rank1.astype(jnp.int32),
                                    jnp.zeros((4, tm), jnp.int32)], axis=0)
    ew_ref[...] = jnp.concatenate([ew, jnp.zeros((6, tm), F32)], axis=0)


def _merge(gf, gb, rf, rb, p, x2, g1, sh2, nfs, gn, wg, wr, wo, wrt, brt, seq, tm):
    t = x2.shape[0]
    tiles_per_batch = seq // tm
    row = lambda i: (i, 0)
    mod = lambda i: (i // tiles_per_batch, 0, 0)
    const = lambda i: (0, 0)
    tok = pl.BlockSpec((tm, D_MODEL), row)
    vec = pl.BlockSpec((1, D_MODEL), const)
    modspec = pl.BlockSpec((1, 1, D_MODEL), mod)
    wspec = pl.BlockSpec(memory_space=pl.ANY)
    upper = jnp.asarray(np.triu(np.ones((tm, tm), np.float32), 1), BF16)

    def pcol(col0):
        return pl.BlockSpec((tm, D_MODEL), lambda i: (i, col0 // D_MODEL))

    return pl.pallas_call(
        _merge_body,
        grid=(t // tm,),
        in_specs=[tok, tok, tok, tok, pcol(COL_GG), pcol(COL_RG), pcol(COL_MG), pcol(COL_MR), tok,
                  modspec, modspec, modspec, vec, wspec, wspec, wspec,
                  pl.BlockSpec((D_MODEL, 2 * ROUTER_W), const), pl.BlockSpec((1, ROUTER_W), const),
                  pl.BlockSpec((tm, tm), const)],
        out_specs=[tok, pl.BlockSpec((tm * FEAT_TILES, LANES), row),
                   pl.BlockSpec((8, tm), lambda i: (0, i)), pl.BlockSpec((8, tm), lambda i: (0, i)),
                   pl.BlockSpec((N_EXPERTS, LANES), const)],
        out_shape=[jax.ShapeDtypeStruct((t, D_MODEL), F32), jax.ShapeDtypeStruct((t * FEAT_TILES, LANES), F32),
                   jax.ShapeDtypeStruct((8, t), jnp.int32), jax.ShapeDtypeStruct((8, t), F32),
                   jax.ShapeDtypeStruct((N_EXPERTS, LANES), jnp.int32)],
        scratch_shapes=[pltpu.VMEM((D_MODEL, D_MODEL), BF16)] * 3
                       + [pltpu.VMEM((D_MODEL, D_MODEL), F32), pltpu.SemaphoreType.DMA((1,))],
        compiler_params=_cparams(("arbitrary",)),
        name="merge",
    )(gf, gb, rf, rb, p, p, p, p, x2, g1, sh2, nfs, gn, wg, wr, wo, wrt, brt, upper)


GATHER_UNROLL = 8


def _aligned(tok):
    off = tok * FEAT_TILES
    return off if isinstance(off, int) else pl.multiple_of(off, FEAT_TILES)


def _token_copy(src_hbm, dst, sem, src_tok, dst_tok):
    return pltpu.make_async_copy(src_hbm.at[pl.ds(_aligned(src_tok), FEAT_TILES), :],
                                 dst.at[pl.ds(_aligned(dst_tok), FEAT_TILES), :], sem)


def _wait_tokens(src_hbm, dst, sem):
    pltpu.make_async_copy(src_hbm.at[pl.ds(0, dst.shape[0]), :], dst, sem).wait()


GATHER_BUFS = 3
GATHER_AHEAD = GATHER_BUFS - 1


def _issue_rows(n, start_row, unrolled):
    if unrolled:
        for r in range(n):
            start_row(r, r % 2)
    else:
        def body(g, carry):
            for u in range(GATHER_UNROLL):
                start_row(g * GATHER_UNROLL + u, u % 2)
            return carry
        lax.fori_loop(0, n // GATHER_UNROLL, body, 0)


EXPERT_GROUP = 2


EXPERT_W_SLOTS = 2


def _expert_body(rank_ref, eseq_ref, src0_ref, stok_ref, meta_ref, hn_hbm, wg_hbm, wu_hbm, wd_hbm, y_ref, *scratch):
    bufs, sems = scratch[:GATHER_BUFS], scratch[GATHER_BUFS]
    stage, w_s, wsem = scratch[GATHER_BUFS + 1:GATHER_BUFS + 4], scratch[GATHER_BUFS + 4:GATHER_BUFS + 7], scratch[-1]
    w_hbm = (wg_hbm, wu_hbm, wd_hbm)
    i = pl.program_id(0)
    nused, nranks = meta_ref[0], meta_ref[1]
    nsteps = (nused + EXPERT_GROUP - 1) // EXPERT_GROUP
    last = stok_ref.shape[0] - 1
    rows = MOE_BLOCK * FEAT_TILES

    def issue(step, s, unrolled):
        bases = [src0_ref[step * EXPERT_GROUP + g] for g in range(EXPERT_GROUP)]

        def start(g, r, priority):
            tok = stok_ref[jnp.minimum(bases[g] + r, last)]
            _token_copy(hn_hbm, bufs[s], sems.at[s], tok, g * MOE_BLOCK + r).start(priority=priority)

        if unrolled:
            _issue_rows(EXPERT_GROUP * MOE_BLOCK, lambda j, p: start(j // MOE_BLOCK, j % MOE_BLOCK, p), True)
        else:
            for g in range(EXPERT_GROUP):
                _issue_rows(MOE_BLOCK, lambda r, p, g=g: start(g, r, p), False)

    def weight_copies(rank):
        slot = rank % EXPERT_W_SLOTS
        expert = eseq_ref[rank]
        return [pltpu.make_async_copy(w_hbm[k].at[expert], stage[k].at[slot], wsem.at[slot]) for k in range(3)]

    def activate(rank):
        slot = rank % EXPERT_W_SLOTS
        for copy in weight_copies(rank):
            copy.wait()
        for k in range(3):
            w_s[k][slot] = stage[k][slot].astype(BF16)

        @pl.when(rank + 1 < nranks)
        def _():
            for copy in weight_copies(rank + 1):
                copy.start()

    @pl.when(i == 0)
    def _():
        for b in range(GATHER_AHEAD):
            issue(b, b, False)

        @pl.when(nranks > 0)
        def _():
            for copy in weight_copies(0):
                copy.start()

    for g in range(EXPERT_GROUP):
        blk = i * EXPERT_GROUP + g
        rank = rank_ref[blk]
        first_of_run = jnp.logical_or(blk == 0, rank != rank_ref[jnp.maximum(blk - 1, 0)])

        @pl.when(jnp.logical_and(blk < nused, first_of_run))
        def _():
            activate(rank)

    def block_phases(cur, g):
        part = pl.ds(g * rows, rows)
        slot = rank_ref[i * EXPERT_GROUP + g] % EXPERT_W_SLOTS
        xb = _load_token_tiled(bufs[cur].at[part], MOE_BLOCK).astype(BF16)
        gate = _dot(xb, w_s[0][slot])
        up = _dot(xb, w_s[1][slot])
        yield
        y = _dot((_silu(gate) * up).astype(BF16), w_s[2][slot])
        yield
        _store_token_tiled(y_ref.at[part], y)

    for cur in range(GATHER_BUFS):
        mine = i % GATHER_BUFS == cur

        @pl.when(jnp.logical_and(mine, i < nsteps))
        def _():
            _wait_tokens(hn_hbm, bufs[cur], sems.at[cur])
            issue(i + GATHER_AHEAD, (cur + GATHER_AHEAD) % GATHER_BUFS, True)
            _round_robin([block_phases(cur, g) for g in range(EXPERT_GROUP)])

        @pl.when(jnp.logical_and(mine, jnp.logical_and(i >= nsteps, i < nsteps + GATHER_AHEAD)))
        def _():
            _wait_tokens(hn_hbm, bufs[cur], sems.at[cur])

    @pl.when(i >= nsteps)
    def _():
        y_ref[...] = jnp.zeros_like(y_ref)


def _experts(block_rank, expert_seq, src0, stok, meta, hn, w_gate, w_up, w_down, nblk):
    rows = EXPERT_GROUP * MOE_BLOCK * FEAT_TILES
    w_shapes = ((D_MODEL, EXPERT_FF), (D_MODEL, EXPERT_FF), (EXPERT_FF, D_MODEL))
    any_spec = pl.BlockSpec(memory_space=pl.ANY)
    return pl.pallas_call(
        _expert_body,
        grid_spec=pltpu.PrefetchScalarGridSpec(
            num_scalar_prefetch=5,
            grid=(nblk // EXPERT_GROUP,),
            in_specs=[any_spec, any_spec, any_spec, any_spec],
            out_specs=pl.BlockSpec((rows, LANES), lambda i, *prefetch: (i, 0)),
            scratch_shapes=[pltpu.VMEM((rows, LANES), F32)] * GATHER_BUFS
                           + [pltpu.SemaphoreType.DMA((GATHER_BUFS,))]
                           + [pltpu.VMEM((EXPERT_W_SLOTS,) + shape, F32) for shape in w_shapes]
                           + [pltpu.VMEM((EXPERT_W_SLOTS,) + shape, BF16) for shape in w_shapes]
                           + [pltpu.SemaphoreType.DMA((EXPERT_W_SLOTS,))]),
        out_shape=jax.ShapeDtypeStruct((nblk // EXPERT_GROUP * rows, LANES), F32),
        compiler_params=_cparams(("arbitrary",)),
        name="experts",
    )(block_rank, expert_seq, src0, stok, meta, hn, w_gate, w_up, w_down)


def _combine_body(pos_ref, y_hbm, h_ref, ew_ref, g2_ref, nf_ref, o_ref, *scratch):
    i = pl.program_id(0)
    n = pl.num_programs(0)
    tm = COMBINE_TM
    t = pos_ref.shape[0] // TOP_K
    bufs, sems = scratch[:GATHER_BUFS], scratch[GATHER_BUFS]

    def issue(tile, s, unrolled):
        def start_row(j, priority):
            k, r = j % TOP_K, j // TOP_K
            _token_copy(y_hbm, bufs[s].at[k], sems.at[s], pos_ref[k * t + tile * tm + r], r).start(priority=priority)
        _issue_rows(tm * TOP_K, start_row, unrolled)

    def wait(s):
        for k in range(TOP_K):
            _wait_tokens(y_hbm, bufs[s].at[k], sems.at[s])

    @pl.when(i == 0)
    def _():
        for b in range(GATHER_AHEAD):
            issue(b, b, False)

    for cur in range(GATHER_BUFS):
        @pl.when(i % GATHER_BUFS == cur)
        def _():
            wait(cur)
            issue(jnp.minimum(i + GATHER_AHEAD, n - 1), (cur + GATHER_AHEAD) % GATHER_BUFS, True)
            wt = jnp.concatenate([ew_ref[...]] * (LANES // 8), axis=0).T
            moe = (wt[:, 0:1] * _load_token_tiled(bufs[cur].at[0], tm)
                   + wt[:, 1:2] * _load_token_tiled(bufs[cur].at[1], tm))
            h = h_ref[...] + g2_ref[0] * moe
            ms = jnp.mean(h * h, axis=-1, keepdims=True)
            o_ref[...] = h * lax.rsqrt(ms + NORM_EPS) * nf_ref[...]

        @pl.when(jnp.logical_and(i % GATHER_BUFS == cur, i == n - 1))
        def _():
            for ahead in range(1, GATHER_BUFS):
                wait((cur + ahead) % GATHER_BUFS)


def _combine(pos, y_pad, h, ew, g2, nf, seq):
    t = h.shape[0]
    tm = COMBINE_TM
    tiles_per_batch = seq // tm
    ybuf = pltpu.VMEM((TOP_K, tm * FEAT_TILES, LANES), F32)
    return pl.pallas_call(
        _combine_body,
        grid_spec=pltpu.PrefetchScalarGridSpec(
            num_scalar_prefetch=1,
            grid=(t // tm,),
            in_specs=[pl.BlockSpec(memory_space=pl.ANY),
                      pl.BlockSpec((tm, D_MODEL), lambda i, pos: (i, 0)),
                      pl.BlockSpec((8, tm), lambda i, pos: (0, i)),
                      pl.BlockSpec((1, 1, D_MODEL), lambda i, pos: (i // tiles_per_batch, 0, 0)),
                      pl.BlockSpec((1, D_MODEL), lambda i, pos: (0, 0))],
            out_specs=pl.BlockSpec((tm, D_MODEL), lambda i, pos: (i, 0)),
            scratch_shapes=[ybuf] * GATHER_BUFS + [pltpu.SemaphoreType.DMA((GATHER_BUFS,))]),
        out_shape=jax.ShapeDtypeStruct((t, D_MODEL), F32),
        compiler_params=_cparams(("arbitrary",)),
        name="combine",
    )(pos, y_pad, h, ew, g2, nf)


def _dispatch_indices(ids, ranks, counts, t):
    a = t * TOP_K
    nblk = a // MOE_BLOCK + N_EXPERTS + GATHER_AHEAD * EXPERT_GROUP
    experts = jnp.arange(N_EXPERTS, dtype=jnp.int32)
    starts = jnp.cumsum(counts) - counts
    padded = (counts + MOE_BLOCK - 1) // MOE_BLOCK * MOE_BLOCK
    pends = jnp.cumsum(padded)
    pstarts = pends - padded
    block_start = jnp.arange(nblk, dtype=jnp.int32) * MOE_BLOCK
    block_e = jnp.minimum(jnp.sum((block_start[:, None] >= pends[None, :]).astype(jnp.int32), axis=1),
                          N_EXPERTS - 1)
    of_block = (block_e[:, None] == experts[None, :]).astype(jnp.int32)
    src0 = block_start + jnp.sum(of_block * (starts - pstarts)[None, :], axis=1)
    nused = pends[-1:] // MOE_BLOCK
    blocks = jnp.arange(nblk, dtype=jnp.int32)
    used = blocks < nused
    starts_run = jnp.logical_and(used, jnp.concatenate([jnp.ones((1,), bool), block_e[1:] != block_e[:-1]]))
    nranks = jnp.sum(starts_run.astype(jnp.int32), keepdims=True)
    run = jnp.cumsum(starts_run.astype(jnp.int32)) - 1
    block_rank = jnp.clip(run, 0, jnp.maximum(nranks - 1, 0))
    run_ids = jnp.arange(N_EXPERTS + 1, dtype=jnp.int32)
    expert_seq = jnp.sum(jnp.where(jnp.logical_and(starts_run[:, None], run[:, None] == run_ids[None, :]),
                                   block_e[:, None], 0), axis=0)
    meta = jnp.concatenate([nused, nranks])
    tok = jnp.arange(t, dtype=jnp.int32)[None, :]
    slot = jnp.arange(TOP_K, dtype=jnp.int32)[:, None]
    key = ids * a + tok * TOP_K + slot
    stok = (jnp.sort(key.reshape(a)) % a) // TOP_K
    ids_d = ids.reshape(a // LANES, LANES)
    pstart_of = jnp.zeros_like(ids_d)
    for e in range(N_EXPERTS):
        pstart_of = jnp.where(ids_d == e, pstarts[e], pstart_of)
    pos = (ranks.reshape(a // LANES, LANES) + pstart_of).reshape(a)
    i32 = lambda v: v.astype(jnp.int32)
    return i32(block_rank), i32(expert_seq), i32(src0), i32(stok), i32(meta), i32(pos), nblk


def kernel(x, c, ctx, c_ctx, w_ada, b_ada, norm_mix, norm_ffn, w_in, gla_lr_w, gla_lr_b, gla_norm,
           w_branch_gla, w_branch_ret, w_out, w_router_group, b_router_group, w_router_expert,
           b_router_expert, w_expert_gate, w_expert_up, w_expert_down, norm_final):
    batch, seq, d = x.shape
    ctx_len = ctx.shape[1]
    assert d == D_MODEL and w_ada.shape[0] == 1, "single-layer block with D_MODEL features"
    t = batch * seq

    c8 = jnp.zeros((8, d), F32).at[:batch].set(c).at[batch].set(c_ctx)
    mod = _ada(c8, w_ada[0], b_ada[0][None, :])
    sh1, sc1, g1, sh2, sc2, g2 = [mod[:, d * i:d * (i + 1)] for i in range(6)]
    lat = lambda m: m[:batch, None, :]
    cx = lambda m: m[batch:batch + 1, None, :]

    w_main, w_lr = _wprep(jnp.swapaxes(w_in[0], 0, 1))
    nm = norm_mix[0][None, :]

    ones = jnp.ones((ctx_len, DK), F32)
    p_ctx, lr_ctx = _proj(ctx.reshape(batch * ctx_len, d), cx(sh1), cx(sc1), nm, w_main, w_lr,
                          ones, jnp.zeros_like(ones), tm=ctx_len, rows_per_mod=batch * ctx_len)
    x2 = x.reshape(t, d)
    cos, sin = _rope_tables(seq)
    p_lat, lr_lat = _proj(x2, lat(sh1), lat(sc1), nm, w_main, w_lr, cos, sin, tm=PROJ_TM, rows_per_mod=seq)

    wlr_f = jnp.zeros((LANES, HEADS * DK), F32).at[:GLA_RANK].set(gla_lr_w[0, 0])
    wlr_b = jnp.zeros((LANES, HEADS * DK), F32).at[GLA_RANK:2 * GLA_RANK].set(gla_lr_w[0, 1])
    blr_f, blr_b = gla_lr_b[0, 0][None, :], gla_lr_b[0, 1][None, :]
    zero_state = jnp.zeros((batch, HEADS, DK, DV), F32)

    _, _, gs_f, gs_b, _ = _gla(p_ctx, lr_ctx, wlr_f, blr_f, wlr_b, blr_b, zero_state, zero_state,
                               batch, ctx_len, tb=ctx_len)
    gla_args = (p_lat, lr_lat, wlr_f, blr_f, wlr_b, blr_b, gs_f, gs_b, batch, seq)
    gla_f, gla_b, _, _, decay_range = _gla(*gla_args, tb=SCAN_TB)
    gla_f, gla_b = lax.cond(jnp.max(decay_range) > GLA_SAFE_RANGE,
                            lambda: tuple(_gla(*gla_args, tb=GLA_EXACT_TB, exact=True)[:2]),
                            lambda: (gla_f, gla_b))

    _, _, rs_f, rs_b = _ret(p_ctx, zero_state, zero_state, batch, ctx_len, tb=ctx_len)
    ret_f, ret_b, _, _ = _ret(p_lat, rs_f, rs_b, batch, seq, tb=SCAN_TB)

    wrt = jnp.zeros((d, ROUTER_W), F32)
    wrt = wrt.at[:, :N_GROUPS].set(w_router_group[0]).at[:, EXP_ROW0:EXP_ROW0 + N_EXPERTS].set(w_router_expert[0])
    wrt_hi = wrt.astype(BF16)
    wrt = jnp.concatenate([wrt_hi, (wrt - wrt_hi.astype(F32)).astype(BF16)], axis=1)
    brt = jnp.zeros((1, ROUTER_W), F32)
    brt = brt.at[0, :N_GROUPS].set(b_router_group[0]).at[0, EXP_ROW0:EXP_ROW0 + N_EXPERTS].set(b_router_expert[0])
    h, hn, ids8, ew8, cnt = _merge(gla_f, gla_b, ret_f, ret_b, p_lat, x2, lat(g1), lat(sh2),
                                   norm_ffn[0][None, None, :] * (1.0 + lat(sc2)),
                                   jnp.tile(gla_norm[0], HEADS)[None, :],
                                   w_branch_gla[0], w_branch_ret[0], w_out[0], wrt, brt, seq, tm=MERGE_TM)

    block_rank, expert_seq, src0, stok, meta, pos, nblk = _dispatch_indices(
        ids8[:TOP_K], ids8[TOP_K:2 * TOP_K], cnt[:, 0], t)
    y_pad = _experts(block_rank, expert_seq, src0, stok, meta, hn,
                     w_expert_gate[0], w_expert_up[0], w_expert_down[0], nblk)
    out = _combine(pos, y_pad, h, ew8, lat(g2), norm_final[None, :], seq)
    return out.reshape(batch, seq, d)
```

```python
import functools

import numpy as np
import jax
import jax.numpy as jnp
from jax import lax
from jax.experimental import pallas as pl
from jax.experimental.pallas import tpu as pltpu

F32 = jnp.float32
BF16 = jnp.bfloat16

D_MODEL = 1024
GRID_W = 64
HEADS = 4
DK = 128
DV = 256
GLA_RANK = 16
GLA_TAU = 16.0
GLA_CHUNK = 64
GLA_SUB = 16
RET_CHUNK = 128
ROPE_BASE = 10000.0
N_GROUPS = 4
EXPERTS_PER_GROUP = 8
N_EXPERTS = N_GROUPS * EXPERTS_PER_GROUP
TOP_K = 2
EXPERT_FF = 256
MOE_BLOCK = 128
NORM_EPS = 1e-6

COL_GQ, COL_GK, COL_GV, COL_GG = 0, 512, 1024, 2048
COL_RQ, COL_RK, COL_RV, COL_RG = 3072, 3584, 4096, 5120
COL_MG, COL_MR = 6144, 7168
PROJ_W = 8192
LANES = 128
FEAT_TILES = D_MODEL // LANES
ROUTER_W = 128
EXP_ROW0 = 8

VMEM_LIMIT = 56 * 1024 * 1024

PROJ_TM = 1024
PROJ_TN = 2048
PROJ_NORM_SPLIT = 4
SCAN_TB = 1024
GLA_PHASE_LAG = 2
RET_PHASE_LAG = 0
GLA_EXACT_TB = 256
GLA_SAFE_RANGE = 60.0
MERGE_TM = 512
MERGE_SPLIT = 2
COMBINE_TM = 256


def _cparams(sem):
    return pltpu.CompilerParams(dimension_semantics=sem, vmem_limit_bytes=VMEM_LIMIT)


def _sigmoid(x):
    return 1.0 / (1.0 + jnp.exp(-x))


def _silu(x):
    return x * _sigmoid(x)


def _dot(a, b):
    return jnp.dot(a, b, preferred_element_type=F32)


def _dot_nt(a, b):
    return lax.dot_general(a, b, (((1,), (1,)), ((), ())), preferred_element_type=F32)


def _ada_body(c_ref, w_ref, b_ref, o_ref):
    s = _silu(c_ref[...])
    o_ref[...] = _dot(s.astype(BF16), w_ref[...].astype(BF16)) + b_ref[...]


def _ada(c8, w, b):
    n = w.shape[1]
    tn = 1536
    return pl.pallas_call(
        _ada_body,
        grid=(n // tn,),
        in_specs=[pl.BlockSpec((8, D_MODEL), lambda j: (0, 0)),
                  pl.BlockSpec((D_MODEL, tn), lambda j: (0, j)),
                  pl.BlockSpec((1, tn), lambda j: (0, j))],
        out_specs=pl.BlockSpec((8, tn), lambda j: (0, j)),
        out_shape=jax.ShapeDtypeStruct((8, n), F32),
        compiler_params=_cparams(("arbitrary",)),
        name="ada",
    )(c8, w, b)


LR_COL0 = COL_GG + HEADS * DV
WPREP_ROWS = 1024


def _wprep_body(w_ref, wlr_ref, main_ref, lr_ref):
    main_ref[...] = w_ref[...].astype(BF16)
    pad = jnp.zeros((LANES - 2 * GLA_RANK, D_MODEL), F32)
    lr_ref[...] = jnp.concatenate([wlr_ref[...], pad], axis=0).astype(BF16)


def _wprep(wt):
    steps_before = LR_COL0 // WPREP_ROWS

    def src_row(i):
        row = jnp.where(i < steps_before, i * WPREP_ROWS, i * WPREP_ROWS + 2 * GLA_RANK)
        return (pl.multiple_of(row, 2 * GLA_RANK), 0)

    return pl.pallas_call(
        _wprep_body,
        grid=(PROJ_W // WPREP_ROWS,),
        in_specs=[pl.BlockSpec((pl.Element(WPREP_ROWS), pl.Element(D_MODEL)), src_row),
                  pl.BlockSpec((pl.Element(2 * GLA_RANK), pl.Element(D_MODEL)), lambda i: (LR_COL0, 0))],
        out_specs=[pl.BlockSpec((WPREP_ROWS, D_MODEL), lambda i: (i, 0)),
                   pl.BlockSpec((LANES, D_MODEL), lambda i: (0, 0))],
        out_shape=[jax.ShapeDtypeStruct((PROJ_W, D_MODEL), BF16), jax.ShapeDtypeStruct((LANES, D_MODEL), BF16)],
        compiler_params=_cparams(("arbitrary",)),
        name="wprep",
    )(wt, wt)


ROPE_TILE = COL_RQ // PROJ_TN
ROPE_COL0 = COL_RQ % PROJ_TN
assert ROPE_TILE != 0 and ROPE_COL0 + 2 * HEADS * DK <= PROJ_TN, "rotated columns: one tile, not the first"


def _rope(x, cos, sin):
    lane = lax.broadcasted_iota(jnp.int32, x.shape, 1)
    partner = jnp.where((lane % 64) < 32, pltpu.roll(x, DK - 32, 1), pltpu.roll(x, 32, 1))
    return x * cos + partner * sin


def _proj_body(x_ref, sh_ref, sc_ref, g_ref, w_ref, wlr_ref, cos_ref, sin_ref, o_ref, lr_ref, h_ref):
    j = pl.program_id(1)

    @pl.when(j == 0)
    def _():
        n = x_ref.shape[0] // PROJ_NORM_SPLIT
        for grp in range(PROJ_NORM_SPLIT):
            rows = pl.ds(grp * n, n)
            x = x_ref[rows, :]
            ms = jnp.mean(x * x, axis=-1, keepdims=True)
            y = x * lax.rsqrt(ms + NORM_EPS) * g_ref[...]
            hb = (y * (1.0 + sc_ref[0]) + sh_ref[0]).astype(BF16)
            h_ref[rows, :] = hb
            lr_ref[rows, :] = _dot_nt(hb, wlr_ref[...])
            o_ref[rows, :] = _dot_nt(hb, w_ref[...]).astype(BF16)

    @pl.when(jnp.logical_and(j != 0, j != ROPE_TILE))
    def _():
        o_ref[...] = _dot_nt(h_ref[...], w_ref[...]).astype(BF16)

    @pl.when(j == ROPE_TILE)
    def _():
        h = h_ref[...]
        cos, sin = cos_ref[...], sin_ref[...]
        for pair in range(HEADS):
            lo = ROPE_COL0 + 2 * DK * pair
            acc = _dot_nt(h, w_ref[lo:lo + 2 * DK, :])
            if pair >= HEADS // 2:
                acc = acc * (DK ** -0.5)
            for half in range(2):
                blk = acc[:, DK * half:DK * (half + 1)]
                o_ref[:, lo + DK * half:lo + DK * (half + 1)] = _rope(blk, cos, sin).astype(BF16)
        rope_end = ROPE_COL0 + 2 * HEADS * DK
        for lo, hi in ((0, ROPE_COL0), (rope_end, PROJ_TN)):
            if lo < hi:
                o_ref[:, lo:hi] = _dot_nt(h, w_ref[lo:hi, :]).astype(BF16)


def _proj(x2, sh, sc, g, w_main, w_lr, cos, sin, tm, rows_per_mod):
    t = x2.shape[0]
    tn = PROJ_TN
    tiles_per_mod = rows_per_mod // tm
    mod_map = lambda i, j: (i // tiles_per_mod, 0, 0)
    pos_tiles = cos.shape[0] // tm
    pos_map = lambda i, j: (i % pos_tiles, 0)
    return pl.pallas_call(
        _proj_body,
        grid=(t // tm, PROJ_W // tn),
        in_specs=[pl.BlockSpec((tm, D_MODEL), lambda i, j: (i, 0)),
                  pl.BlockSpec((1, 1, D_MODEL), mod_map),
                  pl.BlockSpec((1, 1, D_MODEL), mod_map),
                  pl.BlockSpec((1, D_MODEL), lambda i, j: (0, 0)),
                  pl.BlockSpec((tn, D_MODEL), lambda i, j: (j, 0)),
                  pl.BlockSpec((LANES, D_MODEL), lambda i, j: (0, 0)),
                  pl.BlockSpec((tm, DK), pos_map), pl.BlockSpec((tm, DK), pos_map)],
        out_specs=[pl.BlockSpec((tm, tn), lambda i, j: (i, j)),
                   pl.BlockSpec((tm, LANES), lambda i, j: (i, 0))],
        out_shape=[jax.ShapeDtypeStruct((t, PROJ_W), BF16),
                   jax.ShapeDtypeStruct((t, LANES), F32)],
        scratch_shapes=[pltpu.VMEM((tm, D_MODEL), BF16)],
        compiler_params=_cparams(("parallel", "arbitrary")),
        name="proj",
    )(x2, sh, sc, g, w_main, w_lr, cos, sin)


def _gla_consts():
    c, s = GLA_CHUNK, GLA_SUB
    msk = {}
    for rev in (False, True):
        cols = []
        for i in range(c // s):
            keys = np.arange(c)[_key_rows(i, rev)][None, :]
            t = np.arange(c)[:, None]
            visible = (keys > t) if rev else (keys <= t)
            cols.append(np.where(t // s == i, visible, False))
        msk[rev] = np.concatenate(cols, axis=1).astype(np.float32)
    return msk


def _key_rows(i, rev):
    return slice(GLA_SUB * i, GLA_CHUNK) if rev else slice(0, GLA_SUB * (i + 1))


def _log_gate(lr, w, b):
    z = _dot(lr.astype(BF16), w.astype(BF16)) + b
    return -(jnp.maximum(-z, 0.0) + jnp.log(1.0 + jnp.exp(-jnp.abs(z)))) * (1.0 / GLA_TAU)


def _subchunk_scan(la, rev):
    c = la.shape[0]
    pos = lax.broadcasted_iota(jnp.int32, la.shape, 0) % GLA_SUB
    w = la
    step = 1
    while step < GLA_SUB:
        if rev:
            w = w + jnp.where(pos < GLA_SUB - step, pltpu.roll(w, c - step, 0), 0.0)
        else:
            w = w + jnp.where(pos >= step, pltpu.roll(w, step, 0), 0.0)
        step *= 2
    return w


def _gla_intra_exact(q, kf, v, b, rev, scratch):
    q_s, b_s, o_s = scratch
    c = GLA_CHUNK
    q_s[...] = q.astype(F32) * (DK ** -0.5)
    b_s[...] = b
    vf = v.astype(F32)
    row = lax.broadcasted_iota(jnp.int32, (c, 1), 0)

    def body(t, carry):
        qt = q_s[pl.ds(t, 1), :]
        bt = b_s[pl.ds(t, 1), :]
        visible = (row > t) if rev else (row <= t)
        decay = jnp.exp(jnp.where(visible, bt - b, -jnp.inf))
        col = jnp.sum(kf * decay * qt, axis=1, keepdims=True)
        o_s[pl.ds(t, 1), :] = jnp.sum(col * vf, axis=0, keepdims=True)
        return carry

    lax.fori_loop(0, c, body, 0)
    return o_s[...]


def _gla_chunk(q, k, v, la, state_ref, mask, rev, exact_scratch=None):
    c, s = GLA_CHUNK, GLA_SUB
    nsub = c // s
    w = _subchunk_scan(la, rev)
    tot = [w[s * i:s * i + 1] if rev else w[s * (i + 1) - 1:s * (i + 1)] for i in range(nsub)]
    anchors = [None] * nsub
    acc = jnp.zeros_like(tot[0])
    worst = jnp.zeros_like(tot[0])
    for i in (reversed(range(nsub)) if rev else range(nsub)):
        anchors[i] = acc
        acc = acc + tot[i]
        worst = jnp.maximum(worst, -tot[i])
    b_end = acc
    ref = jnp.concatenate([jnp.broadcast_to(a, (s, DK)) for a in anchors], axis=0)
    b = w + ref

    qt = q.astype(F32) * (DK ** -0.5) * jnp.exp(w)
    kf = k.astype(F32)
    if exact_scratch is None:
        kstack = jnp.concatenate([kf[_key_rows(i, rev)] * jnp.exp(anchors[i] - b[_key_rows(i, rev)])
                                  for i in range(nsub)], axis=0).astype(BF16)
        raw = _dot_nt(qt.astype(BF16), kstack)
    kst = kf * jnp.exp(b_end - b)
    xt = jnp.concatenate([kst, jnp.broadcast_to(jnp.exp(b_end), (c, DK))], axis=0).T
    update = _dot(xt[:, :c].astype(BF16), v)
    state = state_ref[...]
    inter = _dot((qt * jnp.exp(ref)).astype(BF16), state.astype(BF16))
    state_ref[...] = xt[:, c:c + 1] * state + update
    yield
    if exact_scratch is None:
        vstack = jnp.concatenate([v[_key_rows(i, rev)] for i in range(nsub)], axis=0)
        intra = _dot(jnp.where(mask != 0.0, raw, 0.0).astype(BF16), vstack)
    else:
        intra = _gla_intra_exact(q, kf, v, b, rev, exact_scratch)
    return inter + intra, worst


def _round_robin(gens):
    results = [None] * len(gens)
    live = list(range(len(gens)))
    while live:
        for g in list(live):
            try:
                next(gens[g])
            except StopIteration as done:
                results[g] = done.value
                live.remove(g)
    return results


def _run_phased(chunks, emit, lag):
    def finish(tag, gen):
        try:
            next(gen)
        except StopIteration as done:
            emit(tag, done.value)
            return
        raise AssertionError("chunk generator has more than two phases")

    pending = []
    for tag, gen in chunks:
        next(gen)
        pending.append((tag, gen))
        if len(pending) > lag:
            finish(*pending.pop(0))
    for item in pending:
        finish(*item)


def _gla_body(qf_ref, kf_ref, vf_ref, lrf_ref, qb_ref, kb_ref, vb_ref, lrb_ref,
              wf_ref, bf_ref, wb_ref, bb_ref, mskf_ref, mskb_ref,
              s0f_ref, s0b_ref, of_ref, ob_ref, sf_ref, sb_ref, rng_ref, *exact_scratch, nchunks):
    @pl.when(pl.program_id(2) == 0)
    def _():
        sf_ref[...] = s0f_ref[...]
        sb_ref[...] = s0b_ref[...]

    c = GLA_CHUNK
    scratch = exact_scratch or None
    la_f = _log_gate(lrf_ref[...], wf_ref[...], bf_ref[...])
    la_b = _log_gate(lrb_ref[...], wb_ref[...], bb_ref[...])
    mskf, mskb = mskf_ref[...], mskb_ref[...]
    st_f, st_b = sf_ref.at[0, 0], sb_ref.at[0, 0]
    worst = [jnp.zeros((1, DK), F32)]

    def chunks():
        for n in range(nchunks):
            rf = slice(c * n, c * (n + 1))
            yield (of_ref, rf), _gla_chunk(qf_ref[rf, :], kf_ref[rf, :], vf_ref[rf, :], la_f[rf, :], st_f, mskf,
                                           False, scratch)
            m = nchunks - 1 - n
            rb = slice(c * m, c * (m + 1))
            yield (ob_ref, rb), _gla_chunk(qb_ref[rb, :], kb_ref[rb, :], vb_ref[rb, :], la_b[rb, :], st_b, mskb,
                                           True, scratch)

    def emit(tag, result):
        o_ref, rows = tag
        o_ref[rows, :] = result[0].astype(BF16)
        worst[0] = jnp.maximum(worst[0], result[1])

    _run_phased(chunks(), emit, GLA_PHASE_LAG)
    rng_ref[0, 0, 0] = jnp.broadcast_to(worst[0], (8, DK))


def _gla(p, lr, wlr_f, blr_f, wlr_b, blr_b, s0f, s0b, batch, seq, tb, exact=False):
    nb = seq // tb
    msk = _gla_consts()
    mskf, mskb = jnp.asarray(msk[False]), jnp.asarray(msk[True])

    def fwd(col0, width):
        return lambda b, h, i: (b * nb + i, col0 // width + h)

    def bwd(col0, width):
        return lambda b, h, i: (b * nb + nb - 1 - i, col0 // width + h)

    const2 = lambda b, h, i: (0, 0)
    headcol = lambda b, h, i: (0, h)
    st_map = lambda b, h, i: (b, h, 0, 0)
    st_spec = pl.BlockSpec((1, 1, DK, DV), st_map)
    st_shape = jax.ShapeDtypeStruct((batch, HEADS, DK, DV), F32)
    o_shape = jax.ShapeDtypeStruct((batch * seq, HEADS * DV), BF16)
    return pl.pallas_call(
        functools.partial(_gla_body, nchunks=tb // GLA_CHUNK),
        grid=(batch, HEADS, nb),
        in_specs=[pl.BlockSpec((tb, DK), fwd(COL_GQ, DK)),
                  pl.BlockSpec((tb, DK), fwd(COL_GK, DK)),
                  pl.BlockSpec((tb, DV), fwd(COL_GV, DV)),
                  pl.BlockSpec((tb, LANES), lambda b, h, i: (b * nb + i, 0)),
                  pl.BlockSpec((tb, DK), bwd(COL_GQ, DK)),
                  pl.BlockSpec((tb, DK), bwd(COL_GK, DK)),
                  pl.BlockSpec((tb, DV), bwd(COL_GV, DV)),
                  pl.BlockSpec((tb, LANES), lambda b, h, i: (b * nb + nb - 1 - i, 0)),
                  pl.BlockSpec((LANES, DK), headcol),
                  pl.BlockSpec((1, DK), headcol),
                  pl.BlockSpec((LANES, DK), headcol),
                  pl.BlockSpec((1, DK), headcol),
                  pl.BlockSpec(mskf.shape, const2),
                  pl.BlockSpec(mskb.shape, const2),
                  st_spec, st_spec],
        out_specs=[pl.BlockSpec((tb, DV), lambda b, h, i: (b * nb + i, h)),
                   pl.BlockSpec((tb, DV), lambda b, h, i: (b * nb + nb - 1 - i, h)),
                   st_spec, st_spec,
                   pl.BlockSpec((1, 1, 1, 8, DK), lambda b, h, i: (b, h, i, 0, 0))],
        out_shape=[o_shape, o_shape, st_shape, st_shape,
                   jax.ShapeDtypeStruct((batch, HEADS, nb, 8, DK), F32)],
        scratch_shapes=([pltpu.VMEM((GLA_CHUNK, DK), F32), pltpu.VMEM((GLA_CHUNK, DK), F32),
                         pltpu.VMEM((GLA_CHUNK, DV), F32)] if exact else []),
        compiler_params=_cparams(("parallel", "parallel", "arbitrary")),
        name="gla_exact" if exact else "gla",
    )(p, p, p, lr, p, p, p, lr, wlr_f, blr_f, wlr_b, blr_b, mskf, mskb, s0f, s0b)


def _ret_consts():
    c = RET_CHUNK
    hh = np.arange(HEADS, dtype=np.float64)
    lg = {False: np.log1p(-np.exp2(-5.0 - hh)), True: np.log1p(-np.exp2(-5.5 - hh))}
    t = np.arange(c, dtype=np.float64)[:, None]
    u = np.arange(c, dtype=np.float64)[None, :]
    pos = np.arange(c, dtype=np.float64)
    out = {}
    for rev in (False, True):
        g = lg[rev][:, None, None]
        if not rev:
            dmat = np.where(u <= t, np.exp((t - u) * g), 0.0)
            qd = np.exp((pos + 1.0)[None, :] * lg[rev][:, None])
            kd = np.exp((c - 1.0 - pos)[None, :] * lg[rev][:, None])
        else:
            dmat = np.where(u > t, np.exp((u - t) * g), 0.0)
            qd = np.exp((c - pos)[None, :] * lg[rev][:, None])
            kd = np.exp(pos[None, :] * lg[rev][:, None])
        cd = np.exp(c * lg[rev])
        out[rev] = (dmat.astype(np.float32),
                    np.repeat(qd[:, :, None], DV, axis=2).astype(np.float32),
                    np.repeat(kd[:, :, None], DK, axis=2).astype(np.float32),
                    np.repeat(cd[:, None, None], DV, axis=2).astype(np.float32))
    return out


def _ret_chunk(q, k, v, state_ref, dmat, qd, kd, cd):
    raw = _dot_nt(q, k)
    update = _dot((k.astype(F32) * kd).T.astype(BF16), v)
    state = state_ref[...]
    inter = _dot(q, state.astype(BF16)) * qd
    state_ref[...] = cd * state + update
    yield
    return inter + _dot((raw * dmat).astype(BF16), v)


def _ret_body(qf_ref, kf_ref, vf_ref, qb_ref, kb_ref, vb_ref,
              dmf_ref, qdf_ref, kdf_ref, cdf_ref, dmb_ref, qdb_ref, kdb_ref, cdb_ref,
              s0f_ref, s0b_ref, of_ref, ob_ref, sf_ref, sb_ref, *, nchunks):
    @pl.when(pl.program_id(2) == 0)
    def _():
        sf_ref[...] = s0f_ref[...]
        sb_ref[...] = s0b_ref[...]

    c = RET_CHUNK
    cf = (dmf_ref[0], qdf_ref[0], kdf_ref[0], cdf_ref[0])
    cb = (dmb_ref[0], qdb_ref[0], kdb_ref[0], cdb_ref[0])
    st_f, st_b = sf_ref.at[0, 0], sb_ref.at[0, 0]
    def chunks():
        for n in range(nchunks):
            rf = slice(c * n, c * (n + 1))
            yield (of_ref, rf), _ret_chunk(qf_ref[rf, :], kf_ref[rf, :], vf_ref[rf, :], st_f, *cf)
            m = nchunks - 1 - n
            rb = slice(c * m, c * (m + 1))
            yield (ob_ref, rb), _ret_chunk(qb_ref[rb, :], kb_ref[rb, :], vb_ref[rb, :], st_b, *cb)

    def emit(tag, out):
        o_ref, rows = tag
        o_ref[rows, :] = out.astype(BF16)

    _run_phased(chunks(), emit, RET_PHASE_LAG)


def _ret(p, s0f, s0b, batch, seq, tb):
    nb = seq // tb
    consts = _ret_consts()

    def fwd(col0, width):
        return lambda b, h, i: (b * nb + i, col0 // width + h)

    def bwd(col0, width):
        return lambda b, h, i: (b * nb + nb - 1 - i, col0 // width + h)

    head3 = lambda b, h, i: (h, 0, 0)
    st_spec = pl.BlockSpec((1, 1, DK, DV), lambda b, h, i: (b, h, 0, 0))
    st_shape = jax.ShapeDtypeStruct((batch, HEADS, DK, DV), F32)
    o_shape = jax.ShapeDtypeStruct((batch * seq, HEADS * DV), BF16)
    const_specs = []
    const_args = []
    for rev in (False, True):
        for a in consts[rev]:
            const_specs.append(pl.BlockSpec((1,) + a.shape[1:], head3))
            const_args.append(jnp.asarray(a))
    return pl.pallas_call(
        functools.partial(_ret_body, nchunks=tb // RET_CHUNK),
        grid=(batch, HEADS, nb),
        in_specs=[pl.BlockSpec((tb, DK), fwd(COL_RQ, DK)),
                  pl.BlockSpec((tb, DK), fwd(COL_RK, DK)),
                  pl.BlockSpec((tb, DV), fwd(COL_RV, DV)),
                  pl.BlockSpec((tb, DK), bwd(COL_RQ, DK)),
                  pl.BlockSpec((tb, DK), bwd(COL_RK, DK)),
                  pl.BlockSpec((tb, DV), bwd(COL_RV, DV))]
                 + const_specs + [st_spec, st_spec],
        out_specs=[pl.BlockSpec((tb, DV), lambda b, h, i: (b * nb + i, h)),
                   pl.BlockSpec((tb, DV), lambda b, h, i: (b * nb + nb - 1 - i, h)),
                   st_spec, st_spec],
        out_shape=[o_shape, o_shape, st_shape, st_shape],
        compiler_params=_cparams(("parallel", "parallel", "arbitrary")),
        name="ret",
    )(p, p, p, p, p, p, *const_args, s0f, s0b)


def _rope_tables(seq):
    n = DK // 4
    inv = np.float32(ROPE_BASE) ** (-np.arange(n, dtype=np.float32) / np.float32(n))
    pos = np.arange(seq)
    ar = (pos // GRID_W).astype(np.float32)[:, None] * inv[None, :]
    ac = (pos % GRID_W).astype(np.float32)[:, None] * inv[None, :]
    cos = np.concatenate([np.cos(ar), np.cos(ar), np.cos(ac), np.cos(ac)], axis=1)
    sin = np.concatenate([-np.sin(ar), np.sin(ar), -np.sin(ac), np.sin(ac)], axis=1)
    return jnp.asarray(cos, F32), jnp.asarray(sin, F32)


def _route(logits_t):
    g = [logits_t[i:i + 1] for i in range(N_GROUPS)]
    gmax = jnp.maximum(jnp.maximum(g[0], g[1]), jnp.maximum(g[2], g[3]))
    gsel = jnp.where(g[0] == gmax, 0, jnp.where(g[1] == gmax, 1, jnp.where(g[2] == gmax, 2, 3)))
    gsum = (jnp.exp(g[0] - gmax) + jnp.exp(g[1] - gmax)) + (jnp.exp(g[2] - gmax) + jnp.exp(g[3] - gmax))
    gw = 1.0 / gsum
    e = [logits_t[EXP_ROW0 + EXPERTS_PER_GROUP * i:EXP_ROW0 + EXPERTS_PER_GROUP * (i + 1)]
         for i in range(N_GROUPS)]
    el = jnp.where(gsel == 0, e[0], jnp.where(gsel == 1, e[1], jnp.where(gsel == 2, e[2], e[3])))
    row = lax.broadcasted_iota(jnp.int32, el.shape, 0).astype(F32)
    none = float(EXPERTS_PER_GROUP)
    m1 = jnp.max(el, axis=0, keepdims=True)
    i1 = jnp.min(jnp.where(el == m1, row, none), axis=0, keepdims=True)
    el2 = jnp.where(row == i1, -jnp.inf, el)
    m2 = jnp.max(el2, axis=0, keepdims=True)
    i2 = jnp.min(jnp.where(el2 == m2, row, none), axis=0, keepdims=True)
    r = jnp.exp(m2 - m1)
    w1 = gw / (1.0 + r)
    w2 = gw * r / (1.0 + r)
    base = gsel * EXPERTS_PER_GROUP
    ids = jnp.concatenate([base + i1.astype(jnp.int32), base + i2.astype(jnp.int32)], axis=0)
    return ids, jnp.concatenate([w1, w2], axis=0)


def _store_token_tiled(ref, val):
    m = val.shape[0]
    for j in range(FEAT_TILES):
        ref[pl.ds(j, m, stride=FEAT_TILES), :] = val[:, LANES * j:LANES * (j + 1)]


def _load_token_tiled(ref, m):
    return jnp.concatenate([ref[pl.ds(j, m, stride=FEAT_TILES), :] for j in range(FEAT_TILES)], axis=1)


def _merge_body(gf_ref, gb_ref, rf_ref, rb_ref, gg_ref, rg_ref, mg_ref, mr_ref, x_ref,
                g1_ref, sh2_ref, nfs_ref, gn_ref, wg_ref, wr_ref, wo_ref, wrt_ref, brt_ref, upper_ref,
                h_ref, hn_ref, ids_ref, ew_ref, cnt_ref, wg_s, wr_s, wo_s, stage, sem):
    @pl.when(pl.program_id(0) == 0)
    def _():
        cnt_ref[...] = jnp.zeros_like(cnt_ref)
        for w_hbm, w_s in ((wg_ref, wg_s), (wr_ref, wr_s), (wo_ref, wo_s)):
            copy = pltpu.make_async_copy(w_hbm, stage, sem.at[0])
            copy.start()
            copy.wait()
            w_s[...] = stage[...].astype(BF16)

    def rows_to_logits(r0, n):
        rows = pl.ds(r0, n)
        og = (gf_ref[rows, :] + gb_ref[rows, :]).astype(F32)
        orr = (rf_ref[rows, :] + rb_ref[rows, :]).astype(F32)
        gparts, rparts = [], []
        for hh in range(HEADS):
            seg = og[:, DV * hh:DV * (hh + 1)]
            ms = jnp.mean(seg * seg, axis=-1, keepdims=True)
            gparts.append(seg * lax.rsqrt(ms + NORM_EPS))
            seg = orr[:, DV * hh:DV * (hh + 1)]
            mu = jnp.mean(seg, axis=-1, keepdims=True)
            cen = seg - mu
            var = jnp.mean(cen * cen, axis=-1, keepdims=True)
            rparts.append(cen * lax.rsqrt(var + NORM_EPS))
        o_gla = (jnp.concatenate(gparts, axis=1) * gn_ref[...]).astype(BF16) * _silu(gg_ref[rows, :])
        o_ret = jnp.concatenate(rparts, axis=1).astype(BF16) * _silu(rg_ref[rows, :])
        a_gla = _dot(o_gla, wg_s[...])
        a_ret = _dot(o_ret, wr_s[...])
        yield
        y = _sigmoid(mg_ref[rows, :]) * a_gla.astype(BF16) + _sigmoid(mr_ref[rows, :]) * a_ret.astype(BF16)
        out = _dot(y, wo_s[...])
        yield
        h = x_ref[rows, :] + g1_ref[0] * out
        h_ref[rows, :] = h
        ms = jnp.mean(h * h, axis=-1, keepdims=True)
        hn = h * lax.rsqrt(ms + NORM_EPS) * nfs_ref[0] + sh2_ref[0]
        _store_token_tiled(hn_ref.at[pl.ds(r0 * FEAT_TILES, n * FEAT_TILES)], hn)
        hn_hi = hn.astype(BF16)
        hn_lo = (hn - hn_hi.astype(F32)).astype(BF16)
        both_w = _dot(hn_hi, wrt_ref[...])
        lo_w = _dot(hn_lo, wrt_ref[:, :ROUTER_W])
        yield
        return (both_w[:, :ROUTER_W] + both_w[:, ROUTER_W:] + lo_w) + brt_ref[...]

    tm = x_ref.shape[0]
    n = tm // MERGE_SPLIT
    logits = jnp.concatenate(_round_robin([rows_to_logits(g * n, n) for g in range(MERGE_SPLIT)]), axis=0)
    ids, ew = _route(logits.T)

    erow = lax.broadcasted_iota(jnp.int32, (N_EXPERTS, tm), 0)
    oh0 = jnp.where(erow == ids[0:1], 1.0, 0.0)
    oh1 = jnp.where(erow == ids[1:2], 1.0, 0.0)
    both = oh0 + oh1
    before = _dot(both.astype(BF16), upper_ref[...]) + cnt_ref[:, 0:1].astype(F32)
    rank0 = jnp.sum(oh0 * before, axis=0, keepdims=True)
    rank1 = jnp.sum(oh1 * before, axis=0, keepdims=True)
    total = cnt_ref[:, 0:1] + jnp.sum(both, axis=1, keepdims=True).astype(jnp.int32)
    cnt_ref[...] = jnp.broadcast_to(total, cnt_ref.shape)
    ids_ref[...] = jnp.concatenate([ids, rank0.astype(jnp.int32), rank1.astype(jnp.int32),
                                    jnp.zeros((4, tm), jnp.int32)], axis=0)
    ew_ref[...] = jnp.concatenate([ew, jnp.zeros((6, tm), F32)], axis=0)


def _merge(gf, gb, rf, rb, p, x2, g1, sh2, nfs, gn, wg, wr, wo, wrt, brt, seq, tm):
    t = x2.shape[0]
    tiles_per_batch = seq // tm
    row = lambda i: (i, 0)
    mod = lambda i: (i // tiles_per_batch, 0, 0)
    const = lambda i: (0, 0)
    tok = pl.BlockSpec((tm, D_MODEL), row)
    vec = pl.BlockSpec((1, D_MODEL), const)
    modspec = pl.BlockSpec((1, 1, D_MODEL), mod)
    wspec = pl.BlockSpec(memory_space=pl.ANY)
    upper = jnp.asarray(np.triu(np.ones((tm, tm), np.float32), 1), BF16)

    def pcol(col0):
        return pl.BlockSpec((tm, D_MODEL), lambda i: (i, col0 // D_MODEL))

    return pl.pallas_call(
        _merge_body,
        grid=(t // tm,),
        in_specs=[tok, tok, tok, tok, pcol(COL_GG), pcol(COL_RG), pcol(COL_MG), pcol(COL_MR), tok,
                  modspec, modspec, modspec, vec, wspec, wspec, wspec,
                  pl.BlockSpec((D_MODEL, 2 * ROUTER_W), const), pl.BlockSpec((1, ROUTER_W), const),
                  pl.BlockSpec((tm, tm), const)],
        out_specs=[tok, pl.BlockSpec((tm * FEAT_TILES, LANES), row),
                   pl.BlockSpec((8, tm), lambda i: (0, i)), pl.BlockSpec((8, tm), lambda i: (0, i)),
                   pl.BlockSpec((N_EXPERTS, LANES), const)],
        out_shape=[jax.ShapeDtypeStruct((t, D_MODEL), F32), jax.ShapeDtypeStruct((t * FEAT_TILES, LANES), F32),
                   jax.ShapeDtypeStruct((8, t), jnp.int32), jax.ShapeDtypeStruct((8, t), F32),
                   jax.ShapeDtypeStruct((N_EXPERTS, LANES), jnp.int32)],
        scratch_shapes=[pltpu.VMEM((D_MODEL, D_MODEL), BF16)] * 3
                       + [pltpu.VMEM((D_MODEL, D_MODEL), F32), pltpu.SemaphoreType.DMA((1,))],
        compiler_params=_cparams(("arbitrary",)),
        name="merge",
    )(gf, gb, rf, rb, p, p, p, p, x2, g1, sh2, nfs, gn, wg, wr, wo, wrt, brt, upper)


GATHER_UNROLL = 8


def _aligned(tok):
    off = tok * FEAT_TILES
    return off if isinstance(off, int) else pl.multiple_of(off, FEAT_TILES)


def _token_copy(src_hbm, dst, sem, src_tok, dst_tok):
    return pltpu.make_async_copy(src_hbm.at[pl.ds(_aligned(src_tok), FEAT_TILES), :],
                                 dst.at[pl.ds(_aligned(dst_tok), FEAT_TILES), :], sem)


def _wait_tokens(src_hbm, dst, sem):
    pltpu.make_async_copy(src_hbm.at[pl.ds(0, dst.shape[0]), :], dst, sem).wait()


GATHER_BUFS = 3
GATHER_AHEAD = GATHER_BUFS - 1


def _issue_rows(n, start_row, unrolled):
    if unrolled:
        for r in range(n):
            start_row(r, r % 2)
    else:
        def body(g, carry):
            for u in range(GATHER_UNROLL):
                start_row(g * GATHER_UNROLL + u, u % 2)
            return carry
        lax.fori_loop(0, n // GATHER_UNROLL, body, 0)


EXPERT_GROUP = 2


EXPERT_W_SLOTS = 2
assert EXPERT_GROUP <= EXPERT_W_SLOTS, "the blocks of one step may belong to EXPERT_GROUP consecutive runs"


def _expert_body(rank_ref, eseq_ref, src0_ref, stok_ref, meta_ref, hn_hbm, wg_hbm, wu_hbm, wd_hbm, y_ref, *scratch):
    bufs, sems = scratch[:GATHER_BUFS], scratch[GATHER_BUFS]
    stage, w_s, wsem = scratch[GATHER_BUFS + 1:GATHER_BUFS + 4], scratch[GATHER_BUFS + 4:GATHER_BUFS + 7], scratch[-1]
    w_hbm = (wg_hbm, wu_hbm, wd_hbm)
    i = pl.program_id(0)
    nused, nranks = meta_ref[0], meta_ref[1]
    nsteps = (nused + EXPERT_GROUP - 1) // EXPERT_GROUP
    last = stok_ref.shape[0] - 1
    rows = MOE_BLOCK * FEAT_TILES

    def issue(step, s, unrolled):
        bases = [src0_ref[step * EXPERT_GROUP + g] for g in range(EXPERT_GROUP)]

        def start(g, r, priority):
            tok = stok_ref[jnp.minimum(bases[g] + r, last)]
            _token_copy(hn_hbm, bufs[s], sems.at[s], tok, g * MOE_BLOCK + r).start(priority=priority)

        if unrolled:
            _issue_rows(EXPERT_GROUP * MOE_BLOCK, lambda j, p: start(j // MOE_BLOCK, j % MOE_BLOCK, p), True)
        else:
            for g in range(EXPERT_GROUP):
                _issue_rows(MOE_BLOCK, lambda r, p, g=g: start(g, r, p), False)

    def weight_copies(rank):
        slot = rank % EXPERT_W_SLOTS
        expert = eseq_ref[rank]
        return [pltpu.make_async_copy(w_hbm[k].at[expert], stage[k].at[slot], wsem.at[slot]) for k in range(3)]

    def activate(rank):
        slot = rank % EXPERT_W_SLOTS
        for copy in weight_copies(rank):
            copy.wait()
        for k in range(3):
            w_s[k][slot] = stage[k][slot].astype(BF16)

        @pl.when(rank + 1 < nranks)
        def _():
            for copy in weight_copies(rank + 1):
                copy.start()

    @pl.when(i == 0)
    def _():
        for b in range(GATHER_AHEAD):
            issue(b, b, False)

        @pl.when(nranks > 0)
        def _():
            for copy in weight_copies(0):
                copy.start()

    for g in range(EXPERT_GROUP):
        blk = i * EXPERT_GROUP + g
        rank = rank_ref[blk]
        first_of_run = jnp.logical_or(blk == 0, rank != rank_ref[jnp.maximum(blk - 1, 0)])

        @pl.when(jnp.logical_and(blk < nused, first_of_run))
        def _():
            activate(rank)

    def block_phases(cur, g):
        part = pl.ds(g * rows, rows)
        slot = rank_ref[i * EXPERT_GROUP + g] % EXPERT_W_SLOTS
        xb = _load_token_tiled(bufs[cur].at[part], MOE_BLOCK).astype(BF16)
        gate = _dot(xb, w_s[0][slot])
        up = _dot(xb, w_s[1][slot])
        yield
        y = _dot((_silu(gate) * up).astype(BF16), w_s[2][slot])
        yield
        _store_token_tiled(y_ref.at[part], y)

    for cur in range(GATHER_BUFS):
        mine = i % GATHER_BUFS == cur

        @pl.when(jnp.logical_and(mine, i < nsteps))
        def _():
            _wait_tokens(hn_hbm, bufs[cur], sems.at[cur])
            issue(i + GATHER_AHEAD, (cur + GATHER_AHEAD) % GATHER_BUFS, True)
            _round_robin([block_phases(cur, g) for g in range(EXPERT_GROUP)])

        @pl.when(jnp.logical_and(mine, jnp.logical_and(i >= nsteps, i < nsteps + GATHER_AHEAD)))
        def _():
            _wait_tokens(hn_hbm, bufs[cur], sems.at[cur])

    @pl.when(i >= nsteps)
    def _():
        y_ref[...] = jnp.zeros_like(y_ref)


def _experts(block_rank, expert_seq, src0, stok, meta, hn, w_gate, w_up, w_down, nblk):
    rows = EXPERT_GROUP * MOE_BLOCK * FEAT_TILES
    w_shapes = ((D_MODEL, EXPERT_FF), (D_MODEL, EXPERT_FF), (EXPERT_FF, D_MODEL))
    any_spec = pl.BlockSpec(memory_space=pl.ANY)
    return pl.pallas_call(
        _expert_body,
        grid_spec=pltpu.PrefetchScalarGridSpec(
            num_scalar_prefetch=5,
            grid=(nblk // EXPERT_GROUP,),
            in_specs=[any_spec, any_spec, any_spec, any_spec],
            out_specs=pl.BlockSpec((rows, LANES), lambda i, *prefetch: (i, 0)),
            scratch_shapes=[pltpu.VMEM((rows, LANES), F32)] * GATHER_BUFS
                           + [pltpu.SemaphoreType.DMA((GATHER_BUFS,))]
                           + [pltpu.VMEM((EXPERT_W_SLOTS,) + shape, F32) for shape in w_shapes]
                           + [pltpu.VMEM((EXPERT_W_SLOTS,) + shape, BF16) for shape in w_shapes]
                           + [pltpu.SemaphoreType.DMA((EXPERT_W_SLOTS,))]),
        out_shape=jax.ShapeDtypeStruct((nblk // EXPERT_GROUP * rows, LANES), F32),
        compiler_params=_cparams(("arbitrary",)),
        name="experts",
    )(block_rank, expert_seq, src0, stok, meta, hn, w_gate, w_up, w_down)


def _combine_body(pos_ref, y_hbm, h_ref, ew_ref, g2_ref, nf_ref, o_ref, *scratch):
    i = pl.program_id(0)
    n = pl.num_programs(0)
    tm = COMBINE_TM
    t = pos_ref.shape[0] // TOP_K
    bufs, sems = scratch[:GATHER_BUFS], scratch[GATHER_BUFS]

    def issue(tile, s, unrolled):
        def start_row(j, priority):
            k, r = j % TOP_K, j // TOP_K
            _token_copy(y_hbm, bufs[s].at[k], sems.at[s], pos_ref[k * t + tile * tm + r], r).start(priority=priority)
        _issue_rows(tm * TOP_K, start_row, unrolled)

    def wait(s):
        for k in range(TOP_K):
            _wait_tokens(y_hbm, bufs[s].at[k], sems.at[s])

    @pl.when(i == 0)
    def _():
        for b in range(GATHER_AHEAD):
            issue(b, b, False)

    for cur in range(GATHER_BUFS):
        @pl.when(i % GATHER_BUFS == cur)
        def _():
            wait(cur)
            issue(jnp.minimum(i + GATHER_AHEAD, n - 1), (cur + GATHER_AHEAD) % GATHER_BUFS, True)
            wt = jnp.concatenate([ew_ref[...]] * (LANES // 8), axis=0).T
            moe = (wt[:, 0:1] * _load_token_tiled(bufs[cur].at[0], tm)
                   + wt[:, 1:2] * _load_token_tiled(bufs[cur].at[1], tm))
            h = h_ref[...] + g2_ref[0] * moe
            ms = jnp.mean(h * h, axis=-1, keepdims=True)
            o_ref[...] = h * lax.rsqrt(ms + NORM_EPS) * nf_ref[...]

        @pl.when(jnp.logical_and(i % GATHER_BUFS == cur, i == n - 1))
        def _():
            for ahead in range(1, GATHER_BUFS):
                wait((cur + ahead) % GATHER_BUFS)


def _combine(pos, y_pad, h, ew, g2, nf, seq):
    t = h.shape[0]
    tm = COMBINE_TM
    tiles_per_batch = seq // tm
    ybuf = pltpu.VMEM((TOP_K, tm * FEAT_TILES, LANES), F32)
    return pl.pallas_call(
        _combine_body,
        grid_spec=pltpu.PrefetchScalarGridSpec(
            num_scalar_prefetch=1,
            grid=(t // tm,),
            in_specs=[pl.BlockSpec(memory_space=pl.ANY),
                      pl.BlockSpec((tm, D_MODEL), lambda i, pos: (i, 0)),
                      pl.BlockSpec((8, tm), lambda i, pos: (0, i)),
                      pl.BlockSpec((1, 1, D_MODEL), lambda i, pos: (i // tiles_per_batch, 0, 0)),
                      pl.BlockSpec((1, D_MODEL), lambda i, pos: (0, 0))],
            out_specs=pl.BlockSpec((tm, D_MODEL), lambda i, pos: (i, 0)),
            scratch_shapes=[ybuf] * GATHER_BUFS + [pltpu.SemaphoreType.DMA((GATHER_BUFS,))]),
        out_shape=jax.ShapeDtypeStruct((t, D_MODEL), F32),
        compiler_params=_cparams(("arbitrary",)),
        name="combine",
    )(pos, y_pad, h, ew, g2, nf)


def _dispatch_indices(ids, ranks, counts, t):
    a = t * TOP_K
    nblk = a // MOE_BLOCK + N_EXPERTS + GATHER_AHEAD * EXPERT_GROUP
    experts = jnp.arange(N_EXPERTS, dtype=jnp.int32)
    starts = jnp.cumsum(counts) - counts
    padded = (counts + MOE_BLOCK - 1) // MOE_BLOCK * MOE_BLOCK
    pends = jnp.cumsum(padded)
    pstarts = pends - padded
    block_start = jnp.arange(nblk, dtype=jnp.int32) * MOE_BLOCK
    block_e = jnp.minimum(jnp.sum((block_start[:, None] >= pends[None, :]).astype(jnp.int32), axis=1),
                          N_EXPERTS - 1)
    of_block = (block_e[:, None] == experts[None, :]).astype(jnp.int32)
    src0 = block_start + jnp.sum(of_block * (starts - pstarts)[None, :], axis=1)
    nused = pends[-1:] // MOE_BLOCK
    blocks = jnp.arange(nblk, dtype=jnp.int32)
    used = blocks < nused
    starts_run = jnp.logical_and(used, jnp.concatenate([jnp.ones((1,), bool), block_e[1:] != block_e[:-1]]))
    nranks = jnp.sum(starts_run.astype(jnp.int32), keepdims=True)
    run = jnp.cumsum(starts_run.astype(jnp.int32)) - 1
    block_rank = jnp.clip(run, 0, jnp.maximum(nranks - 1, 0))
    run_ids = jnp.arange(N_EXPERTS + 1, dtype=jnp.int32)
    expert_seq = jnp.sum(jnp.where(jnp.logical_and(starts_run[:, None], run[:, None] == run_ids[None, :]),
                                   block_e[:, None], 0), axis=0)
    meta = jnp.concatenate([nused, nranks])
    tok = jnp.arange(t, dtype=jnp.int32)[None, :]
    slot = jnp.arange(TOP_K, dtype=jnp.int32)[:, None]
    key = ids * a + tok * TOP_K + slot
    stok = (jnp.sort(key.reshape(a)) % a) // TOP_K
    ids_d = ids.reshape(a // LANES, LANES)
    pstart_of = jnp.zeros_like(ids_d)
    for e in range(N_EXPERTS):
        pstart_of = jnp.where(ids_d == e, pstarts[e], pstart_of)
    pos = (ranks.reshape(a // LANES, LANES) + pstart_of).reshape(a)
    i32 = lambda v: v.astype(jnp.int32)
    return i32(block_rank), i32(expert_seq), i32(src0), i32(stok), i32(meta), i32(pos), nblk


def kernel(x, c, ctx, c_ctx, w_ada, b_ada, norm_mix, norm_ffn, w_in, gla_lr_w, gla_lr_b, gla_norm,
           w_branch_gla, w_branch_ret, w_out, w_router_group, b_router_group, w_router_expert,
           b_router_expert, w_expert_gate, w_expert_up, w_expert_down, norm_final):
    batch, seq, d = x.shape
    ctx_len = ctx.shape[1]
    assert d == D_MODEL and w_ada.shape[0] == 1, "single-layer block with D_MODEL features"
    t = batch * seq

    c8 = jnp.zeros((8, d), F32).at[:batch].set(c).at[batch].set(c_ctx)
    mod = _ada(c8, w_ada[0], b_ada[0][None, :])
    sh1, sc1, g1, sh2, sc2, g2 = [mod[:, d * i:d * (i + 1)] for i in range(6)]
    lat = lambda m: m[:batch, None, :]
    cx = lambda m: m[batch:batch + 1, None, :]

    w_main, w_lr = _wprep(jnp.swapaxes(w_in[0], 0, 1))
    nm = norm_mix[0][None, :]

    ones = jnp.ones((batch * ctx_len, DK), F32)
    p_ctx, lr_ctx = _proj(ctx.reshape(batch * ctx_len, d), cx(sh1), cx(sc1), nm, w_main, w_lr,
                          ones, jnp.zeros_like(ones), tm=batch * ctx_len, rows_per_mod=batch * ctx_len)
    x2 = x.reshape(t, d)
    cos, sin = _rope_tables(seq)
    p_lat, lr_lat = _proj(x2, lat(sh1), lat(sc1), nm, w_main, w_lr, cos, sin, tm=PROJ_TM, rows_per_mod=seq)

    wlr_f = jnp.zeros((LANES, HEADS * DK), F32).at[:GLA_RANK].set(gla_lr_w[0, 0])
    wlr_b = jnp.zeros((LANES, HEADS * DK), F32).at[GLA_RANK:2 * GLA_RANK].set(gla_lr_w[0, 1])
    blr_f, blr_b = gla_lr_b[0, 0][None, :], gla_lr_b[0, 1][None, :]
    zero_state = jnp.zeros((batch, HEADS, DK, DV), F32)

    _, _, gs_f, gs_b, _ = _gla(p_ctx, lr_ctx, wlr_f, blr_f, wlr_b, blr_b, zero_state, zero_state,
                               batch, ctx_len, tb=ctx_len)
    gla_args = (p_lat, lr_lat, wlr_f, blr_f, wlr_b, blr_b, gs_f, gs_b, batch, seq)
    gla_f, gla_b, _, _, decay_range = _gla(*gla_args, tb=SCAN_TB)
    gla_f, gla_b = lax.cond(jnp.max(decay_range) > GLA_SAFE_RANGE,
                            lambda: tuple(_gla(*gla_args, tb=GLA_EXACT_TB, exact=True)[:2]),
                            lambda: (gla_f, gla_b))

    _, _, rs_f, rs_b = _ret(p_ctx, zero_state, zero_state, batch, ctx_len, tb=ctx_len)
    ret_f, ret_b, _, _ = _ret(p_lat, rs_f, rs_b, batch, seq, tb=SCAN_TB)

    wrt = jnp.zeros((d, ROUTER_W), F32)
    wrt = wrt.at[:, :N_GROUPS].set(w_router_group[0]).at[:, EXP_ROW0:EXP_ROW0 + N_EXPERTS].set(w_router_expert[0])
    wrt_hi = wrt.astype(BF16)
    wrt = jnp.concatenate([wrt_hi, (wrt - wrt_hi.astype(F32)).astype(BF16)], axis=1)
    brt = jnp.zeros((1, ROUTER_W), F32)
    brt = brt.at[0, :N_GROUPS].set(b_router_group[0]).at[0, EXP_ROW0:EXP_ROW0 + N_EXPERTS].set(b_router_expert[0])
    h, hn, ids8, ew8, cnt = _merge(gla_f, gla_b, ret_f, ret_b, p_lat, x2, lat(g1), lat(sh2),
                                   norm_ffn[0][None, None, :] * (1.0 + lat(sc2)),
                                   jnp.tile(gla_norm[0], HEADS)[None, :],
                                   w_branch_gla[0], w_branch_ret[0], w_out[0], wrt, brt, seq, tm=MERGE_TM)

    block_rank, expert_seq, src0, stok, meta, pos, nblk = _dispatch_indices(
        ids8[:TOP_K], ids8[TOP_K:2 * TOP_K], cnt[:, 0], t)
    y_pad = _experts(block_rank, expert_seq, src0, stok, meta, hn,
                     w_expert_gate[0], w_expert_up[0], w_expert_down[0], nblk)
    out = _combine(pos, y_pad, h, ew8, lat(g2), norm_final[None, :], seq)
    return out.reshape(batch, seq, d)
```

```python
import functools

import numpy as np
import jax
import jax.numpy as jnp
from jax import lax
from jax.experimental import pallas as pl
from jax.experimental.pallas import tpu as pltpu

F32 = jnp.float32
BF16 = jnp.bfloat16

D_MODEL = 1024
GRID_W = 64
HEADS = 4
DK = 128
DV = 256
GLA_RANK = 16
GLA_TAU = 16.0
GLA_CHUNK = 64
GLA_SUB = 16
RET_CHUNK = 128
ROPE_BASE = 10000.0
N_GROUPS = 4
EXPERTS_PER_GROUP = 8
N_EXPERTS = N_GROUPS * EXPERTS_PER_GROUP
TOP_K = 2
EXPERT_FF = 256
MOE_BLOCK = 128
NORM_EPS = 1e-6

COL_GQ, COL_GK, COL_GV, COL_GG = 0, 512, 1024, 2048
COL_RQ, COL_RK, COL_RV, COL_RG = 3072, 3584, 4096, 5120
COL_MG, COL_MR = 6144, 7168
PROJ_W = 8192
LANES = 128
FEAT_TILES = D_MODEL // LANES
ROUTER_W = 128
EXP_ROW0 = 8

VMEM_LIMIT = 56 * 1024 * 1024

PROJ_TM = 1024
PROJ_TN = 2048
PROJ_NORM_SPLIT = 4
SCAN_TB = 1024
GLA_PHASE_LAG = 2
RET_PHASE_LAG = 0
GLA_EXACT_TB = 256
GLA_SAFE_RANGE = 60.0
MERGE_TM = 512
MERGE_SPLIT = 2
COMBINE_TM = 256


def _cparams(sem):
    return pltpu.CompilerParams(dimension_semantics=sem, vmem_limit_bytes=VMEM_LIMIT)


def _sigmoid(x):
    return 1.0 / (1.0 + jnp.exp(-x))


def _silu(x):
    return x * _sigmoid(x)


def _dot(a, b):
    return jnp.dot(a, b, preferred_element_type=F32)


def _dot_nt(a, b):
    return lax.dot_general(a, b, (((1,), (1,)), ((), ())), preferred_element_type=F32)


def _ada_body(c_ref, w_ref, b_ref, o_ref):
    s = _silu(c_ref[...])
    o_ref[...] = _dot(s.astype(BF16), w_ref[...].astype(BF16)) + b_ref[...]


def _ada(c8, w, b):
    n = w.shape[1]
    tn = 1536
    return pl.pallas_call(
        _ada_body,
        grid=(n // tn,),
        in_specs=[pl.BlockSpec((8, D_MODEL), lambda j: (0, 0)),
                  pl.BlockSpec((D_MODEL, tn), lambda j: (0, j)),
                  pl.BlockSpec((1, tn), lambda j: (0, j))],
        out_specs=pl.BlockSpec((8, tn), lambda j: (0, j)),
        out_shape=jax.ShapeDtypeStruct((8, n), F32),
        compiler_params=_cparams(("arbitrary",)),
        name="ada",
    )(c8, w, b)


LR_COL0 = COL_GG + HEADS * DV
WPREP_ROWS = 1024


def _wprep_body(w_ref, wlr_ref, main_ref, lr_ref):
    main_ref[...] = w_ref[...].astype(BF16)
    pad = jnp.zeros((LANES - 2 * GLA_RANK, D_MODEL), F32)
    lr_ref[...] = jnp.concatenate([wlr_ref[...], pad], axis=0).astype(BF16)


def _wprep(wt):
    steps_before = LR_COL0 // WPREP_ROWS

    def src_row(i):
        row = jnp.where(i < steps_before, i * WPREP_ROWS, i * WPREP_ROWS + 2 * GLA_RANK)
        return (pl.multiple_of(row, 2 * GLA_RANK), 0)

    return pl.pallas_call(
        _wprep_body,
        grid=(PROJ_W // WPREP_ROWS,),
        in_specs=[pl.BlockSpec((pl.Element(WPREP_ROWS), pl.Element(D_MODEL)), src_row),
                  pl.BlockSpec((pl.Element(2 * GLA_RANK), pl.Element(D_MODEL)), lambda i: (LR_COL0, 0))],
        out_specs=[pl.BlockSpec((WPREP_ROWS, D_MODEL), lambda i: (i, 0)),
                   pl.BlockSpec((LANES, D_MODEL), lambda i: (0, 0))],
        out_shape=[jax.ShapeDtypeStruct((PROJ_W, D_MODEL), BF16), jax.ShapeDtypeStruct((LANES, D_MODEL), BF16)],
        compiler_params=_cparams(("arbitrary",)),
        name="wprep",
    )(wt, wt)


ROPE_TILE = COL_RQ // PROJ_TN
ROPE_COL0 = COL_RQ % PROJ_TN
assert ROPE_TILE != 0 and ROPE_COL0 + 2 * HEADS * DK <= PROJ_TN, "rotated columns: one tile, not the first"


def _rope(x, cos, sin):
    lane = lax.broadcasted_iota(jnp.int32, x.shape, 1)
    partner = jnp.where((lane % 64) < 32, pltpu.roll(x, DK - 32, 1), pltpu.roll(x, 32, 1))
    return x * cos + partner * sin


def _proj_body(x_ref, sh_ref, sc_ref, g_ref, w_ref, wlr_ref, cos_ref, sin_ref, o_ref, lr_ref, h_ref):
    j = pl.program_id(1)

    @pl.when(j == 0)
    def _():
        n = x_ref.shape[0] // PROJ_NORM_SPLIT
        for grp in range(PROJ_NORM_SPLIT):
            rows = pl.ds(grp * n, n)
            x = x_ref[rows, :]
            ms = jnp.mean(x * x, axis=-1, keepdims=True)
            y = x * lax.rsqrt(ms + NORM_EPS) * g_ref[...]
            hb = (y * (1.0 + sc_ref[0]) + sh_ref[0]).astype(BF16)
            h_ref[rows, :] = hb
            lr_ref[rows, :] = _dot_nt(hb, wlr_ref[...])
            o_ref[rows, :] = _dot_nt(hb, w_ref[...]).astype(BF16)

    @pl.when(jnp.logical_and(j != 0, j != ROPE_TILE))
    def _():
        o_ref[...] = _dot_nt(h_ref[...], w_ref[...]).astype(BF16)

    @pl.when(j == ROPE_TILE)
    def _():
        h = h_ref[...]
        cos, sin = cos_ref[...], sin_ref[...]
        for pair in range(HEADS):
            lo = ROPE_COL0 + 2 * DK * pair
            acc = _dot_nt(h, w_ref[lo:lo + 2 * DK, :])
            if pair >= HEADS // 2:
                acc = acc * (DK ** -0.5)
            for half in range(2):
                blk = acc[:, DK * half:DK * (half + 1)]
                o_ref[:, lo + DK * half:lo + DK * (half + 1)] = _rope(blk, cos, sin).astype(BF16)
        rope_end = ROPE_COL0 + 2 * HEADS * DK
        for lo, hi in ((0, ROPE_COL0), (rope_end, PROJ_TN)):
            if lo < hi:
                o_ref[:, lo:hi] = _dot_nt(h, w_ref[lo:hi, :]).astype(BF16)


def _proj(x2, sh, sc, g, w_main, w_lr, cos, sin, tm, rows_per_mod):
    t = x2.shape[0]
    tn = PROJ_TN
    tiles_per_mod = rows_per_mod // tm
    mod_map = lambda i, j: (i // tiles_per_mod, 0, 0)
    pos_tiles = cos.shape[0] // tm
    pos_map = lambda i, j: (i % pos_tiles, 0)
    return pl.pallas_call(
        _proj_body,
        grid=(t // tm, PROJ_W // tn),
        in_specs=[pl.BlockSpec((tm, D_MODEL), lambda i, j: (i, 0)),
                  pl.BlockSpec((1, 1, D_MODEL), mod_map),
                  pl.BlockSpec((1, 1, D_MODEL), mod_map),
                  pl.BlockSpec((1, D_MODEL), lambda i, j: (0, 0)),
                  pl.BlockSpec((tn, D_MODEL), lambda i, j: (j, 0)),
                  pl.BlockSpec((LANES, D_MODEL), lambda i, j: (0, 0)),
                  pl.BlockSpec((tm, DK), pos_map), pl.BlockSpec((tm, DK), pos_map)],
        out_specs=[pl.BlockSpec((tm, tn), lambda i, j: (i, j)),
                   pl.BlockSpec((tm, LANES), lambda i, j: (i, 0))],
        out_shape=[jax.ShapeDtypeStruct((t, PROJ_W), BF16),
                   jax.ShapeDtypeStruct((t, LANES), F32)],
        scratch_shapes=[pltpu.VMEM((tm, D_MODEL), BF16)],
        compiler_params=_cparams(("parallel", "arbitrary")),
        name="proj",
    )(x2, sh, sc, g, w_main, w_lr, cos, sin)


def _gla_consts():
    c, s = GLA_CHUNK, GLA_SUB
    msk = {}
    for rev in (False, True):
        cols = []
        for i in range(c // s):
            keys = np.arange(c)[_key_rows(i, rev)][None, :]
            t = np.arange(c)[:, None]
            visible = (keys > t) if rev else (keys <= t)
            cols.append(np.where(t // s == i, visible, False))
        msk[rev] = np.concatenate(cols, axis=1).astype(np.float32)
    return msk


def _key_rows(i, rev):
    return slice(GLA_SUB * i, GLA_CHUNK) if rev else slice(0, GLA_SUB * (i + 1))


def _log_gate(lr, w, b):
    z = _dot(lr.astype(BF16), w.astype(BF16)) + b
    return -(jnp.maximum(-z, 0.0) + jnp.log(1.0 + jnp.exp(-jnp.abs(z)))) * (1.0 / GLA_TAU)


def _subchunk_scan(la, rev):
    c = la.shape[0]
    pos = lax.broadcasted_iota(jnp.int32, la.shape, 0) % GLA_SUB
    w = la
    step = 1
    while step < GLA_SUB:
        if rev:
            w = w + jnp.where(pos < GLA_SUB - step, pltpu.roll(w, c - step, 0), 0.0)
        else:
            w = w + jnp.where(pos >= step, pltpu.roll(w, step, 0), 0.0)
        step *= 2
    return w


def _gla_intra_exact(q, kf, v, b, rev, scratch):
    q_s, b_s, o_s = scratch
    c = GLA_CHUNK
    q_s[...] = q.astype(F32) * (DK ** -0.5)
    b_s[...] = b
    vf = v.astype(F32)
    row = lax.broadcasted_iota(jnp.int32, (c, 1), 0)

    def body(t, carry):
        qt = q_s[pl.ds(t, 1), :]
        bt = b_s[pl.ds(t, 1), :]
        visible = (row > t) if rev else (row <= t)
        decay = jnp.exp(jnp.where(visible, bt - b, -jnp.inf))
        col = jnp.sum(kf * decay * qt, axis=1, keepdims=True)
        o_s[pl.ds(t, 1), :] = jnp.sum(col * vf, axis=0, keepdims=True)
        return carry

    lax.fori_loop(0, c, body, 0)
    return o_s[...]


def _gla_chunk(q, k, v, la, state_ref, mask, rev, exact_scratch=None):
    c, s = GLA_CHUNK, GLA_SUB
    nsub = c // s
    w = _subchunk_scan(la, rev)
    tot = [w[s * i:s * i + 1] if rev else w[s * (i + 1) - 1:s * (i + 1)] for i in range(nsub)]
    anchors = [None] * nsub
    acc = jnp.zeros_like(tot[0])
    worst = jnp.zeros_like(tot[0])
    for i in (reversed(range(nsub)) if rev else range(nsub)):
        anchors[i] = acc
        acc = acc + tot[i]
        worst = jnp.maximum(worst, -tot[i])
    b_end = acc
    ref = jnp.concatenate([jnp.broadcast_to(a, (s, DK)) for a in anchors], axis=0)
    b = w + ref

    qt = q.astype(F32) * (DK ** -0.5) * jnp.exp(w)
    kf = k.astype(F32)
    if exact_scratch is None:
        kstack = jnp.concatenate([kf[_key_rows(i, rev)] * jnp.exp(anchors[i] - b[_key_rows(i, rev)])
                                  for i in range(nsub)], axis=0).astype(BF16)
        raw = _dot_nt(qt.astype(BF16), kstack)
    kst = kf * jnp.exp(b_end - b)
    xt = jnp.concatenate([kst, jnp.broadcast_to(jnp.exp(b_end), (c, DK))], axis=0).T
    update = _dot(xt[:, :c].astype(BF16), v)
    state = state_ref[...]
    inter = _dot((qt * jnp.exp(ref)).astype(BF16), state.astype(BF16))
    state_ref[...] = xt[:, c:c + 1] * state + update
    yield
    if exact_scratch is None:
        vstack = jnp.concatenate([v[_key_rows(i, rev)] for i in range(nsub)], axis=0)
        intra = _dot(jnp.where(mask != 0.0, raw, 0.0).astype(BF16), vstack)
    else:
        intra = _gla_intra_exact(q, kf, v, b, rev, exact_scratch)
    return inter + intra, worst


def _round_robin(gens):
    results = [None] * len(gens)
    live = list(range(len(gens)))
    while live:
        for g in list(live):
            try:
                next(gens[g])
            except StopIteration as done:
                results[g] = done.value
                live.remove(g)
    return results


def _run_phased(chunks, emit, lag):
    def finish(tag, gen):
        try:
            next(gen)
        except StopIteration as done:
            emit(tag, done.value)
            return
        raise AssertionError("chunk generator has more than two phases")

    pending = []
    for tag, gen in chunks:
        next(gen)
        pending.append((tag, gen))
        if len(pending) > lag:
            finish(*pending.pop(0))
    for item in pending:
        finish(*item)


def _gla_body(qf_ref, kf_ref, vf_ref, lrf_ref, qb_ref, kb_ref, vb_ref, lrb_ref,
              wf_ref, bf_ref, wb_ref, bb_ref, mskf_ref, mskb_ref,
              s0f_ref, s0b_ref, of_ref, ob_ref, sf_ref, sb_ref, rng_ref, *exact_scratch, nchunks):
    @pl.when(pl.program_id(2) == 0)
    def _():
        sf_ref[...] = s0f_ref[...]
        sb_ref[...] = s0b_ref[...]

    c = GLA_CHUNK
    scratch = exact_scratch or None
    la_f = _log_gate(lrf_ref[...], wf_ref[...], bf_ref[...])
    la_b = _log_gate(lrb_ref[...], wb_ref[...], bb_ref[...])
    mskf, mskb = mskf_ref[...], mskb_ref[...]
    st_f, st_b = sf_ref.at[0, 0], sb_ref.at[0, 0]
    worst = [jnp.zeros((1, DK), F32)]

    def chunks():
        for n in range(nchunks):
            rf = slice(c * n, c * (n + 1))
            yield (of_ref, rf), _gla_chunk(qf_ref[rf, :], kf_ref[rf, :], vf_ref[rf, :], la_f[rf, :], st_f, mskf,
                                           False, scratch)
            m = nchunks - 1 - n
            rb = slice(c * m, c * (m + 1))
            yield (ob_ref, rb), _gla_chunk(qb_ref[rb, :], kb_ref[rb, :], vb_ref[rb, :], la_b[rb, :], st_b, mskb,
                                           True, scratch)

    def emit(tag, result):
        o_ref, rows = tag
        o_ref[rows, :] = result[0].astype(BF16)
        worst[0] = jnp.maximum(worst[0], result[1])

    _run_phased(chunks(), emit, GLA_PHASE_LAG)
    rng_ref[0, 0, 0] = jnp.broadcast_to(worst[0], (8, DK))


def _gla(p, lr, wlr_f, blr_f, wlr_b, blr_b, s0f, s0b, batch, seq, tb, exact=False):
    nb = seq // tb
    msk = _gla_consts()
    mskf, mskb = jnp.asarray(msk[False]), jnp.asarray(msk[True])

    def fwd(col0, width):
        return lambda b, h, i: (b * nb + i, col0 // width + h)

    def bwd(col0, width):
        return lambda b, h, i: (b * nb + nb - 1 - i, col0 // width + h)

    const2 = lambda b, h, i: (0, 0)
    headcol = lambda b, h, i: (0, h)
    st_map = lambda b, h, i: (b, h, 0, 0)
    st_spec = pl.BlockSpec((1, 1, DK, DV), st_map)
    st_shape = jax.ShapeDtypeStruct((batch, HEADS, DK, DV), F32)
    o_shape = jax.ShapeDtypeStruct((batch * seq, HEADS * DV), BF16)
    return pl.pallas_call(
        functools.partial(_gla_body, nchunks=tb // GLA_CHUNK),
        grid=(batch, HEADS, nb),
        in_specs=[pl.BlockSpec((tb, DK), fwd(COL_GQ, DK)),
                  pl.BlockSpec((tb, DK), fwd(COL_GK, DK)),
                  pl.BlockSpec((tb, DV), fwd(COL_GV, DV)),
                  pl.BlockSpec((tb, LANES), lambda b, h, i: (b * nb + i, 0)),
                  pl.BlockSpec((tb, DK), bwd(COL_GQ, DK)),
                  pl.BlockSpec((tb, DK), bwd(COL_GK, DK)),
                  pl.BlockSpec((tb, DV), bwd(COL_GV, DV)),
                  pl.BlockSpec((tb, LANES), lambda b, h, i: (b * nb + nb - 1 - i, 0)),
                  pl.BlockSpec((LANES, DK), headcol),
                  pl.BlockSpec((1, DK), headcol),
                  pl.BlockSpec((LANES, DK), headcol),
                  pl.BlockSpec((1, DK), headcol),
                  pl.BlockSpec(mskf.shape, const2),
                  pl.BlockSpec(mskb.shape, const2),
                  st_spec, st_spec],
        out_specs=[pl.BlockSpec((tb, DV), lambda b, h, i: (b * nb + i, h)),
                   pl.BlockSpec((tb, DV), lambda b, h, i: (b * nb + nb - 1 - i, h)),
                   st_spec, st_spec,
                   pl.BlockSpec((1, 1, 1, 8, DK), lambda b, h, i: (b, h, i, 0, 0))],
        out_shape=[o_shape, o_shape, st_shape, st_shape,
                   jax.ShapeDtypeStruct((batch, HEADS, nb, 8, DK), F32)],
        scratch_shapes=([pltpu.VMEM((GLA_CHUNK, DK), F32), pltpu.VMEM((GLA_CHUNK, DK), F32),
                         pltpu.VMEM((GLA_CHUNK, DV), F32)] if exact else []),
        compiler_params=_cparams(("parallel", "parallel", "arbitrary")),
        name="gla_exact" if exact else "gla",
    )(p, p, p, lr, p, p, p, lr, wlr_f, blr_f, wlr_b, blr_b, mskf, mskb, s0f, s0b)


def _ret_consts():
    c = RET_CHUNK
    hh = np.arange(HEADS, dtype=np.float64)
    lg = {False: np.log1p(-np.exp2(-5.0 - hh)), True: np.log1p(-np.exp2(-5.5 - hh))}
    t = np.arange(c, dtype=np.float64)[:, None]
    u = np.arange(c, dtype=np.float64)[None, :]
    pos = np.arange(c, dtype=np.float64)
    out = {}
    for rev in (False, True):
        g = lg[rev][:, None, None]
        if not rev:
            dmat = np.where(u <= t, np.exp((t - u) * g), 0.0)
            qd = np.exp((pos + 1.0)[None, :] * lg[rev][:, None])
            kd = np.exp((c - 1.0 - pos)[None, :] * lg[rev][:, None])
        else:
            dmat = np.where(u > t, np.exp((u - t) * g), 0.0)
            qd = np.exp((c - pos)[None, :] * lg[rev][:, None])
            kd = np.exp(pos[None, :] * lg[rev][:, None])
        cd = np.exp(c * lg[rev])
        out[rev] = (dmat.astype(np.float32),
                    np.repeat(qd[:, :, None], DV, axis=2).astype(np.float32),
                    np.repeat(kd[:, :, None], DK, axis=2).astype(np.float32),
                    np.repeat(cd[:, None, None], DV, axis=2).astype(np.float32))
    return out


def _ret_chunk(q, k, v, state_ref, dmat, qd, kd, cd):
    raw = _dot_nt(q, k)
    update = _dot((k.astype(F32) * kd).T.astype(BF16), v)
    state = state_ref[...]
    inter = _dot(q, state.astype(BF16)) * qd
    state_ref[...] = cd * state + update
    yield
    return inter + _dot((raw * dmat).astype(BF16), v)


def _ret_body(qf_ref, kf_ref, vf_ref, qb_ref, kb_ref, vb_ref,
              dmf_ref, qdf_ref, kdf_ref, cdf_ref, dmb_ref, qdb_ref, kdb_ref, cdb_ref,
              s0f_ref, s0b_ref, of_ref, ob_ref, sf_ref, sb_ref, *, nchunks):
    @pl.when(pl.program_id(2) == 0)
    def _():
        sf_ref[...] = s0f_ref[...]
        sb_ref[...] = s0b_ref[...]

    c = RET_CHUNK
    cf = (dmf_ref[0], qdf_ref[0], kdf_ref[0], cdf_ref[0])
    cb = (dmb_ref[0], qdb_ref[0], kdb_ref[0], cdb_ref[0])
    st_f, st_b = sf_ref.at[0, 0], sb_ref.at[0, 0]
    def chunks():
        for n in range(nchunks):
            rf = slice(c * n, c * (n + 1))
            yield (of_ref, rf), _ret_chunk(qf_ref[rf, :], kf_ref[rf, :], vf_ref[rf, :], st_f, *cf)
            m = nchunks - 1 - n
            rb = slice(c * m, c * (m + 1))
            yield (ob_ref, rb), _ret_chunk(qb_ref[rb, :], kb_ref[rb, :], vb_ref[rb, :], st_b, *cb)

    def emit(tag, out):
        o_ref, rows = tag
        o_ref[rows, :] = out.astype(BF16)

    _run_phased(chunks(), emit, RET_PHASE_LAG)


def _ret(p, s0f, s0b, batch, seq, tb):
    nb = seq // tb
    consts = _ret_consts()

    def fwd(col0, width):
        return lambda b, h, i: (b * nb + i, col0 // width + h)

    def bwd(col0, width):
        return lambda b, h, i: (b * nb + nb - 1 - i, col0 // width + h)

    head3 = lambda b, h, i: (h, 0, 0)
    st_spec = pl.BlockSpec((1, 1, DK, DV), lambda b, h, i: (b, h, 0, 0))
    st_shape = jax.ShapeDtypeStruct((batch, HEADS, DK, DV), F32)
    o_shape = jax.ShapeDtypeStruct((batch * seq, HEADS * DV), BF16)
    const_specs = []
    const_args = []
    for rev in (False, True):
        for a in consts[rev]:
            const_specs.append(pl.BlockSpec((1,) + a.shape[1:], head3))
            const_args.append(jnp.asarray(a))
    return pl.pallas_call(
        functools.partial(_ret_body, nchunks=tb // RET_CHUNK),
        grid=(batch, HEADS, nb),
        in_specs=[pl.BlockSpec((tb, DK), fwd(COL_RQ, DK)),
                  pl.BlockSpec((tb, DK), fwd(COL_RK, DK)),
                  pl.BlockSpec((tb, DV), fwd(COL_RV, DV)),
                  pl.BlockSpec((tb, DK), bwd(COL_RQ, DK)),
                  pl.BlockSpec((tb, DK), bwd(COL_RK, DK)),
                  pl.BlockSpec((tb, DV), bwd(COL_RV, DV))]
                 + const_specs + [st_spec, st_spec],
        out_specs=[pl.BlockSpec((tb, DV), lambda b, h, i: (b * nb + i, h)),
                   pl.BlockSpec((tb, DV), lambda b, h, i: (b * nb + nb - 1 - i, h)),
                   st_spec, st_spec],
        out_shape=[o_shape, o_shape, st_shape, st_shape],
        compiler_params=_cparams(("parallel", "parallel", "arbitrary")),
        name="ret",
    )(p, p, p, p, p, p, *const_args, s0f, s0b)


def _rope_tables(seq):
    n = DK // 4
    inv = np.float32(ROPE_BASE) ** (-np.arange(n, dtype=np.float32) / np.float32(n))
    pos = np.arange(seq)
    ar = (pos // GRID_W).astype(np.float32)[:, None] * inv[None, :]
    ac = (pos % GRID_W).astype(np.float32)[:, None] * inv[None, :]
    cos = np.concatenate([np.cos(ar), np.cos(ar), np.cos(ac), np.cos(ac)], axis=1)
    sin = np.concatenate([-np.sin(ar), np.sin(ar), -np.sin(ac), np.sin(ac)], axis=1)
    return jnp.asarray(cos, F32), jnp.asarray(sin, F32)


def _route(logits_t):
    g = [logits_t[i:i + 1] for i in range(N_GROUPS)]
    gmax = jnp.maximum(jnp.maximum(g[0], g[1]), jnp.maximum(g[2], g[3]))
    gsel = jnp.where(g[0] == gmax, 0, jnp.where(g[1] == gmax, 1, jnp.where(g[2] == gmax, 2, 3)))
    gsum = (jnp.exp(g[0] - gmax) + jnp.exp(g[1] - gmax)) + (jnp.exp(g[2] - gmax) + jnp.exp(g[3] - gmax))
    gw = 1.0 / gsum
    e = [logits_t[EXP_ROW0 + EXPERTS_PER_GROUP * i:EXP_ROW0 + EXPERTS_PER_GROUP * (i + 1)]
         for i in range(N_GROUPS)]
    el = jnp.where(gsel == 0, e[0], jnp.where(gsel == 1, e[1], jnp.where(gsel == 2, e[2], e[3])))
    row = lax.broadcasted_iota(jnp.int32, el.shape, 0).astype(F32)
    none = float(EXPERTS_PER_GROUP)
    m1 = jnp.max(el, axis=0, keepdims=True)
    i1 = jnp.min(jnp.where(el == m1, row, none), axis=0, keepdims=True)
    el2 = jnp.where(row == i1, -jnp.inf, el)
    m2 = jnp.max(el2, axis=0, keepdims=True)
    i2 = jnp.min(jnp.where(el2 == m2, row, none), axis=0, keepdims=True)
    r = jnp.exp(m2 - m1)
    w1 = gw / (1.0 + r)
    w2 = gw * r / (1.0 + r)
    base = gsel * EXPERTS_PER_GROUP
    ids = jnp.concatenate([base + i1.astype(jnp.int32), base + i2.astype(jnp.int32)], axis=0)
    return ids, jnp.concatenate([w1, w2], axis=0)


def _store_token_tiled(ref, val):
    m = val.shape[0]
    for j in range(FEAT_TILES):
        ref[pl.ds(j, m, stride=FEAT_TILES), :] = val[:, LANES * j:LANES * (j + 1)]


def _load_token_tiled(ref, m):
    return jnp.concatenate([ref[pl.ds(j, m, stride=FEAT_TILES), :] for j in range(FEAT_TILES)], axis=1)


def _merge_body(gf_ref, gb_ref, rf_ref, rb_ref, gg_ref, rg_ref, mg_ref, mr_ref, x_ref,
                g1_ref, sh2_ref, nfs_ref, gn_ref, wg_ref, wr_ref, wo_ref, wrt_ref, brt_ref, upper_ref,
                h_ref, hn_ref, ids_ref, rank_ref, ew_ref, cnt_ref, wg_s, wr_s, wo_s, stage, sem):
    @pl.when(pl.program_id(0) == 0)
    def _():
        cnt_ref[...] = jnp.zeros_like(cnt_ref)
        for w_hbm, w_s in ((wg_ref, wg_s), (wr_ref, wr_s), (wo_ref, wo_s)):
            copy = pltpu.make_async_copy(w_hbm, stage, sem.at[0])
            copy.start()
            copy.wait()
            w_s[...] = stage[...].astype(BF16)

    def rows_to_logits(r0, n):
        rows = pl.ds(r0, n)
        og = (gf_ref[rows, :] + gb_ref[rows, :]).astype(F32)
        orr = (rf_ref[rows, :] + rb_ref[rows, :]).astype(F32)
        gparts, rparts = [], []
        for hh in range(HEADS):
            seg = og[:, DV * hh:DV * (hh + 1)]
            ms = jnp.mean(seg * seg, axis=-1, keepdims=True)
            gparts.append(seg * lax.rsqrt(ms + NORM_EPS))
            seg = orr[:, DV * hh:DV * (hh + 1)]
            mu = jnp.mean(seg, axis=-1, keepdims=True)
            cen = seg - mu
            var = jnp.mean(cen * cen, axis=-1, keepdims=True)
            rparts.append(cen * lax.rsqrt(var + NORM_EPS))
        o_gla = (jnp.concatenate(gparts, axis=1) * gn_ref[...]).astype(BF16) * _silu(gg_ref[rows, :])
        o_ret = jnp.concatenate(rparts, axis=1).astype(BF16) * _silu(rg_ref[rows, :])
        a_gla = _dot(o_gla, wg_s[...])
        a_ret = _dot(o_ret, wr_s[...])
        yield
        y = _sigmoid(mg_ref[rows, :]) * a_gla.astype(BF16) + _sigmoid(mr_ref[rows, :]) * a_ret.astype(BF16)
        out = _dot(y, wo_s[...])
        yield
        h = x_ref[rows, :] + g1_ref[0] * out
        h_ref[rows, :] = h
        ms = jnp.mean(h * h, axis=-1, keepdims=True)
        hn = h * lax.rsqrt(ms + NORM_EPS) * nfs_ref[0] + sh2_ref[0]
        _store_token_tiled(hn_ref.at[pl.ds(r0 * FEAT_TILES, n * FEAT_TILES)], hn)
        hn_hi = hn.astype(BF16)
        hn_lo = (hn - hn_hi.astype(F32)).astype(BF16)
        both_w = _dot(hn_hi, wrt_ref[...])
        lo_w = _dot(hn_lo, wrt_ref[:, :ROUTER_W])
        yield
        return (both_w[:, :ROUTER_W] + both_w[:, ROUTER_W:] + lo_w) + brt_ref[...]

    tm = x_ref.shape[0]
    n = tm // MERGE_SPLIT
    logits = jnp.concatenate(_round_robin([rows_to_logits(g * n, n) for g in range(MERGE_SPLIT)]), axis=0)
    ids, ew = _route(logits.T)

    erow = lax.broadcasted_iota(jnp.int32, (N_EXPERTS, tm), 0)
    oh0 = jnp.where(erow == ids[0:1], 1.0, 0.0)
    oh1 = jnp.where(erow == ids[1:2], 1.0, 0.0)
    both = oh0 + oh1
    before = _dot(both.astype(BF16), upper_ref[...]) + cnt_ref[:, 0:1].astype(F32)
    rank0 = jnp.sum(oh0 * before, axis=0, keepdims=True)
    rank1 = jnp.sum(oh1 * before, axis=0, keepdims=True)
    total = cnt_ref[:, 0:1] + jnp.sum(both, axis=1, keepdims=True).astype(jnp.int32)
    cnt_ref[...] = jnp.broadcast_to(total, cnt_ref.shape)
    unused = jnp.zeros((8 - TOP_K, tm), jnp.int32)
    ids_ref[...] = jnp.concatenate([ids, unused], axis=0)
    rank_ref[...] = jnp.concatenate([rank0.astype(jnp.int32), rank1.astype(jnp.int32), unused], axis=0)
    ew_ref[...] = jnp.concatenate([ew, jnp.zeros((8 - TOP_K, tm), F32)], axis=0)


def _merge(gf, gb, rf, rb, p, x2, g1, sh2, nfs, gn, wg, wr, wo, wrt, brt, seq, tm):
    t = x2.shape[0]
    tiles_per_batch = seq // tm
    row = lambda i: (i, 0)
    mod = lambda i: (i // tiles_per_batch, 0, 0)
    const = lambda i: (0, 0)
    tok = pl.BlockSpec((tm, D_MODEL), row)
    vec = pl.BlockSpec((1, D_MODEL), const)
    modspec = pl.BlockSpec((1, 1, D_MODEL), mod)
    wspec = pl.BlockSpec(memory_space=pl.ANY)
    upper = jnp.asarray(np.triu(np.ones((tm, tm), np.float32), 1), BF16)

    def pcol(col0):
        return pl.BlockSpec((tm, D_MODEL), lambda i: (i, col0 // D_MODEL))

    return pl.pallas_call(
        _merge_body,
        grid=(t // tm,),
        in_specs=[tok, tok, tok, tok, pcol(COL_GG), pcol(COL_RG), pcol(COL_MG), pcol(COL_MR), tok,
                  modspec, modspec, modspec, vec, wspec, wspec, wspec,
                  pl.BlockSpec((D_MODEL, 2 * ROUTER_W), const), pl.BlockSpec((1, ROUTER_W), const),
                  pl.BlockSpec((tm, tm), const)],
        out_specs=[tok, pl.BlockSpec((tm * FEAT_TILES, LANES), row),
                   pl.BlockSpec((8, tm), lambda i: (0, i)), pl.BlockSpec((8, tm), lambda i: (0, i)),
                   pl.BlockSpec((8, tm), lambda i: (0, i)),
                   pl.BlockSpec((N_EXPERTS, LANES), const)],
        out_shape=[jax.ShapeDtypeStruct((t, D_MODEL), F32), jax.ShapeDtypeStruct((t * FEAT_TILES, LANES), F32),
                   jax.ShapeDtypeStruct((8, t), jnp.int32), jax.ShapeDtypeStruct((8, t), jnp.int32),
                   jax.ShapeDtypeStruct((8, t), F32),
                   jax.ShapeDtypeStruct((N_EXPERTS, LANES), jnp.int32)],
        scratch_shapes=[pltpu.VMEM((D_MODEL, D_MODEL), BF16)] * 3
                       + [pltpu.VMEM((D_MODEL, D_MODEL), F32), pltpu.SemaphoreType.DMA((1,))],
        compiler_params=_cparams(("arbitrary",)),
        name="merge",
    )(gf, gb, rf, rb, p, p, p, p, x2, g1, sh2, nfs, gn, wg, wr, wo, wrt, brt, upper)


GATHER_UNROLL = 8


def _aligned(tok):
    off = tok * FEAT_TILES
    return off if isinstance(off, int) else pl.multiple_of(off, FEAT_TILES)


def _token_copy(src_hbm, dst, sem, src_tok, dst_tok):
    return pltpu.make_async_copy(src_hbm.at[pl.ds(_aligned(src_tok), FEAT_TILES), :],
                                 dst.at[pl.ds(_aligned(dst_tok), FEAT_TILES), :], sem)


def _wait_tokens(src_hbm, dst, sem):
    pltpu.make_async_copy(src_hbm.at[pl.ds(0, dst.shape[0]), :], dst, sem).wait()


GATHER_BUFS = 3
GATHER_AHEAD = GATHER_BUFS - 1


def _issue_rows(n, start_row, unrolled):
    if unrolled:
        for r in range(n):
            start_row(r, r % 2)
    else:
        def body(g, carry):
            for u in range(GATHER_UNROLL):
                start_row(g * GATHER_UNROLL + u, u % 2)
            return carry
        lax.fori_loop(0, n // GATHER_UNROLL, body, 0)


EXPERT_GROUP = 2


EXPERT_W_SLOTS = 2
assert EXPERT_GROUP <= EXPERT_W_SLOTS, "the blocks of one step may belong to EXPERT_GROUP consecutive runs"


def _expert_body(rank_ref, eseq_ref, src0_ref, stok_ref, meta_ref, hn_hbm, wg_hbm, wu_hbm, wd_hbm, y_ref, *scratch):
    bufs, sems = scratch[:GATHER_BUFS], scratch[GATHER_BUFS]
    stage, w_s, wsem = scratch[GATHER_BUFS + 1:GATHER_BUFS + 4], scratch[GATHER_BUFS + 4:GATHER_BUFS + 7], scratch[-1]
    w_hbm = (wg_hbm, wu_hbm, wd_hbm)
    i = pl.program_id(0)
    nused, nranks = meta_ref[0], meta_ref[1]
    nsteps = (nused + EXPERT_GROUP - 1) // EXPERT_GROUP
    last = stok_ref.shape[0] - 1
    rows = MOE_BLOCK * FEAT_TILES

    def issue(step, s, unrolled):
        bases = [src0_ref[step * EXPERT_GROUP + g] for g in range(EXPERT_GROUP)]

        def start(g, r, priority):
            tok = stok_ref[jnp.minimum(bases[g] + r, last)]
            _token_copy(hn_hbm, bufs[s], sems.at[s], tok, g * MOE_BLOCK + r).start(priority=priority)

        if unrolled:
            _issue_rows(EXPERT_GROUP * MOE_BLOCK, lambda j, p: start(j // MOE_BLOCK, j % MOE_BLOCK, p), True)
        else:
            for g in range(EXPERT_GROUP):
                _issue_rows(MOE_BLOCK, lambda r, p, g=g: start(g, r, p), False)

    def weight_copies(rank):
        slot = rank % EXPERT_W_SLOTS
        expert = eseq_ref[rank]
        return [pltpu.make_async_copy(w_hbm[k].at[expert], stage[k].at[slot], wsem.at[slot]) for k in range(3)]

    def activate(rank):
        slot = rank % EXPERT_W_SLOTS
        for copy in weight_copies(rank):
            copy.wait()
        for k in range(3):
            w_s[k][slot] = stage[k][slot].astype(BF16)

        @pl.when(rank + 1 < nranks)
        def _():
            for copy in weight_copies(rank + 1):
                copy.start()

    @pl.when(i == 0)
    def _():
        for b in range(GATHER_AHEAD):
            issue(b, b, False)

        @pl.when(nranks > 0)
        def _():
            for copy in weight_copies(0):
                copy.start()

    for g in range(EXPERT_GROUP):
        blk = i * EXPERT_GROUP + g
        rank = rank_ref[blk]
        first_of_run = jnp.logical_or(blk == 0, rank != rank_ref[jnp.maximum(blk - 1, 0)])

        @pl.when(jnp.logical_and(blk < nused, first_of_run))
        def _():
            activate(rank)

    def block_phases(cur, g):
        part = pl.ds(g * rows, rows)
        slot = rank_ref[i * EXPERT_GROUP + g] % EXPERT_W_SLOTS
        xb = _load_token_tiled(bufs[cur].at[part], MOE_BLOCK).astype(BF16)
        gate = _dot(xb, w_s[0][slot])
        up = _dot(xb, w_s[1][slot])
        yield
        y = _dot((_silu(gate) * up).astype(BF16), w_s[2][slot])
        yield
        _store_token_tiled(y_ref.at[part], y)

    for cur in range(GATHER_BUFS):
        mine = i % GATHER_BUFS == cur

        @pl.when(jnp.logical_and(mine, i < nsteps))
        def _():
            _wait_tokens(hn_hbm, bufs[cur], sems.at[cur])
            issue(i + GATHER_AHEAD, (cur + GATHER_AHEAD) % GATHER_BUFS, True)
            _round_robin([block_phases(cur, g) for g in range(EXPERT_GROUP)])

        @pl.when(jnp.logical_and(mine, jnp.logical_and(i >= nsteps, i < nsteps + GATHER_AHEAD)))
        def _():
            _wait_tokens(hn_hbm, bufs[cur], sems.at[cur])

    @pl.when(i >= nsteps)
    def _():
        y_ref[...] = jnp.zeros_like(y_ref)


def _experts(block_rank, expert_seq, src0, stok, meta, hn, w_gate, w_up, w_down, nblk):
    rows = EXPERT_GROUP * MOE_BLOCK * FEAT_TILES
    w_shapes = ((D_MODEL, EXPERT_FF), (D_MODEL, EXPERT_FF), (EXPERT_FF, D_MODEL))
    any_spec = pl.BlockSpec(memory_space=pl.ANY)
    return pl.pallas_call(
        _expert_body,
        grid_spec=pltpu.PrefetchScalarGridSpec(
            num_scalar_prefetch=5,
            grid=(nblk // EXPERT_GROUP,),
            in_specs=[any_spec, any_spec, any_spec, any_spec],
            out_specs=pl.BlockSpec((rows, LANES), lambda i, *prefetch: (i, 0)),
            scratch_shapes=[pltpu.VMEM((rows, LANES), F32)] * GATHER_BUFS
                           + [pltpu.SemaphoreType.DMA((GATHER_BUFS,))]
                           + [pltpu.VMEM((EXPERT_W_SLOTS,) + shape, F32) for shape in w_shapes]
                           + [pltpu.VMEM((EXPERT_W_SLOTS,) + shape, BF16) for shape in w_shapes]
                           + [pltpu.SemaphoreType.DMA((EXPERT_W_SLOTS,))]),
        out_shape=jax.ShapeDtypeStruct((nblk // EXPERT_GROUP * rows, LANES), F32),
        compiler_params=_cparams(("arbitrary",)),
        name="experts",
    )(block_rank, expert_seq, src0, stok, meta, hn, w_gate, w_up, w_down)


def _combine_body(pos_ref, y_hbm, h_ref, ew_ref, g2_ref, nf_ref, o_ref, *scratch):
    i = pl.program_id(0)
    n = pl.num_programs(0)
    tm = COMBINE_TM
    t = pos_ref.shape[0] // TOP_K
    bufs, sems = scratch[:GATHER_BUFS], scratch[GATHER_BUFS]

    def issue(tile, s, unrolled):
        def start_row(j, priority):
            k, r = j % TOP_K, j // TOP_K
            _token_copy(y_hbm, bufs[s].at[k], sems.at[s], pos_ref[k * t + tile * tm + r], r).start(priority=priority)
        _issue_rows(tm * TOP_K, start_row, unrolled)

    def wait(s):
        for k in range(TOP_K):
            _wait_tokens(y_hbm, bufs[s].at[k], sems.at[s])

    @pl.when(i == 0)
    def _():
        for b in range(GATHER_AHEAD):
            issue(b, b, False)

    for cur in range(GATHER_BUFS):
        @pl.when(i % GATHER_BUFS == cur)
        def _():
            wait(cur)
            issue(jnp.minimum(i + GATHER_AHEAD, n - 1), (cur + GATHER_AHEAD) % GATHER_BUFS, True)
            wt = jnp.concatenate([ew_ref[...]] * (LANES // 8), axis=0).T
            moe = (wt[:, 0:1] * _load_token_tiled(bufs[cur].at[0], tm)
                   + wt[:, 1:2] * _load_token_tiled(bufs[cur].at[1], tm))
            h = h_ref[...] + g2_ref[0] * moe
            ms = jnp.mean(h * h, axis=-1, keepdims=True)
            o_ref[...] = h * lax.rsqrt(ms + NORM_EPS) * nf_ref[...]

        @pl.when(jnp.logical_and(i % GATHER_BUFS == cur, i == n - 1))
        def _():
            for ahead in range(1, GATHER_BUFS):
                wait((cur + ahead) % GATHER_BUFS)


def _combine(pos, y_pad, h, ew, g2, nf, seq):
    t = h.shape[0]
    tm = COMBINE_TM
    tiles_per_batch = seq // tm
    ybuf = pltpu.VMEM((TOP_K, tm * FEAT_TILES, LANES), F32)
    return pl.pallas_call(
        _combine_body,
        grid_spec=pltpu.PrefetchScalarGridSpec(
            num_scalar_prefetch=1,
            grid=(t // tm,),
            in_specs=[pl.BlockSpec(memory_space=pl.ANY),
                      pl.BlockSpec((tm, D_MODEL), lambda i, pos: (i, 0)),
                      pl.BlockSpec((8, tm), lambda i, pos: (0, i)),
                      pl.BlockSpec((1, 1, D_MODEL), lambda i, pos: (i // tiles_per_batch, 0, 0)),
                      pl.BlockSpec((1, D_MODEL), lambda i, pos: (0, 0))],
            out_specs=pl.BlockSpec((tm, D_MODEL), lambda i, pos: (i, 0)),
            scratch_shapes=[ybuf] * GATHER_BUFS + [pltpu.SemaphoreType.DMA((GATHER_BUFS,))]),
        out_shape=jax.ShapeDtypeStruct((t, D_MODEL), F32),
        compiler_params=_cparams(("arbitrary",)),
        name="combine",
    )(pos, y_pad, h, ew, g2, nf)


def _dispatch_indices(ids, ranks, counts, t):
    a = t * TOP_K
    nblk = a // MOE_BLOCK + N_EXPERTS + GATHER_AHEAD * EXPERT_GROUP
    experts = jnp.arange(N_EXPERTS, dtype=jnp.int32)
    starts = jnp.cumsum(counts) - counts
    padded = (counts + MOE_BLOCK - 1) // MOE_BLOCK * MOE_BLOCK
    pends = jnp.cumsum(padded)
    pstarts = pends - padded
    block_start = jnp.arange(nblk, dtype=jnp.int32) * MOE_BLOCK
    block_e = jnp.minimum(jnp.sum((block_start[:, None] >= pends[None, :]).astype(jnp.int32), axis=1),
                          N_EXPERTS - 1)
    of_block = (block_e[:, None] == experts[None, :]).astype(jnp.int32)
    src0 = block_start + jnp.sum(of_block * (starts - pstarts)[None, :], axis=1)
    nused = pends[-1:] // MOE_BLOCK
    blocks = jnp.arange(nblk, dtype=jnp.int32)
    used = blocks < nused
    starts_run = jnp.logical_and(used, jnp.concatenate([jnp.ones((1,), bool), block_e[1:] != block_e[:-1]]))
    nranks = jnp.sum(starts_run.astype(jnp.int32), keepdims=True)
    run = jnp.cumsum(starts_run.astype(jnp.int32)) - 1
    block_rank = jnp.clip(run, 0, jnp.maximum(nranks - 1, 0))
    run_ids = jnp.arange(N_EXPERTS + 1, dtype=jnp.int32)
    expert_seq = jnp.sum(jnp.where(jnp.logical_and(starts_run[:, None], run[:, None] == run_ids[None, :]),
                                   block_e[:, None], 0), axis=0)
    meta = jnp.concatenate([nused, nranks])
    tok = jnp.arange(t, dtype=jnp.int32)[None, :]
    slot = jnp.arange(ids.shape[0], dtype=jnp.int32)[:, None]
    key = ids * a + tok * TOP_K + slot
    stok = (jnp.sort(key[:TOP_K].reshape(a)) % a) // TOP_K
    pstart_of = jnp.zeros_like(ids)
    for e in range(N_EXPERTS):
        pstart_of = jnp.where(ids == e, pstarts[e], pstart_of)
    pos = (ranks + pstart_of)[:TOP_K].reshape(a)
    i32 = lambda v: v.astype(jnp.int32)
    return i32(block_rank), i32(expert_seq), i32(src0), i32(stok), i32(meta), i32(pos), nblk


def kernel(x, c, ctx, c_ctx, w_ada, b_ada, norm_mix, norm_ffn, w_in, gla_lr_w, gla_lr_b, gla_norm,
           w_branch_gla, w_branch_ret, w_out, w_router_group, b_router_group, w_router_expert,
           b_router_expert, w_expert_gate, w_expert_up, w_expert_down, norm_final):
    batch, seq, d = x.shape
    ctx_len = ctx.shape[1]
    assert d == D_MODEL and w_ada.shape[0] == 1, "single-layer block with D_MODEL features"
    t = batch * seq

    c8 = jnp.zeros((8, d), F32).at[:batch].set(c).at[batch].set(c_ctx)
    mod = _ada(c8, w_ada[0], b_ada[0][None, :])
    sh1, sc1, g1, sh2, sc2, g2 = [mod[:, d * i:d * (i + 1)] for i in range(6)]
    lat = lambda m: m[:batch, None, :]
    cx = lambda m: m[batch:batch + 1, None, :]

    w_main, w_lr = _wprep(jnp.swapaxes(w_in[0], 0, 1))
    nm = norm_mix[0][None, :]

    ones = jnp.ones((batch * ctx_len, DK), F32)
    p_ctx, lr_ctx = _proj(ctx.reshape(batch * ctx_len, d), cx(sh1), cx(sc1), nm, w_main, w_lr,
                          ones, jnp.zeros_like(ones), tm=batch * ctx_len, rows_per_mod=batch * ctx_len)
    x2 = x.reshape(t, d)
    cos, sin = _rope_tables(seq)
    p_lat, lr_lat = _proj(x2, lat(sh1), lat(sc1), nm, w_main, w_lr, cos, sin, tm=PROJ_TM, rows_per_mod=seq)

    wlr_f = jnp.zeros((LANES, HEADS * DK), F32).at[:GLA_RANK].set(gla_lr_w[0, 0])
    wlr_b = jnp.zeros((LANES, HEADS * DK), F32).at[GLA_RANK:2 * GLA_RANK].set(gla_lr_w[0, 1])
    blr_f, blr_b = gla_lr_b[0, 0][None, :], gla_lr_b[0, 1][None, :]
    zero_state = jnp.zeros((batch, HEADS, DK, DV), F32)

    _, _, gs_f, gs_b, _ = _gla(p_ctx, lr_ctx, wlr_f, blr_f, wlr_b, blr_b, zero_state, zero_state,
                               batch, ctx_len, tb=ctx_len)
    gla_args = (p_lat, lr_lat, wlr_f, blr_f, wlr_b, blr_b, gs_f, gs_b, batch, seq)
    gla_f, gla_b, _, _, decay_range = _gla(*gla_args, tb=SCAN_TB)
    gla_f, gla_b = lax.cond(jnp.max(decay_range) > GLA_SAFE_RANGE,
                            lambda: tuple(_gla(*gla_args, tb=GLA_EXACT_TB, exact=True)[:2]),
                            lambda: (gla_f, gla_b))

    _, _, rs_f, rs_b = _ret(p_ctx, zero_state, zero_state, batch, ctx_len, tb=ctx_len)
    ret_f, ret_b, _, _ = _ret(p_lat, rs_f, rs_b, batch, seq, tb=SCAN_TB)

    wrt = jnp.zeros((d, ROUTER_W), F32)
    wrt = wrt.at[:, :N_GROUPS].set(w_router_group[0]).at[:, EXP_ROW0:EXP_ROW0 + N_EXPERTS].set(w_router_expert[0])
    wrt_hi = wrt.astype(BF16)
    wrt = jnp.concatenate([wrt_hi, (wrt - wrt_hi.astype(F32)).astype(BF16)], axis=1)
    brt = jnp.zeros((1, ROUTER_W), F32)
    brt = brt.at[0, :N_GROUPS].set(b_router_group[0]).at[0, EXP_ROW0:EXP_ROW0 + N_EXPERTS].set(b_router_expert[0])
    h, hn, ids8, rank8, ew8, cnt = _merge(gla_f, gla_b, ret_f, ret_b, p_lat, x2, lat(g1), lat(sh2),
                                          norm_ffn[0][None, None, :] * (1.0 + lat(sc2)),
                                          jnp.tile(gla_norm[0], HEADS)[None, :],
                                          w_branch_gla[0], w_branch_ret[0], w_out[0], wrt, brt, seq, tm=MERGE_TM)

    block_rank, expert_seq, src0, stok, meta, pos, nblk = _dispatch_indices(ids8, rank8, cnt[:, 0], t)
    y_pad = _experts(block_rank, expert_seq, src0, stok, meta, hn,
                     w_expert_gate[0], w_expert_up[0], w_expert_down[0], nblk)
    out = _combine(pos, y_pad, h, ew8, lat(g2), norm_final[None, :], seq)
    return out.reshape(batch, seq, d)
```

```python
import functools

import numpy as np
import jax
import jax.numpy as jnp
from jax import lax
from jax.experimental import pallas as pl
from jax.experimental.pallas import tpu as pltpu

F32 = jnp.float32
BF16 = jnp.bfloat16

D_MODEL = 1024
GRID_W = 64
HEADS = 4
DK = 128
DV = 256
GLA_RANK = 16
GLA_TAU = 16.0
GLA_CHUNK = 64
GLA_SUB = 16
RET_CHUNK = 128
ROPE_BASE = 10000.0
N_GROUPS = 4
EXPERTS_PER_GROUP = 8
N_EXPERTS = N_GROUPS * EXPERTS_PER_GROUP
TOP_K = 2
EXPERT_FF = 256
MOE_BLOCK = 128
NORM_EPS = 1e-6

COL_GQ, COL_GK, COL_GV, COL_GG = 0, 512, 1024, 2048
COL_RQ, COL_RK, COL_RV, COL_RG = 3072, 3584, 4096, 5120
COL_MG, COL_MR = 6144, 7168
PROJ_W = 8192
LANES = 128
FEAT_TILES = D_MODEL // LANES
ROUTER_W = 128
EXP_ROW0 = 8

VMEM_LIMIT = 56 * 1024 * 1024

PROJ_TM = 1024
PROJ_TN = 2048
PROJ_NORM_SPLIT = 4
SCAN_TB = 1024
GLA_PHASE_LAG = 2
RET_PHASE_LAG = 2
GLA_EXACT_TB = 256
GLA_SAFE_RANGE = 60.0
MERGE_TM = 512
MERGE_SPLIT = 2
COMBINE_TM = 256


def _cparams(sem):
    return pltpu.CompilerParams(dimension_semantics=sem, vmem_limit_bytes=VMEM_LIMIT)


def _sigmoid(x):
    return 1.0 / (1.0 + jnp.exp(-x))


def _silu(x):
    return x * _sigmoid(x)


def _dot(a, b):
    return jnp.dot(a, b, preferred_element_type=F32)


def _dot_nt(a, b):
    return lax.dot_general(a, b, (((1,), (1,)), ((), ())), preferred_element_type=F32)


def _ada_body(c_ref, w_ref, b_ref, o_ref):
    s = _silu(c_ref[...])
    o_ref[...] = _dot(s.astype(BF16), w_ref[...].astype(BF16)) + b_ref[...]


def _ada(c8, w, b):
    n = w.shape[1]
    tn = 1536
    return pl.pallas_call(
        _ada_body,
        grid=(n // tn,),
        in_specs=[pl.BlockSpec((8, D_MODEL), lambda j: (0, 0)),
                  pl.BlockSpec((D_MODEL, tn), lambda j: (0, j)),
                  pl.BlockSpec((1, tn), lambda j: (0, j))],
        out_specs=pl.BlockSpec((8, tn), lambda j: (0, j)),
        out_shape=jax.ShapeDtypeStruct((8, n), F32),
        compiler_params=_cparams(("arbitrary",)),
        name="ada",
    )(c8, w, b)


LR_COL0 = COL_GG + HEADS * DV
WPREP_ROWS = 1024


def _wprep_body(w_ref, wlr_ref, main_ref, lr_ref):
    main_ref[...] = w_ref[...].astype(BF16)
    pad = jnp.zeros((LANES - 2 * GLA_RANK, D_MODEL), F32)
    lr_ref[...] = jnp.concatenate([wlr_ref[...], pad], axis=0).astype(BF16)


def _wprep(wt):
    steps_before = LR_COL0 // WPREP_ROWS

    def src_row(i):
        row = jnp.where(i < steps_before, i * WPREP_ROWS, i * WPREP_ROWS + 2 * GLA_RANK)
        return (pl.multiple_of(row, 2 * GLA_RANK), 0)

    return pl.pallas_call(
        _wprep_body,
        grid=(PROJ_W // WPREP_ROWS,),
        in_specs=[pl.BlockSpec((pl.Element(WPREP_ROWS), pl.Element(D_MODEL)), src_row),
                  pl.BlockSpec((pl.Element(2 * GLA_RANK), pl.Element(D_MODEL)), lambda i: (LR_COL0, 0))],
        out_specs=[pl.BlockSpec((WPREP_ROWS, D_MODEL), lambda i: (i, 0)),
                   pl.BlockSpec((LANES, D_MODEL), lambda i: (0, 0))],
        out_shape=[jax.ShapeDtypeStruct((PROJ_W, D_MODEL), BF16), jax.ShapeDtypeStruct((LANES, D_MODEL), BF16)],
        compiler_params=_cparams(("arbitrary",)),
        name="wprep",
    )(wt, wt)


ROPE_TILE = COL_RQ // PROJ_TN
ROPE_COL0 = COL_RQ % PROJ_TN
assert ROPE_TILE != 0 and ROPE_COL0 + 2 * HEADS * DK <= PROJ_TN, "rotated columns: one tile, not the first"


def _rope(x, cos, sin):
    lane = lax.broadcasted_iota(jnp.int32, x.shape, 1)
    partner = jnp.where((lane % 64) < 32, pltpu.roll(x, DK - 32, 1), pltpu.roll(x, 32, 1))
    return x * cos + partner * sin


def _proj_body(x_ref, sh_ref, sc_ref, g_ref, w_ref, wlr_ref, cos_ref, sin_ref, o_ref, lr_ref, h_ref):
    j = pl.program_id(1)

    @pl.when(j == 0)
    def _():
        n = x_ref.shape[0] // PROJ_NORM_SPLIT
        for grp in range(PROJ_NORM_SPLIT):
            rows = pl.ds(grp * n, n)
            x = x_ref[rows, :]
            ms = jnp.mean(x * x, axis=-1, keepdims=True)
            y = x * lax.rsqrt(ms + NORM_EPS) * g_ref[...]
            hb = (y * (1.0 + sc_ref[0]) + sh_ref[0]).astype(BF16)
            h_ref[rows, :] = hb
            lr_ref[rows, :] = _dot_nt(hb, wlr_ref[...])
            o_ref[rows, :] = _dot_nt(hb, w_ref[...]).astype(BF16)

    @pl.when(jnp.logical_and(j != 0, j != ROPE_TILE))
    def _():
        o_ref[...] = _dot_nt(h_ref[...], w_ref[...]).astype(BF16)

    @pl.when(j == ROPE_TILE)
    def _():
        h = h_ref[...]
        cos, sin = cos_ref[...], sin_ref[...]
        for pair in range(HEADS):
            lo = ROPE_COL0 + 2 * DK * pair
            acc = _dot_nt(h, w_ref[lo:lo + 2 * DK, :])
            if pair >= HEADS // 2:
                acc = acc * (DK ** -0.5)
            for half in range(2):
                blk = acc[:, DK * half:DK * (half + 1)]
                o_ref[:, lo + DK * half:lo + DK * (half + 1)] = _rope(blk, cos, sin).astype(BF16)
        rope_end = ROPE_COL0 + 2 * HEADS * DK
        for lo, hi in ((0, ROPE_COL0), (rope_end, PROJ_TN)):
            if lo < hi:
                o_ref[:, lo:hi] = _dot_nt(h, w_ref[lo:hi, :]).astype(BF16)


def _proj(x2, sh, sc, g, w_main, w_lr, cos, sin, tm, rows_per_mod):
    t = x2.shape[0]
    tn = PROJ_TN
    tiles_per_mod = rows_per_mod // tm
    mod_map = lambda i, j: (i // tiles_per_mod, 0, 0)
    pos_tiles = cos.shape[0] // tm
    pos_map = lambda i, j: (i % pos_tiles, 0)
    return pl.pallas_call(
        _proj_body,
        grid=(t // tm, PROJ_W // tn),
        in_specs=[pl.BlockSpec((tm, D_MODEL), lambda i, j: (i, 0)),
                  pl.BlockSpec((1, 1, D_MODEL), mod_map),
                  pl.BlockSpec((1, 1, D_MODEL), mod_map),
                  pl.BlockSpec((1, D_MODEL), lambda i, j: (0, 0)),
                  pl.BlockSpec((tn, D_MODEL), lambda i, j: (j, 0)),
                  pl.BlockSpec((LANES, D_MODEL), lambda i, j: (0, 0)),
                  pl.BlockSpec((tm, DK), pos_map), pl.BlockSpec((tm, DK), pos_map)],
        out_specs=[pl.BlockSpec((tm, tn), lambda i, j: (i, j)),
                   pl.BlockSpec((tm, LANES), lambda i, j: (i, 0))],
        out_shape=[jax.ShapeDtypeStruct((t, PROJ_W), BF16),
                   jax.ShapeDtypeStruct((t, LANES), F32)],
        scratch_shapes=[pltpu.VMEM((tm, D_MODEL), BF16)],
        compiler_params=_cparams(("parallel", "arbitrary")),
        name="proj",
    )(x2, sh, sc, g, w_main, w_lr, cos, sin)


def _gla_consts():
    c, s = GLA_CHUNK, GLA_SUB
    msk = {}
    for rev in (False, True):
        cols = []
        for i in range(c // s):
            keys = np.arange(c)[_key_rows(i, rev)][None, :]
            t = np.arange(c)[:, None]
            visible = (keys > t) if rev else (keys <= t)
            cols.append(np.where(t // s == i, visible, False))
        msk[rev] = np.concatenate(cols, axis=1).astype(np.float32)
    return msk


def _key_rows(i, rev):
    return slice(GLA_SUB * i, GLA_CHUNK) if rev else slice(0, GLA_SUB * (i + 1))


def _log_gate(lr, w, b):
    z = _dot(lr.astype(BF16), w.astype(BF16)) + b
    return -(jnp.maximum(-z, 0.0) + jnp.log(1.0 + jnp.exp(-jnp.abs(z)))) * (1.0 / GLA_TAU)


def _subchunk_scan(la, rev):
    c = la.shape[0]
    pos = lax.broadcasted_iota(jnp.int32, la.shape, 0) % GLA_SUB
    w = la
    step = 1
    while step < GLA_SUB:
        if rev:
            w = w + jnp.where(pos < GLA_SUB - step, pltpu.roll(w, c - step, 0), 0.0)
        else:
            w = w + jnp.where(pos >= step, pltpu.roll(w, step, 0), 0.0)
        step *= 2
    return w


def _gla_intra_exact(q, kf, v, b, rev, scratch):
    q_s, b_s, o_s = scratch
    c = GLA_CHUNK
    q_s[...] = q.astype(F32) * (DK ** -0.5)
    b_s[...] = b
    vf = v.astype(F32)
    row = lax.broadcasted_iota(jnp.int32, (c, 1), 0)

    def body(t, carry):
        qt = q_s[pl.ds(t, 1), :]
        bt = b_s[pl.ds(t, 1), :]
        visible = (row > t) if rev else (row <= t)
        decay = jnp.exp(jnp.where(visible, bt - b, -jnp.inf))
        col = jnp.sum(kf * decay * qt, axis=1, keepdims=True)
        o_s[pl.ds(t, 1), :] = jnp.sum(col * vf, axis=0, keepdims=True)
        return carry

    lax.fori_loop(0, c, body, 0)
    return o_s[...]


def _gla_chunk(q, k, v, la, state_ref, mask, rev, exact_scratch=None):
    c, s = GLA_CHUNK, GLA_SUB
    nsub = c // s
    w = _subchunk_scan(la, rev)
    tot = [w[s * i:s * i + 1] if rev else w[s * (i + 1) - 1:s * (i + 1)] for i in range(nsub)]
    anchors = [None] * nsub
    acc = jnp.zeros_like(tot[0])
    worst = jnp.zeros_like(tot[0])
    for i in (reversed(range(nsub)) if rev else range(nsub)):
        anchors[i] = acc
        acc = acc + tot[i]
        worst = jnp.maximum(worst, -tot[i])
    b_end = acc
    ref = jnp.concatenate([jnp.broadcast_to(a, (s, DK)) for a in anchors], axis=0)
    b = w + ref

    qt = q.astype(F32) * (DK ** -0.5) * jnp.exp(w)
    kf = k.astype(F32)
    if exact_scratch is None:
        kstack = jnp.concatenate([kf[_key_rows(i, rev)] * jnp.exp(anchors[i] - b[_key_rows(i, rev)])
                                  for i in range(nsub)], axis=0).astype(BF16)
        raw = _dot_nt(qt.astype(BF16), kstack)
    kst = kf * jnp.exp(b_end - b)
    xt = jnp.concatenate([kst, jnp.broadcast_to(jnp.exp(b_end), (c, DK))], axis=0).T
    update = _dot(xt[:, :c].astype(BF16), v)
    state = state_ref[...]
    inter = _dot((qt * jnp.exp(ref)).astype(BF16), state.astype(BF16))
    state_ref[...] = xt[:, c:c + 1] * state + update
    yield
    if exact_scratch is None:
        vstack = jnp.concatenate([v[_key_rows(i, rev)] for i in range(nsub)], axis=0)
        intra = _dot(jnp.where(mask != 0.0, raw, 0.0).astype(BF16), vstack)
    else:
        intra = _gla_intra_exact(q, kf, v, b, rev, exact_scratch)
    return inter + intra, worst


def _round_robin(gens):
    results = [None] * len(gens)
    live = list(range(len(gens)))
    while live:
        for g in list(live):
            try:
                next(gens[g])
            except StopIteration as done:
                results[g] = done.value
                live.remove(g)
    return results


def _run_phased(chunks, emit, lag):
    def finish(tag, gen):
        try:
            next(gen)
        except StopIteration as done:
            emit(tag, done.value)
            return
        raise AssertionError("chunk generator has more than two phases")

    pending = []
    for tag, gen in chunks:
        next(gen)
        pending.append((tag, gen))
        if len(pending) > lag:
            finish(*pending.pop(0))
    for item in pending:
        finish(*item)


def _gla_body(qf_ref, kf_ref, vf_ref, lrf_ref, qb_ref, kb_ref, vb_ref, lrb_ref,
              wf_ref, bf_ref, wb_ref, bb_ref, mskf_ref, mskb_ref,
              s0f_ref, s0b_ref, of_ref, ob_ref, sf_ref, sb_ref, rng_ref, *exact_scratch, nchunks):
    @pl.when(pl.program_id(2) == 0)
    def _():
        sf_ref[...] = s0f_ref[...]
        sb_ref[...] = s0b_ref[...]

    c = GLA_CHUNK
    scratch = exact_scratch or None
    la_f = _log_gate(lrf_ref[...], wf_ref[...], bf_ref[...])
    la_b = _log_gate(lrb_ref[...], wb_ref[...], bb_ref[...])
    mskf, mskb = mskf_ref[...], mskb_ref[...]
    st_f, st_b = sf_ref.at[0, 0], sb_ref.at[0, 0]
    worst = [jnp.zeros((1, DK), F32)]

    def chunks():
        for n in range(nchunks):
            rf = slice(c * n, c * (n + 1))
            yield (of_ref, rf), _gla_chunk(qf_ref[rf, :], kf_ref[rf, :], vf_ref[rf, :], la_f[rf, :], st_f, mskf,
                                           False, scratch)
            m = nchunks - 1 - n
            rb = slice(c * m, c * (m + 1))
            yield (ob_ref, rb), _gla_chunk(qb_ref[rb, :], kb_ref[rb, :], vb_ref[rb, :], la_b[rb, :], st_b, mskb,
                                           True, scratch)

    def emit(tag, result):
        o_ref, rows = tag
        o_ref[rows, :] = result[0].astype(BF16)
        worst[0] = jnp.maximum(worst[0], result[1])

    _run_phased(chunks(), emit, GLA_PHASE_LAG)
    rng_ref[0, 0, 0] = jnp.broadcast_to(worst[0], (8, DK))


def _gla(p, lr, wlr_f, blr_f, wlr_b, blr_b, s0f, s0b, batch, seq, tb, exact=False):
    nb = seq // tb
    msk = _gla_consts()
    mskf, mskb = jnp.asarray(msk[False]), jnp.asarray(msk[True])

    def fwd(col0, width):
        return lambda b, h, i: (b * nb + i, col0 // width + h)

    def bwd(col0, width):
        return lambda b, h, i: (b * nb + nb - 1 - i, col0 // width + h)

    const2 = lambda b, h, i: (0, 0)
    headcol = lambda b, h, i: (0, h)
    st_map = lambda b, h, i: (b, h, 0, 0)
    st_spec = pl.BlockSpec((1, 1, DK, DV), st_map)
    st_shape = jax.ShapeDtypeStruct((batch, HEADS, DK, DV), F32)
    o_shape = jax.ShapeDtypeStruct((batch * seq, HEADS * DV), BF16)
    return pl.pallas_call(
        functools.partial(_gla_body, nchunks=tb // GLA_CHUNK),
        grid=(batch, HEADS, nb),
        in_specs=[pl.BlockSpec((tb, DK), fwd(COL_GQ, DK)),
                  pl.BlockSpec((tb, DK), fwd(COL_GK, DK)),
                  pl.BlockSpec((tb, DV), fwd(COL_GV, DV)),
                  pl.BlockSpec((tb, LANES), lambda b, h, i: (b * nb + i, 0)),
                  pl.BlockSpec((tb, DK), bwd(COL_GQ, DK)),
                  pl.BlockSpec((tb, DK), bwd(COL_GK, DK)),
                  pl.BlockSpec((tb, DV), bwd(COL_GV, DV)),
                  pl.BlockSpec((tb, LANES), lambda b, h, i: (b * nb + nb - 1 - i, 0)),
                  pl.BlockSpec((LANES, DK), headcol),
                  pl.BlockSpec((1, DK), headcol),
                  pl.BlockSpec((LANES, DK), headcol),
                  pl.BlockSpec((1, DK), headcol),
                  pl.BlockSpec(mskf.shape, const2),
                  pl.BlockSpec(mskb.shape, const2),
                  st_spec, st_spec],
        out_specs=[pl.BlockSpec((tb, DV), lambda b, h, i: (b * nb + i, h)),
                   pl.BlockSpec((tb, DV), lambda b, h, i: (b * nb + nb - 1 - i, h)),
                   st_spec, st_spec,
                   pl.BlockSpec((1, 1, 1, 8, DK), lambda b, h, i: (b, h, i, 0, 0))],
        out_shape=[o_shape, o_shape, st_shape, st_shape,
                   jax.ShapeDtypeStruct((batch, HEADS, nb, 8, DK), F32)],
        scratch_shapes=([pltpu.VMEM((GLA_CHUNK, DK), F32), pltpu.VMEM((GLA_CHUNK, DK), F32),
                         pltpu.VMEM((GLA_CHUNK, DV), F32)] if exact else []),
        compiler_params=_cparams(("parallel", "parallel", "arbitrary")),
        name="gla_exact" if exact else "gla",
    )(p, p, p, lr, p, p, p, lr, wlr_f, blr_f, wlr_b, blr_b, mskf, mskb, s0f, s0b)


def _ret_consts():
    c = RET_CHUNK
    hh = np.arange(HEADS, dtype=np.float64)
    lg = {False: np.log1p(-np.exp2(-5.0 - hh)), True: np.log1p(-np.exp2(-5.5 - hh))}
    t = np.arange(c, dtype=np.float64)[:, None]
    u = np.arange(c, dtype=np.float64)[None, :]
    pos = np.arange(c, dtype=np.float64)
    out = {}
    for rev in (False, True):
        g = lg[rev][:, None, None]
        if not rev:
            dmat = np.where(u <= t, np.exp((t - u) * g), 0.0)
            qd = np.exp((pos + 1.0)[None, :] * lg[rev][:, None])
            kd = np.exp((c - 1.0 - pos)[None, :] * lg[rev][:, None])
        else:
            dmat = np.where(u > t, np.exp((u - t) * g), 0.0)
            qd = np.exp((c - pos)[None, :] * lg[rev][:, None])
            kd = np.exp(pos[None, :] * lg[rev][:, None])
        cd = np.exp(c * lg[rev])
        out[rev] = (dmat.astype(np.float32),
                    np.repeat(qd[:, :, None], DV, axis=2).astype(np.float32),
                    np.repeat(kd[:, :, None], DK, axis=2).astype(np.float32),
                    np.repeat(cd[:, None, None], DV, axis=2).astype(np.float32))
    return out


def _ret_chunk(q, k, v, state_ref, dmat, qd, kd, cd):
    raw = _dot_nt(q, k)
    update = _dot((k.astype(F32) * kd).T.astype(BF16), v)
    state = state_ref[...]
    inter = _dot(q, state.astype(BF16)) * qd
    state_ref[...] = cd * state + update
    yield
    return inter + _dot((raw * dmat).astype(BF16), v)


def _ret_body(qf_ref, kf_ref, vf_ref, qb_ref, kb_ref, vb_ref,
              dmf_ref, qdf_ref, kdf_ref, cdf_ref, dmb_ref, qdb_ref, kdb_ref, cdb_ref,
              s0f_ref, s0b_ref, of_ref, ob_ref, sf_ref, sb_ref, *, nchunks):
    @pl.when(pl.program_id(2) == 0)
    def _():
        sf_ref[...] = s0f_ref[...]
        sb_ref[...] = s0b_ref[...]

    c = RET_CHUNK
    cf = (dmf_ref[0], qdf_ref[0], kdf_ref[0], cdf_ref[0])
    cb = (dmb_ref[0], qdb_ref[0], kdb_ref[0], cdb_ref[0])
    st_f, st_b = sf_ref.at[0, 0], sb_ref.at[0, 0]
    def chunks():
        for n in range(nchunks):
            rf = slice(c * n, c * (n + 1))
            yield (of_ref, rf), _ret_chunk(qf_ref[rf, :], kf_ref[rf, :], vf_ref[rf, :], st_f, *cf)
            m = nchunks - 1 - n
            rb = slice(c * m, c * (m + 1))
            yield (ob_ref, rb), _ret_chunk(qb_ref[rb, :], kb_ref[rb, :], vb_ref[rb, :], st_b, *cb)

    def emit(tag, out):
        o_ref, rows = tag
        o_ref[rows, :] = out.astype(BF16)

    _run_phased(chunks(), emit, RET_PHASE_LAG)


def _ret(p, s0f, s0b, batch, seq, tb):
    nb = seq // tb
    consts = _ret_consts()

    def fwd(col0, width):
        return lambda b, h, i: (b * nb + i, col0 // width + h)

    def bwd(col0, width):
        return lambda b, h, i: (b * nb + nb - 1 - i, col0 // width + h)

    head3 = lambda b, h, i: (h, 0, 0)
    st_spec = pl.BlockSpec((1, 1, DK, DV), lambda b, h, i: (b, h, 0, 0))
    st_shape = jax.ShapeDtypeStruct((batch, HEADS, DK, DV), F32)
    o_shape = jax.ShapeDtypeStruct((batch * seq, HEADS * DV), BF16)
    const_specs = []
    const_args = []
    for rev in (False, True):
        for a in consts[rev]:
            const_specs.append(pl.BlockSpec((1,) + a.shape[1:], head3))
            const_args.append(jnp.asarray(a))
    return pl.pallas_call(
        functools.partial(_ret_body, nchunks=tb // RET_CHUNK),
        grid=(batch, HEADS, nb),
        in_specs=[pl.BlockSpec((tb, DK), fwd(COL_RQ, DK)),
                  pl.BlockSpec((tb, DK), fwd(COL_RK, DK)),
                  pl.BlockSpec((tb, DV), fwd(COL_RV, DV)),
                  pl.BlockSpec((tb, DK), bwd(COL_RQ, DK)),
                  pl.BlockSpec((tb, DK), bwd(COL_RK, DK)),
                  pl.BlockSpec((tb, DV), bwd(COL_RV, DV))]
                 + const_specs + [st_spec, st_spec],
        out_specs=[pl.BlockSpec((tb, DV), lambda b, h, i: (b * nb + i, h)),
                   pl.BlockSpec((tb, DV), lambda b, h, i: (b * nb + nb - 1 - i, h)),
                   st_spec, st_spec],
        out_shape=[o_shape, o_shape, st_shape, st_shape],
        compiler_params=_cparams(("parallel", "parallel", "arbitrary")),
        name="ret",
    )(p, p, p, p, p, p, *const_args, s0f, s0b)


def _rope_tables(seq):
    n = DK // 4
    inv = np.float32(ROPE_BASE) ** (-np.arange(n, dtype=np.float32) / np.float32(n))
    pos = np.arange(seq)
    ar = (pos // GRID_W).astype(np.float32)[:, None] * inv[None, :]
    ac = (pos % GRID_W).astype(np.float32)[:, None] * inv[None, :]
    cos = np.concatenate([np.cos(ar), np.cos(ar), np.cos(ac), np.cos(ac)], axis=1)
    sin = np.concatenate([-np.sin(ar), np.sin(ar), -np.sin(ac), np.sin(ac)], axis=1)
    return jnp.asarray(cos, F32), jnp.asarray(sin, F32)


def _route(logits_t):
    g = [logits_t[i:i + 1] for i in range(N_GROUPS)]
    gmax = jnp.maximum(jnp.maximum(g[0], g[1]), jnp.maximum(g[2], g[3]))
    gsel = jnp.where(g[0] == gmax, 0, jnp.where(g[1] == gmax, 1, jnp.where(g[2] == gmax, 2, 3)))
    gsum = (jnp.exp(g[0] - gmax) + jnp.exp(g[1] - gmax)) + (jnp.exp(g[2] - gmax) + jnp.exp(g[3] - gmax))
    gw = 1.0 / gsum
    e = [logits_t[EXP_ROW0 + EXPERTS_PER_GROUP * i:EXP_ROW0 + EXPERTS_PER_GROUP * (i + 1)]
         for i in range(N_GROUPS)]
    el = jnp.where(gsel == 0, e[0], jnp.where(gsel == 1, e[1], jnp.where(gsel == 2, e[2], e[3])))
    row = lax.broadcasted_iota(jnp.int32, el.shape, 0).astype(F32)
    none = float(EXPERTS_PER_GROUP)
    m1 = jnp.max(el, axis=0, keepdims=True)
    i1 = jnp.min(jnp.where(el == m1, row, none), axis=0, keepdims=True)
    el2 = jnp.where(row == i1, -jnp.inf, el)
    m2 = jnp.max(el2, axis=0, keepdims=True)
    i2 = jnp.min(jnp.where(el2 == m2, row, none), axis=0, keepdims=True)
    r = jnp.exp(m2 - m1)
    w1 = gw / (1.0 + r)
    w2 = gw * r / (1.0 + r)
    base = gsel * EXPERTS_PER_GROUP
    ids = jnp.concatenate([base + i1.astype(jnp.int32), base + i2.astype(jnp.int32)], axis=0)
    return ids, jnp.concatenate([w1, w2], axis=0)


def _store_token_tiled(ref, val):
    m = val.shape[0]
    for j in range(FEAT_TILES):
        ref[pl.ds(j, m, stride=FEAT_TILES), :] = val[:, LANES * j:LANES * (j + 1)]


def _load_token_tiled(ref, m):
    return jnp.concatenate([ref[pl.ds(j, m, stride=FEAT_TILES), :] for j in range(FEAT_TILES)], axis=1)


def _merge_body(gf_ref, gb_ref, rf_ref, rb_ref, gg_ref, rg_ref, mg_ref, mr_ref, x_ref,
                g1_ref, sh2_ref, nfs_ref, gn_ref, wg_ref, wr_ref, wo_ref, wrt_ref, brt_ref, upper_ref,
                h_ref, hn_ref, ids_ref, ew_ref, cnt_ref, wg_s, wr_s, wo_s, stage, sem):
    @pl.when(pl.program_id(0) == 0)
    def _():
        cnt_ref[...] = jnp.zeros_like(cnt_ref)
        for w_hbm, w_s in ((wg_ref, wg_s), (wr_ref, wr_s), (wo_ref, wo_s)):
            copy = pltpu.make_async_copy(w_hbm, stage, sem.at[0])
            copy.start()
            copy.wait()
            w_s[...] = stage[...].astype(BF16)

    def rows_to_logits(r0, n):
        rows = pl.ds(r0, n)
        og = (gf_ref[rows, :] + gb_ref[rows, :]).astype(F32)
        orr = (rf_ref[rows, :] + rb_ref[rows, :]).astype(F32)
        gparts, rparts = [], []
        for hh in range(HEADS):
            seg = og[:, DV * hh:DV * (hh + 1)]
            ms = jnp.mean(seg * seg, axis=-1, keepdims=True)
            gparts.append(seg * lax.rsqrt(ms + NORM_EPS))
            seg = orr[:, DV * hh:DV * (hh + 1)]
            mu = jnp.mean(seg, axis=-1, keepdims=True)
            cen = seg - mu
            var = jnp.mean(cen * cen, axis=-1, keepdims=True)
            rparts.append(cen * lax.rsqrt(var + NORM_EPS))
        o_gla = (jnp.concatenate(gparts, axis=1) * gn_ref[...]).astype(BF16) * _silu(gg_ref[rows, :])
        o_ret = jnp.concatenate(rparts, axis=1).astype(BF16) * _silu(rg_ref[rows, :])
        a_gla = _dot(o_gla, wg_s[...])
        a_ret = _dot(o_ret, wr_s[...])
        yield
        y = _sigmoid(mg_ref[rows, :]) * a_gla.astype(BF16) + _sigmoid(mr_ref[rows, :]) * a_ret.astype(BF16)
        out = _dot(y, wo_s[...])
        yield
        h = x_ref[rows, :] + g1_ref[0] * out
        h_ref[rows, :] = h
        ms = jnp.mean(h * h, axis=-1, keepdims=True)
        hn = h * lax.rsqrt(ms + NORM_EPS) * nfs_ref[0] + sh2_ref[0]
        _store_token_tiled(hn_ref.at[pl.ds(r0 * FEAT_TILES, n * FEAT_TILES)], hn)
        hn_hi = hn.astype(BF16)
        hn_lo = (hn - hn_hi.astype(F32)).astype(BF16)
        both_w = _dot(hn_hi, wrt_ref[...])
        lo_w = _dot(hn_lo, wrt_ref[:, :ROUTER_W])
        yield
        return (both_w[:, :ROUTER_W] + both_w[:, ROUTER_W:] + lo_w) + brt_ref[...]

    tm = x_ref.shape[0]
    n = tm // MERGE_SPLIT
    logits = jnp.concatenate(_round_robin([rows_to_logits(g * n, n) for g in range(MERGE_SPLIT)]), axis=0)
    ids, ew = _route(logits.T)

    erow = lax.broadcasted_iota(jnp.int32, (N_EXPERTS, tm), 0)
    oh0 = jnp.where(erow == ids[0:1], 1.0, 0.0)
    oh1 = jnp.where(erow == ids[1:2], 1.0, 0.0)
    both = oh0 + oh1
    before = _dot(both.astype(BF16), upper_ref[...]) + cnt_ref[:, 0:1].astype(F32)
    rank0 = jnp.sum(oh0 * before, axis=0, keepdims=True)
    rank1 = jnp.sum(oh1 * before, axis=0, keepdims=True)
    total = cnt_ref[:, 0:1] + jnp.sum(both, axis=1, keepdims=True).astype(jnp.int32)
    cnt_ref[...] = jnp.broadcast_to(total, cnt_ref.shape)
    ids_ref[...] = jnp.concatenate([ids, rank0.astype(jnp.int32), rank1.astype(jnp.int32),
                                    jnp.zeros((4, tm), jnp.int32)], axis=0)
    ew_ref[...] = jnp.concatenate([ew, jnp.zeros((6, tm), F32)], axis=0)


def _merge(gf, gb, rf, rb, p, x2, g1, sh2, nfs, gn, wg, wr, wo, wrt, brt, seq, tm):
    t = x2.shape[0]
    tiles_per_batch = seq // tm
    row = lambda i: (i, 0)
    mod = lambda i: (i // tiles_per_batch, 0, 0)
    const = lambda i: (0, 0)
    tok = pl.BlockSpec((tm, D_MODEL), row)
    vec = pl.BlockSpec((1, D_MODEL), const)
    modspec = pl.BlockSpec((1, 1, D_MODEL), mod)
    wspec = pl.BlockSpec(memory_space=pl.ANY)
    upper = jnp.asarray(np.triu(np.ones((tm, tm), np.float32), 1), BF16)

    def pcol(col0):
        return pl.BlockSpec((tm, D_MODEL), lambda i: (i, col0 // D_MODEL))

    return pl.pallas_call(
        _merge_body,
        grid=(t // tm,),
        in_specs=[tok, tok, tok, tok, pcol(COL_GG), pcol(COL_RG), pcol(COL_MG), pcol(COL_MR), tok,
                  modspec, modspec, modspec, vec, wspec, wspec, wspec,
                  pl.BlockSpec((D_MODEL, 2 * ROUTER_W), const), pl.BlockSpec((1, ROUTER_W), const),
                  pl.BlockSpec((tm, tm), const)],
        out_specs=[tok, pl.BlockSpec((tm * FEAT_TILES, LANES), row),
                   pl.BlockSpec((8, tm), lambda i: (0, i)), pl.BlockSpec((8, tm), lambda i: (0, i)),
                   pl.BlockSpec((N_EXPERTS, LANES), const)],
        out_shape=[jax.ShapeDtypeStruct((t, D_MODEL), F32), jax.ShapeDtypeStruct((t * FEAT_TILES, LANES), F32),
                   jax.ShapeDtypeStruct((8, t), jnp.int32), jax.ShapeDtypeStruct((8, t), F32),
                   jax.ShapeDtypeStruct((N_EXPERTS, LANES), jnp.int32)],
        scratch_shapes=[pltpu.VMEM((D_MODEL, D_MODEL), BF16)] * 3
                       + [pltpu.VMEM((D_MODEL, D_MODEL), F32), pltpu.SemaphoreType.DMA((1,))],
        compiler_params=_cparams(("arbitrary",)),
        name="merge",
    )(gf, gb, rf, rb, p, p, p, p, x2, g1, sh2, nfs, gn, wg, wr, wo, wrt, brt, upper)


GATHER_UNROLL = 8


def _aligned(tok):
    off = tok * FEAT_TILES
    return off if isinstance(off, int) else pl.multiple_of(off, FEAT_TILES)


def _token_copy(src_hbm, dst, sem, src_tok, dst_tok):
    return pltpu.make_async_copy(src_hbm.at[pl.ds(_aligned(src_tok), FEAT_TILES), :],
                                 dst.at[pl.ds(_aligned(dst_tok), FEAT_TILES), :], sem)


def _wait_tokens(src_hbm, dst, sem):
    pltpu.make_async_copy(src_hbm.at[pl.ds(0, dst.shape[0]), :], dst, sem).wait()


GATHER_BUFS = 3
GATHER_AHEAD = GATHER_BUFS - 1


def _issue_rows(n, start_row, unrolled):
    if unrolled:
        for r in range(n):
            start_row(r, r % 2)
    else:
        def body(g, carry):
            for u in range(GATHER_UNROLL):
                start_row(g * GATHER_UNROLL + u, u % 2)
            return carry
        lax.fori_loop(0, n // GATHER_UNROLL, body, 0)


EXPERT_GROUP = 2


EXPERT_W_SLOTS = 2
assert EXPERT_GROUP <= EXPERT_W_SLOTS, "the blocks of one step may belong to EXPERT_GROUP consecutive runs"


def _expert_body(rank_ref, eseq_ref, src0_ref, stok_ref, meta_ref, hn_hbm, wg_hbm, wu_hbm, wd_hbm, y_ref, *scratch):
    bufs, sems = scratch[:GATHER_BUFS], scratch[GATHER_BUFS]
    stage, w_s, wsem = scratch[GATHER_BUFS + 1:GATHER_BUFS + 4], scratch[GATHER_BUFS + 4:GATHER_BUFS + 7], scratch[-1]
    w_hbm = (wg_hbm, wu_hbm, wd_hbm)
    i = pl.program_id(0)
    nused, nranks = meta_ref[0], meta_ref[1]
    nsteps = (nused + EXPERT_GROUP - 1) // EXPERT_GROUP
    last = stok_ref.shape[0] - 1
    rows = MOE_BLOCK * FEAT_TILES

    def issue(step, s, unrolled):
        bases = [src0_ref[step * EXPERT_GROUP + g] for g in range(EXPERT_GROUP)]

        def start(g, r, priority):
            tok = stok_ref[jnp.minimum(bases[g] + r, last)]
            _token_copy(hn_hbm, bufs[s], sems.at[s], tok, g * MOE_BLOCK + r).start(priority=priority)

        if unrolled:
            _issue_rows(EXPERT_GROUP * MOE_BLOCK, lambda j, p: start(j // MOE_BLOCK, j % MOE_BLOCK, p), True)
        else:
            for g in range(EXPERT_GROUP):
                _issue_rows(MOE_BLOCK, lambda r, p, g=g: start(g, r, p), False)

    def weight_copies(rank):
        slot = rank % EXPERT_W_SLOTS
        expert = eseq_ref[rank]
        return [pltpu.make_async_copy(w_hbm[k].at[expert], stage[k].at[slot], wsem.at[slot]) for k in range(3)]

    def activate(rank):
        slot = rank % EXPERT_W_SLOTS
        for copy in weight_copies(rank):
            copy.wait()
        for k in range(3):
            w_s[k][slot] = stage[k][slot].astype(BF16)

        @pl.when(rank + 1 < nranks)
        def _():
            for copy in weight_copies(rank + 1):
                copy.start()

    @pl.when(i == 0)
    def _():
        for b in range(GATHER_AHEAD):
            issue(b, b, False)

        @pl.when(nranks > 0)
        def _():
            for copy in weight_copies(0):
                copy.start()

    for g in range(EXPERT_GROUP):
        blk = i * EXPERT_GROUP + g
        rank = rank_ref[blk]
        first_of_run = jnp.logical_or(blk == 0, rank != rank_ref[jnp.maximum(blk - 1, 0)])

        @pl.when(jnp.logical_and(blk < nused, first_of_run))
        def _():
            activate(rank)

    def block_phases(cur, g):
        part = pl.ds(g * rows, rows)
        slot = rank_ref[i * EXPERT_GROUP + g] % EXPERT_W_SLOTS
        xb = _load_token_tiled(bufs[cur].at[part], MOE_BLOCK).astype(BF16)
        gate = _dot(xb, w_s[0][slot])
        up = _dot(xb, w_s[1][slot])
        yield
        y = _dot((_silu(gate) * up).astype(BF16), w_s[2][slot])
        yield
        _store_token_tiled(y_ref.at[part], y)

    for cur in range(GATHER_BUFS):
        mine = i % GATHER_BUFS == cur

        @pl.when(jnp.logical_and(mine, i < nsteps))
        def _():
            _wait_tokens(hn_hbm, bufs[cur], sems.at[cur])
            issue(i + GATHER_AHEAD, (cur + GATHER_AHEAD) % GATHER_BUFS, True)
            _round_robin([block_phases(cur, g) for g in range(EXPERT_GROUP)])

        @pl.when(jnp.logical_and(mine, jnp.logical_and(i >= nsteps, i < nsteps + GATHER_AHEAD)))
        def _():
            _wait_tokens(hn_hbm, bufs[cur], sems.at[cur])

    @pl.when(i >= nsteps)
    def _():
        y_ref[...] = jnp.zeros_like(y_ref)


def _experts(block_rank, expert_seq, src0, stok, meta, hn, w_gate, w_up, w_down, nblk):
    rows = EXPERT_GROUP * MOE_BLOCK * FEAT_TILES
    w_shapes = ((D_MODEL, EXPERT_FF), (D_MODEL, EXPERT_FF), (EXPERT_FF, D_MODEL))
    any_spec = pl.BlockSpec(memory_space=pl.ANY)
    return pl.pallas_call(
        _expert_body,
        grid_spec=pltpu.PrefetchScalarGridSpec(
            num_scalar_prefetch=5,
            grid=(nblk // EXPERT_GROUP,),
            in_specs=[any_spec, any_spec, any_spec, any_spec],
            out_specs=pl.BlockSpec((rows, LANES), lambda i, *prefetch: (i, 0)),
            scratch_shapes=[pltpu.VMEM((rows, LANES), F32)] * GATHER_BUFS
                           + [pltpu.SemaphoreType.DMA((GATHER_BUFS,))]
                           + [pltpu.VMEM((EXPERT_W_SLOTS,) + shape, F32) for shape in w_shapes]
                           + [pltpu.VMEM((EXPERT_W_SLOTS,) + shape, BF16) for shape in w_shapes]
                           + [pltpu.SemaphoreType.DMA((EXPERT_W_SLOTS,))]),
        out_shape=jax.ShapeDtypeStruct((nblk // EXPERT_GROUP * rows, LANES), F32),
        compiler_params=_cparams(("arbitrary",)),
        name="experts",
    )(block_rank, expert_seq, src0, stok, meta, hn, w_gate, w_up, w_down)


def _combine_body(pos_ref, y_hbm, h_ref, ew_ref, g2_ref, nf_ref, o_ref, *scratch):
    i = pl.program_id(0)
    n = pl.num_programs(0)
    tm = COMBINE_TM
    t = pos_ref.shape[0] // TOP_K
    bufs, sems = scratch[:GATHER_BUFS], scratch[GATHER_BUFS]

    def issue(tile, s, unrolled):
        def start_row(j, priority):
            k, r = j % TOP_K, j // TOP_K
            _token_copy(y_hbm, bufs[s].at[k], sems.at[s], pos_ref[k * t + tile * tm + r], r).start(priority=priority)
        _issue_rows(tm * TOP_K, start_row, unrolled)

    def wait(s):
        for k in range(TOP_K):
            _wait_tokens(y_hbm, bufs[s].at[k], sems.at[s])

    @pl.when(i == 0)
    def _():
        for b in range(GATHER_AHEAD):
            issue(b, b, False)

    for cur in range(GATHER_BUFS):
        @pl.when(i % GATHER_BUFS == cur)
        def _():
            wait(cur)
            issue(jnp.minimum(i + GATHER_AHEAD, n - 1), (cur + GATHER_AHEAD) % GATHER_BUFS, True)
            wt = jnp.concatenate([ew_ref[...]] * (LANES // 8), axis=0).T
            moe = (wt[:, 0:1] * _load_token_tiled(bufs[cur].at[0], tm)
                   + wt[:, 1:2] * _load_token_tiled(bufs[cur].at[1], tm))
            h = h_ref[...] + g2_ref[0] * moe
            ms = jnp.mean(h * h, axis=-1, keepdims=True)
            o_ref[...] = h * lax.rsqrt(ms + NORM_EPS) * nf_ref[...]

        @pl.when(jnp.logical_and(i % GATHER_BUFS == cur, i == n - 1))
        def _():
            for ahead in range(1, GATHER_BUFS):
                wait((cur + ahead) % GATHER_BUFS)


def _combine(pos, y_pad, h, ew, g2, nf, seq):
    t = h.shape[0]
    tm = COMBINE_TM
    tiles_per_batch = seq // tm
    ybuf = pltpu.VMEM((TOP_K, tm * FEAT_TILES, LANES), F32)
    return pl.pallas_call(
        _combine_body,
        grid_spec=pltpu.PrefetchScalarGridSpec(
            num_scalar_prefetch=1,
            grid=(t // tm,),
            in_specs=[pl.BlockSpec(memory_space=pl.ANY),
                      pl.BlockSpec((tm, D_MODEL), lambda i, pos: (i, 0)),
                      pl.BlockSpec((8, tm), lambda i, pos: (0, i)),
                      pl.BlockSpec((1, 1, D_MODEL), lambda i, pos: (i // tiles_per_batch, 0, 0)),
                      pl.BlockSpec((1, D_MODEL), lambda i, pos: (0, 0))],
            out_specs=pl.BlockSpec((tm, D_MODEL), lambda i, pos: (i, 0)),
            scratch_shapes=[ybuf] * GATHER_BUFS + [pltpu.SemaphoreType.DMA((GATHER_BUFS,))]),
        out_shape=jax.ShapeDtypeStruct((t, D_MODEL), F32),
        compiler_params=_cparams(("arbitrary",)),
        name="combine",
    )(pos, y_pad, h, ew, g2, nf)


def _dispatch_indices(ids, ranks, counts, t):
    a = t * TOP_K
    nblk = a // MOE_BLOCK + N_EXPERTS + GATHER_AHEAD * EXPERT_GROUP
    experts = jnp.arange(N_EXPERTS, dtype=jnp.int32)
    starts = jnp.cumsum(counts) - counts
    padded = (counts + MOE_BLOCK - 1) // MOE_BLOCK * MOE_BLOCK
    pends = jnp.cumsum(padded)
    pstarts = pends - padded
    block_start = jnp.arange(nblk, dtype=jnp.int32) * MOE_BLOCK
    block_e = jnp.minimum(jnp.sum((block_start[:, None] >= pends[None, :]).astype(jnp.int32), axis=1),
                          N_EXPERTS - 1)
    of_block = (block_e[:, None] == experts[None, :]).astype(jnp.int32)
    src0 = block_start + jnp.sum(of_block * (starts - pstarts)[None, :], axis=1)
    nused = pends[-1:] // MOE_BLOCK
    blocks = jnp.arange(nblk, dtype=jnp.int32)
    used = blocks < nused
    starts_run = jnp.logical_and(used, jnp.concatenate([jnp.ones((1,), bool), block_e[1:] != block_e[:-1]]))
    nranks = jnp.sum(starts_run.astype(jnp.int32), keepdims=True)
    run = jnp.cumsum(starts_run.astype(jnp.int32)) - 1
    block_rank = jnp.clip(run, 0, jnp.maximum(nranks - 1, 0))
    run_ids = jnp.arange(N_EXPERTS + 1, dtype=jnp.int32)
    expert_seq = jnp.sum(jnp.where(jnp.logical_and(starts_run[:, None], run[:, None] == run_ids[None, :]),
                                   block_e[:, None], 0), axis=0)
    meta = jnp.concatenate([nused, nranks])
    tok = jnp.arange(t, dtype=jnp.int32)[None, :]
    slot = jnp.arange(TOP_K, dtype=jnp.int32)[:, None]
    key = ids * a + tok * TOP_K + slot
    stok = (jnp.sort(key.reshape(a)) % a) // TOP_K
    ids_d = ids.reshape(a // LANES, LANES)
    pstart_of = jnp.zeros_like(ids_d)
    for e in range(N_EXPERTS):
        pstart_of = jnp.where(ids_d == e, pstarts[e], pstart_of)
    pos = (ranks.reshape(a // LANES, LANES) + pstart_of).reshape(a)
    i32 = lambda v: v.astype(jnp.int32)
    return i32(block_rank), i32(expert_seq), i32(src0), i32(stok), i32(meta), i32(pos), nblk


def kernel(x, c, ctx, c_ctx, w_ada, b_ada, norm_mix, norm_ffn, w_in, gla_lr_w, gla_lr_b, gla_norm,
           w_branch_gla, w_branch_ret, w_out, w_router_group, b_router_group, w_router_expert,
           b_router_expert, w_expert_gate, w_expert_up, w_expert_down, norm_final):
    batch, seq, d = x.shape
    ctx_len = ctx.shape[1]
    assert d == D_MODEL and w_ada.shape[0] == 1, "single-layer block with D_MODEL features"
    t = batch * seq

    c8 = jnp.zeros((8, d), F32).at[:batch].set(c).at[batch].set(c_ctx)
    mod = _ada(c8, w_ada[0], b_ada[0][None, :])
    sh1, sc1, g1, sh2, sc2, g2 = [mod[:, d * i:d * (i + 1)] for i in range(6)]
    lat = lambda m: m[:batch, None, :]
    cx = lambda m: m[batch:batch + 1, None, :]

    w_main, w_lr = _wprep(jnp.swapaxes(w_in[0], 0, 1))
    nm = norm_mix[0][None, :]

    ones = jnp.ones((batch * ctx_len, DK), F32)
    p_ctx, lr_ctx = _proj(ctx.reshape(batch * ctx_len, d), cx(sh1), cx(sc1), nm, w_main, w_lr,
                          ones, jnp.zeros_like(ones), tm=batch * ctx_len, rows_per_mod=batch * ctx_len)
    x2 = x.reshape(t, d)
    cos, sin = _rope_tables(seq)
    p_lat, lr_lat = _proj(x2, lat(sh1), lat(sc1), nm, w_main, w_lr, cos, sin, tm=PROJ_TM, rows_per_mod=seq)

    wlr_f = jnp.zeros((LANES, HEADS * DK), F32).at[:GLA_RANK].set(gla_lr_w[0, 0])
    wlr_b = jnp.zeros((LANES, HEADS * DK), F32).at[GLA_RANK:2 * GLA_RANK].set(gla_lr_w[0, 1])
    blr_f, blr_b = gla_lr_b[0, 0][None, :], gla_lr_b[0, 1][None, :]
    zero_state = jnp.zeros((batch, HEADS, DK, DV), F32)

    _, _, gs_f, gs_b, _ = _gla(p_ctx, lr_ctx, wlr_f, blr_f, wlr_b, blr_b, zero_state, zero_state,
                               batch, ctx_len, tb=ctx_len)
    gla_args = (p_lat, lr_lat, wlr_f, blr_f, wlr_b, blr_b, gs_f, gs_b, batch, seq)
    gla_f, gla_b, _, _, decay_range = _gla(*gla_args, tb=SCAN_TB)
    gla_f, gla_b = lax.cond(jnp.max(decay_range) > GLA_SAFE_RANGE,
                            lambda: tuple(_gla(*gla_args, tb=GLA_EXACT_TB, exact=True)[:2]),
                            lambda: (gla_f, gla_b))

    _, _, rs_f, rs_b = _ret(p_ctx, zero_state, zero_state, batch, ctx_len, tb=ctx_len)
    ret_f, ret_b, _, _ = _ret(p_lat, rs_f, rs_b, batch, seq, tb=SCAN_TB)

    wrt = jnp.zeros((d, ROUTER_W), F32)
    wrt = wrt.at[:, :N_GROUPS].set(w_router_group[0]).at[:, EXP_ROW0:EXP_ROW0 + N_EXPERTS].set(w_router_expert[0])
    wrt_hi = wrt.astype(BF16)
    wrt = jnp.concatenate([wrt_hi, (wrt - wrt_hi.astype(F32)).astype(BF16)], axis=1)
    brt = jnp.zeros((1, ROUTER_W), F32)
    brt = brt.at[0, :N_GROUPS].set(b_router_group[0]).at[0, EXP_ROW0:EXP_ROW0 + N_EXPERTS].set(b_router_expert[0])
    h, hn, ids8, ew8, cnt = _merge(gla_f, gla_b, ret_f, ret_b, p_lat, x2, lat(g1), lat(sh2),
                                   norm_ffn[0][None, None, :] * (1.0 + lat(sc2)),
                                   jnp.tile(gla_norm[0], HEADS)[None, :],
                                   w_branch_gla[0], w_branch_ret[0], w_out[0], wrt, brt, seq, tm=MERGE_TM)

    block_rank, expert_seq, src0, stok, meta, pos, nblk = _dispatch_indices(
        ids8[:TOP_K], ids8[TOP_K:2 * TOP_K], cnt[:, 0], t)
    y_pad = _experts(block_rank, expert_seq, src0, stok, meta, hn,
                     w_expert_gate[0], w_expert_up[0], w_expert_down[0], nblk)
    out = _combine(pos, y_pad, h, ew8, lat(g2), norm_final[None, :], seq)
    return out.reshape(batch, seq, d)
```

```python
import functools

import numpy as np
import jax
import jax.numpy as jnp
from jax import lax
from jax.experimental import pallas as pl
from jax.experimental.pallas import tpu as pltpu

F32 = jnp.float32
BF16 = jnp.bfloat16

D_MODEL = 1024
GRID_W = 64
HEADS = 4
DK = 128
DV = 256
GLA_RANK = 16
GLA_TAU = 16.0
GLA_CHUNK = 64
GLA_SUB = 16
RET_CHUNK = 128
ROPE_BASE = 10000.0
N_GROUPS = 4
EXPERTS_PER_GROUP = 8
N_EXPERTS = N_GROUPS * EXPERTS_PER_GROUP
TOP_K = 2
EXPERT_FF = 256
MOE_BLOCK = 128
NORM_EPS = 1e-6

COL_GQ, COL_GK, COL_GV, COL_GG = 0, 512, 1024, 2048
COL_RQ, COL_RK, COL_RV, COL_RG = 3072, 3584, 4096, 5120
COL_MG, COL_MR = 6144, 7168
PROJ_W = 8192
LANES = 128
FEAT_TILES = D_MODEL // LANES
ROUTER_W = 128
EXP_ROW0 = 8

VMEM_LIMIT = 56 * 1024 * 1024

PROJ_TM = 1024
PROJ_TN = 2048
PROJ_NORM_SPLIT = 4
SCAN_TB = 1024
GLA_PHASE_LAG = 2
RET_PHASE_LAG = 2
GLA_EXACT_TB = 256
GLA_SAFE_RANGE = 60.0
MERGE_TM = 512
MERGE_SPLIT = 2
COMBINE_TM = 256


def _cparams(sem):
    return pltpu.CompilerParams(dimension_semantics=sem, vmem_limit_bytes=VMEM_LIMIT)


def _sigmoid(x):
    return 1.0 / (1.0 + jnp.exp(-x))


def _silu(x):
    return x * _sigmoid(x)


def _dot(a, b):
    return jnp.dot(a, b, preferred_element_type=F32)


def _dot_nt(a, b):
    return lax.dot_general(a, b, (((1,), (1,)), ((), ())), preferred_element_type=F32)


def _ada_body(c_ref, w_ref, b_ref, o_ref):
    s = _silu(c_ref[...])
    o_ref[...] = _dot(s.astype(BF16), w_ref[...].astype(BF16)) + b_ref[...]


def _ada(c8, w, b):
    n = w.shape[1]
    tn = 1536
    return pl.pallas_call(
        _ada_body,
        grid=(n // tn,),
        in_specs=[pl.BlockSpec((8, D_MODEL), lambda j: (0, 0)),
                  pl.BlockSpec((D_MODEL, tn), lambda j: (0, j)),
                  pl.BlockSpec((1, tn), lambda j: (0, j))],
        out_specs=pl.BlockSpec((8, tn), lambda j: (0, j)),
        out_shape=jax.ShapeDtypeStruct((8, n), F32),
        compiler_params=_cparams(("arbitrary",)),
        name="ada",
    )(c8, w, b)


LR_COL0 = COL_GG + HEADS * DV
WPREP_ROWS = 1024


def _wprep_body(w_ref, wlr_ref, main_ref, lr_ref):
    main_ref[...] = w_ref[...].astype(BF16)
    pad = jnp.zeros((LANES - 2 * GLA_RANK, D_MODEL), F32)
    lr_ref[...] = jnp.concatenate([wlr_ref[...], pad], axis=0).astype(BF16)


def _wprep(wt):
    steps_before = LR_COL0 // WPREP_ROWS

    def src_row(i):
        row = jnp.where(i < steps_before, i * WPREP_ROWS, i * WPREP_ROWS + 2 * GLA_RANK)
        return (pl.multiple_of(row, 2 * GLA_RANK), 0)

    return pl.pallas_call(
        _wprep_body,
        grid=(PROJ_W // WPREP_ROWS,),
        in_specs=[pl.BlockSpec((pl.Element(WPREP_ROWS), pl.Element(D_MODEL)), src_row),
                  pl.BlockSpec((pl.Element(2 * GLA_RANK), pl.Element(D_MODEL)), lambda i: (LR_COL0, 0))],
        out_specs=[pl.BlockSpec((WPREP_ROWS, D_MODEL), lambda i: (i, 0)),
                   pl.BlockSpec((LANES, D_MODEL), lambda i: (0, 0))],
        out_shape=[jax.ShapeDtypeStruct((PROJ_W, D_MODEL), BF16), jax.ShapeDtypeStruct((LANES, D_MODEL), BF16)],
        compiler_params=_cparams(("arbitrary",)),
        name="wprep",
    )(wt, wt)


ROPE_TILE = COL_RQ // PROJ_TN
ROPE_COL0 = COL_RQ % PROJ_TN
assert ROPE_TILE != 0 and ROPE_COL0 + 2 * HEADS * DK <= PROJ_TN, "rotated columns: one tile, not the first"


def _rope(x, cos, sin):
    lane = lax.broadcasted_iota(jnp.int32, x.shape, 1)
    partner = jnp.where((lane % 64) < 32, pltpu.roll(x, DK - 32, 1), pltpu.roll(x, 32, 1))
    return x * cos + partner * sin


def _proj_body(x_ref, sh_ref, sc_ref, g_ref, w_ref, wlr_ref, cos_ref, sin_ref, o_ref, lr_ref, h_ref):
    j = pl.program_id(1)

    @pl.when(j == 0)
    def _():
        n = x_ref.shape[0] // PROJ_NORM_SPLIT
        for grp in range(PROJ_NORM_SPLIT):
            rows = pl.ds(grp * n, n)
            x = x_ref[rows, :]
            ms = jnp.mean(x * x, axis=-1, keepdims=True)
            y = x * lax.rsqrt(ms + NORM_EPS) * g_ref[...]
            hb = (y * (1.0 + sc_ref[0]) + sh_ref[0]).astype(BF16)
            h_ref[rows, :] = hb
            lr_ref[rows, :] = _dot_nt(hb, wlr_ref[...])
            o_ref[rows, :] = _dot_nt(hb, w_ref[...]).astype(BF16)

    @pl.when(jnp.logical_and(j != 0, j != ROPE_TILE))
    def _():
        o_ref[...] = _dot_nt(h_ref[...], w_ref[...]).astype(BF16)

    @pl.when(j == ROPE_TILE)
    def _():
        h = h_ref[...]
        cos, sin = cos_ref[...], sin_ref[...]
        for pair in range(HEADS):
            lo = ROPE_COL0 + 2 * DK * pair
            acc = _dot_nt(h, w_ref[lo:lo + 2 * DK, :])
            if pair >= HEADS // 2:
                acc = acc * (DK ** -0.5)
            for half in range(2):
                blk = acc[:, DK * half:DK * (half + 1)]
                o_ref[:, lo + DK * half:lo + DK * (half + 1)] = _rope(blk, cos, sin).astype(BF16)
        rope_end = ROPE_COL0 + 2 * HEADS * DK
        for lo, hi in ((0, ROPE_COL0), (rope_end, PROJ_TN)):
            if lo < hi:
                o_ref[:, lo:hi] = _dot_nt(h, w_ref[lo:hi, :]).astype(BF16)


def _proj(x2, sh, sc, g, w_main, w_lr, cos, sin, tm, rows_per_mod):
    t = x2.shape[0]
    tn = PROJ_TN
    tiles_per_mod = rows_per_mod // tm
    mod_map = lambda i, j: (i // tiles_per_mod, 0, 0)
    pos_tiles = cos.shape[0] // tm
    pos_map = lambda i, j: (i % pos_tiles, 0)
    return pl.pallas_call(
        _proj_body,
        grid=(t // tm, PROJ_W // tn),
        in_specs=[pl.BlockSpec((tm, D_MODEL), lambda i, j: (i, 0)),
                  pl.BlockSpec((1, 1, D_MODEL), mod_map),
                  pl.BlockSpec((1, 1, D_MODEL), mod_map),
                  pl.BlockSpec((1, D_MODEL), lambda i, j: (0, 0)),
                  pl.BlockSpec((tn, D_MODEL), lambda i, j: (j, 0)),
                  pl.BlockSpec((LANES, D_MODEL), lambda i, j: (0, 0)),
                  pl.BlockSpec((tm, DK), pos_map), pl.BlockSpec((tm, DK), pos_map)],
        out_specs=[pl.BlockSpec((tm, tn), lambda i, j: (i, j)),
                   pl.BlockSpec((tm, LANES), lambda i, j: (i, 0))],
        out_shape=[jax.ShapeDtypeStruct((t, PROJ_W), BF16),
                   jax.ShapeDtypeStruct((t, LANES), F32)],
        scratch_shapes=[pltpu.VMEM((tm, D_MODEL), BF16)],
        compiler_params=_cparams(("parallel", "arbitrary")),
        name="proj",
    )(x2, sh, sc, g, w_main, w_lr, cos, sin)


def _gla_consts():
    c, s = GLA_CHUNK, GLA_SUB
    msk = {}
    for rev in (False, True):
        cols = []
        for i in range(c // s):
            keys = np.arange(c)[_key_rows(i, rev)][None, :]
            t = np.arange(c)[:, None]
            visible = (keys > t) if rev else (keys <= t)
            cols.append(np.where(t // s == i, visible, False))
        msk[rev] = np.concatenate(cols, axis=1).astype(np.float32)
    return msk


def _key_rows(i, rev):
    return slice(GLA_SUB * i, GLA_CHUNK) if rev else slice(0, GLA_SUB * (i + 1))


def _log_gate(lr, w, b):
    z = _dot(lr.astype(BF16), w.astype(BF16)) + b
    return -(jnp.maximum(-z, 0.0) + jnp.log(1.0 + jnp.exp(-jnp.abs(z)))) * (1.0 / GLA_TAU)


def _subchunk_scan(la, rev):
    c = la.shape[0]
    pos = lax.broadcasted_iota(jnp.int32, la.shape, 0) % GLA_SUB
    w = la
    step = 1
    while step < GLA_SUB:
        if rev:
            w = w + jnp.where(pos < GLA_SUB - step, pltpu.roll(w, c - step, 0), 0.0)
        else:
            w = w + jnp.where(pos >= step, pltpu.roll(w, step, 0), 0.0)
        step *= 2
    return w


def _gla_intra_exact(q, kf, v, b, rev, scratch):
    q_s, b_s, o_s = scratch
    c = GLA_CHUNK
    q_s[...] = q.astype(F32) * (DK ** -0.5)
    b_s[...] = b
    vf = v.astype(F32)
    row = lax.broadcasted_iota(jnp.int32, (c, 1), 0)

    def body(t, carry):
        qt = q_s[pl.ds(t, 1), :]
        bt = b_s[pl.ds(t, 1), :]
        visible = (row > t) if rev else (row <= t)
        decay = jnp.exp(jnp.where(visible, bt - b, -jnp.inf))
        col = jnp.sum(kf * decay * qt, axis=1, keepdims=True)
        o_s[pl.ds(t, 1), :] = jnp.sum(col * vf, axis=0, keepdims=True)
        return carry

    lax.fori_loop(0, c, body, 0)
    return o_s[...]


def _gla_chunk(q, k, v, la, state_ref, mask, rev, exact_scratch=None):
    c, s = GLA_CHUNK, GLA_SUB
    nsub = c // s
    w = _subchunk_scan(la, rev)
    tot = [w[s * i:s * i + 1] if rev else w[s * (i + 1) - 1:s * (i + 1)] for i in range(nsub)]
    anchors = [None] * nsub
    acc = jnp.zeros_like(tot[0])
    worst = jnp.zeros_like(tot[0])
    for i in (reversed(range(nsub)) if rev else range(nsub)):
        anchors[i] = acc
        acc = acc + tot[i]
        worst = jnp.maximum(worst, -tot[i])
    b_end = acc
    ref = jnp.concatenate([jnp.broadcast_to(a, (s, DK)) for a in anchors], axis=0)
    b = w + ref

    qt = q.astype(F32) * (DK ** -0.5) * jnp.exp(w)
    kf = k.astype(F32)
    if exact_scratch is None:
        kstack = jnp.concatenate([kf[_key_rows(i, rev)] * jnp.exp(anchors[i] - b[_key_rows(i, rev)])
                                  for i in range(nsub)], axis=0).astype(BF16)
        raw = _dot_nt(qt.astype(BF16), kstack)
    kst = kf * jnp.exp(b_end - b)
    xt = jnp.concatenate([kst, jnp.broadcast_to(jnp.exp(b_end), (c, DK))], axis=0).T
    update = _dot(xt[:, :c].astype(BF16), v)
    state = state_ref[...]
    inter = _dot((qt * jnp.exp(ref)).astype(BF16), state.astype(BF16))
    state_ref[...] = xt[:, c:c + 1] * state + update
    yield
    if exact_scratch is None:
        vstack = jnp.concatenate([v[_key_rows(i, rev)] for i in range(nsub)], axis=0)
        intra = _dot(jnp.where(mask != 0.0, raw, 0.0).astype(BF16), vstack)
    else:
        intra = _gla_intra_exact(q, kf, v, b, rev, exact_scratch)
    return inter + intra, worst


def _round_robin(gens):
    results = [None] * len(gens)
    live = list(range(len(gens)))
    while live:
        for g in list(live):
            try:
                next(gens[g])
            except StopIteration as done:
                results[g] = done.value
                live.remove(g)
    return results


def _run_phased(chunks, emit, lag):
    def finish(tag, gen):
        try:
            next(gen)
        except StopIteration as done:
            emit(tag, done.value)
            return
        raise AssertionError("chunk generator has more than two phases")

    pending = []
    for tag, gen in chunks:
        next(gen)
        pending.append((tag, gen))
        if len(pending) > lag:
            finish(*pending.pop(0))
    for item in pending:
        finish(*item)


def _gla_body(qf_ref, kf_ref, vf_ref, lrf_ref, qb_ref, kb_ref, vb_ref, lrb_ref,
              wf_ref, bf_ref, wb_ref, bb_ref, mskf_ref, mskb_ref,
              s0f_ref, s0b_ref, of_ref, ob_ref, sf_ref, sb_ref, rng_ref, *exact_scratch, nchunks):
    @pl.when(pl.program_id(2) == 0)
    def _():
        sf_ref[...] = s0f_ref[...]
        sb_ref[...] = s0b_ref[...]

    c = GLA_CHUNK
    scratch = exact_scratch or None
    la_f = _log_gate(lrf_ref[...], wf_ref[...], bf_ref[...])
    la_b = _log_gate(lrb_ref[...], wb_ref[...], bb_ref[...])
    mskf, mskb = mskf_ref[...], mskb_ref[...]
    st_f, st_b = sf_ref.at[0, 0], sb_ref.at[0, 0]
    worst = [jnp.zeros((1, DK), F32)]

    def chunks():
        for n in range(nchunks):
            rf = slice(c * n, c * (n + 1))
            yield (of_ref, rf), _gla_chunk(qf_ref[rf, :], kf_ref[rf, :], vf_ref[rf, :], la_f[rf, :], st_f, mskf,
                                           False, scratch)
            m = nchunks - 1 - n
            rb = slice(c * m, c * (m + 1))
            yield (ob_ref, rb), _gla_chunk(qb_ref[rb, :], kb_ref[rb, :], vb_ref[rb, :], la_b[rb, :], st_b, mskb,
                                           True, scratch)

    def emit(tag, result):
        o_ref, rows = tag
        o_ref[rows, :] = result[0].astype(BF16)
        worst[0] = jnp.maximum(worst[0], result[1])

    _run_phased(chunks(), emit, GLA_PHASE_LAG)
    rng_ref[0, 0, 0] = jnp.broadcast_to(worst[0], (8, DK))


def _gla(p, lr, wlr_f, blr_f, wlr_b, blr_b, s0f, s0b, batch, seq, tb, exact=False):
    nb = seq // tb
    msk = _gla_consts()
    mskf, mskb = jnp.asarray(msk[False]), jnp.asarray(msk[True])

    def fwd(col0, width):
        return lambda b, h, i: (b * nb + i, col0 // width + h)

    def bwd(col0, width):
        return lambda b, h, i: (b * nb + nb - 1 - i, col0 // width + h)

    const2 = lambda b, h, i: (0, 0)
    headcol = lambda b, h, i: (0, h)
    st_map = lambda b, h, i: (b, h, 0, 0)
    st_spec = pl.BlockSpec((1, 1, DK, DV), st_map)
    st_shape = jax.ShapeDtypeStruct((batch, HEADS, DK, DV), F32)
    o_shape = jax.ShapeDtypeStruct((batch * seq, HEADS * DV), BF16)
    return pl.pallas_call(
        functools.partial(_gla_body, nchunks=tb // GLA_CHUNK),
        grid=(batch, HEADS, nb),
        in_specs=[pl.BlockSpec((tb, DK), fwd(COL_GQ, DK)),
                  pl.BlockSpec((tb, DK), fwd(COL_GK, DK)),
                  pl.BlockSpec((tb, DV), fwd(COL_GV, DV)),
                  pl.BlockSpec((tb, LANES), lambda b, h, i: (b * nb + i, 0)),
                  pl.BlockSpec((tb, DK), bwd(COL_GQ, DK)),
                  pl.BlockSpec((tb, DK), bwd(COL_GK, DK)),
                  pl.BlockSpec((tb, DV), bwd(COL_GV, DV)),
                  pl.BlockSpec((tb, LANES), lambda b, h, i: (b * nb + nb - 1 - i, 0)),
                  pl.BlockSpec((LANES, DK), headcol),
                  pl.BlockSpec((1, DK), headcol),
                  pl.BlockSpec((LANES, DK), headcol),
                  pl.BlockSpec((1, DK), headcol),
                  pl.BlockSpec(mskf.shape, const2),
                  pl.BlockSpec(mskb.shape, const2),
                  st_spec, st_spec],
        out_specs=[pl.BlockSpec((tb, DV), lambda b, h, i: (b * nb + i, h)),
                   pl.BlockSpec((tb, DV), lambda b, h, i: (b * nb + nb - 1 - i, h)),
                   st_spec, st_spec,
                   pl.BlockSpec((1, 1, 1, 8, DK), lambda b, h, i: (b, h, i, 0, 0))],
        out_shape=[o_shape, o_shape, st_shape, st_shape,
                   jax.ShapeDtypeStruct((batch, HEADS, nb, 8, DK), F32)],
        scratch_shapes=([pltpu.VMEM((GLA_CHUNK, DK), F32), pltpu.VMEM((GLA_CHUNK, DK), F32),
                         pltpu.VMEM((GLA_CHUNK, DV), F32)] if exact else []),
        compiler_params=_cparams(("parallel", "parallel", "arbitrary")),
        name="gla_exact" if exact else "gla",
    )(p, p, p, lr, p, p, p, lr, wlr_f, blr_f, wlr_b, blr_b, mskf, mskb, s0f, s0b)


def _ret_consts():
    c = RET_CHUNK
    hh = np.arange(HEADS, dtype=np.float64)
    lg = {False: np.log1p(-np.exp2(-5.0 - hh)), True: np.log1p(-np.exp2(-5.5 - hh))}
    t = np.arange(c, dtype=np.float64)[:, None]
    u = np.arange(c, dtype=np.float64)[None, :]
    pos = np.arange(c, dtype=np.float64)
    out = {}
    for rev in (False, True):
        g = lg[rev][:, None, None]
        if not rev:
            dmat = np.where(u <= t, np.exp((t - u) * g), 0.0)
            qd = np.exp((pos + 1.0)[None, :] * lg[rev][:, None])
            kd = np.exp((c - 1.0 - pos)[None, :] * lg[rev][:, None])
        else:
            dmat = np.where(u > t, np.exp((u - t) * g), 0.0)
            qd = np.exp((c - pos)[None, :] * lg[rev][:, None])
            kd = np.exp(pos[None, :] * lg[rev][:, None])
        cd = np.exp(c * lg[rev])
        out[rev] = (dmat.astype(np.float32),
                    np.repeat(qd[:, :, None], DV, axis=2).astype(np.float32),
                    np.repeat(kd[:, :, None], DK, axis=2).astype(np.float32),
                    np.repeat(cd[:, None, None], DV, axis=2).astype(np.float32))
    return out


def _ret_chunk(q, k, v, state_ref, dmat, qd, kd, cd):
    raw = _dot_nt(q, k)
    update = _dot((k.astype(F32) * kd).T.astype(BF16), v)
    state = state_ref[...]
    inter = _dot(q, state.astype(BF16)) * qd
    state_ref[...] = cd * state + update
    yield
    return inter + _dot((raw * dmat).astype(BF16), v)


def _ret_body(qf_ref, kf_ref, vf_ref, qb_ref, kb_ref, vb_ref,
              dmf_ref, qdf_ref, kdf_ref, cdf_ref, dmb_ref, qdb_ref, kdb_ref, cdb_ref,
              s0f_ref, s0b_ref, of_ref, ob_ref, sf_ref, sb_ref, *, nchunks):
    @pl.when(pl.program_id(2) == 0)
    def _():
        sf_ref[...] = s0f_ref[...]
        sb_ref[...] = s0b_ref[...]

    c = RET_CHUNK
    cf = (dmf_ref[0], qdf_ref[0], kdf_ref[0], cdf_ref[0])
    cb = (dmb_ref[0], qdb_ref[0], kdb_ref[0], cdb_ref[0])
    st_f, st_b = sf_ref.at[0, 0], sb_ref.at[0, 0]
    def chunks():
        for n in range(nchunks):
            rf = slice(c * n, c * (n + 1))
            yield (of_ref, rf), _ret_chunk(qf_ref[rf, :], kf_ref[rf, :], vf_ref[rf, :], st_f, *cf)
            m = nchunks - 1 - n
            rb = slice(c * m, c * (m + 1))
            yield (ob_ref, rb), _ret_chunk(qb_ref[rb, :], kb_ref[rb, :], vb_ref[rb, :], st_b, *cb)

    def emit(tag, out):
        o_ref, rows = tag
        o_ref[rows, :] = out.astype(BF16)

    _run_phased(chunks(), emit, RET_PHASE_LAG)


def _ret(p, s0f, s0b, batch, seq, tb):
    nb = seq // tb
    consts = _ret_consts()

    def fwd(col0, width):
        return lambda b, h, i: (b * nb + i, col0 // width + h)

    def bwd(col0, width):
        return lambda b, h, i: (b * nb + nb - 1 - i, col0 // width + h)

    head3 = lambda b, h, i: (h, 0, 0)
    st_spec = pl.BlockSpec((1, 1, DK, DV), lambda b, h, i: (b, h, 0, 0))
    st_shape = jax.ShapeDtypeStruct((batch, HEADS, DK, DV), F32)
    o_shape = jax.ShapeDtypeStruct((batch * seq, HEADS * DV), BF16)
    const_specs = []
    const_args = []
    for rev in (False, True):
        for a in consts[rev]:
            const_specs.append(pl.BlockSpec((1,) + a.shape[1:], head3))
            const_args.append(jnp.asarray(a))
    return pl.pallas_call(
        functools.partial(_ret_body, nchunks=tb // RET_CHUNK),
        grid=(batch, HEADS, nb),
        in_specs=[pl.BlockSpec((tb, DK), fwd(COL_RQ, DK)),
                  pl.BlockSpec((tb, DK), fwd(COL_RK, DK)),
                  pl.BlockSpec((tb, DV), fwd(COL_RV, DV)),
                  pl.BlockSpec((tb, DK), bwd(COL_RQ, DK)),
                  pl.BlockSpec((tb, DK), bwd(COL_RK, DK)),
                  pl.BlockSpec((tb, DV), bwd(COL_RV, DV))]
                 + const_specs + [st_spec, st_spec],
        out_specs=[pl.BlockSpec((tb, DV), lambda b, h, i: (b * nb + i, h)),
                   pl.BlockSpec((tb, DV), lambda b, h, i: (b * nb + nb - 1 - i, h)),
                   st_spec, st_spec],
        out_shape=[o_shape, o_shape, st_shape, st_shape],
        compiler_params=_cparams(("parallel", "parallel", "arbitrary")),
        name="ret",
    )(p, p, p, p, p, p, *const_args, s0f, s0b)


def _rope_tables(seq):
    n = DK // 4
    inv = np.float32(ROPE_BASE) ** (-np.arange(n, dtype=np.float32) / np.float32(n))
    pos = np.arange(seq)
    ar = (pos // GRID_W).astype(np.float32)[:, None] * inv[None, :]
    ac = (pos % GRID_W).astype(np.float32)[:, None] * inv[None, :]
    cos = np.concatenate([np.cos(ar), np.cos(ar), np.cos(ac), np.cos(ac)], axis=1)
    sin = np.concatenate([-np.sin(ar), np.sin(ar), -np.sin(ac), np.sin(ac)], axis=1)
    return jnp.asarray(cos, F32), jnp.asarray(sin, F32)


def _route(logits_t):
    g = [logits_t[i:i + 1] for i in range(N_GROUPS)]
    gmax = jnp.maximum(jnp.maximum(g[0], g[1]), jnp.maximum(g[2], g[3]))
    gsel = jnp.where(g[0] == gmax, 0, jnp.where(g[1] == gmax, 1, jnp.where(g[2] == gmax, 2, 3)))
    gsum = (jnp.exp(g[0] - gmax) + jnp.exp(g[1] - gmax)) + (jnp.exp(g[2] - gmax) + jnp.exp(g[3] - gmax))
    gw = 1.0 / gsum
    e = [logits_t[EXP_ROW0 + EXPERTS_PER_GROUP * i:EXP_ROW0 + EXPERTS_PER_GROUP * (i + 1)]
         for i in range(N_GROUPS)]
    el = jnp.where(gsel == 0, e[0], jnp.where(gsel == 1, e[1], jnp.where(gsel == 2, e[2], e[3])))
    row = lax.broadcasted_iota(jnp.int32, el.shape, 0).astype(F32)
    none = float(EXPERTS_PER_GROUP)
    m1 = jnp.max(el, axis=0, keepdims=True)
    i1 = jnp.min(jnp.where(el == m1, row, none), axis=0, keepdims=True)
    el2 = jnp.where(row == i1, -jnp.inf, el)
    m2 = jnp.max(el2, axis=0, keepdims=True)
    i2 = jnp.min(jnp.where(el2 == m2, row, none), axis=0, keepdims=True)
    r = jnp.exp(m2 - m1)
    w1 = gw / (1.0 + r)
    w2 = gw * r / (1.0 + r)
    base = gsel * EXPERTS_PER_GROUP
    ids = jnp.concatenate([base + i1.astype(jnp.int32), base + i2.astype(jnp.int32)], axis=0)
    return ids, jnp.concatenate([w1, w2], axis=0)


def _store_token_tiled(ref, val):
    m = val.shape[0]
    for j in range(FEAT_TILES):
        ref[pl.ds(j, m, stride=FEAT_TILES), :] = val[:, LANES * j:LANES * (j + 1)]


def _load_token_tiled(ref, m):
    return jnp.concatenate([ref[pl.ds(j, m, stride=FEAT_TILES), :] for j in range(FEAT_TILES)], axis=1)


def _merge_body(gf_ref, gb_ref, rf_ref, rb_ref, gg_ref, rg_ref, mg_ref, mr_ref, x_ref,
                g1_ref, sh2_ref, nfs_ref, gn_ref, wg_ref, wr_ref, wo_ref, wrt_ref, brt_ref, upper_ref,
                h_ref, hn_ref, ids_ref, ew_ref, cnt_ref, wg_s, wr_s, wo_s, stage, sem):
    @pl.when(pl.program_id(0) == 0)
    def _():
        cnt_ref[...] = jnp.zeros_like(cnt_ref)
        for w_hbm, w_s in ((wg_ref, wg_s), (wr_ref, wr_s), (wo_ref, wo_s)):
            copy = pltpu.make_async_copy(w_hbm, stage, sem.at[0])
            copy.start()
            copy.wait()
            w_s[...] = stage[...].astype(BF16)

    def rows_to_logits(r0, n):
        rows = pl.ds(r0, n)
        og = (gf_ref[rows, :] + gb_ref[rows, :]).astype(F32)
        orr = (rf_ref[rows, :] + rb_ref[rows, :]).astype(F32)
        gparts, rparts = [], []
        for hh in range(HEADS):
            seg = og[:, DV * hh:DV * (hh + 1)]
            ms = jnp.mean(seg * seg, axis=-1, keepdims=True)
            gparts.append(seg * lax.rsqrt(ms + NORM_EPS))
            seg = orr[:, DV * hh:DV * (hh + 1)]
            mu = jnp.mean(seg, axis=-1, keepdims=True)
            cen = seg - mu
            var = jnp.mean(cen * cen, axis=-1, keepdims=True)
            rparts.append(cen * lax.rsqrt(var + NORM_EPS))
        o_gla = (jnp.concatenate(gparts, axis=1) * gn_ref[...]).astype(BF16) * _silu(gg_ref[rows, :])
        o_ret = jnp.concatenate(rparts, axis=1).astype(BF16) * _silu(rg_ref[rows, :])
        a_gla = _dot(o_gla, wg_s[...])
        a_ret = _dot(o_ret, wr_s[...])
        yield
        y = _sigmoid(mg_ref[rows, :]) * a_gla.astype(BF16) + _sigmoid(mr_ref[rows, :]) * a_ret.astype(BF16)
        out = _dot(y, wo_s[...])
        yield
        h = x_ref[rows, :] + g1_ref[0] * out
        h_ref[rows, :] = h
        ms = jnp.mean(h * h, axis=-1, keepdims=True)
        hn = h * lax.rsqrt(ms + NORM_EPS) * nfs_ref[0] + sh2_ref[0]
        _store_token_tiled(hn_ref.at[pl.ds(r0 * FEAT_TILES, n * FEAT_TILES)], hn)
        hn_hi = hn.astype(BF16)
        hn_lo = (hn - hn_hi.astype(F32)).astype(BF16)
        both_w = _dot(hn_hi, wrt_ref[...])
        lo_w = _dot(hn_lo, wrt_ref[:, :ROUTER_W])
        yield
        return (both_w[:, :ROUTER_W] + both_w[:, ROUTER_W:] + lo_w) + brt_ref[...]

    tm = x_ref.shape[0]
    n = tm // MERGE_SPLIT
    logits = jnp.concatenate(_round_robin([rows_to_logits(g * n, n) for g in range(MERGE_SPLIT)]), axis=0)
    ids, ew = _route(logits.T)

    erow = lax.broadcasted_iota(jnp.int32, (N_EXPERTS, tm), 0)
    oh0 = jnp.where(erow == ids[0:1], 1.0, 0.0)
    oh1 = jnp.where(erow == ids[1:2], 1.0, 0.0)
    both = oh0 + oh1
    before = _dot(both.astype(BF16), upper_ref[...]) + cnt_ref[:, 0:1].astype(F32)
    rank0 = jnp.sum(oh0 * before, axis=0, keepdims=True)
    rank1 = jnp.sum(oh1 * before, axis=0, keepdims=True)
    total = cnt_ref[:, 0:1] + jnp.sum(both, axis=1, keepdims=True).astype(jnp.int32)
    cnt_ref[...] = jnp.broadcast_to(total, cnt_ref.shape)
    ids_ref[...] = jnp.concatenate([ids, rank0.astype(jnp.int32), rank1.astype(jnp.int32),
                                    jnp.zeros((4, tm), jnp.int32)], axis=0)
    ew_ref[...] = jnp.concatenate([ew, jnp.zeros((6, tm), F32)], axis=0)


def _merge(gf, gb, rf, rb, p, x2, g1, sh2, nfs, gn, wg, wr, wo, wrt, brt, seq, tm):
    t = x2.shape[0]
    tiles_per_batch = seq // tm
    row = lambda i: (i, 0)
    mod = lambda i: (i // tiles_per_batch, 0, 0)
    const = lambda i: (0, 0)
    tok = pl.BlockSpec((tm, D_MODEL), row)
    vec = pl.BlockSpec((1, D_MODEL), const)
    modspec = pl.BlockSpec((1, 1, D_MODEL), mod)
    wspec = pl.BlockSpec(memory_space=pl.ANY)
    upper = jnp.asarray(np.triu(np.ones((tm, tm), np.float32), 1), BF16)

    def pcol(col0):
        return pl.BlockSpec((tm, D_MODEL), lambda i: (i, col0 // D_MODEL))

    return pl.pallas_call(
        _merge_body,
        grid=(t // tm,),
        in_specs=[tok, tok, tok, tok, pcol(COL_GG), pcol(COL_RG), pcol(COL_MG), pcol(COL_MR), tok,
                  modspec, modspec, modspec, vec, wspec, wspec, wspec,
                  pl.BlockSpec((D_MODEL, 2 * ROUTER_W), const), pl.BlockSpec((1, ROUTER_W), const),
                  pl.BlockSpec((tm, tm), const)],
        out_specs=[tok, pl.BlockSpec((tm * FEAT_TILES, LANES), row),
                   pl.BlockSpec((8, tm), lambda i: (0, i)), pl.BlockSpec((8, tm), lambda i: (0, i)),
                   pl.BlockSpec((N_EXPERTS, LANES), const)],
        out_shape=[jax.ShapeDtypeStruct((t, D_MODEL), F32), jax.ShapeDtypeStruct((t * FEAT_TILES, LANES), F32),
                   jax.ShapeDtypeStruct((8, t), jnp.int32), jax.ShapeDtypeStruct((8, t), F32),
                   jax.ShapeDtypeStruct((N_EXPERTS, LANES), jnp.int32)],
        scratch_shapes=[pltpu.VMEM((D_MODEL, D_MODEL), BF16)] * 3
                       + [pltpu.VMEM((D_MODEL, D_MODEL), F32), pltpu.SemaphoreType.DMA((1,))],
        compiler_params=_cparams(("arbitrary",)),
        name="merge",
    )(gf, gb, rf, rb, p, p, p, p, x2, g1, sh2, nfs, gn, wg, wr, wo, wrt, brt, upper)


GATHER_UNROLL = 8


def _aligned(tok):
    off = tok * FEAT_TILES
    return off if isinstance(off, int) else pl.multiple_of(off, FEAT_TILES)


def _token_copy(src_hbm, dst, sem, src_tok, dst_tok):
    return pltpu.make_async_copy(src_hbm.at[pl.ds(_aligned(src_tok), FEAT_TILES), :],
                                 dst.at[pl.ds(_aligned(dst_tok), FEAT_TILES), :], sem)


def _wait_tokens(src_hbm, dst, sem):
    pltpu.make_async_copy(src_hbm.at[pl.ds(0, dst.shape[0]), :], dst, sem).wait()


GATHER_BUFS = 4
GATHER_AHEAD = GATHER_BUFS - 1


def _issue_rows(n, start_row, unrolled):
    if unrolled:
        for r in range(n):
            start_row(r, r % 2)
    else:
        def body(g, carry):
            for u in range(GATHER_UNROLL):
                start_row(g * GATHER_UNROLL + u, u % 2)
            return carry
        lax.fori_loop(0, n // GATHER_UNROLL, body, 0)


EXPERT_GROUP = 2


EXPERT_W_SLOTS = 2
assert EXPERT_GROUP <= EXPERT_W_SLOTS, "the blocks of one step may belong to EXPERT_GROUP consecutive runs"


def _expert_body(rank_ref, eseq_ref, src0_ref, stok_ref, meta_ref, hn_hbm, wg_hbm, wu_hbm, wd_hbm, y_ref, *scratch):
    bufs, sems = scratch[:GATHER_BUFS], scratch[GATHER_BUFS]
    stage, w_s, wsem = scratch[GATHER_BUFS + 1:GATHER_BUFS + 4], scratch[GATHER_BUFS + 4:GATHER_BUFS + 7], scratch[-1]
    w_hbm = (wg_hbm, wu_hbm, wd_hbm)
    i = pl.program_id(0)
    nused, nranks = meta_ref[0], meta_ref[1]
    nsteps = (nused + EXPERT_GROUP - 1) // EXPERT_GROUP
    last = stok_ref.shape[0] - 1
    rows = MOE_BLOCK * FEAT_TILES

    def issue(step, s, unrolled):
        bases = [src0_ref[step * EXPERT_GROUP + g] for g in range(EXPERT_GROUP)]

        def start(g, r, priority):
            tok = stok_ref[jnp.minimum(bases[g] + r, last)]
            _token_copy(hn_hbm, bufs[s], sems.at[s], tok, g * MOE_BLOCK + r).start(priority=priority)

        if unrolled:
            _issue_rows(EXPERT_GROUP * MOE_BLOCK, lambda j, p: start(j // MOE_BLOCK, j % MOE_BLOCK, p), True)
        else:
            for g in range(EXPERT_GROUP):
                _issue_rows(MOE_BLOCK, lambda r, p, g=g: start(g, r, p), False)

    def weight_copies(rank):
        slot = rank % EXPERT_W_SLOTS
        expert = eseq_ref[rank]
        return [pltpu.make_async_copy(w_hbm[k].at[expert], stage[k].at[slot], wsem.at[slot]) for k in range(3)]

    def activate(rank):
        slot = rank % EXPERT_W_SLOTS
        for copy in weight_copies(rank):
            copy.wait()
        for k in range(3):
            w_s[k][slot] = stage[k][slot].astype(BF16)

        @pl.when(rank + 1 < nranks)
        def _():
            for copy in weight_copies(rank + 1):
                copy.start()

    @pl.when(i == 0)
    def _():
        for b in range(GATHER_AHEAD):
            issue(b, b, False)

        @pl.when(nranks > 0)
        def _():
            for copy in weight_copies(0):
                copy.start()

    for g in range(EXPERT_GROUP):
        blk = i * EXPERT_GROUP + g
        rank = rank_ref[blk]
        first_of_run = jnp.logical_or(blk == 0, rank != rank_ref[jnp.maximum(blk - 1, 0)])

        @pl.when(jnp.logical_and(blk < nused, first_of_run))
        def _():
            activate(rank)

    def block_phases(cur, g):
        part = pl.ds(g * rows, rows)
        slot = rank_ref[i * EXPERT_GROUP + g] % EXPERT_W_SLOTS
        xb = _load_token_tiled(bufs[cur].at[part], MOE_BLOCK).astype(BF16)
        gate = _dot(xb, w_s[0][slot])
        up = _dot(xb, w_s[1][slot])
        yield
        y = _dot((_silu(gate) * up).astype(BF16), w_s[2][slot])
        yield
        _store_token_tiled(y_ref.at[part], y)

    for cur in range(GATHER_BUFS):
        mine = i % GATHER_BUFS == cur

        @pl.when(jnp.logical_and(mine, i < nsteps))
        def _():
            _wait_tokens(hn_hbm, bufs[cur], sems.at[cur])
            issue(i + GATHER_AHEAD, (cur + GATHER_AHEAD) % GATHER_BUFS, True)
            _round_robin([block_phases(cur, g) for g in range(EXPERT_GROUP)])

        @pl.when(jnp.logical_and(mine, jnp.logical_and(i >= nsteps, i < nsteps + GATHER_AHEAD)))
        def _():
            _wait_tokens(hn_hbm, bufs[cur], sems.at[cur])

    @pl.when(i >= nsteps)
    def _():
        y_ref[...] = jnp.zeros_like(y_ref)


def _experts(block_rank, expert_seq, src0, stok, meta, hn, w_gate, w_up, w_down, nblk):
    rows = EXPERT_GROUP * MOE_BLOCK * FEAT_TILES
    w_shapes = ((D_MODEL, EXPERT_FF), (D_MODEL, EXPERT_FF), (EXPERT_FF, D_MODEL))
    any_spec = pl.BlockSpec(memory_space=pl.ANY)
    return pl.pallas_call(
        _expert_body,
        grid_spec=pltpu.PrefetchScalarGridSpec(
            num_scalar_prefetch=5,
            grid=(nblk // EXPERT_GROUP,),
            in_specs=[any_spec, any_spec, any_spec, any_spec],
            out_specs=pl.BlockSpec((rows, LANES), lambda i, *prefetch: (i, 0)),
            scratch_shapes=[pltpu.VMEM((rows, LANES), F32)] * GATHER_BUFS
                           + [pltpu.SemaphoreType.DMA((GATHER_BUFS,))]
                           + [pltpu.VMEM((EXPERT_W_SLOTS,) + shape, F32) for shape in w_shapes]
                           + [pltpu.VMEM((EXPERT_W_SLOTS,) + shape, BF16) for shape in w_shapes]
                           + [pltpu.SemaphoreType.DMA((EXPERT_W_SLOTS,))]),
        out_shape=jax.ShapeDtypeStruct((nblk // EXPERT_GROUP * rows, LANES), F32),
        compiler_params=_cparams(("arbitrary",)),
        name="experts",
    )(block_rank, expert_seq, src0, stok, meta, hn, w_gate, w_up, w_down)


def _combine_body(pos_ref, y_hbm, h_ref, ew_ref, g2_ref, nf_ref, o_ref, *scratch):
    i = pl.program_id(0)
    n = pl.num_programs(0)
    tm = COMBINE_TM
    t = pos_ref.shape[0] // TOP_K
    bufs, sems = scratch[:GATHER_BUFS], scratch[GATHER_BUFS]

    def issue(tile, s, unrolled):
        def start_row(j, priority):
            k, r = j % TOP_K, j // TOP_K
            _token_copy(y_hbm, bufs[s].at[k], sems.at[s], pos_ref[k * t + tile * tm + r], r).start(priority=priority)
        _issue_rows(tm * TOP_K, start_row, unrolled)

    def wait(s):
        for k in range(TOP_K):
            _wait_tokens(y_hbm, bufs[s].at[k], sems.at[s])

    @pl.when(i == 0)
    def _():
        for b in range(GATHER_AHEAD):
            issue(b, b, False)

    for cur in range(GATHER_BUFS):
        @pl.when(i % GATHER_BUFS == cur)
        def _():
            wait(cur)
            issue(jnp.minimum(i + GATHER_AHEAD, n - 1), (cur + GATHER_AHEAD) % GATHER_BUFS, True)
            wt = jnp.concatenate([ew_ref[...]] * (LANES // 8), axis=0).T
            moe = (wt[:, 0:1] * _load_token_tiled(bufs[cur].at[0], tm)
                   + wt[:, 1:2] * _load_token_tiled(bufs[cur].at[1], tm))
            h = h_ref[...] + g2_ref[0] * moe
            ms = jnp.mean(h * h, axis=-1, keepdims=True)
            o_ref[...] = h * lax.rsqrt(ms + NORM_EPS) * nf_ref[...]

        @pl.when(jnp.logical_and(i % GATHER_BUFS == cur, i == n - 1))
        def _():
            for ahead in range(1, GATHER_BUFS):
                wait((cur + ahead) % GATHER_BUFS)


def _combine(pos, y_pad, h, ew, g2, nf, seq):
    t = h.shape[0]
    tm = COMBINE_TM
    tiles_per_batch = seq // tm
    ybuf = pltpu.VMEM((TOP_K, tm * FEAT_TILES, LANES), F32)
    return pl.pallas_call(
        _combine_body,
        grid_spec=pltpu.PrefetchScalarGridSpec(
            num_scalar_prefetch=1,
            grid=(t // tm,),
            in_specs=[pl.BlockSpec(memory_space=pl.ANY),
                      pl.BlockSpec((tm, D_MODEL), lambda i, pos: (i, 0)),
                      pl.BlockSpec((8, tm), lambda i, pos: (0, i)),
                      pl.BlockSpec((1, 1, D_MODEL), lambda i, pos: (i // tiles_per_batch, 0, 0)),
                      pl.BlockSpec((1, D_MODEL), lambda i, pos: (0, 0))],
            out_specs=pl.BlockSpec((tm, D_MODEL), lambda i, pos: (i, 0)),
            scratch_shapes=[ybuf] * GATHER_BUFS + [pltpu.SemaphoreType.DMA((GATHER_BUFS,))]),
        out_shape=jax.ShapeDtypeStruct((t, D_MODEL), F32),
        compiler_params=_cparams(("arbitrary",)),
        name="combine",
    )(pos, y_pad, h, ew, g2, nf)


def _dispatch_indices(ids, ranks, counts, t):
    a = t * TOP_K
    nblk = a // MOE_BLOCK + N_EXPERTS + GATHER_AHEAD * EXPERT_GROUP
    experts = jnp.arange(N_EXPERTS, dtype=jnp.int32)
    starts = jnp.cumsum(counts) - counts
    padded = (counts + MOE_BLOCK - 1) // MOE_BLOCK * MOE_BLOCK
    pends = jnp.cumsum(padded)
    pstarts = pends - padded
    block_start = jnp.arange(nblk, dtype=jnp.int32) * MOE_BLOCK
    block_e = jnp.minimum(jnp.sum((block_start[:, None] >= pends[None, :]).astype(jnp.int32), axis=1),
                          N_EXPERTS - 1)
    of_block = (block_e[:, None] == experts[None, :]).astype(jnp.int32)
    src0 = block_start + jnp.sum(of_block * (starts - pstarts)[None, :], axis=1)
    nused = pends[-1:] // MOE_BLOCK
    blocks = jnp.arange(nblk, dtype=jnp.int32)
    used = blocks < nused
    starts_run = jnp.logical_and(used, jnp.concatenate([jnp.ones((1,), bool), block_e[1:] != block_e[:-1]]))
    nranks = jnp.sum(starts_run.astype(jnp.int32), keepdims=True)
    run = jnp.cumsum(starts_run.astype(jnp.int32)) - 1
    block_rank = jnp.clip(run, 0, jnp.maximum(nranks - 1, 0))
    run_ids = jnp.arange(N_EXPERTS + 1, dtype=jnp.int32)
    expert_seq = jnp.sum(jnp.where(jnp.logical_and(starts_run[:, None], run[:, None] == run_ids[None, :]),
                                   block_e[:, None], 0), axis=0)
    meta = jnp.concatenate([nused, nranks])
    tok = jnp.arange(t, dtype=jnp.int32)[None, :]
    slot = jnp.arange(TOP_K, dtype=jnp.int32)[:, None]
    key = ids * a + tok * TOP_K + slot
    stok = (jnp.sort(key.reshape(a)) % a) // TOP_K
    ids_d = ids.reshape(a // LANES, LANES)
    pstart_of = jnp.zeros_like(ids_d)
    for e in range(N_EXPERTS):
        pstart_of = jnp.where(ids_d == e, pstarts[e], pstart_of)
    pos = (ranks.reshape(a // LANES, LANES) + pstart_of).reshape(a)
    i32 = lambda v: v.astype(jnp.int32)
    return i32(block_rank), i32(expert_seq), i32(src0), i32(stok), i32(meta), i32(pos), nblk


def kernel(x, c, ctx, c_ctx, w_ada, b_ada, norm_mix, norm_ffn, w_in, gla_lr_w, gla_lr_b, gla_norm,
           w_branch_gla, w_branch_ret, w_out, w_router_group, b_router_group, w_router_expert,
           b_router_expert, w_expert_gate, w_expert_up, w_expert_down, norm_final):
    batch, seq, d = x.shape
    ctx_len = ctx.shape[1]
    assert d == D_MODEL and w_ada.shape[0] == 1, "single-layer block with D_MODEL features"
    t = batch * seq

    c8 = jnp.zeros((8, d), F32).at[:batch].set(c).at[batch].set(c_ctx)
    mod = _ada(c8, w_ada[0], b_ada[0][None, :])
    sh1, sc1, g1, sh2, sc2, g2 = [mod[:, d * i:d * (i + 1)] for i in range(6)]
    lat = lambda m: m[:batch, None, :]
    cx = lambda m: m[batch:batch + 1, None, :]

    w_main, w_lr = _wprep(jnp.swapaxes(w_in[0], 0, 1))
    nm = norm_mix[0][None, :]

    ones = jnp.ones((batch * ctx_len, DK), F32)
    p_ctx, lr_ctx = _proj(ctx.reshape(batch * ctx_len, d), cx(sh1), cx(sc1), nm, w_main, w_lr,
                          ones, jnp.zeros_like(ones), tm=batch * ctx_len, rows_per_mod=batch * ctx_len)
    x2 = x.reshape(t, d)
    cos, sin = _rope_tables(seq)
    p_lat, lr_lat = _proj(x2, lat(sh1), lat(sc1), nm, w_main, w_lr, cos, sin, tm=PROJ_TM, rows_per_mod=seq)

    wlr_f = jnp.zeros((LANES, HEADS * DK), F32).at[:GLA_RANK].set(gla_lr_w[0, 0])
    wlr_b = jnp.zeros((LANES, HEADS * DK), F32).at[GLA_RANK:2 * GLA_RANK].set(gla_lr_w[0, 1])
    blr_f, blr_b = gla_lr_b[0, 0][None, :], gla_lr_b[0, 1][None, :]
    zero_state = jnp.zeros((batch, HEADS, DK, DV), F32)

    _, _, gs_f, gs_b, _ = _gla(p_ctx, lr_ctx, wlr_f, blr_f, wlr_b, blr_b, zero_state, zero_state,
                               batch, ctx_len, tb=ctx_len)
    gla_args = (p_lat, lr_lat, wlr_f, blr_f, wlr_b, blr_b, gs_f, gs_b, batch, seq)
    gla_f, gla_b, _, _, decay_range = _gla(*gla_args, tb=SCAN_TB)
    gla_f, gla_b = lax.cond(jnp.max(decay_range) > GLA_SAFE_RANGE,
                            lambda: tuple(_gla(*gla_args, tb=GLA_EXACT_TB, exact=True)[:2]),
                            lambda: (gla_f, gla_b))

    _, _, rs_f, rs_b = _ret(p_ctx, zero_state, zero_state, batch, ctx_len, tb=ctx_len)
    ret_f, ret_b, _, _ = _ret(p_lat, rs_f, rs_b, batch, seq, tb=SCAN_TB)

    wrt = jnp.zeros((d, ROUTER_W), F32)
    wrt = wrt.at[:, :N_GROUPS].set(w_router_group[0]).at[:, EXP_ROW0:EXP_ROW0 + N_EXPERTS].set(w_router_expert[0])
    wrt_hi = wrt.astype(BF16)
    wrt = jnp.concatenate([wrt_hi, (wrt - wrt_hi.astype(F32)).astype(BF16)], axis=1)
    brt = jnp.zeros((1, ROUTER_W), F32)
    brt = brt.at[0, :N_GROUPS].set(b_router_group[0]).at[0, EXP_ROW0:EXP_ROW0 + N_EXPERTS].set(b_router_expert[0])
    h, hn, ids8, ew8, cnt = _merge(gla_f, gla_b, ret_f, ret_b, p_lat, x2, lat(g1), lat(sh2),
                                   norm_ffn[0][None, None, :] * (1.0 + lat(sc2)),
                                   jnp.tile(gla_norm[0], HEADS)[None, :],
                                   w_branch_gla[0], w_branch_ret[0], w_out[0], wrt, brt, seq, tm=MERGE_TM)

    block_rank, expert_seq, src0, stok, meta, pos, nblk = _dispatch_indices(
        ids8[:TOP_K], ids8[TOP_K:2 * TOP_K], cnt[:, 0], t)
    y_pad = _experts(block_rank, expert_seq, src0, stok, meta, hn,
                     w_expert_gate[0], w_expert_up[0], w_expert_down[0], nblk)
    out = _combine(pos, y_pad, h, ew8, lat(g2), norm_final[None, :], seq)
    return out.reshape(batch, seq, d)
```

```python
import functools

import numpy as np
import jax
import jax.numpy as jnp
from jax import lax
from jax.experimental import pallas as pl
from jax.experimental.pallas import tpu as pltpu

F32 = jnp.float32
BF16 = jnp.bfloat16

D_MODEL = 1024
GRID_W = 64
HEADS = 4
DK = 128
DV = 256
GLA_RANK = 16
GLA_TAU = 16.0
GLA_CHUNK = 64
GLA_SUB = 16
RET_CHUNK = 128
ROPE_BASE = 10000.0
N_GROUPS = 4
EXPERTS_PER_GROUP = 8
N_EXPERTS = N_GROUPS * EXPERTS_PER_GROUP
TOP_K = 2
EXPERT_FF = 256
MOE_BLOCK = 128
NORM_EPS = 1e-6

COL_GQ, COL_GK, COL_GV, COL_GG = 0, 512, 1024, 2048
COL_RQ, COL_RK, COL_RV, COL_RG = 3072, 3584, 4096, 5120
COL_MG, COL_MR = 6144, 7168
PROJ_W = 8192
LANES = 128
FEAT_TILES = D_MODEL // LANES
ROUTER_W = 128
EXP_ROW0 = 8

VMEM_LIMIT = 56 * 1024 * 1024

PROJ_TM = 1024
PROJ_TN = 2048
PROJ_NORM_SPLIT = 4
SCAN_TB = 2048
GLA_PHASE_LAG = 2
RET_PHASE_LAG = 2
GLA_EXACT_TB = 256
GLA_SAFE_RANGE = 60.0
MERGE_TM = 512
MERGE_SPLIT = 2
COMBINE_TM = 256


def _cparams(sem):
    return pltpu.CompilerParams(dimension_semantics=sem, vmem_limit_bytes=VMEM_LIMIT)


def _sigmoid(x):
    return 1.0 / (1.0 + jnp.exp(-x))


def _silu(x):
    return x * _sigmoid(x)


def _dot(a, b):
    return jnp.dot(a, b, preferred_element_type=F32)


def _dot_nt(a, b):
    return lax.dot_general(a, b, (((1,), (1,)), ((), ())), preferred_element_type=F32)


def _ada_body(c_ref, w_ref, b_ref, o_ref):
    s = _silu(c_ref[...])
    o_ref[...] = _dot(s.astype(BF16), w_ref[...].astype(BF16)) + b_ref[...]


def _ada(c8, w, b):
    n = w.shape[1]
    tn = 1536
    return pl.pallas_call(
        _ada_body,
        grid=(n // tn,),
        in_specs=[pl.BlockSpec((8, D_MODEL), lambda j: (0, 0)),
                  pl.BlockSpec((D_MODEL, tn), lambda j: (0, j)),
                  pl.BlockSpec((1, tn), lambda j: (0, j))],
        out_specs=pl.BlockSpec((8, tn), lambda j: (0, j)),
        out_shape=jax.ShapeDtypeStruct((8, n), F32),
        compiler_params=_cparams(("arbitrary",)),
        name="ada",
    )(c8, w, b)


LR_COL0 = COL_GG + HEADS * DV
WPREP_ROWS = 1024


def _wprep_body(w_ref, wlr_ref, main_ref, lr_ref):
    main_ref[...] = w_ref[...].astype(BF16)
    pad = jnp.zeros((LANES - 2 * GLA_RANK, D_MODEL), F32)
    lr_ref[...] = jnp.concatenate([wlr_ref[...], pad], axis=0).astype(BF16)


def _wprep(wt):
    steps_before = LR_COL0 // WPREP_ROWS

    def src_row(i):
        row = jnp.where(i < steps_before, i * WPREP_ROWS, i * WPREP_ROWS + 2 * GLA_RANK)
        return (pl.multiple_of(row, 2 * GLA_RANK), 0)

    return pl.pallas_call(
        _wprep_body,
        grid=(PROJ_W // WPREP_ROWS,),
        in_specs=[pl.BlockSpec((pl.Element(WPREP_ROWS), pl.Element(D_MODEL)), src_row),
                  pl.BlockSpec((pl.Element(2 * GLA_RANK), pl.Element(D_MODEL)), lambda i: (LR_COL0, 0))],
        out_specs=[pl.BlockSpec((WPREP_ROWS, D_MODEL), lambda i: (i, 0)),
                   pl.BlockSpec((LANES, D_MODEL), lambda i: (0, 0))],
        out_shape=[jax.ShapeDtypeStruct((PROJ_W, D_MODEL), BF16), jax.ShapeDtypeStruct((LANES, D_MODEL), BF16)],
        compiler_params=_cparams(("arbitrary",)),
        name="wprep",
    )(wt, wt)


ROPE_TILE = COL_RQ // PROJ_TN
ROPE_COL0 = COL_RQ % PROJ_TN
assert ROPE_TILE != 0 and ROPE_COL0 + 2 * HEADS * DK <= PROJ_TN, "rotated columns: one tile, not the first"


def _rope(x, cos, sin):
    lane = lax.broadcasted_iota(jnp.int32, x.shape, 1)
    partner = jnp.where((lane % 64) < 32, pltpu.roll(x, DK - 32, 1), pltpu.roll(x, 32, 1))
    return x * cos + partner * sin


def _proj_body(x_ref, sh_ref, sc_ref, g_ref, w_ref, wlr_ref, cos_ref, sin_ref, o_ref, lr_ref, h_ref):
    j = pl.program_id(1)

    @pl.when(j == 0)
    def _():
        n = x_ref.shape[0] // PROJ_NORM_SPLIT
        for grp in range(PROJ_NORM_SPLIT):
            rows = pl.ds(grp * n, n)
            x = x_ref[rows, :]
            ms = jnp.mean(x * x, axis=-1, keepdims=True)
            y = x * lax.rsqrt(ms + NORM_EPS) * g_ref[...]
            hb = (y * (1.0 + sc_ref[0]) + sh_ref[0]).astype(BF16)
            h_ref[rows, :] = hb
            lr_ref[rows, :] = _dot_nt(hb, wlr_ref[...])
            o_ref[rows, :] = _dot_nt(hb, w_ref[...]).astype(BF16)

    @pl.when(jnp.logical_and(j != 0, j != ROPE_TILE))
    def _():
        o_ref[...] = _dot_nt(h_ref[...], w_ref[...]).astype(BF16)

    @pl.when(j == ROPE_TILE)
    def _():
        h = h_ref[...]
        cos, sin = cos_ref[...], sin_ref[...]
        for pair in range(HEADS):
            lo = ROPE_COL0 + 2 * DK * pair
            acc = _dot_nt(h, w_ref[lo:lo + 2 * DK, :])
            if pair >= HEADS // 2:
                acc = acc * (DK ** -0.5)
            for half in range(2):
                blk = acc[:, DK * half:DK * (half + 1)]
                o_ref[:, lo + DK * half:lo + DK * (half + 1)] = _rope(blk, cos, sin).astype(BF16)
        rope_end = ROPE_COL0 + 2 * HEADS * DK
        for lo, hi in ((0, ROPE_COL0), (rope_end, PROJ_TN)):
            if lo < hi:
                o_ref[:, lo:hi] = _dot_nt(h, w_ref[lo:hi, :]).astype(BF16)


def _proj(x2, sh, sc, g, w_main, w_lr, cos, sin, tm, rows_per_mod):
    t = x2.shape[0]
    tn = PROJ_TN
    tiles_per_mod = rows_per_mod // tm
    mod_map = lambda i, j: (i // tiles_per_mod, 0, 0)
    pos_tiles = cos.shape[0] // tm
    pos_map = lambda i, j: (i % pos_tiles, 0)
    return pl.pallas_call(
        _proj_body,
        grid=(t // tm, PROJ_W // tn),
        in_specs=[pl.BlockSpec((tm, D_MODEL), lambda i, j: (i, 0)),
                  pl.BlockSpec((1, 1, D_MODEL), mod_map),
                  pl.BlockSpec((1, 1, D_MODEL), mod_map),
                  pl.BlockSpec((1, D_MODEL), lambda i, j: (0, 0)),
                  pl.BlockSpec((tn, D_MODEL), lambda i, j: (j, 0)),
                  pl.BlockSpec((LANES, D_MODEL), lambda i, j: (0, 0)),
                  pl.BlockSpec((tm, DK), pos_map), pl.BlockSpec((tm, DK), pos_map)],
        out_specs=[pl.BlockSpec((tm, tn), lambda i, j: (i, j)),
                   pl.BlockSpec((tm, LANES), lambda i, j: (i, 0))],
        out_shape=[jax.ShapeDtypeStruct((t, PROJ_W), BF16),
                   jax.ShapeDtypeStruct((t, LANES), F32)],
        scratch_shapes=[pltpu.VMEM((tm, D_MODEL), BF16)],
        compiler_params=_cparams(("parallel", "arbitrary")),
        name="proj",
    )(x2, sh, sc, g, w_main, w_lr, cos, sin)


def _gla_consts():
    c, s = GLA_CHUNK, GLA_SUB
    msk = {}
    for rev in (False, True):
        cols = []
        for i in range(c // s):
            keys = np.arange(c)[_key_rows(i, rev)][None, :]
            t = np.arange(c)[:, None]
            visible = (keys > t) if rev else (keys <= t)
            cols.append(np.where(t // s == i, visible, False))
        msk[rev] = np.concatenate(cols, axis=1).astype(np.float32)
    return msk


def _key_rows(i, rev):
    return slice(GLA_SUB * i, GLA_CHUNK) if rev else slice(0, GLA_SUB * (i + 1))


def _log_gate(lr, w, b):
    z = _dot(lr.astype(BF16), w.astype(BF16)) + b
    return -(jnp.maximum(-z, 0.0) + jnp.log(1.0 + jnp.exp(-jnp.abs(z)))) * (1.0 / GLA_TAU)


def _subchunk_scan(la, rev):
    c = la.shape[0]
    pos = lax.broadcasted_iota(jnp.int32, la.shape, 0) % GLA_SUB
    w = la
    step = 1
    while step < GLA_SUB:
        if rev:
            w = w + jnp.where(pos < GLA_SUB - step, pltpu.roll(w, c - step, 0), 0.0)
        else:
            w = w + jnp.where(pos >= step, pltpu.roll(w, step, 0), 0.0)
        step *= 2
    return w


def _gla_intra_exact(q, kf, v, b, rev, scratch):
    q_s, b_s, o_s = scratch
    c = GLA_CHUNK
    q_s[...] = q.astype(F32) * (DK ** -0.5)
    b_s[...] = b
    vf = v.astype(F32)
    row = lax.broadcasted_iota(jnp.int32, (c, 1), 0)

    def body(t, carry):
        qt = q_s[pl.ds(t, 1), :]
        bt = b_s[pl.ds(t, 1), :]
        visible = (row > t) if rev else (row <= t)
        decay = jnp.exp(jnp.where(visible, bt - b, -jnp.inf))
        col = jnp.sum(kf * decay * qt, axis=1, keepdims=True)
        o_s[pl.ds(t, 1), :] = jnp.sum(col * vf, axis=0, keepdims=True)
        return carry

    lax.fori_loop(0, c, body, 0)
    return o_s[...]


def _gla_chunk(q, k, v, la, state_ref, mask, rev, exact_scratch=None):
    c, s = GLA_CHUNK, GLA_SUB
    nsub = c // s
    w = _subchunk_scan(la, rev)
    tot = [w[s * i:s * i + 1] if rev else w[s * (i + 1) - 1:s * (i + 1)] for i in range(nsub)]
    anchors = [None] * nsub
    acc = jnp.zeros_like(tot[0])
    worst = jnp.zeros_like(tot[0])
    for i in (reversed(range(nsub)) if rev else range(nsub)):
        anchors[i] = acc
        acc = acc + tot[i]
        worst = jnp.maximum(worst, -tot[i])
    b_end = acc
    ref = jnp.concatenate([jnp.broadcast_to(a, (s, DK)) for a in anchors], axis=0)
    b = w + ref

    qt = q.astype(F32) * (DK ** -0.5) * jnp.exp(w)
    kf = k.astype(F32)
    if exact_scratch is None:
        kstack = jnp.concatenate([kf[_key_rows(i, rev)] * jnp.exp(anchors[i] - b[_key_rows(i, rev)])
                                  for i in range(nsub)], axis=0).astype(BF16)
        raw = _dot_nt(qt.astype(BF16), kstack)
    kst = kf * jnp.exp(b_end - b)
    xt = jnp.concatenate([kst, jnp.broadcast_to(jnp.exp(b_end), (c, DK))], axis=0).T
    update = _dot(xt[:, :c].astype(BF16), v)
    state = state_ref[...]
    inter = _dot((qt * jnp.exp(ref)).astype(BF16), state.astype(BF16))
    state_ref[...] = xt[:, c:c + 1] * state + update
    yield
    if exact_scratch is None:
        vstack = jnp.concatenate([v[_key_rows(i, rev)] for i in range(nsub)], axis=0)
        intra = _dot(jnp.where(mask != 0.0, raw, 0.0).astype(BF16), vstack)
    else:
        intra = _gla_intra_exact(q, kf, v, b, rev, exact_scratch)
    return inter + intra, worst


def _round_robin(gens):
    results = [None] * len(gens)
    live = list(range(len(gens)))
    while live:
        for g in list(live):
            try:
                next(gens[g])
            except StopIteration as done:
                results[g] = done.value
                live.remove(g)
    return results


def _run_phased(chunks, emit, lag):
    def finish(tag, gen):
        try:
            next(gen)
        except StopIteration as done:
            emit(tag, done.value)
            return
        raise AssertionError("chunk generator has more than two phases")

    pending = []
    for tag, gen in chunks:
        next(gen)
        pending.append((tag, gen))
        if len(pending) > lag:
            finish(*pending.pop(0))
    for item in pending:
        finish(*item)


def _gla_body(qf_ref, kf_ref, vf_ref, lrf_ref, qb_ref, kb_ref, vb_ref, lrb_ref,
              wf_ref, bf_ref, wb_ref, bb_ref, mskf_ref, mskb_ref,
              s0f_ref, s0b_ref, of_ref, ob_ref, sf_ref, sb_ref, rng_ref, *exact_scratch, nchunks):
    @pl.when(pl.program_id(2) == 0)
    def _():
        sf_ref[...] = s0f_ref[...]
        sb_ref[...] = s0b_ref[...]

    c = GLA_CHUNK
    scratch = exact_scratch or None
    la_f = _log_gate(lrf_ref[...], wf_ref[...], bf_ref[...])
    la_b = _log_gate(lrb_ref[...], wb_ref[...], bb_ref[...])
    mskf, mskb = mskf_ref[...], mskb_ref[...]
    st_f, st_b = sf_ref.at[0, 0], sb_ref.at[0, 0]
    worst = [jnp.zeros((1, DK), F32)]

    def chunks():
        for n in range(nchunks):
            rf = slice(c * n, c * (n + 1))
            yield (of_ref, rf), _gla_chunk(qf_ref[rf, :], kf_ref[rf, :], vf_ref[rf, :], la_f[rf, :], st_f, mskf,
                                           False, scratch)
            m = nchunks - 1 - n
            rb = slice(c * m, c * (m + 1))
            yield (ob_ref, rb), _gla_chunk(qb_ref[rb, :], kb_ref[rb, :], vb_ref[rb, :], la_b[rb, :], st_b, mskb,
                                           True, scratch)

    def emit(tag, result):
        o_ref, rows = tag
        o_ref[rows, :] = result[0].astype(BF16)
        worst[0] = jnp.maximum(worst[0], result[1])

    _run_phased(chunks(), emit, GLA_PHASE_LAG)
    rng_ref[0, 0, 0] = jnp.broadcast_to(worst[0], (8, DK))


def _gla(p, lr, wlr_f, blr_f, wlr_b, blr_b, s0f, s0b, batch, seq, tb, exact=False):
    nb = seq // tb
    msk = _gla_consts()
    mskf, mskb = jnp.asarray(msk[False]), jnp.asarray(msk[True])

    def fwd(col0, width):
        return lambda b, h, i: (b * nb + i, col0 // width + h)

    def bwd(col0, width):
        return lambda b, h, i: (b * nb + nb - 1 - i, col0 // width + h)

    const2 = lambda b, h, i: (0, 0)
    headcol = lambda b, h, i: (0, h)
    st_map = lambda b, h, i: (b, h, 0, 0)
    st_spec = pl.BlockSpec((1, 1, DK, DV), st_map)
    st_shape = jax.ShapeDtypeStruct((batch, HEADS, DK, DV), F32)
    o_shape = jax.ShapeDtypeStruct((batch * seq, HEADS * DV), BF16)
    return pl.pallas_call(
        functools.partial(_gla_body, nchunks=tb // GLA_CHUNK),
        grid=(batch, HEADS, nb),
        in_specs=[pl.BlockSpec((tb, DK), fwd(COL_GQ, DK)),
                  pl.BlockSpec((tb, DK), fwd(COL_GK, DK)),
                  pl.BlockSpec((tb, DV), fwd(COL_GV, DV)),
                  pl.BlockSpec((tb, LANES), lambda b, h, i: (b * nb + i, 0)),
                  pl.BlockSpec((tb, DK), bwd(COL_GQ, DK)),
                  pl.BlockSpec((tb, DK), bwd(COL_GK, DK)),
                  pl.BlockSpec((tb, DV), bwd(COL_GV, DV)),
                  pl.BlockSpec((tb, LANES), lambda b, h, i: (b * nb + nb - 1 - i, 0)),
                  pl.BlockSpec((LANES, DK), headcol),
                  pl.BlockSpec((1, DK), headcol),
                  pl.BlockSpec((LANES, DK), headcol),
                  pl.BlockSpec((1, DK), headcol),
                  pl.BlockSpec(mskf.shape, const2),
                  pl.BlockSpec(mskb.shape, const2),
                  st_spec, st_spec],
        out_specs=[pl.BlockSpec((tb, DV), lambda b, h, i: (b * nb + i, h)),
                   pl.BlockSpec((tb, DV), lambda b, h, i: (b * nb + nb - 1 - i, h)),
                   st_spec, st_spec,
                   pl.BlockSpec((1, 1, 1, 8, DK), lambda b, h, i: (b, h, i, 0, 0))],
        out_shape=[o_shape, o_shape, st_shape, st_shape,
                   jax.ShapeDtypeStruct((batch, HEADS, nb, 8, DK), F32)],
        scratch_shapes=([pltpu.VMEM((GLA_CHUNK, DK), F32), pltpu.VMEM((GLA_CHUNK, DK), F32),
                         pltpu.VMEM((GLA_CHUNK, DV), F32)] if exact else []),
        compiler_params=_cparams(("parallel", "parallel", "arbitrary")),
        name="gla_exact" if exact else "gla",
    )(p, p, p, lr, p, p, p, lr, wlr_f, blr_f, wlr_b, blr_b, mskf, mskb, s0f, s0b)


def _ret_consts():
    c = RET_CHUNK
    hh = np.arange(HEADS, dtype=np.float64)
    lg = {False: np.log1p(-np.exp2(-5.0 - hh)), True: np.log1p(-np.exp2(-5.5 - hh))}
    t = np.arange(c, dtype=np.float64)[:, None]
    u = np.arange(c, dtype=np.float64)[None, :]
    pos = np.arange(c, dtype=np.float64)
    out = {}
    for rev in (False, True):
        g = lg[rev][:, None, None]
        if not rev:
            dmat = np.where(u <= t, np.exp((t - u) * g), 0.0)
            qd = np.exp((pos + 1.0)[None, :] * lg[rev][:, None])
            kd = np.exp((c - 1.0 - pos)[None, :] * lg[rev][:, None])
        else:
            dmat = np.where(u > t, np.exp((u - t) * g), 0.0)
            qd = np.exp((c - pos)[None, :] * lg[rev][:, None])
            kd = np.exp(pos[None, :] * lg[rev][:, None])
        cd = np.exp(c * lg[rev])
        out[rev] = (dmat.astype(np.float32),
                    np.repeat(qd[:, :, None], DV, axis=2).astype(np.float32),
                    np.repeat(kd[:, :, None], DK, axis=2).astype(np.float32),
                    np.repeat(cd[:, None, None], DV, axis=2).astype(np.float32))
    return out


def _ret_chunk(q, k, v, state_ref, dmat, qd, kd, cd):
    raw = _dot_nt(q, k)
    update = _dot((k.astype(F32) * kd).T.astype(BF16), v)
    state = state_ref[...]
    inter = _dot(q, state.astype(BF16)) * qd
    state_ref[...] = cd * state + update
    yield
    return inter + _dot((raw * dmat).astype(BF16), v)


def _ret_body(qf_ref, kf_ref, vf_ref, qb_ref, kb_ref, vb_ref,
              dmf_ref, qdf_ref, kdf_ref, cdf_ref, dmb_ref, qdb_ref, kdb_ref, cdb_ref,
              s0f_ref, s0b_ref, of_ref, ob_ref, sf_ref, sb_ref, *, nchunks):
    @pl.when(pl.program_id(2) == 0)
    def _():
        sf_ref[...] = s0f_ref[...]
        sb_ref[...] = s0b_ref[...]

    c = RET_CHUNK
    cf = (dmf_ref[0], qdf_ref[0], kdf_ref[0], cdf_ref[0])
    cb = (dmb_ref[0], qdb_ref[0], kdb_ref[0], cdb_ref[0])
    st_f, st_b = sf_ref.at[0, 0], sb_ref.at[0, 0]
    def chunks():
        for n in range(nchunks):
            rf = slice(c * n, c * (n + 1))
            yield (of_ref, rf), _ret_chunk(qf_ref[rf, :], kf_ref[rf, :], vf_ref[rf, :], st_f, *cf)
            m = nchunks - 1 - n
            rb = slice(c * m, c * (m + 1))
            yield (ob_ref, rb), _ret_chunk(qb_ref[rb, :], kb_ref[rb, :], vb_ref[rb, :], st_b, *cb)

    def emit(tag, out):
        o_ref, rows = tag
        o_ref[rows, :] = out.astype(BF16)

    _run_phased(chunks(), emit, RET_PHASE_LAG)


def _ret(p, s0f, s0b, batch, seq, tb):
    nb = seq // tb
    consts = _ret_consts()

    def fwd(col0, width):
        return lambda b, h, i: (b * nb + i, col0 // width + h)

    def bwd(col0, width):
        return lambda b, h, i: (b * nb + nb - 1 - i, col0 // width + h)

    head3 = lambda b, h, i: (h, 0, 0)
    st_spec = pl.BlockSpec((1, 1, DK, DV), lambda b, h, i: (b, h, 0, 0))
    st_shape = jax.ShapeDtypeStruct((batch, HEADS, DK, DV), F32)
    o_shape = jax.ShapeDtypeStruct((batch * seq, HEADS * DV), BF16)
    const_specs = []
    const_args = []
    for rev in (False, True):
        for a in consts[rev]:
            const_specs.append(pl.BlockSpec((1,) + a.shape[1:], head3))
            const_args.append(jnp.asarray(a))
    return pl.pallas_call(
        functools.partial(_ret_body, nchunks=tb // RET_CHUNK),
        grid=(batch, HEADS, nb),
        in_specs=[pl.BlockSpec((tb, DK), fwd(COL_RQ, DK)),
                  pl.BlockSpec((tb, DK), fwd(COL_RK, DK)),
                  pl.BlockSpec((tb, DV), fwd(COL_RV, DV)),
                  pl.BlockSpec((tb, DK), bwd(COL_RQ, DK)),
                  pl.BlockSpec((tb, DK), bwd(COL_RK, DK)),
                  pl.BlockSpec((tb, DV), bwd(COL_RV, DV))]
                 + const_specs + [st_spec, st_spec],
        out_specs=[pl.BlockSpec((tb, DV), lambda b, h, i: (b * nb + i, h)),
                   pl.BlockSpec((tb, DV), lambda b, h, i: (b * nb + nb - 1 - i, h)),
                   st_spec, st_spec],
        out_shape=[o_shape, o_shape, st_shape, st_shape],
        compiler_params=_cparams(("parallel", "parallel", "arbitrary")),
        name="ret",
    )(p, p, p, p, p, p, *const_args, s0f, s0b)


def _rope_tables(seq):
    n = DK // 4
    inv = np.float32(ROPE_BASE) ** (-np.arange(n, dtype=np.float32) / np.float32(n))
    pos = np.arange(seq)
    ar = (pos // GRID_W).astype(np.float32)[:, None] * inv[None, :]
    ac = (pos % GRID_W).astype(np.float32)[:, None] * inv[None, :]
    cos = np.concatenate([np.cos(ar), np.cos(ar), np.cos(ac), np.cos(ac)], axis=1)
    sin = np.concatenate([-np.sin(ar), np.sin(ar), -np.sin(ac), np.sin(ac)], axis=1)
    return jnp.asarray(cos, F32), jnp.asarray(sin, F32)


def _route(logits_t):
    g = [logits_t[i:i + 1] for i in range(N_GROUPS)]
    gmax = jnp.maximum(jnp.maximum(g[0], g[1]), jnp.maximum(g[2], g[3]))
    gsel = jnp.where(g[0] == gmax, 0, jnp.where(g[1] == gmax, 1, jnp.where(g[2] == gmax, 2, 3)))
    gsum = (jnp.exp(g[0] - gmax) + jnp.exp(g[1] - gmax)) + (jnp.exp(g[2] - gmax) + jnp.exp(g[3] - gmax))
    gw = 1.0 / gsum
    e = [logits_t[EXP_ROW0 + EXPERTS_PER_GROUP * i:EXP_ROW0 + EXPERTS_PER_GROUP * (i + 1)]
         for i in range(N_GROUPS)]
    el = jnp.where(gsel == 0, e[0], jnp.where(gsel == 1, e[1], jnp.where(gsel == 2, e[2], e[3])))
    row = lax.broadcasted_iota(jnp.int32, el.shape, 0).astype(F32)
    none = float(EXPERTS_PER_GROUP)
    m1 = jnp.max(el, axis=0, keepdims=True)
    i1 = jnp.min(jnp.where(el == m1, row, none), axis=0, keepdims=True)
    el2 = jnp.where(row == i1, -jnp.inf, el)
    m2 = jnp.max(el2, axis=0, keepdims=True)
    i2 = jnp.min(jnp.where(el2 == m2, row, none), axis=0, keepdims=True)
    r = jnp.exp(m2 - m1)
    w1 = gw / (1.0 + r)
    w2 = gw * r / (1.0 + r)
    base = gsel * EXPERTS_PER_GROUP
    ids = jnp.concatenate([base + i1.astype(jnp.int32), base + i2.astype(jnp.int32)], axis=0)
    return ids, jnp.concatenate([w1, w2], axis=0)


def _store_token_tiled(ref, val):
    m = val.shape[0]
    for j in range(FEAT_TILES):
        ref[pl.ds(j, m, stride=FEAT_TILES), :] = val[:, LANES * j:LANES * (j + 1)]


def _load_token_tiled(ref, m):
    return jnp.concatenate([ref[pl.ds(j, m, stride=FEAT_TILES), :] for j in range(FEAT_TILES)], axis=1)


def _merge_body(gf_ref, gb_ref, rf_ref, rb_ref, gg_ref, rg_ref, mg_ref, mr_ref, x_ref,
                g1_ref, sh2_ref, nfs_ref, gn_ref, wg_ref, wr_ref, wo_ref, wrt_ref, brt_ref, upper_ref,
                h_ref, hn_ref, ids_ref, ew_ref, cnt_ref, wg_s, wr_s, wo_s, stage, sem):
    @pl.when(pl.program_id(0) == 0)
    def _():
        cnt_ref[...] = jnp.zeros_like(cnt_ref)
        for w_hbm, w_s in ((wg_ref, wg_s), (wr_ref, wr_s), (wo_ref, wo_s)):
            copy = pltpu.make_async_copy(w_hbm, stage, sem.at[0])
            copy.start()
            copy.wait()
            w_s[...] = stage[...].astype(BF16)

    def rows_to_logits(r0, n):
        rows = pl.ds(r0, n)
        og = (gf_ref[rows, :] + gb_ref[rows, :]).astype(F32)
        orr = (rf_ref[rows, :] + rb_ref[rows, :]).astype(F32)
        gparts, rparts = [], []
        for hh in range(HEADS):
            seg = og[:, DV * hh:DV * (hh + 1)]
            ms = jnp.mean(seg * seg, axis=-1, keepdims=True)
            gparts.append(seg * lax.rsqrt(ms + NORM_EPS))
            seg = orr[:, DV * hh:DV * (hh + 1)]
            mu = jnp.mean(seg, axis=-1, keepdims=True)
            cen = seg - mu
            var = jnp.mean(cen * cen, axis=-1, keepdims=True)
            rparts.append(cen * lax.rsqrt(var + NORM_EPS))
        o_gla = (jnp.concatenate(gparts, axis=1) * gn_ref[...]).astype(BF16) * _silu(gg_ref[rows, :])
        o_ret = jnp.concatenate(rparts, axis=1).astype(BF16) * _silu(rg_ref[rows, :])
        a_gla = _dot(o_gla, wg_s[...])
        a_ret = _dot(o_ret, wr_s[...])
        yield
        y = _sigmoid(mg_ref[rows, :]) * a_gla.astype(BF16) + _sigmoid(mr_ref[rows, :]) * a_ret.astype(BF16)
        out = _dot(y, wo_s[...])
        yield
        h = x_ref[rows, :] + g1_ref[0] * out
        h_ref[rows, :] = h
        ms = jnp.mean(h * h, axis=-1, keepdims=True)
        hn = h * lax.rsqrt(ms + NORM_EPS) * nfs_ref[0] + sh2_ref[0]
        _store_token_tiled(hn_ref.at[pl.ds(r0 * FEAT_TILES, n * FEAT_TILES)], hn)
        hn_hi = hn.astype(BF16)
        hn_lo = (hn - hn_hi.astype(F32)).astype(BF16)
        both_w = _dot(hn_hi, wrt_ref[...])
        lo_w = _dot(hn_lo, wrt_ref[:, :ROUTER_W])
        yield
        return (both_w[:, :ROUTER_W] + both_w[:, ROUTER_W:] + lo_w) + brt_ref[...]

    tm = x_ref.shape[0]
    n = tm // MERGE_SPLIT
    logits = jnp.concatenate(_round_robin([rows_to_logits(g * n, n) for g in range(MERGE_SPLIT)]), axis=0)
    ids, ew = _route(logits.T)

    erow = lax.broadcasted_iota(jnp.int32, (N_EXPERTS, tm), 0)
    oh0 = jnp.where(erow == ids[0:1], 1.0, 0.0)
    oh1 = jnp.where(erow == ids[1:2], 1.0, 0.0)
    both = oh0 + oh1
    before = _dot(both.astype(BF16), upper_ref[...]) + cnt_ref[:, 0:1].astype(F32)
    rank0 = jnp.sum(oh0 * before, axis=0, keepdims=True)
    rank1 = jnp.sum(oh1 * before, axis=0, keepdims=True)
    total = cnt_ref[:, 0:1] + jnp.sum(both, axis=1, keepdims=True).astype(jnp.int32)
    cnt_ref[...] = jnp.broadcast_to(total, cnt_ref.shape)
    ids_ref[...] = jnp.concatenate([ids, rank0.astype(jnp.int32), rank1.astype(jnp.int32),
                                    jnp.zeros((4, tm), jnp.int32)], axis=0)
    ew_ref[...] = jnp.concatenate([ew, jnp.zeros((6, tm), F32)], axis=0)


def _merge(gf, gb, rf, rb, p, x2, g1, sh2, nfs, gn, wg, wr, wo, wrt, brt, seq, tm):
    t = x2.shape[0]
    tiles_per_batch = seq // tm
    row = lambda i: (i, 0)
    mod = lambda i: (i // tiles_per_batch, 0, 0)
    const = lambda i: (0, 0)
    tok = pl.BlockSpec((tm, D_MODEL), row)
    vec = pl.BlockSpec((1, D_MODEL), const)
    modspec = pl.BlockSpec((1, 1, D_MODEL), mod)
    wspec = pl.BlockSpec(memory_space=pl.ANY)
    upper = jnp.asarray(np.triu(np.ones((tm, tm), np.float32), 1), BF16)

    def pcol(col0):
        return pl.BlockSpec((tm, D_MODEL), lambda i: (i, col0 // D_MODEL))

    return pl.pallas_call(
        _merge_body,
        grid=(t // tm,),
        in_specs=[tok, tok, tok, tok, pcol(COL_GG), pcol(COL_RG), pcol(COL_MG), pcol(COL_MR), tok,
                  modspec, modspec, modspec, vec, wspec, wspec, wspec,
                  pl.BlockSpec((D_MODEL, 2 * ROUTER_W), const), pl.BlockSpec((1, ROUTER_W), const),
                  pl.BlockSpec((tm, tm), const)],
        out_specs=[tok, pl.BlockSpec((tm * FEAT_TILES, LANES), row),
                   pl.BlockSpec((8, tm), lambda i: (0, i)), pl.BlockSpec((8, tm), lambda i: (0, i)),
                   pl.BlockSpec((N_EXPERTS, LANES), const)],
        out_shape=[jax.ShapeDtypeStruct((t, D_MODEL), F32), jax.ShapeDtypeStruct((t * FEAT_TILES, LANES), F32),
                   jax.ShapeDtypeStruct((8, t), jnp.int32), jax.ShapeDtypeStruct((8, t), F32),
                   jax.ShapeDtypeStruct((N_EXPERTS, LANES), jnp.int32)],
        scratch_shapes=[pltpu.VMEM((D_MODEL, D_MODEL), BF16)] * 3
                       + [pltpu.VMEM((D_MODEL, D_MODEL), F32), pltpu.SemaphoreType.DMA((1,))],
        compiler_params=_cparams(("arbitrary",)),
        name="merge",
    )(gf, gb, rf, rb, p, p, p, p, x2, g1, sh2, nfs, gn, wg, wr, wo, wrt, brt, upper)


GATHER_UNROLL = 8


def _aligned(tok):
    off = tok * FEAT_TILES
    return off if isinstance(off, int) else pl.multiple_of(off, FEAT_TILES)


def _token_copy(src_hbm, dst, sem, src_tok, dst_tok):
    return pltpu.make_async_copy(src_hbm.at[pl.ds(_aligned(src_tok), FEAT_TILES), :],
                                 dst.at[pl.ds(_aligned(dst_tok), FEAT_TILES), :], sem)


def _wait_tokens(src_hbm, dst, sem):
    pltpu.make_async_copy(src_hbm.at[pl.ds(0, dst.shape[0]), :], dst, sem).wait()


GATHER_BUFS = 3
GATHER_AHEAD = GATHER_BUFS - 1


def _issue_rows(n, start_row, unrolled):
    if unrolled:
        for r in range(n):
            start_row(r, r % 2)
    else:
        def body(g, carry):
            for u in range(GATHER_UNROLL):
                start_row(g * GATHER_UNROLL + u, u % 2)
            return carry
        lax.fori_loop(0, n // GATHER_UNROLL, body, 0)


EXPERT_GROUP = 2


EXPERT_W_SLOTS = 2
assert EXPERT_GROUP <= EXPERT_W_SLOTS, "the blocks of one step may belong to EXPERT_GROUP consecutive runs"


def _expert_body(rank_ref, eseq_ref, src0_ref, stok_ref, meta_ref, hn_hbm, wg_hbm, wu_hbm, wd_hbm, y_ref, *scratch):
    bufs, sems = scratch[:GATHER_BUFS], scratch[GATHER_BUFS]
    stage, w_s, wsem = scratch[GATHER_BUFS + 1:GATHER_BUFS + 4], scratch[GATHER_BUFS + 4:GATHER_BUFS + 7], scratch[-1]
    w_hbm = (wg_hbm, wu_hbm, wd_hbm)
    i = pl.program_id(0)
    nused, nranks = meta_ref[0], meta_ref[1]
    nsteps = (nused + EXPERT_GROUP - 1) // EXPERT_GROUP
    last = stok_ref.shape[0] - 1
    rows = MOE_BLOCK * FEAT_TILES

    def issue(step, s, unrolled):
        bases = [src0_ref[step * EXPERT_GROUP + g] for g in range(EXPERT_GROUP)]

        def start(g, r, priority):
            tok = stok_ref[jnp.minimum(bases[g] + r, last)]
            _token_copy(hn_hbm, bufs[s], sems.at[s], tok, g * MOE_BLOCK + r).start(priority=priority)

        if unrolled:
            _issue_rows(EXPERT_GROUP * MOE_BLOCK, lambda j, p: start(j // MOE_BLOCK, j % MOE_BLOCK, p), True)
        else:
            for g in range(EXPERT_GROUP):
                _issue_rows(MOE_BLOCK, lambda r, p, g=g: start(g, r, p), False)

    def weight_copies(rank):
        slot = rank % EXPERT_W_SLOTS
        expert = eseq_ref[rank]
        return [pltpu.make_async_copy(w_hbm[k].at[expert], stage[k].at[slot], wsem.at[slot]) for k in range(3)]

    def activate(rank):
        slot = rank % EXPERT_W_SLOTS
        for copy in weight_copies(rank):
            copy.wait()
        for k in range(3):
            w_s[k][slot] = stage[k][slot].astype(BF16)

        @pl.when(rank + 1 < nranks)
        def _():
            for copy in weight_copies(rank + 1):
                copy.start()

    @pl.when(i == 0)
    def _():
        for b in range(GATHER_AHEAD):
            issue(b, b, False)

        @pl.when(nranks > 0)
        def _():
            for copy in weight_copies(0):
                copy.start()

    for g in range(EXPERT_GROUP):
        blk = i * EXPERT_GROUP + g
        rank = rank_ref[blk]
        first_of_run = jnp.logical_or(blk == 0, rank != rank_ref[jnp.maximum(blk - 1, 0)])

        @pl.when(jnp.logical_and(blk < nused, first_of_run))
        def _():
            activate(rank)

    def block_phases(cur, g):
        part = pl.ds(g * rows, rows)
        slot = rank_ref[i * EXPERT_GROUP + g] % EXPERT_W_SLOTS
        xb = _load_token_tiled(bufs[cur].at[part], MOE_BLOCK).astype(BF16)
        gate = _dot(xb, w_s[0][slot])
        up = _dot(xb, w_s[1][slot])
        yield
        y = _dot((_silu(gate) * up).astype(BF16), w_s[2][slot])
        yield
        _store_token_tiled(y_ref.at[part], y)

    for cur in range(GATHER_BUFS):
        mine = i % GATHER_BUFS == cur

        @pl.when(jnp.logical_and(mine, i < nsteps))
        def _():
            _wait_tokens(hn_hbm, bufs[cur], sems.at[cur])
            issue(i + GATHER_AHEAD, (cur + GATHER_AHEAD) % GATHER_BUFS, True)
            _round_robin([block_phases(cur, g) for g in range(EXPERT_GROUP)])

        @pl.when(jnp.logical_and(mine, jnp.logical_and(i >= nsteps, i < nsteps + GATHER_AHEAD)))
        def _():
            _wait_tokens(hn_hbm, bufs[cur], sems.at[cur])

    @pl.when(i >= nsteps)
    def _():
        y_ref[...] = jnp.zeros_like(y_ref)


def _experts(block_rank, expert_seq, src0, stok, meta, hn, w_gate, w_up, w_down, nblk):
    rows = EXPERT_GROUP * MOE_BLOCK * FEAT_TILES
    w_shapes = ((D_MODEL, EXPERT_FF), (D_MODEL, EXPERT_FF), (EXPERT_FF, D_MODEL))
    any_spec = pl.BlockSpec(memory_space=pl.ANY)
    return pl.pallas_call(
        _expert_body,
        grid_spec=pltpu.PrefetchScalarGridSpec(
            num_scalar_prefetch=5,
            grid=(nblk // EXPERT_GROUP,),
            in_specs=[any_spec, any_spec, any_spec, any_spec],
            out_specs=pl.BlockSpec((rows, LANES), lambda i, *prefetch: (i, 0)),
            scratch_shapes=[pltpu.VMEM((rows, LANES), F32)] * GATHER_BUFS
                           + [pltpu.SemaphoreType.DMA((GATHER_BUFS,))]
                           + [pltpu.VMEM((EXPERT_W_SLOTS,) + shape, F32) for shape in w_shapes]
                           + [pltpu.VMEM((EXPERT_W_SLOTS,) + shape, BF16) for shape in w_shapes]
                           + [pltpu.SemaphoreType.DMA((EXPERT_W_SLOTS,))]),
        out_shape=jax.ShapeDtypeStruct((nblk // EXPERT_GROUP * rows, LANES), F32),
        compiler_params=_cparams(("arbitrary",)),
        name="experts",
    )(block_rank, expert_seq, src0, stok, meta, hn, w_gate, w_up, w_down)


def _combine_body(pos_ref, y_hbm, h_ref, ew_ref, g2_ref, nf_ref, o_ref, *scratch):
    i = pl.program_id(0)
    n = pl.num_programs(0)
    tm = COMBINE_TM
    t = pos_ref.shape[0] // TOP_K
    bufs, sems = scratch[:GATHER_BUFS], scratch[GATHER_BUFS]

    def issue(tile, s, unrolled):
        def start_row(j, priority):
            k, r = j % TOP_K, j // TOP_K
            _token_copy(y_hbm, bufs[s].at[k], sems.at[s], pos_ref[k * t + tile * tm + r], r).start(priority=priority)
        _issue_rows(tm * TOP_K, start_row, unrolled)

    def wait(s):
        for k in range(TOP_K):
            _wait_tokens(y_hbm, bufs[s].at[k], sems.at[s])

    @pl.when(i == 0)
    def _():
        for b in range(GATHER_AHEAD):
            issue(b, b, False)

    for cur in range(GATHER_BUFS):
        @pl.when(i % GATHER_BUFS == cur)
        def _():
            wait(cur)
            issue(jnp.minimum(i + GATHER_AHEAD, n - 1), (cur + GATHER_AHEAD) % GATHER_BUFS, True)
            wt = jnp.concatenate([ew_ref[...]] * (LANES // 8), axis=0).T
            moe = (wt[:, 0:1] * _load_token_tiled(bufs[cur].at[0], tm)
                   + wt[:, 1:2] * _load_token_tiled(bufs[cur].at[1], tm))
            h = h_ref[...] + g2_ref[0] * moe
            ms = jnp.mean(h * h, axis=-1, keepdims=True)
            o_ref[...] = h * lax.rsqrt(ms + NORM_EPS) * nf_ref[...]

        @pl.when(jnp.logical_and(i % GATHER_BUFS == cur, i == n - 1))
        def _():
            for ahead in range(1, GATHER_BUFS):
                wait((cur + ahead) % GATHER_BUFS)


def _combine(pos, y_pad, h, ew, g2, nf, seq):
    t = h.shape[0]
    tm = COMBINE_TM
    tiles_per_batch = seq // tm
    ybuf = pltpu.VMEM((TOP_K, tm * FEAT_TILES, LANES), F32)
    return pl.pallas_call(
        _combine_body,
        grid_spec=pltpu.PrefetchScalarGridSpec(
            num_scalar_prefetch=1,
            grid=(t // tm,),
            in_specs=[pl.BlockSpec(memory_space=pl.ANY),
                      pl.BlockSpec((tm, D_MODEL), lambda i, pos: (i, 0)),
                      pl.BlockSpec((8, tm), lambda i, pos: (0, i)),
                      pl.BlockSpec((1, 1, D_MODEL), lambda i, pos: (i // tiles_per_batch, 0, 0)),
                      pl.BlockSpec((1, D_MODEL), lambda i, pos: (0, 0))],
            out_specs=pl.BlockSpec((tm, D_MODEL), lambda i, pos: (i, 0)),
            scratch_shapes=[ybuf] * GATHER_BUFS + [pltpu.SemaphoreType.DMA((GATHER_BUFS,))]),
        out_shape=jax.ShapeDtypeStruct((t, D_MODEL), F32),
        compiler_params=_cparams(("arbitrary",)),
        name="combine",
    )(pos, y_pad, h, ew, g2, nf)


def _dispatch_indices(ids, ranks, counts, t):
    a = t * TOP_K
    nblk = a // MOE_BLOCK + N_EXPERTS + GATHER_AHEAD * EXPERT_GROUP
    experts = jnp.arange(N_EXPERTS, dtype=jnp.int32)
    starts = jnp.cumsum(counts) - counts
    padded = (counts + MOE_BLOCK - 1) // MOE_BLOCK * MOE_BLOCK
    pends = jnp.cumsum(padded)
    pstarts = pends - padded
    block_start = jnp.arange(nblk, dtype=jnp.int32) * MOE_BLOCK
    block_e = jnp.minimum(jnp.sum((block_start[:, None] >= pends[None, :]).astype(jnp.int32), axis=1),
                          N_EXPERTS - 1)
    of_block = (block_e[:, None] == experts[None, :]).astype(jnp.int32)
    src0 = block_start + jnp.sum(of_block * (starts - pstarts)[None, :], axis=1)
    nused = pends[-1:] // MOE_BLOCK
    blocks = jnp.arange(nblk, dtype=jnp.int32)
    used = blocks < nused
    starts_run = jnp.logical_and(used, jnp.concatenate([jnp.ones((1,), bool), block_e[1:] != block_e[:-1]]))
    nranks = jnp.sum(starts_run.astype(jnp.int32), keepdims=True)
    run = jnp.cumsum(starts_run.astype(jnp.int32)) - 1
    block_rank = jnp.clip(run, 0, jnp.maximum(nranks - 1, 0))
    run_ids = jnp.arange(N_EXPERTS + 1, dtype=jnp.int32)
    expert_seq = jnp.sum(jnp.where(jnp.logical_and(starts_run[:, None], run[:, None] == run_ids[None, :]),
                                   block_e[:, None], 0), axis=0)
    meta = jnp.concatenate([nused, nranks])
    tok = jnp.arange(t, dtype=jnp.int32)[None, :]
    slot = jnp.arange(TOP_K, dtype=jnp.int32)[:, None]
    key = ids * a + tok * TOP_K + slot
    stok = (jnp.sort(key.reshape(a)) % a) // TOP_K
    ids_d = ids.reshape(a // LANES, LANES)
    pstart_of = jnp.zeros_like(ids_d)
    for e in range(N_EXPERTS):
        pstart_of = jnp.where(ids_d == e, pstarts[e], pstart_of)
    pos = (ranks.reshape(a // LANES, LANES) + pstart_of).reshape(a)
    i32 = lambda v: v.astype(jnp.int32)
    return i32(block_rank), i32(expert_seq), i32(src0), i32(stok), i32(meta), i32(pos), nblk


def kernel(x, c, ctx, c_ctx, w_ada, b_ada, norm_mix, norm_ffn, w_in, gla_lr_w, gla_lr_b, gla_norm,
           w_branch_gla, w_branch_ret, w_out, w_router_group, b_router_group, w_router_expert,
           b_router_expert, w_expert_gate, w_expert_up, w_expert_down, norm_final):
    batch, seq, d = x.shape
    ctx_len = ctx.shape[1]
    assert d == D_MODEL and w_ada.shape[0] == 1, "single-layer block with D_MODEL features"
    t = batch * seq

    c8 = jnp.zeros((8, d), F32).at[:batch].set(c).at[batch].set(c_ctx)
    mod = _ada(c8, w_ada[0], b_ada[0][None, :])
    sh1, sc1, g1, sh2, sc2, g2 = [mod[:, d * i:d * (i + 1)] for i in range(6)]
    lat = lambda m: m[:batch, None, :]
    cx = lambda m: m[batch:batch + 1, None, :]

    w_main, w_lr = _wprep(jnp.swapaxes(w_in[0], 0, 1))
    nm = norm_mix[0][None, :]

    ones = jnp.ones((batch * ctx_len, DK), F32)
    p_ctx, lr_ctx = _proj(ctx.reshape(batch * ctx_len, d), cx(sh1), cx(sc1), nm, w_main, w_lr,
                          ones, jnp.zeros_like(ones), tm=batch * ctx_len, rows_per_mod=batch * ctx_len)
    x2 = x.reshape(t, d)
    cos, sin = _rope_tables(seq)
    p_lat, lr_lat = _proj(x2, lat(sh1), lat(sc1), nm, w_main, w_lr, cos, sin, tm=PROJ_TM, rows_per_mod=seq)

    wlr_f = jnp.zeros((LANES, HEADS * DK), F32).at[:GLA_RANK].set(gla_lr_w[0, 0])
    wlr_b = jnp.zeros((LANES, HEADS * DK), F32).at[GLA_RANK:2 * GLA_RANK].set(gla_lr_w[0, 1])
    blr_f, blr_b = gla_lr_b[0, 0][None, :], gla_lr_b[0, 1][None, :]
    zero_state = jnp.zeros((batch, HEADS, DK, DV), F32)

    _, _, gs_f, gs_b, _ = _gla(p_ctx, lr_ctx, wlr_f, blr_f, wlr_b, blr_b, zero_state, zero_state,
                               batch, ctx_len, tb=ctx_len)
    gla_args = (p_lat, lr_lat, wlr_f, blr_f, wlr_b, blr_b, gs_f, gs_b, batch, seq)
    gla_f, gla_b, _, _, decay_range = _gla(*gla_args, tb=SCAN_TB)
    gla_f, gla_b = lax.cond(jnp.max(decay_range) > GLA_SAFE_RANGE,
                            lambda: tuple(_gla(*gla_args, tb=GLA_EXACT_TB, exact=True)[:2]),
                            lambda: (gla_f, gla_b))

    _, _, rs_f, rs_b = _ret(p_ctx, zero_state, zero_state, batch, ctx_len, tb=ctx_len)
    ret_f, ret_b, _, _ = _ret(p_lat, rs_f, rs_b, batch, seq, tb=SCAN_TB)

    wrt = jnp.zeros((d, ROUTER_W), F32)
    wrt = wrt.at[:, :N_GROUPS].set(w_router_group[0]).at[:, EXP_ROW0:EXP_ROW0 + N_EXPERTS].set(w_router_expert[0])
    wrt_hi = wrt.astype(BF16)
    wrt = jnp.concatenate([wrt_hi, (wrt - wrt_hi.astype(F32)).astype(BF16)], axis=1)
    brt = jnp.zeros((1, ROUTER_W), F32)
    brt = brt.at[0, :N_GROUPS].set(b_router_group[0]).at[0, EXP_ROW0:EXP_ROW0 + N_EXPERTS].set(b_router_expert[0])
    h, hn, ids8, ew8, cnt = _merge(gla_f, gla_b, ret_f, ret_b, p_lat, x2, lat(g1), lat(sh2),
                                   norm_ffn[0][None, None, :] * (1.0 + lat(sc2)),
                                   jnp.tile(gla_norm[0], HEADS)[None, :],
                                   w_branch_gla[0], w_branch_ret[0], w_out[0], wrt, brt, seq, tm=MERGE_TM)

    block_rank, expert_seq, src0, stok, meta, pos, nblk = _dispatch_indices(
        ids8[:TOP_K], ids8[TOP_K:2 * TOP_K], cnt[:, 0], t)
    y_pad = _experts(block_rank, expert_seq, src0, stok, meta, hn,
                     w_expert_gate[0], w_expert_up[0], w_expert_down[0], nblk)
    out = _combine(pos, y_pad, h, ew8, lat(g2), norm_final[None, :], seq)
    return out.reshape(batch, seq, d)
```

```python
import functools

import numpy as np
import jax
import jax.numpy as jnp
from jax import lax
from jax.experimental import pallas as pl
from jax.experimental.pallas import tpu as pltpu

F32 = jnp.float32
BF16 = jnp.bfloat16

D_MODEL = 1024
GRID_W = 64
HEADS = 4
DK = 128
DV = 256
GLA_RANK = 16
GLA_TAU = 16.0
GLA_CHUNK = 64
GLA_SUB = 16
RET_CHUNK = 128
ROPE_BASE = 10000.0
N_GROUPS = 4
EXPERTS_PER_GROUP = 8
N_EXPERTS = N_GROUPS * EXPERTS_PER_GROUP
TOP_K = 2
EXPERT_FF = 256
MOE_BLOCK = 128
NORM_EPS = 1e-6

COL_GQ, COL_GK, COL_GV, COL_GG = 0, 512, 1024, 2048
COL_RQ, COL_RK, COL_RV, COL_RG = 3072, 3584, 4096, 5120
COL_MG, COL_MR = 6144, 7168
PROJ_W = 8192
LANES = 128
FEAT_TILES = D_MODEL // LANES
ROUTER_W = 128
EXP_ROW0 = 8

VMEM_LIMIT = 56 * 1024 * 1024

PROJ_TM = 1024
PROJ_TN = 2048
PROJ_NORM_SPLIT = 4
SCAN_TB = 4096
GLA_PHASE_LAG = 2
RET_PHASE_LAG = 2
GLA_EXACT_TB = 256
GLA_SAFE_RANGE = 60.0
MERGE_TM = 512
MERGE_SPLIT = 2
COMBINE_TM = 256


def _cparams(sem):
    return pltpu.CompilerParams(dimension_semantics=sem, vmem_limit_bytes=VMEM_LIMIT)


def _sigmoid(x):
    return 1.0 / (1.0 + jnp.exp(-x))


def _silu(x):
    return x * _sigmoid(x)


def _dot(a, b):
    return jnp.dot(a, b, preferred_element_type=F32)


def _dot_nt(a, b):
    return lax.dot_general(a, b, (((1,), (1,)), ((), ())), preferred_element_type=F32)


def _ada_body(c_ref, w_ref, b_ref, o_ref):
    s = _silu(c_ref[...])
    o_ref[...] = _dot(s.astype(BF16), w_ref[...].astype(BF16)) + b_ref[...]


def _ada(c8, w, b):
    n = w.shape[1]
    tn = 1536
    return pl.pallas_call(
        _ada_body,
        grid=(n // tn,),
        in_specs=[pl.BlockSpec((8, D_MODEL), lambda j: (0, 0)),
                  pl.BlockSpec((D_MODEL, tn), lambda j: (0, j)),
                  pl.BlockSpec((1, tn), lambda j: (0, j))],
        out_specs=pl.BlockSpec((8, tn), lambda j: (0, j)),
        out_shape=jax.ShapeDtypeStruct((8, n), F32),
        compiler_params=_cparams(("arbitrary",)),
        name="ada",
    )(c8, w, b)


LR_COL0 = COL_GG + HEADS * DV
WPREP_ROWS = 1024


def _wprep_body(w_ref, wlr_ref, main_ref, lr_ref):
    main_ref[...] = w_ref[...].astype(BF16)
    pad = jnp.zeros((LANES - 2 * GLA_RANK, D_MODEL), F32)
    lr_ref[...] = jnp.concatenate([wlr_ref[...], pad], axis=0).astype(BF16)


def _wprep(wt):
    steps_before = LR_COL0 // WPREP_ROWS

    def src_row(i):
        row = jnp.where(i < steps_before, i * WPREP_ROWS, i * WPREP_ROWS + 2 * GLA_RANK)
        return (pl.multiple_of(row, 2 * GLA_RANK), 0)

    return pl.pallas_call(
        _wprep_body,
        grid=(PROJ_W // WPREP_ROWS,),
        in_specs=[pl.BlockSpec((pl.Element(WPREP_ROWS), pl.Element(D_MODEL)), src_row),
                  pl.BlockSpec((pl.Element(2 * GLA_RANK), pl.Element(D_MODEL)), lambda i: (LR_COL0, 0))],
        out_specs=[pl.BlockSpec((WPREP_ROWS, D_MODEL), lambda i: (i, 0)),
                   pl.BlockSpec((LANES, D_MODEL), lambda i: (0, 0))],
        out_shape=[jax.ShapeDtypeStruct((PROJ_W, D_MODEL), BF16), jax.ShapeDtypeStruct((LANES, D_MODEL), BF16)],
        compiler_params=_cparams(("arbitrary",)),
        name="wprep",
    )(wt, wt)


ROPE_TILE = COL_RQ // PROJ_TN
ROPE_COL0 = COL_RQ % PROJ_TN
assert ROPE_TILE != 0 and ROPE_COL0 + 2 * HEADS * DK <= PROJ_TN, "rotated columns: one tile, not the first"


def _rope(x, cos, sin):
    lane = lax.broadcasted_iota(jnp.int32, x.shape, 1)
    partner = jnp.where((lane % 64) < 32, pltpu.roll(x, DK - 32, 1), pltpu.roll(x, 32, 1))
    return x * cos + partner * sin


def _proj_body(x_ref, sh_ref, sc_ref, g_ref, w_ref, wlr_ref, cos_ref, sin_ref, o_ref, lr_ref, h_ref):
    j = pl.program_id(1)

    @pl.when(j == 0)
    def _():
        n = x_ref.shape[0] // PROJ_NORM_SPLIT
        for grp in range(PROJ_NORM_SPLIT):
            rows = pl.ds(grp * n, n)
            x = x_ref[rows, :]
            ms = jnp.mean(x * x, axis=-1, keepdims=True)
            y = x * lax.rsqrt(ms + NORM_EPS) * g_ref[...]
            hb = (y * (1.0 + sc_ref[0]) + sh_ref[0]).astype(BF16)
            h_ref[rows, :] = hb
            lr_ref[rows, :] = _dot_nt(hb, wlr_ref[...])
            o_ref[rows, :] = _dot_nt(hb, w_ref[...]).astype(BF16)

    @pl.when(jnp.logical_and(j != 0, j != ROPE_TILE))
    def _():
        o_ref[...] = _dot_nt(h_ref[...], w_ref[...]).astype(BF16)

    @pl.when(j == ROPE_TILE)
    def _():
        h = h_ref[...]
        cos, sin = cos_ref[...], sin_ref[...]
        for pair in range(HEADS):
            lo = ROPE_COL0 + 2 * DK * pair
            acc = _dot_nt(h, w_ref[lo:lo + 2 * DK, :])
            if pair >= HEADS // 2:
                acc = acc * (DK ** -0.5)
            for half in range(2):
                blk = acc[:, DK * half:DK * (half + 1)]
                o_ref[:, lo + DK * half:lo + DK * (half + 1)] = _rope(blk, cos, sin).astype(BF16)
        rope_end = ROPE_COL0 + 2 * HEADS * DK
        for lo, hi in ((0, ROPE_COL0), (rope_end, PROJ_TN)):
            if lo < hi:
                o_ref[:, lo:hi] = _dot_nt(h, w_ref[lo:hi, :]).astype(BF16)


def _proj(x2, sh, sc, g, w_main, w_lr, cos, sin, tm, rows_per_mod):
    t = x2.shape[0]
    tn = PROJ_TN
    tiles_per_mod = rows_per_mod // tm
    mod_map = lambda i, j: (i // tiles_per_mod, 0, 0)
    pos_tiles = cos.shape[0] // tm
    pos_map = lambda i, j: (i % pos_tiles, 0)
    return pl.pallas_call(
        _proj_body,
        grid=(t // tm, PROJ_W // tn),
        in_specs=[pl.BlockSpec((tm, D_MODEL), lambda i, j: (i, 0)),
                  pl.BlockSpec((1, 1, D_MODEL), mod_map),
                  pl.BlockSpec((1, 1, D_MODEL), mod_map),
                  pl.BlockSpec((1, D_MODEL), lambda i, j: (0, 0)),
                  pl.BlockSpec((tn, D_MODEL), lambda i, j: (j, 0)),
                  pl.BlockSpec((LANES, D_MODEL), lambda i, j: (0, 0)),
                  pl.BlockSpec((tm, DK), pos_map), pl.BlockSpec((tm, DK), pos_map)],
        out_specs=[pl.BlockSpec((tm, tn), lambda i, j: (i, j)),
                   pl.BlockSpec((tm, LANES), lambda i, j: (i, 0))],
        out_shape=[jax.ShapeDtypeStruct((t, PROJ_W), BF16),
                   jax.ShapeDtypeStruct((t, LANES), F32)],
        scratch_shapes=[pltpu.VMEM((tm, D_MODEL), BF16)],
        compiler_params=_cparams(("parallel", "arbitrary")),
        name="proj",
    )(x2, sh, sc, g, w_main, w_lr, cos, sin)


def _gla_consts():
    c, s = GLA_CHUNK, GLA_SUB
    msk = {}
    for rev in (False, True):
        cols = []
        for i in range(c // s):
            keys = np.arange(c)[_key_rows(i, rev)][None, :]
            t = np.arange(c)[:, None]
            visible = (keys > t) if rev else (keys <= t)
            cols.append(np.where(t // s == i, visible, False))
        msk[rev] = np.concatenate(cols, axis=1).astype(np.float32)
    return msk


def _key_rows(i, rev):
    return slice(GLA_SUB * i, GLA_CHUNK) if rev else slice(0, GLA_SUB * (i + 1))


def _log_gate(lr, w, b):
    z = _dot(lr.astype(BF16), w.astype(BF16)) + b
    return -(jnp.maximum(-z, 0.0) + jnp.log(1.0 + jnp.exp(-jnp.abs(z)))) * (1.0 / GLA_TAU)


def _subchunk_scan(la, rev):
    c = la.shape[0]
    pos = lax.broadcasted_iota(jnp.int32, la.shape, 0) % GLA_SUB
    w = la
    step = 1
    while step < GLA_SUB:
        if rev:
            w = w + jnp.where(pos < GLA_SUB - step, pltpu.roll(w, c - step, 0), 0.0)
        else:
            w = w + jnp.where(pos >= step, pltpu.roll(w, step, 0), 0.0)
        step *= 2
    return w


def _gla_intra_exact(q, kf, v, b, rev, scratch):
    q_s, b_s, o_s = scratch
    c = GLA_CHUNK
    q_s[...] = q.astype(F32) * (DK ** -0.5)
    b_s[...] = b
    vf = v.astype(F32)
    row = lax.broadcasted_iota(jnp.int32, (c, 1), 0)

    def body(t, carry):
        qt = q_s[pl.ds(t, 1), :]
        bt = b_s[pl.ds(t, 1), :]
        visible = (row > t) if rev else (row <= t)
        decay = jnp.exp(jnp.where(visible, bt - b, -jnp.inf))
        col = jnp.sum(kf * decay * qt, axis=1, keepdims=True)
        o_s[pl.ds(t, 1), :] = jnp.sum(col * vf, axis=0, keepdims=True)
        return carry

    lax.fori_loop(0, c, body, 0)
    return o_s[...]


def _gla_chunk(q, k, v, la, state_ref, mask, rev, exact_scratch=None):
    c, s = GLA_CHUNK, GLA_SUB
    nsub = c // s
    w = _subchunk_scan(la, rev)
    tot = [w[s * i:s * i + 1] if rev else w[s * (i + 1) - 1:s * (i + 1)] for i in range(nsub)]
    anchors = [None] * nsub
    acc = jnp.zeros_like(tot[0])
    worst = jnp.zeros_like(tot[0])
    for i in (reversed(range(nsub)) if rev else range(nsub)):
        anchors[i] = acc
        acc = acc + tot[i]
        worst = jnp.maximum(worst, -tot[i])
    b_end = acc
    ref = jnp.concatenate([jnp.broadcast_to(a, (s, DK)) for a in anchors], axis=0)
    b = w + ref

    qt = q.astype(F32) * (DK ** -0.5) * jnp.exp(w)
    kf = k.astype(F32)
    if exact_scratch is None:
        kstack = jnp.concatenate([kf[_key_rows(i, rev)] * jnp.exp(anchors[i] - b[_key_rows(i, rev)])
                                  for i in range(nsub)], axis=0).astype(BF16)
        raw = _dot_nt(qt.astype(BF16), kstack)
    kst = kf * jnp.exp(b_end - b)
    xt = jnp.concatenate([kst, jnp.broadcast_to(jnp.exp(b_end), (c, DK))], axis=0).T
    update = _dot(xt[:, :c].astype(BF16), v)
    state = state_ref[...]
    inter = _dot((qt * jnp.exp(ref)).astype(BF16), state.astype(BF16))
    state_ref[...] = xt[:, c:c + 1] * state + update
    yield
    if exact_scratch is None:
        vstack = jnp.concatenate([v[_key_rows(i, rev)] for i in range(nsub)], axis=0)
        intra = _dot(jnp.where(mask != 0.0, raw, 0.0).astype(BF16), vstack)
    else:
        intra = _gla_intra_exact(q, kf, v, b, rev, exact_scratch)
    return inter + intra, worst


def _round_robin(gens):
    results = [None] * len(gens)
    live = list(range(len(gens)))
    while live:
        for g in list(live):
            try:
                next(gens[g])
            except StopIteration as done:
                results[g] = done.value
                live.remove(g)
    return results


def _run_phased(chunks, emit, lag):
    def finish(tag, gen):
        try:
            next(gen)
        except StopIteration as done:
            emit(tag, done.value)
            return
        raise AssertionError("chunk generator has more than two phases")

    pending = []
    for tag, gen in chunks:
        next(gen)
        pending.append((tag, gen))
        if len(pending) > lag:
            finish(*pending.pop(0))
    for item in pending:
        finish(*item)


def _gla_body(qf_ref, kf_ref, vf_ref, lrf_ref, qb_ref, kb_ref, vb_ref, lrb_ref,
              wf_ref, bf_ref, wb_ref, bb_ref, mskf_ref, mskb_ref,
              s0f_ref, s0b_ref, of_ref, ob_ref, sf_ref, sb_ref, rng_ref, *exact_scratch, nchunks):
    @pl.when(pl.program_id(2) == 0)
    def _():
        sf_ref[...] = s0f_ref[...]
        sb_ref[...] = s0b_ref[...]

    c = GLA_CHUNK
    scratch = exact_scratch or None
    la_f = _log_gate(lrf_ref[...], wf_ref[...], bf_ref[...])
    la_b = _log_gate(lrb_ref[...], wb_ref[...], bb_ref[...])
    mskf, mskb = mskf_ref[...], mskb_ref[...]
    st_f, st_b = sf_ref.at[0, 0], sb_ref.at[0, 0]
    worst = [jnp.zeros((1, DK), F32)]

    def chunks():
        for n in range(nchunks):
            rf = slice(c * n, c * (n + 1))
            yield (of_ref, rf), _gla_chunk(qf_ref[rf, :], kf_ref[rf, :], vf_ref[rf, :], la_f[rf, :], st_f, mskf,
                                           False, scratch)
            m = nchunks - 1 - n
            rb = slice(c * m, c * (m + 1))
            yield (ob_ref, rb), _gla_chunk(qb_ref[rb, :], kb_ref[rb, :], vb_ref[rb, :], la_b[rb, :], st_b, mskb,
                                           True, scratch)

    def emit(tag, result):
        o_ref, rows = tag
        o_ref[rows, :] = result[0].astype(BF16)
        worst[0] = jnp.maximum(worst[0], result[1])

    _run_phased(chunks(), emit, GLA_PHASE_LAG)
    rng_ref[0, 0, 0] = jnp.broadcast_to(worst[0], (8, DK))


def _gla(p, lr, wlr_f, blr_f, wlr_b, blr_b, s0f, s0b, batch, seq, tb, exact=False):
    nb = seq // tb
    msk = _gla_consts()
    mskf, mskb = jnp.asarray(msk[False]), jnp.asarray(msk[True])

    def fwd(col0, width):
        return lambda b, h, i: (b * nb + i, col0 // width + h)

    def bwd(col0, width):
        return lambda b, h, i: (b * nb + nb - 1 - i, col0 // width + h)

    const2 = lambda b, h, i: (0, 0)
    headcol = lambda b, h, i: (0, h)
    st_map = lambda b, h, i: (b, h, 0, 0)
    st_spec = pl.BlockSpec((1, 1, DK, DV), st_map)
    st_shape = jax.ShapeDtypeStruct((batch, HEADS, DK, DV), F32)
    o_shape = jax.ShapeDtypeStruct((batch * seq, HEADS * DV), BF16)
    return pl.pallas_call(
        functools.partial(_gla_body, nchunks=tb // GLA_CHUNK),
        grid=(batch, HEADS, nb),
        in_specs=[pl.BlockSpec((tb, DK), fwd(COL_GQ, DK)),
                  pl.BlockSpec((tb, DK), fwd(COL_GK, DK)),
                  pl.BlockSpec((tb, DV), fwd(COL_GV, DV)),
                  pl.BlockSpec((tb, LANES), lambda b, h, i: (b * nb + i, 0)),
                  pl.BlockSpec((tb, DK), bwd(COL_GQ, DK)),
                  pl.BlockSpec((tb, DK), bwd(COL_GK, DK)),
                  pl.BlockSpec((tb, DV), bwd(COL_GV, DV)),
                  pl.BlockSpec((tb, LANES), lambda b, h, i: (b * nb + nb - 1 - i, 0)),
                  pl.BlockSpec((LANES, DK), headcol),
                  pl.BlockSpec((1, DK), headcol),
                  pl.BlockSpec((LANES, DK), headcol),
                  pl.BlockSpec((1, DK), headcol),
                  pl.BlockSpec(mskf.shape, const2),
                  pl.BlockSpec(mskb.shape, const2),
                  st_spec, st_spec],
        out_specs=[pl.BlockSpec((tb, DV), lambda b, h, i: (b * nb + i, h)),
                   pl.BlockSpec((tb, DV), lambda b, h, i: (b * nb + nb - 1 - i, h)),
                   st_spec, st_spec,
                   pl.BlockSpec((1, 1, 1, 8, DK), lambda b, h, i: (b, h, i, 0, 0))],
        out_shape=[o_shape, o_shape, st_shape, st_shape,
                   jax.ShapeDtypeStruct((batch, HEADS, nb, 8, DK), F32)],
        scratch_shapes=([pltpu.VMEM((GLA_CHUNK, DK), F32), pltpu.VMEM((GLA_CHUNK, DK), F32),
                         pltpu.VMEM((GLA_CHUNK, DV), F32)] if exact else []),
        compiler_params=_cparams(("parallel", "parallel", "arbitrary")),
        name="gla_exact" if exact else "gla",
    )(p, p, p, lr, p, p, p, lr, wlr_f, blr_f, wlr_b, blr_b, mskf, mskb, s0f, s0b)


def _ret_consts():
    c = RET_CHUNK
    hh = np.arange(HEADS, dtype=np.float64)
    lg = {False: np.log1p(-np.exp2(-5.0 - hh)), True: np.log1p(-np.exp2(-5.5 - hh))}
    t = np.arange(c, dtype=np.float64)[:, None]
    u = np.arange(c, dtype=np.float64)[None, :]
    pos = np.arange(c, dtype=np.float64)
    out = {}
    for rev in (False, True):
        g = lg[rev][:, None, None]
        if not rev:
            dmat = np.where(u <= t, np.exp((t - u) * g), 0.0)
            qd = np.exp((pos + 1.0)[None, :] * lg[rev][:, None])
            kd = np.exp((c - 1.0 - pos)[None, :] * lg[rev][:, None])
        else:
            dmat = np.where(u > t, np.exp((u - t) * g), 0.0)
            qd = np.exp((c - pos)[None, :] * lg[rev][:, None])
            kd = np.exp(pos[None, :] * lg[rev][:, None])
        cd = np.exp(c * lg[rev])
        out[rev] = (dmat.astype(np.float32),
                    np.repeat(qd[:, :, None], DV, axis=2).astype(np.float32),
                    np.repeat(kd[:, :, None], DK, axis=2).astype(np.float32),
                    np.repeat(cd[:, None, None], DV, axis=2).astype(np.float32))
    return out


def _ret_chunk(q, k, v, state_ref, dmat, qd, kd, cd):
    raw = _dot_nt(q, k)
    update = _dot((k.astype(F32) * kd).T.astype(BF16), v)
    state = state_ref[...]
    inter = _dot(q, state.astype(BF16)) * qd
    state_ref[...] = cd * state + update
    yield
    return inter + _dot((raw * dmat).astype(BF16), v)


def _ret_body(qf_ref, kf_ref, vf_ref, qb_ref, kb_ref, vb_ref,
              dmf_ref, qdf_ref, kdf_ref, cdf_ref, dmb_ref, qdb_ref, kdb_ref, cdb_ref,
              s0f_ref, s0b_ref, of_ref, ob_ref, sf_ref, sb_ref, *, nchunks):
    @pl.when(pl.program_id(2) == 0)
    def _():
        sf_ref[...] = s0f_ref[...]
        sb_ref[...] = s0b_ref[...]

    c = RET_CHUNK
    cf = (dmf_ref[0], qdf_ref[0], kdf_ref[0], cdf_ref[0])
    cb = (dmb_ref[0], qdb_ref[0], kdb_ref[0], cdb_ref[0])
    st_f, st_b = sf_ref.at[0, 0], sb_ref.at[0, 0]
    def chunks():
        for n in range(nchunks):
            rf = slice(c * n, c * (n + 1))
            yield (of_ref, rf), _ret_chunk(qf_ref[rf, :], kf_ref[rf, :], vf_ref[rf, :], st_f, *cf)
            m = nchunks - 1 - n
            rb = slice(c * m, c * (m + 1))
            yield (ob_ref, rb), _ret_chunk(qb_ref[rb, :], kb_ref[rb, :], vb_ref[rb, :], st_b, *cb)

    def emit(tag, out):
        o_ref, rows = tag
        o_ref[rows, :] = out.astype(BF16)

    _run_phased(chunks(), emit, RET_PHASE_LAG)


def _ret(p, s0f, s0b, batch, seq, tb):
    nb = seq // tb
    consts = _ret_consts()

    def fwd(col0, width):
        return lambda b, h, i: (b * nb + i, col0 // width + h)

    def bwd(col0, width):
        return lambda b, h, i: (b * nb + nb - 1 - i, col0 // width + h)

    head3 = lambda b, h, i: (h, 0, 0)
    st_spec = pl.BlockSpec((1, 1, DK, DV), lambda b, h, i: (b, h, 0, 0))
    st_shape = jax.ShapeDtypeStruct((batch, HEADS, DK, DV), F32)
    o_shape = jax.ShapeDtypeStruct((batch * seq, HEADS * DV), BF16)
    const_specs = []
    const_args = []
    for rev in (False, True):
        for a in consts[rev]:
            const_specs.append(pl.BlockSpec((1,) + a.shape[1:], head3))
            const_args.append(jnp.asarray(a))
    return pl.pallas_call(
        functools.partial(_ret_body, nchunks=tb // RET_CHUNK),
        grid=(batch, HEADS, nb),
        in_specs=[pl.BlockSpec((tb, DK), fwd(COL_RQ, DK)),
                  pl.BlockSpec((tb, DK), fwd(COL_RK, DK)),
                  pl.BlockSpec((tb, DV), fwd(COL_RV, DV)),
                  pl.BlockSpec((tb, DK), bwd(COL_RQ, DK)),
                  pl.BlockSpec((tb, DK), bwd(COL_RK, DK)),
                  pl.BlockSpec((tb, DV), bwd(COL_RV, DV))]
                 + const_specs + [st_spec, st_spec],
        out_specs=[pl.BlockSpec((tb, DV), lambda b, h, i: (b * nb + i, h)),
                   pl.BlockSpec((tb, DV), lambda b, h, i: (b * nb + nb - 1 - i, h)),
                   st_spec, st_spec],
        out_shape=[o_shape, o_shape, st_shape, st_shape],
        compiler_params=_cparams(("parallel", "parallel", "arbitrary")),
        name="ret",
    )(p, p, p, p, p, p, *const_args, s0f, s0b)


def _rope_tables(seq):
    n = DK // 4
    inv = np.float32(ROPE_BASE) ** (-np.arange(n, dtype=np.float32) / np.float32(n))
    pos = np.arange(seq)
    ar = (pos // GRID_W).astype(np.float32)[:, None] * inv[None, :]
    ac = (pos % GRID_W).astype(np.float32)[:, None] * inv[None, :]
    cos = np.concatenate([np.cos(ar), np.cos(ar), np.cos(ac), np.cos(ac)], axis=1)
    sin = np.concatenate([-np.sin(ar), np.sin(ar), -np.sin(ac), np.sin(ac)], axis=1)
    return jnp.asarray(cos, F32), jnp.asarray(sin, F32)


def _route(logits_t):
    g = [logits_t[i:i + 1] for i in range(N_GROUPS)]
    gmax = jnp.maximum(jnp.maximum(g[0], g[1]), jnp.maximum(g[2], g[3]))
    gsel = jnp.where(g[0] == gmax, 0, jnp.where(g[1] == gmax, 1, jnp.where(g[2] == gmax, 2, 3)))
    gsum = (jnp.exp(g[0] - gmax) + jnp.exp(g[1] - gmax)) + (jnp.exp(g[2] - gmax) + jnp.exp(g[3] - gmax))
    gw = 1.0 / gsum
    e = [logits_t[EXP_ROW0 + EXPERTS_PER_GROUP * i:EXP_ROW0 + EXPERTS_PER_GROUP * (i + 1)]
         for i in range(N_GROUPS)]
    el = jnp.where(gsel == 0, e[0], jnp.where(gsel == 1, e[1], jnp.where(gsel == 2, e[2], e[3])))
    row = lax.broadcasted_iota(jnp.int32, el.shape, 0).astype(F32)
    none = float(EXPERTS_PER_GROUP)
    m1 = jnp.max(el, axis=0, keepdims=True)
    i1 = jnp.min(jnp.where(el == m1, row, none), axis=0, keepdims=True)
    el2 = jnp.where(row == i1, -jnp.inf, el)
    m2 = jnp.max(el2, axis=0, keepdims=True)
    i2 = jnp.min(jnp.where(el2 == m2, row, none), axis=0, keepdims=True)
    r = jnp.exp(m2 - m1)
    w1 = gw / (1.0 + r)
    w2 = gw * r / (1.0 + r)
    base = gsel * EXPERTS_PER_GROUP
    ids = jnp.concatenate([base + i1.astype(jnp.int32), base + i2.astype(jnp.int32)], axis=0)
    return ids, jnp.concatenate([w1, w2], axis=0)


def _store_token_tiled(ref, val):
    m = val.shape[0]
    for j in range(FEAT_TILES):
        ref[pl.ds(j, m, stride=FEAT_TILES), :] = val[:, LANES * j:LANES * (j + 1)]


def _load_token_tiled(ref, m):
    return jnp.concatenate([ref[pl.ds(j, m, stride=FEAT_TILES), :] for j in range(FEAT_TILES)], axis=1)


def _merge_body(gf_ref, gb_ref, rf_ref, rb_ref, gg_ref, rg_ref, mg_ref, mr_ref, x_ref,
                g1_ref, sh2_ref, nfs_ref, gn_ref, wg_ref, wr_ref, wo_ref, wrt_ref, brt_ref, upper_ref,
                h_ref, hn_ref, ids_ref, ew_ref, cnt_ref, wg_s, wr_s, wo_s, stage, sem):
    @pl.when(pl.program_id(0) == 0)
    def _():
        cnt_ref[...] = jnp.zeros_like(cnt_ref)
        for w_hbm, w_s in ((wg_ref, wg_s), (wr_ref, wr_s), (wo_ref, wo_s)):
            copy = pltpu.make_async_copy(w_hbm, stage, sem.at[0])
            copy.start()
            copy.wait()
            w_s[...] = stage[...].astype(BF16)

    def rows_to_logits(r0, n):
        rows = pl.ds(r0, n)
        og = (gf_ref[rows, :] + gb_ref[rows, :]).astype(F32)
        orr = (rf_ref[rows, :] + rb_ref[rows, :]).astype(F32)
        gparts, rparts = [], []
        for hh in range(HEADS):
            seg = og[:, DV * hh:DV * (hh + 1)]
            ms = jnp.mean(seg * seg, axis=-1, keepdims=True)
            gparts.append(seg * lax.rsqrt(ms + NORM_EPS))
            seg = orr[:, DV * hh:DV * (hh + 1)]
            mu = jnp.mean(seg, axis=-1, keepdims=True)
            cen = seg - mu
            var = jnp.mean(cen * cen, axis=-1, keepdims=True)
            rparts.append(cen * lax.rsqrt(var + NORM_EPS))
        o_gla = (jnp.concatenate(gparts, axis=1) * gn_ref[...]).astype(BF16) * _silu(gg_ref[rows, :])
        o_ret = jnp.concatenate(rparts, axis=1).astype(BF16) * _silu(rg_ref[rows, :])
        a_gla = _dot(o_gla, wg_s[...])
        a_ret = _dot(o_ret, wr_s[...])
        yield
        y = _sigmoid(mg_ref[rows, :]) * a_gla.astype(BF16) + _sigmoid(mr_ref[rows, :]) * a_ret.astype(BF16)
        out = _dot(y, wo_s[...])
        yield
        h = x_ref[rows, :] + g1_ref[0] * out
        h_ref[rows, :] = h
        ms = jnp.mean(h * h, axis=-1, keepdims=True)
        hn = h * lax.rsqrt(ms + NORM_EPS) * nfs_ref[0] + sh2_ref[0]
        _store_token_tiled(hn_ref.at[pl.ds(r0 * FEAT_TILES, n * FEAT_TILES)], hn)
        hn_hi = hn.astype(BF16)
        hn_lo = (hn - hn_hi.astype(F32)).astype(BF16)
        both_w = _dot(hn_hi, wrt_ref[...])
        lo_w = _dot(hn_lo, wrt_ref[:, :ROUTER_W])
        yield
        return (both_w[:, :ROUTER_W] + both_w[:, ROUTER_W:] + lo_w) + brt_ref[...]

    tm = x_ref.shape[0]
    n = tm // MERGE_SPLIT
    logits = jnp.concatenate(_round_robin([rows_to_logits(g * n, n) for g in range(MERGE_SPLIT)]), axis=0)
    ids, ew = _route(logits.T)

    erow = lax.broadcasted_iota(jnp.int32, (N_EXPERTS, tm), 0)
    oh0 = jnp.where(erow == ids[0:1], 1.0, 0.0)
    oh1 = jnp.where(erow == ids[1:2], 1.0, 0.0)
    both = oh0 + oh1
    before = _dot(both.astype(BF16), upper_ref[...]) + cnt_ref[:, 0:1].astype(F32)
    rank0 = jnp.sum(oh0 * before, axis=0, keepdims=True)
    rank1 = jnp.sum(oh1 * before, axis=0, keepdims=True)
    total = cnt_ref[:, 0:1] + jnp.sum(both, axis=1, keepdims=True).astype(jnp.int32)
    cnt_ref[...] = jnp.broadcast_to(total, cnt_ref.shape)
    ids_ref[...] = jnp.concatenate([ids, rank0.astype(jnp.int32), rank1.astype(jnp.int32),
                                    jnp.zeros((4, tm), jnp.int32)], axis=0)
    ew_ref[...] = jnp.concatenate([ew, jnp.zeros((6, tm), F32)], axis=0)


def _merge(gf, gb, rf, rb, p, x2, g1, sh2, nfs, gn, wg, wr, wo, wrt, brt, seq, tm):
    t = x2.shape[0]
    tiles_per_batch = seq // tm
    row = lambda i: (i, 0)
    mod = lambda i: (i // tiles_per_batch, 0, 0)
    const = lambda i: (0, 0)
    tok = pl.BlockSpec((tm, D_MODEL), row)
    vec = pl.BlockSpec((1, D_MODEL), const)
    modspec = pl.BlockSpec((1, 1, D_MODEL), mod)
    wspec = pl.BlockSpec(memory_space=pl.ANY)
    upper = jnp.asarray(np.triu(np.ones((tm, tm), np.float32), 1), BF16)

    def pcol(col0):
        return pl.BlockSpec((tm, D_MODEL), lambda i: (i, col0 // D_MODEL))

    return pl.pallas_call(
        _merge_body,
        grid=(t // tm,),
        in_specs=[tok, tok, tok, tok, pcol(COL_GG), pcol(COL_RG), pcol(COL_MG), pcol(COL_MR), tok,
                  modspec, modspec, modspec, vec, wspec, wspec, wspec,
                  pl.BlockSpec((D_MODEL, 2 * ROUTER_W), const), pl.BlockSpec((1, ROUTER_W), const),
                  pl.BlockSpec((tm, tm), const)],
        out_specs=[tok, pl.BlockSpec((tm * FEAT_TILES, LANES), row),
                   pl.BlockSpec((8, tm), lambda i: (0, i)), pl.BlockSpec((8, tm), lambda i: (0, i)),
                   pl.BlockSpec((N_EXPERTS, LANES), const)],
        out_shape=[jax.ShapeDtypeStruct((t, D_MODEL), F32), jax.ShapeDtypeStruct((t * FEAT_TILES, LANES), F32),
                   jax.ShapeDtypeStruct((8, t), jnp.int32), jax.ShapeDtypeStruct((8, t), F32),
                   jax.ShapeDtypeStruct((N_EXPERTS, LANES), jnp.int32)],
        scratch_shapes=[pltpu.VMEM((D_MODEL, D_MODEL), BF16)] * 3
                       + [pltpu.VMEM((D_MODEL, D_MODEL), F32), pltpu.SemaphoreType.DMA((1,))],
        compiler_params=_cparams(("arbitrary",)),
        name="merge",
    )(gf, gb, rf, rb, p, p, p, p, x2, g1, sh2, nfs, gn, wg, wr, wo, wrt, brt, upper)


GATHER_UNROLL = 8


def _aligned(tok):
    off = tok * FEAT_TILES
    return off if isinstance(off, int) else pl.multiple_of(off, FEAT_TILES)


def _token_copy(src_hbm, dst, sem, src_tok, dst_tok):
    return pltpu.make_async_copy(src_hbm.at[pl.ds(_aligned(src_tok), FEAT_TILES), :],
                                 dst.at[pl.ds(_aligned(dst_tok), FEAT_TILES), :], sem)


def _wait_tokens(src_hbm, dst, sem):
    pltpu.make_async_copy(src_hbm.at[pl.ds(0, dst.shape[0]), :], dst, sem).wait()


GATHER_BUFS = 3
GATHER_AHEAD = GATHER_BUFS - 1


def _issue_rows(n, start_row, unrolled):
    if unrolled:
        for r in range(n):
            start_row(r, r % 2)
    else:
        def body(g, carry):
            for u in range(GATHER_UNROLL):
                start_row(g * GATHER_UNROLL + u, u % 2)
            return carry
        lax.fori_loop(0, n // GATHER_UNROLL, body, 0)


EXPERT_GROUP = 2


EXPERT_W_SLOTS = 2
assert EXPERT_GROUP <= EXPERT_W_SLOTS, "the blocks of one step may belong to EXPERT_GROUP consecutive runs"


def _expert_body(rank_ref, eseq_ref, src0_ref, stok_ref, meta_ref, hn_hbm, wg_hbm, wu_hbm, wd_hbm, y_ref, *scratch):
    bufs, sems = scratch[:GATHER_BUFS], scratch[GATHER_BUFS]
    stage, w_s, wsem = scratch[GATHER_BUFS + 1:GATHER_BUFS + 4], scratch[GATHER_BUFS + 4:GATHER_BUFS + 7], scratch[-1]
    w_hbm = (wg_hbm, wu_hbm, wd_hbm)
    i = pl.program_id(0)
    nused, nranks = meta_ref[0], meta_ref[1]
    nsteps = (nused + EXPERT_GROUP - 1) // EXPERT_GROUP
    last = stok_ref.shape[0] - 1
    rows = MOE_BLOCK * FEAT_TILES

    def issue(step, s, unrolled):
        bases = [src0_ref[step * EXPERT_GROUP + g] for g in range(EXPERT_GROUP)]

        def start(g, r, priority):
            tok = stok_ref[jnp.minimum(bases[g] + r, last)]
            _token_copy(hn_hbm, bufs[s], sems.at[s], tok, g * MOE_BLOCK + r).start(priority=priority)

        if unrolled:
            _issue_rows(EXPERT_GROUP * MOE_BLOCK, lambda j, p: start(j // MOE_BLOCK, j % MOE_BLOCK, p), True)
        else:
            for g in range(EXPERT_GROUP):
                _issue_rows(MOE_BLOCK, lambda r, p, g=g: start(g, r, p), False)

    def weight_copies(rank):
        slot = rank % EXPERT_W_SLOTS
        expert = eseq_ref[rank]
        return [pltpu.make_async_copy(w_hbm[k].at[expert], stage[k].at[slot], wsem.at[slot]) for k in range(3)]

    def activate(rank):
        slot = rank % EXPERT_W_SLOTS
        for copy in weight_copies(rank):
            copy.wait()
        for k in range(3):
            w_s[k][slot] = stage[k][slot].astype(BF16)

        @pl.when(rank + 1 < nranks)
        def _():
            for copy in weight_copies(rank + 1):
                copy.start()

    @pl.when(i == 0)
    def _():
        for b in range(GATHER_AHEAD):
            issue(b, b, False)

        @pl.when(nranks > 0)
        def _():
            for copy in weight_copies(0):
                copy.start()

    for g in range(EXPERT_GROUP):
        blk = i * EXPERT_GROUP + g
        rank = rank_ref[blk]
        first_of_run = jnp.logical_or(blk == 0, rank != rank_ref[jnp.maximum(blk - 1, 0)])

        @pl.when(jnp.logical_and(blk < nused, first_of_run))
        def _():
            activate(rank)

    def block_phases(cur, g):
        part = pl.ds(g * rows, rows)
        slot = rank_ref[i * EXPERT_GROUP + g] % EXPERT_W_SLOTS
        xb = _load_token_tiled(bufs[cur].at[part], MOE_BLOCK).astype(BF16)
        gate = _dot(xb, w_s[0][slot])
        up = _dot(xb, w_s[1][slot])
        yield
        y = _dot((_silu(gate) * up).astype(BF16), w_s[2][slot])
        yield
        _store_token_tiled(y_ref.at[part], y)

    for cur in range(GATHER_BUFS):
        mine = i % GATHER_BUFS == cur

        @pl.when(jnp.logical_and(mine, i < nsteps))
        def _():
            _wait_tokens(hn_hbm, bufs[cur], sems.at[cur])
            issue(i + GATHER_AHEAD, (cur + GATHER_AHEAD) % GATHER_BUFS, True)
            _round_robin([block_phases(cur, g) for g in range(EXPERT_GROUP)])

        @pl.when(jnp.logical_and(mine, jnp.logical_and(i >= nsteps, i < nsteps + GATHER_AHEAD)))
        def _():
            _wait_tokens(hn_hbm, bufs[cur], sems.at[cur])

    @pl.when(i >= nsteps)
    def _():
        y_ref[...] = jnp.zeros_like(y_ref)


def _experts(block_rank, expert_seq, src0, stok, meta, hn, w_gate, w_up, w_down, nblk):
    rows = EXPERT_GROUP * MOE_BLOCK * FEAT_TILES
    w_shapes = ((D_MODEL, EXPERT_FF), (D_MODEL, EXPERT_FF), (EXPERT_FF, D_MODEL))
    any_spec = pl.BlockSpec(memory_space=pl.ANY)
    return pl.pallas_call(
        _expert_body,
        grid_spec=pltpu.PrefetchScalarGridSpec(
            num_scalar_prefetch=5,
            grid=(nblk // EXPERT_GROUP,),
            in_specs=[any_spec, any_spec, any_spec, any_spec],
            out_specs=pl.BlockSpec((rows, LANES), lambda i, *prefetch: (i, 0)),
            scratch_shapes=[pltpu.VMEM((rows, LANES), F32)] * GATHER_BUFS
                           + [pltpu.SemaphoreType.DMA((GATHER_BUFS,))]
                           + [pltpu.VMEM((EXPERT_W_SLOTS,) + shape, F32) for shape in w_shapes]
                           + [pltpu.VMEM((EXPERT_W_SLOTS,) + shape, BF16) for shape in w_shapes]
                           + [pltpu.SemaphoreType.DMA((EXPERT_W_SLOTS,))]),
        out_shape=jax.ShapeDtypeStruct((nblk // EXPERT_GROUP * rows, LANES), F32),
        compiler_params=_cparams(("arbitrary",)),
        name="experts",
    )(block_rank, expert_seq, src0, stok, meta, hn, w_gate, w_up, w_down)


def _combine_body(pos_ref, y_hbm, h_ref, ew_ref, g2_ref, nf_ref, o_ref, *scratch):
    i = pl.program_id(0)
    n = pl.num_programs(0)
    tm = COMBINE_TM
    t = pos_ref.shape[0] // TOP_K
    bufs, sems = scratch[:GATHER_BUFS], scratch[GATHER_BUFS]

    def issue(tile, s, unrolled):
        def start_row(j, priority):
            k, r = j % TOP_K, j // TOP_K
            _token_copy(y_hbm, bufs[s].at[k], sems.at[s], pos_ref[k * t + tile * tm + r], r).start(priority=priority)
        _issue_rows(tm * TOP_K, start_row, unrolled)

    def wait(s):
        for k in range(TOP_K):
            _wait_tokens(y_hbm, bufs[s].at[k], sems.at[s])

    @pl.when(i == 0)
    def _():
        for b in range(GATHER_AHEAD):
            issue(b, b, False)

    for cur in range(GATHER_BUFS):
        @pl.when(i % GATHER_BUFS == cur)
        def _():
            wait(cur)
            issue(jnp.minimum(i + GATHER_AHEAD, n - 1), (cur + GATHER_AHEAD) % GATHER_BUFS, True)
            wt = jnp.concatenate([ew_ref[...]] * (LANES // 8), axis=0).T
            moe = (wt[:, 0:1] * _load_token_tiled(bufs[cur].at[0], tm)
                   + wt[:, 1:2] * _load_token_tiled(bufs[cur].at[1], tm))
            h = h_ref[...] + g2_ref[0] * moe
            ms = jnp.mean(h * h, axis=-1, keepdims=True)
            o_ref[...] = h * lax.rsqrt(ms + NORM_EPS) * nf_ref[...]

        @pl.when(jnp.logical_and(i % GATHER_BUFS == cur, i == n - 1))
        def _():
            for ahead in range(1, GATHER_BUFS):
                wait((cur + ahead) % GATHER_BUFS)


def _combine(pos, y_pad, h, ew, g2, nf, seq):
    t = h.shape[0]
    tm = COMBINE_TM
    tiles_per_batch = seq // tm
    ybuf = pltpu.VMEM((TOP_K, tm * FEAT_TILES, LANES), F32)
    return pl.pallas_call(
        _combine_body,
        grid_spec=pltpu.PrefetchScalarGridSpec(
            num_scalar_prefetch=1,
            grid=(t // tm,),
            in_specs=[pl.BlockSpec(memory_space=pl.ANY),
                      pl.BlockSpec((tm, D_MODEL), lambda i, pos: (i, 0)),
                      pl.BlockSpec((8, tm), lambda i, pos: (0, i)),
                      pl.BlockSpec((1, 1, D_MODEL), lambda i, pos: (i // tiles_per_batch, 0, 0)),
                      pl.BlockSpec((1, D_MODEL), lambda i, pos: (0, 0))],
            out_specs=pl.BlockSpec((tm, D_MODEL), lambda i, pos: (i, 0)),
            scratch_shapes=[ybuf] * GATHER_BUFS + [pltpu.SemaphoreType.DMA((GATHER_BUFS,))]),
        out_shape=jax.ShapeDtypeStruct((t, D_MODEL), F32),
        compiler_params=_cparams(("arbitrary",)),
        name="combine",
    )(pos, y_pad, h, ew, g2, nf)


def _dispatch_indices(ids, ranks, counts, t):
    a = t * TOP_K
    nblk = a // MOE_BLOCK + N_EXPERTS + GATHER_AHEAD * EXPERT_GROUP
    experts = jnp.arange(N_EXPERTS, dtype=jnp.int32)
    starts = jnp.cumsum(counts) - counts
    padded = (counts + MOE_BLOCK - 1) // MOE_BLOCK * MOE_BLOCK
    pends = jnp.cumsum(padded)
    pstarts = pends - padded
    block_start = jnp.arange(nblk, dtype=jnp.int32) * MOE_BLOCK
    block_e = jnp.minimum(jnp.sum((block_start[:, None] >= pends[None, :]).astype(jnp.int32), axis=1),
                          N_EXPERTS - 1)
    of_block = (block_e[:, None] == experts[None, :]).astype(jnp.int32)
    src0 = block_start + jnp.sum(of_block * (starts - pstarts)[None, :], axis=1)
    nused = pends[-1:] // MOE_BLOCK
    blocks = jnp.arange(nblk, dtype=jnp.int32)
    used = blocks < nused
    starts_run = jnp.logical_and(used, jnp.concatenate([jnp.ones((1,), bool), block_e[1:] != block_e[:-1]]))
    nranks = jnp.sum(starts_run.astype(jnp.int32), keepdims=True)
    run = jnp.cumsum(starts_run.astype(jnp.int32)) - 1
    block_rank = jnp.clip(run, 0, jnp.maximum(nranks - 1, 0))
    run_ids = jnp.arange(N_EXPERTS + 1, dtype=jnp.int32)
    expert_seq = jnp.sum(jnp.where(jnp.logical_and(starts_run[:, None], run[:, None] == run_ids[None, :]),
                                   block_e[:, None], 0), axis=0)
    meta = jnp.concatenate([nused, nranks])
    tok = jnp.arange(t, dtype=jnp.int32)[None, :]
    slot = jnp.arange(TOP_K, dtype=jnp.int32)[:, None]
    key = ids * a + tok * TOP_K + slot
    stok = (jnp.sort(key.reshape(a)) % a) // TOP_K
    ids_d = ids.reshape(a // LANES, LANES)
    pstart_of = jnp.zeros_like(ids_d)
    for e in range(N_EXPERTS):
        pstart_of = jnp.where(ids_d == e, pstarts[e], pstart_of)
    pos = (ranks.reshape(a // LANES, LANES) + pstart_of).reshape(a)
    i32 = lambda v: v.astype(jnp.int32)
    return i32(block_rank), i32(expert_seq), i32(src0), i32(stok), i32(meta), i32(pos), nblk


def kernel(x, c, ctx, c_ctx, w_ada, b_ada, norm_mix, norm_ffn, w_in, gla_lr_w, gla_lr_b, gla_norm,
           w_branch_gla, w_branch_ret, w_out, w_router_group, b_router_group, w_router_expert,
           b_router_expert, w_expert_gate, w_expert_up, w_expert_down, norm_final):
    batch, seq, d = x.shape
    ctx_len = ctx.shape[1]
    assert d == D_MODEL and w_ada.shape[0] == 1, "single-layer block with D_MODEL features"
    t = batch * seq

    c8 = jnp.zeros((8, d), F32).at[:batch].set(c).at[batch].set(c_ctx)
    mod = _ada(c8, w_ada[0], b_ada[0][None, :])
    sh1, sc1, g1, sh2, sc2, g2 = [mod[:, d * i:d * (i + 1)] for i in range(6)]
    lat = lambda m: m[:batch, None, :]
    cx = lambda m: m[batch:batch + 1, None, :]

    w_main, w_lr = _wprep(jnp.swapaxes(w_in[0], 0, 1))
    nm = norm_mix[0][None, :]

    ones = jnp.ones((batch * ctx_len, DK), F32)
    p_ctx, lr_ctx = _proj(ctx.reshape(batch * ctx_len, d), cx(sh1), cx(sc1), nm, w_main, w_lr,
                          ones, jnp.zeros_like(ones), tm=batch * ctx_len, rows_per_mod=batch * ctx_len)
    x2 = x.reshape(t, d)
    cos, sin = _rope_tables(seq)
    p_lat, lr_lat = _proj(x2, lat(sh1), lat(sc1), nm, w_main, w_lr, cos, sin, tm=PROJ_TM, rows_per_mod=seq)

    wlr_f = jnp.zeros((LANES, HEADS * DK), F32).at[:GLA_RANK].set(gla_lr_w[0, 0])
    wlr_b = jnp.zeros((LANES, HEADS * DK), F32).at[GLA_RANK:2 * GLA_RANK].set(gla_lr_w[0, 1])
    blr_f, blr_b = gla_lr_b[0, 0][None, :], gla_lr_b[0, 1][None, :]
    zero_state = jnp.zeros((batch, HEADS, DK, DV), F32)

    _, _, gs_f, gs_b, _ = _gla(p_ctx, lr_ctx, wlr_f, blr_f, wlr_b, blr_b, zero_state, zero_state,
                               batch, ctx_len, tb=ctx_len)
    gla_args = (p_lat, lr_lat, wlr_f, blr_f, wlr_b, blr_b, gs_f, gs_b, batch, seq)
    gla_f, gla_b, _, _, decay_range = _gla(*gla_args, tb=SCAN_TB)
    gla_f, gla_b = lax.cond(jnp.max(decay_range) > GLA_SAFE_RANGE,
                            lambda: tuple(_gla(*gla_args, tb=GLA_EXACT_TB, exact=True)[:2]),
                            lambda: (gla_f, gla_b))

    _, _, rs_f, rs_b = _ret(p_ctx, zero_state, zero_state, batch, ctx_len, tb=ctx_len)
    ret_f, ret_b, _, _ = _ret(p_lat, rs_f, rs_b, batch, seq, tb=SCAN_TB)

    wrt = jnp.zeros((d, ROUTER_W), F32)
    wrt = wrt.at[:, :N_GROUPS].set(w_router_group[0]).at[:, EXP_ROW0:EXP_ROW0 + N_EXPERTS].set(w_router_expert[0])
    wrt_hi = wrt.astype(BF16)
    wrt = jnp.concatenate([wrt_hi, (wrt - wrt_hi.astype(F32)).astype(BF16)], axis=1)
    brt = jnp.zeros((1, ROUTER_W), F32)
    brt = brt.at[0, :N_GROUPS].set(b_router_group[0]).at[0, EXP_ROW0:EXP_ROW0 + N_EXPERTS].set(b_router_expert[0])
    h, hn, ids8, ew8, cnt = _merge(gla_f, gla_b, ret_f, ret_b, p_lat, x2, lat(g1), lat(sh2),
                                   norm_ffn[0][None, None, :] * (1.0 + lat(sc2)),
                                   jnp.tile(gla_norm[0], HEADS)[None, :],
                                   w_branch_gla[0], w_branch_ret[0], w_out[0], wrt, brt, seq, tm=MERGE_TM)

    block_rank, expert_seq, src0, stok, meta, pos, nblk = _dispatch_indices(
        ids8[:TOP_K], ids8[TOP_K:2 * TOP_K], cnt[:, 0], t)
    y_pad = _experts(block_rank, expert_seq, src0, stok, meta, hn,
                     w_expert_gate[0], w_expert_up[0], w_expert_down[0], nblk)
    out = _combine(pos, y_pad, h, ew8, lat(g2), norm_final[None, :], seq)
    return out.reshape(batch, seq, d)
```
